```python
import jax
import jax.numpy as jnp
from jax import lax
import numpy as np


D_MODEL = 1024
BATCH = 8
SEQ = 2048
DEPTH = 4

HEAD_DIM = 64
FOX_HEADS = 8
SB_HEADS = 8
FOX_WIDTH = FOX_HEADS * HEAD_DIM
SB_WIDTH = SB_HEADS * HEAD_DIM
CONV_WIDTH = 512
CONV_K = 3
N_BRANCH = 3
D_FF = 2816
Q_BLOCK = 128
NORM_EPS = 1e-6
NEG_INF = -1e30

SPLIT_POINTS = [
    3 * CONV_WIDTH,
    3 * CONV_WIDTH + 3 * FOX_WIDTH,
    3 * CONV_WIDTH + 3 * FOX_WIDTH + FOX_HEADS,
    3 * CONV_WIDTH + 3 * FOX_WIDTH + FOX_HEADS + 3 * SB_WIDTH,
]
D_IN = 3 * CONV_WIDTH + 3 * FOX_WIDTH + FOX_HEADS + 3 * SB_WIDTH + N_BRANCH * D_MODEL

kernel_name = 'hybrid_gatedconv_fox_stickbreak_convglu'


def rmsnorm(x, g):
    xf = x.astype(jnp.float32)
    y = xf * lax.rsqrt(jnp.mean(xf * xf, axis=-1, keepdims=True) + NORM_EPS)
    return (y * g.astype(jnp.float32)).astype(x.dtype)


def causal_dwconv(x, w):
    k = w.shape[0]
    return lax.conv_general_dilated(
        x, w[:, None, :].astype(x.dtype), window_strides=(1,), padding=[(k - 1, 0)],
        dimension_numbers=('NWC', 'WIO', 'NWC'), feature_group_count=x.shape[-1])


def _heads(t, n):
    b, s, _ = t.shape
    return t.reshape(b, s, n, HEAD_DIM).transpose(0, 2, 1, 3)


def _to_blocks(t):
    b, h, s = t.shape[:3]
    t = t.reshape(b, h, s // Q_BLOCK, Q_BLOCK, *t.shape[3:])
    return jnp.moveaxis(t, 2, 0)


def _from_blocks(o):
    nb, b, h, q, d = o.shape
    return o.transpose(1, 0, 3, 2, 4).reshape(b, nb * q, h * d)


def short_conv_mixer(b_gate, c_gate, h, conv_w):
    return b_gate * causal_dwconv(c_gate * h, conv_w)


def forgetting_attention(q, k, v, log_f, qn_g, kn_g):
    q = rmsnorm(_heads(q, FOX_HEADS), qn_g)
    k = rmsnorm(_heads(k, FOX_HEADS), kn_g)
    v = _heads(v, FOX_HEADS)
    c = lax.cumsum(log_f, axis=1).transpose(0, 2, 1)
    pos = jnp.arange(q.shape[2])
    scale = HEAD_DIM ** -0.5

    def block(args):
        qb, cqb, qpos = args
        logits = (jnp.einsum('bhqd,bhkd->bhqk', qb, k).astype(jnp.float32) * scale
                  + cqb[..., None] - c[:, :, None, :])
        logits = jnp.where(pos[None, :] <= qpos[:, None], logits, NEG_INF)
        p = jax.nn.softmax(logits, axis=-1)
        return jnp.einsum('bhqk,bhkd->bhqd', p.astype(v.dtype), v)

    out = lax.map(block, (_to_blocks(q), _to_blocks(c), pos.reshape(-1, Q_BLOCK)))
    return _from_blocks(out)


def stick_breaking_attention(q, k, v):
    q = _heads(q, SB_HEADS)
    k = _heads(k, SB_HEADS)
    v = _heads(v, SB_HEADS)
    pos = jnp.arange(q.shape[2])
    scale = HEAD_DIM ** -0.5

    def block(args):
        qb, qpos = args
        z = jnp.einsum('bhqd,bhkd->bhqk', qb, k).astype(jnp.float32) * scale
        strict = pos[None, :] < qpos[:, None]
        log_1m = jnp.where(strict, jax.nn.log_sigmoid(-z), 0.0)
        suffix = lax.cumsum(log_1m, axis=log_1m.ndim - 1, reverse=True) - log_1m
        a = jnp.where(strict, jnp.exp(jax.nn.log_sigmoid(z) + suffix), 0.0)
        return jnp.einsum('bhqk,bhkd->bhqd', a.astype(v.dtype), v)

    out = lax.map(block, (_to_blocks(q), pos.reshape(-1, Q_BLOCK)))
    return _from_blocks(out)


def _fwd_setup_inputs(seed: int = 0) -> dict:
    key = jax.random.key(seed)
    ks = jax.random.split(key, 17)
    f32 = jnp.float32
    nrm = lambda k, shape, s: jax.random.normal(k, shape, f32) * s
    return {
        'x': nrm(ks[0], (BATCH, SEQ, D_MODEL), 1.0),
        'norm1_g': 1.0 + nrm(ks[1], (DEPTH, D_MODEL), 0.05),
        'w_in': nrm(ks[2], (DEPTH, D_MODEL, D_IN), D_MODEL ** -0.5),
        'fox_f_bias': 2.0 + nrm(ks[3], (DEPTH, FOX_HEADS), 0.5),
        'gate_bias': nrm(ks[4], (DEPTH, N_BRANCH * D_MODEL), 0.1),
        'conv_w': nrm(ks[5], (DEPTH, CONV_K, CONV_WIDTH), CONV_K ** -0.5),
        'fox_q_norm_g': 1.0 + nrm(ks[6], (DEPTH, HEAD_DIM), 0.05),
        'fox_k_norm_g': 1.0 + nrm(ks[7], (DEPTH, HEAD_DIM), 0.05),
        'w_proj_conv': nrm(ks[8], (DEPTH, CONV_WIDTH, D_MODEL), CONV_WIDTH ** -0.5),
        'w_proj_fox': nrm(ks[9], (DEPTH, FOX_WIDTH, D_MODEL), FOX_WIDTH ** -0.5),
        'w_proj_sb': nrm(ks[10], (DEPTH, SB_WIDTH, D_MODEL), SB_WIDTH ** -0.5),
        'w_out': nrm(ks[11], (DEPTH, D_MODEL, D_MODEL), D_MODEL ** -0.5),
        'norm2_g': 1.0 + nrm(ks[12], (DEPTH, D_MODEL), 0.05),
        'w_up': nrm(ks[13], (DEPTH, D_MODEL, 2 * D_FF), D_MODEL ** -0.5),
        'ffn_conv_w': nrm(ks[14], (DEPTH, CONV_K, D_FF), CONV_K ** -0.5),
        'ffn_conv_b': nrm(ks[15], (DEPTH, D_FF), 0.02),
        'w_down': nrm(ks[16], (DEPTH, D_FF, D_MODEL), D_FF ** -0.5),
    }


def _fwd_reference(x, norm1_g, w_in, fox_f_bias, gate_bias, conv_w, fox_q_norm_g, fox_k_norm_g,
              w_proj_conv, w_proj_fox, w_proj_sb, w_out, norm2_g, w_up, ffn_conv_w,
              ffn_conv_b, w_down):
    for l in range(DEPTH):
        hn = rmsnorm(x, norm1_g[l])
        proj = hn @ w_in[l]
        conv_bch, fox_qkv, fox_f, sb_qkv, gate_logits = jnp.split(proj, SPLIT_POINTS, axis=-1)

        cb, cc, ch = jnp.split(conv_bch, 3, axis=-1)
        y_conv = short_conv_mixer(cb, cc, ch, conv_w[l]) @ w_proj_conv[l]

        fq, fk, fv = jnp.split(fox_qkv, 3, axis=-1)
        log_f = jax.nn.log_sigmoid(fox_f.astype(jnp.float32) + fox_f_bias[l].astype(jnp.float32))
        y_fox = forgetting_attention(fq, fk, fv, log_f, fox_q_norm_g[l], fox_k_norm_g[l]) @ w_proj_fox[l]

        sq, sk, sv = jnp.split(sb_qkv, 3, axis=-1)
        y_sb = stick_breaking_attention(sq, sk, sv) @ w_proj_sb[l]

        g_conv, g_fox, g_sb = jnp.split(jax.nn.sigmoid(gate_logits + gate_bias[l]), 3, axis=-1)
        x = x + (g_conv * y_conv + g_fox * y_fox + g_sb * y_sb) @ w_out[l]

        hn = rmsnorm(x, norm2_g[l])
        u_gate, u_val = jnp.split(hn @ w_up[l], 2, axis=-1)
        act = jax.nn.silu(causal_dwconv(u_gate, ffn_conv_w[l]) + ffn_conv_b[l])
        x = x + (act * u_val) @ w_down[l]
    return x


import jax as _jax
import jax.numpy as _jnp

TWIN_FORMAT = 'train_step'
FWD_PARAMS = ['x', 'norm1_g', 'w_in', 'fox_f_bias', 'gate_bias', 'conv_w', 'fox_q_norm_g', 'fox_k_norm_g', 'w_proj_conv', 'w_proj_fox', 'w_proj_sb', 'w_out', 'norm2_g', 'w_up', 'ffn_conv_w', 'ffn_conv_b', 'w_down']
TWIN_WEIGHTS = ['norm1_g', 'w_in', 'fox_f_bias', 'gate_bias', 'conv_w', 'fox_q_norm_g', 'fox_k_norm_g', 'w_proj_conv', 'w_proj_fox', 'w_proj_sb', 'w_out', 'norm2_g', 'w_up', 'ffn_conv_w', 'ffn_conv_b', 'w_down']
TWIN_DIFF_INPUT = 'x'
TWIN_INPUTS = ['x', 'norm1_g', 'w_in', 'fox_f_bias', 'gate_bias', 'conv_w', 'fox_q_norm_g', 'fox_k_norm_g', 'w_proj_conv', 'w_proj_fox', 'w_proj_sb', 'w_out', 'norm2_g', 'w_up', 'ffn_conv_w', 'ffn_conv_b', 'w_down', 'loss_target', 'm_norm1_g', 'm_w_in', 'm_fox_f_bias', 'm_gate_bias', 'm_conv_w', 'm_fox_q_norm_g', 'm_fox_k_norm_g', 'm_w_proj_conv', 'm_w_proj_fox', 'm_w_proj_sb', 'm_w_out', 'm_norm2_g', 'm_w_up', 'm_ffn_conv_w', 'm_ffn_conv_b', 'm_w_down', 'v_norm1_g', 'v_w_in', 'v_fox_f_bias', 'v_gate_bias', 'v_conv_w', 'v_fox_q_norm_g', 'v_fox_k_norm_g', 'v_w_proj_conv', 'v_w_proj_fox', 'v_w_proj_sb', 'v_w_out', 'v_norm2_g', 'v_w_up', 'v_ffn_conv_w', 'v_ffn_conv_b', 'v_w_down']
TWIN_OUTPUTS = ['loss', 'grad_x', 'grad_norm1_g', 'grad_w_in', 'grad_fox_f_bias', 'grad_gate_bias', 'grad_conv_w', 'grad_fox_q_norm_g', 'grad_fox_k_norm_g', 'grad_w_proj_conv', 'grad_w_proj_fox', 'grad_w_proj_sb', 'grad_w_out', 'grad_norm2_g', 'grad_w_up', 'grad_ffn_conv_w', 'grad_ffn_conv_b', 'grad_w_down', 'delta_norm1_g', 'delta_w_in', 'delta_fox_f_bias', 'delta_gate_bias', 'delta_conv_w', 'delta_fox_q_norm_g', 'delta_fox_k_norm_g', 'delta_w_proj_conv', 'delta_w_proj_fox', 'delta_w_proj_sb', 'delta_w_out', 'delta_norm2_g', 'delta_w_up', 'delta_ffn_conv_w', 'delta_ffn_conv_b', 'delta_w_down', 'new_m_norm1_g', 'new_m_w_in', 'new_m_fox_f_bias', 'new_m_gate_bias', 'new_m_conv_w', 'new_m_fox_q_norm_g', 'new_m_fox_k_norm_g', 'new_m_w_proj_conv', 'new_m_w_proj_fox', 'new_m_w_proj_sb', 'new_m_w_out', 'new_m_norm2_g', 'new_m_w_up', 'new_m_ffn_conv_w', 'new_m_ffn_conv_b', 'new_m_w_down', 'new_v_norm1_g', 'new_v_w_in', 'new_v_fox_f_bias', 'new_v_gate_bias', 'new_v_conv_w', 'new_v_fox_q_norm_g', 'new_v_fox_k_norm_g', 'new_v_w_proj_conv', 'new_v_w_proj_fox', 'new_v_w_proj_sb', 'new_v_w_out', 'new_v_norm2_g', 'new_v_w_up', 'new_v_ffn_conv_w', 'new_v_ffn_conv_b', 'new_v_w_down']
TWIN_LEAF_KINDS = {'loss': 'loss', 'grad_x': 'grad_x', 'grad_norm1_g': 'grad_w', 'grad_w_in': 'grad_w', 'grad_fox_f_bias': 'grad_w', 'grad_gate_bias': 'grad_w', 'grad_conv_w': 'grad_w', 'grad_fox_q_norm_g': 'grad_w', 'grad_fox_k_norm_g': 'grad_w', 'grad_w_proj_conv': 'grad_w', 'grad_w_proj_fox': 'grad_w', 'grad_w_proj_sb': 'grad_w', 'grad_w_out': 'grad_w', 'grad_norm2_g': 'grad_w', 'grad_w_up': 'grad_w', 'grad_ffn_conv_w': 'grad_w', 'grad_ffn_conv_b': 'grad_w', 'grad_w_down': 'grad_w', 'delta_norm1_g': 'delta_w', 'delta_w_in': 'delta_w', 'delta_fox_f_bias': 'delta_w', 'delta_gate_bias': 'delta_w', 'delta_conv_w': 'delta_w', 'delta_fox_q_norm_g': 'delta_w', 'delta_fox_k_norm_g': 'delta_w', 'delta_w_proj_conv': 'delta_w', 'delta_w_proj_fox': 'delta_w', 'delta_w_proj_sb': 'delta_w', 'delta_w_out': 'delta_w', 'delta_norm2_g': 'delta_w', 'delta_w_up': 'delta_w', 'delta_ffn_conv_w': 'delta_w', 'delta_ffn_conv_b': 'delta_w', 'delta_w_down': 'delta_w', 'new_m_norm1_g': 'new_m', 'new_m_w_in': 'new_m', 'new_m_fox_f_bias': 'new_m', 'new_m_gate_bias': 'new_m', 'new_m_conv_w': 'new_m', 'new_m_fox_q_norm_g': 'new_m', 'new_m_fox_k_norm_g': 'new_m', 'new_m_w_proj_conv': 'new_m', 'new_m_w_proj_fox': 'new_m', 'new_m_w_proj_sb': 'new_m', 'new_m_w_out': 'new_m', 'new_m_norm2_g': 'new_m', 'new_m_w_up': 'new_m', 'new_m_ffn_conv_w': 'new_m', 'new_m_ffn_conv_b': 'new_m', 'new_m_w_down': 'new_m', 'new_v_norm1_g': 'new_v', 'new_v_w_in': 'new_v', 'new_v_fox_f_bias': 'new_v', 'new_v_gate_bias': 'new_v', 'new_v_conv_w': 'new_v', 'new_v_fox_q_norm_g': 'new_v', 'new_v_fox_k_norm_g': 'new_v', 'new_v_w_proj_conv': 'new_v', 'new_v_w_proj_fox': 'new_v', 'new_v_w_proj_sb': 'new_v', 'new_v_w_out': 'new_v', 'new_v_norm2_g': 'new_v', 'new_v_w_up': 'new_v', 'new_v_ffn_conv_w': 'new_v', 'new_v_ffn_conv_b': 'new_v', 'new_v_w_down': 'new_v'}


def _forward(args):
    return _fwd_reference(*[args[k] for k in FWD_PARAMS])


def _output_shape():
    out = _jax.eval_shape(lambda: _forward(_fwd_setup_inputs(0)))
    return out.shape, out.dtype

N_MICROBATCH = 1
ADAM_LR = 0.001
ADAM_B1 = 0.9
ADAM_B2 = 0.999
ADAM_EPS = 1e-08
ADAM_WD = 0.01
ADAM_STEP = 10
PER_EXAMPLE_BATCH_AXIS = {'x': 0, 'loss_target': 0}
SHARED_INPUTS = []
_WEIGHT_DTYPES = {'norm1_g': _jnp.float32, 'w_in': _jnp.float32, 'fox_f_bias': _jnp.float32, 'gate_bias': _jnp.float32, 'conv_w': _jnp.float32, 'fox_q_norm_g': _jnp.float32, 'fox_k_norm_g': _jnp.float32, 'w_proj_conv': _jnp.float32, 'w_proj_fox': _jnp.float32, 'w_proj_sb': _jnp.float32, 'w_out': _jnp.float32, 'norm2_g': _jnp.float32, 'w_up': _jnp.float32, 'ffn_conv_w': _jnp.float32, 'ffn_conv_b': _jnp.float32, 'w_down': _jnp.float32}
MOMENT_SCALE = {'norm1_g': 1.838589e+01, 'w_in': 3.639925e-01, 'fox_f_bias': 3.395545e+01, 'gate_bias': 1.022928e+00, 'conv_w': 5.204283e+00, 'fox_q_norm_g': 4.268794e+00, 'fox_k_norm_g': 4.298602e+00, 'w_proj_conv': 4.951790e-01, 'w_proj_fox': 1.987788e-01, 'w_proj_sb': 3.022407e-01, 'w_out': 6.073983e-01, 'norm2_g': 1.278243e+01, 'w_up': 2.679511e-01, 'ffn_conv_w': 1.364594e+00, 'ffn_conv_b': 1.693718e+00, 'w_down': 4.383599e-01}


def _to_microbatches(a, axis):
    t = _jnp.moveaxis(a, axis, 0)
    t = t.reshape((N_MICROBATCH, t.shape[0] // N_MICROBATCH) + t.shape[1:])
    return _jnp.moveaxis(t, 1, axis + 1)


def setup_inputs(seed: int = 0) -> dict:
    inp = _fwd_setup_inputs(seed)
    key = _jax.random.fold_in(_jax.random.key(seed), 7919)
    shape, _ = _output_shape()
    out = dict(inp)
    out["loss_target"] = _jax.random.normal(_jax.random.fold_in(key, 0), shape, _jnp.float32)
    for i, name in enumerate(TWIN_WEIGHTS):
        w = inp[name].astype(_jnp.float32)
        if MOMENT_SCALE is None:
            s = _jnp.sqrt(_jnp.mean(_jnp.square(w)) + 1e-30)
        else:
            s = MOMENT_SCALE[name]
        km, kv = _jax.random.split(_jax.random.fold_in(key, i + 1))
        out[name] = w
        out["m_" + name] = s * _jax.random.normal(km, w.shape, _jnp.float32)
        out["v_" + name] = (s * s) * _jax.random.uniform(kv, w.shape, _jnp.float32, 0.5, 1.5)
    if N_MICROBATCH > 1:
        for name, axis in PER_EXAMPLE_BATCH_AXIS.items():
            out[name] = _to_microbatches(out[name], axis)
    return {'x': out['x'], 'norm1_g': out['norm1_g'], 'w_in': out['w_in'], 'fox_f_bias': out['fox_f_bias'], 'gate_bias': out['gate_bias'], 'conv_w': out['conv_w'], 'fox_q_norm_g': out['fox_q_norm_g'], 'fox_k_norm_g': out['fox_k_norm_g'], 'w_proj_conv': out['w_proj_conv'], 'w_proj_fox': out['w_proj_fox'], 'w_proj_sb': out['w_proj_sb'], 'w_out': out['w_out'], 'norm2_g': out['norm2_g'], 'w_up': out['w_up'], 'ffn_conv_w': out['ffn_conv_w'], 'ffn_conv_b': out['ffn_conv_b'], 'w_down': out['w_down'], 'loss_target': out['loss_target'], 'm_norm1_g': out['m_norm1_g'], 'm_w_in': out['m_w_in'], 'm_fox_f_bias': out['m_fox_f_bias'], 'm_gate_bias': out['m_gate_bias'], 'm_conv_w': out['m_conv_w'], 'm_fox_q_norm_g': out['m_fox_q_norm_g'], 'm_fox_k_norm_g': out['m_fox_k_norm_g'], 'm_w_proj_conv': out['m_w_proj_conv'], 'm_w_proj_fox': out['m_w_proj_fox'], 'm_w_proj_sb': out['m_w_proj_sb'], 'm_w_out': out['m_w_out'], 'm_norm2_g': out['m_norm2_g'], 'm_w_up': out['m_w_up'], 'm_ffn_conv_w': out['m_ffn_conv_w'], 'm_ffn_conv_b': out['m_ffn_conv_b'], 'm_w_down': out['m_w_down'], 'v_norm1_g': out['v_norm1_g'], 'v_w_in': out['v_w_in'], 'v_fox_f_bias': out['v_fox_f_bias'], 'v_gate_bias': out['v_gate_bias'], 'v_conv_w': out['v_conv_w'], 'v_fox_q_norm_g': out['v_fox_q_norm_g'], 'v_fox_k_norm_g': out['v_fox_k_norm_g'], 'v_w_proj_conv': out['v_w_proj_conv'], 'v_w_proj_fox': out['v_w_proj_fox'], 'v_w_proj_sb': out['v_w_proj_sb'], 'v_w_out': out['v_w_out'], 'v_norm2_g': out['v_norm2_g'], 'v_w_up': out['v_w_up'], 'v_ffn_conv_w': out['v_ffn_conv_w'], 'v_ffn_conv_b': out['v_ffn_conv_b'], 'v_w_down': out['v_w_down']}


def _loss(weights, diff, rest, loss_target):
    with _jax.named_scope("forward"):
        args = {**rest, TWIN_DIFF_INPUT: diff, **{k: w.astype(_WEIGHT_DTYPES[k]) for k, w in weights.items()}}
        y = _forward(args)
    with _jax.named_scope("loss_head"):
        err = _jnp.square(y.astype(_jnp.float32) - loss_target)
        return 0.5 * _jnp.sum(_jnp.mean(err, axis=-1)) if err.ndim else 0.5 * err


def _adamw(w, g, m, v):
    m = ADAM_B1 * m + (1.0 - ADAM_B1) * g
    v = ADAM_B2 * v + (1.0 - ADAM_B2) * _jnp.square(g)
    m_hat = m / (1.0 - ADAM_B1 ** ADAM_STEP)
    v_hat = v / (1.0 - ADAM_B2 ** ADAM_STEP)
    delta = -ADAM_LR * (m_hat / (_jnp.sqrt(v_hat) + ADAM_EPS) + ADAM_WD * w)
    return delta, m, v


def reference(x, norm1_g, w_in, fox_f_bias, gate_bias, conv_w, fox_q_norm_g, fox_k_norm_g, w_proj_conv, w_proj_fox, w_proj_sb, w_out, norm2_g, w_up, ffn_conv_w, ffn_conv_b, w_down, loss_target, m_norm1_g, m_w_in, m_fox_f_bias, m_gate_bias, m_conv_w, m_fox_q_norm_g, m_fox_k_norm_g, m_w_proj_conv, m_w_proj_fox, m_w_proj_sb, m_w_out, m_norm2_g, m_w_up, m_ffn_conv_w, m_ffn_conv_b, m_w_down, v_norm1_g, v_w_in, v_fox_f_bias, v_gate_bias, v_conv_w, v_fox_q_norm_g, v_fox_k_norm_g, v_w_proj_conv, v_w_proj_fox, v_w_proj_sb, v_w_out, v_norm2_g, v_w_up, v_ffn_conv_w, v_ffn_conv_b, v_w_down):
    given = dict(x=x, norm1_g=norm1_g, w_in=w_in, fox_f_bias=fox_f_bias, gate_bias=gate_bias, conv_w=conv_w, fox_q_norm_g=fox_q_norm_g, fox_k_norm_g=fox_k_norm_g, w_proj_conv=w_proj_conv, w_proj_fox=w_proj_fox, w_proj_sb=w_proj_sb, w_out=w_out, norm2_g=norm2_g, w_up=w_up, ffn_conv_w=ffn_conv_w, ffn_conv_b=ffn_conv_b, w_down=w_down, loss_target=loss_target, m_norm1_g=m_norm1_g, m_w_in=m_w_in, m_fox_f_bias=m_fox_f_bias, m_gate_bias=m_gate_bias, m_conv_w=m_conv_w, m_fox_q_norm_g=m_fox_q_norm_g, m_fox_k_norm_g=m_fox_k_norm_g, m_w_proj_conv=m_w_proj_conv, m_w_proj_fox=m_w_proj_fox, m_w_proj_sb=m_w_proj_sb, m_w_out=m_w_out, m_norm2_g=m_norm2_g, m_w_up=m_w_up, m_ffn_conv_w=m_ffn_conv_w, m_ffn_conv_b=m_ffn_conv_b, m_w_down=m_w_down, v_norm1_g=v_norm1_g, v_w_in=v_w_in, v_fox_f_bias=v_fox_f_bias, v_gate_bias=v_gate_bias, v_conv_w=v_conv_w, v_fox_q_norm_g=v_fox_q_norm_g, v_fox_k_norm_g=v_fox_k_norm_g, v_w_proj_conv=v_w_proj_conv, v_w_proj_fox=v_w_proj_fox, v_w_proj_sb=v_w_proj_sb, v_w_out=v_w_out, v_norm2_g=v_norm2_g, v_w_up=v_w_up, v_ffn_conv_w=v_ffn_conv_w, v_ffn_conv_b=v_ffn_conv_b, v_w_down=v_w_down)
    weights = {n: given[n] for n in TWIN_WEIGHTS}
    shared = {n: given[n] for n in SHARED_INPUTS}
    per_example = {n: given[n] for n in ['x']}
    grad_fn = _jax.value_and_grad(_loss, argnums=(0, 1))

    def one_microbatch(ex, loss_target):
        ex = dict(ex)
        diff = ex.pop(TWIN_DIFF_INPUT)
        return grad_fn(weights, diff, {**shared, **ex}, loss_target)

    if N_MICROBATCH == 1:
        loss, (grad_w, grad_x) = one_microbatch(per_example, given["loss_target"])
    else:
        def body(carry, xs):
            loss_sum, grad_sum = carry
            l_k, (gw_k, gx_k) = one_microbatch(xs[0], xs[1])
            with _jax.named_scope("update"):
                return (loss_sum + l_k, _jax.tree.map(_jnp.add, grad_sum, gw_k)), gx_k

        init = (_jnp.zeros((), _jnp.float32), _jax.tree.map(_jnp.zeros_like, weights))
        (loss, grad_w), grad_x = _jax.lax.scan(body, init, (per_example, given["loss_target"]))
    with _jax.named_scope("update"):
        delta_w, new_m, new_v = {}, {}, {}
        for n in TWIN_WEIGHTS:
            delta_w[n], new_m[n], new_v[n] = _adamw(weights[n], grad_w[n], given["m_" + n], given["v_" + n])
    return (loss, grad_x, *[grad_w[n] for n in TWIN_WEIGHTS], *[delta_w[n] for n in TWIN_WEIGHTS],
            *[new_m[n] for n in TWIN_WEIGHTS], *[new_v[n] for n in TWIN_WEIGHTS])
```

```python
import functools

import numpy as np
import jax
import jax.numpy as jnp
from jax import lax
from jax.experimental import pallas as pl
from jax.experimental.pallas import tpu as pltpu

F32 = jnp.float32
BF16 = jnp.bfloat16

D_MODEL = 1024
DEPTH = 4
HEAD_DIM = 64
N_HEADS = 8
WIDTH = 512
D_FF = 2816
NORM_EPS = 1e-6
NEG_INF = -1e30
QK_SCALE = HEAD_DIM ** -0.5
LANES = 128
N_PAIR = N_HEADS // 2

GATE_OFF = 0
CONV_OFF = 3 * D_MODEL
FOX_OFF = CONV_OFF + 3 * WIDTH
SB_OFF = FOX_OFF + 3 * WIDTH
D_INA = SB_OFF + 3 * WIDTH

ADAM_LR = 0.001
ADAM_B1 = 0.9
ADAM_B2 = 0.999
ADAM_EPS = 1e-08
ADAM_WD = 0.01
ADAM_STEP = 10

VMEM_LIMIT = 48 * 1024 * 1024
ADAMW_BLOCK_BYTES = 1024 * 1024

PACK_COLS = 1024
PACK_ROW_ALIGN = 32
MESH = pl.DeviceIdType.MESH


def _params(sem):
    return pltpu.CompilerParams(dimension_semantics=sem, vmem_limit_bytes=VMEM_LIMIT)


def _dot(a, b, dims):
    return lax.dot_general(a, b, (dims, ((), ())), preferred_element_type=F32)


_NN = ((1,), (0,))
_NT = ((1,), (1,))
_TN = ((0,), (0,))


def _pick(dim, pref):
    if dim <= pref:
        return dim
    best = None
    for mult in range(1, dim // LANES + 1):
        t = mult * LANES
        if t <= pref and dim % t == 0:
            best = t
    assert best is not None, (dim, pref)
    return best


def _mm(a, b, mode, out_dtype=F32, tm=512, tn=512, tk=2048, res=None, name="mm"):
    if mode == "nn":
        (m, k), (_, n) = a.shape, b.shape
    elif mode == "nt":
        (m, k), (n, _) = a.shape, b.shape
    else:
        (k, m), (_, n) = a.shape, b.shape
    tm, tn, tk = _pick(m, tm), _pick(n, tn), _pick(k, tk)
    nk = k // tk
    dims = {"nn": _NN, "nt": _NT, "tn": _TN}[mode]
    if mode == "tn":
        a_spec = pl.BlockSpec((tk, tm), lambda i, j, kk: (kk, i))
    else:
        a_spec = pl.BlockSpec((tm, tk), lambda i, j, kk: (i, kk))
    if mode == "nt":
        b_spec = pl.BlockSpec((tn, tk), lambda i, j, kk: (j, kk))
    else:
        b_spec = pl.BlockSpec((tk, tn), lambda i, j, kk: (kk, j))
    o_spec = pl.BlockSpec((tm, tn), lambda i, j, kk: (i, j))
    in_specs = [a_spec, b_spec] + ([o_spec] if res is not None else [])

    def body(*refs):
        if res is not None:
            a_ref, b_ref, r_ref, o_ref, acc_ref = refs
        else:
            a_ref, b_ref, o_ref, acc_ref = refs
            r_ref = None
        kk = pl.program_id(2)
        part = _dot(a_ref[...].astype(BF16), b_ref[...].astype(BF16), dims)

        def finish(total):
            if r_ref is not None:
                total = total + r_ref[...].astype(F32)
            o_ref[...] = total.astype(out_dtype)

        if nk == 1:
            finish(part)
        else:
            @pl.when(kk == 0)
            def _():
                acc_ref[...] = part

            @pl.when(kk > 0)
            def _():
                acc_ref[...] += part

            @pl.when(kk == nk - 1)
            def _():
                finish(acc_ref[...])

    args = (a, b) + ((res,) if res is not None else ())
    return pl.pallas_call(
        body, name=name, grid=(m // tm, n // tn, nk), in_specs=in_specs, out_specs=o_spec,
        out_shape=jax.ShapeDtypeStruct((m, n), out_dtype),
        scratch_shapes=[pltpu.VMEM((tm, tn) if nk > 1 else (8, LANES), F32)],
        compiler_params=_params(("parallel", "parallel", "arbitrary")),
    )(*args)


def _norm_mm(x, g, w, out_dtype, tm=256, tn=512, name="norm_mm"):
    m, d = x.shape
    n = w.shape[1]
    tm, tn = min(tm, m), min(tn, n)
    assert m % tm == 0 and n % tn == 0

    def body(x_ref, g_ref, w_ref, o_ref, hn_ref):
        @pl.when(pl.program_id(1) == 0)
        def _():
            xf = x_ref[...]
            r = lax.rsqrt(jnp.mean(xf * xf, axis=-1, keepdims=True) + NORM_EPS)
            hn_ref[...] = (xf * r * g_ref[...]).astype(BF16)

        o_ref[...] = _dot(hn_ref[...], w_ref[...], _NN).astype(out_dtype)

    return pl.pallas_call(
        body, name=name, grid=(m // tm, n // tn),
        in_specs=[pl.BlockSpec((tm, d), lambda i, j: (i, 0)),
                  pl.BlockSpec((1, d), lambda i, j: (0, 0)),
                  pl.BlockSpec((d, tn), lambda i, j: (0, j))],
        out_specs=[pl.BlockSpec((tm, tn), lambda i, j: (i, j)),
                   pl.BlockSpec((tm, d), lambda i, j: (i, 0))],
        out_shape=[jax.ShapeDtypeStruct((m, n), out_dtype), jax.ShapeDtypeStruct((m, d), BF16)],
        compiler_params=_params(("parallel", "arbitrary")),
    )(x, g.reshape(1, d), w)


def _norm_bwd(dhn, x, g, dx_in, tm=256, name="norm_bwd"):
    m, d = x.shape
    tm = min(tm, m)

    def body(dhn_ref, x_ref, g_ref, dxi_ref, dx_ref, gg_ref):
        xf = x_ref[...]
        r = lax.rsqrt(jnp.mean(xf * xf, axis=-1, keepdims=True) + NORM_EPS)
        xhat = xf * r
        dh = dhn_ref[...].astype(F32)
        dxn = dh * g_ref[...]
        mean = jnp.mean(dxn * xhat, axis=-1, keepdims=True)
        dx_ref[...] = dxi_ref[...] + r * (dxn - xhat * mean)
        part = jnp.sum(dh * xhat, axis=0, keepdims=True)

        @pl.when(pl.program_id(0) == 0)
        def _():
            gg_ref[...] = part

        @pl.when(pl.program_id(0) > 0)
        def _():
            gg_ref[...] += part

    row = pl.BlockSpec((tm, d), lambda i: (i, 0))
    vec = pl.BlockSpec((1, d), lambda i: (0, 0))
    dx, gg = pl.pallas_call(
        body, name=name, grid=(m // tm,), in_specs=[row, row, vec, row], out_specs=[row, vec],
        out_shape=[jax.ShapeDtypeStruct((m, d), F32), jax.ShapeDtypeStruct((1, d), F32)],
        compiler_params=_params(("arbitrary",)),
    )(dhn, x, g.reshape(1, d), dx_in)
    return dx, gg.reshape(d)


def _down(u, k):
    s = u.shape[0]
    rows = lax.broadcasted_iota(jnp.int32, u.shape, 0)
    return jnp.where(rows < k, 0.0, pltpu.roll(u, k, axis=0))


def _up(u, k):
    s = u.shape[0]
    rows = lax.broadcasted_iota(jnp.int32, u.shape, 0)
    return jnp.where(rows >= s - k, 0.0, pltpu.roll(u, s - k, axis=0))


def _conv_mix_fwd(proj, conv_w, tc=128):
    s = proj.shape[0]
    nb = WIDTH // tc
    off = CONV_OFF // tc

    def body(b_ref, c_ref, h_ref, w_ref, o_ref):
        u = c_ref[...].astype(F32) * h_ref[...].astype(F32)
        w = w_ref[...]
        cv = w[0:1] * _down(u, 2) + w[1:2] * _down(u, 1) + w[2:3] * u
        o_ref[...] = (b_ref[...].astype(F32) * cv).astype(BF16)

    col = lambda k: pl.BlockSpec((s, tc), lambda j, k=k: (0, off + k * nb + j))
    return pl.pallas_call(
        body, name="conv_mix_fwd", grid=(nb,),
        in_specs=[col(0), col(1), col(2), pl.BlockSpec((3, tc), lambda j: (0, j))],
        out_specs=pl.BlockSpec((s, tc), lambda j: (0, j)),
        out_shape=jax.ShapeDtypeStruct((s, WIDTH), BF16),
        compiler_params=_params(("parallel",)),
    )(proj, proj, proj, conv_w)


def _conv_mix_bwd(do, proj, conv_w, tc=128):
    s = proj.shape[0]
    nb = WIDTH // tc
    off = CONV_OFF // tc

    def body(do_ref, b_ref, c_ref, h_ref, w_ref, db_ref, dc_ref, dh_ref, gw_ref):
        b = b_ref[...].astype(F32)
        c = c_ref[...].astype(F32)
        h = h_ref[...].astype(F32)
        g = do_ref[...].astype(F32)
        w = w_ref[...]
        u = c * h
        u1, u2 = _down(u, 1), _down(u, 2)
        cv = w[0:1] * u2 + w[1:2] * u1 + w[2:3] * u
        db_ref[...] = (g * cv).astype(BF16)
        dcv = g * b
        gw_ref[0:1, :] = jnp.sum(dcv * u2, axis=0, keepdims=True)
        gw_ref[1:2, :] = jnp.sum(dcv * u1, axis=0, keepdims=True)
        gw_ref[2:3, :] = jnp.sum(dcv * u, axis=0, keepdims=True)
        du = w[2:3] * dcv + w[1:2] * _up(dcv, 1) + w[0:1] * _up(dcv, 2)
        dc_ref[...] = (du * h).astype(BF16)
        dh_ref[...] = (du * c).astype(BF16)

    col = lambda k: pl.BlockSpec((s, tc), lambda j, k=k: (0, off + k * nb + j))
    own = pl.BlockSpec((s, tc), lambda j: (0, j))
    wsp = pl.BlockSpec((3, tc), lambda j: (0, j))
    act = jax.ShapeDtypeStruct((s, WIDTH), BF16)
    return pl.pallas_call(
        body, name="conv_mix_bwd", grid=(nb,),
        in_specs=[own, col(0), col(1), col(2), wsp], out_specs=[own, own, own, wsp],
        out_shape=[act, act, act, jax.ShapeDtypeStruct((3, WIDTH), F32)],
        compiler_params=_params(("parallel",)),
    )(do, proj, proj, proj, conv_w)


def _glu_fwd(up, w, b, tc=256):
    s = up.shape[0]
    nb = D_FF // tc

    def body(g_ref, v_ref, w_ref, b_ref, o_ref):
        ug = g_ref[...].astype(F32)
        wv = w_ref[...]
        cg = wv[0:1] * _down(ug, 2) + wv[1:2] * _down(ug, 1) + wv[2:3] * ug + b_ref[...]
        act = cg * jax.nn.sigmoid(cg)
        o_ref[...] = (act * v_ref[...].astype(F32)).astype(BF16)

    return pl.pallas_call(
        body, name="glu_fwd", grid=(nb,),
        in_specs=[pl.BlockSpec((s, tc), lambda j: (0, j)), pl.BlockSpec((s, tc), lambda j: (0, nb + j)),
                  pl.BlockSpec((3, tc), lambda j: (0, j)), pl.BlockSpec((1, tc), lambda j: (0, j))],
        out_specs=pl.BlockSpec((s, tc), lambda j: (0, j)),
        out_shape=jax.ShapeDtypeStruct((s, D_FF), BF16),
        compiler_params=_params(("parallel",)),
    )(up, up, w, b.reshape(1, D_FF))


def _glu_bwd(dh, up, w, b, tc=256):
    s = up.shape[0]
    nb = D_FF // tc

    def body(dh_ref, g_ref, v_ref, w_ref, b_ref, dg_ref, dv_ref, gw_ref, gb_ref):
        ug = g_ref[...].astype(F32)
        uv = v_ref[...].astype(F32)
        d = dh_ref[...].astype(F32)
        wv = w_ref[...]
        u1, u2 = _down(ug, 1), _down(ug, 2)
        cg = wv[0:1] * u2 + wv[1:2] * u1 + wv[2:3] * ug + b_ref[...]
        sg = jax.nn.sigmoid(cg)
        dv_ref[...] = (d * (cg * sg)).astype(BF16)
        dcg = d * uv * (sg * (1.0 + cg * (1.0 - sg)))
        gb_ref[...] = jnp.sum(dcg, axis=0, keepdims=True)
        gw_ref[0:1, :] = jnp.sum(dcg * u2, axis=0, keepdims=True)
        gw_ref[1:2, :] = jnp.sum(dcg * u1, axis=0, keepdims=True)
        gw_ref[2:3, :] = jnp.sum(dcg * ug, axis=0, keepdims=True)
        dg_ref[...] = (wv[2:3] * dcg + wv[1:2] * _up(dcg, 1) + wv[0:1] * _up(dcg, 2)).astype(BF16)

    own = pl.BlockSpec((s, tc), lambda j: (0, j))
    wsp = pl.BlockSpec((3, tc), lambda j: (0, j))
    bsp = pl.BlockSpec((1, tc), lambda j: (0, j))
    act = jax.ShapeDtypeStruct((s, D_FF), BF16)
    dg, dv, gw, gb = pl.pallas_call(
        body, name="glu_bwd", grid=(nb,),
        in_specs=[own, own, pl.BlockSpec((s, tc), lambda j: (0, nb + j)), wsp, bsp],
        out_specs=[own, own, wsp, bsp],
        out_shape=[act, act, jax.ShapeDtypeStruct((3, D_FF), F32), jax.ShapeDtypeStruct((1, D_FF), F32)],
        compiler_params=_params(("parallel",)),
    )(dh, up, up, w, b.reshape(1, D_FF))
    return dg, dv, gw, gb.reshape(D_FF)


def _merge_fwd(x, oc, of, osb, wpc, wpf, wps, proj, gb, wout, tm=256):
    s, d = x.shape
    tm = min(tm, s)

    def body(x_ref, oc_ref, of_ref, os_ref, wpc_ref, wpf_ref, wps_ref, g0_ref, g1_ref, g2_ref, gb_ref, wo_ref,
             xo_ref, mg_ref, yc_ref, yf_ref, ys_ref):
        merged = jnp.zeros((tm, d), F32)
        for k, (o_ref, w_ref, g_ref, y_ref) in enumerate(
                ((oc_ref, wpc_ref, g0_ref, yc_ref), (of_ref, wpf_ref, g1_ref, yf_ref), (os_ref, wps_ref, g2_ref, ys_ref))):
            y = _dot(o_ref[...].astype(BF16), w_ref[...], _NN)
            y_ref[...] = y.astype(BF16)
            gate = jax.nn.sigmoid(g_ref[...].astype(F32) + gb_ref[:, k * d:(k + 1) * d])
            merged = merged + gate * y
        mb = merged.astype(BF16)
        mg_ref[...] = mb
        xo_ref[...] = x_ref[...] + _dot(mb, wo_ref[...], _NN)

    rowd = pl.BlockSpec((tm, d), lambda i: (i, 0))
    roww = pl.BlockSpec((tm, WIDTH), lambda i: (i, 0))
    wp = pl.BlockSpec((WIDTH, d), lambda i: (0, 0))
    gcol = lambda k: pl.BlockSpec((tm, d), lambda i, k=k: (i, GATE_OFF // d + k))
    actd = jax.ShapeDtypeStruct((s, d), BF16)
    return pl.pallas_call(
        body, name="merge_fwd", grid=(s // tm,),
        in_specs=[rowd, roww, roww, roww, wp, wp, wp, gcol(0), gcol(1), gcol(2),
                  pl.BlockSpec((1, 3 * d), lambda i: (0, 0)), pl.BlockSpec((d, d), lambda i: (0, 0))],
        out_specs=[rowd, rowd, rowd, rowd, rowd],
        out_shape=[jax.ShapeDtypeStruct((s, d), F32), actd, actd, actd, actd],
        compiler_params=_params(("parallel",)),
    )(x, oc, of, osb, wpc, wpf, wps, proj, proj, proj, gb.reshape(1, 3 * d), wout)


def _gate_bwd(dm, yc, yf, ys, proj, gb, tm=256):
    s, d = dm.shape
    tm = min(tm, s)

    def body(dm_ref, yc_ref, yf_ref, ys_ref, g0_ref, g1_ref, g2_ref, gb_ref, dyc_ref, dyf_ref, dys_ref, dgl_ref, ggb_ref):
        g = dm_ref[...].astype(F32)
        parts = []
        for k, (y_ref, g_ref, dy_ref) in enumerate(((yc_ref, g0_ref, dyc_ref), (yf_ref, g1_ref, dyf_ref), (ys_ref, g2_ref, dys_ref))):
            gate = jax.nn.sigmoid(g_ref[...].astype(F32) + gb_ref[:, k * d:(k + 1) * d])
            dy_ref[...] = (g * gate).astype(BF16)
            dgl = g * y_ref[...].astype(F32) * gate * (1.0 - gate)
            dgl_ref[:, k * d:(k + 1) * d] = dgl.astype(BF16)
            parts.append(jnp.sum(dgl, axis=0, keepdims=True))
        part = jnp.concatenate(parts, axis=1)

        @pl.when(pl.program_id(0) == 0)
        def _():
            ggb_ref[...] = part

        @pl.when(pl.program_id(0) > 0)
        def _():
            ggb_ref[...] += part

    rowd = pl.BlockSpec((tm, d), lambda i: (i, 0))
    gcol = lambda k: pl.BlockSpec((tm, d), lambda i, k=k: (i, GATE_OFF // d + k))
    vec = pl.BlockSpec((1, 3 * d), lambda i: (0, 0))
    actd = jax.ShapeDtypeStruct((s, d), BF16)
    dyc, dyf, dys, dgl, ggb = pl.pallas_call(
        body, name="gate_bwd", grid=(s // tm,),
        in_specs=[rowd, rowd, rowd, rowd, gcol(0), gcol(1), gcol(2), vec],
        out_specs=[rowd, rowd, rowd, pl.BlockSpec((tm, 3 * d), lambda i: (i, 0)), vec],
        out_shape=[actd, actd, actd, jax.ShapeDtypeStruct((s, 3 * d), BF16), jax.ShapeDtypeStruct((1, 3 * d), F32)],
        compiler_params=_params(("arbitrary",)),
    )(dm, yc, yf, ys, proj, proj, proj, gb.reshape(1, 3 * d))
    return dyc, dyf, dys, dgl, ggb.reshape(3 * d)


def _tri_tables(n, order):
    qs, ks = [], []
    if order == "k_outer":
        for kj in range(n):
            for qi in range(kj, n):
                qs.append(qi)
                ks.append(kj)
    else:
        for qi in range(n):
            for kj in (range(qi, -1, -1) if order == "k_desc" else range(qi + 1)):
                qs.append(qi)
                ks.append(kj)
    return jnp.asarray(np.array(qs, np.int32)), jnp.asarray(np.array(ks, np.int32))


def _lo_mask(shape):
    return lax.broadcasted_iota(jnp.int32, shape, len(shape) - 1) < HEAD_DIM


def _head(x, h):
    lo = _lo_mask(x.shape)
    return jnp.where(lo if h == 0 else jnp.logical_not(lo), x, jnp.zeros_like(x))


def _pair_cols(x, h):
    return x[:, h * HEAD_DIM:h * HEAD_DIM + 1]


def _rep(a0, a1, shape):
    return jnp.where(_lo_mask(shape), a0, a1)


def _positions(qi, kj, t):
    row = qi * t + lax.broadcasted_iota(jnp.int32, (t, t), 0)
    col = kj * t + lax.broadcasted_iota(jnp.int32, (t, t), 1)
    return row, col


def _head_norm(x, g):
    lo = _lo_mask(x.shape)
    sq = x * x
    s0 = jnp.sum(jnp.where(lo, sq, 0.0), axis=-1, keepdims=True)
    s1 = jnp.sum(jnp.where(lo, 0.0, sq), axis=-1, keepdims=True)
    r = jnp.where(lo, lax.rsqrt(s0 / HEAD_DIM + NORM_EPS), lax.rsqrt(s1 / HEAD_DIM + NORM_EPS))
    return x * r, r


def _fox_prep(proj, gq, gk, tm=512):
    s = proj.shape[0]
    tm = min(tm, s)
    off = FOX_OFF // LANES

    def body(q_ref, k_ref, gq_ref, gk_ref, qn_ref, kn_ref):
        qh, _ = _head_norm(q_ref[...].astype(F32), None)
        kh, _ = _head_norm(k_ref[...].astype(F32), None)
        qn_ref[...] = (qh * gq_ref[...] * QK_SCALE).astype(BF16)
        kn_ref[...] = (kh * gk_ref[...]).astype(BF16)

    vec = pl.BlockSpec((1, LANES), lambda p, i: (0, 0))
    own = pl.BlockSpec((tm, LANES), lambda p, i: (i, p))
    act = jax.ShapeDtypeStruct((s, WIDTH), BF16)
    return pl.pallas_call(
        body, name="fox_prep", grid=(N_PAIR, s // tm),
        in_specs=[pl.BlockSpec((tm, LANES), lambda p, i: (i, off + p)),
                  pl.BlockSpec((tm, LANES), lambda p, i: (i, off + N_PAIR + p)), vec, vec],
        out_specs=[own, own], out_shape=[act, act],
        compiler_params=_params(("parallel", "parallel")),
    )(proj, proj, jnp.tile(gq, 2).reshape(1, LANES), jnp.tile(gk, 2).reshape(1, LANES))


def _fox_post(dqs, dkn, proj, gq, gk, tm=512):
    s = proj.shape[0]
    tm = min(tm, s)
    off = FOX_OFF // LANES

    def one(d_ref, x_ref, g_ref, scale, dx_ref, gg_ref, first):
        xhat, r = _head_norm(x_ref[...].astype(F32), None)
        dy = d_ref[...] * scale
        part = jnp.sum(dy * xhat, axis=0, keepdims=True)

        @pl.when(first)
        def _():
            gg_ref[...] = part

        @pl.when(jnp.logical_not(first))
        def _():
            gg_ref[...] += part

        dxh = dy * g_ref[...]
        lo = _lo_mask(dxh.shape)
        pr = dxh * xhat
        m0 = jnp.sum(jnp.where(lo, pr, 0.0), axis=-1, keepdims=True)
        m1 = jnp.sum(jnp.where(lo, 0.0, pr), axis=-1, keepdims=True)
        mean = jnp.where(lo, m0, m1) / HEAD_DIM
        dx_ref[...] = (r * (dxh - xhat * mean)).astype(BF16)

    def body(dq_ref, dk_ref, q_ref, k_ref, gq_ref, gk_ref, dxq_ref, dxk_ref, ggq_ref, ggk_ref):
        first = pl.program_id(1) == 0
        one(dq_ref, q_ref, gq_ref, QK_SCALE, dxq_ref, ggq_ref, first)
        one(dk_ref, k_ref, gk_ref, 1.0, dxk_ref, ggk_ref, first)

    vec = pl.BlockSpec((1, LANES), lambda p, i: (0, 0))
    own = pl.BlockSpec((tm, LANES), lambda p, i: (i, p))
    ggs = pl.BlockSpec((None, 1, LANES), lambda p, i: (p, 0, 0))
    act = jax.ShapeDtypeStruct((s, WIDTH), BF16)
    ggo = jax.ShapeDtypeStruct((N_PAIR, 1, LANES), F32)
    dxq, dxk, ggq, ggk = pl.pallas_call(
        body, name="fox_post", grid=(N_PAIR, s // tm),
        in_specs=[own, own, pl.BlockSpec((tm, LANES), lambda p, i: (i, off + p)),
                  pl.BlockSpec((tm, LANES), lambda p, i: (i, off + N_PAIR + p)), vec, vec],
        out_specs=[own, own, ggs, ggs], out_shape=[act, act, ggo, ggo],
        compiler_params=_params(("parallel", "arbitrary")),
    )(dqs, dkn, proj, proj, jnp.tile(gq, 2).reshape(1, LANES), jnp.tile(gk, 2).reshape(1, LANES))
    fold = lambda a: a.reshape(N_HEADS, HEAD_DIM).sum(axis=0)
    return dxq, dxk, fold(ggq), fold(ggk)


def _split3(x):
    a = x.astype(BF16)
    r = x - a.astype(F32)
    b = r.astype(BF16)
    c = (r - b.astype(F32)).astype(BF16)
    return a, b, c


def _split2(x):
    a = x.astype(BF16)
    b = (x - a.astype(F32)).astype(BF16)
    return a, b


def _log_sigmoid(x):
    return jnp.minimum(x, 0.0) - jnp.log(1.0 + jnp.exp(-jnp.abs(x)))


def _fox_gates(ft, bias):
    h, s = ft.shape
    nb = s // LANES

    def body(f_ref, b_ref, c_ref):
        lf = _log_sigmoid(f_ref[...] + b_ref[...])
        i = lax.broadcasted_iota(jnp.int32, (s, LANES), 0)
        j = pl.program_id(0) * LANES + lax.broadcasted_iota(jnp.int32, (s, LANES), 1)
        tri = jnp.where(i <= j, 1.0, 0.0).astype(BF16)
        c_ref[...] = sum(_dot(p, tri, _NN) for p in _split3(lf))

    return pl.pallas_call(
        body, name="fox_gates", grid=(nb,),
        in_specs=[pl.BlockSpec((h, s), lambda j: (0, 0)), pl.BlockSpec((h, 1), lambda j: (0, 0))],
        out_specs=pl.BlockSpec((h, LANES), lambda j: (0, j)),
        out_shape=jax.ShapeDtypeStruct((h, s), F32),
        compiler_params=_params(("parallel",)),
    )(ft, bias.reshape(h, 1))


def _fox_gates_bwd(dc, ft, bias):
    h, s = ft.shape
    nb = s // LANES

    def body(dc_ref, f_ref, fb_ref, b_ref, df_ref, gb_ref):
        i = lax.broadcasted_iota(jnp.int32, (s, LANES), 0)
        j = pl.program_id(0) * LANES + lax.broadcasted_iota(jnp.int32, (s, LANES), 1)
        tri = jnp.where(i >= j, 1.0, 0.0).astype(BF16)
        dlf = sum(_dot(p, tri, _NN) for p in _split3(dc_ref[...]))
        df = dlf * jax.nn.sigmoid(-(fb_ref[...] + b_ref[...]))
        df_ref[...] = df
        part = jnp.sum(df, axis=-1, keepdims=True)

        @pl.when(pl.program_id(0) == 0)
        def _():
            gb_ref[...] = part

        @pl.when(pl.program_id(0) > 0)
        def _():
            gb_ref[...] += part

    full = pl.BlockSpec((h, s), lambda j: (0, 0))
    blk = pl.BlockSpec((h, LANES), lambda j: (0, j))
    one = pl.BlockSpec((h, 1), lambda j: (0, 0))
    df, gb = pl.pallas_call(
        body, name="fox_gates_bwd", grid=(nb,), in_specs=[full, full, blk, one], out_specs=[blk, one],
        out_shape=[jax.ShapeDtypeStruct((h, s), F32), jax.ShapeDtypeStruct((h, 1), F32)],
        compiler_params=_params(("arbitrary",)),
    )(dc, ft, ft, bias.reshape(h, 1))
    return df, gb.reshape(h)


def _delta_rep(do, o, tm=512):
    s = do.shape[0]
    tm = min(tm, s)

    def body(do_ref, o_ref, d_ref):
        pr = do_ref[...].astype(F32) * o_ref[...].astype(F32)
        lo = _lo_mask(pr.shape)
        d0 = jnp.sum(jnp.where(lo, pr, 0.0), axis=-1, keepdims=True)
        d1 = jnp.sum(jnp.where(lo, 0.0, pr), axis=-1, keepdims=True)
        d_ref[...] = jnp.where(lo, d0, d1)

    own = pl.BlockSpec((tm, LANES), lambda p, i: (i, p))
    return pl.pallas_call(
        body, name="delta_rep", grid=(N_PAIR, s // tm), in_specs=[own, own],
        out_specs=pl.BlockSpec((None, tm, LANES), lambda p, i: (p, i, 0)),
        out_shape=jax.ShapeDtypeStruct((N_PAIR, s, LANES), F32),
        compiler_params=_params(("parallel", "parallel")),
    )(do, o)


def _tile(s, t):
    t = min(t, s)
    assert s % t == 0
    return t


def _fox_fwd(qn, kn, proj, ccol, crow, t=256):
    s = qn.shape[0]
    t = _tile(s, t)
    n = s // t
    qtab, ktab = _tri_tables(n, "k_asc")
    voff = FOX_OFF // LANES + 2 * N_PAIR

    def body(qt_ref, kt_ref, q_ref, k_ref, v_ref, cc_ref, cr_ref, o_ref, lse_ref, m_ref, l_ref, acc_ref):
        i = pl.program_id(1)
        qi, kj = qt_ref[i], kt_ref[i]

        @pl.when(kj == 0)
        def _():
            m_ref[...] = jnp.full(m_ref.shape, NEG_INF, F32)
            l_ref[...] = jnp.zeros(l_ref.shape, F32)
            acc_ref[...] = jnp.zeros(acc_ref.shape, F32)

        q, k, v = q_ref[...], k_ref[...], v_ref[...]
        row, col = _positions(qi, kj, t)
        causal = col <= row
        m_old = m_ref[...]
        mn, rs, pv = [], [], []
        for h in range(2):
            sc = _dot(_head(q, h), k, _NT) + _pair_cols(cc_ref[...], h) - cr_ref[h:h + 1, :]
            sc = jnp.where(causal, sc, NEG_INF)
            m_new = jnp.maximum(_pair_cols(m_old, h), jnp.max(sc, axis=-1, keepdims=True))
            p = jnp.exp(sc - m_new)
            mn.append(m_new)
            rs.append(jnp.sum(p, axis=-1, keepdims=True))
            pv.append(_dot(p.astype(BF16), _head(v, h), _NN))
        m_rep = _rep(mn[0], mn[1], m_old.shape)
        alpha = jnp.exp(m_old - m_rep)
        l_ref[...] = alpha * l_ref[...] + _rep(rs[0], rs[1], m_old.shape)
        acc_ref[...] = alpha * acc_ref[...] + pv[0] + pv[1]
        m_ref[...] = m_rep

        @pl.when(kj == qi)
        def _():
            o_ref[...] = acc_ref[...] / l_ref[...]
            lse_ref[...] = m_ref[...] + jnp.log(l_ref[...])

    grid_spec = pltpu.PrefetchScalarGridSpec(
        num_scalar_prefetch=2, grid=(N_PAIR, n * (n + 1) // 2),
        in_specs=[pl.BlockSpec((t, LANES), lambda p, i, qt, kt: (qt[i], p)),
                  pl.BlockSpec((t, LANES), lambda p, i, qt, kt: (kt[i], p)),
                  pl.BlockSpec((t, LANES), lambda p, i, qt, kt: (kt[i], voff + p)),
                  pl.BlockSpec((None, t, LANES), lambda p, i, qt, kt: (p, qt[i], 0)),
                  pl.BlockSpec((None, 2, t), lambda p, i, qt, kt: (p, 0, kt[i]))],
        out_specs=[pl.BlockSpec((t, LANES), lambda p, i, qt, kt: (qt[i], p)),
                   pl.BlockSpec((None, t, LANES), lambda p, i, qt, kt: (p, qt[i], 0))],
        scratch_shapes=[pltpu.VMEM((t, LANES), F32)] * 3)
    return pl.pallas_call(
        body, name="fox_fwd", grid_spec=grid_spec,
        out_shape=[jax.ShapeDtypeStruct((s, WIDTH), F32), jax.ShapeDtypeStruct((N_PAIR, s, LANES), F32)],
        compiler_params=_params(("parallel", "arbitrary")),
    )(qtab, ktab, qn, kn, proj, ccol, crow)


def _fox_bwd(qn, kn, proj, do, lse, delta, ccol, crow, t=256):
    s = qn.shape[0]
    t = _tile(s, t)
    n = s // t
    qtab, ktab = _tri_tables(n, "k_outer")
    voff = FOX_OFF // LANES + 2 * N_PAIR

    def body(qt_ref, kt_ref, q_ref, k_ref, v_ref, do_ref, lse_ref, dl_ref, cc_ref, cr_ref,
             dq_ref, dk_ref, dv_ref, dc_ref, dcq_ref, dka_ref, dva_ref, dca_ref):
        i = pl.program_id(1)
        qi, kj = qt_ref[i], kt_ref[i]

        @pl.when(i == 0)
        def _():
            dq_ref[...] = jnp.zeros(dq_ref.shape, F32)
            dcq_ref[...] = jnp.zeros(dcq_ref.shape, F32)

        @pl.when(qi == kj)
        def _():
            dka_ref[...] = jnp.zeros(dka_ref.shape, F32)
            dva_ref[...] = jnp.zeros(dva_ref.shape, F32)
            dca_ref[...] = jnp.zeros(dca_ref.shape, F32)

        q, k, v, g = q_ref[...], k_ref[...], v_ref[...], do_ref[...]
        row, col = _positions(qi, kj, t)
        causal = col <= row
        dq = jnp.zeros((t, LANES), F32)
        dk = jnp.zeros((t, LANES), F32)
        dv = jnp.zeros((t, LANES), F32)
        rowsum = []
        for h in range(2):
            qh, gh = _head(q, h), _head(g, h)
            sc = _dot(qh, k, _NT) + _pair_cols(cc_ref[...], h) - cr_ref[h:h + 1, :]
            p = jnp.where(causal, jnp.exp(sc - _pair_cols(lse_ref[...], h)), 0.0)
            dp = _dot(gh, v, _NT)
            ds = p * (dp - _pair_cols(dl_ref[...], h))
            dsb = ds.astype(BF16)
            dv = dv + _dot(p.astype(BF16), gh, _TN)
            dk = dk + _dot(dsb, qh, _TN)
            dq = dq + _dot(dsb, _head(k, h), _NN)
            dca_ref[h:h + 1, :] -= jnp.sum(ds, axis=0, keepdims=True)
            rowsum.append(jnp.sum(ds, axis=-1, keepdims=True))
        dka_ref[...] += dk
        dva_ref[...] += dv
        rows = pl.ds(pl.multiple_of(qi * t, t), t)
        dq_ref[rows, :] += dq
        dcq_ref[rows, :] += _rep(rowsum[0], rowsum[1], (t, LANES))

        @pl.when(qi == n - 1)
        def _():
            dk_ref[...] = dka_ref[...]
            dv_ref[...] = dva_ref[...].astype(BF16)
            dc_ref[...] = dca_ref[...]

    qblk = lambda p, i, qt, kt: (qt[i], p)
    kblk = lambda p, i, qt, kt: (kt[i], p)
    qrep = pl.BlockSpec((None, t, LANES), lambda p, i, qt, kt: (p, qt[i], 0))
    crs = pl.BlockSpec((None, 2, t), lambda p, i, qt, kt: (p, 0, kt[i]))
    grid_spec = pltpu.PrefetchScalarGridSpec(
        num_scalar_prefetch=2, grid=(N_PAIR, n * (n + 1) // 2),
        in_specs=[pl.BlockSpec((t, LANES), qblk), pl.BlockSpec((t, LANES), kblk),
                  pl.BlockSpec((t, LANES), lambda p, i, qt, kt: (kt[i], voff + p)),
                  pl.BlockSpec((t, LANES), qblk), qrep, qrep, qrep, crs],
        out_specs=[pl.BlockSpec((s, LANES), lambda p, i, qt, kt: (0, p)),
                   pl.BlockSpec((t, LANES), kblk), pl.BlockSpec((t, LANES), kblk), crs,
                   pl.BlockSpec((None, s, LANES), lambda p, i, qt, kt: (p, 0, 0))],
        scratch_shapes=[pltpu.VMEM((t, LANES), F32), pltpu.VMEM((t, LANES), F32), pltpu.VMEM((2, t), F32)])
    return pl.pallas_call(
        body, name="fox_bwd", grid_spec=grid_spec,
        out_shape=[jax.ShapeDtypeStruct((s, WIDTH), F32), jax.ShapeDtypeStruct((s, WIDTH), F32),
                   jax.ShapeDtypeStruct((s, WIDTH), BF16), jax.ShapeDtypeStruct((N_PAIR, 2, s), F32),
                   jax.ShapeDtypeStruct((N_PAIR, s, LANES), F32)],
        compiler_params=_params(("parallel", "arbitrary")),
    )(qtab, ktab, qn, kn, proj, do, lse, delta, ccol, crow)


def _sb_tile(qh, k, strict, t, r_col):
    z = _dot(qh, k, _NT)
    lg = jnp.where(strict, -(jnp.maximum(z, 0.0) + jnp.log(1.0 + jnp.exp(-jnp.abs(z)))), 0.0)
    jj = lax.broadcasted_iota(jnp.int32, (t, t), 0)
    ss = lax.broadcasted_iota(jnp.int32, (t, t), 1)
    above = jnp.where(jj > ss, 1.0, 0.0).astype(BF16)
    suffix = sum(_dot(p, above, _NN) for p in _split2(lg)) + r_col
    a = jnp.where(strict, jnp.exp(lg + z + suffix), 0.0)
    return z, lg, a


def _sb_fwd(proj, t=256):
    s = proj.shape[0]
    t = _tile(s, t)
    n = s // t
    qtab, ktab = _tri_tables(n, "k_desc")
    off = SB_OFF // LANES

    def body(qt_ref, kt_ref, q_ref, k_ref, v_ref, o_ref, acc_ref, r_ref):
        i = pl.program_id(1)
        qi, kj = qt_ref[i], kt_ref[i]

        @pl.when(kj == qi)
        def _():
            acc_ref[...] = jnp.zeros(acc_ref.shape, F32)
            r_ref[...] = jnp.zeros(r_ref.shape, F32)

        q = q_ref[...] * QK_SCALE
        k, v = k_ref[...], v_ref[...]
        row, col = _positions(qi, kj, t)
        strict = col < row
        acc = acc_ref[...]
        for h in range(2):
            _, lg, a = _sb_tile(_head(q, h), k, strict, t, r_ref[h])
            acc = acc + _dot(a.astype(BF16), _head(v, h), _NN)
            r_ref[h] += jnp.sum(lg, axis=-1, keepdims=True)
        acc_ref[...] = acc

        @pl.when(kj == 0)
        def _():
            o_ref[...] = acc_ref[...]

    grid_spec = pltpu.PrefetchScalarGridSpec(
        num_scalar_prefetch=2, grid=(N_PAIR, n * (n + 1) // 2),
        in_specs=[pl.BlockSpec((t, LANES), lambda p, i, qt, kt: (qt[i], off + p)),
                  pl.BlockSpec((t, LANES), lambda p, i, qt, kt: (kt[i], off + N_PAIR + p)),
                  pl.BlockSpec((t, LANES), lambda p, i, qt, kt: (kt[i], off + 2 * N_PAIR + p))],
        out_specs=pl.BlockSpec((t, LANES), lambda p, i, qt, kt: (qt[i], p)),
        scratch_shapes=[pltpu.VMEM((t, LANES), F32), pltpu.VMEM((2, t, 1), F32)])
    return pl.pallas_call(
        body, name="sb_fwd", grid_spec=grid_spec, out_shape=jax.ShapeDtypeStruct((s, WIDTH), F32),
        compiler_params=_params(("parallel", "arbitrary")),
    )(qtab, ktab, proj, proj, proj)


def _sb_bwd(proj, do, delta, t=256):
    s = proj.shape[0]
    t = _tile(s, t)
    n = s // t
    ntri = n * (n + 1) // 2
    qtab, ktab = _tri_tables(n, "k_desc")
    off = SB_OFF // LANES

    def body(qt_ref, kt_ref, q_ref, k_ref, v_ref, do_ref, dl_ref, dq_ref, dk_ref, dv_ref,
             dqa_ref, dka_ref, dva_ref, r_ref, rd_ref):
        i = pl.program_id(1)
        qi, kj = qt_ref[i], kt_ref[i]

        @pl.when(i == 0)
        def _():
            dka_ref[...] = jnp.zeros(dka_ref.shape, F32)
            dva_ref[...] = jnp.zeros(dva_ref.shape, F32)

        @pl.when(kj == qi)
        def _():
            dqa_ref[...] = jnp.zeros(dqa_ref.shape, F32)
            r_ref[...] = jnp.zeros(r_ref.shape, F32)
            rd_ref[...] = jnp.zeros(rd_ref.shape, F32)

        q = q_ref[...] * QK_SCALE
        k, v, g = k_ref[...], v_ref[...], do_ref[...]
        row, col = _positions(qi, kj, t)
        strict = col < row
        ss = lax.broadcasted_iota(jnp.int32, (t, t), 0)
        jj = lax.broadcasted_iota(jnp.int32, (t, t), 1)
        at_or_after = jnp.where(ss >= jj, 1.0, 0.0).astype(BF16)
        dq = jnp.zeros((t, LANES), F32)
        dk = jnp.zeros((t, LANES), F32)
        dv = jnp.zeros((t, LANES), F32)
        for h in range(2):
            qh, gh = _head(q, h), _head(g, h)
            z, lg, a = _sb_tile(qh, k, strict, t, r_ref[h])
            da = _dot(gh, v, _NT)
            ab = a.astype(BF16)
            d_a = da * ab.astype(F32)
            incl = sum(_dot(p, at_or_after, _NN) for p in _split2(d_a))
            d_l = _pair_cols(dl_ref[...], h) - rd_ref[h] - incl
            sig = jnp.exp(lg + z)
            dz = jnp.where(strict, d_a * (1.0 - sig) - d_l * sig, 0.0).astype(BF16)
            dq = dq + _dot(dz, _head(k, h), _NN)
            dk = dk + _dot(dz, qh, _TN)
            dv = dv + _dot(ab, gh, _TN)
            r_ref[h] += jnp.sum(lg, axis=-1, keepdims=True)
            rd_ref[h] += jnp.sum(d_a, axis=-1, keepdims=True)
        dqa_ref[...] += dq
        rows = pl.ds(pl.multiple_of(kj * t, t), t)
        dka_ref[rows, :] += dk
        dva_ref[rows, :] += dv

        @pl.when(kj == 0)
        def _():
            dq_ref[...] = (dqa_ref[...] * QK_SCALE).astype(BF16)

        @pl.when(i == ntri - 1)
        def _():
            dk_ref[...] = dka_ref[...].astype(BF16)
            dv_ref[...] = dva_ref[...].astype(BF16)

    qblk = lambda p, i, qt, kt: (qt[i], p)
    whole = pl.BlockSpec((s, LANES), lambda p, i, qt, kt: (0, p))
    grid_spec = pltpu.PrefetchScalarGridSpec(
        num_scalar_prefetch=2, grid=(N_PAIR, ntri),
        in_specs=[pl.BlockSpec((t, LANES), lambda p, i, qt, kt: (qt[i], off + p)),
                  pl.BlockSpec((t, LANES), lambda p, i, qt, kt: (kt[i], off + N_PAIR + p)),
                  pl.BlockSpec((t, LANES), lambda p, i, qt, kt: (kt[i], off + 2 * N_PAIR + p)),
                  pl.BlockSpec((t, LANES), qblk),
                  pl.BlockSpec((None, t, LANES), lambda p, i, qt, kt: (p, qt[i], 0))],
        out_specs=[pl.BlockSpec((t, LANES), qblk), whole, whole],
        scratch_shapes=[pltpu.VMEM((t, LANES), F32), pltpu.VMEM((s, LANES), F32), pltpu.VMEM((s, LANES), F32),
                        pltpu.VMEM((2, t, 1), F32), pltpu.VMEM((2, t, 1), F32)])
    act = jax.ShapeDtypeStruct((s, WIDTH), BF16)
    return pl.pallas_call(
        body, name="sb_bwd", grid_spec=grid_spec, out_shape=[act, act, act],
        compiler_params=_params(("parallel", "arbitrary")),
    )(qtab, ktab, proj, proj, proj, do, delta)


def _loss_head(y, target, tm=256):
    s, d = y.shape
    tm = min(tm, s)

    def body(y_ref, t_ref, l_ref, dy_ref):
        e = y_ref[...] - t_ref[...]
        dy_ref[...] = e / d
        part = jnp.sum(jnp.sum(e * e, axis=0, keepdims=True), axis=1, keepdims=True)

        @pl.when(pl.program_id(0) == 0)
        def _():
            l_ref[...] = jnp.broadcast_to(part, l_ref.shape)

        @pl.when(pl.program_id(0) > 0)
        def _():
            l_ref[...] += jnp.broadcast_to(part, l_ref.shape)

    row = pl.BlockSpec((tm, d), lambda i: (i, 0))
    return pl.pallas_call(
        body, name="loss_head", grid=(s // tm,), in_specs=[row, row],
        out_specs=[pl.BlockSpec((1, LANES), lambda i: (0, 0)), row],
        out_shape=[jax.ShapeDtypeStruct((1, LANES), F32), jax.ShapeDtypeStruct((s, d), F32)],
        compiler_params=_params(("arbitrary",)),
    )(y, target)


def _adamw(w, g, m, v, name):
    shape = w.shape
    cols = shape[-1]
    rows = int(np.prod(shape[:-1]))
    tm = rows
    for cand in (512, 256, 128, 64, 32, 16, 8):
        if rows % cand == 0 and rows > cand and cand * cols * 4 <= ADAMW_BLOCK_BYTES:
            tm = cand
            break

    def body(w_ref, g_ref, m_ref, v_ref, d_ref, mo_ref, vo_ref):
        gr = g_ref[...]
        mn = ADAM_B1 * m_ref[...] + (1.0 - ADAM_B1) * gr
        vn = ADAM_B2 * v_ref[...] + (1.0 - ADAM_B2) * (gr * gr)
        m_hat = mn / (1.0 - ADAM_B1 ** ADAM_STEP)
        v_hat = vn / (1.0 - ADAM_B2 ** ADAM_STEP)
        d_ref[...] = -ADAM_LR * (m_hat / (jnp.sqrt(v_hat) + ADAM_EPS) + ADAM_WD * w_ref[...])
        mo_ref[...] = mn
        vo_ref[...] = vn

    blk = pl.BlockSpec((tm, cols), lambda i: (i, 0))
    out = jax.ShapeDtypeStruct((rows, cols), F32)
    r2 = lambda a: a.reshape(rows, cols)
    outs = pl.pallas_call(
        body, name=name, grid=(rows // tm,), in_specs=[blk] * 4, out_specs=[blk] * 3, out_shape=[out] * 3,
        compiler_params=_params(("parallel",)),
    )(r2(w), r2(g), r2(m), r2(v))
    return tuple(o.reshape(shape) for o in outs)


def _split_w_in(w_in):
    c0, c1, c2, c3 = 3 * WIDTH, 6 * WIDTH, 6 * WIDTH + N_HEADS, 9 * WIDTH + N_HEADS
    wa = jnp.concatenate([w_in[:, c3:], w_in[:, :c0], w_in[:, c0:c1], w_in[:, c2:c3]], axis=1)
    wf = jnp.pad(w_in[:, c1:c2], ((0, 0), (0, LANES - N_HEADS)))
    return wa, wf


def _join_w_in(ga, gf):
    g, c, f, s = 3 * D_MODEL, CONV_OFF, FOX_OFF, SB_OFF
    return jnp.concatenate([ga[:, c:f], ga[:, f:s], gf[:, :N_HEADS], ga[:, s:], ga[:, :g]], axis=1)


def _layer_fwd(x, p):
    s = x.shape[0]
    proj, hn1 = _norm_mm(x, p["norm1_g"], p["wa"], BF16, name="in_proj")
    fraw, _ = _norm_mm(x, p["norm1_g"], p["wf"], F32, name="in_proj_f")
    ft = fraw[:, :N_HEADS].T
    crow8 = _fox_gates(ft, p["fox_f_bias"])
    crow = crow8.reshape(N_PAIR, 2, s)
    ccol = jnp.repeat(crow.transpose(0, 2, 1), HEAD_DIM, axis=2)
    oc = _conv_mix_fwd(proj, p["conv_w"])
    qn, kn = _fox_prep(proj, p["fox_q_norm_g"], p["fox_k_norm_g"])
    of, lse = _fox_fwd(qn, kn, proj, ccol, crow)
    osb = _sb_fwd(proj)
    xm, merged, yc, yf, ys = _merge_fwd(x, oc, of, osb, p["w_proj_conv"], p["w_proj_fox"], p["w_proj_sb"], proj,
                                        p["gate_bias"], p["w_out"])
    up, hn2 = _norm_mm(xm, p["norm2_g"], p["w_up"], BF16, name="up_proj")
    hh = _glu_fwd(up, p["ffn_conv_w"], p["ffn_conv_b"])
    xo = _mm(hh, p["w_down"], "nn", F32, res=xm, name="down_proj")
    saved = dict(x=x, hn1=hn1, proj=proj, ft=ft, crow=crow, ccol=ccol, oc=oc, qn=qn, kn=kn, of=of, lse=lse, osb=osb,
                 merged=merged, yc=yc, yf=yf, ys=ys, xm=xm, hn2=hn2, up=up, hh=hh)
    return xo, saved


def _layer_bwd(dx, p, a):
    s = dx.shape[0]
    g = {}
    dhh = _mm(dx, p["w_down"], "nt", BF16, name="d_down_in")
    g["w_down"] = _mm(a["hh"], dx, "tn", F32, name="g_w_down")
    dug, duv, g["ffn_conv_w"], g["ffn_conv_b"] = _glu_bwd(dhh, a["up"], p["ffn_conv_w"], p["ffn_conv_b"])
    dup = jnp.concatenate([dug, duv], axis=1)
    g["w_up"] = _mm(a["hn2"], dup, "tn", F32, name="g_w_up")
    dhn2 = _mm(dup, p["w_up"], "nt", F32, name="d_up_in")
    dx, g["norm2_g"] = _norm_bwd(dhn2, a["xm"], p["norm2_g"], dx, name="norm2_bwd")
    dm = _mm(dx, p["w_out"], "nt", BF16, name="d_out_in")
    g["w_out"] = _mm(a["merged"], dx, "tn", F32, name="g_w_out")
    dyc, dyf, dys, dgl, g["gate_bias"] = _gate_bwd(dm, a["yc"], a["yf"], a["ys"], a["proj"], p["gate_bias"])
    doc = _mm(dyc, p["w_proj_conv"], "nt", BF16, name="d_pc_in")
    dof = _mm(dyf, p["w_proj_fox"], "nt", BF16, name="d_pf_in")
    dos = _mm(dys, p["w_proj_sb"], "nt", BF16, name="d_ps_in")
    g["w_proj_conv"] = _mm(a["oc"], dyc, "tn", F32, name="g_w_pc")
    g["w_proj_fox"] = _mm(a["of"], dyf, "tn", F32, name="g_w_pf")
    g["w_proj_sb"] = _mm(a["osb"], dys, "tn", F32, name="g_w_ps")
    dcb, dcc, dch, g["conv_w"] = _conv_mix_bwd(doc, a["proj"], p["conv_w"])
    delta_f = _delta_rep(dof, a["of"])
    dqs, dkn, dfv, dcrow, dcq = _fox_bwd(a["qn"], a["kn"], a["proj"], dof, a["lse"], delta_f, a["ccol"], a["crow"])
    dc = dcrow.reshape(N_HEADS, s) + dcq[:, :, ::HEAD_DIM].transpose(0, 2, 1).reshape(N_HEADS, s)
    dfq, dfk, g["fox_q_norm_g"], g["fox_k_norm_g"] = _fox_post(dqs, dkn, a["proj"], p["fox_q_norm_g"], p["fox_k_norm_g"])
    dft, g["fox_f_bias"] = _fox_gates_bwd(dc, a["ft"], p["fox_f_bias"])
    delta_s = _delta_rep(dos, a["osb"])
    dsq, dsk, dsv = _sb_bwd(a["proj"], dos, delta_s)
    dproj = jnp.concatenate([dgl, dcb, dcc, dch, dfq, dfk, dfv, dsq, dsk, dsv], axis=1)
    dfp = jnp.pad(dft.T, ((0, 0), (0, LANES - N_HEADS))).astype(BF16)
    ga = _mm(a["hn1"], dproj, "tn", F32, name="g_w_in")
    gf = _mm(a["hn1"], dfp, "tn", F32, name="g_w_in_f")
    g["w_in"] = _join_w_in(ga, gf)
    dhn1 = _mm(dfp, p["wf"], "nt", F32, name="d_in_f")
    dhn1 = _mm(dproj, p["wa"], "nt", F32, res=dhn1, tk=1536, name="d_in")
    dx, g["norm1_g"] = _norm_bwd(dhn1, a["x"], p["norm1_g"], dx, name="norm1_bwd")
    return dx, g


MATMUL_WEIGHTS = ("w_in", "w_proj_conv", "w_proj_fox", "w_proj_sb", "w_out", "w_up", "w_down")
WEIGHTS = ("norm1_g", "w_in", "fox_f_bias", "gate_bias", "conv_w", "fox_q_norm_g", "fox_k_norm_g", "w_proj_conv",
           "w_proj_fox", "w_proj_sb", "w_out", "norm2_g", "w_up", "ffn_conv_w", "ffn_conv_b", "w_down")


def _local_step(x, target, layers, reduce_layer=lambda l, g: g):
    layers = [dict(p) for p in layers]
    for p in layers:
        p["wa"], p["wf"] = _split_w_in(p.pop("w_in"))
    saved = []
    for p in layers:
        x, a = _layer_fwd(x, p)
        saved.append(a)
    sq, dx = _loss_head(x, target)
    grads = [None] * len(layers)
    for l in reversed(range(len(layers))):
        dx, g = _layer_bwd(dx, layers[l], saved[l])
        grads[l] = reduce_layer(l, g)
    return sq, dx, grads


ANY = pl.BlockSpec(memory_space=pl.ANY)


def _place():
    x, y, c = lax.axis_index("x"), lax.axis_index("y"), lax.axis_index("c")
    chips = [(1 - x, y), (x, 1 - y), (1 - x, 1 - y)]
    return x, y, c, chips


def _all_gather_chips(shards, name):
    n = len(shards)

    def body(*refs):
        x_refs, out_refs = refs[:n], refs[n:2 * n]
        send_sems, recv_sems, local_sems = refs[2 * n:]
        x, y, c, chips = _place()
        me = 2 * x + y
        sibling = (x, y, 1 - c)

        def copy(k, t, chip_index, which_half, to, from_input=False):
            half = shards[t].shape[0] // 2
            rows = pl.ds(which_half * half, half)
            dst = out_refs[t].at[chip_index, rows, :]
            return pltpu.make_async_remote_copy(
                src_ref=x_refs[t].at[rows, :] if from_input else dst, dst_ref=dst,
                send_sem=send_sems.at[k, t], recv_sem=recv_sems.at[k, t], device_id=to, device_id_type=MESH)

        mine = [pltpu.make_async_copy(x_refs[t], out_refs[t].at[me], local_sems.at[t]) for t in range(n)]
        for cp in mine:
            cp.start()
        first = [copy(p, t, me, c, (*chip, c), from_input=True) for t in range(n) for p, chip in enumerate(chips)]
        for cp in first:
            cp.start()
        passed = []
        for t in range(n):
            for p, chip in enumerate(chips):
                copy(p, t, 2 * chip[0] + chip[1], c, (x, y, c)).wait_recv()
                fwd = copy(3 + p, t, 2 * chip[0] + chip[1], c, sibling)
                fwd.start()
                passed.append(fwd)
        for t in range(n):
            for p, chip in enumerate(chips):
                copy(3 + p, t, 2 * chip[0] + chip[1], 1 - c, (x, y, c)).wait_recv()
        for cp in first + passed:
            cp.wait_send()
        for cp in mine:
            cp.wait()

    return pl.pallas_call(
        body, name=name, in_specs=[ANY] * n, out_specs=[ANY] * n,
        out_shape=[jax.ShapeDtypeStruct((4, *s.shape), s.dtype) for s in shards],
        scratch_shapes=[pltpu.SemaphoreType.DMA((6, n)), pltpu.SemaphoreType.DMA((6, n)), pltpu.SemaphoreType.DMA((n,))],
    )(*shards)


def _pair_exchange(gs):
    n = len(gs)

    def body(*refs):
        g_refs, recv_refs = refs[:n], refs[n:2 * n]
        send_sems, recv_sems = refs[2 * n:]
        x, y, c, _ = _place()
        cps = []
        for t in range(n):
            half = gs[t].shape[1] // 2
            cps.append(pltpu.make_async_remote_copy(
                src_ref=g_refs[t].at[:, pl.ds((1 - c) * half, half), :], dst_ref=recv_refs[t],
                send_sem=send_sems.at[t], recv_sem=recv_sems.at[t], device_id=(x, y, 1 - c), device_id_type=MESH))
        for cp in cps:
            cp.start()
        for cp in cps:
            cp.wait()

    return pl.pallas_call(
        body, name="rs_pair_exchange", in_specs=[ANY] * n, out_specs=[ANY] * n,
        out_shape=[jax.ShapeDtypeStruct((4, g.shape[1] // 2, g.shape[2]), g.dtype) for g in gs],
        scratch_shapes=[pltpu.SemaphoreType.DMA((n,)), pltpu.SemaphoreType.DMA((n,))],
    )(*gs)


def _pair_sum(g, recv, core, tr=256):
    n, r, cols = g.shape
    half = r // 2
    tr = _row_tile(half, tr)
    nb = half // tr

    def body(c_ref, g_ref, r_ref, o_ref):
        o_ref[...] = (g_ref[...] + r_ref[...]).astype(BF16)

    grid_spec = pltpu.PrefetchScalarGridSpec(
        num_scalar_prefetch=1, grid=(n, nb),
        in_specs=[pl.BlockSpec((None, tr, cols), lambda k, i, c: (k, c[0] * nb + i, 0)),
                  pl.BlockSpec((None, tr, cols), lambda k, i, c: (k, i, 0))],
        out_specs=pl.BlockSpec((None, tr, cols), lambda k, i, c: (k, i, 0)))
    return pl.pallas_call(
        body, name="rs_pair_sum", grid_spec=grid_spec, out_shape=jax.ShapeDtypeStruct((n, half, cols), BF16),
        compiler_params=_params(("parallel", "parallel")),
    )(core, g, recv)


def _row_tile(rows, pref):
    best = None
    for t in range(16, min(rows, pref) + 1, 16):
        if rows % t == 0:
            best = t
    assert best is not None, (rows, pref)
    return best


def _chip_exchange(s1s):
    n = len(s1s)

    def body(*refs):
        s_refs, recv_refs = refs[:n], refs[n:2 * n]
        send_sems, recv_sems = refs[2 * n:]
        x, y, c, chips = _place()
        cps = [pltpu.make_async_remote_copy(
            src_ref=s_refs[t].at[2 * chip[0] + chip[1]], dst_ref=recv_refs[t].at[p],
            send_sem=send_sems.at[p, t], recv_sem=recv_sems.at[p, t], device_id=(*chip, c), device_id_type=MESH)
            for t in range(n) for p, chip in enumerate(chips)]
        for cp in cps:
            cp.start()
        for cp in cps:
            cp.wait()

    return pl.pallas_call(
        body, name="rs_chip_exchange", in_specs=[ANY] * n, out_specs=[ANY] * n,
        out_shape=[jax.ShapeDtypeStruct((3, *s.shape[1:]), s.dtype) for s in s1s],
        scratch_shapes=[pltpu.SemaphoreType.DMA((3, n)), pltpu.SemaphoreType.DMA((3, n))],
    )(*s1s)


def _final_sum(g, recv_a, recv_b, core, chip, tr=256):
    n, r, cols = g.shape
    half = r // 2
    tr = _row_tile(half, tr)
    nb = half // tr

    def body(c_ref, k_ref, g_ref, a_ref, b0_ref, b1_ref, b2_ref, o_ref):
        total = g_ref[...] + a_ref[...]
        for b_ref in (b0_ref, b1_ref, b2_ref):
            total = total + b_ref[...].astype(F32)
        o_ref[...] = total

    rel = lambda p: pl.BlockSpec((None, tr, cols), lambda i, c, k, p=p: (p, i, 0))
    grid_spec = pltpu.PrefetchScalarGridSpec(
        num_scalar_prefetch=2, grid=(nb,),
        in_specs=[pl.BlockSpec((None, tr, cols), lambda i, c, k: (k[0], c[0] * nb + i, 0)),
                  pl.BlockSpec((None, tr, cols), lambda i, c, k: (k[0], i, 0)), rel(0), rel(1), rel(2)],
        out_specs=pl.BlockSpec((tr, cols), lambda i, c, k: (i, 0)))
    return pl.pallas_call(
        body, name="rs_final_sum", grid_spec=grid_spec, out_shape=jax.ShapeDtypeStruct((half, cols), F32),
        compiler_params=_params(("parallel",)),
    )(core, chip, g, recv_a, recv_b, recv_b, recv_b)


def _pair_join(fhs):
    n = len(fhs)

    def body(*refs):
        f_refs, out_refs = refs[:n], refs[n:2 * n]
        send_sems, recv_sems, local_sems = refs[2 * n:]
        x, y, c, _ = _place()

        def copy(t, which_half):
            h = fhs[t].shape[0]
            return pltpu.make_async_remote_copy(
                src_ref=f_refs[t], dst_ref=out_refs[t].at[pl.ds(which_half * h, h), :], send_sem=send_sems.at[t],
                recv_sem=recv_sems.at[t], device_id=(x, y, 1 - c), device_id_type=MESH)

        local = [pltpu.make_async_copy(f_refs[t], out_refs[t].at[pl.ds(c * fhs[t].shape[0], fhs[t].shape[0]), :],
                                       local_sems.at[t]) for t in range(n)]
        sends = [copy(t, c) for t in range(n)]
        for cp in local + sends:
            cp.start()
        for t in range(n):
            sends[t].wait_send()
            copy(t, 1 - c).wait_recv()
        for cp in local:
            cp.wait()

    return pl.pallas_call(
        body, name="rs_pair_join", in_specs=[ANY] * n, out_specs=[ANY] * n,
        out_shape=[jax.ShapeDtypeStruct((2 * f.shape[0], f.shape[1]), f.dtype) for f in fhs],
        scratch_shapes=[pltpu.SemaphoreType.DMA((n,)), pltpu.SemaphoreType.DMA((n,)), pltpu.SemaphoreType.DMA((n,))],
    )(*fhs)


def _reduce_scatter_chips(gs):
    core = lax.axis_index("c").astype(jnp.int32).reshape(1)
    chip = (2 * lax.axis_index("x") + lax.axis_index("y")).astype(jnp.int32).reshape(1)
    recv_a = _pair_exchange(gs)
    s1 = [_pair_sum(g, ra, core) for g, ra in zip(gs, recv_a)]
    recv_b = _chip_exchange(s1)
    return _pair_join([_final_sum(g, ra, rb, core, chip) for g, ra, rb in zip(gs, recv_a, recv_b)])


def _all_reduce_small(v):
    r, cols = v.shape

    def body(v_ref, out_ref, buf_ref, send_sems, recv_sems):
        x, y, c, _ = _place()
        flip = lambda a, bit: 1 - a if bit else a
        buf_ref[4 * x + 2 * y + c] = v_ref[...]
        cps = []
        for rel in range(1, 8):
            peer = (flip(x, rel & 4), flip(y, rel & 2), flip(c, rel & 1))
            cps.append(pltpu.make_async_remote_copy(
                src_ref=v_ref, dst_ref=buf_ref.at[4 * x + 2 * y + c], send_sem=send_sems.at[rel - 1], recv_sem=recv_sems.at[rel - 1],
                device_id=peer, device_id_type=MESH))
        for cp in cps:
            cp.start()
        for cp in cps:
            cp.wait()
        total = buf_ref[0]
        for d in range(1, 8):
            total = total + buf_ref[d]
        out_ref[...] = total

    vm = pl.BlockSpec(memory_space=pltpu.VMEM)
    return pl.pallas_call(
        body, name="all_reduce_small", in_specs=[vm], out_specs=vm, out_shape=jax.ShapeDtypeStruct((r, cols), F32),
        scratch_shapes=[pltpu.VMEM((8, r, cols), F32), pltpu.SemaphoreType.DMA((7,)), pltpu.SemaphoreType.DMA((7,))],
    )(v)


SHARD_AXIS = {"w_in": 1, "conv_w": 1, "w_proj_conv": 1, "w_proj_fox": 1, "w_proj_sb": 1, "w_out": 0, "w_up": 1,
              "ffn_conv_w": 1, "w_down": 0}
SMALL_SHARDED = ("conv_w", "ffn_conv_w")
BIG = tuple(k for k in SHARD_AXIS if k not in SMALL_SHARDED)
REPLICATED = tuple(k for k in WEIGHTS if k not in SHARD_AXIS)
SMALL = REPLICATED + SMALL_SHARDED


def _pack_small(parts, row_align):
    flat = jnp.concatenate([p.reshape(-1) for p in parts])
    rows = -(-flat.shape[0] // (PACK_COLS * row_align)) * row_align
    return jnp.pad(flat, (0, rows * PACK_COLS - flat.shape[0])).reshape(rows, PACK_COLS)


def _unpack_small(packed, shapes):
    flat = packed.reshape(-1)
    out, off = [], 0
    for shape in shapes:
        size = int(np.prod(shape))
        out.append(flat[off:off + size].reshape(shape))
        off += size
    return out


def _to_full(stacked, axis):
    _, r, c = stacked.shape
    if axis == 0:
        return stacked.reshape(4 * r, c)
    return jnp.moveaxis(stacked, 0, 1).reshape(r, 4 * c)


def _to_chips(full, axis):
    a, b = full.shape
    if axis == 0:
        return full.reshape(4, a // 4, b)
    return jnp.moveaxis(full.reshape(a, 4, b // 4), 1, 0)


def kernel(x, norm1_g, w_in, fox_f_bias, gate_bias, conv_w, fox_q_norm_g, fox_k_norm_g, w_proj_conv, w_proj_fox, w_proj_sb, w_out, norm2_g, w_up, ffn_conv_w, ffn_conv_b, w_down, loss_target, m_norm1_g, m_w_in, m_fox_f_bias, m_gate_bias, m_conv_w, m_fox_q_norm_g, m_fox_k_norm_g, m_w_proj_conv, m_w_proj_fox, m_w_proj_sb, m_w_out, m_norm2_g, m_w_up, m_ffn_conv_w, m_ffn_conv_b, m_w_down, v_norm1_g, v_w_in, v_fox_f_bias, v_gate_bias, v_conv_w, v_fox_q_norm_g, v_fox_k_norm_g, v_w_proj_conv, v_w_proj_fox, v_w_proj_sb, v_w_out, v_norm2_g, v_w_up, v_ffn_conv_w, v_ffn_conv_b, v_w_down):
    given = dict(x=x, norm1_g=norm1_g, w_in=w_in, fox_f_bias=fox_f_bias, gate_bias=gate_bias, conv_w=conv_w, fox_q_norm_g=fox_q_norm_g, fox_k_norm_g=fox_k_norm_g, w_proj_conv=w_proj_conv, w_proj_fox=w_proj_fox, w_proj_sb=w_proj_sb, w_out=w_out, norm2_g=norm2_g, w_up=w_up, ffn_conv_w=ffn_conv_w, ffn_conv_b=ffn_conv_b, w_down=w_down, loss_target=loss_target, m_norm1_g=m_norm1_g, m_w_in=m_w_in, m_fox_f_bias=m_fox_f_bias, m_gate_bias=m_gate_bias, m_conv_w=m_conv_w, m_fox_q_norm_g=m_fox_q_norm_g, m_fox_k_norm_g=m_fox_k_norm_g, m_w_proj_conv=m_w_proj_conv, m_w_proj_fox=m_w_proj_fox, m_w_proj_sb=m_w_proj_sb, m_w_out=m_w_out, m_norm2_g=m_norm2_g, m_w_up=m_w_up, m_ffn_conv_w=m_ffn_conv_w, m_ffn_conv_b=m_ffn_conv_b, m_w_down=m_w_down, v_norm1_g=v_norm1_g, v_w_in=v_w_in, v_fox_f_bias=v_fox_f_bias, v_gate_bias=v_gate_bias, v_conv_w=v_conv_w, v_fox_q_norm_g=v_fox_q_norm_g, v_fox_k_norm_g=v_fox_k_norm_g, v_w_proj_conv=v_w_proj_conv, v_w_proj_fox=v_w_proj_fox, v_w_proj_sb=v_w_proj_sb, v_w_out=v_w_out, v_norm2_g=v_norm2_g, v_w_up=v_w_up, v_ffn_conv_w=v_ffn_conv_w, v_ffn_conv_b=v_ffn_conv_b, v_w_down=v_w_down)
    depth = given["norm1_g"].shape[0]
    chip = 2 * lax.axis_index("x") + lax.axis_index("y")

    layers = []
    for l in range(depth):
        got = _all_gather_chips([given[k][l].astype(BF16) for k in BIG], "gather_weights")
        p = {k: given[k][l] for k in REPLICATED}
        p.update({k: _to_full(stacked, SHARD_AXIS[k]) for k, stacked in zip(BIG, got)})
        layers.append(p)
    conv_shapes = [given[k].shape for k in SMALL_SHARDED]
    (got_conv,) = _all_gather_chips([_pack_small([given[k] for k in SMALL_SHARDED], 16)], "gather_conv_weights")
    by_chip = [_unpack_small(got_conv[j], conv_shapes) for j in range(4)]
    for i, k in enumerate(SMALL_SHARDED):
        whole = jnp.concatenate([by_chip[j][i] for j in range(4)], axis=-1)
        for l in range(depth):
            layers[l][k] = whole[l]

    def reduce_layer(l, g):
        reduced = _reduce_scatter_chips([_to_chips(g[k], SHARD_AXIS[k]) for k in BIG])
        out = {k: g[k] for k in SMALL}
        out.update(zip(BIG, reduced))
        return out

    sq, dx, grads = _local_step(given["x"][0], given["loss_target"][0], layers, reduce_layer)
    loss = lax.psum(0.5 * sq[0, 0] / D_MODEL, ("x", "y", "c"))

    gsum = {k: jnp.stack([g[k] for g in grads]) for k in BIG}
    small_shapes = [(depth, *grads[0][k].shape) for k in SMALL]
    summed = _all_reduce_small(_pack_small([jnp.stack([g[k] for g in grads]) for k in SMALL], 8))
    for k, total in zip(SMALL, _unpack_small(summed, small_shapes)):
        if k in SMALL_SHARDED:
            total = lax.dynamic_index_in_dim(total.reshape(*total.shape[:-1], 4, total.shape[-1] // 4), chip, axis=2, keepdims=False)
        gsum[k] = total

    deltas, new_m, new_v = {}, {}, {}
    for k in WEIGHTS:
        deltas[k], new_m[k], new_v[k] = _adamw(given[k], gsum[k], given["m_" + k], given["v_" + k], "adamw_" + k)
    return (loss, dx[None], *[gsum[k] for k in WEIGHTS], *[deltas[k] for k in WEIGHTS],
            *[new_m[k] for k in WEIGHTS], *[new_v[k] for k in WEIGHTS])
```

```python
import functools

import numpy as np
import jax
import jax.numpy as jnp
from jax import lax
from jax.experimental import pallas as pl
from jax.experimental.pallas import tpu as pltpu

F32 = jnp.float32
BF16 = jnp.bfloat16

D_MODEL = 1024
DEPTH = 4
HEAD_DIM = 64
N_HEADS = 8
WIDTH = 512
D_FF = 2816
NORM_EPS = 1e-6
NEG_INF = -1e30
QK_SCALE = HEAD_DIM ** -0.5
LANES = 128
N_PAIR = N_HEADS // 2

GATE_OFF = 0
CONV_OFF = 3 * D_MODEL
FOX_OFF = CONV_OFF + 3 * WIDTH
SB_OFF = FOX_OFF + 3 * WIDTH
D_INA = SB_OFF + 3 * WIDTH

ADAM_LR = 0.001
ADAM_B1 = 0.9
ADAM_B2 = 0.999
ADAM_EPS = 1e-08
ADAM_WD = 0.01
ADAM_STEP = 10

VMEM_LIMIT = 48 * 1024 * 1024
ADAMW_BLOCK_BYTES = 1024 * 1024

PACK_COLS = 1024
PACK_ROW_ALIGN = 32
MESH = pl.DeviceIdType.MESH


def _params(sem):
    return pltpu.CompilerParams(dimension_semantics=sem, vmem_limit_bytes=VMEM_LIMIT)


def _dot(a, b, dims):
    return lax.dot_general(a, b, (dims, ((), ())), preferred_element_type=F32)


_NN = ((1,), (0,))
_NT = ((1,), (1,))
_TN = ((0,), (0,))


def _pick(dim, pref):
    if dim <= pref:
        return dim
    best = None
    for mult in range(1, dim // LANES + 1):
        t = mult * LANES
        if t <= pref and dim % t == 0:
            best = t
    assert best is not None, (dim, pref)
    return best


def _mm(a, b, mode, out_dtype=F32, tm=1024, tn=1024, tk=2048, res=None, name="mm"):
    if mode == "nn":
        (m, k), (_, n) = a.shape, b.shape
    elif mode == "nt":
        (m, k), (n, _) = a.shape, b.shape
    else:
        (k, m), (_, n) = a.shape, b.shape
    tm, tn, tk = _pick(m, tm), _pick(n, tn), _pick(k, tk)
    nk = k // tk
    dims = {"nn": _NN, "nt": _NT, "tn": _TN}[mode]
    if mode == "tn":
        a_spec = pl.BlockSpec((tk, tm), lambda i, j, kk: (kk, i))
    else:
        a_spec = pl.BlockSpec((tm, tk), lambda i, j, kk: (i, kk))
    if mode == "nt":
        b_spec = pl.BlockSpec((tn, tk), lambda i, j, kk: (j, kk))
    else:
        b_spec = pl.BlockSpec((tk, tn), lambda i, j, kk: (kk, j))
    o_spec = pl.BlockSpec((tm, tn), lambda i, j, kk: (i, j))
    in_specs = [a_spec, b_spec] + ([o_spec] if res is not None else [])

    def body(*refs):
        if res is not None:
            a_ref, b_ref, r_ref, o_ref, acc_ref = refs
        else:
            a_ref, b_ref, o_ref, acc_ref = refs
            r_ref = None
        kk = pl.program_id(2)
        part = _dot(a_ref[...].astype(BF16), b_ref[...].astype(BF16), dims)

        def finish(total):
            if r_ref is not None:
                total = total + r_ref[...].astype(F32)
            o_ref[...] = total.astype(out_dtype)

        if nk == 1:
            finish(part)
        else:
            @pl.when(kk == 0)
            def _():
                acc_ref[...] = part

            @pl.when(kk > 0)
            def _():
                acc_ref[...] += part

            @pl.when(kk == nk - 1)
            def _():
                finish(acc_ref[...])

    args = (a, b) + ((res,) if res is not None else ())
    return pl.pallas_call(
        body, name=name, grid=(m // tm, n // tn, nk), in_specs=in_specs, out_specs=o_spec,
        out_shape=jax.ShapeDtypeStruct((m, n), out_dtype),
        scratch_shapes=[pltpu.VMEM((tm, tn) if nk > 1 else (8, LANES), F32)],
        compiler_params=_params(("parallel", "parallel", "arbitrary")),
    )(*args)


def _norm_mm(x, g, w, out_dtype, tm=1024, tn=1536, name="norm_mm"):
    m, d = x.shape
    n = w.shape[1]
    tm, tn = _pick(m, tm), _pick(n, tn)

    def body(x_ref, g_ref, w_ref, o_ref, hn_ref):
        @pl.when(pl.program_id(1) == 0)
        def _():
            xf = x_ref[...]
            r = lax.rsqrt(jnp.mean(xf * xf, axis=-1, keepdims=True) + NORM_EPS)
            hn_ref[...] = (xf * r * g_ref[...]).astype(BF16)

        o_ref[...] = _dot(hn_ref[...], w_ref[...], _NN).astype(out_dtype)

    return pl.pallas_call(
        body, name=name, grid=(m // tm, n // tn),
        in_specs=[pl.BlockSpec((tm, d), lambda i, j: (i, 0)),
                  pl.BlockSpec((1, d), lambda i, j: (0, 0)),
                  pl.BlockSpec((d, tn), lambda i, j: (0, j))],
        out_specs=[pl.BlockSpec((tm, tn), lambda i, j: (i, j)),
                   pl.BlockSpec((tm, d), lambda i, j: (i, 0))],
        out_shape=[jax.ShapeDtypeStruct((m, n), out_dtype), jax.ShapeDtypeStruct((m, d), BF16)],
        compiler_params=_params(("parallel", "arbitrary")),
    )(x, g.reshape(1, d), w)


def _norm_bwd(dhn, x, g, dx_in, tm=256, name="norm_bwd"):
    m, d = x.shape
    tm = min(tm, m)

    def body(dhn_ref, x_ref, g_ref, dxi_ref, dx_ref, gg_ref):
        xf = x_ref[...]
        r = lax.rsqrt(jnp.mean(xf * xf, axis=-1, keepdims=True) + NORM_EPS)
        xhat = xf * r
        dh = dhn_ref[...].astype(F32)
        dxn = dh * g_ref[...]
        mean = jnp.mean(dxn * xhat, axis=-1, keepdims=True)
        dx_ref[...] = dxi_ref[...] + r * (dxn - xhat * mean)
        part = jnp.sum(dh * xhat, axis=0, keepdims=True)

        @pl.when(pl.program_id(0) == 0)
        def _():
            gg_ref[...] = part

        @pl.when(pl.program_id(0) > 0)
        def _():
            gg_ref[...] += part

    row = pl.BlockSpec((tm, d), lambda i: (i, 0))
    vec = pl.BlockSpec((1, d), lambda i: (0, 0))
    dx, gg = pl.pallas_call(
        body, name=name, grid=(m // tm,), in_specs=[row, row, vec, row], out_specs=[row, vec],
        out_shape=[jax.ShapeDtypeStruct((m, d), F32), jax.ShapeDtypeStruct((1, d), F32)],
        compiler_params=_params(("arbitrary",)),
    )(dhn, x, g.reshape(1, d), dx_in)
    return dx, gg.reshape(d)


def _down(u, k):
    s = u.shape[0]
    rows = lax.broadcasted_iota(jnp.int32, u.shape, 0)
    return jnp.where(rows < k, 0.0, pltpu.roll(u, k, axis=0))


def _up(u, k):
    s = u.shape[0]
    rows = lax.broadcasted_iota(jnp.int32, u.shape, 0)
    return jnp.where(rows >= s - k, 0.0, pltpu.roll(u, s - k, axis=0))


def _conv_mix_fwd(proj, conv_w, tc=128):
    s = proj.shape[0]
    nb = WIDTH // tc
    off = CONV_OFF // tc

    def body(b_ref, c_ref, h_ref, w_ref, o_ref):
        u = c_ref[...].astype(F32) * h_ref[...].astype(F32)
        w = w_ref[...]
        cv = w[0:1] * _down(u, 2) + w[1:2] * _down(u, 1) + w[2:3] * u
        o_ref[...] = (b_ref[...].astype(F32) * cv).astype(BF16)

    col = lambda k: pl.BlockSpec((s, tc), lambda j, k=k: (0, off + k * nb + j))
    return pl.pallas_call(
        body, name="conv_mix_fwd", grid=(nb,),
        in_specs=[col(0), col(1), col(2), pl.BlockSpec((3, tc), lambda j: (0, j))],
        out_specs=pl.BlockSpec((s, tc), lambda j: (0, j)),
        out_shape=jax.ShapeDtypeStruct((s, WIDTH), BF16),
        compiler_params=_params(("parallel",)),
    )(proj, proj, proj, conv_w)


def _conv_mix_bwd(do, proj, conv_w, tc=128):
    s = proj.shape[0]
    nb = WIDTH // tc
    off = CONV_OFF // tc

    def body(do_ref, b_ref, c_ref, h_ref, w_ref, db_ref, dc_ref, dh_ref, gw_ref):
        b = b_ref[...].astype(F32)
        c = c_ref[...].astype(F32)
        h = h_ref[...].astype(F32)
        g = do_ref[...].astype(F32)
        w = w_ref[...]
        u = c * h
        u1, u2 = _down(u, 1), _down(u, 2)
        cv = w[0:1] * u2 + w[1:2] * u1 + w[2:3] * u
        db_ref[...] = (g * cv).astype(BF16)
        dcv = g * b
        gw_ref[0:1, :] = jnp.sum(dcv * u2, axis=0, keepdims=True)
        gw_ref[1:2, :] = jnp.sum(dcv * u1, axis=0, keepdims=True)
        gw_ref[2:3, :] = jnp.sum(dcv * u, axis=0, keepdims=True)
        du = w[2:3] * dcv + w[1:2] * _up(dcv, 1) + w[0:1] * _up(dcv, 2)
        dc_ref[...] = (du * h).astype(BF16)
        dh_ref[...] = (du * c).astype(BF16)

    col = lambda k: pl.BlockSpec((s, tc), lambda j, k=k: (0, off + k * nb + j))
    own = pl.BlockSpec((s, tc), lambda j: (0, j))
    wsp = pl.BlockSpec((3, tc), lambda j: (0, j))
    act = jax.ShapeDtypeStruct((s, WIDTH), BF16)
    return pl.pallas_call(
        body, name="conv_mix_bwd", grid=(nb,),
        in_specs=[own, col(0), col(1), col(2), wsp], out_specs=[own, own, own, wsp],
        out_shape=[act, act, act, jax.ShapeDtypeStruct((3, WIDTH), F32)],
        compiler_params=_params(("parallel",)),
    )(do, proj, proj, proj, conv_w)


def _glu_fwd(up, w, b, tc=256):
    s = up.shape[0]
    nb = D_FF // tc

    def body(g_ref, v_ref, w_ref, b_ref, o_ref):
        ug = g_ref[...].astype(F32)
        wv = w_ref[...]
        cg = wv[0:1] * _down(ug, 2) + wv[1:2] * _down(ug, 1) + wv[2:3] * ug + b_ref[...]
        act = cg * jax.nn.sigmoid(cg)
        o_ref[...] = (act * v_ref[...].astype(F32)).astype(BF16)

    return pl.pallas_call(
        body, name="glu_fwd", grid=(nb,),
        in_specs=[pl.BlockSpec((s, tc), lambda j: (0, j)), pl.BlockSpec((s, tc), lambda j: (0, nb + j)),
                  pl.BlockSpec((3, tc), lambda j: (0, j)), pl.BlockSpec((1, tc), lambda j: (0, j))],
        out_specs=pl.BlockSpec((s, tc), lambda j: (0, j)),
        out_shape=jax.ShapeDtypeStruct((s, D_FF), BF16),
        compiler_params=_params(("parallel",)),
    )(up, up, w, b.reshape(1, D_FF))


def _glu_bwd(dh, up, w, b, tc=256):
    s = up.shape[0]
    nb = D_FF // tc

    def body(dh_ref, g_ref, v_ref, w_ref, b_ref, dg_ref, dv_ref, gw_ref, gb_ref):
        ug = g_ref[...].astype(F32)
        uv = v_ref[...].astype(F32)
        d = dh_ref[...].astype(F32)
        wv = w_ref[...]
        u1, u2 = _down(ug, 1), _down(ug, 2)
        cg = wv[0:1] * u2 + wv[1:2] * u1 + wv[2:3] * ug + b_ref[...]
        sg = jax.nn.sigmoid(cg)
        dv_ref[...] = (d * (cg * sg)).astype(BF16)
        dcg = d * uv * (sg * (1.0 + cg * (1.0 - sg)))
        gb_ref[...] = jnp.sum(dcg, axis=0, keepdims=True)
        gw_ref[0:1, :] = jnp.sum(dcg * u2, axis=0, keepdims=True)
        gw_ref[1:2, :] = jnp.sum(dcg * u1, axis=0, keepdims=True)
        gw_ref[2:3, :] = jnp.sum(dcg * ug, axis=0, keepdims=True)
        dg_ref[...] = (wv[2:3] * dcg + wv[1:2] * _up(dcg, 1) + wv[0:1] * _up(dcg, 2)).astype(BF16)

    own = pl.BlockSpec((s, tc), lambda j: (0, j))
    wsp = pl.BlockSpec((3, tc), lambda j: (0, j))
    bsp = pl.BlockSpec((1, tc), lambda j: (0, j))
    act = jax.ShapeDtypeStruct((s, D_FF), BF16)
    dg, dv, gw, gb = pl.pallas_call(
        body, name="glu_bwd", grid=(nb,),
        in_specs=[own, own, pl.BlockSpec((s, tc), lambda j: (0, nb + j)), wsp, bsp],
        out_specs=[own, own, wsp, bsp],
        out_shape=[act, act, jax.ShapeDtypeStruct((3, D_FF), F32), jax.ShapeDtypeStruct((1, D_FF), F32)],
        compiler_params=_params(("parallel",)),
    )(dh, up, up, w, b.reshape(1, D_FF))
    return dg, dv, gw, gb.reshape(D_FF)


def _merge_fwd(x, oc, of, osb, wpc, wpf, wps, proj, gb, wout, tm=256):
    s, d = x.shape
    tm = min(tm, s)

    def body(x_ref, oc_ref, of_ref, os_ref, wpc_ref, wpf_ref, wps_ref, g0_ref, g1_ref, g2_ref, gb_ref, wo_ref,
             xo_ref, mg_ref, yc_ref, yf_ref, ys_ref):
        merged = jnp.zeros((tm, d), F32)
        for k, (o_ref, w_ref, g_ref, y_ref) in enumerate(
                ((oc_ref, wpc_ref, g0_ref, yc_ref), (of_ref, wpf_ref, g1_ref, yf_ref), (os_ref, wps_ref, g2_ref, ys_ref))):
            y = _dot(o_ref[...].astype(BF16), w_ref[...], _NN)
            y_ref[...] = y.astype(BF16)
            gate = jax.nn.sigmoid(g_ref[...].astype(F32) + gb_ref[:, k * d:(k + 1) * d])
            merged = merged + gate * y
        mb = merged.astype(BF16)
        mg_ref[...] = mb
        xo_ref[...] = x_ref[...] + _dot(mb, wo_ref[...], _NN)

    rowd = pl.BlockSpec((tm, d), lambda i: (i, 0))
    roww = pl.BlockSpec((tm, WIDTH), lambda i: (i, 0))
    wp = pl.BlockSpec((WIDTH, d), lambda i: (0, 0))
    gcol = lambda k: pl.BlockSpec((tm, d), lambda i, k=k: (i, GATE_OFF // d + k))
    actd = jax.ShapeDtypeStruct((s, d), BF16)
    return pl.pallas_call(
        body, name="merge_fwd", grid=(s // tm,),
        in_specs=[rowd, roww, roww, roww, wp, wp, wp, gcol(0), gcol(1), gcol(2),
                  pl.BlockSpec((1, 3 * d), lambda i: (0, 0)), pl.BlockSpec((d, d), lambda i: (0, 0))],
        out_specs=[rowd, rowd, rowd, rowd, rowd],
        out_shape=[jax.ShapeDtypeStruct((s, d), F32), actd, actd, actd, actd],
        compiler_params=_params(("parallel",)),
    )(x, oc, of, osb, wpc, wpf, wps, proj, proj, proj, gb.reshape(1, 3 * d), wout)


def _gate_bwd(dm, yc, yf, ys, proj, gb, tm=256):
    s, d = dm.shape
    tm = min(tm, s)

    def body(dm_ref, yc_ref, yf_ref, ys_ref, g0_ref, g1_ref, g2_ref, gb_ref, dyc_ref, dyf_ref, dys_ref, dgl_ref, ggb_ref):
        g = dm_ref[...].astype(F32)
        parts = []
        for k, (y_ref, g_ref, dy_ref) in enumerate(((yc_ref, g0_ref, dyc_ref), (yf_ref, g1_ref, dyf_ref), (ys_ref, g2_ref, dys_ref))):
            gate = jax.nn.sigmoid(g_ref[...].astype(F32) + gb_ref[:, k * d:(k + 1) * d])
            dy_ref[...] = (g * gate).astype(BF16)
            dgl = g * y_ref[...].astype(F32) * gate * (1.0 - gate)
            dgl_ref[:, k * d:(k + 1) * d] = dgl.astype(BF16)
            parts.append(jnp.sum(dgl, axis=0, keepdims=True))
        part = jnp.concatenate(parts, axis=1)

        @pl.when(pl.program_id(0) == 0)
        def _():
            ggb_ref[...] = part

        @pl.when(pl.program_id(0) > 0)
        def _():
            ggb_ref[...] += part

    rowd = pl.BlockSpec((tm, d), lambda i: (i, 0))
    gcol = lambda k: pl.BlockSpec((tm, d), lambda i, k=k: (i, GATE_OFF // d + k))
    vec = pl.BlockSpec((1, 3 * d), lambda i: (0, 0))
    actd = jax.ShapeDtypeStruct((s, d), BF16)
    dyc, dyf, dys, dgl, ggb = pl.pallas_call(
        body, name="gate_bwd", grid=(s // tm,),
        in_specs=[rowd, rowd, rowd, rowd, gcol(0), gcol(1), gcol(2), vec],
        out_specs=[rowd, rowd, rowd, pl.BlockSpec((tm, 3 * d), lambda i: (i, 0)), vec],
        out_shape=[actd, actd, actd, jax.ShapeDtypeStruct((s, 3 * d), BF16), jax.ShapeDtypeStruct((1, 3 * d), F32)],
        compiler_params=_params(("arbitrary",)),
    )(dm, yc, yf, ys, proj, proj, proj, gb.reshape(1, 3 * d))
    return dyc, dyf, dys, dgl, ggb.reshape(3 * d)


def _tri_tables(n, order):
    qs, ks = [], []
    if order == "k_outer":
        for kj in range(n):
            for qi in range(kj, n):
                qs.append(qi)
                ks.append(kj)
    else:
        for qi in range(n):
            for kj in (range(qi, -1, -1) if order == "k_desc" else range(qi + 1)):
                qs.append(qi)
                ks.append(kj)
    return jnp.asarray(np.array(qs, np.int32)), jnp.asarray(np.array(ks, np.int32))


def _lo_mask(shape):
    return lax.broadcasted_iota(jnp.int32, shape, len(shape) - 1) < HEAD_DIM


def _head(x, h):
    lo = _lo_mask(x.shape)
    return jnp.where(lo if h == 0 else jnp.logical_not(lo), x, jnp.zeros_like(x))


def _pair_cols(x, h):
    return x[:, h * HEAD_DIM:h * HEAD_DIM + 1]


def _rep(a0, a1, shape):
    return jnp.where(_lo_mask(shape), a0, a1)


def _positions(qi, kj, t):
    row = qi * t + lax.broadcasted_iota(jnp.int32, (t, t), 0)
    col = kj * t + lax.broadcasted_iota(jnp.int32, (t, t), 1)
    return row, col


def _head_norm(x, g):
    lo = _lo_mask(x.shape)
    sq = x * x
    s0 = jnp.sum(jnp.where(lo, sq, 0.0), axis=-1, keepdims=True)
    s1 = jnp.sum(jnp.where(lo, 0.0, sq), axis=-1, keepdims=True)
    r = jnp.where(lo, lax.rsqrt(s0 / HEAD_DIM + NORM_EPS), lax.rsqrt(s1 / HEAD_DIM + NORM_EPS))
    return x * r, r


def _fox_prep(proj, gq, gk, tm=512):
    s = proj.shape[0]
    tm = min(tm, s)
    off = FOX_OFF // LANES

    def body(q_ref, k_ref, gq_ref, gk_ref, qn_ref, kn_ref):
        qh, _ = _head_norm(q_ref[...].astype(F32), None)
        kh, _ = _head_norm(k_ref[...].astype(F32), None)
        qn_ref[...] = (qh * gq_ref[...] * QK_SCALE).astype(BF16)
        kn_ref[...] = (kh * gk_ref[...]).astype(BF16)

    vec = pl.BlockSpec((1, LANES), lambda p, i: (0, 0))
    own = pl.BlockSpec((tm, LANES), lambda p, i: (i, p))
    act = jax.ShapeDtypeStruct((s, WIDTH), BF16)
    return pl.pallas_call(
        body, name="fox_prep", grid=(N_PAIR, s // tm),
        in_specs=[pl.BlockSpec((tm, LANES), lambda p, i: (i, off + p)),
                  pl.BlockSpec((tm, LANES), lambda p, i: (i, off + N_PAIR + p)), vec, vec],
        out_specs=[own, own], out_shape=[act, act],
        compiler_params=_params(("parallel", "parallel")),
    )(proj, proj, jnp.tile(gq, 2).reshape(1, LANES), jnp.tile(gk, 2).reshape(1, LANES))


def _fox_post(dqs, dkn, proj, gq, gk, tm=512):
    s = proj.shape[0]
    tm = min(tm, s)
    off = FOX_OFF // LANES

    def one(d_ref, x_ref, g_ref, scale, dx_ref, gg_ref, first):
        xhat, r = _head_norm(x_ref[...].astype(F32), None)
        dy = d_ref[...] * scale
        part = jnp.sum(dy * xhat, axis=0, keepdims=True)

        @pl.when(first)
        def _():
            gg_ref[...] = part

        @pl.when(jnp.logical_not(first))
        def _():
            gg_ref[...] += part

        dxh = dy * g_ref[...]
        lo = _lo_mask(dxh.shape)
        pr = dxh * xhat
        m0 = jnp.sum(jnp.where(lo, pr, 0.0), axis=-1, keepdims=True)
        m1 = jnp.sum(jnp.where(lo, 0.0, pr), axis=-1, keepdims=True)
        mean = jnp.where(lo, m0, m1) / HEAD_DIM
        dx_ref[...] = (r * (dxh - xhat * mean)).astype(BF16)

    def body(dq_ref, dk_ref, q_ref, k_ref, gq_ref, gk_ref, dxq_ref, dxk_ref, ggq_ref, ggk_ref):
        first = pl.program_id(1) == 0
        one(dq_ref, q_ref, gq_ref, QK_SCALE, dxq_ref, ggq_ref, first)
        one(dk_ref, k_ref, gk_ref, 1.0, dxk_ref, ggk_ref, first)

    vec = pl.BlockSpec((1, LANES), lambda p, i: (0, 0))
    own = pl.BlockSpec((tm, LANES), lambda p, i: (i, p))
    ggs = pl.BlockSpec((None, 1, LANES), lambda p, i: (p, 0, 0))
    act = jax.ShapeDtypeStruct((s, WIDTH), BF16)
    ggo = jax.ShapeDtypeStruct((N_PAIR, 1, LANES), F32)
    dxq, dxk, ggq, ggk = pl.pallas_call(
        body, name="fox_post", grid=(N_PAIR, s // tm),
        in_specs=[own, own, pl.BlockSpec((tm, LANES), lambda p, i: (i, off + p)),
                  pl.BlockSpec((tm, LANES), lambda p, i: (i, off + N_PAIR + p)), vec, vec],
        out_specs=[own, own, ggs, ggs], out_shape=[act, act, ggo, ggo],
        compiler_params=_params(("parallel", "arbitrary")),
    )(dqs, dkn, proj, proj, jnp.tile(gq, 2).reshape(1, LANES), jnp.tile(gk, 2).reshape(1, LANES))
    fold = lambda a: a.reshape(N_HEADS, HEAD_DIM).sum(axis=0)
    return dxq, dxk, fold(ggq), fold(ggk)


def _split3(x):
    a = x.astype(BF16)
    r = x - a.astype(F32)
    b = r.astype(BF16)
    c = (r - b.astype(F32)).astype(BF16)
    return a, b, c


def _split2(x):
    a = x.astype(BF16)
    b = (x - a.astype(F32)).astype(BF16)
    return a, b


def _log_sigmoid(x):
    return jnp.minimum(x, 0.0) - jnp.log(1.0 + jnp.exp(-jnp.abs(x)))


def _fox_gates(ft, bias):
    h, s = ft.shape
    nb = s // LANES

    def body(f_ref, b_ref, c_ref):
        lf = _log_sigmoid(f_ref[...] + b_ref[...])
        i = lax.broadcasted_iota(jnp.int32, (s, LANES), 0)
        j = pl.program_id(0) * LANES + lax.broadcasted_iota(jnp.int32, (s, LANES), 1)
        tri = jnp.where(i <= j, 1.0, 0.0).astype(BF16)
        c_ref[...] = sum(_dot(p, tri, _NN) for p in _split3(lf))

    return pl.pallas_call(
        body, name="fox_gates", grid=(nb,),
        in_specs=[pl.BlockSpec((h, s), lambda j: (0, 0)), pl.BlockSpec((h, 1), lambda j: (0, 0))],
        out_specs=pl.BlockSpec((h, LANES), lambda j: (0, j)),
        out_shape=jax.ShapeDtypeStruct((h, s), F32),
        compiler_params=_params(("parallel",)),
    )(ft, bias.reshape(h, 1))


def _fox_gates_bwd(dc, ft, bias):
    h, s = ft.shape
    nb = s // LANES

    def body(dc_ref, f_ref, fb_ref, b_ref, df_ref, gb_ref):
        i = lax.broadcasted_iota(jnp.int32, (s, LANES), 0)
        j = pl.program_id(0) * LANES + lax.broadcasted_iota(jnp.int32, (s, LANES), 1)
        tri = jnp.where(i >= j, 1.0, 0.0).astype(BF16)
        dlf = sum(_dot(p, tri, _NN) for p in _split3(dc_ref[...]))
        df = dlf * jax.nn.sigmoid(-(fb_ref[...] + b_ref[...]))
        df_ref[...] = df
        part = jnp.sum(df, axis=-1, keepdims=True)

        @pl.when(pl.program_id(0) == 0)
        def _():
            gb_ref[...] = part

        @pl.when(pl.program_id(0) > 0)
        def _():
            gb_ref[...] += part

    full = pl.BlockSpec((h, s), lambda j: (0, 0))
    blk = pl.BlockSpec((h, LANES), lambda j: (0, j))
    one = pl.BlockSpec((h, 1), lambda j: (0, 0))
    df, gb = pl.pallas_call(
        body, name="fox_gates_bwd", grid=(nb,), in_specs=[full, full, blk, one], out_specs=[blk, one],
        out_shape=[jax.ShapeDtypeStruct((h, s), F32), jax.ShapeDtypeStruct((h, 1), F32)],
        compiler_params=_params(("arbitrary",)),
    )(dc, ft, ft, bias.reshape(h, 1))
    return df, gb.reshape(h)


def _delta_rep(do, o, tm=512):
    s = do.shape[0]
    tm = min(tm, s)

    def body(do_ref, o_ref, d_ref):
        pr = do_ref[...].astype(F32) * o_ref[...].astype(F32)
        lo = _lo_mask(pr.shape)
        d0 = jnp.sum(jnp.where(lo, pr, 0.0), axis=-1, keepdims=True)
        d1 = jnp.sum(jnp.where(lo, 0.0, pr), axis=-1, keepdims=True)
        d_ref[...] = jnp.where(lo, d0, d1)

    own = pl.BlockSpec((tm, LANES), lambda p, i: (i, p))
    return pl.pallas_call(
        body, name="delta_rep", grid=(N_PAIR, s // tm), in_specs=[own, own],
        out_specs=pl.BlockSpec((None, tm, LANES), lambda p, i: (p, i, 0)),
        out_shape=jax.ShapeDtypeStruct((N_PAIR, s, LANES), F32),
        compiler_params=_params(("parallel", "parallel")),
    )(do, o)


def _tile(s, t):
    t = min(t, s)
    assert s % t == 0
    return t


def _fox_fwd(qn, kn, proj, ccol, crow, t=512):
    s = qn.shape[0]
    t = _tile(s, t)
    n = s // t
    qtab, ktab = _tri_tables(n, "k_asc")
    voff = FOX_OFF // LANES + 2 * N_PAIR

    def body(qt_ref, kt_ref, q_ref, k_ref, v_ref, cc_ref, cr_ref, o_ref, lse_ref, m_ref, l_ref, acc_ref):
        i = pl.program_id(1)
        qi, kj = qt_ref[i], kt_ref[i]

        @pl.when(kj == 0)
        def _():
            m_ref[...] = jnp.full(m_ref.shape, NEG_INF, F32)
            l_ref[...] = jnp.zeros(l_ref.shape, F32)
            acc_ref[...] = jnp.zeros(acc_ref.shape, F32)

        q, k, v = q_ref[...], k_ref[...], v_ref[...]
        row, col = _positions(qi, kj, t)
        causal = col <= row
        m_old = m_ref[...]
        mn, rs, pv = [], [], []
        for h in range(2):
            sc = _dot(_head(q, h), k, _NT) + _pair_cols(cc_ref[...], h) - cr_ref[h:h + 1, :]
            sc = jnp.where(causal, sc, NEG_INF)
            m_new = jnp.maximum(_pair_cols(m_old, h), jnp.max(sc, axis=-1, keepdims=True))
            p = jnp.exp(sc - m_new)
            mn.append(m_new)
            rs.append(jnp.sum(p, axis=-1, keepdims=True))
            pv.append(_dot(p.astype(BF16), _head(v, h), _NN))
        m_rep = _rep(mn[0], mn[1], m_old.shape)
        alpha = jnp.exp(m_old - m_rep)
        l_ref[...] = alpha * l_ref[...] + _rep(rs[0], rs[1], m_old.shape)
        acc_ref[...] = alpha * acc_ref[...] + pv[0] + pv[1]
        m_ref[...] = m_rep

        @pl.when(kj == qi)
        def _():
            o_ref[...] = acc_ref[...] / l_ref[...]
            lse_ref[...] = m_ref[...] + jnp.log(l_ref[...])

    grid_spec = pltpu.PrefetchScalarGridSpec(
        num_scalar_prefetch=2, grid=(N_PAIR, n * (n + 1) // 2),
        in_specs=[pl.BlockSpec((t, LANES), lambda p, i, qt, kt: (qt[i], p)),
                  pl.BlockSpec((t, LANES), lambda p, i, qt, kt: (kt[i], p)),
                  pl.BlockSpec((t, LANES), lambda p, i, qt, kt: (kt[i], voff + p)),
                  pl.BlockSpec((None, t, LANES), lambda p, i, qt, kt: (p, qt[i], 0)),
                  pl.BlockSpec((None, 2, t), lambda p, i, qt, kt: (p, 0, kt[i]))],
        out_specs=[pl.BlockSpec((t, LANES), lambda p, i, qt, kt: (qt[i], p)),
                   pl.BlockSpec((None, t, LANES), lambda p, i, qt, kt: (p, qt[i], 0))],
        scratch_shapes=[pltpu.VMEM((t, LANES), F32)] * 3)
    return pl.pallas_call(
        body, name="fox_fwd", grid_spec=grid_spec,
        out_shape=[jax.ShapeDtypeStruct((s, WIDTH), F32), jax.ShapeDtypeStruct((N_PAIR, s, LANES), F32)],
        compiler_params=_params(("parallel", "arbitrary")),
    )(qtab, ktab, qn, kn, proj, ccol, crow)


def _fox_bwd(qn, kn, proj, do, lse, delta, ccol, crow, t=512):
    s = qn.shape[0]
    t = _tile(s, t)
    n = s // t
    qtab, ktab = _tri_tables(n, "k_outer")
    voff = FOX_OFF // LANES + 2 * N_PAIR

    def body(qt_ref, kt_ref, q_ref, k_ref, v_ref, do_ref, lse_ref, dl_ref, cc_ref, cr_ref,
             dq_ref, dk_ref, dv_ref, dc_ref, dcq_ref, dka_ref, dva_ref, dca_ref):
        i = pl.program_id(1)
        qi, kj = qt_ref[i], kt_ref[i]

        @pl.when(i == 0)
        def _():
            dq_ref[...] = jnp.zeros(dq_ref.shape, F32)
            dcq_ref[...] = jnp.zeros(dcq_ref.shape, F32)

        @pl.when(qi == kj)
        def _():
            dka_ref[...] = jnp.zeros(dka_ref.shape, F32)
            dva_ref[...] = jnp.zeros(dva_ref.shape, F32)
            dca_ref[...] = jnp.zeros(dca_ref.shape, F32)

        q, k, v, g = q_ref[...], k_ref[...], v_ref[...], do_ref[...]
        row, col = _positions(qi, kj, t)
        causal = col <= row
        dq = jnp.zeros((t, LANES), F32)
        dk = jnp.zeros((t, LANES), F32)
        dv = jnp.zeros((t, LANES), F32)
        rowsum = []
        for h in range(2):
            qh, gh = _head(q, h), _head(g, h)
            sc = _dot(qh, k, _NT) + _pair_cols(cc_ref[...], h) - cr_ref[h:h + 1, :]
            p = jnp.where(causal, jnp.exp(sc - _pair_cols(lse_ref[...], h)), 0.0)
            dp = _dot(gh, v, _NT)
            ds = p * (dp - _pair_cols(dl_ref[...], h))
            dsb = ds.astype(BF16)
            dv = dv + _dot(p.astype(BF16), gh, _TN)
            dk = dk + _dot(dsb, qh, _TN)
            dq = dq + _dot(dsb, _head(k, h), _NN)
            dca_ref[h:h + 1, :] -= jnp.sum(ds, axis=0, keepdims=True)
            rowsum.append(jnp.sum(ds, axis=-1, keepdims=True))
        dka_ref[...] += dk
        dva_ref[...] += dv
        rows = pl.ds(pl.multiple_of(qi * t, t), t)
        dq_ref[rows, :] += dq
        dcq_ref[rows, :] += _rep(rowsum[0], rowsum[1], (t, LANES))

        @pl.when(qi == n - 1)
        def _():
            dk_ref[...] = dka_ref[...]
            dv_ref[...] = dva_ref[...].astype(BF16)
            dc_ref[...] = dca_ref[...]

    qblk = lambda p, i, qt, kt: (qt[i], p)
    kblk = lambda p, i, qt, kt: (kt[i], p)
    qrep = pl.BlockSpec((None, t, LANES), lambda p, i, qt, kt: (p, qt[i], 0))
    crs = pl.BlockSpec((None, 2, t), lambda p, i, qt, kt: (p, 0, kt[i]))
    grid_spec = pltpu.PrefetchScalarGridSpec(
        num_scalar_prefetch=2, grid=(N_PAIR, n * (n + 1) // 2),
        in_specs=[pl.BlockSpec((t, LANES), qblk), pl.BlockSpec((t, LANES), kblk),
                  pl.BlockSpec((t, LANES), lambda p, i, qt, kt: (kt[i], voff + p)),
                  pl.BlockSpec((t, LANES), qblk), qrep, qrep, qrep, crs],
        out_specs=[pl.BlockSpec((s, LANES), lambda p, i, qt, kt: (0, p)),
                   pl.BlockSpec((t, LANES), kblk), pl.BlockSpec((t, LANES), kblk), crs,
                   pl.BlockSpec((None, s, LANES), lambda p, i, qt, kt: (p, 0, 0))],
        scratch_shapes=[pltpu.VMEM((t, LANES), F32), pltpu.VMEM((t, LANES), F32), pltpu.VMEM((2, t), F32)])
    return pl.pallas_call(
        body, name="fox_bwd", grid_spec=grid_spec,
        out_shape=[jax.ShapeDtypeStruct((s, WIDTH), F32), jax.ShapeDtypeStruct((s, WIDTH), F32),
                   jax.ShapeDtypeStruct((s, WIDTH), BF16), jax.ShapeDtypeStruct((N_PAIR, 2, s), F32),
                   jax.ShapeDtypeStruct((N_PAIR, s, LANES), F32)],
        compiler_params=_params(("parallel", "arbitrary")),
    )(qtab, ktab, qn, kn, proj, do, lse, delta, ccol, crow)


def _sb_tile(qh, k, strict, t, r_col):
    z = _dot(qh, k, _NT)
    lg = jnp.where(strict, -(jnp.maximum(z, 0.0) + jnp.log(1.0 + jnp.exp(-jnp.abs(z)))), 0.0)
    jj = lax.broadcasted_iota(jnp.int32, (t, t), 0)
    ss = lax.broadcasted_iota(jnp.int32, (t, t), 1)
    above = jnp.where(jj > ss, 1.0, 0.0).astype(BF16)
    suffix = sum(_dot(p, above, _NN) for p in _split2(lg)) + r_col
    a = jnp.where(strict, jnp.exp(lg + z + suffix), 0.0)
    return z, lg, a


def _sb_fwd(proj, t=256):
    s = proj.shape[0]
    t = _tile(s, t)
    n = s // t
    qtab, ktab = _tri_tables(n, "k_desc")
    off = SB_OFF // LANES

    def body(qt_ref, kt_ref, q_ref, k_ref, v_ref, o_ref, acc_ref, r_ref):
        i = pl.program_id(1)
        qi, kj = qt_ref[i], kt_ref[i]

        @pl.when(kj == qi)
        def _():
            acc_ref[...] = jnp.zeros(acc_ref.shape, F32)
            r_ref[...] = jnp.zeros(r_ref.shape, F32)

        q = q_ref[...] * QK_SCALE
        k, v = k_ref[...], v_ref[...]
        row, col = _positions(qi, kj, t)
        strict = col < row
        acc = acc_ref[...]
        for h in range(2):
            _, lg, a = _sb_tile(_head(q, h), k, strict, t, r_ref[h])
            acc = acc + _dot(a.astype(BF16), _head(v, h), _NN)
            r_ref[h] += jnp.sum(lg, axis=-1, keepdims=True)
        acc_ref[...] = acc

        @pl.when(kj == 0)
        def _():
            o_ref[...] = acc_ref[...]

    grid_spec = pltpu.PrefetchScalarGridSpec(
        num_scalar_prefetch=2, grid=(N_PAIR, n * (n + 1) // 2),
        in_specs=[pl.BlockSpec((t, LANES), lambda p, i, qt, kt: (qt[i], off + p)),
                  pl.BlockSpec((t, LANES), lambda p, i, qt, kt: (kt[i], off + N_PAIR + p)),
                  pl.BlockSpec((t, LANES), lambda p, i, qt, kt: (kt[i], off + 2 * N_PAIR + p))],
        out_specs=pl.BlockSpec((t, LANES), lambda p, i, qt, kt: (qt[i], p)),
        scratch_shapes=[pltpu.VMEM((t, LANES), F32), pltpu.VMEM((2, t, 1), F32)])
    return pl.pallas_call(
        body, name="sb_fwd", grid_spec=grid_spec, out_shape=jax.ShapeDtypeStruct((s, WIDTH), F32),
        compiler_params=_params(("parallel", "arbitrary")),
    )(qtab, ktab, proj, proj, proj)


def _sb_bwd(proj, do, delta, t=256):
    s = proj.shape[0]
    t = _tile(s, t)
    n = s // t
    ntri = n * (n + 1) // 2
    qtab, ktab = _tri_tables(n, "k_desc")
    off = SB_OFF // LANES

    def body(qt_ref, kt_ref, q_ref, k_ref, v_ref, do_ref, dl_ref, dq_ref, dk_ref, dv_ref,
             dqa_ref, dka_ref, dva_ref, r_ref, rd_ref):
        i = pl.program_id(1)
        qi, kj = qt_ref[i], kt_ref[i]

        @pl.when(i == 0)
        def _():
            dka_ref[...] = jnp.zeros(dka_ref.shape, F32)
            dva_ref[...] = jnp.zeros(dva_ref.shape, F32)

        @pl.when(kj == qi)
        def _():
            dqa_ref[...] = jnp.zeros(dqa_ref.shape, F32)
            r_ref[...] = jnp.zeros(r_ref.shape, F32)
            rd_ref[...] = jnp.zeros(rd_ref.shape, F32)

        q = q_ref[...] * QK_SCALE
        k, v, g = k_ref[...], v_ref[...], do_ref[...]
        row, col = _positions(qi, kj, t)
        strict = col < row
        ss = lax.broadcasted_iota(jnp.int32, (t, t), 0)
        jj = lax.broadcasted_iota(jnp.int32, (t, t), 1)
        at_or_after = jnp.where(ss >= jj, 1.0, 0.0).astype(BF16)
        dq = jnp.zeros((t, LANES), F32)
        dk = jnp.zeros((t, LANES), F32)
        dv = jnp.zeros((t, LANES), F32)
        for h in range(2):
            qh, gh = _head(q, h), _head(g, h)
            z, lg, a = _sb_tile(qh, k, strict, t, r_ref[h])
            da = _dot(gh, v, _NT)
            ab = a.astype(BF16)
            d_a = da * ab.astype(F32)
            incl = sum(_dot(p, at_or_after, _NN) for p in _split2(d_a))
            d_l = _pair_cols(dl_ref[...], h) - rd_ref[h] - incl
            sig = jnp.exp(lg + z)
            dz = jnp.where(strict, d_a * (1.0 - sig) - d_l * sig, 0.0).astype(BF16)
            dq = dq + _dot(dz, _head(k, h), _NN)
            dk = dk + _dot(dz, qh, _TN)
            dv = dv + _dot(ab, gh, _TN)
            r_ref[h] += jnp.sum(lg, axis=-1, keepdims=True)
            rd_ref[h] += jnp.sum(d_a, axis=-1, keepdims=True)
        dqa_ref[...] += dq
        rows = pl.ds(pl.multiple_of(kj * t, t), t)
        dka_ref[rows, :] += dk
        dva_ref[rows, :] += dv

        @pl.when(kj == 0)
        def _():
            dq_ref[...] = (dqa_ref[...] * QK_SCALE).astype(BF16)

        @pl.when(i == ntri - 1)
        def _():
            dk_ref[...] = dka_ref[...].astype(BF16)
            dv_ref[...] = dva_ref[...].astype(BF16)

    qblk = lambda p, i, qt, kt: (qt[i], p)
    whole = pl.BlockSpec((s, LANES), lambda p, i, qt, kt: (0, p))
    grid_spec = pltpu.PrefetchScalarGridSpec(
        num_scalar_prefetch=2, grid=(N_PAIR, ntri),
        in_specs=[pl.BlockSpec((t, LANES), lambda p, i, qt, kt: (qt[i], off + p)),
                  pl.BlockSpec((t, LANES), lambda p, i, qt, kt: (kt[i], off + N_PAIR + p)),
                  pl.BlockSpec((t, LANES), lambda p, i, qt, kt: (kt[i], off + 2 * N_PAIR + p)),
                  pl.BlockSpec((t, LANES), qblk),
                  pl.BlockSpec((None, t, LANES), lambda p, i, qt, kt: (p, qt[i], 0))],
        out_specs=[pl.BlockSpec((t, LANES), qblk), whole, whole],
        scratch_shapes=[pltpu.VMEM((t, LANES), F32), pltpu.VMEM((s, LANES), F32), pltpu.VMEM((s, LANES), F32),
                        pltpu.VMEM((2, t, 1), F32), pltpu.VMEM((2, t, 1), F32)])
    act = jax.ShapeDtypeStruct((s, WIDTH), BF16)
    return pl.pallas_call(
        body, name="sb_bwd", grid_spec=grid_spec, out_shape=[act, act, act],
        compiler_params=_params(("parallel", "arbitrary")),
    )(qtab, ktab, proj, proj, proj, do, delta)


def _loss_head(y, target, tm=256):
    s, d = y.shape
    tm = min(tm, s)

    def body(y_ref, t_ref, l_ref, dy_ref):
        e = y_ref[...] - t_ref[...]
        dy_ref[...] = e / d
        part = jnp.sum(jnp.sum(e * e, axis=0, keepdims=True), axis=1, keepdims=True)

        @pl.when(pl.program_id(0) == 0)
        def _():
            l_ref[...] = jnp.broadcast_to(part, l_ref.shape)

        @pl.when(pl.program_id(0) > 0)
        def _():
            l_ref[...] += jnp.broadcast_to(part, l_ref.shape)

    row = pl.BlockSpec((tm, d), lambda i: (i, 0))
    return pl.pallas_call(
        body, name="loss_head", grid=(s // tm,), in_specs=[row, row],
        out_specs=[pl.BlockSpec((1, LANES), lambda i: (0, 0)), row],
        out_shape=[jax.ShapeDtypeStruct((1, LANES), F32), jax.ShapeDtypeStruct((s, d), F32)],
        compiler_params=_params(("arbitrary",)),
    )(y, target)


def _adamw(w, g, m, v, name):
    shape = w.shape
    cols = shape[-1]
    rows = int(np.prod(shape[:-1]))
    tm = rows
    for cand in (512, 256, 128, 64, 32, 16, 8):
        if rows % cand == 0 and rows > cand and cand * cols * 4 <= ADAMW_BLOCK_BYTES:
            tm = cand
            break

    def body(w_ref, g_ref, m_ref, v_ref, d_ref, mo_ref, vo_ref):
        gr = g_ref[...]
        mn = ADAM_B1 * m_ref[...] + (1.0 - ADAM_B1) * gr
        vn = ADAM_B2 * v_ref[...] + (1.0 - ADAM_B2) * (gr * gr)
        m_hat = mn / (1.0 - ADAM_B1 ** ADAM_STEP)
        v_hat = vn / (1.0 - ADAM_B2 ** ADAM_STEP)
        d_ref[...] = -ADAM_LR * (m_hat / (jnp.sqrt(v_hat) + ADAM_EPS) + ADAM_WD * w_ref[...])
        mo_ref[...] = mn
        vo_ref[...] = vn

    blk = pl.BlockSpec((tm, cols), lambda i: (i, 0))
    out = jax.ShapeDtypeStruct((rows, cols), F32)
    r2 = lambda a: a.reshape(rows, cols)
    outs = pl.pallas_call(
        body, name=name, grid=(rows // tm,), in_specs=[blk] * 4, out_specs=[blk] * 3, out_shape=[out] * 3,
        compiler_params=_params(("parallel",)),
    )(r2(w), r2(g), r2(m), r2(v))
    return tuple(o.reshape(shape) for o in outs)


def _split_w_in(w_in):
    c0, c1, c2, c3 = 3 * WIDTH, 6 * WIDTH, 6 * WIDTH + N_HEADS, 9 * WIDTH + N_HEADS
    wa = jnp.concatenate([w_in[:, c3:], w_in[:, :c0], w_in[:, c0:c1], w_in[:, c2:c3]], axis=1)
    wf = jnp.pad(w_in[:, c1:c2], ((0, 0), (0, LANES - N_HEADS)))
    return wa, wf


def _join_w_in(ga, gf):
    g, c, f, s = 3 * D_MODEL, CONV_OFF, FOX_OFF, SB_OFF
    return jnp.concatenate([ga[:, c:f], ga[:, f:s], gf[:, :N_HEADS], ga[:, s:], ga[:, :g]], axis=1)


def _layer_fwd(x, p):
    s = x.shape[0]
    proj, hn1 = _norm_mm(x, p["norm1_g"], p["wa"], BF16, name="in_proj")
    fraw, _ = _norm_mm(x, p["norm1_g"], p["wf"], F32, name="in_proj_f")
    ft = fraw[:, :N_HEADS].T
    crow8 = _fox_gates(ft, p["fox_f_bias"])
    crow = crow8.reshape(N_PAIR, 2, s)
    ccol = jnp.repeat(crow.transpose(0, 2, 1), HEAD_DIM, axis=2)
    oc = _conv_mix_fwd(proj, p["conv_w"])
    qn, kn = _fox_prep(proj, p["fox_q_norm_g"], p["fox_k_norm_g"])
    of, lse = _fox_fwd(qn, kn, proj, ccol, crow)
    osb = _sb_fwd(proj)
    xm, merged, yc, yf, ys = _merge_fwd(x, oc, of, osb, p["w_proj_conv"], p["w_proj_fox"], p["w_proj_sb"], proj,
                                        p["gate_bias"], p["w_out"])
    up, hn2 = _norm_mm(xm, p["norm2_g"], p["w_up"], BF16, name="up_proj")
    hh = _glu_fwd(up, p["ffn_conv_w"], p["ffn_conv_b"])
    xo = _mm(hh, p["w_down"], "nn", F32, res=xm, name="down_proj")
    saved = dict(x=x, hn1=hn1, proj=proj, ft=ft, crow=crow, ccol=ccol, oc=oc, qn=qn, kn=kn, of=of, lse=lse, osb=osb,
                 merged=merged, yc=yc, yf=yf, ys=ys, xm=xm, hn2=hn2, up=up, hh=hh)
    return xo, saved


def _layer_bwd(dx, p, a):
    s = dx.shape[0]
    g = {}
    dhh = _mm(dx, p["w_down"], "nt", BF16, tn=1408, name="d_down_in")
    g["w_down"] = _mm(a["hh"], dx, "tn", F32, name="g_w_down")
    dug, duv, g["ffn_conv_w"], g["ffn_conv_b"] = _glu_bwd(dhh, a["up"], p["ffn_conv_w"], p["ffn_conv_b"])
    dup = jnp.concatenate([dug, duv], axis=1)
    g["w_up"] = _mm(a["hn2"], dup, "tn", F32, tn=1408, name="g_w_up")
    dhn2 = _mm(dup, p["w_up"], "nt", F32, name="d_up_in")
    dx, g["norm2_g"] = _norm_bwd(dhn2, a["xm"], p["norm2_g"], dx, name="norm2_bwd")
    dm = _mm(dx, p["w_out"], "nt", BF16, name="d_out_in")
    g["w_out"] = _mm(a["merged"], dx, "tn", F32, tn=512, name="g_w_out")
    dyc, dyf, dys, dgl, g["gate_bias"] = _gate_bwd(dm, a["yc"], a["yf"], a["ys"], a["proj"], p["gate_bias"])
    doc = _mm(dyc, p["w_proj_conv"], "nt", BF16, name="d_pc_in")
    dof = _mm(dyf, p["w_proj_fox"], "nt", BF16, name="d_pf_in")
    dos = _mm(dys, p["w_proj_sb"], "nt", BF16, name="d_ps_in")
    g["w_proj_conv"] = _mm(a["oc"], dyc, "tn", F32, name="g_w_pc")
    g["w_proj_fox"] = _mm(a["of"], dyf, "tn", F32, name="g_w_pf")
    g["w_proj_sb"] = _mm(a["osb"], dys, "tn", F32, name="g_w_ps")
    dcb, dcc, dch, g["conv_w"] = _conv_mix_bwd(doc, a["proj"], p["conv_w"])
    delta_f = _delta_rep(dof, a["of"])
    dqs, dkn, dfv, dcrow, dcq = _fox_bwd(a["qn"], a["kn"], a["proj"], dof, a["lse"], delta_f, a["ccol"], a["crow"])
    dc = dcrow.reshape(N_HEADS, s) + dcq[:, :, ::HEAD_DIM].transpose(0, 2, 1).reshape(N_HEADS, s)
    dfq, dfk, g["fox_q_norm_g"], g["fox_k_norm_g"] = _fox_post(dqs, dkn, a["proj"], p["fox_q_norm_g"], p["fox_k_norm_g"])
    dft, g["fox_f_bias"] = _fox_gates_bwd(dc, a["ft"], p["fox_f_bias"])
    delta_s = _delta_rep(dos, a["osb"])
    dsq, dsk, dsv = _sb_bwd(a["proj"], dos, delta_s)
    dproj = jnp.concatenate([dgl, dcb, dcc, dch, dfq, dfk, dfv, dsq, dsk, dsv], axis=1)
    dfp = jnp.pad(dft.T, ((0, 0), (0, LANES - N_HEADS))).astype(BF16)
    ga = _mm(a["hn1"], dproj, "tn", F32, tn=768, name="g_w_in")
    gf = _mm(a["hn1"], dfp, "tn", F32, name="g_w_in_f")
    g["w_in"] = _join_w_in(ga, gf)
    dhn1 = _mm(dfp, p["wf"], "nt", F32, name="d_in_f")
    dhn1 = _mm(dproj, p["wa"], "nt", F32, res=dhn1, tk=1536, name="d_in")
    dx, g["norm1_g"] = _norm_bwd(dhn1, a["x"], p["norm1_g"], dx, name="norm1_bwd")
    return dx, g


MATMUL_WEIGHTS = ("w_in", "w_proj_conv", "w_proj_fox", "w_proj_sb", "w_out", "w_up", "w_down")
WEIGHTS = ("norm1_g", "w_in", "fox_f_bias", "gate_bias", "conv_w", "fox_q_norm_g", "fox_k_norm_g", "w_proj_conv",
           "w_proj_fox", "w_proj_sb", "w_out", "norm2_g", "w_up", "ffn_conv_w", "ffn_conv_b", "w_down")


def _local_step(x, target, layers, reduce_layer=lambda l, g: g):
    layers = [dict(p) for p in layers]
    for p in layers:
        p["wa"], p["wf"] = _split_w_in(p.pop("w_in"))
    saved = []
    for p in layers:
        x, a = _layer_fwd(x, p)
        saved.append(a)
    sq, dx = _loss_head(x, target)
    grads = [None] * len(layers)
    for l in reversed(range(len(layers))):
        dx, g = _layer_bwd(dx, layers[l], saved[l])
        grads[l] = reduce_layer(l, g)
    return sq, dx, grads


ANY = pl.BlockSpec(memory_space=pl.ANY)


def _place():
    x, y, c = lax.axis_index("x"), lax.axis_index("y"), lax.axis_index("c")
    chips = [(1 - x, y), (x, 1 - y), (1 - x, 1 - y)]
    return x, y, c, chips


def _all_gather_chips(shards, name):
    n = len(shards)

    def body(*refs):
        x_refs, out_refs = refs[:n], refs[n:2 * n]
        send_sems, recv_sems = refs[2 * n:]
        x, y, c, chips = _place()
        me = 2 * x + y
        sibling = (x, y, 1 - c)

        def copy(k, t, chip_index, which_half, to, from_input=False):
            half = shards[t].shape[0] // 2
            rows = pl.ds(which_half * half, half)
            dst = out_refs[t].at[chip_index, rows, :]
            return pltpu.make_async_remote_copy(
                src_ref=x_refs[t].at[rows, :] if from_input else dst, dst_ref=dst,
                send_sem=send_sems.at[k, t], recv_sem=recv_sems.at[k, t], device_id=to, device_id_type=MESH)

        first = [copy(p, t, me, c, (*chip, c), from_input=True) for t in range(n) for p, chip in enumerate(chips)]
        for cp in first:
            cp.start()
        passed = []
        for t in range(n):
            for p, chip in enumerate(chips):
                copy(p, t, 2 * chip[0] + chip[1], c, (x, y, c)).wait_recv()
                fwd = copy(3 + p, t, 2 * chip[0] + chip[1], c, sibling)
                fwd.start()
                passed.append(fwd)
        for t in range(n):
            for p, chip in enumerate(chips):
                copy(3 + p, t, 2 * chip[0] + chip[1], 1 - c, (x, y, c)).wait_recv()
        for cp in first + passed:
            cp.wait_send()

    got = pl.pallas_call(
        body, name=name, in_specs=[ANY] * n, out_specs=[ANY] * n,
        out_shape=[jax.ShapeDtypeStruct((4, *s.shape), s.dtype) for s in shards],
        scratch_shapes=[pltpu.SemaphoreType.DMA((6, n)), pltpu.SemaphoreType.DMA((6, n))],
    )(*shards)
    chip = 2 * lax.axis_index("x") + lax.axis_index("y")
    return [lax.dynamic_update_index_in_dim(g, s, chip, 0) for g, s in zip(got, shards)]


def _pair_exchange(gs):
    n = len(gs)

    def body(*refs):
        g_refs, recv_refs = refs[:n], refs[n:2 * n]
        send_sems, recv_sems = refs[2 * n:]
        x, y, c, _ = _place()
        cps = []
        for t in range(n):
            half = gs[t].shape[1] // 2
            cps.append(pltpu.make_async_remote_copy(
                src_ref=g_refs[t].at[:, pl.ds((1 - c) * half, half), :], dst_ref=recv_refs[t],
                send_sem=send_sems.at[t], recv_sem=recv_sems.at[t], device_id=(x, y, 1 - c), device_id_type=MESH))
        for cp in cps:
            cp.start()
        for cp in cps:
            cp.wait()

    return pl.pallas_call(
        body, name="rs_pair_exchange", in_specs=[ANY] * n, out_specs=[ANY] * n,
        out_shape=[jax.ShapeDtypeStruct((4, g.shape[1] // 2, g.shape[2]), g.dtype) for g in gs],
        scratch_shapes=[pltpu.SemaphoreType.DMA((n,)), pltpu.SemaphoreType.DMA((n,))],
    )(*gs)


def _pair_sum(g, recv, core, tr=256):
    n, r, cols = g.shape
    half = r // 2
    tr = _row_tile(half, tr)
    nb = half // tr

    def body(c_ref, g_ref, r_ref, o_ref):
        o_ref[...] = (g_ref[...] + r_ref[...]).astype(BF16)

    grid_spec = pltpu.PrefetchScalarGridSpec(
        num_scalar_prefetch=1, grid=(n, nb),
        in_specs=[pl.BlockSpec((None, tr, cols), lambda k, i, c: (k, c[0] * nb + i, 0)),
                  pl.BlockSpec((None, tr, cols), lambda k, i, c: (k, i, 0))],
        out_specs=pl.BlockSpec((None, tr, cols), lambda k, i, c: (k, i, 0)))
    return pl.pallas_call(
        body, name="rs_pair_sum", grid_spec=grid_spec, out_shape=jax.ShapeDtypeStruct((n, half, cols), BF16),
        compiler_params=_params(("parallel", "parallel")),
    )(core, g, recv)


def _row_tile(rows, pref):
    best = None
    for t in range(16, min(rows, pref) + 1, 16):
        if rows % t == 0:
            best = t
    assert best is not None, (rows, pref)
    return best


def _chip_exchange(s1s):
    n = len(s1s)

    def body(*refs):
        s_refs, recv_refs = refs[:n], refs[n:2 * n]
        send_sems, recv_sems = refs[2 * n:]
        x, y, c, chips = _place()
        cps = [pltpu.make_async_remote_copy(
            src_ref=s_refs[t].at[2 * chip[0] + chip[1]], dst_ref=recv_refs[t].at[p],
            send_sem=send_sems.at[p, t], recv_sem=recv_sems.at[p, t], device_id=(*chip, c), device_id_type=MESH)
            for t in range(n) for p, chip in enumerate(chips)]
        for cp in cps:
            cp.start()
        for cp in cps:
            cp.wait()

    return pl.pallas_call(
        body, name="rs_chip_exchange", in_specs=[ANY] * n, out_specs=[ANY] * n,
        out_shape=[jax.ShapeDtypeStruct((3, *s.shape[1:]), s.dtype) for s in s1s],
        scratch_shapes=[pltpu.SemaphoreType.DMA((3, n)), pltpu.SemaphoreType.DMA((3, n))],
    )(*s1s)


def _final_sum(g, recv_a, recv_b, core, chip, tr=256):
    n, r, cols = g.shape
    half = r // 2
    tr = _row_tile(half, tr)
    nb = half // tr

    def body(c_ref, k_ref, g_ref, a_ref, b0_ref, b1_ref, b2_ref, o_ref):
        total = g_ref[...] + a_ref[...]
        for b_ref in (b0_ref, b1_ref, b2_ref):
            total = total + b_ref[...].astype(F32)
        o_ref[...] = total

    rel = lambda p: pl.BlockSpec((None, tr, cols), lambda i, c, k, p=p: (p, i, 0))
    grid_spec = pltpu.PrefetchScalarGridSpec(
        num_scalar_prefetch=2, grid=(nb,),
        in_specs=[pl.BlockSpec((None, tr, cols), lambda i, c, k: (k[0], c[0] * nb + i, 0)),
                  pl.BlockSpec((None, tr, cols), lambda i, c, k: (k[0], i, 0)), rel(0), rel(1), rel(2)],
        out_specs=pl.BlockSpec((tr, cols), lambda i, c, k: (c[0] * nb + i, 0)))
    return pl.pallas_call(
        body, name="rs_final_sum", grid_spec=grid_spec, out_shape=jax.ShapeDtypeStruct((r, cols), F32),
        compiler_params=_params(("parallel",)),
    )(core, chip, g, recv_a, recv_b, recv_b, recv_b)


def _pair_join(fs):
    n = len(fs)

    def body(*refs):
        out_refs = refs[n:2 * n]
        send_sems, recv_sems = refs[2 * n:]
        x, y, c, _ = _place()

        def copy(t, which_half):
            h = fs[t].shape[0] // 2
            rows = out_refs[t].at[pl.ds(which_half * h, h), :]
            return pltpu.make_async_remote_copy(
                src_ref=rows, dst_ref=rows, send_sem=send_sems.at[t], recv_sem=recv_sems.at[t],
                device_id=(x, y, 1 - c), device_id_type=MESH)

        sends = [copy(t, c) for t in range(n)]
        for cp in sends:
            cp.start()
        for t in range(n):
            sends[t].wait_send()
            copy(t, 1 - c).wait_recv()

    return pl.pallas_call(
        body, name="rs_pair_join", in_specs=[ANY] * n, out_specs=[ANY] * n,
        out_shape=[jax.ShapeDtypeStruct(f.shape, f.dtype) for f in fs],
        input_output_aliases={t: t for t in range(n)},
        scratch_shapes=[pltpu.SemaphoreType.DMA((n,)), pltpu.SemaphoreType.DMA((n,))],
    )(*fs)


def _reduce_scatter_chips(gs):
    core = lax.axis_index("c").astype(jnp.int32).reshape(1)
    chip = (2 * lax.axis_index("x") + lax.axis_index("y")).astype(jnp.int32).reshape(1)
    recv_a = _pair_exchange(gs)
    s1 = [_pair_sum(g, ra, core) for g, ra in zip(gs, recv_a)]
    recv_b = _chip_exchange(s1)
    return _pair_join([_final_sum(g, ra, rb, core, chip) for g, ra, rb in zip(gs, recv_a, recv_b)])


def _all_reduce_small(v):
    r, cols = v.shape

    def body(v_ref, out_ref, buf_ref, send_sems, recv_sems):
        x, y, c, _ = _place()
        flip = lambda a, bit: 1 - a if bit else a
        buf_ref[4 * x + 2 * y + c] = v_ref[...]
        cps = []
        for rel in range(1, 8):
            peer = (flip(x, rel & 4), flip(y, rel & 2), flip(c, rel & 1))
            cps.append(pltpu.make_async_remote_copy(
                src_ref=v_ref, dst_ref=buf_ref.at[4 * x + 2 * y + c], send_sem=send_sems.at[rel - 1], recv_sem=recv_sems.at[rel - 1],
                device_id=peer, device_id_type=MESH))
        for cp in cps:
            cp.start()
        for cp in cps:
            cp.wait()
        total = buf_ref[0]
        for d in range(1, 8):
            total = total + buf_ref[d]
        out_ref[...] = total

    vm = pl.BlockSpec(memory_space=pltpu.VMEM)
    return pl.pallas_call(
        body, name="all_reduce_small", in_specs=[vm], out_specs=vm, out_shape=jax.ShapeDtypeStruct((r, cols), F32),
        scratch_shapes=[pltpu.VMEM((8, r, cols), F32), pltpu.SemaphoreType.DMA((7,)), pltpu.SemaphoreType.DMA((7,))],
    )(v)


SHARD_AXIS = {"w_in": 1, "conv_w": 1, "w_proj_conv": 1, "w_proj_fox": 1, "w_proj_sb": 1, "w_out": 0, "w_up": 1,
              "ffn_conv_w": 1, "w_down": 0}
SMALL_SHARDED = ("conv_w", "ffn_conv_w")
BIG = tuple(k for k in SHARD_AXIS if k not in SMALL_SHARDED)
REPLICATED = tuple(k for k in WEIGHTS if k not in SHARD_AXIS)
SMALL = REPLICATED + SMALL_SHARDED


def _pack_small(parts, row_align):
    flat = jnp.concatenate([p.reshape(-1) for p in parts])
    rows = -(-flat.shape[0] // (PACK_COLS * row_align)) * row_align
    return jnp.pad(flat, (0, rows * PACK_COLS - flat.shape[0])).reshape(rows, PACK_COLS)


def _unpack_small(packed, shapes):
    flat = packed.reshape(-1)
    out, off = [], 0
    for shape in shapes:
        size = int(np.prod(shape))
        out.append(flat[off:off + size].reshape(shape))
        off += size
    return out


def _to_full(stacked, axis):
    _, r, c = stacked.shape
    if axis == 0:
        return stacked.reshape(4 * r, c)
    return jnp.moveaxis(stacked, 0, 1).reshape(r, 4 * c)


def _to_chips(full, axis):
    a, b = full.shape
    if axis == 0:
        return full.reshape(4, a // 4, b)
    return jnp.moveaxis(full.reshape(a, 4, b // 4), 1, 0)


def kernel(x, norm1_g, w_in, fox_f_bias, gate_bias, conv_w, fox_q_norm_g, fox_k_norm_g, w_proj_conv, w_proj_fox, w_proj_sb, w_out, norm2_g, w_up, ffn_conv_w, ffn_conv_b, w_down, loss_target, m_norm1_g, m_w_in, m_fox_f_bias, m_gate_bias, m_conv_w, m_fox_q_norm_g, m_fox_k_norm_g, m_w_proj_conv, m_w_proj_fox, m_w_proj_sb, m_w_out, m_norm2_g, m_w_up, m_ffn_conv_w, m_ffn_conv_b, m_w_down, v_norm1_g, v_w_in, v_fox_f_bias, v_gate_bias, v_conv_w, v_fox_q_norm_g, v_fox_k_norm_g, v_w_proj_conv, v_w_proj_fox, v_w_proj_sb, v_w_out, v_norm2_g, v_w_up, v_ffn_conv_w, v_ffn_conv_b, v_w_down):
    given = dict(x=x, norm1_g=norm1_g, w_in=w_in, fox_f_bias=fox_f_bias, gate_bias=gate_bias, conv_w=conv_w, fox_q_norm_g=fox_q_norm_g, fox_k_norm_g=fox_k_norm_g, w_proj_conv=w_proj_conv, w_proj_fox=w_proj_fox, w_proj_sb=w_proj_sb, w_out=w_out, norm2_g=norm2_g, w_up=w_up, ffn_conv_w=ffn_conv_w, ffn_conv_b=ffn_conv_b, w_down=w_down, loss_target=loss_target, m_norm1_g=m_norm1_g, m_w_in=m_w_in, m_fox_f_bias=m_fox_f_bias, m_gate_bias=m_gate_bias, m_conv_w=m_conv_w, m_fox_q_norm_g=m_fox_q_norm_g, m_fox_k_norm_g=m_fox_k_norm_g, m_w_proj_conv=m_w_proj_conv, m_w_proj_fox=m_w_proj_fox, m_w_proj_sb=m_w_proj_sb, m_w_out=m_w_out, m_norm2_g=m_norm2_g, m_w_up=m_w_up, m_ffn_conv_w=m_ffn_conv_w, m_ffn_conv_b=m_ffn_conv_b, m_w_down=m_w_down, v_norm1_g=v_norm1_g, v_w_in=v_w_in, v_fox_f_bias=v_fox_f_bias, v_gate_bias=v_gate_bias, v_conv_w=v_conv_w, v_fox_q_norm_g=v_fox_q_norm_g, v_fox_k_norm_g=v_fox_k_norm_g, v_w_proj_conv=v_w_proj_conv, v_w_proj_fox=v_w_proj_fox, v_w_proj_sb=v_w_proj_sb, v_w_out=v_w_out, v_norm2_g=v_norm2_g, v_w_up=v_w_up, v_ffn_conv_w=v_ffn_conv_w, v_ffn_conv_b=v_ffn_conv_b, v_w_down=v_w_down)
    depth = given["norm1_g"].shape[0]
    chip = 2 * lax.axis_index("x") + lax.axis_index("y")

    layers = []
    for l in range(depth):
        got = _all_gather_chips([given[k][l].astype(BF16) for k in BIG], "gather_weights")
        p = {k: given[k][l] for k in REPLICATED}
        p.update({k: _to_full(stacked, SHARD_AXIS[k]) for k, stacked in zip(BIG, got)})
        layers.append(p)
    conv_shapes = [given[k].shape for k in SMALL_SHARDED]
    (got_conv,) = _all_gather_chips([_pack_small([given[k] for k in SMALL_SHARDED], 16)], "gather_conv_weights")
    by_chip = [_unpack_small(got_conv[j], conv_shapes) for j in range(4)]
    for i, k in enumerate(SMALL_SHARDED):
        whole = jnp.concatenate([by_chip[j][i] for j in range(4)], axis=-1)
        for l in range(depth):
            layers[l][k] = whole[l]

    def reduce_layer(l, g):
        reduced = _reduce_scatter_chips([_to_chips(g[k], SHARD_AXIS[k]) for k in BIG])
        out = {k: g[k] for k in SMALL}
        out.update(zip(BIG, reduced))
        return out

    sq, dx, grads = _local_step(given["x"][0], given["loss_target"][0], layers, reduce_layer)
    loss = lax.psum(0.5 * sq[0, 0] / D_MODEL, ("x", "y", "c"))

    gsum = {k: jnp.stack([g[k] for g in grads]) for k in BIG}
    small_shapes = [(depth, *grads[0][k].shape) for k in SMALL]
    summed = _all_reduce_small(_pack_small([jnp.stack([g[k] for g in grads]) for k in SMALL], 8))
    for k, total in zip(SMALL, _unpack_small(summed, small_shapes)):
        if k in SMALL_SHARDED:
            total = lax.dynamic_index_in_dim(total.reshape(*total.shape[:-1], 4, total.shape[-1] // 4), chip, axis=2, keepdims=False)
        gsum[k] = total

    deltas, new_m, new_v = {}, {}, {}
    for k in WEIGHTS:
        deltas[k], new_m[k], new_v[k] = _adamw(given[k], gsum[k], given["m_" + k], given["v_" + k], "adamw_" + k)
    return (loss, dx[None], *[gsum[k] for k in WEIGHTS], *[deltas[k] for k in WEIGHTS],
            *[new_m[k] for k in WEIGHTS], *[new_v[k] for k in WEIGHTS])
```

```python
import collections

import numpy as np
import jax
import jax.numpy as jnp
from jax import lax
from jax.experimental import pallas as pl
from jax.experimental.pallas import tpu as pltpu

F32 = jnp.float32
BF16 = jnp.bfloat16

D_MODEL = 1024
DEPTH = 4
HEAD_DIM = 64
N_HEADS = 8
WIDTH = 512
D_FF = 2816
NORM_EPS = 1e-6
NEG_INF = -1e30
QK_SCALE = HEAD_DIM ** -0.5
LANES = 128
N_PAIR = N_HEADS // 2

GATE_OFF = 0
CONV_OFF = 3 * D_MODEL
FOX_OFF = CONV_OFF + 3 * WIDTH
SB_OFF = FOX_OFF + 3 * WIDTH
D_INA = SB_OFF + 3 * WIDTH

ADAM_LR = 0.001
ADAM_B1 = 0.9
ADAM_B2 = 0.999
ADAM_EPS = 1e-08
ADAM_WD = 0.01
ADAM_STEP = 10

VMEM_LIMIT = 48 * 1024 * 1024
ADAMW_BLOCK_BYTES = 1024 * 1024

PACK_COLS = 1024
PACK_ROW_ALIGN = 32
MESH = pl.DeviceIdType.MESH


def _params(sem):
    return pltpu.CompilerParams(dimension_semantics=sem, vmem_limit_bytes=VMEM_LIMIT)


def _dot(a, b, dims):
    return lax.dot_general(a, b, (dims, ((), ())), preferred_element_type=F32)


_NN = ((1,), (0,))
_NT = ((1,), (1,))
_TN = ((0,), (0,))


def _pick(dim, pref):
    if dim <= pref:
        return dim
    best = None
    for mult in range(1, dim // LANES + 1):
        t = mult * LANES
        if t <= pref and dim % t == 0:
            best = t
    assert best is not None, (dim, pref)
    return best


def _mm(a, b, mode, out_dtype=F32, tm=1024, tn=1024, tk=2048, res=None, name="mm"):
    if mode == "nn":
        (m, k), (_, n) = a.shape, b.shape
    elif mode == "nt":
        (m, k), (n, _) = a.shape, b.shape
    else:
        (k, m), (_, n) = a.shape, b.shape
    tm, tn, tk = _pick(m, tm), _pick(n, tn), _pick(k, tk)
    nk = k // tk
    dims = {"nn": _NN, "nt": _NT, "tn": _TN}[mode]
    if mode == "tn":
        a_spec = pl.BlockSpec((tk, tm), lambda i, j, kk: (kk, i))
    else:
        a_spec = pl.BlockSpec((tm, tk), lambda i, j, kk: (i, kk))
    if mode == "nt":
        b_spec = pl.BlockSpec((tn, tk), lambda i, j, kk: (j, kk))
    else:
        b_spec = pl.BlockSpec((tk, tn), lambda i, j, kk: (kk, j))
    o_spec = pl.BlockSpec((tm, tn), lambda i, j, kk: (i, j))
    in_specs = [a_spec, b_spec] + ([o_spec] if res is not None else [])

    def body(*refs):
        if res is not None:
            a_ref, b_ref, r_ref, o_ref, acc_ref = refs
        else:
            a_ref, b_ref, o_ref, acc_ref = refs
            r_ref = None
        kk = pl.program_id(2)
        part = _dot(a_ref[...].astype(BF16), b_ref[...].astype(BF16), dims)

        def finish(total):
            if r_ref is not None:
                total = total + r_ref[...].astype(F32)
            o_ref[...] = total.astype(out_dtype)

        if nk == 1:
            finish(part)
        else:
            @pl.when(kk == 0)
            def _():
                acc_ref[...] = part

            @pl.when(kk > 0)
            def _():
                acc_ref[...] += part

            @pl.when(kk == nk - 1)
            def _():
                finish(acc_ref[...])

    args = (a, b) + ((res,) if res is not None else ())
    return pl.pallas_call(
        body, name=name, grid=(m // tm, n // tn, nk), in_specs=in_specs, out_specs=o_spec,
        out_shape=jax.ShapeDtypeStruct((m, n), out_dtype),
        scratch_shapes=[pltpu.VMEM((tm, tn) if nk > 1 else (8, LANES), F32)],
        compiler_params=_params(("parallel", "parallel", "arbitrary")),
    )(*args)


def _norm_mm(x, g, w, out_dtype, tm=1024, tn=1536, name="norm_mm"):
    m, d = x.shape
    n = w.shape[1]
    tm, tn = _pick(m, tm), _pick(n, tn)

    def body(x_ref, g_ref, w_ref, o_ref, hn_ref):
        @pl.when(pl.program_id(1) == 0)
        def _():
            xf = x_ref[...]
            r = lax.rsqrt(jnp.mean(xf * xf, axis=-1, keepdims=True) + NORM_EPS)
            hn_ref[...] = (xf * r * g_ref[...]).astype(BF16)

        o_ref[...] = _dot(hn_ref[...], w_ref[...], _NN).astype(out_dtype)

    return pl.pallas_call(
        body, name=name, grid=(m // tm, n // tn),
        in_specs=[pl.BlockSpec((tm, d), lambda i, j: (i, 0)),
                  pl.BlockSpec((1, d), lambda i, j: (0, 0)),
                  pl.BlockSpec((d, tn), lambda i, j: (0, j))],
        out_specs=[pl.BlockSpec((tm, tn), lambda i, j: (i, j)),
                   pl.BlockSpec((tm, d), lambda i, j: (i, 0))],
        out_shape=[jax.ShapeDtypeStruct((m, n), out_dtype), jax.ShapeDtypeStruct((m, d), BF16)],
        compiler_params=_params(("parallel", "arbitrary")),
    )(x, g.reshape(1, d), w)


def _norm_bwd(dhn, x, g, dx_in, tm=256, name="norm_bwd"):
    m, d = x.shape
    tm = min(tm, m)

    def body(dhn_ref, x_ref, g_ref, dxi_ref, dx_ref, gg_ref):
        xf = x_ref[...]
        r = lax.rsqrt(jnp.mean(xf * xf, axis=-1, keepdims=True) + NORM_EPS)
        xhat = xf * r
        dh = dhn_ref[...].astype(F32)
        dxn = dh * g_ref[...]
        mean = jnp.mean(dxn * xhat, axis=-1, keepdims=True)
        dx_ref[...] = dxi_ref[...] + r * (dxn - xhat * mean)
        part = jnp.sum(dh * xhat, axis=0, keepdims=True)

        @pl.when(pl.program_id(0) == 0)
        def _():
            gg_ref[...] = part

        @pl.when(pl.program_id(0) > 0)
        def _():
            gg_ref[...] += part

    row = pl.BlockSpec((tm, d), lambda i: (i, 0))
    vec = pl.BlockSpec((1, d), lambda i: (0, 0))
    dx, gg = pl.pallas_call(
        body, name=name, grid=(m // tm,), in_specs=[row, row, vec, row], out_specs=[row, vec],
        out_shape=[jax.ShapeDtypeStruct((m, d), F32), jax.ShapeDtypeStruct((1, d), F32)],
        compiler_params=_params(("arbitrary",)),
    )(dhn, x, g.reshape(1, d), dx_in)
    return dx, gg.reshape(d)


def _down(u, k):
    s = u.shape[0]
    rows = lax.broadcasted_iota(jnp.int32, u.shape, 0)
    return jnp.where(rows < k, 0.0, pltpu.roll(u, k, axis=0))


def _up(u, k):
    s = u.shape[0]
    rows = lax.broadcasted_iota(jnp.int32, u.shape, 0)
    return jnp.where(rows >= s - k, 0.0, pltpu.roll(u, s - k, axis=0))


def _conv_mix_fwd(proj, conv_w, tc=128):
    s = proj.shape[0]
    nb = WIDTH // tc
    off = CONV_OFF // tc

    def body(b_ref, c_ref, h_ref, w_ref, o_ref):
        u = c_ref[...].astype(F32) * h_ref[...].astype(F32)
        w = w_ref[...]
        cv = w[0:1] * _down(u, 2) + w[1:2] * _down(u, 1) + w[2:3] * u
        o_ref[...] = (b_ref[...].astype(F32) * cv).astype(BF16)

    col = lambda k: pl.BlockSpec((s, tc), lambda j, k=k: (0, off + k * nb + j))
    return pl.pallas_call(
        body, name="conv_mix_fwd", grid=(nb,),
        in_specs=[col(0), col(1), col(2), pl.BlockSpec((3, tc), lambda j: (0, j))],
        out_specs=pl.BlockSpec((s, tc), lambda j: (0, j)),
        out_shape=jax.ShapeDtypeStruct((s, WIDTH), BF16),
        compiler_params=_params(("parallel",)),
    )(proj, proj, proj, conv_w)


def _conv_mix_bwd(do, proj, conv_w, tc=128):
    s = proj.shape[0]
    nb = WIDTH // tc
    off = CONV_OFF // tc

    def body(do_ref, b_ref, c_ref, h_ref, w_ref, db_ref, dc_ref, dh_ref, gw_ref):
        b = b_ref[...].astype(F32)
        c = c_ref[...].astype(F32)
        h = h_ref[...].astype(F32)
        g = do_ref[...].astype(F32)
        w = w_ref[...]
        u = c * h
        u1, u2 = _down(u, 1), _down(u, 2)
        cv = w[0:1] * u2 + w[1:2] * u1 + w[2:3] * u
        db_ref[...] = (g * cv).astype(BF16)
        dcv = g * b
        gw_ref[0:1, :] = jnp.sum(dcv * u2, axis=0, keepdims=True)
        gw_ref[1:2, :] = jnp.sum(dcv * u1, axis=0, keepdims=True)
        gw_ref[2:3, :] = jnp.sum(dcv * u, axis=0, keepdims=True)
        du = w[2:3] * dcv + w[1:2] * _up(dcv, 1) + w[0:1] * _up(dcv, 2)
        dc_ref[...] = (du * h).astype(BF16)
        dh_ref[...] = (du * c).astype(BF16)

    col = lambda k: pl.BlockSpec((s, tc), lambda j, k=k: (0, off + k * nb + j))
    own = pl.BlockSpec((s, tc), lambda j: (0, j))
    wsp = pl.BlockSpec((3, tc), lambda j: (0, j))
    act = jax.ShapeDtypeStruct((s, WIDTH), BF16)
    return pl.pallas_call(
        body, name="conv_mix_bwd", grid=(nb,),
        in_specs=[own, col(0), col(1), col(2), wsp], out_specs=[own, own, own, wsp],
        out_shape=[act, act, act, jax.ShapeDtypeStruct((3, WIDTH), F32)],
        compiler_params=_params(("parallel",)),
    )(do, proj, proj, proj, conv_w)


def _glu_fwd(up, w, b, tc=256):
    s = up.shape[0]
    nb = D_FF // tc

    def body(g_ref, v_ref, w_ref, b_ref, o_ref):
        ug = g_ref[...].astype(F32)
        wv = w_ref[...]
        cg = wv[0:1] * _down(ug, 2) + wv[1:2] * _down(ug, 1) + wv[2:3] * ug + b_ref[...]
        act = cg * jax.nn.sigmoid(cg)
        o_ref[...] = (act * v_ref[...].astype(F32)).astype(BF16)

    return pl.pallas_call(
        body, name="glu_fwd", grid=(nb,),
        in_specs=[pl.BlockSpec((s, tc), lambda j: (0, j)), pl.BlockSpec((s, tc), lambda j: (0, nb + j)),
                  pl.BlockSpec((3, tc), lambda j: (0, j)), pl.BlockSpec((1, tc), lambda j: (0, j))],
        out_specs=pl.BlockSpec((s, tc), lambda j: (0, j)),
        out_shape=jax.ShapeDtypeStruct((s, D_FF), BF16),
        compiler_params=_params(("parallel",)),
    )(up, up, w, b.reshape(1, D_FF))


def _glu_bwd(dh, up, w, b, tc=256):
    s = up.shape[0]
    nb = D_FF // tc

    def body(dh_ref, g_ref, v_ref, w_ref, b_ref, dg_ref, dv_ref, gw_ref, gb_ref):
        ug = g_ref[...].astype(F32)
        uv = v_ref[...].astype(F32)
        d = dh_ref[...].astype(F32)
        wv = w_ref[...]
        u1, u2 = _down(ug, 1), _down(ug, 2)
        cg = wv[0:1] * u2 + wv[1:2] * u1 + wv[2:3] * ug + b_ref[...]
        sg = jax.nn.sigmoid(cg)
        dv_ref[...] = (d * (cg * sg)).astype(BF16)
        dcg = d * uv * (sg * (1.0 + cg * (1.0 - sg)))
        gb_ref[...] = jnp.sum(dcg, axis=0, keepdims=True)
        gw_ref[0:1, :] = jnp.sum(dcg * u2, axis=0, keepdims=True)
        gw_ref[1:2, :] = jnp.sum(dcg * u1, axis=0, keepdims=True)
        gw_ref[2:3, :] = jnp.sum(dcg * ug, axis=0, keepdims=True)
        dg_ref[...] = (wv[2:3] * dcg + wv[1:2] * _up(dcg, 1) + wv[0:1] * _up(dcg, 2)).astype(BF16)

    own = pl.BlockSpec((s, tc), lambda j: (0, j))
    wsp = pl.BlockSpec((3, tc), lambda j: (0, j))
    bsp = pl.BlockSpec((1, tc), lambda j: (0, j))
    act = jax.ShapeDtypeStruct((s, D_FF), BF16)
    dg, dv, gw, gb = pl.pallas_call(
        body, name="glu_bwd", grid=(nb,),
        in_specs=[own, own, pl.BlockSpec((s, tc), lambda j: (0, nb + j)), wsp, bsp],
        out_specs=[own, own, wsp, bsp],
        out_shape=[act, act, jax.ShapeDtypeStruct((3, D_FF), F32), jax.ShapeDtypeStruct((1, D_FF), F32)],
        compiler_params=_params(("parallel",)),
    )(dh, up, up, w, b.reshape(1, D_FF))
    return dg, dv, gw, gb.reshape(D_FF)


def _merge_fwd(x, oc, of, osb, wpc, wpf, wps, proj, gb, wout, tm=256):
    s, d = x.shape
    tm = min(tm, s)

    def body(x_ref, oc_ref, of_ref, os_ref, wpc_ref, wpf_ref, wps_ref, g0_ref, g1_ref, g2_ref, gb_ref, wo_ref,
             xo_ref, mg_ref, yc_ref, yf_ref, ys_ref):
        merged = jnp.zeros((tm, d), F32)
        for k, (o_ref, w_ref, g_ref, y_ref) in enumerate(
                ((oc_ref, wpc_ref, g0_ref, yc_ref), (of_ref, wpf_ref, g1_ref, yf_ref), (os_ref, wps_ref, g2_ref, ys_ref))):
            y = _dot(o_ref[...].astype(BF16), w_ref[...], _NN)
            y_ref[...] = y.astype(BF16)
            gate = jax.nn.sigmoid(g_ref[...].astype(F32) + gb_ref[:, k * d:(k + 1) * d])
            merged = merged + gate * y
        mb = merged.astype(BF16)
        mg_ref[...] = mb
        xo_ref[...] = x_ref[...] + _dot(mb, wo_ref[...], _NN)

    rowd = pl.BlockSpec((tm, d), lambda i: (i, 0))
    roww = pl.BlockSpec((tm, WIDTH), lambda i: (i, 0))
    wp = pl.BlockSpec((WIDTH, d), lambda i: (0, 0))
    gcol = lambda k: pl.BlockSpec((tm, d), lambda i, k=k: (i, GATE_OFF // d + k))
    actd = jax.ShapeDtypeStruct((s, d), BF16)
    return pl.pallas_call(
        body, name="merge_fwd", grid=(s // tm,),
        in_specs=[rowd, roww, roww, roww, wp, wp, wp, gcol(0), gcol(1), gcol(2),
                  pl.BlockSpec((1, 3 * d), lambda i: (0, 0)), pl.BlockSpec((d, d), lambda i: (0, 0))],
        out_specs=[rowd, rowd, rowd, rowd, rowd],
        out_shape=[jax.ShapeDtypeStruct((s, d), F32), actd, actd, actd, actd],
        compiler_params=_params(("parallel",)),
    )(x, oc, of, osb, wpc, wpf, wps, proj, proj, proj, gb.reshape(1, 3 * d), wout)


def _gate_bwd(dm, yc, yf, ys, proj, gb, tm=256):
    s, d = dm.shape
    tm = min(tm, s)

    def body(dm_ref, yc_ref, yf_ref, ys_ref, g0_ref, g1_ref, g2_ref, gb_ref, dyc_ref, dyf_ref, dys_ref, dgl_ref, ggb_ref):
        g = dm_ref[...].astype(F32)
        parts = []
        for k, (y_ref, g_ref, dy_ref) in enumerate(((yc_ref, g0_ref, dyc_ref), (yf_ref, g1_ref, dyf_ref), (ys_ref, g2_ref, dys_ref))):
            gate = jax.nn.sigmoid(g_ref[...].astype(F32) + gb_ref[:, k * d:(k + 1) * d])
            dy_ref[...] = (g * gate).astype(BF16)
            dgl = g * y_ref[...].astype(F32) * gate * (1.0 - gate)
            dgl_ref[:, k * d:(k + 1) * d] = dgl.astype(BF16)
            parts.append(jnp.sum(dgl, axis=0, keepdims=True))
        part = jnp.concatenate(parts, axis=1)

        @pl.when(pl.program_id(0) == 0)
        def _():
            ggb_ref[...] = part

        @pl.when(pl.program_id(0) > 0)
        def _():
            ggb_ref[...] += part

    rowd = pl.BlockSpec((tm, d), lambda i: (i, 0))
    gcol = lambda k: pl.BlockSpec((tm, d), lambda i, k=k: (i, GATE_OFF // d + k))
    vec = pl.BlockSpec((1, 3 * d), lambda i: (0, 0))
    actd = jax.ShapeDtypeStruct((s, d), BF16)
    dyc, dyf, dys, dgl, ggb = pl.pallas_call(
        body, name="gate_bwd", grid=(s // tm,),
        in_specs=[rowd, rowd, rowd, rowd, gcol(0), gcol(1), gcol(2), vec],
        out_specs=[rowd, rowd, rowd, pl.BlockSpec((tm, 3 * d), lambda i: (i, 0)), vec],
        out_shape=[actd, actd, actd, jax.ShapeDtypeStruct((s, 3 * d), BF16), jax.ShapeDtypeStruct((1, 3 * d), F32)],
        compiler_params=_params(("arbitrary",)),
    )(dm, yc, yf, ys, proj, proj, proj, gb.reshape(1, 3 * d))
    return dyc, dyf, dys, dgl, ggb.reshape(3 * d)


def _tri_tables(nq, r, order):
    last = lambda qi: (qi + 1) * r - 1
    if order == "k_outer":
        pairs = [(qi, kj) for kj in range(nq * r) for qi in range(kj // r, nq)]
    elif order == "k_desc":
        pairs = [(qi, kj) for qi in range(nq) for kj in range(last(qi), -1, -1)]
    else:
        pairs = [(qi, kj) for qi in range(nq) for kj in range(last(qi) + 1)]
    qs, ks = zip(*pairs)
    return jnp.asarray(np.array(qs, np.int32)), jnp.asarray(np.array(ks, np.int32)), len(pairs)


def _lo_mask(shape):
    return lax.broadcasted_iota(jnp.int32, shape, len(shape) - 1) < HEAD_DIM


def _head(x, h):
    lo = _lo_mask(x.shape)
    return jnp.where(lo if h == 0 else jnp.logical_not(lo), x, jnp.zeros_like(x))


def _pair_cols(x, h):
    return x[:, h * HEAD_DIM:h * HEAD_DIM + 1]


def _rep(a0, a1, shape):
    return jnp.where(_lo_mask(shape), a0, a1)


def _positions(qi, kj, tq, tk):
    row = qi * tq + lax.broadcasted_iota(jnp.int32, (tq, tk), 0)
    col = kj * tk + lax.broadcasted_iota(jnp.int32, (tq, tk), 1)
    return row, col


def _head_norm(x, g):
    lo = _lo_mask(x.shape)
    sq = x * x
    s0 = jnp.sum(jnp.where(lo, sq, 0.0), axis=-1, keepdims=True)
    s1 = jnp.sum(jnp.where(lo, 0.0, sq), axis=-1, keepdims=True)
    r = jnp.where(lo, lax.rsqrt(s0 / HEAD_DIM + NORM_EPS), lax.rsqrt(s1 / HEAD_DIM + NORM_EPS))
    return x * r, r


def _fox_prep(proj, gq, gk, tm=512):
    s = proj.shape[0]
    tm = min(tm, s)
    off = FOX_OFF // LANES

    def body(q_ref, k_ref, gq_ref, gk_ref, qn_ref, kn_ref):
        qh, _ = _head_norm(q_ref[...].astype(F32), None)
        kh, _ = _head_norm(k_ref[...].astype(F32), None)
        qn_ref[...] = (qh * gq_ref[...] * QK_SCALE).astype(BF16)
        kn_ref[...] = (kh * gk_ref[...]).astype(BF16)

    vec = pl.BlockSpec((1, LANES), lambda p, i: (0, 0))
    own = pl.BlockSpec((tm, LANES), lambda p, i: (i, p))
    act = jax.ShapeDtypeStruct((s, WIDTH), BF16)
    return pl.pallas_call(
        body, name="fox_prep", grid=(N_PAIR, s // tm),
        in_specs=[pl.BlockSpec((tm, LANES), lambda p, i: (i, off + p)),
                  pl.BlockSpec((tm, LANES), lambda p, i: (i, off + N_PAIR + p)), vec, vec],
        out_specs=[own, own], out_shape=[act, act],
        compiler_params=_params(("parallel", "parallel")),
    )(proj, proj, jnp.tile(gq, 2).reshape(1, LANES), jnp.tile(gk, 2).reshape(1, LANES))


def _fox_post(dqs, dkn, proj, gq, gk, tm=512):
    s = proj.shape[0]
    tm = min(tm, s)
    off = FOX_OFF // LANES

    def one(d_ref, x_ref, g_ref, scale, dx_ref, gg_ref, first):
        xhat, r = _head_norm(x_ref[...].astype(F32), None)
        dy = d_ref[...] * scale
        part = jnp.sum(dy * xhat, axis=0, keepdims=True)

        @pl.when(first)
        def _():
            gg_ref[...] = part

        @pl.when(jnp.logical_not(first))
        def _():
            gg_ref[...] += part

        dxh = dy * g_ref[...]
        lo = _lo_mask(dxh.shape)
        pr = dxh * xhat
        m0 = jnp.sum(jnp.where(lo, pr, 0.0), axis=-1, keepdims=True)
        m1 = jnp.sum(jnp.where(lo, 0.0, pr), axis=-1, keepdims=True)
        mean = jnp.where(lo, m0, m1) / HEAD_DIM
        dx_ref[...] = (r * (dxh - xhat * mean)).astype(BF16)

    def body(dq_ref, dk_ref, q_ref, k_ref, gq_ref, gk_ref, dxq_ref, dxk_ref, ggq_ref, ggk_ref):
        first = pl.program_id(1) == 0
        one(dq_ref, q_ref, gq_ref, QK_SCALE, dxq_ref, ggq_ref, first)
        one(dk_ref, k_ref, gk_ref, 1.0, dxk_ref, ggk_ref, first)

    vec = pl.BlockSpec((1, LANES), lambda p, i: (0, 0))
    own = pl.BlockSpec((tm, LANES), lambda p, i: (i, p))
    ggs = pl.BlockSpec((None, 1, LANES), lambda p, i: (p, 0, 0))
    act = jax.ShapeDtypeStruct((s, WIDTH), BF16)
    ggo = jax.ShapeDtypeStruct((N_PAIR, 1, LANES), F32)
    dxq, dxk, ggq, ggk = pl.pallas_call(
        body, name="fox_post", grid=(N_PAIR, s // tm),
        in_specs=[own, own, pl.BlockSpec((tm, LANES), lambda p, i: (i, off + p)),
                  pl.BlockSpec((tm, LANES), lambda p, i: (i, off + N_PAIR + p)), vec, vec],
        out_specs=[own, own, ggs, ggs], out_shape=[act, act, ggo, ggo],
        compiler_params=_params(("parallel", "arbitrary")),
    )(dqs, dkn, proj, proj, jnp.tile(gq, 2).reshape(1, LANES), jnp.tile(gk, 2).reshape(1, LANES))
    fold = lambda a: a.reshape(N_HEADS, HEAD_DIM).sum(axis=0)
    return dxq, dxk, fold(ggq), fold(ggk)


def _split3(x):
    a = x.astype(BF16)
    r = x - a.astype(F32)
    b = r.astype(BF16)
    c = (r - b.astype(F32)).astype(BF16)
    return a, b, c


def _split2(x):
    a = x.astype(BF16)
    b = (x - a.astype(F32)).astype(BF16)
    return a, b


def _log_sigmoid(x):
    return jnp.minimum(x, 0.0) - jnp.log(1.0 + jnp.exp(-jnp.abs(x)))


def _fox_gates(ft, bias):
    h, s = ft.shape
    nb = s // LANES

    def body(f_ref, b_ref, c_ref):
        lf = _log_sigmoid(f_ref[...] + b_ref[...])
        i = lax.broadcasted_iota(jnp.int32, (s, LANES), 0)
        j = pl.program_id(0) * LANES + lax.broadcasted_iota(jnp.int32, (s, LANES), 1)
        tri = jnp.where(i <= j, 1.0, 0.0).astype(BF16)
        c_ref[...] = sum(_dot(p, tri, _NN) for p in _split3(lf))

    return pl.pallas_call(
        body, name="fox_gates", grid=(nb,),
        in_specs=[pl.BlockSpec((h, s), lambda j: (0, 0)), pl.BlockSpec((h, 1), lambda j: (0, 0))],
        out_specs=pl.BlockSpec((h, LANES), lambda j: (0, j)),
        out_shape=jax.ShapeDtypeStruct((h, s), F32),
        compiler_params=_params(("parallel",)),
    )(ft, bias.reshape(h, 1))


def _fox_gates_bwd(dc, ft, bias):
    h, s = ft.shape
    nb = s // LANES

    def body(dc_ref, f_ref, fb_ref, b_ref, df_ref, gb_ref):
        i = lax.broadcasted_iota(jnp.int32, (s, LANES), 0)
        j = pl.program_id(0) * LANES + lax.broadcasted_iota(jnp.int32, (s, LANES), 1)
        tri = jnp.where(i >= j, 1.0, 0.0).astype(BF16)
        dlf = sum(_dot(p, tri, _NN) for p in _split3(dc_ref[...]))
        df = dlf * jax.nn.sigmoid(-(fb_ref[...] + b_ref[...]))
        df_ref[...] = df
        part = jnp.sum(df, axis=-1, keepdims=True)

        @pl.when(pl.program_id(0) == 0)
        def _():
            gb_ref[...] = part

        @pl.when(pl.program_id(0) > 0)
        def _():
            gb_ref[...] += part

    full = pl.BlockSpec((h, s), lambda j: (0, 0))
    blk = pl.BlockSpec((h, LANES), lambda j: (0, j))
    one = pl.BlockSpec((h, 1), lambda j: (0, 0))
    df, gb = pl.pallas_call(
        body, name="fox_gates_bwd", grid=(nb,), in_specs=[full, full, blk, one], out_specs=[blk, one],
        out_shape=[jax.ShapeDtypeStruct((h, s), F32), jax.ShapeDtypeStruct((h, 1), F32)],
        compiler_params=_params(("arbitrary",)),
    )(dc, ft, ft, bias.reshape(h, 1))
    return df, gb.reshape(h)


def _delta_rep(do, o, tm=512):
    s = do.shape[0]
    tm = min(tm, s)

    def body(do_ref, o_ref, d_ref):
        pr = do_ref[...].astype(F32) * o_ref[...].astype(F32)
        lo = _lo_mask(pr.shape)
        d0 = jnp.sum(jnp.where(lo, pr, 0.0), axis=-1, keepdims=True)
        d1 = jnp.sum(jnp.where(lo, 0.0, pr), axis=-1, keepdims=True)
        d_ref[...] = jnp.where(lo, d0, d1)

    own = pl.BlockSpec((tm, LANES), lambda p, i: (i, p))
    return pl.pallas_call(
        body, name="delta_rep", grid=(N_PAIR, s // tm), in_specs=[own, own],
        out_specs=pl.BlockSpec((None, tm, LANES), lambda p, i: (p, i, 0)),
        out_shape=jax.ShapeDtypeStruct((N_PAIR, s, LANES), F32),
        compiler_params=_params(("parallel", "parallel")),
    )(do, o)


def _tile(s, t):
    t = min(t, s)
    assert s % t == 0
    return t


def _fox_fwd(qn, kn, proj, ccol, crow, t=512):
    s = qn.shape[0]
    t = _tile(s, t)
    n = s // t
    qtab, ktab, ntri = _tri_tables(n, 1, "k_asc")
    voff = FOX_OFF // LANES + 2 * N_PAIR

    def body(qt_ref, kt_ref, q_ref, k_ref, v_ref, cc_ref, cr_ref, o_ref, lse_ref, m_ref, l_ref, acc_ref):
        i = pl.program_id(1)
        qi, kj = qt_ref[i], kt_ref[i]

        @pl.when(kj == 0)
        def _():
            m_ref[...] = jnp.full(m_ref.shape, NEG_INF, F32)
            l_ref[...] = jnp.zeros(l_ref.shape, F32)
            acc_ref[...] = jnp.zeros(acc_ref.shape, F32)

        q, k, v = q_ref[...], k_ref[...], v_ref[...]
        row, col = _positions(qi, kj, t, t)
        causal = col <= row
        m_old = m_ref[...]
        mn, rs, pv = [], [], []
        for h in range(2):
            sc = _dot(_head(q, h), k, _NT) + _pair_cols(cc_ref[...], h) - cr_ref[h:h + 1, :]
            sc = jnp.where(causal, sc, NEG_INF)
            m_new = jnp.maximum(_pair_cols(m_old, h), jnp.max(sc, axis=-1, keepdims=True))
            p = jnp.exp(sc - m_new)
            mn.append(m_new)
            rs.append(jnp.sum(p, axis=-1, keepdims=True))
            pv.append(_dot(p.astype(BF16), _head(v, h), _NN))
        m_rep = _rep(mn[0], mn[1], m_old.shape)
        alpha = jnp.exp(m_old - m_rep)
        l_ref[...] = alpha * l_ref[...] + _rep(rs[0], rs[1], m_old.shape)
        acc_ref[...] = alpha * acc_ref[...] + pv[0] + pv[1]
        m_ref[...] = m_rep

        @pl.when(kj == qi)
        def _():
            o_ref[...] = acc_ref[...] / l_ref[...]
            lse_ref[...] = m_ref[...] + jnp.log(l_ref[...])

    grid_spec = pltpu.PrefetchScalarGridSpec(
        num_scalar_prefetch=2, grid=(N_PAIR, ntri),
        in_specs=[pl.BlockSpec((t, LANES), lambda p, i, qt, kt: (qt[i], p)),
                  pl.BlockSpec((t, LANES), lambda p, i, qt, kt: (kt[i], p)),
                  pl.BlockSpec((t, LANES), lambda p, i, qt, kt: (kt[i], voff + p)),
                  pl.BlockSpec((None, t, LANES), lambda p, i, qt, kt: (p, qt[i], 0)),
                  pl.BlockSpec((None, 2, t), lambda p, i, qt, kt: (p, 0, kt[i]))],
        out_specs=[pl.BlockSpec((t, LANES), lambda p, i, qt, kt: (qt[i], p)),
                   pl.BlockSpec((None, t, LANES), lambda p, i, qt, kt: (p, qt[i], 0))],
        scratch_shapes=[pltpu.VMEM((t, LANES), F32)] * 3)
    return pl.pallas_call(
        body, name="fox_fwd", grid_spec=grid_spec,
        out_shape=[jax.ShapeDtypeStruct((s, WIDTH), F32), jax.ShapeDtypeStruct((N_PAIR, s, LANES), F32)],
        compiler_params=_params(("parallel", "arbitrary")),
    )(qtab, ktab, qn, kn, proj, ccol, crow)


def _fox_bwd(qn, kn, proj, do, lse, delta, ccol, crow, t=512, comm=None):
    s = qn.shape[0]
    t = _tile(s, t)
    n = s // t
    qtab, ktab, ntri = _tri_tables(n, 1, "k_outer")
    voff = FOX_OFF // LANES + 2 * N_PAIR
    n_in, n_out = (len(comm.inputs), len(comm.out_shapes)) if comm else (0, 0)

    def body(qt_ref, kt_ref, q_ref, k_ref, v_ref, do_ref, lse_ref, dl_ref, cc_ref, cr_ref, *rest):
        comm_in, rest = rest[:n_in], rest[n_in:]
        (dq_ref, dk_ref, dv_ref, dc_ref, dcq_ref), rest = rest[:5], rest[5:]
        comm_out, rest = rest[:n_out], rest[n_out:]
        (dka_ref, dva_ref, dca_ref), sems = rest[:3], rest[3:]
        i = pl.program_id(1)
        qi, kj = qt_ref[i], kt_ref[i]
        if comm:
            pl.when((pl.program_id(0) == 0) & (i == 0))(lambda: comm.start(comm_in, comm_out, sems))

        @pl.when(i == 0)
        def _():
            dq_ref[...] = jnp.zeros(dq_ref.shape, F32)
            dcq_ref[...] = jnp.zeros(dcq_ref.shape, F32)

        @pl.when(qi == kj)
        def _():
            dka_ref[...] = jnp.zeros(dka_ref.shape, F32)
            dva_ref[...] = jnp.zeros(dva_ref.shape, F32)
            dca_ref[...] = jnp.zeros(dca_ref.shape, F32)

        q, k, v, g = q_ref[...], k_ref[...], v_ref[...], do_ref[...]
        row, col = _positions(qi, kj, t, t)
        causal = col <= row
        dq = jnp.zeros((t, LANES), F32)
        dk = jnp.zeros((t, LANES), F32)
        dv = jnp.zeros((t, LANES), F32)
        rowsum = []
        for h in range(2):
            qh, gh = _head(q, h), _head(g, h)
            sc = _dot(qh, k, _NT) + _pair_cols(cc_ref[...], h) - cr_ref[h:h + 1, :]
            p = jnp.where(causal, jnp.exp(sc - _pair_cols(lse_ref[...], h)), 0.0)
            dp = _dot(gh, v, _NT)
            ds = p * (dp - _pair_cols(dl_ref[...], h))
            dsb = ds.astype(BF16)
            dv = dv + _dot(p.astype(BF16), gh, _TN)
            dk = dk + _dot(dsb, qh, _TN)
            dq = dq + _dot(dsb, _head(k, h), _NN)
            dca_ref[h:h + 1, :] -= jnp.sum(ds, axis=0, keepdims=True)
            rowsum.append(jnp.sum(ds, axis=-1, keepdims=True))
        dka_ref[...] += dk
        dva_ref[...] += dv
        rows = pl.ds(pl.multiple_of(qi * t, t), t)
        dq_ref[rows, :] += dq
        dcq_ref[rows, :] += _rep(rowsum[0], rowsum[1], (t, LANES))

        @pl.when(qi == n - 1)
        def _():
            dk_ref[...] = dka_ref[...]
            dv_ref[...] = dva_ref[...].astype(BF16)
            dc_ref[...] = dca_ref[...]

        if comm:
            pl.when((pl.program_id(0) == N_PAIR - 1) & (i == ntri - 1))(lambda: comm.finish(comm_in, comm_out, sems))

    qblk = lambda p, i, qt, kt: (qt[i], p)
    kblk = lambda p, i, qt, kt: (kt[i], p)
    qrep = pl.BlockSpec((None, t, LANES), lambda p, i, qt, kt: (p, qt[i], 0))
    crs = pl.BlockSpec((None, 2, t), lambda p, i, qt, kt: (p, 0, kt[i]))
    grid_spec = pltpu.PrefetchScalarGridSpec(
        num_scalar_prefetch=2, grid=(N_PAIR, ntri),
        in_specs=[pl.BlockSpec((t, LANES), qblk), pl.BlockSpec((t, LANES), kblk),
                  pl.BlockSpec((t, LANES), lambda p, i, qt, kt: (kt[i], voff + p)),
                  pl.BlockSpec((t, LANES), qblk), qrep, qrep, qrep, crs] + [ANY] * n_in,
        out_specs=[pl.BlockSpec((s, LANES), lambda p, i, qt, kt: (0, p)),
                   pl.BlockSpec((t, LANES), kblk), pl.BlockSpec((t, LANES), kblk), crs,
                   pl.BlockSpec((None, s, LANES), lambda p, i, qt, kt: (p, 0, 0))] + [ANY] * n_out,
        scratch_shapes=[pltpu.VMEM((t, LANES), F32), pltpu.VMEM((t, LANES), F32), pltpu.VMEM((2, t), F32)]
        + (comm.sems if comm else []))
    outs = pl.pallas_call(
        body, name="fox_bwd", grid_spec=grid_spec,
        out_shape=[jax.ShapeDtypeStruct((s, WIDTH), F32), jax.ShapeDtypeStruct((s, WIDTH), F32),
                   jax.ShapeDtypeStruct((s, WIDTH), BF16), jax.ShapeDtypeStruct((N_PAIR, 2, s), F32),
                   jax.ShapeDtypeStruct((N_PAIR, s, LANES), F32)] + (comm.out_shapes if comm else []),
        compiler_params=_params(("arbitrary", "arbitrary") if comm else ("parallel", "arbitrary")),
    )(qtab, ktab, qn, kn, proj, do, lse, delta, ccol, crow, *(comm.inputs if comm else []))
    return outs[:5], outs[5:]


def _sb_tile(qh, k, strict, tk, r_col):
    z = _dot(qh, k, _NT)
    lg = jnp.where(strict, -(jnp.maximum(z, 0.0) + jnp.log(1.0 + jnp.exp(-jnp.abs(z)))), 0.0)
    jj = lax.broadcasted_iota(jnp.int32, (tk, tk), 0)
    ss = lax.broadcasted_iota(jnp.int32, (tk, tk), 1)
    above = jnp.where(jj > ss, 1.0, 0.0).astype(BF16)
    suffix = sum(_dot(p, above, _NN) for p in _split2(lg)) + r_col
    a = jnp.where(strict, jnp.exp(lg + z + suffix), 0.0)
    return z, lg, a


def _sb_fwd(proj, tq=512, tk=256, comm=None):
    s = proj.shape[0]
    tq = _tile(s, tq)
    tk = _tile(tq, tk)
    nq, r = s // tq, tq // tk
    qtab, ktab, ntri = _tri_tables(nq, r, "k_desc")
    off = SB_OFF // LANES
    n_in, n_out = (len(comm.inputs), len(comm.out_shapes)) if comm else (0, 0)

    def body(qt_ref, kt_ref, q_ref, k_ref, v_ref, *rest):
        comm_in, o_ref, rest = rest[:n_in], rest[n_in], rest[n_in + 1:]
        comm_out, rest = rest[:n_out], rest[n_out:]
        (acc_ref, r_ref), sems = rest[:2], rest[2:]
        i = pl.program_id(1)
        qi, kj = qt_ref[i], kt_ref[i]
        if comm:
            pl.when((pl.program_id(0) == 0) & (i == 0))(lambda: comm.start(comm_in, comm_out, sems))

        @pl.when(kj == (qi + 1) * r - 1)
        def _():
            acc_ref[...] = jnp.zeros(acc_ref.shape, F32)
            r_ref[...] = jnp.zeros(r_ref.shape, F32)

        q = q_ref[...] * QK_SCALE
        k, v = k_ref[...], v_ref[...]
        row, col = _positions(qi, kj, tq, tk)
        strict = col < row
        acc = acc_ref[...]
        for h in range(2):
            _, lg, a = _sb_tile(_head(q, h), k, strict, tk, r_ref[h])
            acc = acc + _dot(a.astype(BF16), _head(v, h), _NN)
            r_ref[h] += jnp.sum(lg, axis=-1, keepdims=True)
        acc_ref[...] = acc

        @pl.when(kj == 0)
        def _():
            o_ref[...] = acc_ref[...]

        if comm:
            pl.when((pl.program_id(0) == N_PAIR - 1) & (i == ntri - 1))(lambda: comm.finish(comm_in, comm_out, sems))

    grid_spec = pltpu.PrefetchScalarGridSpec(
        num_scalar_prefetch=2, grid=(N_PAIR, ntri),
        in_specs=[pl.BlockSpec((tq, LANES), lambda p, i, qt, kt: (qt[i], off + p)),
                  pl.BlockSpec((tk, LANES), lambda p, i, qt, kt: (kt[i], off + N_PAIR + p)),
                  pl.BlockSpec((tk, LANES), lambda p, i, qt, kt: (kt[i], off + 2 * N_PAIR + p))] + [ANY] * n_in,
        out_specs=[pl.BlockSpec((tq, LANES), lambda p, i, qt, kt: (qt[i], p))] + [ANY] * n_out,
        scratch_shapes=[pltpu.VMEM((tq, LANES), F32), pltpu.VMEM((2, tq, 1), F32)] + (comm.sems if comm else []))
    outs = pl.pallas_call(
        body, name="sb_fwd", grid_spec=grid_spec,
        out_shape=[jax.ShapeDtypeStruct((s, WIDTH), F32)] + (comm.out_shapes if comm else []),
        compiler_params=_params(("arbitrary", "arbitrary") if comm else ("parallel", "arbitrary")),
    )(qtab, ktab, proj, proj, proj, *(comm.inputs if comm else []))
    return outs[0], outs[1:]


def _sb_bwd(proj, do, delta, tq=512, tk=256, comm=None):
    s = proj.shape[0]
    tq = _tile(s, tq)
    tk = _tile(tq, tk)
    nq, r = s // tq, tq // tk
    qtab, ktab, ntri = _tri_tables(nq, r, "k_desc")
    off = SB_OFF // LANES
    n_in, n_out = (len(comm.inputs), len(comm.out_shapes)) if comm else (0, 0)

    def body(qt_ref, kt_ref, q_ref, k_ref, v_ref, do_ref, dl_ref, *rest):
        comm_in, rest = rest[:n_in], rest[n_in:]
        (dq_ref, dk_ref, dv_ref), rest = rest[:3], rest[3:]
        comm_out, rest = rest[:n_out], rest[n_out:]
        (dqa_ref, dka_ref, dva_ref, r_ref, rd_ref), sems = rest[:5], rest[5:]
        i = pl.program_id(1)
        qi, kj = qt_ref[i], kt_ref[i]
        if comm:
            pl.when((pl.program_id(0) == 0) & (i == 0))(lambda: comm.start(comm_in, comm_out, sems))

        @pl.when(i == 0)
        def _():
            dka_ref[...] = jnp.zeros(dka_ref.shape, F32)
            dva_ref[...] = jnp.zeros(dva_ref.shape, F32)

        @pl.when(kj == (qi + 1) * r - 1)
        def _():
            dqa_ref[...] = jnp.zeros(dqa_ref.shape, F32)
            r_ref[...] = jnp.zeros(r_ref.shape, F32)
            rd_ref[...] = jnp.zeros(rd_ref.shape, F32)

        q = q_ref[...] * QK_SCALE
        k, v, g = k_ref[...], v_ref[...], do_ref[...]
        row, col = _positions(qi, kj, tq, tk)
        strict = col < row
        ss = lax.broadcasted_iota(jnp.int32, (tk, tk), 0)
        jj = lax.broadcasted_iota(jnp.int32, (tk, tk), 1)
        at_or_after = jnp.where(ss >= jj, 1.0, 0.0).astype(BF16)
        dq = jnp.zeros((tq, LANES), F32)
        dk = jnp.zeros((tk, LANES), F32)
        dv = jnp.zeros((tk, LANES), F32)
        for h in range(2):
            qh, gh = _head(q, h), _head(g, h)
            z, lg, a = _sb_tile(qh, k, strict, tk, r_ref[h])
            da = _dot(gh, v, _NT)
            ab = a.astype(BF16)
            d_a = da * ab.astype(F32)
            incl = sum(_dot(p, at_or_after, _NN) for p in _split2(d_a))
            d_l = _pair_cols(dl_ref[...], h) - rd_ref[h] - incl
            sig = jnp.exp(lg + z)
            dz = jnp.where(strict, d_a * (1.0 - sig) - d_l * sig, 0.0).astype(BF16)
            dq = dq + _dot(dz, _head(k, h), _NN)
            dk = dk + _dot(dz, qh, _TN)
            dv = dv + _dot(ab, gh, _TN)
            r_ref[h] += jnp.sum(lg, axis=-1, keepdims=True)
            rd_ref[h] += jnp.sum(d_a, axis=-1, keepdims=True)
        dqa_ref[...] += dq
        rows = pl.ds(pl.multiple_of(kj * tk, tk), tk)
        dka_ref[rows, :] += dk
        dva_ref[rows, :] += dv

        @pl.when(kj == 0)
        def _():
            dq_ref[...] = (dqa_ref[...] * QK_SCALE).astype(BF16)

        @pl.when(i == ntri - 1)
        def _():
            dk_ref[...] = dka_ref[...].astype(BF16)
            dv_ref[...] = dva_ref[...].astype(BF16)

        if comm:
            pl.when((pl.program_id(0) == N_PAIR - 1) & (i == ntri - 1))(lambda: comm.finish(comm_in, comm_out, sems))

    qblk = lambda p, i, qt, kt: (qt[i], p)
    whole = pl.BlockSpec((s, LANES), lambda p, i, qt, kt: (0, p))
    grid_spec = pltpu.PrefetchScalarGridSpec(
        num_scalar_prefetch=2, grid=(N_PAIR, ntri),
        in_specs=[pl.BlockSpec((tq, LANES), lambda p, i, qt, kt: (qt[i], off + p)),
                  pl.BlockSpec((tk, LANES), lambda p, i, qt, kt: (kt[i], off + N_PAIR + p)),
                  pl.BlockSpec((tk, LANES), lambda p, i, qt, kt: (kt[i], off + 2 * N_PAIR + p)),
                  pl.BlockSpec((tq, LANES), qblk),
                  pl.BlockSpec((None, tq, LANES), lambda p, i, qt, kt: (p, qt[i], 0))] + [ANY] * n_in,
        out_specs=[pl.BlockSpec((tq, LANES), qblk), whole, whole] + [ANY] * n_out,
        scratch_shapes=[pltpu.VMEM((tq, LANES), F32), pltpu.VMEM((s, LANES), F32), pltpu.VMEM((s, LANES), F32),
                        pltpu.VMEM((2, tq, 1), F32), pltpu.VMEM((2, tq, 1), F32)] + (comm.sems if comm else []))
    act = jax.ShapeDtypeStruct((s, WIDTH), BF16)
    outs = pl.pallas_call(
        body, name="sb_bwd", grid_spec=grid_spec, out_shape=[act, act, act] + (comm.out_shapes if comm else []),
        compiler_params=_params(("arbitrary", "arbitrary") if comm else ("parallel", "arbitrary")),
    )(qtab, ktab, proj, proj, proj, do, delta, *(comm.inputs if comm else []))
    return outs[:3], outs[3:]


def _loss_head(y, target, tm=256):
    s, d = y.shape
    tm = min(tm, s)

    def body(y_ref, t_ref, l_ref, dy_ref):
        e = y_ref[...] - t_ref[...]
        dy_ref[...] = e / d
        part = jnp.sum(jnp.sum(e * e, axis=0, keepdims=True), axis=1, keepdims=True)

        @pl.when(pl.program_id(0) == 0)
        def _():
            l_ref[...] = jnp.broadcast_to(part, l_ref.shape)

        @pl.when(pl.program_id(0) > 0)
        def _():
            l_ref[...] += jnp.broadcast_to(part, l_ref.shape)

    row = pl.BlockSpec((tm, d), lambda i: (i, 0))
    return pl.pallas_call(
        body, name="loss_head", grid=(s // tm,), in_specs=[row, row],
        out_specs=[pl.BlockSpec((1, LANES), lambda i: (0, 0)), row],
        out_shape=[jax.ShapeDtypeStruct((1, LANES), F32), jax.ShapeDtypeStruct((s, d), F32)],
        compiler_params=_params(("arbitrary",)),
    )(y, target)


def _adamw(w, g, m, v, name):
    shape = w.shape
    cols = shape[-1]
    rows = int(np.prod(shape[:-1]))
    tm = rows
    for cand in (512, 256, 128, 64, 32, 16, 8):
        if rows % cand == 0 and rows > cand and cand * cols * 4 <= ADAMW_BLOCK_BYTES:
            tm = cand
            break

    def body(w_ref, g_ref, m_ref, v_ref, d_ref, mo_ref, vo_ref):
        gr = g_ref[...]
        mn = ADAM_B1 * m_ref[...] + (1.0 - ADAM_B1) * gr
        vn = ADAM_B2 * v_ref[...] + (1.0 - ADAM_B2) * (gr * gr)
        m_hat = mn / (1.0 - ADAM_B1 ** ADAM_STEP)
        v_hat = vn / (1.0 - ADAM_B2 ** ADAM_STEP)
        d_ref[...] = -ADAM_LR * (m_hat / (jnp.sqrt(v_hat) + ADAM_EPS) + ADAM_WD * w_ref[...])
        mo_ref[...] = mn
        vo_ref[...] = vn

    blk = pl.BlockSpec((tm, cols), lambda i: (i, 0))
    out = jax.ShapeDtypeStruct((rows, cols), F32)
    r2 = lambda a: a.reshape(rows, cols)
    outs = pl.pallas_call(
        body, name=name, grid=(rows // tm,), in_specs=[blk] * 4, out_specs=[blk] * 3, out_shape=[out] * 3,
        compiler_params=_params(("parallel",)),
    )(r2(w), r2(g), r2(m), r2(v))
    return tuple(o.reshape(shape) for o in outs)


def _split_w_in(w_in):
    c0, c1, c2, c3 = 3 * WIDTH, 6 * WIDTH, 6 * WIDTH + N_HEADS, 9 * WIDTH + N_HEADS
    wa = jnp.concatenate([w_in[:, c3:], w_in[:, :c0], w_in[:, c0:c1], w_in[:, c2:c3]], axis=1)
    wf = jnp.pad(w_in[:, c1:c2], ((0, 0), (0, LANES - N_HEADS)))
    return wa, wf


def _join_w_in(ga, gf):
    g, c, f, s = 3 * D_MODEL, CONV_OFF, FOX_OFF, SB_OFF
    return jnp.concatenate([ga[:, c:f], ga[:, f:s], gf[:, :N_HEADS], ga[:, s:], ga[:, :g]], axis=1)


def _layer_fwd(x, p, carry=None):
    s = x.shape[0]
    proj, hn1 = _norm_mm(x, p["norm1_g"], p["wa"], BF16, name="in_proj")
    fraw, _ = _norm_mm(x, p["norm1_g"], p["wf"], F32, name="in_proj_f")
    ft = fraw[:, :N_HEADS].T
    crow8 = _fox_gates(ft, p["fox_f_bias"])
    crow = crow8.reshape(N_PAIR, 2, s)
    ccol = jnp.repeat(crow.transpose(0, 2, 1), HEAD_DIM, axis=2)
    oc = _conv_mix_fwd(proj, p["conv_w"])
    qn, kn = _fox_prep(proj, p["fox_q_norm_g"], p["fox_k_norm_g"])
    of, lse = _fox_fwd(qn, kn, proj, ccol, crow)
    osb, carried = _sb_fwd(proj, comm=carry)
    xm, merged, yc, yf, ys = _merge_fwd(x, oc, of, osb, p["w_proj_conv"], p["w_proj_fox"], p["w_proj_sb"], proj,
                                        p["gate_bias"], p["w_out"])
    up, hn2 = _norm_mm(xm, p["norm2_g"], p["w_up"], BF16, name="up_proj")
    hh = _glu_fwd(up, p["ffn_conv_w"], p["ffn_conv_b"])
    xo = _mm(hh, p["w_down"], "nn", F32, res=xm, name="down_proj")
    saved = dict(x=x, hn1=hn1, proj=proj, ft=ft, crow=crow, ccol=ccol, oc=oc, qn=qn, kn=kn, of=of, lse=lse, osb=osb,
                 merged=merged, yc=yc, yf=yf, ys=ys, xm=xm, hn2=hn2, up=up, hh=hh)
    return xo, saved, carried


def _layer_bwd(dx, p, a, pending=None):
    s = dx.shape[0]
    g = {}
    dhh = _mm(dx, p["w_down"], "nt", BF16, tn=1408, name="d_down_in")
    g["w_down"] = _mm(a["hh"], dx, "tn", F32, name="g_w_down")
    dug, duv, g["ffn_conv_w"], g["ffn_conv_b"] = _glu_bwd(dhh, a["up"], p["ffn_conv_w"], p["ffn_conv_b"])
    dup = jnp.concatenate([dug, duv], axis=1)
    g["w_up"] = _mm(a["hn2"], dup, "tn", F32, tn=1408, name="g_w_up")
    dhn2 = _mm(dup, p["w_up"], "nt", F32, name="d_up_in")
    dx, g["norm2_g"] = _norm_bwd(dhn2, a["xm"], p["norm2_g"], dx, name="norm2_bwd")
    dm = _mm(dx, p["w_out"], "nt", BF16, name="d_out_in")
    g["w_out"] = _mm(a["merged"], dx, "tn", F32, tn=512, name="g_w_out")
    dyc, dyf, dys, dgl, g["gate_bias"] = _gate_bwd(dm, a["yc"], a["yf"], a["ys"], a["proj"], p["gate_bias"])
    doc = _mm(dyc, p["w_proj_conv"], "nt", BF16, name="d_pc_in")
    dof = _mm(dyf, p["w_proj_fox"], "nt", BF16, name="d_pf_in")
    dos = _mm(dys, p["w_proj_sb"], "nt", BF16, name="d_ps_in")
    g["w_proj_conv"] = _mm(a["oc"], dyc, "tn", F32, name="g_w_pc")
    g["w_proj_fox"] = _mm(a["of"], dyf, "tn", F32, name="g_w_pf")
    g["w_proj_sb"] = _mm(a["osb"], dys, "tn", F32, name="g_w_ps")
    dcb, dcc, dch, g["conv_w"] = _conv_mix_bwd(doc, a["proj"], p["conv_w"])
    delta_f = _delta_rep(dof, a["of"])
    (dqs, dkn, dfv, dcrow, dcq), recv_a = _fox_bwd(a["qn"], a["kn"], a["proj"], dof, a["lse"], delta_f, a["ccol"], a["crow"],
                                                   comm=pending.exchange() if pending else None)
    dc = dcrow.reshape(N_HEADS, s) + dcq[:, :, ::HEAD_DIM].transpose(0, 2, 1).reshape(N_HEADS, s)
    dfq, dfk, g["fox_q_norm_g"], g["fox_k_norm_g"] = _fox_post(dqs, dkn, a["proj"], p["fox_q_norm_g"], p["fox_k_norm_g"])
    dft, g["fox_f_bias"] = _fox_gates_bwd(dc, a["ft"], p["fox_f_bias"])
    delta_s = _delta_rep(dos, a["osb"])
    (dsq, dsk, dsv), recv_b = _sb_bwd(a["proj"], dos, delta_s, comm=pending.sums(recv_a) if pending else None)
    done = pending.finish(recv_b) if pending else None
    dproj = jnp.concatenate([dgl, dcb, dcc, dch, dfq, dfk, dfv, dsq, dsk, dsv], axis=1)
    dfp = jnp.pad(dft.T, ((0, 0), (0, LANES - N_HEADS))).astype(BF16)
    ga = _mm(a["hn1"], dproj, "tn", F32, tn=768, name="g_w_in")
    gf = _mm(a["hn1"], dfp, "tn", F32, name="g_w_in_f")
    g["w_in"] = _join_w_in(ga, gf)
    dhn1 = _mm(dfp, p["wf"], "nt", F32, name="d_in_f")
    dhn1 = _mm(dproj, p["wa"], "nt", F32, res=dhn1, tk=1536, name="d_in")
    dx, g["norm1_g"] = _norm_bwd(dhn1, a["x"], p["norm1_g"], dx, name="norm1_bwd")
    return dx, g, done


MATMUL_WEIGHTS = ("w_in", "w_proj_conv", "w_proj_fox", "w_proj_sb", "w_out", "w_up", "w_down")
WEIGHTS = ("norm1_g", "w_in", "fox_f_bias", "gate_bias", "conv_w", "fox_q_norm_g", "fox_k_norm_g", "w_proj_conv",
           "w_proj_fox", "w_proj_sb", "w_out", "norm2_g", "w_up", "ffn_conv_w", "ffn_conv_b", "w_down")


def _local_step(x, target, plan):
    depth = plan.depth
    layers, saved = [plan.first_layer()], []
    for l in range(depth):
        p = layers[l]
        p["wa"], p["wf"] = _split_w_in(p.pop("w_in"))
        x, a, carried = _layer_fwd(x, p, plan.gather(l + 1) if l + 1 < depth else None)
        saved.append(a)
        if l + 1 < depth:
            layers.append(plan.layer(l + 1, carried))
    sq, dx = _loss_head(x, target)
    grads, pending = [None] * depth, None
    for l in reversed(range(depth)):
        dx, grads[l], done = _layer_bwd(dx, layers[l], saved[l], pending)
        if pending is not None:
            grads[l + 1].update(zip(BIG, done))
        pending = plan.reduction(grads[l])
    if pending is not None:
        grads[0].update(zip(BIG, pending.run()))
    return sq, dx, grads


ANY = pl.BlockSpec(memory_space=pl.ANY)


def _place():
    x, y, c = lax.axis_index("x"), lax.axis_index("y"), lax.axis_index("c")
    chips = [(1 - x, y), (x, 1 - y), (1 - x, 1 - y)]
    return x, y, c, chips


_Comm = collections.namedtuple("_Comm", "inputs out_shapes sems start finish")


def _run_comm(comm, name):
    n_in, n_out = len(comm.inputs), len(comm.out_shapes)

    def body(*refs):
        ins, outs, sems = refs[:n_in], refs[n_in:n_in + n_out], refs[n_in + n_out:]
        comm.start(ins, outs, sems)
        comm.finish(ins, outs, sems)

    return pl.pallas_call(body, name=name, in_specs=[ANY] * n_in, out_specs=[ANY] * n_out, out_shape=comm.out_shapes,
                          scratch_shapes=comm.sems)(*comm.inputs)


def _gather_comm(shards):
    n = len(shards)

    def copy(x_refs, out_refs, sems, k, t, chip_index, which_half, to, from_input=False):
        half = shards[t].shape[0] // 2
        rows = pl.ds(which_half * half, half)
        dst = out_refs[t].at[chip_index, rows, :]
        return pltpu.make_async_remote_copy(
            src_ref=x_refs[t].at[rows, :] if from_input else dst, dst_ref=dst,
            send_sem=sems[0].at[k, t], recv_sem=sems[1].at[k, t], device_id=to, device_id_type=MESH)

    def first(x_refs, out_refs, sems):
        x, y, c, chips = _place()
        return [copy(x_refs, out_refs, sems, p, t, 2 * x + y, c, (*chip, c), from_input=True)
                for t in range(n) for p, chip in enumerate(chips)]

    def start(x_refs, out_refs, sems):
        for cp in first(x_refs, out_refs, sems):
            cp.start()

    def finish(x_refs, out_refs, sems):
        x, y, c, chips = _place()
        passed = []
        for t in range(n):
            for p, chip in enumerate(chips):
                copy(x_refs, out_refs, sems, p, t, 2 * chip[0] + chip[1], c, (x, y, c)).wait_recv()
                fwd = copy(x_refs, out_refs, sems, 3 + p, t, 2 * chip[0] + chip[1], c, (x, y, 1 - c))
                fwd.start()
                passed.append(fwd)
        for t in range(n):
            for p, chip in enumerate(chips):
                copy(x_refs, out_refs, sems, 3 + p, t, 2 * chip[0] + chip[1], 1 - c, (x, y, c)).wait_recv()
        for cp in first(x_refs, out_refs, sems) + passed:
            cp.wait_send()

    return _Comm(list(shards), [jax.ShapeDtypeStruct((4, *s.shape), s.dtype) for s in shards],
                 [pltpu.SemaphoreType.DMA((6, n)), pltpu.SemaphoreType.DMA((6, n))], start, finish)


def _own_block(got, shards):
    chip = 2 * lax.axis_index("x") + lax.axis_index("y")
    return [lax.dynamic_update_index_in_dim(g, s, chip, 0) for g, s in zip(got, shards)]


def _pair_exchange_comm(gs):
    n = len(gs)

    def copies(g_refs, recv_refs, sems):
        x, y, c, _ = _place()
        return [pltpu.make_async_remote_copy(
            src_ref=g_refs[t].at[:, pl.ds((1 - c) * (gs[t].shape[1] // 2), gs[t].shape[1] // 2), :], dst_ref=recv_refs[t],
            send_sem=sems[0].at[t], recv_sem=sems[1].at[t], device_id=(x, y, 1 - c), device_id_type=MESH) for t in range(n)]

    def start(g_refs, recv_refs, sems):
        for cp in copies(g_refs, recv_refs, sems):
            cp.start()

    def finish(g_refs, recv_refs, sems):
        for cp in copies(g_refs, recv_refs, sems):
            cp.wait()

    return _Comm(list(gs), [jax.ShapeDtypeStruct((4, g.shape[1] // 2, g.shape[2]), g.dtype) for g in gs],
                 [pltpu.SemaphoreType.DMA((n,)), pltpu.SemaphoreType.DMA((n,))], start, finish)


def _pair_sum(g, recv, core, tr=256):
    n, r, cols = g.shape
    half = r // 2
    tr = _row_tile(half, tr)
    nb = half // tr

    def body(c_ref, g_ref, r_ref, o_ref):
        o_ref[...] = (g_ref[...] + r_ref[...]).astype(BF16)

    grid_spec = pltpu.PrefetchScalarGridSpec(
        num_scalar_prefetch=1, grid=(n, nb),
        in_specs=[pl.BlockSpec((None, tr, cols), lambda k, i, c: (k, c[0] * nb + i, 0)),
                  pl.BlockSpec((None, tr, cols), lambda k, i, c: (k, i, 0))],
        out_specs=pl.BlockSpec((None, tr, cols), lambda k, i, c: (k, i, 0)))
    return pl.pallas_call(
        body, name="rs_pair_sum", grid_spec=grid_spec, out_shape=jax.ShapeDtypeStruct((n, half, cols), BF16),
        compiler_params=_params(("parallel", "parallel")),
    )(core, g, recv)


def _row_tile(rows, pref):
    best = None
    for t in range(16, min(rows, pref) + 1, 16):
        if rows % t == 0:
            best = t
    assert best is not None, (rows, pref)
    return best


def _chip_exchange_comm(s1s):
    n = len(s1s)

    def copies(s_refs, recv_refs, sems):
        x, y, c, chips = _place()
        return [pltpu.make_async_remote_copy(
            src_ref=s_refs[t].at[2 * chip[0] + chip[1]], dst_ref=recv_refs[t].at[p],
            send_sem=sems[0].at[p, t], recv_sem=sems[1].at[p, t], device_id=(*chip, c), device_id_type=MESH)
            for t in range(n) for p, chip in enumerate(chips)]

    def start(s_refs, recv_refs, sems):
        for cp in copies(s_refs, recv_refs, sems):
            cp.start()

    def finish(s_refs, recv_refs, sems):
        for cp in copies(s_refs, recv_refs, sems):
            cp.wait()

    return _Comm(list(s1s), [jax.ShapeDtypeStruct((3, *s.shape[1:]), s.dtype) for s in s1s],
                 [pltpu.SemaphoreType.DMA((3, n)), pltpu.SemaphoreType.DMA((3, n))], start, finish)


def _final_sum(g, recv_a, recv_b, core, chip, tr=256):
    n, r, cols = g.shape
    half = r // 2
    tr = _row_tile(half, tr)
    nb = half // tr

    def body(c_ref, k_ref, g_ref, a_ref, b0_ref, b1_ref, b2_ref, o_ref):
        total = g_ref[...] + a_ref[...]
        for b_ref in (b0_ref, b1_ref, b2_ref):
            total = total + b_ref[...].astype(F32)
        o_ref[...] = total

    rel = lambda p: pl.BlockSpec((None, tr, cols), lambda i, c, k, p=p: (p, i, 0))
    grid_spec = pltpu.PrefetchScalarGridSpec(
        num_scalar_prefetch=2, grid=(nb,),
        in_specs=[pl.BlockSpec((None, tr, cols), lambda i, c, k: (k[0], c[0] * nb + i, 0)),
                  pl.BlockSpec((None, tr, cols), lambda i, c, k: (k[0], i, 0)), rel(0), rel(1), rel(2)],
        out_specs=pl.BlockSpec((tr, cols), lambda i, c, k: (c[0] * nb + i, 0)))
    return pl.pallas_call(
        body, name="rs_final_sum", grid_spec=grid_spec, out_shape=jax.ShapeDtypeStruct((r, cols), F32),
        compiler_params=_params(("parallel",)),
    )(core, chip, g, recv_a, recv_b, recv_b, recv_b)


def _pair_join(fs):
    n = len(fs)

    def body(*refs):
        out_refs = refs[n:2 * n]
        send_sems, recv_sems = refs[2 * n:]
        x, y, c, _ = _place()

        def copy(t, which_half):
            h = fs[t].shape[0] // 2
            rows = out_refs[t].at[pl.ds(which_half * h, h), :]
            return pltpu.make_async_remote_copy(
                src_ref=rows, dst_ref=rows, send_sem=send_sems.at[t], recv_sem=recv_sems.at[t],
                device_id=(x, y, 1 - c), device_id_type=MESH)

        sends = [copy(t, c) for t in range(n)]
        for cp in sends:
            cp.start()
        for t in range(n):
            sends[t].wait_send()
            copy(t, 1 - c).wait_recv()

    return pl.pallas_call(
        body, name="rs_pair_join", in_specs=[ANY] * n, out_specs=[ANY] * n,
        out_shape=[jax.ShapeDtypeStruct(f.shape, f.dtype) for f in fs],
        input_output_aliases={t: t for t in range(n)},
        scratch_shapes=[pltpu.SemaphoreType.DMA((n,)), pltpu.SemaphoreType.DMA((n,))],
    )(*fs)


class _Reduction:
    def __init__(self, gs):
        self.gs = gs
        self.core = lax.axis_index("c").astype(jnp.int32).reshape(1)
        self.chip = (2 * lax.axis_index("x") + lax.axis_index("y")).astype(jnp.int32).reshape(1)

    def exchange(self):
        return _pair_exchange_comm(self.gs)

    def sums(self, recv_a):
        self.recv_a = list(recv_a)
        return _chip_exchange_comm([_pair_sum(g, ra, self.core) for g, ra in zip(self.gs, self.recv_a)])

    def finish(self, recv_b):
        return _pair_join([_final_sum(g, ra, rb, self.core, self.chip) for g, ra, rb in zip(self.gs, self.recv_a, recv_b)])

    def run(self):
        recv_a = _run_comm(self.exchange(), "rs_pair_exchange")
        return self.finish(_run_comm(self.sums(recv_a), "rs_chip_exchange"))


def _all_reduce_small(v):
    r, cols = v.shape

    def body(v_ref, out_ref, buf_ref, send_sems, recv_sems):
        x, y, c, _ = _place()
        flip = lambda a, bit: 1 - a if bit else a
        buf_ref[4 * x + 2 * y + c] = v_ref[...]
        cps = []
        for rel in range(1, 8):
            peer = (flip(x, rel & 4), flip(y, rel & 2), flip(c, rel & 1))
            cps.append(pltpu.make_async_remote_copy(
                src_ref=v_ref, dst_ref=buf_ref.at[4 * x + 2 * y + c], send_sem=send_sems.at[rel - 1], recv_sem=recv_sems.at[rel - 1],
                device_id=peer, device_id_type=MESH))
        for cp in cps:
            cp.start()
        for cp in cps:
            cp.wait()
        total = buf_ref[0]
        for d in range(1, 8):
            total = total + buf_ref[d]
        out_ref[...] = total

    vm = pl.BlockSpec(memory_space=pltpu.VMEM)
    return pl.pallas_call(
        body, name="all_reduce_small", in_specs=[vm], out_specs=vm, out_shape=jax.ShapeDtypeStruct((r, cols), F32),
        scratch_shapes=[pltpu.VMEM((8, r, cols), F32), pltpu.SemaphoreType.DMA((7,)), pltpu.SemaphoreType.DMA((7,))],
    )(v)


SHARD_AXIS = {"w_in": 1, "conv_w": 1, "w_proj_conv": 1, "w_proj_fox": 1, "w_proj_sb": 1, "w_out": 0, "w_up": 1,
              "ffn_conv_w": 1, "w_down": 0}
SMALL_SHARDED = ("conv_w", "ffn_conv_w")
BIG = tuple(k for k in SHARD_AXIS if k not in SMALL_SHARDED)
REPLICATED = tuple(k for k in WEIGHTS if k not in SHARD_AXIS)
SMALL = REPLICATED + SMALL_SHARDED


def _pack_small(parts, row_align):
    flat = jnp.concatenate([p.reshape(-1) for p in parts])
    rows = -(-flat.shape[0] // (PACK_COLS * row_align)) * row_align
    return jnp.pad(flat, (0, rows * PACK_COLS - flat.shape[0])).reshape(rows, PACK_COLS)


def _unpack_small(packed, shapes):
    flat = packed.reshape(-1)
    out, off = [], 0
    for shape in shapes:
        size = int(np.prod(shape))
        out.append(flat[off:off + size].reshape(shape))
        off += size
    return out


def _to_full(stacked, axis):
    _, r, c = stacked.shape
    if axis == 0:
        return stacked.reshape(4 * r, c)
    return jnp.moveaxis(stacked, 0, 1).reshape(r, 4 * c)


def _to_chips(full, axis):
    a, b = full.shape
    if axis == 0:
        return full.reshape(4, a // 4, b)
    return jnp.moveaxis(full.reshape(a, 4, b // 4), 1, 0)


class _Plan:
    def __init__(self, given):
        self.given = given
        self.depth = given["norm1_g"].shape[0]
        conv_shapes = [given[k].shape for k in SMALL_SHARDED]
        packed = [_pack_small([given[k] for k in SMALL_SHARDED], 16)]
        (got,) = _own_block(_run_comm(_gather_comm(packed), "gather_conv_weights"), packed)
        by_chip = [_unpack_small(got[j], conv_shapes) for j in range(4)]
        self.conv = {k: jnp.concatenate([by_chip[j][i] for j in range(4)], axis=-1) for i, k in enumerate(SMALL_SHARDED)}
        self.shards = {}

    def _shards(self, l):
        if l not in self.shards:
            self.shards[l] = [self.given[k][l].astype(BF16) for k in BIG]
        return self.shards[l]

    def first_layer(self):
        return self.layer(0, _run_comm(self.gather(0), "gather_weights"))

    def gather(self, l):
        return _gather_comm(self._shards(l))

    def layer(self, l, got):
        p = {k: self.given[k][l] for k in REPLICATED}
        p.update({k: self.conv[k][l] for k in SMALL_SHARDED})
        p.update({k: _to_full(stacked, SHARD_AXIS[k]) for k, stacked in zip(BIG, _own_block(got, self._shards(l)))})
        return p

    def reduction(self, g):
        return _Reduction([_to_chips(g[k], SHARD_AXIS[k]) for k in BIG])


def kernel(x, norm1_g, w_in, fox_f_bias, gate_bias, conv_w, fox_q_norm_g, fox_k_norm_g, w_proj_conv, w_proj_fox, w_proj_sb, w_out, norm2_g, w_up, ffn_conv_w, ffn_conv_b, w_down, loss_target, m_norm1_g, m_w_in, m_fox_f_bias, m_gate_bias, m_conv_w, m_fox_q_norm_g, m_fox_k_norm_g, m_w_proj_conv, m_w_proj_fox, m_w_proj_sb, m_w_out, m_norm2_g, m_w_up, m_ffn_conv_w, m_ffn_conv_b, m_w_down, v_norm1_g, v_w_in, v_fox_f_bias, v_gate_bias, v_conv_w, v_fox_q_norm_g, v_fox_k_norm_g, v_w_proj_conv, v_w_proj_fox, v_w_proj_sb, v_w_out, v_norm2_g, v_w_up, v_ffn_conv_w, v_ffn_conv_b, v_w_down):
    given = dict(x=x, norm1_g=norm1_g, w_in=w_in, fox_f_bias=fox_f_bias, gate_bias=gate_bias, conv_w=conv_w, fox_q_norm_g=fox_q_norm_g, fox_k_norm_g=fox_k_norm_g, w_proj_conv=w_proj_conv, w_proj_fox=w_proj_fox, w_proj_sb=w_proj_sb, w_out=w_out, norm2_g=norm2_g, w_up=w_up, ffn_conv_w=ffn_conv_w, ffn_conv_b=ffn_conv_b, w_down=w_down, loss_target=loss_target, m_norm1_g=m_norm1_g, m_w_in=m_w_in, m_fox_f_bias=m_fox_f_bias, m_gate_bias=m_gate_bias, m_conv_w=m_conv_w, m_fox_q_norm_g=m_fox_q_norm_g, m_fox_k_norm_g=m_fox_k_norm_g, m_w_proj_conv=m_w_proj_conv, m_w_proj_fox=m_w_proj_fox, m_w_proj_sb=m_w_proj_sb, m_w_out=m_w_out, m_norm2_g=m_norm2_g, m_w_up=m_w_up, m_ffn_conv_w=m_ffn_conv_w, m_ffn_conv_b=m_ffn_conv_b, m_w_down=m_w_down, v_norm1_g=v_norm1_g, v_w_in=v_w_in, v_fox_f_bias=v_fox_f_bias, v_gate_bias=v_gate_bias, v_conv_w=v_conv_w, v_fox_q_norm_g=v_fox_q_norm_g, v_fox_k_norm_g=v_fox_k_norm_g, v_w_proj_conv=v_w_proj_conv, v_w_proj_fox=v_w_proj_fox, v_w_proj_sb=v_w_proj_sb, v_w_out=v_w_out, v_norm2_g=v_norm2_g, v_w_up=v_w_up, v_ffn_conv_w=v_ffn_conv_w, v_ffn_conv_b=v_ffn_conv_b, v_w_down=v_w_down)
    depth = given["norm1_g"].shape[0]
    chip = 2 * lax.axis_index("x") + lax.axis_index("y")

    sq, dx, grads = _local_step(given["x"][0], given["loss_target"][0], _Plan(given))
    loss = lax.psum(0.5 * sq[0, 0] / D_MODEL, ("x", "y", "c"))

    gsum = {k: jnp.stack([g[k] for g in grads]) for k in BIG}
    small_shapes = [(depth, *grads[0][k].shape) for k in SMALL]
    summed = _all_reduce_small(_pack_small([jnp.stack([g[k] for g in grads]) for k in SMALL], 8))
    for k, total in zip(SMALL, _unpack_small(summed, small_shapes)):
        if k in SMALL_SHARDED:
            total = lax.dynamic_index_in_dim(total.reshape(*total.shape[:-1], 4, total.shape[-1] // 4), chip, axis=2, keepdims=False)
        gsum[k] = total

    deltas, new_m, new_v = {}, {}, {}
    for k in WEIGHTS:
        deltas[k], new_m[k], new_v[k] = _adamw(given[k], gsum[k], given["m_" + k], given["v_" + k], "adamw_" + k)
    return (loss, dx[None], *[gsum[k] for k in WEIGHTS], *[deltas[k] for k in WEIGHTS],
            *[new_m[k] for k in WEIGHTS], *[new_v[k] for k in WEIGHTS])
```

```python
import collections

import numpy as np
import jax
import jax.numpy as jnp
from jax import lax
from jax.experimental import pallas as pl
from jax.experimental.pallas import tpu as pltpu

F32 = jnp.float32
BF16 = jnp.bfloat16

D_MODEL = 1024
DEPTH = 4
HEAD_DIM = 64
N_HEADS = 8
WIDTH = 512
D_FF = 2816
NORM_EPS = 1e-6
NEG_INF = -1e30
QK_SCALE = HEAD_DIM ** -0.5
LANES = 128
N_PAIR = N_HEADS // 2

GATE_OFF = 0
CONV_OFF = 3 * D_MODEL
FOX_OFF = CONV_OFF + 3 * WIDTH
SB_OFF = FOX_OFF + 3 * WIDTH
D_INA = SB_OFF + 3 * WIDTH

ADAM_LR = 0.001
ADAM_B1 = 0.9
ADAM_B2 = 0.999
ADAM_EPS = 1e-08
ADAM_WD = 0.01
ADAM_STEP = 10

VMEM_LIMIT = 48 * 1024 * 1024
ADAMW_BLOCK_BYTES = 1024 * 1024

PACK_COLS = 1024
PACK_ROW_ALIGN = 32
MESH = pl.DeviceIdType.MESH


def _params(sem):
    return pltpu.CompilerParams(dimension_semantics=sem, vmem_limit_bytes=VMEM_LIMIT)


def _dot(a, b, dims):
    return lax.dot_general(a, b, (dims, ((), ())), preferred_element_type=F32)


_NN = ((1,), (0,))
_NT = ((1,), (1,))
_TN = ((0,), (0,))


def _pick(dim, pref):
    if dim <= pref:
        return dim
    best = None
    for mult in range(1, dim // LANES + 1):
        t = mult * LANES
        if t <= pref and dim % t == 0:
            best = t
    assert best is not None, (dim, pref)
    return best


def _mm(a, b, mode, out_dtype=F32, tm=1024, tn=1024, tk=2048, res=None, name="mm"):
    if mode == "nn":
        (m, k), (_, n) = a.shape, b.shape
    elif mode == "nt":
        (m, k), (n, _) = a.shape, b.shape
    else:
        (k, m), (_, n) = a.shape, b.shape
    tm, tn, tk = _pick(m, tm), _pick(n, tn), _pick(k, tk)
    nk = k // tk
    dims = {"nn": _NN, "nt": _NT, "tn": _TN}[mode]
    if mode == "tn":
        a_spec = pl.BlockSpec((tk, tm), lambda i, j, kk: (kk, i))
    else:
        a_spec = pl.BlockSpec((tm, tk), lambda i, j, kk: (i, kk))
    if mode == "nt":
        b_spec = pl.BlockSpec((tn, tk), lambda i, j, kk: (j, kk))
    else:
        b_spec = pl.BlockSpec((tk, tn), lambda i, j, kk: (kk, j))
    o_spec = pl.BlockSpec((tm, tn), lambda i, j, kk: (i, j))
    in_specs = [a_spec, b_spec] + ([o_spec] if res is not None else [])

    def body(*refs):
        if res is not None:
            a_ref, b_ref, r_ref, o_ref, acc_ref = refs
        else:
            a_ref, b_ref, o_ref, acc_ref = refs
            r_ref = None
        kk = pl.program_id(2)
        part = _dot(a_ref[...].astype(BF16), b_ref[...].astype(BF16), dims)

        def finish(total):
            if r_ref is not None:
                total = total + r_ref[...].astype(F32)
            o_ref[...] = total.astype(out_dtype)

        if nk == 1:
            finish(part)
        else:
            @pl.when(kk == 0)
            def _():
                acc_ref[...] = part

            @pl.when(kk > 0)
            def _():
                acc_ref[...] += part

            @pl.when(kk == nk - 1)
            def _():
                finish(acc_ref[...])

    args = (a, b) + ((res,) if res is not None else ())
    return pl.pallas_call(
        body, name=name, grid=(m // tm, n // tn, nk), in_specs=in_specs, out_specs=o_spec,
        out_shape=jax.ShapeDtypeStruct((m, n), out_dtype),
        scratch_shapes=[pltpu.VMEM((tm, tn) if nk > 1 else (8, LANES), F32)],
        compiler_params=_params(("parallel", "parallel", "arbitrary")),
    )(*args)


def _norm_mm(x, g, w, out_dtype, tm=1024, tn=1536, name="norm_mm"):
    m, d = x.shape
    n = w.shape[1]
    tm, tn = _pick(m, tm), _pick(n, tn)

    def body(x_ref, g_ref, w_ref, o_ref, hn_ref):
        @pl.when(pl.program_id(1) == 0)
        def _():
            xf = x_ref[...]
            r = lax.rsqrt(jnp.mean(xf * xf, axis=-1, keepdims=True) + NORM_EPS)
            hn_ref[...] = (xf * r * g_ref[...]).astype(BF16)

        o_ref[...] = _dot(hn_ref[...], w_ref[...], _NN).astype(out_dtype)

    return pl.pallas_call(
        body, name=name, grid=(m // tm, n // tn),
        in_specs=[pl.BlockSpec((tm, d), lambda i, j: (i, 0)),
                  pl.BlockSpec((1, d), lambda i, j: (0, 0)),
                  pl.BlockSpec((d, tn), lambda i, j: (0, j))],
        out_specs=[pl.BlockSpec((tm, tn), lambda i, j: (i, j)),
                   pl.BlockSpec((tm, d), lambda i, j: (i, 0))],
        out_shape=[jax.ShapeDtypeStruct((m, n), out_dtype), jax.ShapeDtypeStruct((m, d), BF16)],
        compiler_params=_params(("parallel", "arbitrary")),
    )(x, g.reshape(1, d), w)


def _norm_bwd(dhn, x, g, dx_in, tm=256, name="norm_bwd"):
    m, d = x.shape
    tm = min(tm, m)

    def body(dhn_ref, x_ref, g_ref, dxi_ref, dx_ref, gg_ref):
        xf = x_ref[...]
        r = lax.rsqrt(jnp.mean(xf * xf, axis=-1, keepdims=True) + NORM_EPS)
        xhat = xf * r
        dh = dhn_ref[...].astype(F32)
        dxn = dh * g_ref[...]
        mean = jnp.mean(dxn * xhat, axis=-1, keepdims=True)
        dx_ref[...] = dxi_ref[...] + r * (dxn - xhat * mean)
        part = jnp.sum(dh * xhat, axis=0, keepdims=True)

        @pl.when(pl.program_id(0) == 0)
        def _():
            gg_ref[...] = part

        @pl.when(pl.program_id(0) > 0)
        def _():
            gg_ref[...] += part

    row = pl.BlockSpec((tm, d), lambda i: (i, 0))
    vec = pl.BlockSpec((1, d), lambda i: (0, 0))
    dx, gg = pl.pallas_call(
        body, name=name, grid=(m // tm,), in_specs=[row, row, vec, row], out_specs=[row, vec],
        out_shape=[jax.ShapeDtypeStruct((m, d), F32), jax.ShapeDtypeStruct((1, d), F32)],
        compiler_params=_params(("arbitrary",)),
    )(dhn, x, g.reshape(1, d), dx_in)
    return dx, gg.reshape(d)


def _down(u, k):
    s = u.shape[0]
    rows = lax.broadcasted_iota(jnp.int32, u.shape, 0)
    return jnp.where(rows < k, 0.0, pltpu.roll(u, k, axis=0))


def _up(u, k):
    s = u.shape[0]
    rows = lax.broadcasted_iota(jnp.int32, u.shape, 0)
    return jnp.where(rows >= s - k, 0.0, pltpu.roll(u, s - k, axis=0))


def _conv_mix_fwd(proj, conv_w, tc=128):
    s = proj.shape[0]
    nb = WIDTH // tc
    off = CONV_OFF // tc

    def body(b_ref, c_ref, h_ref, w_ref, o_ref):
        u = c_ref[...].astype(F32) * h_ref[...].astype(F32)
        w = w_ref[...]
        cv = w[0:1] * _down(u, 2) + w[1:2] * _down(u, 1) + w[2:3] * u
        o_ref[...] = (b_ref[...].astype(F32) * cv).astype(BF16)

    col = lambda k: pl.BlockSpec((s, tc), lambda j, k=k: (0, off + k * nb + j))
    return pl.pallas_call(
        body, name="conv_mix_fwd", grid=(nb,),
        in_specs=[col(0), col(1), col(2), pl.BlockSpec((3, tc), lambda j: (0, j))],
        out_specs=pl.BlockSpec((s, tc), lambda j: (0, j)),
        out_shape=jax.ShapeDtypeStruct((s, WIDTH), BF16),
        compiler_params=_params(("parallel",)),
    )(proj, proj, proj, conv_w)


def _conv_mix_bwd(do, proj, conv_w, tc=128):
    s = proj.shape[0]
    nb = WIDTH // tc
    off = CONV_OFF // tc

    def body(do_ref, b_ref, c_ref, h_ref, w_ref, db_ref, dc_ref, dh_ref, gw_ref):
        b = b_ref[...].astype(F32)
        c = c_ref[...].astype(F32)
        h = h_ref[...].astype(F32)
        g = do_ref[...].astype(F32)
        w = w_ref[...]
        u = c * h
        u1, u2 = _down(u, 1), _down(u, 2)
        cv = w[0:1] * u2 + w[1:2] * u1 + w[2:3] * u
        db_ref[...] = (g * cv).astype(BF16)
        dcv = g * b
        gw_ref[0:1, :] = jnp.sum(dcv * u2, axis=0, keepdims=True)
        gw_ref[1:2, :] = jnp.sum(dcv * u1, axis=0, keepdims=True)
        gw_ref[2:3, :] = jnp.sum(dcv * u, axis=0, keepdims=True)
        du = w[2:3] * dcv + w[1:2] * _up(dcv, 1) + w[0:1] * _up(dcv, 2)
        dc_ref[...] = (du * h).astype(BF16)
        dh_ref[...] = (du * c).astype(BF16)

    col = lambda k: pl.BlockSpec((s, tc), lambda j, k=k: (0, off + k * nb + j))
    own = pl.BlockSpec((s, tc), lambda j: (0, j))
    wsp = pl.BlockSpec((3, tc), lambda j: (0, j))
    act = jax.ShapeDtypeStruct((s, WIDTH), BF16)
    return pl.pallas_call(
        body, name="conv_mix_bwd", grid=(nb,),
        in_specs=[own, col(0), col(1), col(2), wsp], out_specs=[own, own, own, wsp],
        out_shape=[act, act, act, jax.ShapeDtypeStruct((3, WIDTH), F32)],
        compiler_params=_params(("parallel",)),
    )(do, proj, proj, proj, conv_w)


def _glu_fwd(up, w, b, tc=256):
    s = up.shape[0]
    nb = D_FF // tc

    def body(g_ref, v_ref, w_ref, b_ref, o_ref):
        ug = g_ref[...].astype(F32)
        wv = w_ref[...]
        cg = wv[0:1] * _down(ug, 2) + wv[1:2] * _down(ug, 1) + wv[2:3] * ug + b_ref[...]
        act = cg * jax.nn.sigmoid(cg)
        o_ref[...] = (act * v_ref[...].astype(F32)).astype(BF16)

    return pl.pallas_call(
        body, name="glu_fwd", grid=(nb,),
        in_specs=[pl.BlockSpec((s, tc), lambda j: (0, j)), pl.BlockSpec((s, tc), lambda j: (0, nb + j)),
                  pl.BlockSpec((3, tc), lambda j: (0, j)), pl.BlockSpec((1, tc), lambda j: (0, j))],
        out_specs=pl.BlockSpec((s, tc), lambda j: (0, j)),
        out_shape=jax.ShapeDtypeStruct((s, D_FF), BF16),
        compiler_params=_params(("parallel",)),
    )(up, up, w, b.reshape(1, D_FF))


def _glu_bwd(dh, up, w, b, tc=256):
    s = up.shape[0]
    nb = D_FF // tc

    def body(dh_ref, g_ref, v_ref, w_ref, b_ref, dg_ref, dv_ref, gw_ref, gb_ref):
        ug = g_ref[...].astype(F32)
        uv = v_ref[...].astype(F32)
        d = dh_ref[...].astype(F32)
        wv = w_ref[...]
        u1, u2 = _down(ug, 1), _down(ug, 2)
        cg = wv[0:1] * u2 + wv[1:2] * u1 + wv[2:3] * ug + b_ref[...]
        sg = jax.nn.sigmoid(cg)
        dv_ref[...] = (d * (cg * sg)).astype(BF16)
        dcg = d * uv * (sg * (1.0 + cg * (1.0 - sg)))
        gb_ref[...] = jnp.sum(dcg, axis=0, keepdims=True)
        gw_ref[0:1, :] = jnp.sum(dcg * u2, axis=0, keepdims=True)
        gw_ref[1:2, :] = jnp.sum(dcg * u1, axis=0, keepdims=True)
        gw_ref[2:3, :] = jnp.sum(dcg * ug, axis=0, keepdims=True)
        dg_ref[...] = (wv[2:3] * dcg + wv[1:2] * _up(dcg, 1) + wv[0:1] * _up(dcg, 2)).astype(BF16)

    own = pl.BlockSpec((s, tc), lambda j: (0, j))
    wsp = pl.BlockSpec((3, tc), lambda j: (0, j))
    bsp = pl.BlockSpec((1, tc), lambda j: (0, j))
    act = jax.ShapeDtypeStruct((s, D_FF), BF16)
    dg, dv, gw, gb = pl.pallas_call(
        body, name="glu_bwd", grid=(nb,),
        in_specs=[own, own, pl.BlockSpec((s, tc), lambda j: (0, nb + j)), wsp, bsp],
        out_specs=[own, own, wsp, bsp],
        out_shape=[act, act, jax.ShapeDtypeStruct((3, D_FF), F32), jax.ShapeDtypeStruct((1, D_FF), F32)],
        compiler_params=_params(("parallel",)),
    )(dh, up, up, w, b.reshape(1, D_FF))
    return dg, dv, gw, gb.reshape(D_FF)


def _merge_fwd(x, oc, of, osb, wpc, wpf, wps, proj, gb, wout, tm=256):
    s, d = x.shape
    tm = min(tm, s)

    def body(x_ref, oc_ref, of_ref, os_ref, wpc_ref, wpf_ref, wps_ref, g0_ref, g1_ref, g2_ref, gb_ref, wo_ref,
             xo_ref, mg_ref, yc_ref, yf_ref, ys_ref):
        merged = jnp.zeros((tm, d), F32)
        for k, (o_ref, w_ref, g_ref, y_ref) in enumerate(
                ((oc_ref, wpc_ref, g0_ref, yc_ref), (of_ref, wpf_ref, g1_ref, yf_ref), (os_ref, wps_ref, g2_ref, ys_ref))):
            y = _dot(o_ref[...].astype(BF16), w_ref[...], _NN)
            y_ref[...] = y.astype(BF16)
            gate = jax.nn.sigmoid(g_ref[...].astype(F32) + gb_ref[:, k * d:(k + 1) * d])
            merged = merged + gate * y
        mb = merged.astype(BF16)
        mg_ref[...] = mb
        xo_ref[...] = x_ref[...] + _dot(mb, wo_ref[...], _NN)

    rowd = pl.BlockSpec((tm, d), lambda i: (i, 0))
    roww = pl.BlockSpec((tm, WIDTH), lambda i: (i, 0))
    wp = pl.BlockSpec((WIDTH, d), lambda i: (0, 0))
    gcol = lambda k: pl.BlockSpec((tm, d), lambda i, k=k: (i, GATE_OFF // d + k))
    actd = jax.ShapeDtypeStruct((s, d), BF16)
    return pl.pallas_call(
        body, name="merge_fwd", grid=(s // tm,),
        in_specs=[rowd, roww, roww, roww, wp, wp, wp, gcol(0), gcol(1), gcol(2),
                  pl.BlockSpec((1, 3 * d), lambda i: (0, 0)), pl.BlockSpec((d, d), lambda i: (0, 0))],
        out_specs=[rowd, rowd, rowd, rowd, rowd],
        out_shape=[jax.ShapeDtypeStruct((s, d), F32), actd, actd, actd, actd],
        compiler_params=_params(("parallel",)),
    )(x, oc, of, osb, wpc, wpf, wps, proj, proj, proj, gb.reshape(1, 3 * d), wout)


def _gate_bwd(dm, yc, yf, ys, proj, gb, tm=256):
    s, d = dm.shape
    tm = min(tm, s)

    def body(dm_ref, yc_ref, yf_ref, ys_ref, g0_ref, g1_ref, g2_ref, gb_ref, dyc_ref, dyf_ref, dys_ref, dgl_ref, ggb_ref):
        g = dm_ref[...].astype(F32)
        parts = []
        for k, (y_ref, g_ref, dy_ref) in enumerate(((yc_ref, g0_ref, dyc_ref), (yf_ref, g1_ref, dyf_ref), (ys_ref, g2_ref, dys_ref))):
            gate = jax.nn.sigmoid(g_ref[...].astype(F32) + gb_ref[:, k * d:(k + 1) * d])
            dy_ref[...] = (g * gate).astype(BF16)
            dgl = g * y_ref[...].astype(F32) * gate * (1.0 - gate)
            dgl_ref[:, k * d:(k + 1) * d] = dgl.astype(BF16)
            parts.append(jnp.sum(dgl, axis=0, keepdims=True))
        part = jnp.concatenate(parts, axis=1)

        @pl.when(pl.program_id(0) == 0)
        def _():
            ggb_ref[...] = part

        @pl.when(pl.program_id(0) > 0)
        def _():
            ggb_ref[...] += part

    rowd = pl.BlockSpec((tm, d), lambda i: (i, 0))
    gcol = lambda k: pl.BlockSpec((tm, d), lambda i, k=k: (i, GATE_OFF // d + k))
    vec = pl.BlockSpec((1, 3 * d), lambda i: (0, 0))
    actd = jax.ShapeDtypeStruct((s, d), BF16)
    dyc, dyf, dys, dgl, ggb = pl.pallas_call(
        body, name="gate_bwd", grid=(s // tm,),
        in_specs=[rowd, rowd, rowd, rowd, gcol(0), gcol(1), gcol(2), vec],
        out_specs=[rowd, rowd, rowd, pl.BlockSpec((tm, 3 * d), lambda i: (i, 0)), vec],
        out_shape=[actd, actd, actd, jax.ShapeDtypeStruct((s, 3 * d), BF16), jax.ShapeDtypeStruct((1, 3 * d), F32)],
        compiler_params=_params(("arbitrary",)),
    )(dm, yc, yf, ys, proj, proj, proj, gb.reshape(1, 3 * d))
    return dyc, dyf, dys, dgl, ggb.reshape(3 * d)


def _tri_tables(nq, r, order):
    last = lambda qi: (qi + 1) * r - 1
    if order == "k_outer":
        pairs = [(qi, kj) for kj in range(nq * r) for qi in range(kj // r, nq)]
    elif order == "k_desc":
        pairs = [(qi, kj) for qi in range(nq) for kj in range(last(qi), -1, -1)]
    else:
        pairs = [(qi, kj) for qi in range(nq) for kj in range(last(qi) + 1)]
    qs, ks = zip(*pairs)
    return jnp.asarray(np.array(qs, np.int32)), jnp.asarray(np.array(ks, np.int32)), len(pairs)


def _lo_mask(shape):
    return lax.broadcasted_iota(jnp.int32, shape, len(shape) - 1) < HEAD_DIM


def _head(x, h):
    lo = _lo_mask(x.shape)
    return jnp.where(lo if h == 0 else jnp.logical_not(lo), x, jnp.zeros_like(x))


def _pair_cols(x, h):
    return x[:, h * HEAD_DIM:h * HEAD_DIM + 1]


def _rep(a0, a1, shape):
    return jnp.where(_lo_mask(shape), a0, a1)


def _positions(qi, kj, tq, tk):
    row = qi * tq + lax.broadcasted_iota(jnp.int32, (tq, tk), 0)
    col = kj * tk + lax.broadcasted_iota(jnp.int32, (tq, tk), 1)
    return row, col


def _head_norm(x, g):
    lo = _lo_mask(x.shape)
    sq = x * x
    s0 = jnp.sum(jnp.where(lo, sq, 0.0), axis=-1, keepdims=True)
    s1 = jnp.sum(jnp.where(lo, 0.0, sq), axis=-1, keepdims=True)
    r = jnp.where(lo, lax.rsqrt(s0 / HEAD_DIM + NORM_EPS), lax.rsqrt(s1 / HEAD_DIM + NORM_EPS))
    return x * r, r


def _fox_prep(proj, gq, gk, tm=512):
    s = proj.shape[0]
    tm = min(tm, s)
    off = FOX_OFF // LANES

    def body(q_ref, k_ref, gq_ref, gk_ref, qn_ref, kn_ref):
        qh, _ = _head_norm(q_ref[...].astype(F32), None)
        kh, _ = _head_norm(k_ref[...].astype(F32), None)
        qn_ref[...] = (qh * gq_ref[...] * QK_SCALE).astype(BF16)
        kn_ref[...] = (kh * gk_ref[...]).astype(BF16)

    vec = pl.BlockSpec((1, LANES), lambda p, i: (0, 0))
    own = pl.BlockSpec((tm, LANES), lambda p, i: (i, p))
    act = jax.ShapeDtypeStruct((s, WIDTH), BF16)
    return pl.pallas_call(
        body, name="fox_prep", grid=(N_PAIR, s // tm),
        in_specs=[pl.BlockSpec((tm, LANES), lambda p, i: (i, off + p)),
                  pl.BlockSpec((tm, LANES), lambda p, i: (i, off + N_PAIR + p)), vec, vec],
        out_specs=[own, own], out_shape=[act, act],
        compiler_params=_params(("parallel", "parallel")),
    )(proj, proj, jnp.tile(gq, 2).reshape(1, LANES), jnp.tile(gk, 2).reshape(1, LANES))


def _fox_post(dqs, dkn, proj, gq, gk, tm=512):
    s = proj.shape[0]
    tm = min(tm, s)
    off = FOX_OFF // LANES

    def one(d_ref, x_ref, g_ref, scale, dx_ref, gg_ref, first):
        xhat, r = _head_norm(x_ref[...].astype(F32), None)
        dy = d_ref[...] * scale
        part = jnp.sum(dy * xhat, axis=0, keepdims=True)

        @pl.when(first)
        def _():
            gg_ref[...] = part

        @pl.when(jnp.logical_not(first))
        def _():
            gg_ref[...] += part

        dxh = dy * g_ref[...]
        lo = _lo_mask(dxh.shape)
        pr = dxh * xhat
        m0 = jnp.sum(jnp.where(lo, pr, 0.0), axis=-1, keepdims=True)
        m1 = jnp.sum(jnp.where(lo, 0.0, pr), axis=-1, keepdims=True)
        mean = jnp.where(lo, m0, m1) / HEAD_DIM
        dx_ref[...] = (r * (dxh - xhat * mean)).astype(BF16)

    def body(dq_ref, dk_ref, q_ref, k_ref, gq_ref, gk_ref, dxq_ref, dxk_ref, ggq_ref, ggk_ref):
        first = pl.program_id(1) == 0
        one(dq_ref, q_ref, gq_ref, QK_SCALE, dxq_ref, ggq_ref, first)
        one(dk_ref, k_ref, gk_ref, 1.0, dxk_ref, ggk_ref, first)

    vec = pl.BlockSpec((1, LANES), lambda p, i: (0, 0))
    own = pl.BlockSpec((tm, LANES), lambda p, i: (i, p))
    ggs = pl.BlockSpec((None, 1, LANES), lambda p, i: (p, 0, 0))
    act = jax.ShapeDtypeStruct((s, WIDTH), BF16)
    ggo = jax.ShapeDtypeStruct((N_PAIR, 1, LANES), F32)
    dxq, dxk, ggq, ggk = pl.pallas_call(
        body, name="fox_post", grid=(N_PAIR, s // tm),
        in_specs=[own, own, pl.BlockSpec((tm, LANES), lambda p, i: (i, off + p)),
                  pl.BlockSpec((tm, LANES), lambda p, i: (i, off + N_PAIR + p)), vec, vec],
        out_specs=[own, own, ggs, ggs], out_shape=[act, act, ggo, ggo],
        compiler_params=_params(("parallel", "arbitrary")),
    )(dqs, dkn, proj, proj, jnp.tile(gq, 2).reshape(1, LANES), jnp.tile(gk, 2).reshape(1, LANES))
    fold = lambda a: a.reshape(N_HEADS, HEAD_DIM).sum(axis=0)
    return dxq, dxk, fold(ggq), fold(ggk)


def _split3(x):
    a = x.astype(BF16)
    r = x - a.astype(F32)
    b = r.astype(BF16)
    c = (r - b.astype(F32)).astype(BF16)
    return a, b, c


def _split2(x):
    a = x.astype(BF16)
    b = (x - a.astype(F32)).astype(BF16)
    return a, b


def _log_sigmoid(x):
    return jnp.minimum(x, 0.0) - jnp.log(1.0 + jnp.exp(-jnp.abs(x)))


def _fox_gates(ft, bias):
    h, s = ft.shape
    nb = s // LANES

    def body(f_ref, b_ref, c_ref):
        lf = _log_sigmoid(f_ref[...] + b_ref[...])
        i = lax.broadcasted_iota(jnp.int32, (s, LANES), 0)
        j = pl.program_id(0) * LANES + lax.broadcasted_iota(jnp.int32, (s, LANES), 1)
        tri = jnp.where(i <= j, 1.0, 0.0).astype(BF16)
        c_ref[...] = sum(_dot(p, tri, _NN) for p in _split3(lf))

    return pl.pallas_call(
        body, name="fox_gates", grid=(nb,),
        in_specs=[pl.BlockSpec((h, s), lambda j: (0, 0)), pl.BlockSpec((h, 1), lambda j: (0, 0))],
        out_specs=pl.BlockSpec((h, LANES), lambda j: (0, j)),
        out_shape=jax.ShapeDtypeStruct((h, s), F32),
        compiler_params=_params(("parallel",)),
    )(ft, bias.reshape(h, 1))


def _fox_gates_bwd(dc, ft, bias):
    h, s = ft.shape
    nb = s // LANES

    def body(dc_ref, f_ref, fb_ref, b_ref, df_ref, gb_ref):
        i = lax.broadcasted_iota(jnp.int32, (s, LANES), 0)
        j = pl.program_id(0) * LANES + lax.broadcasted_iota(jnp.int32, (s, LANES), 1)
        tri = jnp.where(i >= j, 1.0, 0.0).astype(BF16)
        dlf = sum(_dot(p, tri, _NN) for p in _split3(dc_ref[...]))
        df = dlf * jax.nn.sigmoid(-(fb_ref[...] + b_ref[...]))
        df_ref[...] = df
        part = jnp.sum(df, axis=-1, keepdims=True)

        @pl.when(pl.program_id(0) == 0)
        def _():
            gb_ref[...] = part

        @pl.when(pl.program_id(0) > 0)
        def _():
            gb_ref[...] += part

    full = pl.BlockSpec((h, s), lambda j: (0, 0))
    blk = pl.BlockSpec((h, LANES), lambda j: (0, j))
    one = pl.BlockSpec((h, 1), lambda j: (0, 0))
    df, gb = pl.pallas_call(
        body, name="fox_gates_bwd", grid=(nb,), in_specs=[full, full, blk, one], out_specs=[blk, one],
        out_shape=[jax.ShapeDtypeStruct((h, s), F32), jax.ShapeDtypeStruct((h, 1), F32)],
        compiler_params=_params(("arbitrary",)),
    )(dc, ft, ft, bias.reshape(h, 1))
    return df, gb.reshape(h)


def _delta_rep(do, o, tm=512):
    s = do.shape[0]
    tm = min(tm, s)

    def body(do_ref, o_ref, d_ref):
        pr = do_ref[...].astype(F32) * o_ref[...].astype(F32)
        lo = _lo_mask(pr.shape)
        d0 = jnp.sum(jnp.where(lo, pr, 0.0), axis=-1, keepdims=True)
        d1 = jnp.sum(jnp.where(lo, 0.0, pr), axis=-1, keepdims=True)
        d_ref[...] = jnp.where(lo, d0, d1)

    own = pl.BlockSpec((tm, LANES), lambda p, i: (i, p))
    return pl.pallas_call(
        body, name="delta_rep", grid=(N_PAIR, s // tm), in_specs=[own, own],
        out_specs=pl.BlockSpec((None, tm, LANES), lambda p, i: (p, i, 0)),
        out_shape=jax.ShapeDtypeStruct((N_PAIR, s, LANES), F32),
        compiler_params=_params(("parallel", "parallel")),
    )(do, o)


def _tile(s, t):
    t = min(t, s)
    assert s % t == 0
    return t


def _fox_fwd(qn, kn, proj, ccol, crow, t=512, comm=None):
    s = qn.shape[0]
    t = _tile(s, t)
    n = s // t
    qtab, ktab, ntri = _tri_tables(n, 1, "k_asc")
    voff = FOX_OFF // LANES + 2 * N_PAIR
    n_in, n_out = (len(comm.inputs), len(comm.out_shapes)) if comm else (0, 0)

    def body(qt_ref, kt_ref, q_ref, k_ref, v_ref, cc_ref, cr_ref, *rest):
        comm_in, rest = rest[:n_in], rest[n_in:]
        (o_ref, lse_ref), rest = rest[:2], rest[2:]
        comm_out, rest = rest[:n_out], rest[n_out:]
        (m_ref, l_ref, acc_ref), sems = rest[:3], rest[3:]
        i = pl.program_id(1)
        qi, kj = qt_ref[i], kt_ref[i]
        if comm:
            pl.when((pl.program_id(0) == 0) & (i == 0))(lambda: comm.start(comm_in, comm_out, sems))

        @pl.when(kj == 0)
        def _():
            m_ref[...] = jnp.full(m_ref.shape, NEG_INF, F32)
            l_ref[...] = jnp.zeros(l_ref.shape, F32)
            acc_ref[...] = jnp.zeros(acc_ref.shape, F32)

        q, k, v = q_ref[...], k_ref[...], v_ref[...]
        row, col = _positions(qi, kj, t, t)
        causal = col <= row
        m_old = m_ref[...]
        mn, rs, pv = [], [], []
        for h in range(2):
            sc = _dot(_head(q, h), k, _NT) + _pair_cols(cc_ref[...], h) - cr_ref[h:h + 1, :]
            sc = jnp.where(causal, sc, NEG_INF)
            m_new = jnp.maximum(_pair_cols(m_old, h), jnp.max(sc, axis=-1, keepdims=True))
            p = jnp.exp(sc - m_new)
            mn.append(m_new)
            rs.append(jnp.sum(p, axis=-1, keepdims=True))
            pv.append(_dot(p.astype(BF16), _head(v, h), _NN))
        m_rep = _rep(mn[0], mn[1], m_old.shape)
        alpha = jnp.exp(m_old - m_rep)
        l_ref[...] = alpha * l_ref[...] + _rep(rs[0], rs[1], m_old.shape)
        acc_ref[...] = alpha * acc_ref[...] + pv[0] + pv[1]
        m_ref[...] = m_rep

        @pl.when(kj == qi)
        def _():
            o_ref[...] = acc_ref[...] / l_ref[...]
            lse_ref[...] = m_ref[...] + jnp.log(l_ref[...])

        if comm:
            pl.when((pl.program_id(0) == N_PAIR - 1) & (i == ntri - 1))(lambda: comm.finish(comm_in, comm_out, sems))

    grid_spec = pltpu.PrefetchScalarGridSpec(
        num_scalar_prefetch=2, grid=(N_PAIR, ntri),
        in_specs=[pl.BlockSpec((t, LANES), lambda p, i, qt, kt: (qt[i], p)),
                  pl.BlockSpec((t, LANES), lambda p, i, qt, kt: (kt[i], p)),
                  pl.BlockSpec((t, LANES), lambda p, i, qt, kt: (kt[i], voff + p)),
                  pl.BlockSpec((None, t, LANES), lambda p, i, qt, kt: (p, qt[i], 0)),
                  pl.BlockSpec((None, 2, t), lambda p, i, qt, kt: (p, 0, kt[i]))] + [ANY] * n_in,
        out_specs=[pl.BlockSpec((t, LANES), lambda p, i, qt, kt: (qt[i], p)),
                   pl.BlockSpec((None, t, LANES), lambda p, i, qt, kt: (p, qt[i], 0))] + [ANY] * n_out,
        scratch_shapes=[pltpu.VMEM((t, LANES), F32)] * 3 + (comm.sems if comm else []))
    outs = pl.pallas_call(
        body, name="fox_fwd", grid_spec=grid_spec,
        out_shape=[jax.ShapeDtypeStruct((s, WIDTH), F32), jax.ShapeDtypeStruct((N_PAIR, s, LANES), F32)]
        + (comm.out_shapes if comm else []),
        compiler_params=_params(("arbitrary", "arbitrary") if comm else ("parallel", "arbitrary")),
    )(qtab, ktab, qn, kn, proj, ccol, crow, *(comm.inputs if comm else []))
    return outs[:2], outs[2:]


def _fox_bwd(qn, kn, proj, do, lse, delta, ccol, crow, t=512, comm=None):
    s = qn.shape[0]
    t = _tile(s, t)
    n = s // t
    qtab, ktab, ntri = _tri_tables(n, 1, "k_outer")
    voff = FOX_OFF // LANES + 2 * N_PAIR
    n_in, n_out = (len(comm.inputs), len(comm.out_shapes)) if comm else (0, 0)

    def body(qt_ref, kt_ref, q_ref, k_ref, v_ref, do_ref, lse_ref, dl_ref, cc_ref, cr_ref, *rest):
        comm_in, rest = rest[:n_in], rest[n_in:]
        (dq_ref, dk_ref, dv_ref, dc_ref, dcq_ref), rest = rest[:5], rest[5:]
        comm_out, rest = rest[:n_out], rest[n_out:]
        (dka_ref, dva_ref, dca_ref), sems = rest[:3], rest[3:]
        i = pl.program_id(1)
        qi, kj = qt_ref[i], kt_ref[i]
        if comm:
            pl.when((pl.program_id(0) == 0) & (i == 0))(lambda: comm.start(comm_in, comm_out, sems))

        @pl.when(i == 0)
        def _():
            dq_ref[...] = jnp.zeros(dq_ref.shape, F32)
            dcq_ref[...] = jnp.zeros(dcq_ref.shape, F32)

        @pl.when(qi == kj)
        def _():
            dka_ref[...] = jnp.zeros(dka_ref.shape, F32)
            dva_ref[...] = jnp.zeros(dva_ref.shape, F32)
            dca_ref[...] = jnp.zeros(dca_ref.shape, F32)

        q, k, v, g = q_ref[...], k_ref[...], v_ref[...], do_ref[...]
        row, col = _positions(qi, kj, t, t)
        causal = col <= row
        dq = jnp.zeros((t, LANES), F32)
        dk = jnp.zeros((t, LANES), F32)
        dv = jnp.zeros((t, LANES), F32)
        rowsum = []
        for h in range(2):
            qh, gh = _head(q, h), _head(g, h)
            sc = _dot(qh, k, _NT) + _pair_cols(cc_ref[...], h) - cr_ref[h:h + 1, :]
            p = jnp.where(causal, jnp.exp(sc - _pair_cols(lse_ref[...], h)), 0.0)
            dp = _dot(gh, v, _NT)
            ds = p * (dp - _pair_cols(dl_ref[...], h))
            dsb = ds.astype(BF16)
            dv = dv + _dot(p.astype(BF16), gh, _TN)
            dk = dk + _dot(dsb, qh, _TN)
            dq = dq + _dot(dsb, _head(k, h), _NN)
            dca_ref[h:h + 1, :] -= jnp.sum(ds, axis=0, keepdims=True)
            rowsum.append(jnp.sum(ds, axis=-1, keepdims=True))
        dka_ref[...] += dk
        dva_ref[...] += dv
        rows = pl.ds(pl.multiple_of(qi * t, t), t)
        dq_ref[rows, :] += dq
        dcq_ref[rows, :] += _rep(rowsum[0], rowsum[1], (t, LANES))

        @pl.when(qi == n - 1)
        def _():
            dk_ref[...] = dka_ref[...]
            dv_ref[...] = dva_ref[...].astype(BF16)
            dc_ref[...] = dca_ref[...]

        if comm:
            pl.when((pl.program_id(0) == N_PAIR - 1) & (i == ntri - 1))(lambda: comm.finish(comm_in, comm_out, sems))

    qblk = lambda p, i, qt, kt: (qt[i], p)
    kblk = lambda p, i, qt, kt: (kt[i], p)
    qrep = pl.BlockSpec((None, t, LANES), lambda p, i, qt, kt: (p, qt[i], 0))
    crs = pl.BlockSpec((None, 2, t), lambda p, i, qt, kt: (p, 0, kt[i]))
    grid_spec = pltpu.PrefetchScalarGridSpec(
        num_scalar_prefetch=2, grid=(N_PAIR, ntri),
        in_specs=[pl.BlockSpec((t, LANES), qblk), pl.BlockSpec((t, LANES), kblk),
                  pl.BlockSpec((t, LANES), lambda p, i, qt, kt: (kt[i], voff + p)),
                  pl.BlockSpec((t, LANES), qblk), qrep, qrep, qrep, crs] + [ANY] * n_in,
        out_specs=[pl.BlockSpec((s, LANES), lambda p, i, qt, kt: (0, p)),
                   pl.BlockSpec((t, LANES), kblk), pl.BlockSpec((t, LANES), kblk), crs,
                   pl.BlockSpec((None, s, LANES), lambda p, i, qt, kt: (p, 0, 0))] + [ANY] * n_out,
        scratch_shapes=[pltpu.VMEM((t, LANES), F32), pltpu.VMEM((t, LANES), F32), pltpu.VMEM((2, t), F32)]
        + (comm.sems if comm else []))
    outs = pl.pallas_call(
        body, name="fox_bwd", grid_spec=grid_spec,
        out_shape=[jax.ShapeDtypeStruct((s, WIDTH), F32), jax.ShapeDtypeStruct((s, WIDTH), F32),
                   jax.ShapeDtypeStruct((s, WIDTH), BF16), jax.ShapeDtypeStruct((N_PAIR, 2, s), F32),
                   jax.ShapeDtypeStruct((N_PAIR, s, LANES), F32)] + (comm.out_shapes if comm else []),
        compiler_params=_params(("arbitrary", "arbitrary") if comm else ("parallel", "arbitrary")),
    )(qtab, ktab, qn, kn, proj, do, lse, delta, ccol, crow, *(comm.inputs if comm else []))
    return outs[:5], outs[5:]


def _sb_tile(qh, k, strict, tk, r_col):
    z = _dot(qh, k, _NT)
    lg = jnp.where(strict, -(jnp.maximum(z, 0.0) + jnp.log(1.0 + jnp.exp(-jnp.abs(z)))), 0.0)
    jj = lax.broadcasted_iota(jnp.int32, (tk, tk), 0)
    ss = lax.broadcasted_iota(jnp.int32, (tk, tk), 1)
    above = jnp.where(jj > ss, 1.0, 0.0).astype(BF16)
    suffix = sum(_dot(p, above, _NN) for p in _split2(lg)) + r_col
    a = jnp.where(strict, jnp.exp(lg + z + suffix), 0.0)
    return z, lg, a


def _sb_fwd(proj, tq=512, tk=256, comm=None):
    s = proj.shape[0]
    tq = _tile(s, tq)
    tk = _tile(tq, tk)
    nq, r = s // tq, tq // tk
    qtab, ktab, ntri = _tri_tables(nq, r, "k_desc")
    off = SB_OFF // LANES
    n_in, n_out = (len(comm.inputs), len(comm.out_shapes)) if comm else (0, 0)

    def body(qt_ref, kt_ref, q_ref, k_ref, v_ref, *rest):
        comm_in, o_ref, rest = rest[:n_in], rest[n_in], rest[n_in + 1:]
        comm_out, rest = rest[:n_out], rest[n_out:]
        (acc_ref, r_ref), sems = rest[:2], rest[2:]
        i = pl.program_id(1)
        qi, kj = qt_ref[i], kt_ref[i]
        if comm:
            pl.when((pl.program_id(0) == 0) & (i == 0))(lambda: comm.start(comm_in, comm_out, sems))

        @pl.when(kj == (qi + 1) * r - 1)
        def _():
            acc_ref[...] = jnp.zeros(acc_ref.shape, F32)
            r_ref[...] = jnp.zeros(r_ref.shape, F32)

        q = q_ref[...] * QK_SCALE
        k, v = k_ref[...], v_ref[...]
        row, col = _positions(qi, kj, tq, tk)
        strict = col < row
        acc = acc_ref[...]
        for h in range(2):
            _, lg, a = _sb_tile(_head(q, h), k, strict, tk, r_ref[h])
            acc = acc + _dot(a.astype(BF16), _head(v, h), _NN)
            r_ref[h] += jnp.sum(lg, axis=-1, keepdims=True)
        acc_ref[...] = acc

        @pl.when(kj == 0)
        def _():
            o_ref[...] = acc_ref[...]

        if comm:
            pl.when((pl.program_id(0) == N_PAIR - 1) & (i == ntri - 1))(lambda: comm.finish(comm_in, comm_out, sems))

    grid_spec = pltpu.PrefetchScalarGridSpec(
        num_scalar_prefetch=2, grid=(N_PAIR, ntri),
        in_specs=[pl.BlockSpec((tq, LANES), lambda p, i, qt, kt: (qt[i], off + p)),
                  pl.BlockSpec((tk, LANES), lambda p, i, qt, kt: (kt[i], off + N_PAIR + p)),
                  pl.BlockSpec((tk, LANES), lambda p, i, qt, kt: (kt[i], off + 2 * N_PAIR + p))] + [ANY] * n_in,
        out_specs=[pl.BlockSpec((tq, LANES), lambda p, i, qt, kt: (qt[i], p))] + [ANY] * n_out,
        scratch_shapes=[pltpu.VMEM((tq, LANES), F32), pltpu.VMEM((2, tq, 1), F32)] + (comm.sems if comm else []))
    outs = pl.pallas_call(
        body, name="sb_fwd", grid_spec=grid_spec,
        out_shape=[jax.ShapeDtypeStruct((s, WIDTH), F32)] + (comm.out_shapes if comm else []),
        compiler_params=_params(("arbitrary", "arbitrary") if comm else ("parallel", "arbitrary")),
    )(qtab, ktab, proj, proj, proj, *(comm.inputs if comm else []))
    return outs[0], outs[1:]


def _sb_bwd(proj, do, delta, tq=512, tk=256, comm=None):
    s = proj.shape[0]
    tq = _tile(s, tq)
    tk = _tile(tq, tk)
    nq, r = s // tq, tq // tk
    qtab, ktab, ntri = _tri_tables(nq, r, "k_desc")
    off = SB_OFF // LANES
    n_in, n_out = (len(comm.inputs), len(comm.out_shapes)) if comm else (0, 0)

    def body(qt_ref, kt_ref, q_ref, k_ref, v_ref, do_ref, dl_ref, *rest):
        comm_in, rest = rest[:n_in], rest[n_in:]
        (dq_ref, dk_ref, dv_ref), rest = rest[:3], rest[3:]
        comm_out, rest = rest[:n_out], rest[n_out:]
        (dqa_ref, dka_ref, dva_ref, r_ref, rd_ref), sems = rest[:5], rest[5:]
        i = pl.program_id(1)
        qi, kj = qt_ref[i], kt_ref[i]
        if comm:
            pl.when((pl.program_id(0) == 0) & (i == 0))(lambda: comm.start(comm_in, comm_out, sems))

        @pl.when(i == 0)
        def _():
            dka_ref[...] = jnp.zeros(dka_ref.shape, F32)
            dva_ref[...] = jnp.zeros(dva_ref.shape, F32)

        @pl.when(kj == (qi + 1) * r - 1)
        def _():
            dqa_ref[...] = jnp.zeros(dqa_ref.shape, F32)
            r_ref[...] = jnp.zeros(r_ref.shape, F32)
            rd_ref[...] = jnp.zeros(rd_ref.shape, F32)

        q = q_ref[...] * QK_SCALE
        k, v, g = k_ref[...], v_ref[...], do_ref[...]
        row, col = _positions(qi, kj, tq, tk)
        strict = col < row
        ss = lax.broadcasted_iota(jnp.int32, (tk, tk), 0)
        jj = lax.broadcasted_iota(jnp.int32, (tk, tk), 1)
        at_or_after = jnp.where(ss >= jj, 1.0, 0.0).astype(BF16)
        dq = jnp.zeros((tq, LANES), F32)
        dk = jnp.zeros((tk, LANES), F32)
        dv = jnp.zeros((tk, LANES), F32)
        for h in range(2):
            qh, gh = _head(q, h), _head(g, h)
            z, lg, a = _sb_tile(qh, k, strict, tk, r_ref[h])
            da = _dot(gh, v, _NT)
            ab = a.astype(BF16)
            d_a = da * ab.astype(F32)
            incl = sum(_dot(p, at_or_after, _NN) for p in _split2(d_a))
            d_l = _pair_cols(dl_ref[...], h) - rd_ref[h] - incl
            sig = jnp.exp(lg + z)
            dz = jnp.where(strict, d_a * (1.0 - sig) - d_l * sig, 0.0).astype(BF16)
            dq = dq + _dot(dz, _head(k, h), _NN)
            dk = dk + _dot(dz, qh, _TN)
            dv = dv + _dot(ab, gh, _TN)
            r_ref[h] += jnp.sum(lg, axis=-1, keepdims=True)
            rd_ref[h] += jnp.sum(d_a, axis=-1, keepdims=True)
        dqa_ref[...] += dq
        rows = pl.ds(pl.multiple_of(kj * tk, tk), tk)
        dka_ref[rows, :] += dk
        dva_ref[rows, :] += dv

        @pl.when(kj == 0)
        def _():
            dq_ref[...] = (dqa_ref[...] * QK_SCALE).astype(BF16)

        @pl.when(i == ntri - 1)
        def _():
            dk_ref[...] = dka_ref[...].astype(BF16)
            dv_ref[...] = dva_ref[...].astype(BF16)

        if comm:
            pl.when((pl.program_id(0) == N_PAIR - 1) & (i == ntri - 1))(lambda: comm.finish(comm_in, comm_out, sems))

    qblk = lambda p, i, qt, kt: (qt[i], p)
    whole = pl.BlockSpec((s, LANES), lambda p, i, qt, kt: (0, p))
    grid_spec = pltpu.PrefetchScalarGridSpec(
        num_scalar_prefetch=2, grid=(N_PAIR, ntri),
        in_specs=[pl.BlockSpec((tq, LANES), lambda p, i, qt, kt: (qt[i], off + p)),
                  pl.BlockSpec((tk, LANES), lambda p, i, qt, kt: (kt[i], off + N_PAIR + p)),
                  pl.BlockSpec((tk, LANES), lambda p, i, qt, kt: (kt[i], off + 2 * N_PAIR + p)),
                  pl.BlockSpec((tq, LANES), qblk),
                  pl.BlockSpec((None, tq, LANES), lambda p, i, qt, kt: (p, qt[i], 0))] + [ANY] * n_in,
        out_specs=[pl.BlockSpec((tq, LANES), qblk), whole, whole] + [ANY] * n_out,
        scratch_shapes=[pltpu.VMEM((tq, LANES), F32), pltpu.VMEM((s, LANES), F32), pltpu.VMEM((s, LANES), F32),
                        pltpu.VMEM((2, tq, 1), F32), pltpu.VMEM((2, tq, 1), F32)] + (comm.sems if comm else []))
    act = jax.ShapeDtypeStruct((s, WIDTH), BF16)
    outs = pl.pallas_call(
        body, name="sb_bwd", grid_spec=grid_spec, out_shape=[act, act, act] + (comm.out_shapes if comm else []),
        compiler_params=_params(("arbitrary", "arbitrary") if comm else ("parallel", "arbitrary")),
    )(qtab, ktab, proj, proj, proj, do, delta, *(comm.inputs if comm else []))
    return outs[:3], outs[3:]


def _loss_head(y, target, tm=256):
    s, d = y.shape
    tm = min(tm, s)

    def body(y_ref, t_ref, l_ref, dy_ref):
        e = y_ref[...] - t_ref[...]
        dy_ref[...] = e / d
        part = jnp.sum(jnp.sum(e * e, axis=0, keepdims=True), axis=1, keepdims=True)

        @pl.when(pl.program_id(0) == 0)
        def _():
            l_ref[...] = jnp.broadcast_to(part, l_ref.shape)

        @pl.when(pl.program_id(0) > 0)
        def _():
            l_ref[...] += jnp.broadcast_to(part, l_ref.shape)

    row = pl.BlockSpec((tm, d), lambda i: (i, 0))
    return pl.pallas_call(
        body, name="loss_head", grid=(s // tm,), in_specs=[row, row],
        out_specs=[pl.BlockSpec((1, LANES), lambda i: (0, 0)), row],
        out_shape=[jax.ShapeDtypeStruct((1, LANES), F32), jax.ShapeDtypeStruct((s, d), F32)],
        compiler_params=_params(("arbitrary",)),
    )(y, target)


def _adamw(w, g, m, v, name):
    shape = w.shape
    cols = shape[-1]
    rows = int(np.prod(shape[:-1]))
    tm = rows
    for cand in (512, 256, 128, 64, 32, 16, 8):
        if rows % cand == 0 and rows > cand and cand * cols * 4 <= ADAMW_BLOCK_BYTES:
            tm = cand
            break

    def body(w_ref, g_ref, m_ref, v_ref, d_ref, mo_ref, vo_ref):
        gr = g_ref[...]
        mn = ADAM_B1 * m_ref[...] + (1.0 - ADAM_B1) * gr
        vn = ADAM_B2 * v_ref[...] + (1.0 - ADAM_B2) * (gr * gr)
        m_hat = mn / (1.0 - ADAM_B1 ** ADAM_STEP)
        v_hat = vn / (1.0 - ADAM_B2 ** ADAM_STEP)
        d_ref[...] = -ADAM_LR * (m_hat / (jnp.sqrt(v_hat) + ADAM_EPS) + ADAM_WD * w_ref[...])
        mo_ref[...] = mn
        vo_ref[...] = vn

    blk = pl.BlockSpec((tm, cols), lambda i: (i, 0))
    out = jax.ShapeDtypeStruct((rows, cols), F32)
    r2 = lambda a: a.reshape(rows, cols)
    outs = pl.pallas_call(
        body, name=name, grid=(rows // tm,), in_specs=[blk] * 4, out_specs=[blk] * 3, out_shape=[out] * 3,
        compiler_params=_params(("parallel",)),
    )(r2(w), r2(g), r2(m), r2(v))
    return tuple(o.reshape(shape) for o in outs)


def _split_w_in(w_in):
    c0, c1, c2, c3 = 3 * WIDTH, 6 * WIDTH, 6 * WIDTH + N_HEADS, 9 * WIDTH + N_HEADS
    wa = jnp.concatenate([w_in[:, c3:], w_in[:, :c0], w_in[:, c0:c1], w_in[:, c2:c3]], axis=1)
    wf = jnp.pad(w_in[:, c1:c2], ((0, 0), (0, LANES - N_HEADS)))
    return wa, wf


def _join_w_in(ga, gf):
    g, c, f, s = 3 * D_MODEL, CONV_OFF, FOX_OFF, SB_OFF
    return jnp.concatenate([ga[:, c:f], ga[:, f:s], gf[:, :N_HEADS], ga[:, s:], ga[:, :g]], axis=1)


def _layer_fwd(x, p, plan, l):
    s = x.shape[0]
    proj, hn1 = _norm_mm(x, p["norm1_g"], p["wa"], BF16, name="in_proj")
    fraw, _ = _norm_mm(x, p["norm1_g"], p["wf"], F32, name="in_proj_f")
    ft = fraw[:, :N_HEADS].T
    crow8 = _fox_gates(ft, p["fox_f_bias"])
    crow = crow8.reshape(N_PAIR, 2, s)
    ccol = jnp.repeat(crow.transpose(0, 2, 1), HEAD_DIM, axis=2)
    oc = _conv_mix_fwd(proj, p["conv_w"])
    qn, kn = _fox_prep(proj, p["fox_q_norm_g"], p["fox_k_norm_g"])
    last = l + 1 == plan.depth
    (of, lse), got_in = _fox_fwd(qn, kn, proj, ccol, crow, comm=None if last else plan.gather(l + 1, BIG_IN))
    osb, got_rest = _sb_fwd(proj, comm=plan.gather(l, BIG_REST))
    p.update(plan.weights(l, BIG_REST, got_rest))
    nxt = None if last else plan.layer(l + 1, got_in)
    xm, merged, yc, yf, ys = _merge_fwd(x, oc, of, osb, p["w_proj_conv"], p["w_proj_fox"], p["w_proj_sb"], proj,
                                        p["gate_bias"], p["w_out"])
    up, hn2 = _norm_mm(xm, p["norm2_g"], p["w_up"], BF16, name="up_proj")
    hh = _glu_fwd(up, p["ffn_conv_w"], p["ffn_conv_b"])
    xo = _mm(hh, p["w_down"], "nn", F32, res=xm, name="down_proj")
    saved = dict(x=x, hn1=hn1, proj=proj, ft=ft, crow=crow, ccol=ccol, oc=oc, qn=qn, kn=kn, of=of, lse=lse, osb=osb,
                 merged=merged, yc=yc, yf=yf, ys=ys, xm=xm, hn2=hn2, up=up, hh=hh)
    return xo, saved, nxt


def _layer_bwd(dx, p, a, pending=None):
    s = dx.shape[0]
    g = {}
    dhh = _mm(dx, p["w_down"], "nt", BF16, tn=1408, name="d_down_in")
    g["w_down"] = _mm(a["hh"], dx, "tn", F32, name="g_w_down")
    dug, duv, g["ffn_conv_w"], g["ffn_conv_b"] = _glu_bwd(dhh, a["up"], p["ffn_conv_w"], p["ffn_conv_b"])
    dup = jnp.concatenate([dug, duv], axis=1)
    g["w_up"] = _mm(a["hn2"], dup, "tn", F32, tn=1408, name="g_w_up")
    dhn2 = _mm(dup, p["w_up"], "nt", F32, name="d_up_in")
    dx, g["norm2_g"] = _norm_bwd(dhn2, a["xm"], p["norm2_g"], dx, name="norm2_bwd")
    dm = _mm(dx, p["w_out"], "nt", BF16, name="d_out_in")
    g["w_out"] = _mm(a["merged"], dx, "tn", F32, tn=512, name="g_w_out")
    dyc, dyf, dys, dgl, g["gate_bias"] = _gate_bwd(dm, a["yc"], a["yf"], a["ys"], a["proj"], p["gate_bias"])
    doc = _mm(dyc, p["w_proj_conv"], "nt", BF16, name="d_pc_in")
    dof = _mm(dyf, p["w_proj_fox"], "nt", BF16, name="d_pf_in")
    dos = _mm(dys, p["w_proj_sb"], "nt", BF16, name="d_ps_in")
    g["w_proj_conv"] = _mm(a["oc"], dyc, "tn", F32, name="g_w_pc")
    g["w_proj_fox"] = _mm(a["of"], dyf, "tn", F32, name="g_w_pf")
    g["w_proj_sb"] = _mm(a["osb"], dys, "tn", F32, name="g_w_ps")
    dcb, dcc, dch, g["conv_w"] = _conv_mix_bwd(doc, a["proj"], p["conv_w"])
    delta_f = _delta_rep(dof, a["of"])
    (dqs, dkn, dfv, dcrow, dcq), recv_a = _fox_bwd(a["qn"], a["kn"], a["proj"], dof, a["lse"], delta_f, a["ccol"], a["crow"],
                                                   comm=pending.exchange() if pending else None)
    dc = dcrow.reshape(N_HEADS, s) + dcq[:, :, ::HEAD_DIM].transpose(0, 2, 1).reshape(N_HEADS, s)
    dfq, dfk, g["fox_q_norm_g"], g["fox_k_norm_g"] = _fox_post(dqs, dkn, a["proj"], p["fox_q_norm_g"], p["fox_k_norm_g"])
    dft, g["fox_f_bias"] = _fox_gates_bwd(dc, a["ft"], p["fox_f_bias"])
    delta_s = _delta_rep(dos, a["osb"])
    (dsq, dsk, dsv), recv_b = _sb_bwd(a["proj"], dos, delta_s, comm=pending.sums(recv_a) if pending else None)
    done = pending.finish(recv_b) if pending else None
    dproj = jnp.concatenate([dgl, dcb, dcc, dch, dfq, dfk, dfv, dsq, dsk, dsv], axis=1)
    dfp = jnp.pad(dft.T, ((0, 0), (0, LANES - N_HEADS))).astype(BF16)
    ga = _mm(a["hn1"], dproj, "tn", F32, tn=768, name="g_w_in")
    gf = _mm(a["hn1"], dfp, "tn", F32, name="g_w_in_f")
    g["w_in"] = _join_w_in(ga, gf)
    dhn1 = _mm(dfp, p["wf"], "nt", F32, name="d_in_f")
    dhn1 = _mm(dproj, p["wa"], "nt", F32, res=dhn1, tk=1536, name="d_in")
    dx, g["norm1_g"] = _norm_bwd(dhn1, a["x"], p["norm1_g"], dx, name="norm1_bwd")
    return dx, g, done


MATMUL_WEIGHTS = ("w_in", "w_proj_conv", "w_proj_fox", "w_proj_sb", "w_out", "w_up", "w_down")
WEIGHTS = ("norm1_g", "w_in", "fox_f_bias", "gate_bias", "conv_w", "fox_q_norm_g", "fox_k_norm_g", "w_proj_conv",
           "w_proj_fox", "w_proj_sb", "w_out", "norm2_g", "w_up", "ffn_conv_w", "ffn_conv_b", "w_down")


def _local_step(x, target, plan):
    depth = plan.depth
    layers, saved = [plan.layer(0, None)], []
    for l in range(depth):
        x, a, nxt = _layer_fwd(x, layers[l], plan, l)
        saved.append(a)
        if nxt is not None:
            layers.append(nxt)
    sq, dx = _loss_head(x, target)
    grads, pending = [None] * depth, None
    for l in reversed(range(depth)):
        dx, grads[l], done = _layer_bwd(dx, layers[l], saved[l], pending)
        if pending is not None:
            grads[l + 1].update(zip(BIG, done))
        pending = plan.reduction(grads[l])
    if pending is not None:
        grads[0].update(zip(BIG, pending.run()))
    return sq, dx, grads


ANY = pl.BlockSpec(memory_space=pl.ANY)


def _place():
    x, y, c = lax.axis_index("x"), lax.axis_index("y"), lax.axis_index("c")
    chips = [(1 - x, y), (x, 1 - y), (1 - x, 1 - y)]
    return x, y, c, chips


_Comm = collections.namedtuple("_Comm", "inputs out_shapes sems start finish")


def _run_comm(comm, name):
    n_in, n_out = len(comm.inputs), len(comm.out_shapes)

    def body(*refs):
        ins, outs, sems = refs[:n_in], refs[n_in:n_in + n_out], refs[n_in + n_out:]
        comm.start(ins, outs, sems)
        comm.finish(ins, outs, sems)

    return pl.pallas_call(body, name=name, in_specs=[ANY] * n_in, out_specs=[ANY] * n_out, out_shape=comm.out_shapes,
                          scratch_shapes=comm.sems)(*comm.inputs)


def _gather_comm(shards):
    n = len(shards)

    def copy(x_refs, out_refs, sems, k, t, chip_index, which_half, to, from_input=False):
        half = shards[t].shape[0] // 2
        rows = pl.ds(which_half * half, half)
        dst = out_refs[t].at[chip_index, rows, :]
        return pltpu.make_async_remote_copy(
            src_ref=x_refs[t].at[rows, :] if from_input else dst, dst_ref=dst,
            send_sem=sems[0].at[k, t], recv_sem=sems[1].at[k, t], device_id=to, device_id_type=MESH)

    def first(x_refs, out_refs, sems):
        x, y, c, chips = _place()
        return [copy(x_refs, out_refs, sems, p, t, 2 * x + y, c, (*chip, c), from_input=True)
                for t in range(n) for p, chip in enumerate(chips)]

    def start(x_refs, out_refs, sems):
        for cp in first(x_refs, out_refs, sems):
            cp.start()

    def finish(x_refs, out_refs, sems):
        x, y, c, chips = _place()
        passed = []
        for t in range(n):
            for p, chip in enumerate(chips):
                copy(x_refs, out_refs, sems, p, t, 2 * chip[0] + chip[1], c, (x, y, c)).wait_recv()
                fwd = copy(x_refs, out_refs, sems, 3 + p, t, 2 * chip[0] + chip[1], c, (x, y, 1 - c))
                fwd.start()
                passed.append(fwd)
        for t in range(n):
            for p, chip in enumerate(chips):
                copy(x_refs, out_refs, sems, 3 + p, t, 2 * chip[0] + chip[1], 1 - c, (x, y, c)).wait_recv()
        for cp in first(x_refs, out_refs, sems) + passed:
            cp.wait_send()

    return _Comm(list(shards), [jax.ShapeDtypeStruct((4, *s.shape), s.dtype) for s in shards],
                 [pltpu.SemaphoreType.DMA((6, n)), pltpu.SemaphoreType.DMA((6, n))], start, finish)


def _own_block(got, shards):
    chip = 2 * lax.axis_index("x") + lax.axis_index("y")
    return [lax.dynamic_update_index_in_dim(g, s, chip, 0) for g, s in zip(got, shards)]


def _pair_exchange_comm(gs):
    n = len(gs)

    def copies(g_refs, recv_refs, sems):
        x, y, c, _ = _place()
        return [pltpu.make_async_remote_copy(
            src_ref=g_refs[t].at[:, pl.ds((1 - c) * (gs[t].shape[1] // 2), gs[t].shape[1] // 2), :], dst_ref=recv_refs[t],
            send_sem=sems[0].at[t], recv_sem=sems[1].at[t], device_id=(x, y, 1 - c), device_id_type=MESH) for t in range(n)]

    def start(g_refs, recv_refs, sems):
        for cp in copies(g_refs, recv_refs, sems):
            cp.start()

    def finish(g_refs, recv_refs, sems):
        for cp in copies(g_refs, recv_refs, sems):
            cp.wait()

    return _Comm(list(gs), [jax.ShapeDtypeStruct((4, g.shape[1] // 2, g.shape[2]), g.dtype) for g in gs],
                 [pltpu.SemaphoreType.DMA((n,)), pltpu.SemaphoreType.DMA((n,))], start, finish)


def _pair_sum(g, recv, core, tr=256):
    n, r, cols = g.shape
    half = r // 2
    tr = _row_tile(half, tr)
    nb = half // tr

    def body(c_ref, g_ref, r_ref, o_ref):
        o_ref[...] = (g_ref[...] + r_ref[...]).astype(BF16)

    grid_spec = pltpu.PrefetchScalarGridSpec(
        num_scalar_prefetch=1, grid=(n, nb),
        in_specs=[pl.BlockSpec((None, tr, cols), lambda k, i, c: (k, c[0] * nb + i, 0)),
                  pl.BlockSpec((None, tr, cols), lambda k, i, c: (k, i, 0))],
        out_specs=pl.BlockSpec((None, tr, cols), lambda k, i, c: (k, i, 0)))
    return pl.pallas_call(
        body, name="rs_pair_sum", grid_spec=grid_spec, out_shape=jax.ShapeDtypeStruct((n, half, cols), BF16),
        compiler_params=_params(("parallel", "parallel")),
    )(core, g, recv)


def _row_tile(rows, pref):
    best = None
    for t in range(16, min(rows, pref) + 1, 16):
        if rows % t == 0:
            best = t
    assert best is not None, (rows, pref)
    return best


def _chip_exchange_comm(s1s):
    n = len(s1s)

    def copies(s_refs, recv_refs, sems):
        x, y, c, chips = _place()
        return [pltpu.make_async_remote_copy(
            src_ref=s_refs[t].at[2 * chip[0] + chip[1]], dst_ref=recv_refs[t].at[p],
            send_sem=sems[0].at[p, t], recv_sem=sems[1].at[p, t], device_id=(*chip, c), device_id_type=MESH)
            for t in range(n) for p, chip in enumerate(chips)]

    def start(s_refs, recv_refs, sems):
        for cp in copies(s_refs, recv_refs, sems):
            cp.start()

    def finish(s_refs, recv_refs, sems):
        for cp in copies(s_refs, recv_refs, sems):
            cp.wait()

    return _Comm(list(s1s), [jax.ShapeDtypeStruct((3, *s.shape[1:]), s.dtype) for s in s1s],
                 [pltpu.SemaphoreType.DMA((3, n)), pltpu.SemaphoreType.DMA((3, n))], start, finish)


def _final_sum(g, recv_a, recv_b, core, chip, tr=256):
    n, r, cols = g.shape
    half = r // 2
    tr = _row_tile(half, tr)
    nb = half // tr

    def body(c_ref, k_ref, g_ref, a_ref, b0_ref, b1_ref, b2_ref, o_ref):
        total = g_ref[...] + a_ref[...]
        for b_ref in (b0_ref, b1_ref, b2_ref):
            total = total + b_ref[...].astype(F32)
        o_ref[...] = total

    rel = lambda p: pl.BlockSpec((None, tr, cols), lambda i, c, k, p=p: (p, i, 0))
    grid_spec = pltpu.PrefetchScalarGridSpec(
        num_scalar_prefetch=2, grid=(nb,),
        in_specs=[pl.BlockSpec((None, tr, cols), lambda i, c, k: (k[0], c[0] * nb + i, 0)),
                  pl.BlockSpec((None, tr, cols), lambda i, c, k: (k[0], i, 0)), rel(0), rel(1), rel(2)],
        out_specs=pl.BlockSpec((tr, cols), lambda i, c, k: (c[0] * nb + i, 0)))
    return pl.pallas_call(
        body, name="rs_final_sum", grid_spec=grid_spec, out_shape=jax.ShapeDtypeStruct((r, cols), F32),
        compiler_params=_params(("parallel",)),
    )(core, chip, g, recv_a, recv_b, recv_b, recv_b)


def _pair_join(fs):
    n = len(fs)

    def body(*refs):
        out_refs = refs[n:2 * n]
        send_sems, recv_sems = refs[2 * n:]
        x, y, c, _ = _place()

        def copy(t, which_half):
            h = fs[t].shape[0] // 2
            rows = out_refs[t].at[pl.ds(which_half * h, h), :]
            return pltpu.make_async_remote_copy(
                src_ref=rows, dst_ref=rows, send_sem=send_sems.at[t], recv_sem=recv_sems.at[t],
                device_id=(x, y, 1 - c), device_id_type=MESH)

        sends = [copy(t, c) for t in range(n)]
        for cp in sends:
            cp.start()
        for t in range(n):
            sends[t].wait_send()
            copy(t, 1 - c).wait_recv()

    return pl.pallas_call(
        body, name="rs_pair_join", in_specs=[ANY] * n, out_specs=[ANY] * n,
        out_shape=[jax.ShapeDtypeStruct(f.shape, f.dtype) for f in fs],
        input_output_aliases={t: t for t in range(n)},
        scratch_shapes=[pltpu.SemaphoreType.DMA((n,)), pltpu.SemaphoreType.DMA((n,))],
    )(*fs)


class _Reduction:
    def __init__(self, gs):
        self.gs = gs
        self.core = lax.axis_index("c").astype(jnp.int32).reshape(1)
        self.chip = (2 * lax.axis_index("x") + lax.axis_index("y")).astype(jnp.int32).reshape(1)

    def exchange(self):
        return _pair_exchange_comm(self.gs)

    def sums(self, recv_a):
        self.recv_a = list(recv_a)
        return _chip_exchange_comm([_pair_sum(g, ra, self.core) for g, ra in zip(self.gs, self.recv_a)])

    def finish(self, recv_b):
        return _pair_join([_final_sum(g, ra, rb, self.core, self.chip) for g, ra, rb in zip(self.gs, self.recv_a, recv_b)])

    def run(self):
        recv_a = _run_comm(self.exchange(), "rs_pair_exchange")
        return self.finish(_run_comm(self.sums(recv_a), "rs_chip_exchange"))


def _all_reduce_small(v):
    r, cols = v.shape

    def body(v_ref, out_ref, buf_ref, send_sems, recv_sems):
        x, y, c, _ = _place()
        flip = lambda a, bit: 1 - a if bit else a
        buf_ref[4 * x + 2 * y + c] = v_ref[...]
        cps = []
        for rel in range(1, 8):
            peer = (flip(x, rel & 4), flip(y, rel & 2), flip(c, rel & 1))
            cps.append(pltpu.make_async_remote_copy(
                src_ref=v_ref, dst_ref=buf_ref.at[4 * x + 2 * y + c], send_sem=send_sems.at[rel - 1], recv_sem=recv_sems.at[rel - 1],
                device_id=peer, device_id_type=MESH))
        for cp in cps:
            cp.start()
        for cp in cps:
            cp.wait()
        total = buf_ref[0]
        for d in range(1, 8):
            total = total + buf_ref[d]
        out_ref[...] = total

    vm = pl.BlockSpec(memory_space=pltpu.VMEM)
    return pl.pallas_call(
        body, name="all_reduce_small", in_specs=[vm], out_specs=vm, out_shape=jax.ShapeDtypeStruct((r, cols), F32),
        scratch_shapes=[pltpu.VMEM((8, r, cols), F32), pltpu.SemaphoreType.DMA((7,)), pltpu.SemaphoreType.DMA((7,))],
    )(v)


SHARD_AXIS = {"w_in": 1, "conv_w": 1, "w_proj_conv": 1, "w_proj_fox": 1, "w_proj_sb": 1, "w_out": 0, "w_up": 1,
              "ffn_conv_w": 1, "w_down": 0}
SMALL_SHARDED = ("conv_w", "ffn_conv_w")
BIG = tuple(k for k in SHARD_AXIS if k not in SMALL_SHARDED)
BIG_IN = ("w_in",)
BIG_REST = tuple(k for k in BIG if k not in BIG_IN)
REPLICATED = tuple(k for k in WEIGHTS if k not in SHARD_AXIS)
SMALL = REPLICATED + SMALL_SHARDED


def _pack_small(parts, row_align):
    flat = jnp.concatenate([p.reshape(-1) for p in parts])
    rows = -(-flat.shape[0] // (PACK_COLS * row_align)) * row_align
    return jnp.pad(flat, (0, rows * PACK_COLS - flat.shape[0])).reshape(rows, PACK_COLS)


def _unpack_small(packed, shapes):
    flat = packed.reshape(-1)
    out, off = [], 0
    for shape in shapes:
        size = int(np.prod(shape))
        out.append(flat[off:off + size].reshape(shape))
        off += size
    return out


def _to_full(stacked, axis):
    _, r, c = stacked.shape
    if axis == 0:
        return stacked.reshape(4 * r, c)
    return jnp.moveaxis(stacked, 0, 1).reshape(r, 4 * c)


def _to_chips(full, axis):
    a, b = full.shape
    if axis == 0:
        return full.reshape(4, a // 4, b)
    return jnp.moveaxis(full.reshape(a, 4, b // 4), 1, 0)


class _Plan:
    def __init__(self, given):
        self.given = given
        self.depth = given["norm1_g"].shape[0]
        conv_shapes = [given[k].shape for k in SMALL_SHARDED]
        packed = [_pack_small([given[k] for k in SMALL_SHARDED], 16)]
        (got,) = _own_block(_run_comm(_gather_comm(packed), "gather_conv_weights"), packed)
        by_chip = [_unpack_small(got[j], conv_shapes) for j in range(4)]
        self.conv = {k: jnp.concatenate([by_chip[j][i] for j in range(4)], axis=-1) for i, k in enumerate(SMALL_SHARDED)}
        self.shards = {}

    def _shards(self, l, names):
        if (l, names) not in self.shards:
            self.shards[l, names] = [self.given[k][l].astype(BF16) for k in names]
        return self.shards[l, names]

    def gather(self, l, names):
        return _gather_comm(self._shards(l, names))

    def weights(self, l, names, got):
        return {k: _to_full(stacked, SHARD_AXIS[k]) for k, stacked in zip(names, _own_block(got, self._shards(l, names)))}

    def layer(self, l, got_in):
        if got_in is None:
            got_in = _run_comm(self.gather(l, BIG_IN), "gather_w_in")
        p = {k: self.given[k][l] for k in REPLICATED}
        p.update({k: self.conv[k][l] for k in SMALL_SHARDED})
        p["wa"], p["wf"] = _split_w_in(self.weights(l, BIG_IN, got_in)["w_in"])
        return p

    def reduction(self, g):
        return _Reduction([_to_chips(g[k], SHARD_AXIS[k]) for k in BIG])


def kernel(x, norm1_g, w_in, fox_f_bias, gate_bias, conv_w, fox_q_norm_g, fox_k_norm_g, w_proj_conv, w_proj_fox, w_proj_sb, w_out, norm2_g, w_up, ffn_conv_w, ffn_conv_b, w_down, loss_target, m_norm1_g, m_w_in, m_fox_f_bias, m_gate_bias, m_conv_w, m_fox_q_norm_g, m_fox_k_norm_g, m_w_proj_conv, m_w_proj_fox, m_w_proj_sb, m_w_out, m_norm2_g, m_w_up, m_ffn_conv_w, m_ffn_conv_b, m_w_down, v_norm1_g, v_w_in, v_fox_f_bias, v_gate_bias, v_conv_w, v_fox_q_norm_g, v_fox_k_norm_g, v_w_proj_conv, v_w_proj_fox, v_w_proj_sb, v_w_out, v_norm2_g, v_w_up, v_ffn_conv_w, v_ffn_conv_b, v_w_down):
    given = dict(x=x, norm1_g=norm1_g, w_in=w_in, fox_f_bias=fox_f_bias, gate_bias=gate_bias, conv_w=conv_w, fox_q_norm_g=fox_q_norm_g, fox_k_norm_g=fox_k_norm_g, w_proj_conv=w_proj_conv, w_proj_fox=w_proj_fox, w_proj_sb=w_proj_sb, w_out=w_out, norm2_g=norm2_g, w_up=w_up, ffn_conv_w=ffn_conv_w, ffn_conv_b=ffn_conv_b, w_down=w_down, loss_target=loss_target, m_norm1_g=m_norm1_g, m_w_in=m_w_in, m_fox_f_bias=m_fox_f_bias, m_gate_bias=m_gate_bias, m_conv_w=m_conv_w, m_fox_q_norm_g=m_fox_q_norm_g, m_fox_k_norm_g=m_fox_k_norm_g, m_w_proj_conv=m_w_proj_conv, m_w_proj_fox=m_w_proj_fox, m_w_proj_sb=m_w_proj_sb, m_w_out=m_w_out, m_norm2_g=m_norm2_g, m_w_up=m_w_up, m_ffn_conv_w=m_ffn_conv_w, m_ffn_conv_b=m_ffn_conv_b, m_w_down=m_w_down, v_norm1_g=v_norm1_g, v_w_in=v_w_in, v_fox_f_bias=v_fox_f_bias, v_gate_bias=v_gate_bias, v_conv_w=v_conv_w, v_fox_q_norm_g=v_fox_q_norm_g, v_fox_k_norm_g=v_fox_k_norm_g, v_w_proj_conv=v_w_proj_conv, v_w_proj_fox=v_w_proj_fox, v_w_proj_sb=v_w_proj_sb, v_w_out=v_w_out, v_norm2_g=v_norm2_g, v_w_up=v_w_up, v_ffn_conv_w=v_ffn_conv_w, v_ffn_conv_b=v_ffn_conv_b, v_w_down=v_w_down)
    depth = given["norm1_g"].shape[0]
    chip = 2 * lax.axis_index("x") + lax.axis_index("y")

    sq, dx, grads = _local_step(given["x"][0], given["loss_target"][0], _Plan(given))
    loss = lax.psum(0.5 * sq[0, 0] / D_MODEL, ("x", "y", "c"))

    gsum = {k: jnp.stack([g[k] for g in grads]) for k in BIG}
    small_shapes = [(depth, *grads[0][k].shape) for k in SMALL]
    summed = _all_reduce_small(_pack_small([jnp.stack([g[k] for g in grads]) for k in SMALL], 8))
    for k, total in zip(SMALL, _unpack_small(summed, small_shapes)):
        if k in SMALL_SHARDED:
            total = lax.dynamic_index_in_dim(total.reshape(*total.shape[:-1], 4, total.shape[-1] // 4), chip, axis=2, keepdims=False)
        gsum[k] = total

    deltas, new_m, new_v = {}, {}, {}
    for k in WEIGHTS:
        deltas[k], new_m[k], new_v[k] = _adamw(given[k], gsum[k], given["m_" + k], given["v_" + k], "adamw_" + k)
    return (loss, dx[None], *[gsum[k] for k in WEIGHTS], *[deltas[k] for k in WEIGHTS],
            *[new_m[k] for k in WEIGHTS], *[new_v[k] for k in WEIGHTS])
```

```python
import collections

import numpy as np
import jax
import jax.numpy as jnp
from jax import lax
from jax.experimental import pallas as pl
from jax.experimental.pallas import tpu as pltpu

F32 = jnp.float32
BF16 = jnp.bfloat16

D_MODEL = 1024
DEPTH = 4
HEAD_DIM = 64
N_HEADS = 8
WIDTH = 512
D_FF = 2816
NORM_EPS = 1e-6
NEG_INF = -1e30
QK_SCALE = HEAD_DIM ** -0.5
LANES = 128
N_PAIR = N_HEADS // 2

GATE_OFF = 0
CONV_OFF = 3 * D_MODEL
FOX_OFF = CONV_OFF + 3 * WIDTH
SB_OFF = FOX_OFF + 3 * WIDTH
D_INA = SB_OFF + 3 * WIDTH

ADAM_LR = 0.001
ADAM_B1 = 0.9
ADAM_B2 = 0.999
ADAM_EPS = 1e-08
ADAM_WD = 0.01
ADAM_STEP = 10

VMEM_LIMIT = 48 * 1024 * 1024
ADAMW_BLOCK_BYTES = 1024 * 1024

PACK_COLS = 1024
PACK_ROW_ALIGN = 32
MESH = pl.DeviceIdType.MESH


def _params(sem):
    return pltpu.CompilerParams(dimension_semantics=sem, vmem_limit_bytes=VMEM_LIMIT)


def _dot(a, b, dims):
    return lax.dot_general(a, b, (dims, ((), ())), preferred_element_type=F32)


_NN = ((1,), (0,))
_NT = ((1,), (1,))
_TN = ((0,), (0,))


def _pick(dim, pref):
    if dim <= pref:
        return dim
    best = None
    for mult in range(1, dim // LANES + 1):
        t = mult * LANES
        if t <= pref and dim % t == 0:
            best = t
    assert best is not None, (dim, pref)
    return best


def _mm(a, b, mode, out_dtype=F32, tm=1024, tn=1024, tk=2048, res=None, name="mm"):
    if mode == "nn":
        (m, k), (_, n) = a.shape, b.shape
    elif mode == "nt":
        (m, k), (n, _) = a.shape, b.shape
    else:
        (k, m), (_, n) = a.shape, b.shape
    tm, tn, tk = _pick(m, tm), _pick(n, tn), _pick(k, tk)
    nk = k // tk
    dims = {"nn": _NN, "nt": _NT, "tn": _TN}[mode]
    if mode == "tn":
        a_spec = pl.BlockSpec((tk, tm), lambda i, j, kk: (kk, i))
    else:
        a_spec = pl.BlockSpec((tm, tk), lambda i, j, kk: (i, kk))
    if mode == "nt":
        b_spec = pl.BlockSpec((tn, tk), lambda i, j, kk: (j, kk))
    else:
        b_spec = pl.BlockSpec((tk, tn), lambda i, j, kk: (kk, j))
    o_spec = pl.BlockSpec((tm, tn), lambda i, j, kk: (i, j))
    in_specs = [a_spec, b_spec] + ([o_spec] if res is not None else [])

    def body(*refs):
        if res is not None:
            a_ref, b_ref, r_ref, o_ref, acc_ref = refs
        else:
            a_ref, b_ref, o_ref, acc_ref = refs
            r_ref = None
        kk = pl.program_id(2)
        part = _dot(a_ref[...].astype(BF16), b_ref[...].astype(BF16), dims)

        def finish(total):
            if r_ref is not None:
                total = total + r_ref[...].astype(F32)
            o_ref[...] = total.astype(out_dtype)

        if nk == 1:
            finish(part)
        else:
            @pl.when(kk == 0)
            def _():
                acc_ref[...] = part

            @pl.when(kk > 0)
            def _():
                acc_ref[...] += part

            @pl.when(kk == nk - 1)
            def _():
                finish(acc_ref[...])

    args = (a, b) + ((res,) if res is not None else ())
    return pl.pallas_call(
        body, name=name, grid=(m // tm, n // tn, nk), in_specs=in_specs, out_specs=o_spec,
        out_shape=jax.ShapeDtypeStruct((m, n), out_dtype),
        scratch_shapes=[pltpu.VMEM((tm, tn) if nk > 1 else (8, LANES), F32)],
        compiler_params=_params(("parallel", "parallel", "arbitrary")),
    )(*args)


def _norm_mm(x, g, w, out_dtype, tm=1024, tn=1536, name="norm_mm"):
    m, d = x.shape
    n = w.shape[1]
    tm, tn = _pick(m, tm), _pick(n, tn)

    def body(x_ref, g_ref, w_ref, o_ref, hn_ref):
        @pl.when(pl.program_id(1) == 0)
        def _():
            xf = x_ref[...]
            r = lax.rsqrt(jnp.mean(xf * xf, axis=-1, keepdims=True) + NORM_EPS)
            hn_ref[...] = (xf * r * g_ref[...]).astype(BF16)

        o_ref[...] = _dot(hn_ref[...], w_ref[...], _NN).astype(out_dtype)

    return pl.pallas_call(
        body, name=name, grid=(m // tm, n // tn),
        in_specs=[pl.BlockSpec((tm, d), lambda i, j: (i, 0)),
                  pl.BlockSpec((1, d), lambda i, j: (0, 0)),
                  pl.BlockSpec((d, tn), lambda i, j: (0, j))],
        out_specs=[pl.BlockSpec((tm, tn), lambda i, j: (i, j)),
                   pl.BlockSpec((tm, d), lambda i, j: (i, 0))],
        out_shape=[jax.ShapeDtypeStruct((m, n), out_dtype), jax.ShapeDtypeStruct((m, d), BF16)],
        compiler_params=_params(("parallel", "arbitrary")),
    )(x, g.reshape(1, d), w)


def _norm_bwd(dhn, x, g, dx_in, tm=256, name="norm_bwd"):
    m, d = x.shape
    tm = min(tm, m)

    def body(dhn_ref, x_ref, g_ref, dxi_ref, dx_ref, gg_ref):
        xf = x_ref[...]
        r = lax.rsqrt(jnp.mean(xf * xf, axis=-1, keepdims=True) + NORM_EPS)
        xhat = xf * r
        dh = dhn_ref[...].astype(F32)
        dxn = dh * g_ref[...]
        mean = jnp.mean(dxn * xhat, axis=-1, keepdims=True)
        dx_ref[...] = dxi_ref[...] + r * (dxn - xhat * mean)
        part = jnp.sum(dh * xhat, axis=0, keepdims=True)

        @pl.when(pl.program_id(0) == 0)
        def _():
            gg_ref[...] = part

        @pl.when(pl.program_id(0) > 0)
        def _():
            gg_ref[...] += part

    row = pl.BlockSpec((tm, d), lambda i: (i, 0))
    vec = pl.BlockSpec((1, d), lambda i: (0, 0))
    dx, gg = pl.pallas_call(
        body, name=name, grid=(m // tm,), in_specs=[row, row, vec, row], out_specs=[row, vec],
        out_shape=[jax.ShapeDtypeStruct((m, d), F32), jax.ShapeDtypeStruct((1, d), F32)],
        compiler_params=_params(("arbitrary",)),
    )(dhn, x, g.reshape(1, d), dx_in)
    return dx, gg.reshape(d)


def _down(u, k):
    s = u.shape[0]
    rows = lax.broadcasted_iota(jnp.int32, u.shape, 0)
    return jnp.where(rows < k, 0.0, pltpu.roll(u, k, axis=0))


def _up(u, k):
    s = u.shape[0]
    rows = lax.broadcasted_iota(jnp.int32, u.shape, 0)
    return jnp.where(rows >= s - k, 0.0, pltpu.roll(u, s - k, axis=0))


def _conv_mix_fwd(proj, conv_w, tc=128):
    s = proj.shape[0]
    nb = WIDTH // tc
    off = CONV_OFF // tc

    def body(b_ref, c_ref, h_ref, w_ref, o_ref):
        u = c_ref[...].astype(F32) * h_ref[...].astype(F32)
        w = w_ref[...]
        cv = w[0:1] * _down(u, 2) + w[1:2] * _down(u, 1) + w[2:3] * u
        o_ref[...] = (b_ref[...].astype(F32) * cv).astype(BF16)

    col = lambda k: pl.BlockSpec((s, tc), lambda j, k=k: (0, off + k * nb + j))
    return pl.pallas_call(
        body, name="conv_mix_fwd", grid=(nb,),
        in_specs=[col(0), col(1), col(2), pl.BlockSpec((3, tc), lambda j: (0, j))],
        out_specs=pl.BlockSpec((s, tc), lambda j: (0, j)),
        out_shape=jax.ShapeDtypeStruct((s, WIDTH), BF16),
        compiler_params=_params(("parallel",)),
    )(proj, proj, proj, conv_w)


def _conv_mix_bwd(do, proj, conv_w, tc=128):
    s = proj.shape[0]
    nb = WIDTH // tc
    off = CONV_OFF // tc

    def body(do_ref, b_ref, c_ref, h_ref, w_ref, db_ref, dc_ref, dh_ref, gw_ref):
        b = b_ref[...].astype(F32)
        c = c_ref[...].astype(F32)
        h = h_ref[...].astype(F32)
        g = do_ref[...].astype(F32)
        w = w_ref[...]
        u = c * h
        u1, u2 = _down(u, 1), _down(u, 2)
        cv = w[0:1] * u2 + w[1:2] * u1 + w[2:3] * u
        db_ref[...] = (g * cv).astype(BF16)
        dcv = g * b
        gw_ref[0:1, :] = jnp.sum(dcv * u2, axis=0, keepdims=True)
        gw_ref[1:2, :] = jnp.sum(dcv * u1, axis=0, keepdims=True)
        gw_ref[2:3, :] = jnp.sum(dcv * u, axis=0, keepdims=True)
        du = w[2:3] * dcv + w[1:2] * _up(dcv, 1) + w[0:1] * _up(dcv, 2)
        dc_ref[...] = (du * h).astype(BF16)
        dh_ref[...] = (du * c).astype(BF16)

    col = lambda k: pl.BlockSpec((s, tc), lambda j, k=k: (0, off + k * nb + j))
    own = pl.BlockSpec((s, tc), lambda j: (0, j))
    wsp = pl.BlockSpec((3, tc), lambda j: (0, j))
    act = jax.ShapeDtypeStruct((s, WIDTH), BF16)
    return pl.pallas_call(
        body, name="conv_mix_bwd", grid=(nb,),
        in_specs=[own, col(0), col(1), col(2), wsp], out_specs=[own, own, own, wsp],
        out_shape=[act, act, act, jax.ShapeDtypeStruct((3, WIDTH), F32)],
        compiler_params=_params(("parallel",)),
    )(do, proj, proj, proj, conv_w)


def _glu_fwd(up, w, b, tc=256):
    s = up.shape[0]
    nb = D_FF // tc

    def body(g_ref, v_ref, w_ref, b_ref, o_ref):
        ug = g_ref[...].astype(F32)
        wv = w_ref[...]
        cg = wv[0:1] * _down(ug, 2) + wv[1:2] * _down(ug, 1) + wv[2:3] * ug + b_ref[...]
        act = cg * jax.nn.sigmoid(cg)
        o_ref[...] = (act * v_ref[...].astype(F32)).astype(BF16)

    return pl.pallas_call(
        body, name="glu_fwd", grid=(nb,),
        in_specs=[pl.BlockSpec((s, tc), lambda j: (0, j)), pl.BlockSpec((s, tc), lambda j: (0, nb + j)),
                  pl.BlockSpec((3, tc), lambda j: (0, j)), pl.BlockSpec((1, tc), lambda j: (0, j))],
        out_specs=pl.BlockSpec((s, tc), lambda j: (0, j)),
        out_shape=jax.ShapeDtypeStruct((s, D_FF), BF16),
        compiler_params=_params(("parallel",)),
    )(up, up, w, b.reshape(1, D_FF))


def _glu_bwd(dh, up, w, b, tc=256):
    s = up.shape[0]
    nb = D_FF // tc

    def body(dh_ref, g_ref, v_ref, w_ref, b_ref, dg_ref, dv_ref, gw_ref, gb_ref):
        ug = g_ref[...].astype(F32)
        uv = v_ref[...].astype(F32)
        d = dh_ref[...].astype(F32)
        wv = w_ref[...]
        u1, u2 = _down(ug, 1), _down(ug, 2)
        cg = wv[0:1] * u2 + wv[1:2] * u1 + wv[2:3] * ug + b_ref[...]
        sg = jax.nn.sigmoid(cg)
        dv_ref[...] = (d * (cg * sg)).astype(BF16)
        dcg = d * uv * (sg * (1.0 + cg * (1.0 - sg)))
        gb_ref[...] = jnp.sum(dcg, axis=0, keepdims=True)
        gw_ref[0:1, :] = jnp.sum(dcg * u2, axis=0, keepdims=True)
        gw_ref[1:2, :] = jnp.sum(dcg * u1, axis=0, keepdims=True)
        gw_ref[2:3, :] = jnp.sum(dcg * ug, axis=0, keepdims=True)
        dg_ref[...] = (wv[2:3] * dcg + wv[1:2] * _up(dcg, 1) + wv[0:1] * _up(dcg, 2)).astype(BF16)

    own = pl.BlockSpec((s, tc), lambda j: (0, j))
    wsp = pl.BlockSpec((3, tc), lambda j: (0, j))
    bsp = pl.BlockSpec((1, tc), lambda j: (0, j))
    act = jax.ShapeDtypeStruct((s, D_FF), BF16)
    dg, dv, gw, gb = pl.pallas_call(
        body, name="glu_bwd", grid=(nb,),
        in_specs=[own, own, pl.BlockSpec((s, tc), lambda j: (0, nb + j)), wsp, bsp],
        out_specs=[own, own, wsp, bsp],
        out_shape=[act, act, jax.ShapeDtypeStruct((3, D_FF), F32), jax.ShapeDtypeStruct((1, D_FF), F32)],
        compiler_params=_params(("parallel",)),
    )(dh, up, up, w, b.reshape(1, D_FF))
    return dg, dv, gw, gb.reshape(D_FF)


def _merge_fwd(x, oc, of, osb, wpc, wpf, wps, proj, gb, wout, tm=256):
    s, d = x.shape
    tm = min(tm, s)

    def body(x_ref, oc_ref, of_ref, os_ref, wpc_ref, wpf_ref, wps_ref, g0_ref, g1_ref, g2_ref, gb_ref, wo_ref,
             xo_ref, mg_ref, yc_ref, yf_ref, ys_ref):
        merged = jnp.zeros((tm, d), F32)
        for k, (o_ref, w_ref, g_ref, y_ref) in enumerate(
                ((oc_ref, wpc_ref, g0_ref, yc_ref), (of_ref, wpf_ref, g1_ref, yf_ref), (os_ref, wps_ref, g2_ref, ys_ref))):
            y = _dot(o_ref[...].astype(BF16), w_ref[...], _NN)
            y_ref[...] = y.astype(BF16)
            gate = jax.nn.sigmoid(g_ref[...].astype(F32) + gb_ref[:, k * d:(k + 1) * d])
            merged = merged + gate * y
        mb = merged.astype(BF16)
        mg_ref[...] = mb
        xo_ref[...] = x_ref[...] + _dot(mb, wo_ref[...], _NN)

    rowd = pl.BlockSpec((tm, d), lambda i: (i, 0))
    roww = pl.BlockSpec((tm, WIDTH), lambda i: (i, 0))
    wp = pl.BlockSpec((WIDTH, d), lambda i: (0, 0))
    gcol = lambda k: pl.BlockSpec((tm, d), lambda i, k=k: (i, GATE_OFF // d + k))
    actd = jax.ShapeDtypeStruct((s, d), BF16)
    return pl.pallas_call(
        body, name="merge_fwd", grid=(s // tm,),
        in_specs=[rowd, roww, roww, roww, wp, wp, wp, gcol(0), gcol(1), gcol(2),
                  pl.BlockSpec((1, 3 * d), lambda i: (0, 0)), pl.BlockSpec((d, d), lambda i: (0, 0))],
        out_specs=[rowd, rowd, rowd, rowd, rowd],
        out_shape=[jax.ShapeDtypeStruct((s, d), F32), actd, actd, actd, actd],
        compiler_params=_params(("parallel",)),
    )(x, oc, of, osb, wpc, wpf, wps, proj, proj, proj, gb.reshape(1, 3 * d), wout)


def _gate_bwd(dm, yc, yf, ys, proj, gb, tm=256):
    s, d = dm.shape
    tm = min(tm, s)

    def body(dm_ref, yc_ref, yf_ref, ys_ref, g0_ref, g1_ref, g2_ref, gb_ref, dyc_ref, dyf_ref, dys_ref, dgl_ref, ggb_ref):
        g = dm_ref[...].astype(F32)
        parts = []
        for k, (y_ref, g_ref, dy_ref) in enumerate(((yc_ref, g0_ref, dyc_ref), (yf_ref, g1_ref, dyf_ref), (ys_ref, g2_ref, dys_ref))):
            gate = jax.nn.sigmoid(g_ref[...].astype(F32) + gb_ref[:, k * d:(k + 1) * d])
            dy_ref[...] = (g * gate).astype(BF16)
            dgl = g * y_ref[...].astype(F32) * gate * (1.0 - gate)
            dgl_ref[:, k * d:(k + 1) * d] = dgl.astype(BF16)
            parts.append(jnp.sum(dgl, axis=0, keepdims=True))
        part = jnp.concatenate(parts, axis=1)

        @pl.when(pl.program_id(0) == 0)
        def _():
            ggb_ref[...] = part

        @pl.when(pl.program_id(0) > 0)
        def _():
            ggb_ref[...] += part

    rowd = pl.BlockSpec((tm, d), lambda i: (i, 0))
    gcol = lambda k: pl.BlockSpec((tm, d), lambda i, k=k: (i, GATE_OFF // d + k))
    vec = pl.BlockSpec((1, 3 * d), lambda i: (0, 0))
    actd = jax.ShapeDtypeStruct((s, d), BF16)
    dyc, dyf, dys, dgl, ggb = pl.pallas_call(
        body, name="gate_bwd", grid=(s // tm,),
        in_specs=[rowd, rowd, rowd, rowd, gcol(0), gcol(1), gcol(2), vec],
        out_specs=[rowd, rowd, rowd, pl.BlockSpec((tm, 3 * d), lambda i: (i, 0)), vec],
        out_shape=[actd, actd, actd, jax.ShapeDtypeStruct((s, 3 * d), BF16), jax.ShapeDtypeStruct((1, 3 * d), F32)],
        compiler_params=_params(("arbitrary",)),
    )(dm, yc, yf, ys, proj, proj, proj, gb.reshape(1, 3 * d))
    return dyc, dyf, dys, dgl, ggb.reshape(3 * d)


def _tri_tables(nq, r, order):
    last = lambda qi: (qi + 1) * r - 1
    if order == "k_outer":
        pairs = [(qi, kj) for kj in range(nq * r) for qi in range(kj // r, nq)]
    elif order == "k_desc":
        pairs = [(qi, kj) for qi in range(nq) for kj in range(last(qi), -1, -1)]
    else:
        pairs = [(qi, kj) for qi in range(nq) for kj in range(last(qi) + 1)]
    qs, ks = zip(*pairs)
    return jnp.asarray(np.array(qs, np.int32)), jnp.asarray(np.array(ks, np.int32)), len(pairs)


def _lo_mask(shape):
    return lax.broadcasted_iota(jnp.int32, shape, len(shape) - 1) < HEAD_DIM


def _head(x, h):
    lo = _lo_mask(x.shape)
    return jnp.where(lo if h == 0 else jnp.logical_not(lo), x, jnp.zeros_like(x))


def _pair_cols(x, h):
    return x[:, h * HEAD_DIM:h * HEAD_DIM + 1]


def _rep(a0, a1, shape):
    return jnp.where(_lo_mask(shape), a0, a1)


def _positions(qi, kj, tq, tk):
    row = qi * tq + lax.broadcasted_iota(jnp.int32, (tq, tk), 0)
    col = kj * tk + lax.broadcasted_iota(jnp.int32, (tq, tk), 1)
    return row, col


def _head_norm(x, g):
    lo = _lo_mask(x.shape)
    sq = x * x
    s0 = jnp.sum(jnp.where(lo, sq, 0.0), axis=-1, keepdims=True)
    s1 = jnp.sum(jnp.where(lo, 0.0, sq), axis=-1, keepdims=True)
    r = jnp.where(lo, lax.rsqrt(s0 / HEAD_DIM + NORM_EPS), lax.rsqrt(s1 / HEAD_DIM + NORM_EPS))
    return x * r, r


def _fox_prep(proj, gq, gk, tm=512):
    s = proj.shape[0]
    tm = min(tm, s)
    off = FOX_OFF // LANES

    def body(q_ref, k_ref, gq_ref, gk_ref, qn_ref, kn_ref):
        qh, _ = _head_norm(q_ref[...].astype(F32), None)
        kh, _ = _head_norm(k_ref[...].astype(F32), None)
        qn_ref[...] = (qh * gq_ref[...] * QK_SCALE).astype(BF16)
        kn_ref[...] = (kh * gk_ref[...]).astype(BF16)

    vec = pl.BlockSpec((1, LANES), lambda p, i: (0, 0))
    own = pl.BlockSpec((tm, LANES), lambda p, i: (i, p))
    act = jax.ShapeDtypeStruct((s, WIDTH), BF16)
    return pl.pallas_call(
        body, name="fox_prep", grid=(N_PAIR, s // tm),
        in_specs=[pl.BlockSpec((tm, LANES), lambda p, i: (i, off + p)),
                  pl.BlockSpec((tm, LANES), lambda p, i: (i, off + N_PAIR + p)), vec, vec],
        out_specs=[own, own], out_shape=[act, act],
        compiler_params=_params(("parallel", "parallel")),
    )(proj, proj, jnp.tile(gq, 2).reshape(1, LANES), jnp.tile(gk, 2).reshape(1, LANES))


def _fox_post(dqs, dkn, proj, gq, gk, tm=512):
    s = proj.shape[0]
    tm = min(tm, s)
    off = FOX_OFF // LANES

    def one(d_ref, x_ref, g_ref, scale, dx_ref, gg_ref, first):
        xhat, r = _head_norm(x_ref[...].astype(F32), None)
        dy = d_ref[...] * scale
        part = jnp.sum(dy * xhat, axis=0, keepdims=True)

        @pl.when(first)
        def _():
            gg_ref[...] = part

        @pl.when(jnp.logical_not(first))
        def _():
            gg_ref[...] += part

        dxh = dy * g_ref[...]
        lo = _lo_mask(dxh.shape)
        pr = dxh * xhat
        m0 = jnp.sum(jnp.where(lo, pr, 0.0), axis=-1, keepdims=True)
        m1 = jnp.sum(jnp.where(lo, 0.0, pr), axis=-1, keepdims=True)
        mean = jnp.where(lo, m0, m1) / HEAD_DIM
        dx_ref[...] = (r * (dxh - xhat * mean)).astype(BF16)

    def body(dq_ref, dk_ref, q_ref, k_ref, gq_ref, gk_ref, dxq_ref, dxk_ref, ggq_ref, ggk_ref):
        first = pl.program_id(1) == 0
        one(dq_ref, q_ref, gq_ref, QK_SCALE, dxq_ref, ggq_ref, first)
        one(dk_ref, k_ref, gk_ref, 1.0, dxk_ref, ggk_ref, first)

    vec = pl.BlockSpec((1, LANES), lambda p, i: (0, 0))
    own = pl.BlockSpec((tm, LANES), lambda p, i: (i, p))
    ggs = pl.BlockSpec((None, 1, LANES), lambda p, i: (p, 0, 0))
    act = jax.ShapeDtypeStruct((s, WIDTH), BF16)
    ggo = jax.ShapeDtypeStruct((N_PAIR, 1, LANES), F32)
    dxq, dxk, ggq, ggk = pl.pallas_call(
        body, name="fox_post", grid=(N_PAIR, s // tm),
        in_specs=[own, own, pl.BlockSpec((tm, LANES), lambda p, i: (i, off + p)),
                  pl.BlockSpec((tm, LANES), lambda p, i: (i, off + N_PAIR + p)), vec, vec],
        out_specs=[own, own, ggs, ggs], out_shape=[act, act, ggo, ggo],
        compiler_params=_params(("parallel", "arbitrary")),
    )(dqs, dkn, proj, proj, jnp.tile(gq, 2).reshape(1, LANES), jnp.tile(gk, 2).reshape(1, LANES))
    fold = lambda a: a.reshape(N_HEADS, HEAD_DIM).sum(axis=0)
    return dxq, dxk, fold(ggq), fold(ggk)


def _split3(x):
    a = x.astype(BF16)
    r = x - a.astype(F32)
    b = r.astype(BF16)
    c = (r - b.astype(F32)).astype(BF16)
    return a, b, c


def _split2(x):
    a = x.astype(BF16)
    b = (x - a.astype(F32)).astype(BF16)
    return a, b


def _log_sigmoid(x):
    return jnp.minimum(x, 0.0) - jnp.log(1.0 + jnp.exp(-jnp.abs(x)))


def _fox_gates(ft, bias):
    h, s = ft.shape
    nb = s // LANES

    def body(f_ref, b_ref, c_ref):
        lf = _log_sigmoid(f_ref[...] + b_ref[...])
        i = lax.broadcasted_iota(jnp.int32, (s, LANES), 0)
        j = pl.program_id(0) * LANES + lax.broadcasted_iota(jnp.int32, (s, LANES), 1)
        tri = jnp.where(i <= j, 1.0, 0.0).astype(BF16)
        c_ref[...] = sum(_dot(p, tri, _NN) for p in _split3(lf))

    return pl.pallas_call(
        body, name="fox_gates", grid=(nb,),
        in_specs=[pl.BlockSpec((h, s), lambda j: (0, 0)), pl.BlockSpec((h, 1), lambda j: (0, 0))],
        out_specs=pl.BlockSpec((h, LANES), lambda j: (0, j)),
        out_shape=jax.ShapeDtypeStruct((h, s), F32),
        compiler_params=_params(("parallel",)),
    )(ft, bias.reshape(h, 1))


def _fox_gates_bwd(dc, ft, bias):
    h, s = ft.shape
    nb = s // LANES

    def body(dc_ref, f_ref, fb_ref, b_ref, df_ref, gb_ref):
        i = lax.broadcasted_iota(jnp.int32, (s, LANES), 0)
        j = pl.program_id(0) * LANES + lax.broadcasted_iota(jnp.int32, (s, LANES), 1)
        tri = jnp.where(i >= j, 1.0, 0.0).astype(BF16)
        dlf = sum(_dot(p, tri, _NN) for p in _split3(dc_ref[...]))
        df = dlf * jax.nn.sigmoid(-(fb_ref[...] + b_ref[...]))
        df_ref[...] = df
        part = jnp.sum(df, axis=-1, keepdims=True)

        @pl.when(pl.program_id(0) == 0)
        def _():
            gb_ref[...] = part

        @pl.when(pl.program_id(0) > 0)
        def _():
            gb_ref[...] += part

    full = pl.BlockSpec((h, s), lambda j: (0, 0))
    blk = pl.BlockSpec((h, LANES), lambda j: (0, j))
    one = pl.BlockSpec((h, 1), lambda j: (0, 0))
    df, gb = pl.pallas_call(
        body, name="fox_gates_bwd", grid=(nb,), in_specs=[full, full, blk, one], out_specs=[blk, one],
        out_shape=[jax.ShapeDtypeStruct((h, s), F32), jax.ShapeDtypeStruct((h, 1), F32)],
        compiler_params=_params(("arbitrary",)),
    )(dc, ft, ft, bias.reshape(h, 1))
    return df, gb.reshape(h)


def _delta_rep(do, o, tm=512):
    s = do.shape[0]
    tm = min(tm, s)

    def body(do_ref, o_ref, d_ref):
        pr = do_ref[...].astype(F32) * o_ref[...].astype(F32)
        lo = _lo_mask(pr.shape)
        d0 = jnp.sum(jnp.where(lo, pr, 0.0), axis=-1, keepdims=True)
        d1 = jnp.sum(jnp.where(lo, 0.0, pr), axis=-1, keepdims=True)
        d_ref[...] = jnp.where(lo, d0, d1)

    own = pl.BlockSpec((tm, LANES), lambda p, i: (i, p))
    return pl.pallas_call(
        body, name="delta_rep", grid=(N_PAIR, s // tm), in_specs=[own, own],
        out_specs=pl.BlockSpec((None, tm, LANES), lambda p, i: (p, i, 0)),
        out_shape=jax.ShapeDtypeStruct((N_PAIR, s, LANES), F32),
        compiler_params=_params(("parallel", "parallel")),
    )(do, o)


def _tile(s, t):
    t = min(t, s)
    assert s % t == 0
    return t


def _fox_fwd(qn, kn, proj, ccol, crow, t=512, comm=None):
    s = qn.shape[0]
    t = _tile(s, t)
    n = s // t
    qtab, ktab, ntri = _tri_tables(n, 1, "k_asc")
    voff = FOX_OFF // LANES + 2 * N_PAIR
    n_in, n_out = (len(comm.inputs), len(comm.out_shapes)) if comm else (0, 0)

    def body(qt_ref, kt_ref, q_ref, k_ref, v_ref, cc_ref, cr_ref, *rest):
        comm_in, rest = rest[:n_in], rest[n_in:]
        (o_ref, lse_ref), rest = rest[:2], rest[2:]
        comm_out, rest = rest[:n_out], rest[n_out:]
        (m_ref, l_ref, acc_ref), sems = rest[:3], rest[3:]
        i = pl.program_id(1)
        qi, kj = qt_ref[i], kt_ref[i]
        if comm:
            pl.when((pl.program_id(0) == 0) & (i == 0))(lambda: comm.start(comm_in, comm_out, sems))

        @pl.when(kj == 0)
        def _():
            m_ref[...] = jnp.full(m_ref.shape, NEG_INF, F32)
            l_ref[...] = jnp.zeros(l_ref.shape, F32)
            acc_ref[...] = jnp.zeros(acc_ref.shape, F32)

        q, k, v = q_ref[...], k_ref[...], v_ref[...]
        row, col = _positions(qi, kj, t, t)
        causal = col <= row
        m_old = m_ref[...]
        mn, rs, pv = [], [], []
        for h in range(2):
            sc = _dot(_head(q, h), k, _NT) + _pair_cols(cc_ref[...], h) - cr_ref[h:h + 1, :]
            sc = jnp.where(causal, sc, NEG_INF)
            m_new = jnp.maximum(_pair_cols(m_old, h), jnp.max(sc, axis=-1, keepdims=True))
            p = jnp.exp(sc - m_new)
            mn.append(m_new)
            rs.append(jnp.sum(p, axis=-1, keepdims=True))
            pv.append(_dot(p.astype(BF16), _head(v, h), _NN))
        m_rep = _rep(mn[0], mn[1], m_old.shape)
        alpha = jnp.exp(m_old - m_rep)
        l_ref[...] = alpha * l_ref[...] + _rep(rs[0], rs[1], m_old.shape)
        acc_ref[...] = alpha * acc_ref[...] + pv[0] + pv[1]
        m_ref[...] = m_rep

        @pl.when(kj == qi)
        def _():
            o_ref[...] = acc_ref[...] / l_ref[...]
            lse_ref[...] = m_ref[...] + jnp.log(l_ref[...])

        if comm:
            pl.when((pl.program_id(0) == N_PAIR - 1) & (i == ntri - 1))(lambda: comm.finish(comm_in, comm_out, sems))

    grid_spec = pltpu.PrefetchScalarGridSpec(
        num_scalar_prefetch=2, grid=(N_PAIR, ntri),
        in_specs=[pl.BlockSpec((t, LANES), lambda p, i, qt, kt: (qt[i], p)),
                  pl.BlockSpec((t, LANES), lambda p, i, qt, kt: (kt[i], p)),
                  pl.BlockSpec((t, LANES), lambda p, i, qt, kt: (kt[i], voff + p)),
                  pl.BlockSpec((None, t, LANES), lambda p, i, qt, kt: (p, qt[i], 0)),
                  pl.BlockSpec((None, 2, t), lambda p, i, qt, kt: (p, 0, kt[i]))] + [ANY] * n_in,
        out_specs=[pl.BlockSpec((t, LANES), lambda p, i, qt, kt: (qt[i], p)),
                   pl.BlockSpec((None, t, LANES), lambda p, i, qt, kt: (p, qt[i], 0))] + [ANY] * n_out,
        scratch_shapes=[pltpu.VMEM((t, LANES), F32)] * 3 + (comm.sems if comm else []))
    outs = pl.pallas_call(
        body, name="fox_fwd", grid_spec=grid_spec,
        out_shape=[jax.ShapeDtypeStruct((s, WIDTH), F32), jax.ShapeDtypeStruct((N_PAIR, s, LANES), F32)]
        + (comm.out_shapes if comm else []),
        compiler_params=_params(("arbitrary", "arbitrary") if comm else ("parallel", "arbitrary")),
    )(qtab, ktab, qn, kn, proj, ccol, crow, *(comm.inputs if comm else []))
    return outs[:2], outs[2:]


def _fox_bwd(qn, kn, proj, do, lse, delta, ccol, crow, t=512, comm=None):
    s = qn.shape[0]
    t = _tile(s, t)
    n = s // t
    qtab, ktab, ntri = _tri_tables(n, 1, "k_outer")
    voff = FOX_OFF // LANES + 2 * N_PAIR
    n_in, n_out = (len(comm.inputs), len(comm.out_shapes)) if comm else (0, 0)

    def body(qt_ref, kt_ref, q_ref, k_ref, v_ref, do_ref, lse_ref, dl_ref, cc_ref, cr_ref, *rest):
        comm_in, rest = rest[:n_in], rest[n_in:]
        (dq_ref, dk_ref, dv_ref, dc_ref, dcq_ref), rest = rest[:5], rest[5:]
        comm_out, rest = rest[:n_out], rest[n_out:]
        (dka_ref, dva_ref, dca_ref), sems = rest[:3], rest[3:]
        i = pl.program_id(1)
        qi, kj = qt_ref[i], kt_ref[i]
        if comm:
            pl.when((pl.program_id(0) == 0) & (i == 0))(lambda: comm.start(comm_in, comm_out, sems))

        @pl.when(i == 0)
        def _():
            dq_ref[...] = jnp.zeros(dq_ref.shape, F32)
            dcq_ref[...] = jnp.zeros(dcq_ref.shape, F32)

        @pl.when(qi == kj)
        def _():
            dka_ref[...] = jnp.zeros(dka_ref.shape, F32)
            dva_ref[...] = jnp.zeros(dva_ref.shape, F32)
            dca_ref[...] = jnp.zeros(dca_ref.shape, F32)

        q, k, v, g = q_ref[...], k_ref[...], v_ref[...], do_ref[...]
        row, col = _positions(qi, kj, t, t)
        causal = col <= row
        dq = jnp.zeros((t, LANES), F32)
        dk = jnp.zeros((t, LANES), F32)
        dv = jnp.zeros((t, LANES), F32)
        rowsum = []
        for h in range(2):
            qh, gh = _head(q, h), _head(g, h)
            sc = _dot(qh, k, _NT) + _pair_cols(cc_ref[...], h) - cr_ref[h:h + 1, :]
            p = jnp.where(causal, jnp.exp(sc - _pair_cols(lse_ref[...], h)), 0.0)
            dp = _dot(gh, v, _NT)
            ds = p * (dp - _pair_cols(dl_ref[...], h))
            dsb = ds.astype(BF16)
            dv = dv + _dot(p.astype(BF16), gh, _TN)
            dk = dk + _dot(dsb, qh, _TN)
            dq = dq + _dot(dsb, _head(k, h), _NN)
            dca_ref[h:h + 1, :] -= jnp.sum(ds, axis=0, keepdims=True)
            rowsum.append(jnp.sum(ds, axis=-1, keepdims=True))
        dka_ref[...] += dk
        dva_ref[...] += dv
        rows = pl.ds(pl.multiple_of(qi * t, t), t)
        dq_ref[rows, :] += dq
        dcq_ref[rows, :] += _rep(rowsum[0], rowsum[1], (t, LANES))

        @pl.when(qi == n - 1)
        def _():
            dk_ref[...] = dka_ref[...]
            dv_ref[...] = dva_ref[...].astype(BF16)
            dc_ref[...] = dca_ref[...]

        if comm:
            pl.when((pl.program_id(0) == N_PAIR - 1) & (i == ntri - 1))(lambda: comm.finish(comm_in, comm_out, sems))

    qblk = lambda p, i, qt, kt: (qt[i], p)
    kblk = lambda p, i, qt, kt: (kt[i], p)
    qrep = pl.BlockSpec((None, t, LANES), lambda p, i, qt, kt: (p, qt[i], 0))
    crs = pl.BlockSpec((None, 2, t), lambda p, i, qt, kt: (p, 0, kt[i]))
    grid_spec = pltpu.PrefetchScalarGridSpec(
        num_scalar_prefetch=2, grid=(N_PAIR, ntri),
        in_specs=[pl.BlockSpec((t, LANES), qblk), pl.BlockSpec((t, LANES), kblk),
                  pl.BlockSpec((t, LANES), lambda p, i, qt, kt: (kt[i], voff + p)),
                  pl.BlockSpec((t, LANES), qblk), qrep, qrep, qrep, crs] + [ANY] * n_in,
        out_specs=[pl.BlockSpec((s, LANES), lambda p, i, qt, kt: (0, p)),
                   pl.BlockSpec((t, LANES), kblk), pl.BlockSpec((t, LANES), kblk), crs,
                   pl.BlockSpec((None, s, LANES), lambda p, i, qt, kt: (p, 0, 0))] + [ANY] * n_out,
        scratch_shapes=[pltpu.VMEM((t, LANES), F32), pltpu.VMEM((t, LANES), F32), pltpu.VMEM((2, t), F32)]
        + (comm.sems if comm else []))
    outs = pl.pallas_call(
        body, name="fox_bwd", grid_spec=grid_spec,
        out_shape=[jax.ShapeDtypeStruct((s, WIDTH), F32), jax.ShapeDtypeStruct((s, WIDTH), F32),
                   jax.ShapeDtypeStruct((s, WIDTH), BF16), jax.ShapeDtypeStruct((N_PAIR, 2, s), F32),
                   jax.ShapeDtypeStruct((N_PAIR, s, LANES), F32)] + (comm.out_shapes if comm else []),
        compiler_params=_params(("arbitrary", "arbitrary") if comm else ("parallel", "arbitrary")),
    )(qtab, ktab, qn, kn, proj, do, lse, delta, ccol, crow, *(comm.inputs if comm else []))
    return outs[:5], outs[5:]


def _sb_tile(qh, k, strict, tk, r_col):
    z = _dot(qh, k, _NT)
    lg = jnp.where(strict, -(jnp.maximum(z, 0.0) + jnp.log(1.0 + jnp.exp(-jnp.abs(z)))), 0.0)
    jj = lax.broadcasted_iota(jnp.int32, (tk, tk), 0)
    ss = lax.broadcasted_iota(jnp.int32, (tk, tk), 1)
    above = jnp.where(jj > ss, 1.0, 0.0).astype(BF16)
    suffix = sum(_dot(p, above, _NN) for p in _split2(lg)) + r_col
    a = jnp.where(strict, jnp.exp(lg + z + suffix), 0.0)
    return z, lg, a


def _sb_fwd(proj, tq=512, tk=256, comm=None):
    s = proj.shape[0]
    tq = _tile(s, tq)
    tk = _tile(tq, tk)
    nq, r = s // tq, tq // tk
    qtab, ktab, ntri = _tri_tables(nq, r, "k_desc")
    off = SB_OFF // LANES
    n_in, n_out = (len(comm.inputs), len(comm.out_shapes)) if comm else (0, 0)

    def body(qt_ref, kt_ref, q_ref, k_ref, v_ref, *rest):
        comm_in, o_ref, rest = rest[:n_in], rest[n_in], rest[n_in + 1:]
        comm_out, rest = rest[:n_out], rest[n_out:]
        (acc_ref, r_ref), sems = rest[:2], rest[2:]
        i = pl.program_id(1)
        qi, kj = qt_ref[i], kt_ref[i]
        if comm:
            pl.when((pl.program_id(0) == 0) & (i == 0))(lambda: comm.start(comm_in, comm_out, sems))

        @pl.when(kj == (qi + 1) * r - 1)
        def _():
            acc_ref[...] = jnp.zeros(acc_ref.shape, F32)
            r_ref[...] = jnp.zeros(r_ref.shape, F32)

        q = q_ref[...] * QK_SCALE
        k, v = k_ref[...], v_ref[...]
        row, col = _positions(qi, kj, tq, tk)
        strict = col < row
        acc = acc_ref[...]
        for h in range(2):
            _, lg, a = _sb_tile(_head(q, h), k, strict, tk, r_ref[h])
            acc = acc + _dot(a.astype(BF16), _head(v, h), _NN)
            r_ref[h] += jnp.sum(lg, axis=-1, keepdims=True)
        acc_ref[...] = acc

        @pl.when(kj == 0)
        def _():
            o_ref[...] = acc_ref[...]

        if comm:
            pl.when((pl.program_id(0) == N_PAIR - 1) & (i == ntri - 1))(lambda: comm.finish(comm_in, comm_out, sems))

    grid_spec = pltpu.PrefetchScalarGridSpec(
        num_scalar_prefetch=2, grid=(N_PAIR, ntri),
        in_specs=[pl.BlockSpec((tq, LANES), lambda p, i, qt, kt: (qt[i], off + p)),
                  pl.BlockSpec((tk, LANES), lambda p, i, qt, kt: (kt[i], off + N_PAIR + p)),
                  pl.BlockSpec((tk, LANES), lambda p, i, qt, kt: (kt[i], off + 2 * N_PAIR + p))] + [ANY] * n_in,
        out_specs=[pl.BlockSpec((tq, LANES), lambda p, i, qt, kt: (qt[i], p))] + [ANY] * n_out,
        scratch_shapes=[pltpu.VMEM((tq, LANES), F32), pltpu.VMEM((2, tq, 1), F32)] + (comm.sems if comm else []))
    outs = pl.pallas_call(
        body, name="sb_fwd", grid_spec=grid_spec,
        out_shape=[jax.ShapeDtypeStruct((s, WIDTH), F32)] + (comm.out_shapes if comm else []),
        compiler_params=_params(("arbitrary", "arbitrary") if comm else ("parallel", "arbitrary")),
    )(qtab, ktab, proj, proj, proj, *(comm.inputs if comm else []))
    return outs[0], outs[1:]


def _sb_bwd(proj, do, delta, tq=512, tk=256, comm=None):
    s = proj.shape[0]
    tq = _tile(s, tq)
    tk = _tile(tq, tk)
    nq, r = s // tq, tq // tk
    qtab, ktab, ntri = _tri_tables(nq, r, "k_desc")
    off = SB_OFF // LANES
    n_in, n_out = (len(comm.inputs), len(comm.out_shapes)) if comm else (0, 0)

    def body(qt_ref, kt_ref, q_ref, k_ref, v_ref, do_ref, dl_ref, *rest):
        comm_in, rest = rest[:n_in], rest[n_in:]
        (dq_ref, dk_ref, dv_ref), rest = rest[:3], rest[3:]
        comm_out, rest = rest[:n_out], rest[n_out:]
        (dqa_ref, dka_ref, dva_ref, r_ref, rd_ref), sems = rest[:5], rest[5:]
        i = pl.program_id(1)
        qi, kj = qt_ref[i], kt_ref[i]
        if comm:
            pl.when((pl.program_id(0) == 0) & (i == 0))(lambda: comm.start(comm_in, comm_out, sems))

        @pl.when(i == 0)
        def _():
            dka_ref[...] = jnp.zeros(dka_ref.shape, F32)
            dva_ref[...] = jnp.zeros(dva_ref.shape, F32)

        @pl.when(kj == (qi + 1) * r - 1)
        def _():
            dqa_ref[...] = jnp.zeros(dqa_ref.shape, F32)
            r_ref[...] = jnp.zeros(r_ref.shape, F32)
            rd_ref[...] = jnp.zeros(rd_ref.shape, F32)

        q = q_ref[...] * QK_SCALE
        k, v, g = k_ref[...], v_ref[...], do_ref[...]
        row, col = _positions(qi, kj, tq, tk)
        strict = col < row
        ss = lax.broadcasted_iota(jnp.int32, (tk, tk), 0)
        jj = lax.broadcasted_iota(jnp.int32, (tk, tk), 1)
        at_or_after = jnp.where(ss >= jj, 1.0, 0.0).astype(BF16)
        dq = jnp.zeros((tq, LANES), F32)
        dk = jnp.zeros((tk, LANES), F32)
        dv = jnp.zeros((tk, LANES), F32)
        for h in range(2):
            qh, gh = _head(q, h), _head(g, h)
            z, lg, a = _sb_tile(qh, k, strict, tk, r_ref[h])
            da = _dot(gh, v, _NT)
            ab = a.astype(BF16)
            d_a = da * ab.astype(F32)
            incl = _dot(d_a.astype(BF16), at_or_after, _NN)
            d_l = _pair_cols(dl_ref[...], h) - rd_ref[h] - incl
            sig = jnp.exp(lg + z)
            dz = jnp.where(strict, d_a * (1.0 - sig) - d_l * sig, 0.0).astype(BF16)
            dq = dq + _dot(dz, _head(k, h), _NN)
            dk = dk + _dot(dz, qh, _TN)
            dv = dv + _dot(ab, gh, _TN)
            r_ref[h] += jnp.sum(lg, axis=-1, keepdims=True)
            rd_ref[h] += jnp.sum(d_a, axis=-1, keepdims=True)
        dqa_ref[...] += dq
        rows = pl.ds(pl.multiple_of(kj * tk, tk), tk)
        dka_ref[rows, :] += dk
        dva_ref[rows, :] += dv

        @pl.when(kj == 0)
        def _():
            dq_ref[...] = (dqa_ref[...] * QK_SCALE).astype(BF16)

        @pl.when(i == ntri - 1)
        def _():
            dk_ref[...] = dka_ref[...].astype(BF16)
            dv_ref[...] = dva_ref[...].astype(BF16)

        if comm:
            pl.when((pl.program_id(0) == N_PAIR - 1) & (i == ntri - 1))(lambda: comm.finish(comm_in, comm_out, sems))

    qblk = lambda p, i, qt, kt: (qt[i], p)
    whole = pl.BlockSpec((s, LANES), lambda p, i, qt, kt: (0, p))
    grid_spec = pltpu.PrefetchScalarGridSpec(
        num_scalar_prefetch=2, grid=(N_PAIR, ntri),
        in_specs=[pl.BlockSpec((tq, LANES), lambda p, i, qt, kt: (qt[i], off + p)),
                  pl.BlockSpec((tk, LANES), lambda p, i, qt, kt: (kt[i], off + N_PAIR + p)),
                  pl.BlockSpec((tk, LANES), lambda p, i, qt, kt: (kt[i], off + 2 * N_PAIR + p)),
                  pl.BlockSpec((tq, LANES), qblk),
                  pl.BlockSpec((None, tq, LANES), lambda p, i, qt, kt: (p, qt[i], 0))] + [ANY] * n_in,
        out_specs=[pl.BlockSpec((tq, LANES), qblk), whole, whole] + [ANY] * n_out,
        scratch_shapes=[pltpu.VMEM((tq, LANES), F32), pltpu.VMEM((s, LANES), F32), pltpu.VMEM((s, LANES), F32),
                        pltpu.VMEM((2, tq, 1), F32), pltpu.VMEM((2, tq, 1), F32)] + (comm.sems if comm else []))
    act = jax.ShapeDtypeStruct((s, WIDTH), BF16)
    outs = pl.pallas_call(
        body, name="sb_bwd", grid_spec=grid_spec, out_shape=[act, act, act] + (comm.out_shapes if comm else []),
        compiler_params=_params(("arbitrary", "arbitrary") if comm else ("parallel", "arbitrary")),
    )(qtab, ktab, proj, proj, proj, do, delta, *(comm.inputs if comm else []))
    return outs[:3], outs[3:]


def _loss_head(y, target, tm=256):
    s, d = y.shape
    tm = min(tm, s)

    def body(y_ref, t_ref, l_ref, dy_ref):
        e = y_ref[...] - t_ref[...]
        dy_ref[...] = e / d
        part = jnp.sum(jnp.sum(e * e, axis=0, keepdims=True), axis=1, keepdims=True)

        @pl.when(pl.program_id(0) == 0)
        def _():
            l_ref[...] = jnp.broadcast_to(part, l_ref.shape)

        @pl.when(pl.program_id(0) > 0)
        def _():
            l_ref[...] += jnp.broadcast_to(part, l_ref.shape)

    row = pl.BlockSpec((tm, d), lambda i: (i, 0))
    return pl.pallas_call(
        body, name="loss_head", grid=(s // tm,), in_specs=[row, row],
        out_specs=[pl.BlockSpec((1, LANES), lambda i: (0, 0)), row],
        out_shape=[jax.ShapeDtypeStruct((1, LANES), F32), jax.ShapeDtypeStruct((s, d), F32)],
        compiler_params=_params(("arbitrary",)),
    )(y, target)


def _adamw(w, g, m, v, name):
    shape = w.shape
    cols = shape[-1]
    rows = int(np.prod(shape[:-1]))
    tm = rows
    for cand in (512, 256, 128, 64, 32, 16, 8):
        if rows % cand == 0 and rows > cand and cand * cols * 4 <= ADAMW_BLOCK_BYTES:
            tm = cand
            break

    def body(w_ref, g_ref, m_ref, v_ref, d_ref, mo_ref, vo_ref):
        gr = g_ref[...]
        mn = ADAM_B1 * m_ref[...] + (1.0 - ADAM_B1) * gr
        vn = ADAM_B2 * v_ref[...] + (1.0 - ADAM_B2) * (gr * gr)
        m_hat = mn / (1.0 - ADAM_B1 ** ADAM_STEP)
        v_hat = vn / (1.0 - ADAM_B2 ** ADAM_STEP)
        d_ref[...] = -ADAM_LR * (m_hat / (jnp.sqrt(v_hat) + ADAM_EPS) + ADAM_WD * w_ref[...])
        mo_ref[...] = mn
        vo_ref[...] = vn

    blk = pl.BlockSpec((tm, cols), lambda i: (i, 0))
    out = jax.ShapeDtypeStruct((rows, cols), F32)
    r2 = lambda a: a.reshape(rows, cols)
    outs = pl.pallas_call(
        body, name=name, grid=(rows // tm,), in_specs=[blk] * 4, out_specs=[blk] * 3, out_shape=[out] * 3,
        compiler_params=_params(("parallel",)),
    )(r2(w), r2(g), r2(m), r2(v))
    return tuple(o.reshape(shape) for o in outs)


def _split_w_in(w_in):
    c0, c1, c2, c3 = 3 * WIDTH, 6 * WIDTH, 6 * WIDTH + N_HEADS, 9 * WIDTH + N_HEADS
    wa = jnp.concatenate([w_in[:, c3:], w_in[:, :c0], w_in[:, c0:c1], w_in[:, c2:c3]], axis=1)
    wf = jnp.pad(w_in[:, c1:c2], ((0, 0), (0, LANES - N_HEADS)))
    return wa, wf


def _join_w_in(ga, gf):
    g, c, f, s = 3 * D_MODEL, CONV_OFF, FOX_OFF, SB_OFF
    return jnp.concatenate([ga[:, c:f], ga[:, f:s], gf[:, :N_HEADS], ga[:, s:], ga[:, :g]], axis=1)


def _layer_fwd(x, p, plan, l):
    s = x.shape[0]
    proj, hn1 = _norm_mm(x, p["norm1_g"], p["wa"], BF16, name="in_proj")
    fraw, _ = _norm_mm(x, p["norm1_g"], p["wf"], F32, name="in_proj_f")
    ft = fraw[:, :N_HEADS].T
    crow8 = _fox_gates(ft, p["fox_f_bias"])
    crow = crow8.reshape(N_PAIR, 2, s)
    ccol = jnp.repeat(crow.transpose(0, 2, 1), HEAD_DIM, axis=2)
    oc = _conv_mix_fwd(proj, p["conv_w"])
    qn, kn = _fox_prep(proj, p["fox_q_norm_g"], p["fox_k_norm_g"])
    last = l + 1 == plan.depth
    (of, lse), got_in = _fox_fwd(qn, kn, proj, ccol, crow, comm=None if last else plan.gather(l + 1, BIG_IN))
    osb, got_rest = _sb_fwd(proj, comm=plan.gather(l, BIG_REST))
    p.update(plan.weights(l, BIG_REST, got_rest))
    nxt = None if last else plan.layer(l + 1, got_in)
    xm, merged, yc, yf, ys = _merge_fwd(x, oc, of, osb, p["w_proj_conv"], p["w_proj_fox"], p["w_proj_sb"], proj,
                                        p["gate_bias"], p["w_out"])
    up, hn2 = _norm_mm(xm, p["norm2_g"], p["w_up"], BF16, name="up_proj")
    hh = _glu_fwd(up, p["ffn_conv_w"], p["ffn_conv_b"])
    xo = _mm(hh, p["w_down"], "nn", F32, res=xm, name="down_proj")
    saved = dict(x=x, hn1=hn1, proj=proj, ft=ft, crow=crow, ccol=ccol, oc=oc, qn=qn, kn=kn, of=of, lse=lse, osb=osb,
                 merged=merged, yc=yc, yf=yf, ys=ys, xm=xm, hn2=hn2, up=up, hh=hh)
    return xo, saved, nxt


def _layer_bwd(dx, p, a, plan, pending_in):
    s = dx.shape[0]
    g = {}
    dhh = _mm(dx, p["w_down"], "nt", BF16, tn=1408, name="d_down_in")
    g["w_down"] = _mm(a["hh"], dx, "tn", F32, name="g_w_down")
    dug, duv, g["ffn_conv_w"], g["ffn_conv_b"] = _glu_bwd(dhh, a["up"], p["ffn_conv_w"], p["ffn_conv_b"])
    dup = jnp.concatenate([dug, duv], axis=1)
    g["w_up"] = _mm(a["hn2"], dup, "tn", F32, tn=1408, name="g_w_up")
    dhn2 = _mm(dup, p["w_up"], "nt", F32, name="d_up_in")
    dx, g["norm2_g"] = _norm_bwd(dhn2, a["xm"], p["norm2_g"], dx, name="norm2_bwd")
    dm = _mm(dx, p["w_out"], "nt", BF16, name="d_out_in")
    g["w_out"] = _mm(a["merged"], dx, "tn", F32, tn=512, name="g_w_out")
    dyc, dyf, dys, dgl, g["gate_bias"] = _gate_bwd(dm, a["yc"], a["yf"], a["ys"], a["proj"], p["gate_bias"])
    doc = _mm(dyc, p["w_proj_conv"], "nt", BF16, name="d_pc_in")
    dof = _mm(dyf, p["w_proj_fox"], "nt", BF16, name="d_pf_in")
    dos = _mm(dys, p["w_proj_sb"], "nt", BF16, name="d_ps_in")
    g["w_proj_conv"] = _mm(a["oc"], dyc, "tn", F32, name="g_w_pc")
    g["w_proj_fox"] = _mm(a["of"], dyf, "tn", F32, name="g_w_pf")
    g["w_proj_sb"] = _mm(a["osb"], dys, "tn", F32, name="g_w_ps")
    dcb, dcc, dch, g["conv_w"] = _conv_mix_bwd(doc, a["proj"], p["conv_w"])
    delta_f = _delta_rep(dof, a["of"])
    pending = plan.reduction(pending_in, {k: g[k] for k in BIG_REST})
    (dqs, dkn, dfv, dcrow, dcq), recv_a = _fox_bwd(a["qn"], a["kn"], a["proj"], dof, a["lse"], delta_f, a["ccol"], a["crow"],
                                                   comm=pending.exchange() if pending else None)
    dc = dcrow.reshape(N_HEADS, s) + dcq[:, :, ::HEAD_DIM].transpose(0, 2, 1).reshape(N_HEADS, s)
    dfq, dfk, g["fox_q_norm_g"], g["fox_k_norm_g"] = _fox_post(dqs, dkn, a["proj"], p["fox_q_norm_g"], p["fox_k_norm_g"])
    dft, g["fox_f_bias"] = _fox_gates_bwd(dc, a["ft"], p["fox_f_bias"])
    delta_s = _delta_rep(dos, a["osb"])
    (dsq, dsk, dsv), recv_b = _sb_bwd(a["proj"], dos, delta_s, comm=pending.sums(recv_a) if pending else None)
    done_in = None
    if pending:
        done = pending.finish(recv_b)
        if pending_in is not None:
            done_in, done = done[0], done[1:]
        g.update(zip(BIG_REST, done))
    dproj = jnp.concatenate([dgl, dcb, dcc, dch, dfq, dfk, dfv, dsq, dsk, dsv], axis=1)
    dfp = jnp.pad(dft.T, ((0, 0), (0, LANES - N_HEADS))).astype(BF16)
    ga = _mm(a["hn1"], dproj, "tn", F32, tn=768, name="g_w_in")
    gf = _mm(a["hn1"], dfp, "tn", F32, name="g_w_in_f")
    g["w_in"] = _join_w_in(ga, gf)
    dhn1 = _mm(dfp, p["wf"], "nt", F32, name="d_in_f")
    dhn1 = _mm(dproj, p["wa"], "nt", F32, res=dhn1, tk=1536, name="d_in")
    dx, g["norm1_g"] = _norm_bwd(dhn1, a["x"], p["norm1_g"], dx, name="norm1_bwd")
    return dx, g, done_in


MATMUL_WEIGHTS = ("w_in", "w_proj_conv", "w_proj_fox", "w_proj_sb", "w_out", "w_up", "w_down")
WEIGHTS = ("norm1_g", "w_in", "fox_f_bias", "gate_bias", "conv_w", "fox_q_norm_g", "fox_k_norm_g", "w_proj_conv",
           "w_proj_fox", "w_proj_sb", "w_out", "norm2_g", "w_up", "ffn_conv_w", "ffn_conv_b", "w_down")


def _local_step(x, target, plan):
    depth = plan.depth
    layers, saved = [plan.layer(0, None)], []
    for l in range(depth):
        x, a, nxt = _layer_fwd(x, layers[l], plan, l)
        saved.append(a)
        if nxt is not None:
            layers.append(nxt)
    sq, dx = _loss_head(x, target)
    grads, pending_in = [None] * depth, None
    for l in reversed(range(depth)):
        dx, grads[l], done_in = _layer_bwd(dx, layers[l], saved[l], plan, pending_in)
        if done_in is not None:
            grads[l + 1]["w_in"] = done_in
        pending_in = grads[l]["w_in"]
    last = plan.reduction(pending_in, {})
    if last is not None:
        (grads[0]["w_in"],) = last.run()
    return sq, dx, grads


ANY = pl.BlockSpec(memory_space=pl.ANY)


def _place():
    x, y, c = lax.axis_index("x"), lax.axis_index("y"), lax.axis_index("c")
    chips = [(1 - x, y), (x, 1 - y), (1 - x, 1 - y)]
    return x, y, c, chips


_Comm = collections.namedtuple("_Comm", "inputs out_shapes sems start finish")


def _run_comm(comm, name):
    n_in, n_out = len(comm.inputs), len(comm.out_shapes)

    def body(*refs):
        ins, outs, sems = refs[:n_in], refs[n_in:n_in + n_out], refs[n_in + n_out:]
        comm.start(ins, outs, sems)
        comm.finish(ins, outs, sems)

    return pl.pallas_call(body, name=name, in_specs=[ANY] * n_in, out_specs=[ANY] * n_out, out_shape=comm.out_shapes,
                          scratch_shapes=comm.sems)(*comm.inputs)


def _gather_comm(shards):
    n = len(shards)

    def copy(x_refs, out_refs, sems, k, t, chip_index, which_half, to, from_input=False):
        half = shards[t].shape[0] // 2
        rows = pl.ds(which_half * half, half)
        dst = out_refs[t].at[chip_index, rows, :]
        return pltpu.make_async_remote_copy(
            src_ref=x_refs[t].at[rows, :] if from_input else dst, dst_ref=dst,
            send_sem=sems[0].at[k, t], recv_sem=sems[1].at[k, t], device_id=to, device_id_type=MESH)

    def first(x_refs, out_refs, sems):
        x, y, c, chips = _place()
        return [copy(x_refs, out_refs, sems, p, t, 2 * x + y, c, (*chip, c), from_input=True)
                for t in range(n) for p, chip in enumerate(chips)]

    def start(x_refs, out_refs, sems):
        for cp in first(x_refs, out_refs, sems):
            cp.start()

    def finish(x_refs, out_refs, sems):
        x, y, c, chips = _place()
        passed = []
        for t in range(n):
            for p, chip in enumerate(chips):
                copy(x_refs, out_refs, sems, p, t, 2 * chip[0] + chip[1], c, (x, y, c)).wait_recv()
                fwd = copy(x_refs, out_refs, sems, 3 + p, t, 2 * chip[0] + chip[1], c, (x, y, 1 - c))
                fwd.start()
                passed.append(fwd)
        for t in range(n):
            for p, chip in enumerate(chips):
                copy(x_refs, out_refs, sems, 3 + p, t, 2 * chip[0] + chip[1], 1 - c, (x, y, c)).wait_recv()
        for cp in first(x_refs, out_refs, sems) + passed:
            cp.wait_send()

    return _Comm(list(shards), [jax.ShapeDtypeStruct((4, *s.shape), s.dtype) for s in shards],
                 [pltpu.SemaphoreType.DMA((6, n)), pltpu.SemaphoreType.DMA((6, n))], start, finish)


def _own_block(got, shards):
    chip = 2 * lax.axis_index("x") + lax.axis_index("y")
    return [lax.dynamic_update_index_in_dim(g, s, chip, 0) for g, s in zip(got, shards)]


def _pair_exchange_comm(gs):
    n = len(gs)

    def copies(g_refs, recv_refs, sems):
        x, y, c, _ = _place()
        return [pltpu.make_async_remote_copy(
            src_ref=g_refs[t].at[:, pl.ds((1 - c) * (gs[t].shape[1] // 2), gs[t].shape[1] // 2), :], dst_ref=recv_refs[t],
            send_sem=sems[0].at[t], recv_sem=sems[1].at[t], device_id=(x, y, 1 - c), device_id_type=MESH) for t in range(n)]

    def start(g_refs, recv_refs, sems):
        for cp in copies(g_refs, recv_refs, sems):
            cp.start()

    def finish(g_refs, recv_refs, sems):
        for cp in copies(g_refs, recv_refs, sems):
            cp.wait()

    return _Comm(list(gs), [jax.ShapeDtypeStruct((4, g.shape[1] // 2, g.shape[2]), g.dtype) for g in gs],
                 [pltpu.SemaphoreType.DMA((n,)), pltpu.SemaphoreType.DMA((n,))], start, finish)


def _pair_sum(g, recv, core, tr=256):
    n, r, cols = g.shape
    half = r // 2
    tr = _row_tile(half, tr)
    nb = half // tr

    def body(c_ref, g_ref, r_ref, o_ref):
        o_ref[...] = (g_ref[...] + r_ref[...]).astype(BF16)

    grid_spec = pltpu.PrefetchScalarGridSpec(
        num_scalar_prefetch=1, grid=(n, nb),
        in_specs=[pl.BlockSpec((None, tr, cols), lambda k, i, c: (k, c[0] * nb + i, 0)),
                  pl.BlockSpec((None, tr, cols), lambda k, i, c: (k, i, 0))],
        out_specs=pl.BlockSpec((None, tr, cols), lambda k, i, c: (k, i, 0)))
    return pl.pallas_call(
        body, name="rs_pair_sum", grid_spec=grid_spec, out_shape=jax.ShapeDtypeStruct((n, half, cols), BF16),
        compiler_params=_params(("parallel", "parallel")),
    )(core, g, recv)


def _row_tile(rows, pref):
    best = None
    for t in range(16, min(rows, pref) + 1, 16):
        if rows % t == 0:
            best = t
    assert best is not None, (rows, pref)
    return best


def _chip_exchange_comm(s1s):
    n = len(s1s)

    def copies(s_refs, recv_refs, sems):
        x, y, c, chips = _place()
        return [pltpu.make_async_remote_copy(
            src_ref=s_refs[t].at[2 * chip[0] + chip[1]], dst_ref=recv_refs[t].at[p],
            send_sem=sems[0].at[p, t], recv_sem=sems[1].at[p, t], device_id=(*chip, c), device_id_type=MESH)
            for t in range(n) for p, chip in enumerate(chips)]

    def start(s_refs, recv_refs, sems):
        for cp in copies(s_refs, recv_refs, sems):
            cp.start()

    def finish(s_refs, recv_refs, sems):
        for cp in copies(s_refs, recv_refs, sems):
            cp.wait()

    return _Comm(list(s1s), [jax.ShapeDtypeStruct((3, *s.shape[1:]), s.dtype) for s in s1s],
                 [pltpu.SemaphoreType.DMA((3, n)), pltpu.SemaphoreType.DMA((3, n))], start, finish)


def _final_sum(g, recv_a, recv_b, core, chip, tr=256):
    n, r, cols = g.shape
    half = r // 2
    tr = _row_tile(half, tr)
    nb = half // tr

    def body(c_ref, k_ref, g_ref, a_ref, b0_ref, b1_ref, b2_ref, o_ref):
        total = g_ref[...] + a_ref[...]
        for b_ref in (b0_ref, b1_ref, b2_ref):
            total = total + b_ref[...].astype(F32)
        o_ref[...] = total

    rel = lambda p: pl.BlockSpec((None, tr, cols), lambda i, c, k, p=p: (p, i, 0))
    grid_spec = pltpu.PrefetchScalarGridSpec(
        num_scalar_prefetch=2, grid=(nb,),
        in_specs=[pl.BlockSpec((None, tr, cols), lambda i, c, k: (k[0], c[0] * nb + i, 0)),
                  pl.BlockSpec((None, tr, cols), lambda i, c, k: (k[0], i, 0)), rel(0), rel(1), rel(2)],
        out_specs=pl.BlockSpec((tr, cols), lambda i, c, k: (c[0] * nb + i, 0)))
    return pl.pallas_call(
        body, name="rs_final_sum", grid_spec=grid_spec, out_shape=jax.ShapeDtypeStruct((r, cols), F32),
        compiler_params=_params(("parallel",)),
    )(core, chip, g, recv_a, recv_b, recv_b, recv_b)


def _pair_join(fs):
    n = len(fs)

    def body(*refs):
        out_refs = refs[n:2 * n]
        send_sems, recv_sems = refs[2 * n:]
        x, y, c, _ = _place()

        def copy(t, which_half):
            h = fs[t].shape[0] // 2
            rows = out_refs[t].at[pl.ds(which_half * h, h), :]
            return pltpu.make_async_remote_copy(
                src_ref=rows, dst_ref=rows, send_sem=send_sems.at[t], recv_sem=recv_sems.at[t],
                device_id=(x, y, 1 - c), device_id_type=MESH)

        sends = [copy(t, c) for t in range(n)]
        for cp in sends:
            cp.start()
        for t in range(n):
            sends[t].wait_send()
            copy(t, 1 - c).wait_recv()

    return pl.pallas_call(
        body, name="rs_pair_join", in_specs=[ANY] * n, out_specs=[ANY] * n,
        out_shape=[jax.ShapeDtypeStruct(f.shape, f.dtype) for f in fs],
        input_output_aliases={t: t for t in range(n)},
        scratch_shapes=[pltpu.SemaphoreType.DMA((n,)), pltpu.SemaphoreType.DMA((n,))],
    )(*fs)


class _Reduction:
    def __init__(self, gs):
        self.gs = gs
        self.core = lax.axis_index("c").astype(jnp.int32).reshape(1)
        self.chip = (2 * lax.axis_index("x") + lax.axis_index("y")).astype(jnp.int32).reshape(1)

    def exchange(self):
        return _pair_exchange_comm(self.gs)

    def sums(self, recv_a):
        self.recv_a = list(recv_a)
        return _chip_exchange_comm([_pair_sum(g, ra, self.core) for g, ra in zip(self.gs, self.recv_a)])

    def finish(self, recv_b):
        return _pair_join([_final_sum(g, ra, rb, self.core, self.chip) for g, ra, rb in zip(self.gs, self.recv_a, recv_b)])

    def run(self):
        recv_a = _run_comm(self.exchange(), "rs_pair_exchange")
        return self.finish(_run_comm(self.sums(recv_a), "rs_chip_exchange"))


def _all_reduce_small(v):
    r, cols = v.shape

    def body(v_ref, out_ref, buf_ref, send_sems, recv_sems):
        x, y, c, _ = _place()
        flip = lambda a, bit: 1 - a if bit else a
        buf_ref[4 * x + 2 * y + c] = v_ref[...]
        cps = []
        for rel in range(1, 8):
            peer = (flip(x, rel & 4), flip(y, rel & 2), flip(c, rel & 1))
            cps.append(pltpu.make_async_remote_copy(
                src_ref=v_ref, dst_ref=buf_ref.at[4 * x + 2 * y + c], send_sem=send_sems.at[rel - 1], recv_sem=recv_sems.at[rel - 1],
                device_id=peer, device_id_type=MESH))
        for cp in cps:
            cp.start()
        for cp in cps:
            cp.wait()
        total = buf_ref[0]
        for d in range(1, 8):
            total = total + buf_ref[d]
        out_ref[...] = total

    vm = pl.BlockSpec(memory_space=pltpu.VMEM)
    return pl.pallas_call(
        body, name="all_reduce_small", in_specs=[vm], out_specs=vm, out_shape=jax.ShapeDtypeStruct((r, cols), F32),
        scratch_shapes=[pltpu.VMEM((8, r, cols), F32), pltpu.SemaphoreType.DMA((7,)), pltpu.SemaphoreType.DMA((7,))],
    )(v)


SHARD_AXIS = {"w_in": 1, "conv_w": 1, "w_proj_conv": 1, "w_proj_fox": 1, "w_proj_sb": 1, "w_out": 0, "w_up": 1,
              "ffn_conv_w": 1, "w_down": 0}
SMALL_SHARDED = ("conv_w", "ffn_conv_w")
BIG = tuple(k for k in SHARD_AXIS if k not in SMALL_SHARDED)
BIG_IN = ("w_in",)
BIG_REST = tuple(k for k in BIG if k not in BIG_IN)
REPLICATED = tuple(k for k in WEIGHTS if k not in SHARD_AXIS)
SMALL = REPLICATED + SMALL_SHARDED


def _pack_small(parts, row_align):
    flat = jnp.concatenate([p.reshape(-1) for p in parts])
    rows = -(-flat.shape[0] // (PACK_COLS * row_align)) * row_align
    return jnp.pad(flat, (0, rows * PACK_COLS - flat.shape[0])).reshape(rows, PACK_COLS)


def _unpack_small(packed, shapes):
    flat = packed.reshape(-1)
    out, off = [], 0
    for shape in shapes:
        size = int(np.prod(shape))
        out.append(flat[off:off + size].reshape(shape))
        off += size
    return out


def _to_full(stacked, axis):
    _, r, c = stacked.shape
    if axis == 0:
        return stacked.reshape(4 * r, c)
    return jnp.moveaxis(stacked, 0, 1).reshape(r, 4 * c)


def _to_chips(full, axis):
    a, b = full.shape
    if axis == 0:
        return full.reshape(4, a // 4, b)
    return jnp.moveaxis(full.reshape(a, 4, b // 4), 1, 0)


class _Plan:
    def __init__(self, given):
        self.given = given
        self.depth = given["norm1_g"].shape[0]
        conv_shapes = [given[k].shape for k in SMALL_SHARDED]
        packed = [_pack_small([given[k] for k in SMALL_SHARDED], 16)]
        (got,) = _own_block(_run_comm(_gather_comm(packed), "gather_conv_weights"), packed)
        by_chip = [_unpack_small(got[j], conv_shapes) for j in range(4)]
        self.conv = {k: jnp.concatenate([by_chip[j][i] for j in range(4)], axis=-1) for i, k in enumerate(SMALL_SHARDED)}
        self.shards = {}

    def _shards(self, l, names):
        if (l, names) not in self.shards:
            self.shards[l, names] = [self.given[k][l].astype(BF16) for k in names]
        return self.shards[l, names]

    def gather(self, l, names):
        return _gather_comm(self._shards(l, names))

    def weights(self, l, names, got):
        return {k: _to_full(stacked, SHARD_AXIS[k]) for k, stacked in zip(names, _own_block(got, self._shards(l, names)))}

    def layer(self, l, got_in):
        if got_in is None:
            got_in = _run_comm(self.gather(l, BIG_IN), "gather_w_in")
        p = {k: self.given[k][l] for k in REPLICATED}
        p.update({k: self.conv[k][l] for k in SMALL_SHARDED})
        p["wa"], p["wf"] = _split_w_in(self.weights(l, BIG_IN, got_in)["w_in"])
        return p

    def reduction(self, w_in_grad, rest):
        gs = [] if w_in_grad is None else [_to_chips(w_in_grad, SHARD_AXIS["w_in"])]
        return _Reduction(gs + [_to_chips(rest[k], SHARD_AXIS[k]) for k in BIG_REST if k in rest])


def kernel(x, norm1_g, w_in, fox_f_bias, gate_bias, conv_w, fox_q_norm_g, fox_k_norm_g, w_proj_conv, w_proj_fox, w_proj_sb, w_out, norm2_g, w_up, ffn_conv_w, ffn_conv_b, w_down, loss_target, m_norm1_g, m_w_in, m_fox_f_bias, m_gate_bias, m_conv_w, m_fox_q_norm_g, m_fox_k_norm_g, m_w_proj_conv, m_w_proj_fox, m_w_proj_sb, m_w_out, m_norm2_g, m_w_up, m_ffn_conv_w, m_ffn_conv_b, m_w_down, v_norm1_g, v_w_in, v_fox_f_bias, v_gate_bias, v_conv_w, v_fox_q_norm_g, v_fox_k_norm_g, v_w_proj_conv, v_w_proj_fox, v_w_proj_sb, v_w_out, v_norm2_g, v_w_up, v_ffn_conv_w, v_ffn_conv_b, v_w_down):
    given = dict(x=x, norm1_g=norm1_g, w_in=w_in, fox_f_bias=fox_f_bias, gate_bias=gate_bias, conv_w=conv_w, fox_q_norm_g=fox_q_norm_g, fox_k_norm_g=fox_k_norm_g, w_proj_conv=w_proj_conv, w_proj_fox=w_proj_fox, w_proj_sb=w_proj_sb, w_out=w_out, norm2_g=norm2_g, w_up=w_up, ffn_conv_w=ffn_conv_w, ffn_conv_b=ffn_conv_b, w_down=w_down, loss_target=loss_target, m_norm1_g=m_norm1_g, m_w_in=m_w_in, m_fox_f_bias=m_fox_f_bias, m_gate_bias=m_gate_bias, m_conv_w=m_conv_w, m_fox_q_norm_g=m_fox_q_norm_g, m_fox_k_norm_g=m_fox_k_norm_g, m_w_proj_conv=m_w_proj_conv, m_w_proj_fox=m_w_proj_fox, m_w_proj_sb=m_w_proj_sb, m_w_out=m_w_out, m_norm2_g=m_norm2_g, m_w_up=m_w_up, m_ffn_conv_w=m_ffn_conv_w, m_ffn_conv_b=m_ffn_conv_b, m_w_down=m_w_down, v_norm1_g=v_norm1_g, v_w_in=v_w_in, v_fox_f_bias=v_fox_f_bias, v_gate_bias=v_gate_bias, v_conv_w=v_conv_w, v_fox_q_norm_g=v_fox_q_norm_g, v_fox_k_norm_g=v_fox_k_norm_g, v_w_proj_conv=v_w_proj_conv, v_w_proj_fox=v_w_proj_fox, v_w_proj_sb=v_w_proj_sb, v_w_out=v_w_out, v_norm2_g=v_norm2_g, v_w_up=v_w_up, v_ffn_conv_w=v_ffn_conv_w, v_ffn_conv_b=v_ffn_conv_b, v_w_down=v_w_down)
    depth = given["norm1_g"].shape[0]
    chip = 2 * lax.axis_index("x") + lax.axis_index("y")

    sq, dx, grads = _local_step(given["x"][0], given["loss_target"][0], _Plan(given))
    loss = lax.psum(0.5 * sq[0, 0] / D_MODEL, ("x", "y", "c"))

    gsum = {k: jnp.stack([g[k] for g in grads]) for k in BIG}
    small_shapes = [(depth, *grads[0][k].shape) for k in SMALL]
    summed = _all_reduce_small(_pack_small([jnp.stack([g[k] for g in grads]) for k in SMALL], 8))
    for k, total in zip(SMALL, _unpack_small(summed, small_shapes)):
        if k in SMALL_SHARDED:
            total = lax.dynamic_index_in_dim(total.reshape(*total.shape[:-1], 4, total.shape[-1] // 4), chip, axis=2, keepdims=False)
        gsum[k] = total

    deltas, new_m, new_v = {}, {}, {}
    for k in WEIGHTS:
        deltas[k], new_m[k], new_v[k] = _adamw(given[k], gsum[k], given["m_" + k], given["v_" + k], "adamw_" + k)
    return (loss, dx[None], *[gsum[k] for k in WEIGHTS], *[deltas[k] for k in WEIGHTS],
            *[new_m[k] for k in WEIGHTS], *[new_v[k] for k in WEIGHTS])
```

```python
import collections

import numpy as np
import jax
import jax.numpy as jnp
from jax import lax
from jax.experimental import pallas as pl
from jax.experimental.pallas import tpu as pltpu

F32 = jnp.float32
BF16 = jnp.bfloat16

D_MODEL = 1024
DEPTH = 4
HEAD_DIM = 64
N_HEADS = 8
WIDTH = 512
D_FF = 2816
NORM_EPS = 1e-6
NEG_INF = -1e30
QK_SCALE = HEAD_DIM ** -0.5
LANES = 128
N_PAIR = N_HEADS // 2

GATE_OFF = 0
CONV_OFF = 3 * D_MODEL
FOX_OFF = CONV_OFF + 3 * WIDTH
SB_OFF = FOX_OFF + 3 * WIDTH
D_INA = SB_OFF + 3 * WIDTH

ADAM_LR = 0.001
ADAM_B1 = 0.9
ADAM_B2 = 0.999
ADAM_EPS = 1e-08
ADAM_WD = 0.01
ADAM_STEP = 10

VMEM_LIMIT = 48 * 1024 * 1024
ADAMW_BLOCK_BYTES = 1024 * 1024

PACK_COLS = 1024
PACK_ROW_ALIGN = 32
MESH = pl.DeviceIdType.MESH


def _params(sem):
    return pltpu.CompilerParams(dimension_semantics=sem, vmem_limit_bytes=VMEM_LIMIT)


def _dot(a, b, dims):
    return lax.dot_general(a, b, (dims, ((), ())), preferred_element_type=F32)


_NN = ((1,), (0,))
_NT = ((1,), (1,))
_TN = ((0,), (0,))


def _pick(dim, pref):
    if dim <= pref:
        return dim
    best = None
    for mult in range(1, dim // LANES + 1):
        t = mult * LANES
        if t <= pref and dim % t == 0:
            best = t
    assert best is not None, (dim, pref)
    return best


def _mm(a, b, mode, out_dtype=F32, tm=1024, tn=1024, tk=2048, res=None, name="mm"):
    if mode == "nn":
        (m, k), (_, n) = a.shape, b.shape
    elif mode == "nt":
        (m, k), (n, _) = a.shape, b.shape
    else:
        (k, m), (_, n) = a.shape, b.shape
    tm, tn, tk = _pick(m, tm), _pick(n, tn), _pick(k, tk)
    nk = k // tk
    dims = {"nn": _NN, "nt": _NT, "tn": _TN}[mode]
    if mode == "tn":
        a_spec = pl.BlockSpec((tk, tm), lambda i, j, kk: (kk, i))
    else:
        a_spec = pl.BlockSpec((tm, tk), lambda i, j, kk: (i, kk))
    if mode == "nt":
        b_spec = pl.BlockSpec((tn, tk), lambda i, j, kk: (j, kk))
    else:
        b_spec = pl.BlockSpec((tk, tn), lambda i, j, kk: (kk, j))
    o_spec = pl.BlockSpec((tm, tn), lambda i, j, kk: (i, j))
    in_specs = [a_spec, b_spec] + ([o_spec] if res is not None else [])

    def body(*refs):
        if res is not None:
            a_ref, b_ref, r_ref, o_ref, acc_ref = refs
        else:
            a_ref, b_ref, o_ref, acc_ref = refs
            r_ref = None
        kk = pl.program_id(2)
        part = _dot(a_ref[...].astype(BF16), b_ref[...].astype(BF16), dims)

        def finish(total):
            if r_ref is not None:
                total = total + r_ref[...].astype(F32)
            o_ref[...] = total.astype(out_dtype)

        if nk == 1:
            finish(part)
        else:
            @pl.when(kk == 0)
            def _():
                acc_ref[...] = part

            @pl.when(kk > 0)
            def _():
                acc_ref[...] += part

            @pl.when(kk == nk - 1)
            def _():
                finish(acc_ref[...])

    args = (a, b) + ((res,) if res is not None else ())
    return pl.pallas_call(
        body, name=name, grid=(m // tm, n // tn, nk), in_specs=in_specs, out_specs=o_spec,
        out_shape=jax.ShapeDtypeStruct((m, n), out_dtype),
        scratch_shapes=[pltpu.VMEM((tm, tn) if nk > 1 else (8, LANES), F32)],
        compiler_params=_params(("parallel", "parallel", "arbitrary")),
    )(*args)


def _norm_mm(x, g, w, out_dtype, tm=1024, tn=1536, name="norm_mm"):
    m, d = x.shape
    n = w.shape[1]
    tm, tn = _pick(m, tm), _pick(n, tn)

    def body(x_ref, g_ref, w_ref, o_ref, hn_ref):
        @pl.when(pl.program_id(1) == 0)
        def _():
            xf = x_ref[...]
            r = lax.rsqrt(jnp.mean(xf * xf, axis=-1, keepdims=True) + NORM_EPS)
            hn_ref[...] = (xf * r * g_ref[...]).astype(BF16)

        o_ref[...] = _dot(hn_ref[...], w_ref[...], _NN).astype(out_dtype)

    return pl.pallas_call(
        body, name=name, grid=(m // tm, n // tn),
        in_specs=[pl.BlockSpec((tm, d), lambda i, j: (i, 0)),
                  pl.BlockSpec((1, d), lambda i, j: (0, 0)),
                  pl.BlockSpec((d, tn), lambda i, j: (0, j))],
        out_specs=[pl.BlockSpec((tm, tn), lambda i, j: (i, j)),
                   pl.BlockSpec((tm, d), lambda i, j: (i, 0))],
        out_shape=[jax.ShapeDtypeStruct((m, n), out_dtype), jax.ShapeDtypeStruct((m, d), BF16)],
        compiler_params=_params(("parallel", "arbitrary")),
    )(x, g.reshape(1, d), w)


def _norm_bwd(dhn, x, g, dx_in, tm=256, name="norm_bwd"):
    m, d = x.shape
    tm = min(tm, m)

    def body(dhn_ref, x_ref, g_ref, dxi_ref, dx_ref, gg_ref):
        xf = x_ref[...]
        r = lax.rsqrt(jnp.mean(xf * xf, axis=-1, keepdims=True) + NORM_EPS)
        xhat = xf * r
        dh = dhn_ref[...].astype(F32)
        dxn = dh * g_ref[...]
        mean = jnp.mean(dxn * xhat, axis=-1, keepdims=True)
        dx_ref[...] = dxi_ref[...] + r * (dxn - xhat * mean)
        part = jnp.sum(dh * xhat, axis=0, keepdims=True)

        @pl.when(pl.program_id(0) == 0)
        def _():
            gg_ref[...] = part

        @pl.when(pl.program_id(0) > 0)
        def _():
            gg_ref[...] += part

    row = pl.BlockSpec((tm, d), lambda i: (i, 0))
    vec = pl.BlockSpec((1, d), lambda i: (0, 0))
    dx, gg = pl.pallas_call(
        body, name=name, grid=(m // tm,), in_specs=[row, row, vec, row], out_specs=[row, vec],
        out_shape=[jax.ShapeDtypeStruct((m, d), F32), jax.ShapeDtypeStruct((1, d), F32)],
        compiler_params=_params(("arbitrary",)),
    )(dhn, x, g.reshape(1, d), dx_in)
    return dx, gg.reshape(d)


def _down(u, k):
    s = u.shape[0]
    rows = lax.broadcasted_iota(jnp.int32, u.shape, 0)
    return jnp.where(rows < k, 0.0, pltpu.roll(u, k, axis=0))


def _up(u, k):
    s = u.shape[0]
    rows = lax.broadcasted_iota(jnp.int32, u.shape, 0)
    return jnp.where(rows >= s - k, 0.0, pltpu.roll(u, s - k, axis=0))


def _conv_mix_fwd(proj, conv_w, tc=128):
    s = proj.shape[0]
    nb = WIDTH // tc
    off = CONV_OFF // tc

    def body(b_ref, c_ref, h_ref, w_ref, o_ref):
        u = c_ref[...].astype(F32) * h_ref[...].astype(F32)
        w = w_ref[...]
        cv = w[0:1] * _down(u, 2) + w[1:2] * _down(u, 1) + w[2:3] * u
        o_ref[...] = (b_ref[...].astype(F32) * cv).astype(BF16)

    col = lambda k: pl.BlockSpec((s, tc), lambda j, k=k: (0, off + k * nb + j))
    return pl.pallas_call(
        body, name="conv_mix_fwd", grid=(nb,),
        in_specs=[col(0), col(1), col(2), pl.BlockSpec((3, tc), lambda j: (0, j))],
        out_specs=pl.BlockSpec((s, tc), lambda j: (0, j)),
        out_shape=jax.ShapeDtypeStruct((s, WIDTH), BF16),
        compiler_params=_params(("parallel",)),
    )(proj, proj, proj, conv_w)


def _conv_mix_bwd(do, proj, conv_w, tc=128):
    s = proj.shape[0]
    nb = WIDTH // tc
    off = CONV_OFF // tc

    def body(do_ref, b_ref, c_ref, h_ref, w_ref, db_ref, dc_ref, dh_ref, gw_ref):
        b = b_ref[...].astype(F32)
        c = c_ref[...].astype(F32)
        h = h_ref[...].astype(F32)
        g = do_ref[...].astype(F32)
        w = w_ref[...]
        u = c * h
        u1, u2 = _down(u, 1), _down(u, 2)
        cv = w[0:1] * u2 + w[1:2] * u1 + w[2:3] * u
        db_ref[...] = (g * cv).astype(BF16)
        dcv = g * b
        gw_ref[0:1, :] = jnp.sum(dcv * u2, axis=0, keepdims=True)
        gw_ref[1:2, :] = jnp.sum(dcv * u1, axis=0, keepdims=True)
        gw_ref[2:3, :] = jnp.sum(dcv * u, axis=0, keepdims=True)
        du = w[2:3] * dcv + w[1:2] * _up(dcv, 1) + w[0:1] * _up(dcv, 2)
        dc_ref[...] = (du * h).astype(BF16)
        dh_ref[...] = (du * c).astype(BF16)

    col = lambda k: pl.BlockSpec((s, tc), lambda j, k=k: (0, off + k * nb + j))
    own = pl.BlockSpec((s, tc), lambda j: (0, j))
    wsp = pl.BlockSpec((3, tc), lambda j: (0, j))
    act = jax.ShapeDtypeStruct((s, WIDTH), BF16)
    return pl.pallas_call(
        body, name="conv_mix_bwd", grid=(nb,),
        in_specs=[own, col(0), col(1), col(2), wsp], out_specs=[own, own, own, wsp],
        out_shape=[act, act, act, jax.ShapeDtypeStruct((3, WIDTH), F32)],
        compiler_params=_params(("parallel",)),
    )(do, proj, proj, proj, conv_w)


def _glu_fwd(up, w, b, tc=256):
    s = up.shape[0]
    nb = D_FF // tc

    def body(g_ref, v_ref, w_ref, b_ref, o_ref):
        ug = g_ref[...].astype(F32)
        wv = w_ref[...]
        cg = wv[0:1] * _down(ug, 2) + wv[1:2] * _down(ug, 1) + wv[2:3] * ug + b_ref[...]
        act = cg * jax.nn.sigmoid(cg)
        o_ref[...] = (act * v_ref[...].astype(F32)).astype(BF16)

    return pl.pallas_call(
        body, name="glu_fwd", grid=(nb,),
        in_specs=[pl.BlockSpec((s, tc), lambda j: (0, j)), pl.BlockSpec((s, tc), lambda j: (0, nb + j)),
                  pl.BlockSpec((3, tc), lambda j: (0, j)), pl.BlockSpec((1, tc), lambda j: (0, j))],
        out_specs=pl.BlockSpec((s, tc), lambda j: (0, j)),
        out_shape=jax.ShapeDtypeStruct((s, D_FF), BF16),
        compiler_params=_params(("parallel",)),
    )(up, up, w, b.reshape(1, D_FF))


def _glu_bwd(dh, up, w, b, tc=256):
    s = up.shape[0]
    nb = D_FF // tc

    def body(dh_ref, g_ref, v_ref, w_ref, b_ref, dg_ref, dv_ref, gw_ref, gb_ref):
        ug = g_ref[...].astype(F32)
        uv = v_ref[...].astype(F32)
        d = dh_ref[...].astype(F32)
        wv = w_ref[...]
        u1, u2 = _down(ug, 1), _down(ug, 2)
        cg = wv[0:1] * u2 + wv[1:2] * u1 + wv[2:3] * ug + b_ref[...]
        sg = jax.nn.sigmoid(cg)
        dv_ref[...] = (d * (cg * sg)).astype(BF16)
        dcg = d * uv * (sg * (1.0 + cg * (1.0 - sg)))
        gb_ref[...] = jnp.sum(dcg, axis=0, keepdims=True)
        gw_ref[0:1, :] = jnp.sum(dcg * u2, axis=0, keepdims=True)
        gw_ref[1:2, :] = jnp.sum(dcg * u1, axis=0, keepdims=True)
        gw_ref[2:3, :] = jnp.sum(dcg * ug, axis=0, keepdims=True)
        dg_ref[...] = (wv[2:3] * dcg + wv[1:2] * _up(dcg, 1) + wv[0:1] * _up(dcg, 2)).astype(BF16)

    own = pl.BlockSpec((s, tc), lambda j: (0, j))
    wsp = pl.BlockSpec((3, tc), lambda j: (0, j))
    bsp = pl.BlockSpec((1, tc), lambda j: (0, j))
    act = jax.ShapeDtypeStruct((s, D_FF), BF16)
    dg, dv, gw, gb = pl.pallas_call(
        body, name="glu_bwd", grid=(nb,),
        in_specs=[own, own, pl.BlockSpec((s, tc), lambda j: (0, nb + j)), wsp, bsp],
        out_specs=[own, own, wsp, bsp],
        out_shape=[act, act, jax.ShapeDtypeStruct((3, D_FF), F32), jax.ShapeDtypeStruct((1, D_FF), F32)],
        compiler_params=_params(("parallel",)),
    )(dh, up, up, w, b.reshape(1, D_FF))
    return dg, dv, gw, gb.reshape(D_FF)


def _merge_fwd(x, oc, of, osb, wpc, wpf, wps, proj, gb, wout, tm=256):
    s, d = x.shape
    tm = min(tm, s)

    def body(x_ref, oc_ref, of_ref, os_ref, wpc_ref, wpf_ref, wps_ref, g0_ref, g1_ref, g2_ref, gb_ref, wo_ref,
             xo_ref, mg_ref, yc_ref, yf_ref, ys_ref):
        merged = jnp.zeros((tm, d), F32)
        for k, (o_ref, w_ref, g_ref, y_ref) in enumerate(
                ((oc_ref, wpc_ref, g0_ref, yc_ref), (of_ref, wpf_ref, g1_ref, yf_ref), (os_ref, wps_ref, g2_ref, ys_ref))):
            y = _dot(o_ref[...].astype(BF16), w_ref[...], _NN)
            y_ref[...] = y.astype(BF16)
            gate = jax.nn.sigmoid(g_ref[...].astype(F32) + gb_ref[:, k * d:(k + 1) * d])
            merged = merged + gate * y
        mb = merged.astype(BF16)
        mg_ref[...] = mb
        xo_ref[...] = x_ref[...] + _dot(mb, wo_ref[...], _NN)

    rowd = pl.BlockSpec((tm, d), lambda i: (i, 0))
    roww = pl.BlockSpec((tm, WIDTH), lambda i: (i, 0))
    wp = pl.BlockSpec((WIDTH, d), lambda i: (0, 0))
    gcol = lambda k: pl.BlockSpec((tm, d), lambda i, k=k: (i, GATE_OFF // d + k))
    actd = jax.ShapeDtypeStruct((s, d), BF16)
    return pl.pallas_call(
        body, name="merge_fwd", grid=(s // tm,),
        in_specs=[rowd, roww, roww, roww, wp, wp, wp, gcol(0), gcol(1), gcol(2),
                  pl.BlockSpec((1, 3 * d), lambda i: (0, 0)), pl.BlockSpec((d, d), lambda i: (0, 0))],
        out_specs=[rowd, rowd, rowd, rowd, rowd],
        out_shape=[jax.ShapeDtypeStruct((s, d), F32), actd, actd, actd, actd],
        compiler_params=_params(("parallel",)),
    )(x, oc, of, osb, wpc, wpf, wps, proj, proj, proj, gb.reshape(1, 3 * d), wout)


def _gate_bwd(dm, yc, yf, ys, proj, gb, tm=256):
    s, d = dm.shape
    tm = min(tm, s)

    def body(dm_ref, yc_ref, yf_ref, ys_ref, g0_ref, g1_ref, g2_ref, gb_ref, dyc_ref, dyf_ref, dys_ref, dgl_ref, ggb_ref):
        g = dm_ref[...].astype(F32)
        parts = []
        for k, (y_ref, g_ref, dy_ref) in enumerate(((yc_ref, g0_ref, dyc_ref), (yf_ref, g1_ref, dyf_ref), (ys_ref, g2_ref, dys_ref))):
            gate = jax.nn.sigmoid(g_ref[...].astype(F32) + gb_ref[:, k * d:(k + 1) * d])
            dy_ref[...] = (g * gate).astype(BF16)
            dgl = g * y_ref[...].astype(F32) * gate * (1.0 - gate)
            dgl_ref[:, k * d:(k + 1) * d] = dgl.astype(BF16)
            parts.append(jnp.sum(dgl, axis=0, keepdims=True))
        part = jnp.concatenate(parts, axis=1)

        @pl.when(pl.program_id(0) == 0)
        def _():
            ggb_ref[...] = part

        @pl.when(pl.program_id(0) > 0)
        def _():
            ggb_ref[...] += part

    rowd = pl.BlockSpec((tm, d), lambda i: (i, 0))
    gcol = lambda k: pl.BlockSpec((tm, d), lambda i, k=k: (i, GATE_OFF // d + k))
    vec = pl.BlockSpec((1, 3 * d), lambda i: (0, 0))
    actd = jax.ShapeDtypeStruct((s, d), BF16)
    dyc, dyf, dys, dgl, ggb = pl.pallas_call(
        body, name="gate_bwd", grid=(s // tm,),
        in_specs=[rowd, rowd, rowd, rowd, gcol(0), gcol(1), gcol(2), vec],
        out_specs=[rowd, rowd, rowd, pl.BlockSpec((tm, 3 * d), lambda i: (i, 0)), vec],
        out_shape=[actd, actd, actd, jax.ShapeDtypeStruct((s, 3 * d), BF16), jax.ShapeDtypeStruct((1, 3 * d), F32)],
        compiler_params=_params(("arbitrary",)),
    )(dm, yc, yf, ys, proj, proj, proj, gb.reshape(1, 3 * d))
    return dyc, dyf, dys, dgl, ggb.reshape(3 * d)


def _tri_tables(nq, r, order):
    last = lambda qi: (qi + 1) * r - 1
    if order == "k_outer":
        pairs = [(qi, kj) for kj in range(nq * r) for qi in range(kj // r, nq)]
    elif order == "k_desc":
        pairs = [(qi, kj) for qi in range(nq) for kj in range(last(qi), -1, -1)]
    else:
        pairs = [(qi, kj) for qi in range(nq) for kj in range(last(qi) + 1)]
    qs, ks = zip(*pairs)
    return jnp.asarray(np.array(qs, np.int32)), jnp.asarray(np.array(ks, np.int32)), len(pairs)


def _lo_mask(shape):
    return lax.broadcasted_iota(jnp.int32, shape, len(shape) - 1) < HEAD_DIM


def _head(x, h):
    lo = _lo_mask(x.shape)
    return jnp.where(lo if h == 0 else jnp.logical_not(lo), x, jnp.zeros_like(x))


def _pair_cols(x, h):
    return x[:, h * HEAD_DIM:h * HEAD_DIM + 1]


def _rep(a0, a1, shape):
    return jnp.where(_lo_mask(shape), a0, a1)


def _positions(qi, kj, tq, tk):
    row = qi * tq + lax.broadcasted_iota(jnp.int32, (tq, tk), 0)
    col = kj * tk + lax.broadcasted_iota(jnp.int32, (tq, tk), 1)
    return row, col


def _head_norm(x, g):
    lo = _lo_mask(x.shape)
    sq = x * x
    s0 = jnp.sum(jnp.where(lo, sq, 0.0), axis=-1, keepdims=True)
    s1 = jnp.sum(jnp.where(lo, 0.0, sq), axis=-1, keepdims=True)
    r = jnp.where(lo, lax.rsqrt(s0 / HEAD_DIM + NORM_EPS), lax.rsqrt(s1 / HEAD_DIM + NORM_EPS))
    return x * r, r


def _fox_prep(proj, gq, gk, tm=512):
    s = proj.shape[0]
    tm = min(tm, s)
    off = FOX_OFF // LANES

    def body(q_ref, k_ref, gq_ref, gk_ref, qn_ref, kn_ref):
        qh, _ = _head_norm(q_ref[...].astype(F32), None)
        kh, _ = _head_norm(k_ref[...].astype(F32), None)
        qn_ref[...] = (qh * gq_ref[...] * QK_SCALE).astype(BF16)
        kn_ref[...] = (kh * gk_ref[...]).astype(BF16)

    vec = pl.BlockSpec((1, LANES), lambda p, i: (0, 0))
    own = pl.BlockSpec((tm, LANES), lambda p, i: (i, p))
    act = jax.ShapeDtypeStruct((s, WIDTH), BF16)
    return pl.pallas_call(
        body, name="fox_prep", grid=(N_PAIR, s // tm),
        in_specs=[pl.BlockSpec((tm, LANES), lambda p, i: (i, off + p)),
                  pl.BlockSpec((tm, LANES), lambda p, i: (i, off + N_PAIR + p)), vec, vec],
        out_specs=[own, own], out_shape=[act, act],
        compiler_params=_params(("parallel", "parallel")),
    )(proj, proj, jnp.tile(gq, 2).reshape(1, LANES), jnp.tile(gk, 2).reshape(1, LANES))


def _fox_post(dqs, dkn, proj, gq, gk, tm=512):
    s = proj.shape[0]
    tm = min(tm, s)
    off = FOX_OFF // LANES

    def one(d_ref, x_ref, g_ref, scale, dx_ref, gg_ref, first):
        xhat, r = _head_norm(x_ref[...].astype(F32), None)
        dy = d_ref[...] * scale
        part = jnp.sum(dy * xhat, axis=0, keepdims=True)

        @pl.when(first)
        def _():
            gg_ref[...] = part

        @pl.when(jnp.logical_not(first))
        def _():
            gg_ref[...] += part

        dxh = dy * g_ref[...]
        lo = _lo_mask(dxh.shape)
        pr = dxh * xhat
        m0 = jnp.sum(jnp.where(lo, pr, 0.0), axis=-1, keepdims=True)
        m1 = jnp.sum(jnp.where(lo, 0.0, pr), axis=-1, keepdims=True)
        mean = jnp.where(lo, m0, m1) / HEAD_DIM
        dx_ref[...] = (r * (dxh - xhat * mean)).astype(BF16)

    def body(dq_ref, dk_ref, q_ref, k_ref, gq_ref, gk_ref, dxq_ref, dxk_ref, ggq_ref, ggk_ref):
        first = pl.program_id(1) == 0
        one(dq_ref, q_ref, gq_ref, QK_SCALE, dxq_ref, ggq_ref, first)
        one(dk_ref, k_ref, gk_ref, 1.0, dxk_ref, ggk_ref, first)

    vec = pl.BlockSpec((1, LANES), lambda p, i: (0, 0))
    own = pl.BlockSpec((tm, LANES), lambda p, i: (i, p))
    ggs = pl.BlockSpec((None, 1, LANES), lambda p, i: (p, 0, 0))
    act = jax.ShapeDtypeStruct((s, WIDTH), BF16)
    ggo = jax.ShapeDtypeStruct((N_PAIR, 1, LANES), F32)
    dxq, dxk, ggq, ggk = pl.pallas_call(
        body, name="fox_post", grid=(N_PAIR, s // tm),
        in_specs=[own, own, pl.BlockSpec((tm, LANES), lambda p, i: (i, off + p)),
                  pl.BlockSpec((tm, LANES), lambda p, i: (i, off + N_PAIR + p)), vec, vec],
        out_specs=[own, own, ggs, ggs], out_shape=[act, act, ggo, ggo],
        compiler_params=_params(("parallel", "arbitrary")),
    )(dqs, dkn, proj, proj, jnp.tile(gq, 2).reshape(1, LANES), jnp.tile(gk, 2).reshape(1, LANES))
    fold = lambda a: a.reshape(N_HEADS, HEAD_DIM).sum(axis=0)
    return dxq, dxk, fold(ggq), fold(ggk)


def _split3(x):
    a = x.astype(BF16)
    r = x - a.astype(F32)
    b = r.astype(BF16)
    c = (r - b.astype(F32)).astype(BF16)
    return a, b, c


def _split2(x):
    a = x.astype(BF16)
    b = (x - a.astype(F32)).astype(BF16)
    return a, b


def _log_sigmoid(x):
    return jnp.minimum(x, 0.0) - jnp.log(1.0 + jnp.exp(-jnp.abs(x)))


def _fox_gates(ft, bias):
    h, s = ft.shape
    nb = s // LANES

    def body(f_ref, b_ref, c_ref):
        lf = _log_sigmoid(f_ref[...] + b_ref[...])
        i = lax.broadcasted_iota(jnp.int32, (s, LANES), 0)
        j = pl.program_id(0) * LANES + lax.broadcasted_iota(jnp.int32, (s, LANES), 1)
        tri = jnp.where(i <= j, 1.0, 0.0).astype(BF16)
        c_ref[...] = sum(_dot(p, tri, _NN) for p in _split3(lf))

    return pl.pallas_call(
        body, name="fox_gates", grid=(nb,),
        in_specs=[pl.BlockSpec((h, s), lambda j: (0, 0)), pl.BlockSpec((h, 1), lambda j: (0, 0))],
        out_specs=pl.BlockSpec((h, LANES), lambda j: (0, j)),
        out_shape=jax.ShapeDtypeStruct((h, s), F32),
        compiler_params=_params(("parallel",)),
    )(ft, bias.reshape(h, 1))


def _fox_gates_bwd(dc, ft, bias):
    h, s = ft.shape
    nb = s // LANES

    def body(dc_ref, f_ref, fb_ref, b_ref, df_ref, gb_ref):
        i = lax.broadcasted_iota(jnp.int32, (s, LANES), 0)
        j = pl.program_id(0) * LANES + lax.broadcasted_iota(jnp.int32, (s, LANES), 1)
        tri = jnp.where(i >= j, 1.0, 0.0).astype(BF16)
        dlf = sum(_dot(p, tri, _NN) for p in _split3(dc_ref[...]))
        df = dlf * jax.nn.sigmoid(-(fb_ref[...] + b_ref[...]))
        df_ref[...] = df
        part = jnp.sum(df, axis=-1, keepdims=True)

        @pl.when(pl.program_id(0) == 0)
        def _():
            gb_ref[...] = part

        @pl.when(pl.program_id(0) > 0)
        def _():
            gb_ref[...] += part

    full = pl.BlockSpec((h, s), lambda j: (0, 0))
    blk = pl.BlockSpec((h, LANES), lambda j: (0, j))
    one = pl.BlockSpec((h, 1), lambda j: (0, 0))
    df, gb = pl.pallas_call(
        body, name="fox_gates_bwd", grid=(nb,), in_specs=[full, full, blk, one], out_specs=[blk, one],
        out_shape=[jax.ShapeDtypeStruct((h, s), F32), jax.ShapeDtypeStruct((h, 1), F32)],
        compiler_params=_params(("arbitrary",)),
    )(dc, ft, ft, bias.reshape(h, 1))
    return df, gb.reshape(h)


def _delta_rep(do, o, tm=512):
    s = do.shape[0]
    tm = min(tm, s)

    def body(do_ref, o_ref, d_ref):
        pr = do_ref[...].astype(F32) * o_ref[...].astype(F32)
        lo = _lo_mask(pr.shape)
        d0 = jnp.sum(jnp.where(lo, pr, 0.0), axis=-1, keepdims=True)
        d1 = jnp.sum(jnp.where(lo, 0.0, pr), axis=-1, keepdims=True)
        d_ref[...] = jnp.where(lo, d0, d1)

    own = pl.BlockSpec((tm, LANES), lambda p, i: (i, p))
    return pl.pallas_call(
        body, name="delta_rep", grid=(N_PAIR, s // tm), in_specs=[own, own],
        out_specs=pl.BlockSpec((None, tm, LANES), lambda p, i: (p, i, 0)),
        out_shape=jax.ShapeDtypeStruct((N_PAIR, s, LANES), F32),
        compiler_params=_params(("parallel", "parallel")),
    )(do, o)


def _tile(s, t):
    t = min(t, s)
    assert s % t == 0
    return t


def _fox_fwd(qn, kn, proj, ccol, crow, t=512, comm=None):
    s = qn.shape[0]
    t = _tile(s, t)
    n = s // t
    qtab, ktab, ntri = _tri_tables(n, 1, "k_asc")
    voff = FOX_OFF // LANES + 2 * N_PAIR
    n_in, n_out = (len(comm.inputs), len(comm.out_shapes)) if comm else (0, 0)

    def body(qt_ref, kt_ref, q_ref, k_ref, v_ref, cc_ref, cr_ref, *rest):
        comm_in, rest = rest[:n_in], rest[n_in:]
        (o_ref, lse_ref), rest = rest[:2], rest[2:]
        comm_out, rest = rest[:n_out], rest[n_out:]
        (m_ref, l_ref, acc_ref), sems = rest[:3], rest[3:]
        i = pl.program_id(1)
        qi, kj = qt_ref[i], kt_ref[i]
        if comm:
            pl.when((pl.program_id(0) == 0) & (i == 0))(lambda: comm.start(comm_in, comm_out, sems))

        @pl.when(kj == 0)
        def _():
            m_ref[...] = jnp.full(m_ref.shape, NEG_INF, F32)
            l_ref[...] = jnp.zeros(l_ref.shape, F32)
            acc_ref[...] = jnp.zeros(acc_ref.shape, F32)

        q, k, v = q_ref[...], k_ref[...], v_ref[...]
        row, col = _positions(qi, kj, t, t)
        causal = col <= row
        m_old = m_ref[...]
        mn, rs, pv = [], [], []
        for h in range(2):
            sc = _dot(_head(q, h), k, _NT) + _pair_cols(cc_ref[...], h) - cr_ref[h:h + 1, :]
            sc = jnp.where(causal, sc, NEG_INF)
            m_new = jnp.maximum(_pair_cols(m_old, h), jnp.max(sc, axis=-1, keepdims=True))
            p = jnp.exp(sc - m_new)
            mn.append(m_new)
            rs.append(jnp.sum(p, axis=-1, keepdims=True))
            pv.append(_dot(p.astype(BF16), _head(v, h), _NN))
        m_rep = _rep(mn[0], mn[1], m_old.shape)
        alpha = jnp.exp(m_old - m_rep)
        l_ref[...] = alpha * l_ref[...] + _rep(rs[0], rs[1], m_old.shape)
        acc_ref[...] = alpha * acc_ref[...] + pv[0] + pv[1]
        m_ref[...] = m_rep

        @pl.when(kj == qi)
        def _():
            o_ref[...] = acc_ref[...] / l_ref[...]
            lse_ref[...] = m_ref[...] + jnp.log(l_ref[...])

        if comm:
            pl.when((pl.program_id(0) == N_PAIR - 1) & (i == ntri - 1))(lambda: comm.finish(comm_in, comm_out, sems))

    grid_spec = pltpu.PrefetchScalarGridSpec(
        num_scalar_prefetch=2, grid=(N_PAIR, ntri),
        in_specs=[pl.BlockSpec((t, LANES), lambda p, i, qt, kt: (qt[i], p)),
                  pl.BlockSpec((t, LANES), lambda p, i, qt, kt: (kt[i], p)),
                  pl.BlockSpec((t, LANES), lambda p, i, qt, kt: (kt[i], voff + p)),
                  pl.BlockSpec((None, t, LANES), lambda p, i, qt, kt: (p, qt[i], 0)),
                  pl.BlockSpec((None, 2, t), lambda p, i, qt, kt: (p, 0, kt[i]))] + [ANY] * n_in,
        out_specs=[pl.BlockSpec((t, LANES), lambda p, i, qt, kt: (qt[i], p)),
                   pl.BlockSpec((None, t, LANES), lambda p, i, qt, kt: (p, qt[i], 0))] + [ANY] * n_out,
        scratch_shapes=[pltpu.VMEM((t, LANES), F32)] * 3 + (comm.sems if comm else []))
    outs = pl.pallas_call(
        body, name="fox_fwd", grid_spec=grid_spec,
        out_shape=[jax.ShapeDtypeStruct((s, WIDTH), F32), jax.ShapeDtypeStruct((N_PAIR, s, LANES), F32)]
        + (comm.out_shapes if comm else []),
        compiler_params=_params(("arbitrary", "arbitrary") if comm else ("parallel", "arbitrary")),
    )(qtab, ktab, qn, kn, proj, ccol, crow, *(comm.inputs if comm else []))
    return outs[:2], outs[2:]


def _fox_bwd(qn, kn, proj, do, lse, delta, ccol, crow, t=512, comm=None):
    s = qn.shape[0]
    t = _tile(s, t)
    n = s // t
    qtab, ktab, ntri = _tri_tables(n, 1, "k_outer")
    voff = FOX_OFF // LANES + 2 * N_PAIR
    n_in, n_out = (len(comm.inputs), len(comm.out_shapes)) if comm else (0, 0)

    def body(qt_ref, kt_ref, q_ref, k_ref, v_ref, do_ref, lse_ref, dl_ref, cc_ref, cr_ref, *rest):
        comm_in, rest = rest[:n_in], rest[n_in:]
        (dq_ref, dk_ref, dv_ref, dc_ref, dcq_ref), rest = rest[:5], rest[5:]
        comm_out, rest = rest[:n_out], rest[n_out:]
        (dka_ref, dva_ref, dca_ref, dcqa_ref), sems = rest[:4], rest[4:]
        i = pl.program_id(1)
        qi, kj = qt_ref[i], kt_ref[i]
        if comm:
            pl.when((pl.program_id(0) == 0) & (i == 0))(lambda: comm.start(comm_in, comm_out, sems))

        @pl.when(i == 0)
        def _():
            dq_ref[...] = jnp.zeros(dq_ref.shape, F32)
            dcqa_ref[...] = jnp.zeros(dcqa_ref.shape, F32)

        @pl.when(qi == kj)
        def _():
            dka_ref[...] = jnp.zeros(dka_ref.shape, F32)
            dva_ref[...] = jnp.zeros(dva_ref.shape, F32)
            dca_ref[...] = jnp.zeros(dca_ref.shape, F32)

        q, k, v, g = q_ref[...], k_ref[...], v_ref[...], do_ref[...]
        row, col = _positions(qi, kj, t, t)
        causal = col <= row
        dq = jnp.zeros((t, LANES), F32)
        dk = jnp.zeros((t, LANES), F32)
        dv = jnp.zeros((t, LANES), F32)
        rowsum = []
        for h in range(2):
            qh, gh = _head(q, h), _head(g, h)
            sc = _dot(qh, k, _NT) + _pair_cols(cc_ref[...], h) - cr_ref[h:h + 1, :]
            p = jnp.where(causal, jnp.exp(sc - _pair_cols(lse_ref[...], h)), 0.0)
            dp = _dot(gh, v, _NT)
            ds = p * (dp - _pair_cols(dl_ref[...], h))
            dsb = ds.astype(BF16)
            dv = dv + _dot(p.astype(BF16), gh, _TN)
            dk = dk + _dot(dsb, qh, _TN)
            dq = dq + _dot(dsb, _head(k, h), _NN)
            dca_ref[h:h + 1, :] -= jnp.sum(ds, axis=0, keepdims=True)
            rowsum.append(jnp.sum(ds, axis=-1, keepdims=True))
        dka_ref[...] += dk
        dva_ref[...] += dv
        rows = pl.ds(pl.multiple_of(qi * t, t), t)
        dq_ref[rows, :] += dq
        dcqa_ref[rows, :] += _rep(rowsum[0], rowsum[1], (t, LANES))

        @pl.when(qi == n - 1)
        def _():
            dk_ref[...] = dka_ref[...]
            dv_ref[...] = dva_ref[...].astype(BF16)
            dc_ref[...] = dca_ref[...]

        @pl.when(i == ntri - 1)
        def _():
            across = dcqa_ref[...].T
            dcq_ref[0:1, :] = across[0:1, :]
            dcq_ref[1:2, :] = across[HEAD_DIM:HEAD_DIM + 1, :]

        if comm:
            pl.when((pl.program_id(0) == N_PAIR - 1) & (i == ntri - 1))(lambda: comm.finish(comm_in, comm_out, sems))

    qblk = lambda p, i, qt, kt: (qt[i], p)
    kblk = lambda p, i, qt, kt: (kt[i], p)
    qrep = pl.BlockSpec((None, t, LANES), lambda p, i, qt, kt: (p, qt[i], 0))
    crs = pl.BlockSpec((None, 2, t), lambda p, i, qt, kt: (p, 0, kt[i]))
    grid_spec = pltpu.PrefetchScalarGridSpec(
        num_scalar_prefetch=2, grid=(N_PAIR, ntri),
        in_specs=[pl.BlockSpec((t, LANES), qblk), pl.BlockSpec((t, LANES), kblk),
                  pl.BlockSpec((t, LANES), lambda p, i, qt, kt: (kt[i], voff + p)),
                  pl.BlockSpec((t, LANES), qblk), qrep, qrep, qrep, crs] + [ANY] * n_in,
        out_specs=[pl.BlockSpec((s, LANES), lambda p, i, qt, kt: (0, p)),
                   pl.BlockSpec((t, LANES), kblk), pl.BlockSpec((t, LANES), kblk), crs,
                   pl.BlockSpec((None, 2, s), lambda p, i, qt, kt: (p, 0, 0))] + [ANY] * n_out,
        scratch_shapes=[pltpu.VMEM((t, LANES), F32), pltpu.VMEM((t, LANES), F32), pltpu.VMEM((2, t), F32),
                        pltpu.VMEM((s, LANES), F32)] + (comm.sems if comm else []))
    outs = pl.pallas_call(
        body, name="fox_bwd", grid_spec=grid_spec,
        out_shape=[jax.ShapeDtypeStruct((s, WIDTH), F32), jax.ShapeDtypeStruct((s, WIDTH), F32),
                   jax.ShapeDtypeStruct((s, WIDTH), BF16), jax.ShapeDtypeStruct((N_PAIR, 2, s), F32),
                   jax.ShapeDtypeStruct((N_PAIR, 2, s), F32)] + (comm.out_shapes if comm else []),
        compiler_params=_params(("arbitrary", "arbitrary") if comm else ("parallel", "arbitrary")),
    )(qtab, ktab, qn, kn, proj, do, lse, delta, ccol, crow, *(comm.inputs if comm else []))
    return outs[:5], outs[5:]


def _sb_tile(qh, k, strict, tk, r_col):
    z = _dot(qh, k, _NT)
    lg = jnp.where(strict, -(jnp.maximum(z, 0.0) + jnp.log(1.0 + jnp.exp(-jnp.abs(z)))), 0.0)
    jj = lax.broadcasted_iota(jnp.int32, (tk, tk), 0)
    ss = lax.broadcasted_iota(jnp.int32, (tk, tk), 1)
    above = jnp.where(jj > ss, 1.0, 0.0).astype(BF16)
    suffix = sum(_dot(p, above, _NN) for p in _split2(lg)) + r_col
    a = jnp.where(strict, jnp.exp(lg + z + suffix), 0.0)
    return z, lg, a


def _sb_fwd(proj, tq=512, tk=256, comm=None):
    s = proj.shape[0]
    tq = _tile(s, tq)
    tk = _tile(tq, tk)
    nq, r = s // tq, tq // tk
    qtab, ktab, ntri = _tri_tables(nq, r, "k_desc")
    off = SB_OFF // LANES
    n_in, n_out = (len(comm.inputs), len(comm.out_shapes)) if comm else (0, 0)

    def body(qt_ref, kt_ref, q_ref, k_ref, v_ref, *rest):
        comm_in, o_ref, rest = rest[:n_in], rest[n_in], rest[n_in + 1:]
        comm_out, rest = rest[:n_out], rest[n_out:]
        (acc_ref, r_ref), sems = rest[:2], rest[2:]
        i = pl.program_id(1)
        qi, kj = qt_ref[i], kt_ref[i]
        if comm:
            pl.when((pl.program_id(0) == 0) & (i == 0))(lambda: comm.start(comm_in, comm_out, sems))

        @pl.when(kj == (qi + 1) * r - 1)
        def _():
            acc_ref[...] = jnp.zeros(acc_ref.shape, F32)
            r_ref[...] = jnp.zeros(r_ref.shape, F32)

        q = q_ref[...] * QK_SCALE
        k, v = k_ref[...], v_ref[...]
        row, col = _positions(qi, kj, tq, tk)
        strict = col < row
        acc = acc_ref[...]
        for h in range(2):
            _, lg, a = _sb_tile(_head(q, h), k, strict, tk, r_ref[h])
            acc = acc + _dot(a.astype(BF16), _head(v, h), _NN)
            r_ref[h] += jnp.sum(lg, axis=-1, keepdims=True)
        acc_ref[...] = acc

        @pl.when(kj == 0)
        def _():
            o_ref[...] = acc_ref[...]

        if comm:
            pl.when((pl.program_id(0) == N_PAIR - 1) & (i == ntri - 1))(lambda: comm.finish(comm_in, comm_out, sems))

    grid_spec = pltpu.PrefetchScalarGridSpec(
        num_scalar_prefetch=2, grid=(N_PAIR, ntri),
        in_specs=[pl.BlockSpec((tq, LANES), lambda p, i, qt, kt: (qt[i], off + p)),
                  pl.BlockSpec((tk, LANES), lambda p, i, qt, kt: (kt[i], off + N_PAIR + p)),
                  pl.BlockSpec((tk, LANES), lambda p, i, qt, kt: (kt[i], off + 2 * N_PAIR + p))] + [ANY] * n_in,
        out_specs=[pl.BlockSpec((tq, LANES), lambda p, i, qt, kt: (qt[i], p))] + [ANY] * n_out,
        scratch_shapes=[pltpu.VMEM((tq, LANES), F32), pltpu.VMEM((2, tq, 1), F32)] + (comm.sems if comm else []))
    outs = pl.pallas_call(
        body, name="sb_fwd", grid_spec=grid_spec,
        out_shape=[jax.ShapeDtypeStruct((s, WIDTH), F32)] + (comm.out_shapes if comm else []),
        compiler_params=_params(("arbitrary", "arbitrary") if comm else ("parallel", "arbitrary")),
    )(qtab, ktab, proj, proj, proj, *(comm.inputs if comm else []))
    return outs[0], outs[1:]


def _sb_bwd(proj, do, delta, tq=512, tk=256, comm=None):
    s = proj.shape[0]
    tq = _tile(s, tq)
    tk = _tile(tq, tk)
    nq, r = s // tq, tq // tk
    qtab, ktab, ntri = _tri_tables(nq, r, "k_desc")
    off = SB_OFF // LANES
    n_in, n_out = (len(comm.inputs), len(comm.out_shapes)) if comm else (0, 0)

    def body(qt_ref, kt_ref, q_ref, k_ref, v_ref, do_ref, dl_ref, *rest):
        comm_in, rest = rest[:n_in], rest[n_in:]
        (dq_ref, dk_ref, dv_ref), rest = rest[:3], rest[3:]
        comm_out, rest = rest[:n_out], rest[n_out:]
        (dqa_ref, dka_ref, dva_ref, r_ref, rd_ref), sems = rest[:5], rest[5:]
        i = pl.program_id(1)
        qi, kj = qt_ref[i], kt_ref[i]
        if comm:
            pl.when((pl.program_id(0) == 0) & (i == 0))(lambda: comm.start(comm_in, comm_out, sems))

        @pl.when(i == 0)
        def _():
            dka_ref[...] = jnp.zeros(dka_ref.shape, F32)
            dva_ref[...] = jnp.zeros(dva_ref.shape, F32)

        @pl.when(kj == (qi + 1) * r - 1)
        def _():
            dqa_ref[...] = jnp.zeros(dqa_ref.shape, F32)
            r_ref[...] = jnp.zeros(r_ref.shape, F32)
            rd_ref[...] = jnp.zeros(rd_ref.shape, F32)

        q = q_ref[...] * QK_SCALE
        k, v, g = k_ref[...], v_ref[...], do_ref[...]
        row, col = _positions(qi, kj, tq, tk)
        strict = col < row
        ss = lax.broadcasted_iota(jnp.int32, (tk, tk), 0)
        jj = lax.broadcasted_iota(jnp.int32, (tk, tk), 1)
        at_or_after = jnp.where(ss >= jj, 1.0, 0.0).astype(BF16)
        dq = jnp.zeros((tq, LANES), F32)
        dk = jnp.zeros((tk, LANES), F32)
        dv = jnp.zeros((tk, LANES), F32)
        for h in range(2):
            qh, gh = _head(q, h), _head(g, h)
            z, lg, a = _sb_tile(qh, k, strict, tk, r_ref[h])
            da = _dot(gh, v, _NT)
            ab = a.astype(BF16)
            d_a = da * ab.astype(F32)
            incl = _dot(d_a.astype(BF16), at_or_after, _NN)
            d_l = _pair_cols(dl_ref[...], h) - rd_ref[h] - incl
            sig = jnp.exp(lg + z)
            dz = jnp.where(strict, d_a * (1.0 - sig) - d_l * sig, 0.0).astype(BF16)
            dq = dq + _dot(dz, _head(k, h), _NN)
            dk = dk + _dot(dz, qh, _TN)
            dv = dv + _dot(ab, gh, _TN)
            r_ref[h] += jnp.sum(lg, axis=-1, keepdims=True)
            rd_ref[h] += jnp.sum(d_a, axis=-1, keepdims=True)
        dqa_ref[...] += dq
        rows = pl.ds(pl.multiple_of(kj * tk, tk), tk)
        dka_ref[rows, :] += dk
        dva_ref[rows, :] += dv

        @pl.when(kj == 0)
        def _():
            dq_ref[...] = (dqa_ref[...] * QK_SCALE).astype(BF16)

        @pl.when(i == ntri - 1)
        def _():
            dk_ref[...] = dka_ref[...].astype(BF16)
            dv_ref[...] = dva_ref[...].astype(BF16)

        if comm:
            pl.when((pl.program_id(0) == N_PAIR - 1) & (i == ntri - 1))(lambda: comm.finish(comm_in, comm_out, sems))

    qblk = lambda p, i, qt, kt: (qt[i], p)
    whole = pl.BlockSpec((s, LANES), lambda p, i, qt, kt: (0, p))
    grid_spec = pltpu.PrefetchScalarGridSpec(
        num_scalar_prefetch=2, grid=(N_PAIR, ntri),
        in_specs=[pl.BlockSpec((tq, LANES), lambda p, i, qt, kt: (qt[i], off + p)),
                  pl.BlockSpec((tk, LANES), lambda p, i, qt, kt: (kt[i], off + N_PAIR + p)),
                  pl.BlockSpec((tk, LANES), lambda p, i, qt, kt: (kt[i], off + 2 * N_PAIR + p)),
                  pl.BlockSpec((tq, LANES), qblk),
                  pl.BlockSpec((None, tq, LANES), lambda p, i, qt, kt: (p, qt[i], 0))] + [ANY] * n_in,
        out_specs=[pl.BlockSpec((tq, LANES), qblk), whole, whole] + [ANY] * n_out,
        scratch_shapes=[pltpu.VMEM((tq, LANES), F32), pltpu.VMEM((s, LANES), F32), pltpu.VMEM((s, LANES), F32),
                        pltpu.VMEM((2, tq, 1), F32), pltpu.VMEM((2, tq, 1), F32)] + (comm.sems if comm else []))
    act = jax.ShapeDtypeStruct((s, WIDTH), BF16)
    outs = pl.pallas_call(
        body, name="sb_bwd", grid_spec=grid_spec, out_shape=[act, act, act] + (comm.out_shapes if comm else []),
        compiler_params=_params(("arbitrary", "arbitrary") if comm else ("parallel", "arbitrary")),
    )(qtab, ktab, proj, proj, proj, do, delta, *(comm.inputs if comm else []))
    return outs[:3], outs[3:]


def _loss_head(y, target, tm=256):
    s, d = y.shape
    tm = min(tm, s)

    def body(y_ref, t_ref, l_ref, dy_ref):
        e = y_ref[...] - t_ref[...]
        dy_ref[...] = e / d
        part = jnp.sum(jnp.sum(e * e, axis=0, keepdims=True), axis=1, keepdims=True)

        @pl.when(pl.program_id(0) == 0)
        def _():
            l_ref[...] = jnp.broadcast_to(part, l_ref.shape)

        @pl.when(pl.program_id(0) > 0)
        def _():
            l_ref[...] += jnp.broadcast_to(part, l_ref.shape)

    row = pl.BlockSpec((tm, d), lambda i: (i, 0))
    return pl.pallas_call(
        body, name="loss_head", grid=(s // tm,), in_specs=[row, row],
        out_specs=[pl.BlockSpec((1, LANES), lambda i: (0, 0)), row],
        out_shape=[jax.ShapeDtypeStruct((1, LANES), F32), jax.ShapeDtypeStruct((s, d), F32)],
        compiler_params=_params(("arbitrary",)),
    )(y, target)


def _adamw(w, g, m, v, name):
    shape = w.shape
    cols = shape[-1]
    rows = int(np.prod(shape[:-1]))
    tm = rows
    for cand in (512, 256, 128, 64, 32, 16, 8):
        if rows % cand == 0 and rows > cand and cand * cols * 4 <= ADAMW_BLOCK_BYTES:
            tm = cand
            break

    def body(w_ref, g_ref, m_ref, v_ref, d_ref, mo_ref, vo_ref):
        gr = g_ref[...]
        mn = ADAM_B1 * m_ref[...] + (1.0 - ADAM_B1) * gr
        vn = ADAM_B2 * v_ref[...] + (1.0 - ADAM_B2) * (gr * gr)
        m_hat = mn / (1.0 - ADAM_B1 ** ADAM_STEP)
        v_hat = vn / (1.0 - ADAM_B2 ** ADAM_STEP)
        d_ref[...] = -ADAM_LR * (m_hat / (jnp.sqrt(v_hat) + ADAM_EPS) + ADAM_WD * w_ref[...])
        mo_ref[...] = mn
        vo_ref[...] = vn

    blk = pl.BlockSpec((tm, cols), lambda i: (i, 0))
    out = jax.ShapeDtypeStruct((rows, cols), F32)
    r2 = lambda a: a.reshape(rows, cols)
    outs = pl.pallas_call(
        body, name=name, grid=(rows // tm,), in_specs=[blk] * 4, out_specs=[blk] * 3, out_shape=[out] * 3,
        compiler_params=_params(("parallel",)),
    )(r2(w), r2(g), r2(m), r2(v))
    return tuple(o.reshape(shape) for o in outs)


W_IN_NATURAL = (("a", CONV_OFF, 3 * WIDTH), ("a", FOX_OFF, 3 * WIDTH), ("f", 0, N_HEADS), ("a", SB_OFF, 3 * WIDTH),
                ("a", GATE_OFF, 3 * D_MODEL))
W_IN_COLS = D_INA + N_HEADS
W_IN_SHARD = W_IN_COLS // 4


def _w_in_from_chips(blocks):
    def natural(lo, hi):
        cols = []
        for k, blk in enumerate(blocks):
            a, b = max(lo, k * W_IN_SHARD), min(hi, (k + 1) * W_IN_SHARD)
            if a < b:
                cols.append(blk[:, a - k * W_IN_SHARD:b - k * W_IN_SHARD])
        return cols

    start = {}
    pos = 0
    for part, at, length in W_IN_NATURAL:
        start[part, at] = (pos, pos + length)
        pos += length
    order = sorted((at, rng) for (part, at), rng in start.items() if part == "a")
    wa = jnp.concatenate([c for _, (lo, hi) in order for c in natural(lo, hi)], axis=1)
    wf = jnp.pad(jnp.concatenate(natural(*start["f", 0]), axis=1), ((0, 0), (0, LANES - N_HEADS)))
    return wa, wf


def _w_in_to_chips(ga, gf):
    chips = []
    for k in range(4):
        lo, hi, pos, cols = k * W_IN_SHARD, (k + 1) * W_IN_SHARD, 0, []
        for part, at, length in W_IN_NATURAL:
            a, b = max(lo, pos), min(hi, pos + length)
            if a < b:
                cols.append((ga if part == "a" else gf)[:, at + a - pos:at + b - pos])
            pos += length
        chips.append(jnp.concatenate(cols, axis=1))
    return jnp.stack(chips)


def _layer_fwd(x, p, plan, l):
    s = x.shape[0]
    proj, hn1 = _norm_mm(x, p["norm1_g"], p["wa"], BF16, name="in_proj")
    fraw, _ = _norm_mm(x, p["norm1_g"], p["wf"], F32, name="in_proj_f")
    ft = fraw[:, :N_HEADS].T
    crow8 = _fox_gates(ft, p["fox_f_bias"])
    crow = crow8.reshape(N_PAIR, 2, s)
    ccol = jnp.repeat(crow.transpose(0, 2, 1), HEAD_DIM, axis=2)
    oc = _conv_mix_fwd(proj, p["conv_w"])
    qn, kn = _fox_prep(proj, p["fox_q_norm_g"], p["fox_k_norm_g"])
    last = l + 1 == plan.depth
    (of, lse), got_in = _fox_fwd(qn, kn, proj, ccol, crow, comm=None if last else plan.gather(l + 1, BIG_IN))
    osb, got_rest = _sb_fwd(proj, comm=plan.gather(l, BIG_REST))
    p.update(plan.weights(l, BIG_REST, got_rest))
    nxt = None if last else plan.layer(l + 1, got_in)
    xm, merged, yc, yf, ys = _merge_fwd(x, oc, of, osb, p["w_proj_conv"], p["w_proj_fox"], p["w_proj_sb"], proj,
                                        p["gate_bias"], p["w_out"])
    up, hn2 = _norm_mm(xm, p["norm2_g"], p["w_up"], BF16, name="up_proj")
    hh = _glu_fwd(up, p["ffn_conv_w"], p["ffn_conv_b"])
    xo = _mm(hh, p["w_down"], "nn", F32, res=xm, name="down_proj")
    saved = dict(x=x, hn1=hn1, proj=proj, ft=ft, crow=crow, ccol=ccol, oc=oc, qn=qn, kn=kn, of=of, lse=lse, osb=osb,
                 merged=merged, yc=yc, yf=yf, ys=ys, xm=xm, hn2=hn2, up=up, hh=hh)
    return xo, saved, nxt


def _layer_bwd(dx, p, a, plan, pending_in):
    s = dx.shape[0]
    g = {}
    dhh = _mm(dx, p["w_down"], "nt", BF16, tn=1408, name="d_down_in")
    g["w_down"] = _mm(a["hh"], dx, "tn", F32, name="g_w_down")
    dug, duv, g["ffn_conv_w"], g["ffn_conv_b"] = _glu_bwd(dhh, a["up"], p["ffn_conv_w"], p["ffn_conv_b"])
    dup = jnp.concatenate([dug, duv], axis=1)
    g["w_up"] = _mm(a["hn2"], dup, "tn", F32, tn=1408, name="g_w_up")
    dhn2 = _mm(dup, p["w_up"], "nt", F32, name="d_up_in")
    dx, g["norm2_g"] = _norm_bwd(dhn2, a["xm"], p["norm2_g"], dx, name="norm2_bwd")
    dm = _mm(dx, p["w_out"], "nt", BF16, name="d_out_in")
    g["w_out"] = _mm(a["merged"], dx, "tn", F32, tn=512, name="g_w_out")
    dyc, dyf, dys, dgl, g["gate_bias"] = _gate_bwd(dm, a["yc"], a["yf"], a["ys"], a["proj"], p["gate_bias"])
    doc = _mm(dyc, p["w_proj_conv"], "nt", BF16, name="d_pc_in")
    dof = _mm(dyf, p["w_proj_fox"], "nt", BF16, name="d_pf_in")
    dos = _mm(dys, p["w_proj_sb"], "nt", BF16, name="d_ps_in")
    g["w_proj_conv"] = _mm(a["oc"], dyc, "tn", F32, name="g_w_pc")
    g["w_proj_fox"] = _mm(a["of"], dyf, "tn", F32, name="g_w_pf")
    g["w_proj_sb"] = _mm(a["osb"], dys, "tn", F32, name="g_w_ps")
    dcb, dcc, dch, g["conv_w"] = _conv_mix_bwd(doc, a["proj"], p["conv_w"])
    delta_f = _delta_rep(dof, a["of"])
    pending = plan.reduction(pending_in, {k: g[k] for k in BIG_REST})
    (dqs, dkn, dfv, dcrow, dcq), recv_a = _fox_bwd(a["qn"], a["kn"], a["proj"], dof, a["lse"], delta_f, a["ccol"], a["crow"],
                                                   comm=pending.exchange() if pending else None)
    dc = (dcrow + dcq).reshape(N_HEADS, s)
    dfq, dfk, g["fox_q_norm_g"], g["fox_k_norm_g"] = _fox_post(dqs, dkn, a["proj"], p["fox_q_norm_g"], p["fox_k_norm_g"])
    dft, g["fox_f_bias"] = _fox_gates_bwd(dc, a["ft"], p["fox_f_bias"])
    delta_s = _delta_rep(dos, a["osb"])
    (dsq, dsk, dsv), recv_b = _sb_bwd(a["proj"], dos, delta_s, comm=pending.sums(recv_a) if pending else None)
    done_in = None
    if pending:
        done = pending.finish(recv_b)
        if pending_in is not None:
            done_in, done = done[0], done[1:]
        g.update(zip(BIG_REST, done))
    dproj = jnp.concatenate([dgl, dcb, dcc, dch, dfq, dfk, dfv, dsq, dsk, dsv], axis=1)
    dfp = jnp.pad(dft.T, ((0, 0), (0, LANES - N_HEADS))).astype(BF16)
    ga = _mm(a["hn1"], dproj, "tn", F32, tn=768, name="g_w_in")
    gf = _mm(a["hn1"], dfp, "tn", F32, name="g_w_in_f")
    g["w_in"] = _w_in_to_chips(ga, gf)
    dhn1 = _mm(dfp, p["wf"], "nt", F32, name="d_in_f")
    dhn1 = _mm(dproj, p["wa"], "nt", F32, res=dhn1, tk=1536, name="d_in")
    dx, g["norm1_g"] = _norm_bwd(dhn1, a["x"], p["norm1_g"], dx, name="norm1_bwd")
    return dx, g, done_in


MATMUL_WEIGHTS = ("w_in", "w_proj_conv", "w_proj_fox", "w_proj_sb", "w_out", "w_up", "w_down")
WEIGHTS = ("norm1_g", "w_in", "fox_f_bias", "gate_bias", "conv_w", "fox_q_norm_g", "fox_k_norm_g", "w_proj_conv",
           "w_proj_fox", "w_proj_sb", "w_out", "norm2_g", "w_up", "ffn_conv_w", "ffn_conv_b", "w_down")


def _local_step(x, target, plan):
    depth = plan.depth
    layers, saved = [plan.layer(0, None)], []
    for l in range(depth):
        x, a, nxt = _layer_fwd(x, layers[l], plan, l)
        saved.append(a)
        if nxt is not None:
            layers.append(nxt)
    sq, dx = _loss_head(x, target)
    grads, pending_in = [None] * depth, None
    for l in reversed(range(depth)):
        dx, grads[l], done_in = _layer_bwd(dx, layers[l], saved[l], plan, pending_in)
        if done_in is not None:
            grads[l + 1]["w_in"] = done_in
        pending_in = grads[l]["w_in"]
    last = plan.reduction(pending_in, {})
    if last is not None:
        (grads[0]["w_in"],) = last.run()
    return sq, dx, grads


ANY = pl.BlockSpec(memory_space=pl.ANY)


def _place():
    x, y, c = lax.axis_index("x"), lax.axis_index("y"), lax.axis_index("c")
    chips = [(1 - x, y), (x, 1 - y), (1 - x, 1 - y)]
    return x, y, c, chips


_Comm = collections.namedtuple("_Comm", "inputs out_shapes sems start finish")


def _run_comm(comm, name):
    n_in, n_out = len(comm.inputs), len(comm.out_shapes)

    def body(*refs):
        ins, outs, sems = refs[:n_in], refs[n_in:n_in + n_out], refs[n_in + n_out:]
        comm.start(ins, outs, sems)
        comm.finish(ins, outs, sems)

    return pl.pallas_call(body, name=name, in_specs=[ANY] * n_in, out_specs=[ANY] * n_out, out_shape=comm.out_shapes,
                          scratch_shapes=comm.sems)(*comm.inputs)


def _gather_comm(shards):
    n = len(shards)

    def copy(x_refs, out_refs, sems, k, t, chip_index, which_half, to, from_input=False):
        half = shards[t].shape[0] // 2
        rows = pl.ds(which_half * half, half)
        dst = out_refs[t].at[chip_index, rows, :]
        return pltpu.make_async_remote_copy(
            src_ref=x_refs[t].at[rows, :] if from_input else dst, dst_ref=dst,
            send_sem=sems[0].at[k, t], recv_sem=sems[1].at[k, t], device_id=to, device_id_type=MESH)

    def first(x_refs, out_refs, sems):
        x, y, c, chips = _place()
        return [copy(x_refs, out_refs, sems, p, t, 2 * x + y, c, (*chip, c), from_input=True)
                for t in range(n) for p, chip in enumerate(chips)]

    def start(x_refs, out_refs, sems):
        for cp in first(x_refs, out_refs, sems):
            cp.start()

    def finish(x_refs, out_refs, sems):
        x, y, c, chips = _place()
        passed = []
        for t in range(n):
            for p, chip in enumerate(chips):
                copy(x_refs, out_refs, sems, p, t, 2 * chip[0] + chip[1], c, (x, y, c)).wait_recv()
                fwd = copy(x_refs, out_refs, sems, 3 + p, t, 2 * chip[0] + chip[1], c, (x, y, 1 - c))
                fwd.start()
                passed.append(fwd)
        for t in range(n):
            for p, chip in enumerate(chips):
                copy(x_refs, out_refs, sems, 3 + p, t, 2 * chip[0] + chip[1], 1 - c, (x, y, c)).wait_recv()
        for cp in first(x_refs, out_refs, sems) + passed:
            cp.wait_send()

    return _Comm(list(shards), [jax.ShapeDtypeStruct((4, *s.shape), s.dtype) for s in shards],
                 [pltpu.SemaphoreType.DMA((6, n)), pltpu.SemaphoreType.DMA((6, n))], start, finish)


def _own_block(got, shards):
    chip = 2 * lax.axis_index("x") + lax.axis_index("y")
    return [lax.dynamic_update_index_in_dim(g, s, chip, 0) for g, s in zip(got, shards)]


def _pair_exchange_comm(gs):
    n = len(gs)

    def copies(g_refs, recv_refs, sems):
        x, y, c, _ = _place()
        return [pltpu.make_async_remote_copy(
            src_ref=g_refs[t].at[:, pl.ds((1 - c) * (gs[t].shape[1] // 2), gs[t].shape[1] // 2), :], dst_ref=recv_refs[t],
            send_sem=sems[0].at[t], recv_sem=sems[1].at[t], device_id=(x, y, 1 - c), device_id_type=MESH) for t in range(n)]

    def start(g_refs, recv_refs, sems):
        for cp in copies(g_refs, recv_refs, sems):
            cp.start()

    def finish(g_refs, recv_refs, sems):
        for cp in copies(g_refs, recv_refs, sems):
            cp.wait()

    return _Comm(list(gs), [jax.ShapeDtypeStruct((4, g.shape[1] // 2, g.shape[2]), g.dtype) for g in gs],
                 [pltpu.SemaphoreType.DMA((n,)), pltpu.SemaphoreType.DMA((n,))], start, finish)


def _pair_sum(g, recv, core, tr=256):
    n, r, cols = g.shape
    half = r // 2
    tr = _row_tile(half, tr)
    nb = half // tr

    def body(c_ref, g_ref, r_ref, o_ref):
        o_ref[...] = (g_ref[...] + r_ref[...]).astype(BF16)

    grid_spec = pltpu.PrefetchScalarGridSpec(
        num_scalar_prefetch=1, grid=(n, nb),
        in_specs=[pl.BlockSpec((None, tr, cols), lambda k, i, c: (k, c[0] * nb + i, 0)),
                  pl.BlockSpec((None, tr, cols), lambda k, i, c: (k, i, 0))],
        out_specs=pl.BlockSpec((None, tr, cols), lambda k, i, c: (k, i, 0)))
    return pl.pallas_call(
        body, name="rs_pair_sum", grid_spec=grid_spec, out_shape=jax.ShapeDtypeStruct((n, half, cols), BF16),
        compiler_params=_params(("parallel", "parallel")),
    )(core, g, recv)


def _row_tile(rows, pref):
    best = None
    for t in range(16, min(rows, pref) + 1, 16):
        if rows % t == 0:
            best = t
    assert best is not None, (rows, pref)
    return best


def _chip_exchange_comm(s1s):
    n = len(s1s)

    def copies(s_refs, recv_refs, sems):
        x, y, c, chips = _place()
        return [pltpu.make_async_remote_copy(
            src_ref=s_refs[t].at[2 * chip[0] + chip[1]], dst_ref=recv_refs[t].at[p],
            send_sem=sems[0].at[p, t], recv_sem=sems[1].at[p, t], device_id=(*chip, c), device_id_type=MESH)
            for t in range(n) for p, chip in enumerate(chips)]

    def start(s_refs, recv_refs, sems):
        for cp in copies(s_refs, recv_refs, sems):
            cp.start()

    def finish(s_refs, recv_refs, sems):
        for cp in copies(s_refs, recv_refs, sems):
            cp.wait()

    return _Comm(list(s1s), [jax.ShapeDtypeStruct((3, *s.shape[1:]), s.dtype) for s in s1s],
                 [pltpu.SemaphoreType.DMA((3, n)), pltpu.SemaphoreType.DMA((3, n))], start, finish)


def _final_sum(g, recv_a, recv_b, core, chip, tr=256):
    n, r, cols = g.shape
    half = r // 2
    tr = _row_tile(half, tr)
    nb = half // tr

    def body(c_ref, k_ref, g_ref, a_ref, b0_ref, b1_ref, b2_ref, o_ref):
        total = g_ref[...] + a_ref[...]
        for b_ref in (b0_ref, b1_ref, b2_ref):
            total = total + b_ref[...].astype(F32)
        o_ref[...] = total

    rel = lambda p: pl.BlockSpec((None, tr, cols), lambda i, c, k, p=p: (p, i, 0))
    grid_spec = pltpu.PrefetchScalarGridSpec(
        num_scalar_prefetch=2, grid=(nb,),
        in_specs=[pl.BlockSpec((None, tr, cols), lambda i, c, k: (k[0], c[0] * nb + i, 0)),
                  pl.BlockSpec((None, tr, cols), lambda i, c, k: (k[0], i, 0)), rel(0), rel(1), rel(2)],
        out_specs=pl.BlockSpec((tr, cols), lambda i, c, k: (c[0] * nb + i, 0)))
    return pl.pallas_call(
        body, name="rs_final_sum", grid_spec=grid_spec, out_shape=jax.ShapeDtypeStruct((r, cols), F32),
        compiler_params=_params(("parallel",)),
    )(core, chip, g, recv_a, recv_b, recv_b, recv_b)


def _pair_join(fs):
    n = len(fs)

    def body(*refs):
        out_refs = refs[n:2 * n]
        send_sems, recv_sems = refs[2 * n:]
        x, y, c, _ = _place()

        def copy(t, which_half):
            h = fs[t].shape[0] // 2
            rows = out_refs[t].at[pl.ds(which_half * h, h), :]
            return pltpu.make_async_remote_copy(
                src_ref=rows, dst_ref=rows, send_sem=send_sems.at[t], recv_sem=recv_sems.at[t],
                device_id=(x, y, 1 - c), device_id_type=MESH)

        sends = [copy(t, c) for t in range(n)]
        for cp in sends:
            cp.start()
        for t in range(n):
            sends[t].wait_send()
            copy(t, 1 - c).wait_recv()

    return pl.pallas_call(
        body, name="rs_pair_join", in_specs=[ANY] * n, out_specs=[ANY] * n,
        out_shape=[jax.ShapeDtypeStruct(f.shape, f.dtype) for f in fs],
        input_output_aliases={t: t for t in range(n)},
        scratch_shapes=[pltpu.SemaphoreType.DMA((n,)), pltpu.SemaphoreType.DMA((n,))],
    )(*fs)


class _Reduction:
    def __init__(self, gs):
        self.gs = gs
        self.core = lax.axis_index("c").astype(jnp.int32).reshape(1)
        self.chip = (2 * lax.axis_index("x") + lax.axis_index("y")).astype(jnp.int32).reshape(1)

    def exchange(self):
        return _pair_exchange_comm(self.gs)

    def sums(self, recv_a):
        self.recv_a = list(recv_a)
        return _chip_exchange_comm([_pair_sum(g, ra, self.core) for g, ra in zip(self.gs, self.recv_a)])

    def finish(self, recv_b):
        return _pair_join([_final_sum(g, ra, rb, self.core, self.chip) for g, ra, rb in zip(self.gs, self.recv_a, recv_b)])

    def run(self):
        recv_a = _run_comm(self.exchange(), "rs_pair_exchange")
        return self.finish(_run_comm(self.sums(recv_a), "rs_chip_exchange"))


def _all_reduce_small(v):
    r, cols = v.shape

    def body(v_ref, out_ref, buf_ref, send_sems, recv_sems):
        x, y, c, _ = _place()
        flip = lambda a, bit: 1 - a if bit else a
        buf_ref[4 * x + 2 * y + c] = v_ref[...]
        cps = []
        for rel in range(1, 8):
            peer = (flip(x, rel & 4), flip(y, rel & 2), flip(c, rel & 1))
            cps.append(pltpu.make_async_remote_copy(
                src_ref=v_ref, dst_ref=buf_ref.at[4 * x + 2 * y + c], send_sem=send_sems.at[rel - 1], recv_sem=recv_sems.at[rel - 1],
                device_id=peer, device_id_type=MESH))
        for cp in cps:
            cp.start()
        for cp in cps:
            cp.wait()
        total = buf_ref[0]
        for d in range(1, 8):
            total = total + buf_ref[d]
        out_ref[...] = total

    vm = pl.BlockSpec(memory_space=pltpu.VMEM)
    return pl.pallas_call(
        body, name="all_reduce_small", in_specs=[vm], out_specs=vm, out_shape=jax.ShapeDtypeStruct((r, cols), F32),
        scratch_shapes=[pltpu.VMEM((8, r, cols), F32), pltpu.SemaphoreType.DMA((7,)), pltpu.SemaphoreType.DMA((7,))],
    )(v)


SHARD_AXIS = {"w_in": 1, "conv_w": 1, "w_proj_conv": 1, "w_proj_fox": 1, "w_proj_sb": 1, "w_out": 0, "w_up": 1,
              "ffn_conv_w": 1, "w_down": 0}
SMALL_SHARDED = ("conv_w", "ffn_conv_w")
BIG = tuple(k for k in SHARD_AXIS if k not in SMALL_SHARDED)
BIG_IN = ("w_in",)
BIG_REST = tuple(k for k in BIG if k not in BIG_IN)
REPLICATED = tuple(k for k in WEIGHTS if k not in SHARD_AXIS)
SMALL = REPLICATED + SMALL_SHARDED


def _pack_small(parts, row_align):
    flat = jnp.concatenate([p.reshape(-1) for p in parts])
    rows = -(-flat.shape[0] // (PACK_COLS * row_align)) * row_align
    return jnp.pad(flat, (0, rows * PACK_COLS - flat.shape[0])).reshape(rows, PACK_COLS)


def _unpack_small(packed, shapes):
    flat = packed.reshape(-1)
    out, off = [], 0
    for shape in shapes:
        size = int(np.prod(shape))
        out.append(flat[off:off + size].reshape(shape))
        off += size
    return out


def _blocks(stacked, own):
    chip = 2 * lax.axis_index("x") + lax.axis_index("y")
    return [jnp.where(chip == k, own, stacked[k]) for k in range(4)]


def _to_chips(full, axis):
    a, b = full.shape
    if axis == 0:
        return full.reshape(4, a // 4, b)
    return jnp.moveaxis(full.reshape(a, 4, b // 4), 1, 0)


class _Plan:
    def __init__(self, given):
        self.given = given
        self.depth = given["norm1_g"].shape[0]
        conv_shapes = [given[k].shape for k in SMALL_SHARDED]
        packed = [_pack_small([given[k] for k in SMALL_SHARDED], 16)]
        (got,) = _own_block(_run_comm(_gather_comm(packed), "gather_conv_weights"), packed)
        by_chip = [_unpack_small(got[j], conv_shapes) for j in range(4)]
        self.conv = {k: jnp.concatenate([by_chip[j][i] for j in range(4)], axis=-1) for i, k in enumerate(SMALL_SHARDED)}
        self.shards = {}

    def _shards(self, l, names):
        if (l, names) not in self.shards:
            self.shards[l, names] = [self.given[k][l].astype(BF16) for k in names]
        return self.shards[l, names]

    def gather(self, l, names):
        return _gather_comm(self._shards(l, names))

    def weights(self, l, names, got):
        out = {}
        for k, stacked, own in zip(names, got, self._shards(l, names)):
            if k == "w_in":
                out["wa"], out["wf"] = _w_in_from_chips(_blocks(stacked, own))
            else:
                out[k] = jnp.concatenate(_blocks(stacked, own), axis=SHARD_AXIS[k])
        return out

    def layer(self, l, got_in):
        if got_in is None:
            got_in = _run_comm(self.gather(l, BIG_IN), "gather_w_in")
        p = {k: self.given[k][l] for k in REPLICATED}
        p.update({k: self.conv[k][l] for k in SMALL_SHARDED})
        p.update(self.weights(l, BIG_IN, got_in))
        return p

    def reduction(self, w_in_grad, rest):
        gs = [] if w_in_grad is None else [w_in_grad]
        return _Reduction(gs + [_to_chips(rest[k], SHARD_AXIS[k]) for k in BIG_REST if k in rest])


def kernel(x, norm1_g, w_in, fox_f_bias, gate_bias, conv_w, fox_q_norm_g, fox_k_norm_g, w_proj_conv, w_proj_fox, w_proj_sb, w_out, norm2_g, w_up, ffn_conv_w, ffn_conv_b, w_down, loss_target, m_norm1_g, m_w_in, m_fox_f_bias, m_gate_bias, m_conv_w, m_fox_q_norm_g, m_fox_k_norm_g, m_w_proj_conv, m_w_proj_fox, m_w_proj_sb, m_w_out, m_norm2_g, m_w_up, m_ffn_conv_w, m_ffn_conv_b, m_w_down, v_norm1_g, v_w_in, v_fox_f_bias, v_gate_bias, v_conv_w, v_fox_q_norm_g, v_fox_k_norm_g, v_w_proj_conv, v_w_proj_fox, v_w_proj_sb, v_w_out, v_norm2_g, v_w_up, v_ffn_conv_w, v_ffn_conv_b, v_w_down):
    given = dict(x=x, norm1_g=norm1_g, w_in=w_in, fox_f_bias=fox_f_bias, gate_bias=gate_bias, conv_w=conv_w, fox_q_norm_g=fox_q_norm_g, fox_k_norm_g=fox_k_norm_g, w_proj_conv=w_proj_conv, w_proj_fox=w_proj_fox, w_proj_sb=w_proj_sb, w_out=w_out, norm2_g=norm2_g, w_up=w_up, ffn_conv_w=ffn_conv_w, ffn_conv_b=ffn_conv_b, w_down=w_down, loss_target=loss_target, m_norm1_g=m_norm1_g, m_w_in=m_w_in, m_fox_f_bias=m_fox_f_bias, m_gate_bias=m_gate_bias, m_conv_w=m_conv_w, m_fox_q_norm_g=m_fox_q_norm_g, m_fox_k_norm_g=m_fox_k_norm_g, m_w_proj_conv=m_w_proj_conv, m_w_proj_fox=m_w_proj_fox, m_w_proj_sb=m_w_proj_sb, m_w_out=m_w_out, m_norm2_g=m_norm2_g, m_w_up=m_w_up, m_ffn_conv_w=m_ffn_conv_w, m_ffn_conv_b=m_ffn_conv_b, m_w_down=m_w_down, v_norm1_g=v_norm1_g, v_w_in=v_w_in, v_fox_f_bias=v_fox_f_bias, v_gate_bias=v_gate_bias, v_conv_w=v_conv_w, v_fox_q_norm_g=v_fox_q_norm_g, v_fox_k_norm_g=v_fox_k_norm_g, v_w_proj_conv=v_w_proj_conv, v_w_proj_fox=v_w_proj_fox, v_w_proj_sb=v_w_proj_sb, v_w_out=v_w_out, v_norm2_g=v_norm2_g, v_w_up=v_w_up, v_ffn_conv_w=v_ffn_conv_w, v_ffn_conv_b=v_ffn_conv_b, v_w_down=v_w_down)
    depth = given["norm1_g"].shape[0]
    chip = 2 * lax.axis_index("x") + lax.axis_index("y")

    sq, dx, grads = _local_step(given["x"][0], given["loss_target"][0], _Plan(given))
    loss = lax.psum(0.5 * sq[0, 0] / D_MODEL, ("x", "y", "c"))

    gsum = {k: jnp.stack([g[k] for g in grads]) for k in BIG}
    small_shapes = [(depth, *grads[0][k].shape) for k in SMALL]
    summed = _all_reduce_small(_pack_small([jnp.stack([g[k] for g in grads]) for k in SMALL], 8))
    for k, total in zip(SMALL, _unpack_small(summed, small_shapes)):
        if k in SMALL_SHARDED:
            total = lax.dynamic_index_in_dim(total.reshape(*total.shape[:-1], 4, total.shape[-1] // 4), chip, axis=2, keepdims=False)
        gsum[k] = total

    deltas, new_m, new_v = {}, {}, {}
    for k in WEIGHTS:
        deltas[k], new_m[k], new_v[k] = _adamw(given[k], gsum[k], given["m_" + k], given["v_" + k], "adamw_" + k)
    return (loss, dx[None], *[gsum[k] for k in WEIGHTS], *[deltas[k] for k in WEIGHTS],
            *[new_m[k] for k in WEIGHTS], *[new_v[k] for k in WEIGHTS])
```

```python
import collections

import numpy as np
import jax
import jax.numpy as jnp
from jax import lax
from jax.experimental import pallas as pl
from jax.experimental.pallas import tpu as pltpu

F32 = jnp.float32
BF16 = jnp.bfloat16

D_MODEL = 1024
DEPTH = 4
HEAD_DIM = 64
N_HEADS = 8
WIDTH = 512
D_FF = 2816
NORM_EPS = 1e-6
NEG_INF = -1e30
QK_SCALE = HEAD_DIM ** -0.5
LANES = 128
N_PAIR = N_HEADS // 2

GATE_OFF = 0
CONV_OFF = 3 * D_MODEL
FOX_OFF = CONV_OFF + 3 * WIDTH
SB_OFF = FOX_OFF + 3 * WIDTH
D_INA = SB_OFF + 3 * WIDTH

ADAM_LR = 0.001
ADAM_B1 = 0.9
ADAM_B2 = 0.999
ADAM_EPS = 1e-08
ADAM_WD = 0.01
ADAM_STEP = 10

VMEM_LIMIT = 48 * 1024 * 1024
ADAMW_BLOCK_BYTES = 1024 * 1024

PACK_COLS = 1024
PACK_ROW_ALIGN = 32
MESH = pl.DeviceIdType.MESH


def _params(sem):
    return pltpu.CompilerParams(dimension_semantics=sem, vmem_limit_bytes=VMEM_LIMIT)


def _dot(a, b, dims):
    return lax.dot_general(a, b, (dims, ((), ())), preferred_element_type=F32)


_NN = ((1,), (0,))
_NT = ((1,), (1,))
_TN = ((0,), (0,))


def _pick(dim, pref):
    if dim <= pref:
        return dim
    best = None
    for mult in range(1, dim // LANES + 1):
        t = mult * LANES
        if t <= pref and dim % t == 0:
            best = t
    assert best is not None, (dim, pref)
    return best


def _mm(a, b, mode, out_dtype=F32, tm=1024, tn=1024, tk=2048, res=None, name="mm"):
    if mode == "nn":
        (m, k), (_, n) = a.shape, b.shape
    elif mode == "nt":
        (m, k), (n, _) = a.shape, b.shape
    else:
        (k, m), (_, n) = a.shape, b.shape
    tm, tn, tk = _pick(m, tm), _pick(n, tn), _pick(k, tk)
    nk = k // tk
    dims = {"nn": _NN, "nt": _NT, "tn": _TN}[mode]
    if mode == "tn":
        a_spec = pl.BlockSpec((tk, tm), lambda i, j, kk: (kk, i))
    else:
        a_spec = pl.BlockSpec((tm, tk), lambda i, j, kk: (i, kk))
    if mode == "nt":
        b_spec = pl.BlockSpec((tn, tk), lambda i, j, kk: (j, kk))
    else:
        b_spec = pl.BlockSpec((tk, tn), lambda i, j, kk: (kk, j))
    o_spec = pl.BlockSpec((tm, tn), lambda i, j, kk: (i, j))
    in_specs = [a_spec, b_spec] + ([o_spec] if res is not None else [])

    def body(*refs):
        if res is not None:
            a_ref, b_ref, r_ref, o_ref, acc_ref = refs
        else:
            a_ref, b_ref, o_ref, acc_ref = refs
            r_ref = None
        kk = pl.program_id(2)
        part = _dot(a_ref[...].astype(BF16), b_ref[...].astype(BF16), dims)

        def finish(total):
            if r_ref is not None:
                total = total + r_ref[...].astype(F32)
            o_ref[...] = total.astype(out_dtype)

        if nk == 1:
            finish(part)
        else:
            @pl.when(kk == 0)
            def _():
                acc_ref[...] = part

            @pl.when(kk > 0)
            def _():
                acc_ref[...] += part

            @pl.when(kk == nk - 1)
            def _():
                finish(acc_ref[...])

    args = (a, b) + ((res,) if res is not None else ())
    return pl.pallas_call(
        body, name=name, grid=(m // tm, n // tn, nk), in_specs=in_specs, out_specs=o_spec,
        out_shape=jax.ShapeDtypeStruct((m, n), out_dtype),
        scratch_shapes=[pltpu.VMEM((tm, tn) if nk > 1 else (8, LANES), F32)],
        compiler_params=_params(("parallel", "parallel", "arbitrary")),
    )(*args)


def _norm_mm(x, g, w, out_dtype, tm=1024, tn=1536, name="norm_mm"):
    m, d = x.shape
    n = w.shape[1]
    tm, tn = _pick(m, tm), _pick(n, tn)

    def body(x_ref, g_ref, w_ref, o_ref, hn_ref):
        @pl.when(pl.program_id(1) == 0)
        def _():
            xf = x_ref[...]
            r = lax.rsqrt(jnp.mean(xf * xf, axis=-1, keepdims=True) + NORM_EPS)
            hn_ref[...] = (xf * r * g_ref[...]).astype(BF16)

        o_ref[...] = _dot(hn_ref[...], w_ref[...], _NN).astype(out_dtype)

    return pl.pallas_call(
        body, name=name, grid=(m // tm, n // tn),
        in_specs=[pl.BlockSpec((tm, d), lambda i, j: (i, 0)),
                  pl.BlockSpec((1, d), lambda i, j: (0, 0)),
                  pl.BlockSpec((d, tn), lambda i, j: (0, j))],
        out_specs=[pl.BlockSpec((tm, tn), lambda i, j: (i, j)),
                   pl.BlockSpec((tm, d), lambda i, j: (i, 0))],
        out_shape=[jax.ShapeDtypeStruct((m, n), out_dtype), jax.ShapeDtypeStruct((m, d), BF16)],
        compiler_params=_params(("parallel", "arbitrary")),
    )(x, g.reshape(1, d), w)


def _norm_bwd(dhn, x, g, dx_in, tm=256, name="norm_bwd"):
    m, d = x.shape
    tm = min(tm, m)

    def body(dhn_ref, x_ref, g_ref, dxi_ref, dx_ref, gg_ref):
        xf = x_ref[...]
        r = lax.rsqrt(jnp.mean(xf * xf, axis=-1, keepdims=True) + NORM_EPS)
        xhat = xf * r
        dh = dhn_ref[...].astype(F32)
        dxn = dh * g_ref[...]
        mean = jnp.mean(dxn * xhat, axis=-1, keepdims=True)
        dx_ref[...] = dxi_ref[...] + r * (dxn - xhat * mean)
        part = jnp.sum(dh * xhat, axis=0, keepdims=True)

        @pl.when(pl.program_id(0) == 0)
        def _():
            gg_ref[...] = part

        @pl.when(pl.program_id(0) > 0)
        def _():
            gg_ref[...] += part

    row = pl.BlockSpec((tm, d), lambda i: (i, 0))
    vec = pl.BlockSpec((1, d), lambda i: (0, 0))
    dx, gg = pl.pallas_call(
        body, name=name, grid=(m // tm,), in_specs=[row, row, vec, row], out_specs=[row, vec],
        out_shape=[jax.ShapeDtypeStruct((m, d), F32), jax.ShapeDtypeStruct((1, d), F32)],
        compiler_params=_params(("arbitrary",)),
    )(dhn, x, g.reshape(1, d), dx_in)
    return dx, gg.reshape(d)


def _down(u, k):
    s = u.shape[0]
    rows = lax.broadcasted_iota(jnp.int32, u.shape, 0)
    return jnp.where(rows < k, 0.0, pltpu.roll(u, k, axis=0))


def _up(u, k):
    s = u.shape[0]
    rows = lax.broadcasted_iota(jnp.int32, u.shape, 0)
    return jnp.where(rows >= s - k, 0.0, pltpu.roll(u, s - k, axis=0))


def _conv_mix_fwd(proj, conv_w, tc=128):
    s = proj.shape[0]
    nb = WIDTH // tc
    off = CONV_OFF // tc

    def body(b_ref, c_ref, h_ref, w_ref, o_ref):
        u = c_ref[...].astype(F32) * h_ref[...].astype(F32)
        w = w_ref[...]
        cv = w[0:1] * _down(u, 2) + w[1:2] * _down(u, 1) + w[2:3] * u
        o_ref[...] = (b_ref[...].astype(F32) * cv).astype(BF16)

    col = lambda k: pl.BlockSpec((s, tc), lambda j, k=k: (0, off + k * nb + j))
    return pl.pallas_call(
        body, name="conv_mix_fwd", grid=(nb,),
        in_specs=[col(0), col(1), col(2), pl.BlockSpec((3, tc), lambda j: (0, j))],
        out_specs=pl.BlockSpec((s, tc), lambda j: (0, j)),
        out_shape=jax.ShapeDtypeStruct((s, WIDTH), BF16),
        compiler_params=_params(("parallel",)),
    )(proj, proj, proj, conv_w)


def _conv_mix_bwd(do, proj, conv_w, tc=128):
    s = proj.shape[0]
    nb = WIDTH // tc
    off = CONV_OFF // tc

    def body(do_ref, b_ref, c_ref, h_ref, w_ref, db_ref, dc_ref, dh_ref, gw_ref):
        b = b_ref[...].astype(F32)
        c = c_ref[...].astype(F32)
        h = h_ref[...].astype(F32)
        g = do_ref[...].astype(F32)
        w = w_ref[...]
        u = c * h
        u1, u2 = _down(u, 1), _down(u, 2)
        cv = w[0:1] * u2 + w[1:2] * u1 + w[2:3] * u
        db_ref[...] = (g * cv).astype(BF16)
        dcv = g * b
        gw_ref[0:1, :] = jnp.sum(dcv * u2, axis=0, keepdims=True)
        gw_ref[1:2, :] = jnp.sum(dcv * u1, axis=0, keepdims=True)
        gw_ref[2:3, :] = jnp.sum(dcv * u, axis=0, keepdims=True)
        du = w[2:3] * dcv + w[1:2] * _up(dcv, 1) + w[0:1] * _up(dcv, 2)
        dc_ref[...] = (du * h).astype(BF16)
        dh_ref[...] = (du * c).astype(BF16)

    col = lambda k: pl.BlockSpec((s, tc), lambda j, k=k: (0, off + k * nb + j))
    own = pl.BlockSpec((s, tc), lambda j: (0, j))
    wsp = pl.BlockSpec((3, tc), lambda j: (0, j))
    act = jax.ShapeDtypeStruct((s, WIDTH), BF16)
    return pl.pallas_call(
        body, name="conv_mix_bwd", grid=(nb,),
        in_specs=[own, col(0), col(1), col(2), wsp], out_specs=[own, own, own, wsp],
        out_shape=[act, act, act, jax.ShapeDtypeStruct((3, WIDTH), F32)],
        compiler_params=_params(("parallel",)),
    )(do, proj, proj, proj, conv_w)


def _glu_fwd(up, w, b, tc=256):
    s = up.shape[0]
    nb = D_FF // tc

    def body(g_ref, v_ref, w_ref, b_ref, o_ref):
        ug = g_ref[...].astype(F32)
        wv = w_ref[...]
        cg = wv[0:1] * _down(ug, 2) + wv[1:2] * _down(ug, 1) + wv[2:3] * ug + b_ref[...]
        act = cg * jax.nn.sigmoid(cg)
        o_ref[...] = (act * v_ref[...].astype(F32)).astype(BF16)

    return pl.pallas_call(
        body, name="glu_fwd", grid=(nb,),
        in_specs=[pl.BlockSpec((s, tc), lambda j: (0, j)), pl.BlockSpec((s, tc), lambda j: (0, nb + j)),
                  pl.BlockSpec((3, tc), lambda j: (0, j)), pl.BlockSpec((1, tc), lambda j: (0, j))],
        out_specs=pl.BlockSpec((s, tc), lambda j: (0, j)),
        out_shape=jax.ShapeDtypeStruct((s, D_FF), BF16),
        compiler_params=_params(("parallel",)),
    )(up, up, w, b.reshape(1, D_FF))


def _glu_bwd(dh, up, w, b, tc=256):
    s = up.shape[0]
    nb = D_FF // tc

    def body(dh_ref, g_ref, v_ref, w_ref, b_ref, dg_ref, dv_ref, gw_ref, gb_ref):
        ug = g_ref[...].astype(F32)
        uv = v_ref[...].astype(F32)
        d = dh_ref[...].astype(F32)
        wv = w_ref[...]
        u1, u2 = _down(ug, 1), _down(ug, 2)
        cg = wv[0:1] * u2 + wv[1:2] * u1 + wv[2:3] * ug + b_ref[...]
        sg = jax.nn.sigmoid(cg)
        dv_ref[...] = (d * (cg * sg)).astype(BF16)
        dcg = d * uv * (sg * (1.0 + cg * (1.0 - sg)))
        gb_ref[...] = jnp.sum(dcg, axis=0, keepdims=True)
        gw_ref[0:1, :] = jnp.sum(dcg * u2, axis=0, keepdims=True)
        gw_ref[1:2, :] = jnp.sum(dcg * u1, axis=0, keepdims=True)
        gw_ref[2:3, :] = jnp.sum(dcg * ug, axis=0, keepdims=True)
        dg_ref[...] = (wv[2:3] * dcg + wv[1:2] * _up(dcg, 1) + wv[0:1] * _up(dcg, 2)).astype(BF16)

    own = pl.BlockSpec((s, tc), lambda j: (0, j))
    wsp = pl.BlockSpec((3, tc), lambda j: (0, j))
    bsp = pl.BlockSpec((1, tc), lambda j: (0, j))
    act = jax.ShapeDtypeStruct((s, D_FF), BF16)
    dg, dv, gw, gb = pl.pallas_call(
        body, name="glu_bwd", grid=(nb,),
        in_specs=[own, own, pl.BlockSpec((s, tc), lambda j: (0, nb + j)), wsp, bsp],
        out_specs=[own, own, wsp, bsp],
        out_shape=[act, act, jax.ShapeDtypeStruct((3, D_FF), F32), jax.ShapeDtypeStruct((1, D_FF), F32)],
        compiler_params=_params(("parallel",)),
    )(dh, up, up, w, b.reshape(1, D_FF))
    return dg, dv, gw, gb.reshape(D_FF)


def _merge_fwd(x, oc, of, osb, wpc, wpf, wps, proj, gb, wout, tm=256):
    s, d = x.shape
    tm = min(tm, s)

    def body(x_ref, oc_ref, of_ref, os_ref, wpc_ref, wpf_ref, wps_ref, g0_ref, g1_ref, g2_ref, gb_ref, wo_ref,
             xo_ref, mg_ref, yc_ref, yf_ref, ys_ref):
        merged = jnp.zeros((tm, d), F32)
        for k, (o_ref, w_ref, g_ref, y_ref) in enumerate(
                ((oc_ref, wpc_ref, g0_ref, yc_ref), (of_ref, wpf_ref, g1_ref, yf_ref), (os_ref, wps_ref, g2_ref, ys_ref))):
            y = _dot(o_ref[...].astype(BF16), w_ref[...], _NN)
            y_ref[...] = y.astype(BF16)
            gate = jax.nn.sigmoid(g_ref[...].astype(F32) + gb_ref[:, k * d:(k + 1) * d])
            merged = merged + gate * y
        mb = merged.astype(BF16)
        mg_ref[...] = mb
        xo_ref[...] = x_ref[...] + _dot(mb, wo_ref[...], _NN)

    rowd = pl.BlockSpec((tm, d), lambda i: (i, 0))
    roww = pl.BlockSpec((tm, WIDTH), lambda i: (i, 0))
    wp = pl.BlockSpec((WIDTH, d), lambda i: (0, 0))
    gcol = lambda k: pl.BlockSpec((tm, d), lambda i, k=k: (i, GATE_OFF // d + k))
    actd = jax.ShapeDtypeStruct((s, d), BF16)
    return pl.pallas_call(
        body, name="merge_fwd", grid=(s // tm,),
        in_specs=[rowd, roww, roww, roww, wp, wp, wp, gcol(0), gcol(1), gcol(2),
                  pl.BlockSpec((1, 3 * d), lambda i: (0, 0)), pl.BlockSpec((d, d), lambda i: (0, 0))],
        out_specs=[rowd, rowd, rowd, rowd, rowd],
        out_shape=[jax.ShapeDtypeStruct((s, d), F32), actd, actd, actd, actd],
        compiler_params=_params(("parallel",)),
    )(x, oc, of, osb, wpc, wpf, wps, proj, proj, proj, gb.reshape(1, 3 * d), wout)


def _gate_bwd(dm, yc, yf, ys, proj, gb, tm=256):
    s, d = dm.shape
    tm = min(tm, s)

    def body(dm_ref, yc_ref, yf_ref, ys_ref, g0_ref, g1_ref, g2_ref, gb_ref, dyc_ref, dyf_ref, dys_ref, dgl_ref, ggb_ref):
        g = dm_ref[...].astype(F32)
        parts = []
        for k, (y_ref, g_ref, dy_ref) in enumerate(((yc_ref, g0_ref, dyc_ref), (yf_ref, g1_ref, dyf_ref), (ys_ref, g2_ref, dys_ref))):
            gate = jax.nn.sigmoid(g_ref[...].astype(F32) + gb_ref[:, k * d:(k + 1) * d])
            dy_ref[...] = (g * gate).astype(BF16)
            dgl = g * y_ref[...].astype(F32) * gate * (1.0 - gate)
            dgl_ref[:, k * d:(k + 1) * d] = dgl.astype(BF16)
            parts.append(jnp.sum(dgl, axis=0, keepdims=True))
        part = jnp.concatenate(parts, axis=1)

        @pl.when(pl.program_id(0) == 0)
        def _():
            ggb_ref[...] = part

        @pl.when(pl.program_id(0) > 0)
        def _():
            ggb_ref[...] += part

    rowd = pl.BlockSpec((tm, d), lambda i: (i, 0))
    gcol = lambda k: pl.BlockSpec((tm, d), lambda i, k=k: (i, GATE_OFF // d + k))
    vec = pl.BlockSpec((1, 3 * d), lambda i: (0, 0))
    actd = jax.ShapeDtypeStruct((s, d), BF16)
    dyc, dyf, dys, dgl, ggb = pl.pallas_call(
        body, name="gate_bwd", grid=(s // tm,),
        in_specs=[rowd, rowd, rowd, rowd, gcol(0), gcol(1), gcol(2), vec],
        out_specs=[rowd, rowd, rowd, pl.BlockSpec((tm, 3 * d), lambda i: (i, 0)), vec],
        out_shape=[actd, actd, actd, jax.ShapeDtypeStruct((s, 3 * d), BF16), jax.ShapeDtypeStruct((1, 3 * d), F32)],
        compiler_params=_params(("arbitrary",)),
    )(dm, yc, yf, ys, proj, proj, proj, gb.reshape(1, 3 * d))
    return dyc, dyf, dys, dgl, ggb.reshape(3 * d)


def _tri_tables(nq, r, order):
    last = lambda qi: (qi + 1) * r - 1
    if order == "k_outer":
        pairs = [(qi, kj) for kj in range(nq * r) for qi in range(kj // r, nq)]
    elif order == "k_desc":
        pairs = [(qi, kj) for qi in range(nq) for kj in range(last(qi), -1, -1)]
    else:
        pairs = [(qi, kj) for qi in range(nq) for kj in range(last(qi) + 1)]
    qs, ks = zip(*pairs)
    return jnp.asarray(np.array(qs, np.int32)), jnp.asarray(np.array(ks, np.int32)), len(pairs)


def _lo_mask(shape):
    return lax.broadcasted_iota(jnp.int32, shape, len(shape) - 1) < HEAD_DIM


def _head(x, h):
    lo = _lo_mask(x.shape)
    return jnp.where(lo if h == 0 else jnp.logical_not(lo), x, jnp.zeros_like(x))


def _pair_cols(x, h):
    return x[:, h * HEAD_DIM:h * HEAD_DIM + 1]


def _rep(a0, a1, shape):
    return jnp.where(_lo_mask(shape), a0, a1)


def _positions(qi, kj, tq, tk):
    row = qi * tq + lax.broadcasted_iota(jnp.int32, (tq, tk), 0)
    col = kj * tk + lax.broadcasted_iota(jnp.int32, (tq, tk), 1)
    return row, col


def _head_norm(x, g):
    lo = _lo_mask(x.shape)
    sq = x * x
    s0 = jnp.sum(jnp.where(lo, sq, 0.0), axis=-1, keepdims=True)
    s1 = jnp.sum(jnp.where(lo, 0.0, sq), axis=-1, keepdims=True)
    r = jnp.where(lo, lax.rsqrt(s0 / HEAD_DIM + NORM_EPS), lax.rsqrt(s1 / HEAD_DIM + NORM_EPS))
    return x * r, r


def _fox_prep(proj, gq, gk, tm=512):
    s = proj.shape[0]
    tm = min(tm, s)
    off = FOX_OFF // LANES

    def body(q_ref, k_ref, gq_ref, gk_ref, qn_ref, kn_ref):
        qh, _ = _head_norm(q_ref[...].astype(F32), None)
        kh, _ = _head_norm(k_ref[...].astype(F32), None)
        qn_ref[...] = (qh * gq_ref[...] * QK_SCALE).astype(BF16)
        kn_ref[...] = (kh * gk_ref[...]).astype(BF16)

    vec = pl.BlockSpec((1, LANES), lambda p, i: (0, 0))
    own = pl.BlockSpec((tm, LANES), lambda p, i: (i, p))
    act = jax.ShapeDtypeStruct((s, WIDTH), BF16)
    return pl.pallas_call(
        body, name="fox_prep", grid=(N_PAIR, s // tm),
        in_specs=[pl.BlockSpec((tm, LANES), lambda p, i: (i, off + p)),
                  pl.BlockSpec((tm, LANES), lambda p, i: (i, off + N_PAIR + p)), vec, vec],
        out_specs=[own, own], out_shape=[act, act],
        compiler_params=_params(("parallel", "parallel")),
    )(proj, proj, jnp.tile(gq, 2).reshape(1, LANES), jnp.tile(gk, 2).reshape(1, LANES))


def _fox_post(dqs, dkn, proj, gq, gk, tm=512):
    s = proj.shape[0]
    tm = min(tm, s)
    off = FOX_OFF // LANES

    def one(d_ref, x_ref, g_ref, scale, dx_ref, gg_ref, first):
        xhat, r = _head_norm(x_ref[...].astype(F32), None)
        dy = d_ref[...] * scale
        part = jnp.sum(dy * xhat, axis=0, keepdims=True)

        @pl.when(first)
        def _():
            gg_ref[...] = part

        @pl.when(jnp.logical_not(first))
        def _():
            gg_ref[...] += part

        dxh = dy * g_ref[...]
        lo = _lo_mask(dxh.shape)
        pr = dxh * xhat
        m0 = jnp.sum(jnp.where(lo, pr, 0.0), axis=-1, keepdims=True)
        m1 = jnp.sum(jnp.where(lo, 0.0, pr), axis=-1, keepdims=True)
        mean = jnp.where(lo, m0, m1) / HEAD_DIM
        dx_ref[...] = (r * (dxh - xhat * mean)).astype(BF16)

    def body(dq_ref, dk_ref, q_ref, k_ref, gq_ref, gk_ref, dxq_ref, dxk_ref, ggq_ref, ggk_ref):
        first = pl.program_id(1) == 0
        one(dq_ref, q_ref, gq_ref, QK_SCALE, dxq_ref, ggq_ref, first)
        one(dk_ref, k_ref, gk_ref, 1.0, dxk_ref, ggk_ref, first)

    vec = pl.BlockSpec((1, LANES), lambda p, i: (0, 0))
    own = pl.BlockSpec((tm, LANES), lambda p, i: (i, p))
    ggs = pl.BlockSpec((None, 1, LANES), lambda p, i: (p, 0, 0))
    act = jax.ShapeDtypeStruct((s, WIDTH), BF16)
    ggo = jax.ShapeDtypeStruct((N_PAIR, 1, LANES), F32)
    dxq, dxk, ggq, ggk = pl.pallas_call(
        body, name="fox_post", grid=(N_PAIR, s // tm),
        in_specs=[own, own, pl.BlockSpec((tm, LANES), lambda p, i: (i, off + p)),
                  pl.BlockSpec((tm, LANES), lambda p, i: (i, off + N_PAIR + p)), vec, vec],
        out_specs=[own, own, ggs, ggs], out_shape=[act, act, ggo, ggo],
        compiler_params=_params(("parallel", "arbitrary")),
    )(dqs, dkn, proj, proj, jnp.tile(gq, 2).reshape(1, LANES), jnp.tile(gk, 2).reshape(1, LANES))
    fold = lambda a: a.reshape(N_HEADS, HEAD_DIM).sum(axis=0)
    return dxq, dxk, fold(ggq), fold(ggk)


def _split3(x):
    a = x.astype(BF16)
    r = x - a.astype(F32)
    b = r.astype(BF16)
    c = (r - b.astype(F32)).astype(BF16)
    return a, b, c


def _split2(x):
    a = x.astype(BF16)
    b = (x - a.astype(F32)).astype(BF16)
    return a, b


def _log_sigmoid(x):
    return jnp.minimum(x, 0.0) - jnp.log(1.0 + jnp.exp(-jnp.abs(x)))


def _fox_gates(ft, bias):
    h, s = ft.shape
    nb = s // LANES

    def body(f_ref, b_ref, c_ref):
        lf = _log_sigmoid(f_ref[...] + b_ref[...])
        i = lax.broadcasted_iota(jnp.int32, (s, LANES), 0)
        j = pl.program_id(0) * LANES + lax.broadcasted_iota(jnp.int32, (s, LANES), 1)
        tri = jnp.where(i <= j, 1.0, 0.0).astype(BF16)
        c_ref[...] = sum(_dot(p, tri, _NN) for p in _split3(lf))

    return pl.pallas_call(
        body, name="fox_gates", grid=(nb,),
        in_specs=[pl.BlockSpec((h, s), lambda j: (0, 0)), pl.BlockSpec((h, 1), lambda j: (0, 0))],
        out_specs=pl.BlockSpec((h, LANES), lambda j: (0, j)),
        out_shape=jax.ShapeDtypeStruct((h, s), F32),
        compiler_params=_params(("parallel",)),
    )(ft, bias.reshape(h, 1))


def _fox_gates_bwd(dc, ft, bias):
    h, s = ft.shape
    nb = s // LANES

    def body(dc_ref, f_ref, fb_ref, b_ref, df_ref, gb_ref):
        i = lax.broadcasted_iota(jnp.int32, (s, LANES), 0)
        j = pl.program_id(0) * LANES + lax.broadcasted_iota(jnp.int32, (s, LANES), 1)
        tri = jnp.where(i >= j, 1.0, 0.0).astype(BF16)
        dlf = sum(_dot(p, tri, _NN) for p in _split3(dc_ref[...]))
        df = dlf * jax.nn.sigmoid(-(fb_ref[...] + b_ref[...]))
        df_ref[...] = df
        part = jnp.sum(df, axis=-1, keepdims=True)

        @pl.when(pl.program_id(0) == 0)
        def _():
            gb_ref[...] = part

        @pl.when(pl.program_id(0) > 0)
        def _():
            gb_ref[...] += part

    full = pl.BlockSpec((h, s), lambda j: (0, 0))
    blk = pl.BlockSpec((h, LANES), lambda j: (0, j))
    one = pl.BlockSpec((h, 1), lambda j: (0, 0))
    df, gb = pl.pallas_call(
        body, name="fox_gates_bwd", grid=(nb,), in_specs=[full, full, blk, one], out_specs=[blk, one],
        out_shape=[jax.ShapeDtypeStruct((h, s), F32), jax.ShapeDtypeStruct((h, 1), F32)],
        compiler_params=_params(("arbitrary",)),
    )(dc, ft, ft, bias.reshape(h, 1))
    return df, gb.reshape(h)


def _delta_rep(do, o, tm=512):
    s = do.shape[0]
    tm = min(tm, s)

    def body(do_ref, o_ref, d_ref):
        pr = do_ref[...].astype(F32) * o_ref[...].astype(F32)
        lo = _lo_mask(pr.shape)
        d0 = jnp.sum(jnp.where(lo, pr, 0.0), axis=-1, keepdims=True)
        d1 = jnp.sum(jnp.where(lo, 0.0, pr), axis=-1, keepdims=True)
        d_ref[...] = jnp.where(lo, d0, d1)

    own = pl.BlockSpec((tm, LANES), lambda p, i: (i, p))
    return pl.pallas_call(
        body, name="delta_rep", grid=(N_PAIR, s // tm), in_specs=[own, own],
        out_specs=pl.BlockSpec((None, tm, LANES), lambda p, i: (p, i, 0)),
        out_shape=jax.ShapeDtypeStruct((N_PAIR, s, LANES), F32),
        compiler_params=_params(("parallel", "parallel")),
    )(do, o)


def _tile(s, t):
    t = min(t, s)
    assert s % t == 0
    return t


def _fox_fwd(qn, kn, proj, ccol, crow, t=512, comm=None):
    s = qn.shape[0]
    t = _tile(s, t)
    n = s // t
    qtab, ktab, ntri = _tri_tables(n, 1, "k_asc")
    voff = FOX_OFF // LANES + 2 * N_PAIR
    n_in, n_out = (len(comm.inputs), len(comm.out_shapes)) if comm else (0, 0)

    def body(qt_ref, kt_ref, q_ref, k_ref, v_ref, cc_ref, cr_ref, *rest):
        comm_in, rest = rest[:n_in], rest[n_in:]
        (o_ref, lse_ref), rest = rest[:2], rest[2:]
        comm_out, rest = rest[:n_out], rest[n_out:]
        (m_ref, l_ref, acc_ref), sems = rest[:3], rest[3:]
        i = pl.program_id(1)
        qi, kj = qt_ref[i], kt_ref[i]
        if comm:
            pl.when((pl.program_id(0) == 0) & (i == 0))(lambda: comm.start(comm_in, comm_out, sems))

        @pl.when(kj == 0)
        def _():
            m_ref[...] = jnp.full(m_ref.shape, NEG_INF, F32)
            l_ref[...] = jnp.zeros(l_ref.shape, F32)
            acc_ref[...] = jnp.zeros(acc_ref.shape, F32)

        q, k, v = q_ref[...], k_ref[...], v_ref[...]
        row, col = _positions(qi, kj, t, t)
        causal = col <= row
        m_old = m_ref[...]
        mn, rs, pv = [], [], []
        for h in range(2):
            sc = _dot(_head(q, h), k, _NT) + _pair_cols(cc_ref[...], h) - cr_ref[h:h + 1, :]
            sc = jnp.where(causal, sc, NEG_INF)
            m_new = jnp.maximum(_pair_cols(m_old, h), jnp.max(sc, axis=-1, keepdims=True))
            p = jnp.exp(sc - m_new)
            mn.append(m_new)
            rs.append(jnp.sum(p, axis=-1, keepdims=True))
            pv.append(_dot(p.astype(BF16), _head(v, h), _NN))
        m_rep = _rep(mn[0], mn[1], m_old.shape)
        alpha = jnp.exp(m_old - m_rep)
        l_ref[...] = alpha * l_ref[...] + _rep(rs[0], rs[1], m_old.shape)
        acc_ref[...] = alpha * acc_ref[...] + pv[0] + pv[1]
        m_ref[...] = m_rep

        @pl.when(kj == qi)
        def _():
            o_ref[...] = acc_ref[...] / l_ref[...]
            lse_ref[...] = m_ref[...] + jnp.log(l_ref[...])

        if comm:
            pl.when((pl.program_id(0) == N_PAIR - 1) & (i == ntri - 1))(lambda: comm.finish(comm_in, comm_out, sems))

    grid_spec = pltpu.PrefetchScalarGridSpec(
        num_scalar_prefetch=2, grid=(N_PAIR, ntri),
        in_specs=[pl.BlockSpec((t, LANES), lambda p, i, qt, kt: (qt[i], p)),
                  pl.BlockSpec((t, LANES), lambda p, i, qt, kt: (kt[i], p)),
                  pl.BlockSpec((t, LANES), lambda p, i, qt, kt: (kt[i], voff + p)),
                  pl.BlockSpec((None, t, LANES), lambda p, i, qt, kt: (p, qt[i], 0)),
                  pl.BlockSpec((None, 2, t), lambda p, i, qt, kt: (p, 0, kt[i]))] + [ANY] * n_in,
        out_specs=[pl.BlockSpec((t, LANES), lambda p, i, qt, kt: (qt[i], p)),
                   pl.BlockSpec((None, t, LANES), lambda p, i, qt, kt: (p, qt[i], 0))] + [ANY] * n_out,
        scratch_shapes=[pltpu.VMEM((t, LANES), F32)] * 3 + (comm.sems if comm else []))
    outs = pl.pallas_call(
        body, name="fox_fwd", grid_spec=grid_spec,
        out_shape=[jax.ShapeDtypeStruct((s, WIDTH), F32), jax.ShapeDtypeStruct((N_PAIR, s, LANES), F32)]
        + (comm.out_shapes if comm else []),
        compiler_params=_params(("arbitrary", "arbitrary") if comm else ("parallel", "arbitrary")),
    )(qtab, ktab, qn, kn, proj, ccol, crow, *(comm.inputs if comm else []))
    return outs[:2], outs[2:]


def _fox_bwd(qn, kn, proj, do, lse, delta, ccol, crow, t=512, comm=None):
    s = qn.shape[0]
    t = _tile(s, t)
    n = s // t
    qtab, ktab, ntri = _tri_tables(n, 1, "k_outer")
    voff = FOX_OFF // LANES + 2 * N_PAIR
    n_in, n_out = (len(comm.inputs), len(comm.out_shapes)) if comm else (0, 0)

    def body(qt_ref, kt_ref, q_ref, k_ref, v_ref, do_ref, lse_ref, dl_ref, cc_ref, cr_ref, *rest):
        comm_in, rest = rest[:n_in], rest[n_in:]
        (dq_ref, dk_ref, dv_ref, dc_ref, dcq_ref), rest = rest[:5], rest[5:]
        comm_out, rest = rest[:n_out], rest[n_out:]
        (dka_ref, dva_ref, dca_ref, dcqa_ref), sems = rest[:4], rest[4:]
        i = pl.program_id(1)
        qi, kj = qt_ref[i], kt_ref[i]
        if comm:
            pl.when((pl.program_id(0) == 0) & (i == 0))(lambda: comm.start(comm_in, comm_out, sems))

        @pl.when(i == 0)
        def _():
            dq_ref[...] = jnp.zeros(dq_ref.shape, F32)
            dcqa_ref[...] = jnp.zeros(dcqa_ref.shape, F32)

        @pl.when(qi == kj)
        def _():
            dka_ref[...] = jnp.zeros(dka_ref.shape, F32)
            dva_ref[...] = jnp.zeros(dva_ref.shape, F32)
            dca_ref[...] = jnp.zeros(dca_ref.shape, F32)

        q, k, v, g = q_ref[...], k_ref[...], v_ref[...], do_ref[...]
        row, col = _positions(qi, kj, t, t)
        causal = col <= row
        dq = jnp.zeros((t, LANES), F32)
        dk = jnp.zeros((t, LANES), F32)
        dv = jnp.zeros((t, LANES), F32)
        rowsum = []
        for h in range(2):
            qh, gh = _head(q, h), _head(g, h)
            sc = _dot(qh, k, _NT) + _pair_cols(cc_ref[...], h) - cr_ref[h:h + 1, :]
            p = jnp.where(causal, jnp.exp(sc - _pair_cols(lse_ref[...], h)), 0.0)
            dp = _dot(gh, v, _NT)
            ds = p * (dp - _pair_cols(dl_ref[...], h))
            dsb = ds.astype(BF16)
            dv = dv + _dot(p.astype(BF16), gh, _TN)
            dk = dk + _dot(dsb, qh, _TN)
            dq = dq + _dot(dsb, _head(k, h), _NN)
            dca_ref[h:h + 1, :] -= jnp.sum(ds, axis=0, keepdims=True)
            rowsum.append(jnp.sum(ds, axis=-1, keepdims=True))
        dka_ref[...] += dk
        dva_ref[...] += dv
        rows = pl.ds(pl.multiple_of(qi * t, t), t)
        dq_ref[rows, :] += dq
        dcqa_ref[rows, :] += _rep(rowsum[0], rowsum[1], (t, LANES))

        @pl.when(qi == n - 1)
        def _():
            dk_ref[...] = dka_ref[...]
            dv_ref[...] = dva_ref[...].astype(BF16)
            dc_ref[...] = dca_ref[...]

        @pl.when(i == ntri - 1)
        def _():
            across = dcqa_ref[...].T
            dcq_ref[0:1, :] = across[0:1, :]
            dcq_ref[1:2, :] = across[HEAD_DIM:HEAD_DIM + 1, :]

        if comm:
            pl.when((pl.program_id(0) == N_PAIR - 1) & (i == ntri - 1))(lambda: comm.finish(comm_in, comm_out, sems))

    qblk = lambda p, i, qt, kt: (qt[i], p)
    kblk = lambda p, i, qt, kt: (kt[i], p)
    qrep = pl.BlockSpec((None, t, LANES), lambda p, i, qt, kt: (p, qt[i], 0))
    crs = pl.BlockSpec((None, 2, t), lambda p, i, qt, kt: (p, 0, kt[i]))
    grid_spec = pltpu.PrefetchScalarGridSpec(
        num_scalar_prefetch=2, grid=(N_PAIR, ntri),
        in_specs=[pl.BlockSpec((t, LANES), qblk), pl.BlockSpec((t, LANES), kblk),
                  pl.BlockSpec((t, LANES), lambda p, i, qt, kt: (kt[i], voff + p)),
                  pl.BlockSpec((t, LANES), qblk), qrep, qrep, qrep, crs] + [ANY] * n_in,
        out_specs=[pl.BlockSpec((s, LANES), lambda p, i, qt, kt: (0, p)),
                   pl.BlockSpec((t, LANES), kblk), pl.BlockSpec((t, LANES), kblk), crs,
                   pl.BlockSpec((None, 2, s), lambda p, i, qt, kt: (p, 0, 0))] + [ANY] * n_out,
        scratch_shapes=[pltpu.VMEM((t, LANES), F32), pltpu.VMEM((t, LANES), F32), pltpu.VMEM((2, t), F32),
                        pltpu.VMEM((s, LANES), F32)] + (comm.sems if comm else []))
    outs = pl.pallas_call(
        body, name="fox_bwd", grid_spec=grid_spec,
        out_shape=[jax.ShapeDtypeStruct((s, WIDTH), F32), jax.ShapeDtypeStruct((s, WIDTH), F32),
                   jax.ShapeDtypeStruct((s, WIDTH), BF16), jax.ShapeDtypeStruct((N_PAIR, 2, s), F32),
                   jax.ShapeDtypeStruct((N_PAIR, 2, s), F32)] + (comm.out_shapes if comm else []),
        compiler_params=_params(("arbitrary", "arbitrary") if comm else ("parallel", "arbitrary")),
    )(qtab, ktab, qn, kn, proj, do, lse, delta, ccol, crow, *(comm.inputs if comm else []))
    return outs[:5], outs[5:]


def _sb_tile(qh, k, strict, tk, r_col):
    z = _dot(qh, k, _NT)
    lg = jnp.where(strict, -(jnp.maximum(z, 0.0) + jnp.log(1.0 + jnp.exp(-jnp.abs(z)))), 0.0)
    jj = lax.broadcasted_iota(jnp.int32, (tk, tk), 0)
    ss = lax.broadcasted_iota(jnp.int32, (tk, tk), 1)
    above = jnp.where(jj > ss, 1.0, 0.0).astype(BF16)
    suffix = sum(_dot(p, above, _NN) for p in _split2(lg)) + r_col
    a = jnp.where(strict, jnp.exp(lg + z + suffix), 0.0)
    return z, lg, a


def _sb_fwd(proj, tq=512, tk=256, comm=None):
    s = proj.shape[0]
    tq = _tile(s, tq)
    tk = _tile(tq, tk)
    nq, r = s // tq, tq // tk
    qtab, ktab, ntri = _tri_tables(nq, r, "k_desc")
    off = SB_OFF // LANES
    n_in, n_out = (len(comm.inputs), len(comm.out_shapes)) if comm else (0, 0)

    def body(qt_ref, kt_ref, q_ref, k_ref, v_ref, *rest):
        comm_in, o_ref, rest = rest[:n_in], rest[n_in], rest[n_in + 1:]
        comm_out, rest = rest[:n_out], rest[n_out:]
        (acc_ref, r_ref), sems = rest[:2], rest[2:]
        i = pl.program_id(1)
        qi, kj = qt_ref[i], kt_ref[i]
        if comm:
            pl.when((pl.program_id(0) == 0) & (i == 0))(lambda: comm.start(comm_in, comm_out, sems))

        @pl.when(kj == (qi + 1) * r - 1)
        def _():
            acc_ref[...] = jnp.zeros(acc_ref.shape, F32)
            r_ref[...] = jnp.zeros(r_ref.shape, F32)

        q = q_ref[...] * QK_SCALE
        k, v = k_ref[...], v_ref[...]
        row, col = _positions(qi, kj, tq, tk)
        strict = col < row
        acc = acc_ref[...]
        for h in range(2):
            _, lg, a = _sb_tile(_head(q, h), k, strict, tk, r_ref[h])
            acc = acc + _dot(a.astype(BF16), _head(v, h), _NN)
            r_ref[h] += jnp.sum(lg, axis=-1, keepdims=True)
        acc_ref[...] = acc

        @pl.when(kj == 0)
        def _():
            o_ref[...] = acc_ref[...]

        if comm:
            pl.when((pl.program_id(0) == N_PAIR - 1) & (i == ntri - 1))(lambda: comm.finish(comm_in, comm_out, sems))

    grid_spec = pltpu.PrefetchScalarGridSpec(
        num_scalar_prefetch=2, grid=(N_PAIR, ntri),
        in_specs=[pl.BlockSpec((tq, LANES), lambda p, i, qt, kt: (qt[i], off + p)),
                  pl.BlockSpec((tk, LANES), lambda p, i, qt, kt: (kt[i], off + N_PAIR + p)),
                  pl.BlockSpec((tk, LANES), lambda p, i, qt, kt: (kt[i], off + 2 * N_PAIR + p))] + [ANY] * n_in,
        out_specs=[pl.BlockSpec((tq, LANES), lambda p, i, qt, kt: (qt[i], p))] + [ANY] * n_out,
        scratch_shapes=[pltpu.VMEM((tq, LANES), F32), pltpu.VMEM((2, tq, 1), F32)] + (comm.sems if comm else []))
    outs = pl.pallas_call(
        body, name="sb_fwd", grid_spec=grid_spec,
        out_shape=[jax.ShapeDtypeStruct((s, WIDTH), F32)] + (comm.out_shapes if comm else []),
        compiler_params=_params(("arbitrary", "arbitrary") if comm else ("parallel", "arbitrary")),
    )(qtab, ktab, proj, proj, proj, *(comm.inputs if comm else []))
    return outs[0], outs[1:]


def _sb_bwd(proj, do, delta, tq=512, tk=256, comm=None):
    s = proj.shape[0]
    tq = _tile(s, tq)
    tk = _tile(tq, tk)
    nq, r = s // tq, tq // tk
    qtab, ktab, ntri = _tri_tables(nq, r, "k_desc")
    off = SB_OFF // LANES
    n_in, n_out = (len(comm.inputs), len(comm.out_shapes)) if comm else (0, 0)

    def body(qt_ref, kt_ref, q_ref, k_ref, v_ref, do_ref, dl_ref, *rest):
        comm_in, rest = rest[:n_in], rest[n_in:]
        (dq_ref, dk_ref, dv_ref), rest = rest[:3], rest[3:]
        comm_out, rest = rest[:n_out], rest[n_out:]
        (dqa_ref, dka_ref, dva_ref, r_ref, rd_ref), sems = rest[:5], rest[5:]
        i = pl.program_id(1)
        qi, kj = qt_ref[i], kt_ref[i]
        if comm:
            pl.when((pl.program_id(0) == 0) & (i == 0))(lambda: comm.start(comm_in, comm_out, sems))

        @pl.when(i == 0)
        def _():
            dka_ref[...] = jnp.zeros(dka_ref.shape, F32)
            dva_ref[...] = jnp.zeros(dva_ref.shape, F32)

        @pl.when(kj == (qi + 1) * r - 1)
        def _():
            dqa_ref[...] = jnp.zeros(dqa_ref.shape, F32)
            r_ref[...] = jnp.zeros(r_ref.shape, F32)
            rd_ref[...] = jnp.zeros(rd_ref.shape, F32)

        q = q_ref[...] * QK_SCALE
        k, v, g = k_ref[...], v_ref[...], do_ref[...]
        row, col = _positions(qi, kj, tq, tk)
        strict = col < row
        ss = lax.broadcasted_iota(jnp.int32, (tk, tk), 0)
        jj = lax.broadcasted_iota(jnp.int32, (tk, tk), 1)
        at_or_after = jnp.where(ss >= jj, 1.0, 0.0).astype(BF16)
        dq = jnp.zeros((tq, LANES), F32)
        dk = jnp.zeros((tk, LANES), F32)
        dv = jnp.zeros((tk, LANES), F32)
        for h in range(2):
            qh, gh = _head(q, h), _head(g, h)
            z, lg, a = _sb_tile(qh, k, strict, tk, r_ref[h])
            da = _dot(gh, v, _NT)
            ab = a.astype(BF16)
            d_a = da * ab.astype(F32)
            incl = _dot(d_a.astype(BF16), at_or_after, _NN)
            d_l = _pair_cols(dl_ref[...], h) - rd_ref[h] - incl
            sig = jnp.exp(lg + z)
            dz = jnp.where(strict, d_a * (1.0 - sig) - d_l * sig, 0.0).astype(BF16)
            dq = dq + _dot(dz, _head(k, h), _NN)
            dk = dk + _dot(dz, qh, _TN)
            dv = dv + _dot(ab, gh, _TN)
            r_ref[h] += jnp.sum(lg, axis=-1, keepdims=True)
            rd_ref[h] += jnp.sum(d_a, axis=-1, keepdims=True)
        dqa_ref[...] += dq
        rows = pl.ds(pl.multiple_of(kj * tk, tk), tk)
        dka_ref[rows, :] += dk
        dva_ref[rows, :] += dv

        @pl.when(kj == 0)
        def _():
            dq_ref[...] = (dqa_ref[...] * QK_SCALE).astype(BF16)

        @pl.when(i == ntri - 1)
        def _():
            dk_ref[...] = dka_ref[...].astype(BF16)
            dv_ref[...] = dva_ref[...].astype(BF16)

        if comm:
            pl.when((pl.program_id(0) == N_PAIR - 1) & (i == ntri - 1))(lambda: comm.finish(comm_in, comm_out, sems))

    qblk = lambda p, i, qt, kt: (qt[i], p)
    whole = pl.BlockSpec((s, LANES), lambda p, i, qt, kt: (0, p))
    grid_spec = pltpu.PrefetchScalarGridSpec(
        num_scalar_prefetch=2, grid=(N_PAIR, ntri),
        in_specs=[pl.BlockSpec((tq, LANES), lambda p, i, qt, kt: (qt[i], off + p)),
                  pl.BlockSpec((tk, LANES), lambda p, i, qt, kt: (kt[i], off + N_PAIR + p)),
                  pl.BlockSpec((tk, LANES), lambda p, i, qt, kt: (kt[i], off + 2 * N_PAIR + p)),
                  pl.BlockSpec((tq, LANES), qblk),
                  pl.BlockSpec((None, tq, LANES), lambda p, i, qt, kt: (p, qt[i], 0))] + [ANY] * n_in,
        out_specs=[pl.BlockSpec((tq, LANES), qblk), whole, whole] + [ANY] * n_out,
        scratch_shapes=[pltpu.VMEM((tq, LANES), F32), pltpu.VMEM((s, LANES), F32), pltpu.VMEM((s, LANES), F32),
                        pltpu.VMEM((2, tq, 1), F32), pltpu.VMEM((2, tq, 1), F32)] + (comm.sems if comm else []))
    act = jax.ShapeDtypeStruct((s, WIDTH), BF16)
    outs = pl.pallas_call(
        body, name="sb_bwd", grid_spec=grid_spec, out_shape=[act, act, act] + (comm.out_shapes if comm else []),
        compiler_params=_params(("arbitrary", "arbitrary") if comm else ("parallel", "arbitrary")),
    )(qtab, ktab, proj, proj, proj, do, delta, *(comm.inputs if comm else []))
    return outs[:3], outs[3:]


def _loss_head(y, target, tm=256):
    s, d = y.shape
    tm = min(tm, s)

    def body(y_ref, t_ref, l_ref, dy_ref):
        e = y_ref[...] - t_ref[...]
        dy_ref[...] = e / d
        part = jnp.sum(jnp.sum(e * e, axis=0, keepdims=True), axis=1, keepdims=True)

        @pl.when(pl.program_id(0) == 0)
        def _():
            l_ref[...] = jnp.broadcast_to(part, l_ref.shape)

        @pl.when(pl.program_id(0) > 0)
        def _():
            l_ref[...] += jnp.broadcast_to(part, l_ref.shape)

    row = pl.BlockSpec((tm, d), lambda i: (i, 0))
    return pl.pallas_call(
        body, name="loss_head", grid=(s // tm,), in_specs=[row, row],
        out_specs=[pl.BlockSpec((1, LANES), lambda i: (0, 0)), row],
        out_shape=[jax.ShapeDtypeStruct((1, LANES), F32), jax.ShapeDtypeStruct((s, d), F32)],
        compiler_params=_params(("arbitrary",)),
    )(y, target)


def _adamw(w, g, m, v, name, comm=None):
    shape = w.shape
    cols = shape[-1]
    rows = int(np.prod(shape[:-1]))
    tm = rows
    for cand in (512, 256, 128, 64, 32, 16, 8):
        if rows % cand == 0 and rows > cand and cand * cols * 4 <= ADAMW_BLOCK_BYTES:
            tm = cand
            break

    n_in, n_out = (len(comm.inputs), len(comm.out_shapes)) if comm else (0, 0)
    steps = rows // tm

    def body(w_ref, g_ref, m_ref, v_ref, *rest):
        comm_in, rest = rest[:n_in], rest[n_in:]
        (d_ref, mo_ref, vo_ref), rest = rest[:3], rest[3:]
        comm_out, sems = rest[:n_out], rest[n_out:]
        if comm:
            pl.when(pl.program_id(0) == 0)(lambda: comm.start(comm_in, comm_out, sems))
        gr = g_ref[...]
        mn = ADAM_B1 * m_ref[...] + (1.0 - ADAM_B1) * gr
        vn = ADAM_B2 * v_ref[...] + (1.0 - ADAM_B2) * (gr * gr)
        m_hat = mn / (1.0 - ADAM_B1 ** ADAM_STEP)
        v_hat = vn / (1.0 - ADAM_B2 ** ADAM_STEP)
        d_ref[...] = -ADAM_LR * (m_hat / (jnp.sqrt(v_hat) + ADAM_EPS) + ADAM_WD * w_ref[...])
        mo_ref[...] = mn
        vo_ref[...] = vn
        if comm:
            pl.when(pl.program_id(0) == steps - 1)(lambda: comm.finish(comm_in, comm_out, sems))

    blk = pl.BlockSpec((tm, cols), lambda i: (i, 0))
    out = jax.ShapeDtypeStruct((rows, cols), F32)
    r2 = lambda a: a.reshape(rows, cols)
    outs = pl.pallas_call(
        body, name=name, grid=(steps,), in_specs=[blk] * 4 + [ANY] * n_in, out_specs=[blk] * 3 + [ANY] * n_out,
        out_shape=[out] * 3 + (comm.out_shapes if comm else []), scratch_shapes=comm.sems if comm else [],
        compiler_params=_params(("arbitrary",) if comm else ("parallel",)),
    )(r2(w), r2(g), r2(m), r2(v), *(comm.inputs if comm else []))
    return tuple(o.reshape(shape) for o in outs[:3]), outs[3:]


W_IN_NATURAL = (("a", CONV_OFF, 3 * WIDTH), ("a", FOX_OFF, 3 * WIDTH), ("f", 0, N_HEADS), ("a", SB_OFF, 3 * WIDTH),
                ("a", GATE_OFF, 3 * D_MODEL))
W_IN_COLS = D_INA + N_HEADS
W_IN_SHARD = W_IN_COLS // 4


def _w_in_from_chips(blocks):
    def natural(lo, hi):
        cols = []
        for k, blk in enumerate(blocks):
            a, b = max(lo, k * W_IN_SHARD), min(hi, (k + 1) * W_IN_SHARD)
            if a < b:
                cols.append(blk[:, a - k * W_IN_SHARD:b - k * W_IN_SHARD])
        return cols

    start = {}
    pos = 0
    for part, at, length in W_IN_NATURAL:
        start[part, at] = (pos, pos + length)
        pos += length
    order = sorted((at, rng) for (part, at), rng in start.items() if part == "a")
    wa = jnp.concatenate([c for _, (lo, hi) in order for c in natural(lo, hi)], axis=1)
    wf = jnp.pad(jnp.concatenate(natural(*start["f", 0]), axis=1), ((0, 0), (0, LANES - N_HEADS)))
    return wa, wf


def _w_in_to_chips(ga, gf):
    chips = []
    for k in range(4):
        lo, hi, pos, cols = k * W_IN_SHARD, (k + 1) * W_IN_SHARD, 0, []
        for part, at, length in W_IN_NATURAL:
            a, b = max(lo, pos), min(hi, pos + length)
            if a < b:
                cols.append((ga if part == "a" else gf)[:, at + a - pos:at + b - pos])
            pos += length
        chips.append(jnp.concatenate(cols, axis=1))
    return jnp.stack(chips)


def _layer_fwd(x, p, plan, l):
    s = x.shape[0]
    proj, hn1 = _norm_mm(x, p["norm1_g"], p["wa"], BF16, name="in_proj")
    fraw, _ = _norm_mm(x, p["norm1_g"], p["wf"], F32, name="in_proj_f")
    ft = fraw[:, :N_HEADS].T
    crow8 = _fox_gates(ft, p["fox_f_bias"])
    crow = crow8.reshape(N_PAIR, 2, s)
    ccol = jnp.repeat(crow.transpose(0, 2, 1), HEAD_DIM, axis=2)
    oc = _conv_mix_fwd(proj, p["conv_w"])
    qn, kn = _fox_prep(proj, p["fox_q_norm_g"], p["fox_k_norm_g"])
    last = l + 1 == plan.depth
    (of, lse), got_in = _fox_fwd(qn, kn, proj, ccol, crow, comm=None if last else plan.gather(l + 1, BIG_IN))
    osb, got_rest = _sb_fwd(proj, comm=plan.gather(l, BIG_REST))
    p.update(plan.weights(l, BIG_REST, got_rest))
    nxt = None if last else plan.layer(l + 1, got_in)
    xm, merged, yc, yf, ys = _merge_fwd(x, oc, of, osb, p["w_proj_conv"], p["w_proj_fox"], p["w_proj_sb"], proj,
                                        p["gate_bias"], p["w_out"])
    up, hn2 = _norm_mm(xm, p["norm2_g"], p["w_up"], BF16, name="up_proj")
    hh = _glu_fwd(up, p["ffn_conv_w"], p["ffn_conv_b"])
    xo = _mm(hh, p["w_down"], "nn", F32, res=xm, name="down_proj")
    saved = dict(x=x, hn1=hn1, proj=proj, ft=ft, crow=crow, ccol=ccol, oc=oc, qn=qn, kn=kn, of=of, lse=lse, osb=osb,
                 merged=merged, yc=yc, yf=yf, ys=ys, xm=xm, hn2=hn2, up=up, hh=hh)
    return xo, saved, nxt


def _layer_bwd(dx, p, a, plan, pending_in):
    s = dx.shape[0]
    g = {}
    dhh = _mm(dx, p["w_down"], "nt", BF16, tn=1408, name="d_down_in")
    g["w_down"] = _mm(a["hh"], dx, "tn", F32, name="g_w_down")
    dug, duv, g["ffn_conv_w"], g["ffn_conv_b"] = _glu_bwd(dhh, a["up"], p["ffn_conv_w"], p["ffn_conv_b"])
    dup = jnp.concatenate([dug, duv], axis=1)
    g["w_up"] = _mm(a["hn2"], dup, "tn", F32, tn=1408, name="g_w_up")
    dhn2 = _mm(dup, p["w_up"], "nt", F32, name="d_up_in")
    dx, g["norm2_g"] = _norm_bwd(dhn2, a["xm"], p["norm2_g"], dx, name="norm2_bwd")
    dm = _mm(dx, p["w_out"], "nt", BF16, name="d_out_in")
    g["w_out"] = _mm(a["merged"], dx, "tn", F32, tn=512, name="g_w_out")
    dyc, dyf, dys, dgl, g["gate_bias"] = _gate_bwd(dm, a["yc"], a["yf"], a["ys"], a["proj"], p["gate_bias"])
    doc = _mm(dyc, p["w_proj_conv"], "nt", BF16, name="d_pc_in")
    dof = _mm(dyf, p["w_proj_fox"], "nt", BF16, name="d_pf_in")
    dos = _mm(dys, p["w_proj_sb"], "nt", BF16, name="d_ps_in")
    g["w_proj_conv"] = _mm(a["oc"], dyc, "tn", F32, name="g_w_pc")
    g["w_proj_fox"] = _mm(a["of"], dyf, "tn", F32, name="g_w_pf")
    g["w_proj_sb"] = _mm(a["osb"], dys, "tn", F32, name="g_w_ps")
    dcb, dcc, dch, g["conv_w"] = _conv_mix_bwd(doc, a["proj"], p["conv_w"])
    delta_f = _delta_rep(dof, a["of"])
    pending = plan.reduction(pending_in, {k: g[k] for k in BIG_REST})
    (dqs, dkn, dfv, dcrow, dcq), recv_a = _fox_bwd(a["qn"], a["kn"], a["proj"], dof, a["lse"], delta_f, a["ccol"], a["crow"],
                                                   comm=pending.exchange() if pending else None)
    dc = (dcrow + dcq).reshape(N_HEADS, s)
    dfq, dfk, g["fox_q_norm_g"], g["fox_k_norm_g"] = _fox_post(dqs, dkn, a["proj"], p["fox_q_norm_g"], p["fox_k_norm_g"])
    dft, g["fox_f_bias"] = _fox_gates_bwd(dc, a["ft"], p["fox_f_bias"])
    delta_s = _delta_rep(dos, a["osb"])
    (dsq, dsk, dsv), recv_b = _sb_bwd(a["proj"], dos, delta_s, comm=pending.sums(recv_a) if pending else None)
    done_in = None
    if pending:
        done = pending.finish(recv_b)
        if pending_in is not None:
            done_in, done = done[0], done[1:]
        g.update(zip(BIG_REST, done))
    dproj = jnp.concatenate([dgl, dcb, dcc, dch, dfq, dfk, dfv, dsq, dsk, dsv], axis=1)
    dfp = jnp.pad(dft.T, ((0, 0), (0, LANES - N_HEADS))).astype(BF16)
    ga = _mm(a["hn1"], dproj, "tn", F32, tn=768, name="g_w_in")
    gf = _mm(a["hn1"], dfp, "tn", F32, name="g_w_in_f")
    g["w_in"] = _w_in_to_chips(ga, gf)
    dhn1 = _mm(dfp, p["wf"], "nt", F32, name="d_in_f")
    dhn1 = _mm(dproj, p["wa"], "nt", F32, res=dhn1, tk=1536, name="d_in")
    dx, g["norm1_g"] = _norm_bwd(dhn1, a["x"], p["norm1_g"], dx, name="norm1_bwd")
    return dx, g, done_in


MATMUL_WEIGHTS = ("w_in", "w_proj_conv", "w_proj_fox", "w_proj_sb", "w_out", "w_up", "w_down")
WEIGHTS = ("norm1_g", "w_in", "fox_f_bias", "gate_bias", "conv_w", "fox_q_norm_g", "fox_k_norm_g", "w_proj_conv",
           "w_proj_fox", "w_proj_sb", "w_out", "norm2_g", "w_up", "ffn_conv_w", "ffn_conv_b", "w_down")


def _local_step(x, target, plan):
    depth = plan.depth
    layers, saved = [plan.layer(0, None)], []
    for l in range(depth):
        x, a, nxt = _layer_fwd(x, layers[l], plan, l)
        saved.append(a)
        if nxt is not None:
            layers.append(nxt)
    sq, dx = _loss_head(x, target)
    grads, pending_in = [None] * depth, None
    for l in reversed(range(depth)):
        dx, grads[l], done_in = _layer_bwd(dx, layers[l], saved[l], plan, pending_in)
        if done_in is not None:
            grads[l + 1]["w_in"] = done_in
        pending_in = grads[l]["w_in"]
    return sq, dx, grads, plan.reduction(pending_in, {})


ANY = pl.BlockSpec(memory_space=pl.ANY)


def _place():
    x, y, c = lax.axis_index("x"), lax.axis_index("y"), lax.axis_index("c")
    chips = [(1 - x, y), (x, 1 - y), (1 - x, 1 - y)]
    return x, y, c, chips


_Comm = collections.namedtuple("_Comm", "inputs out_shapes sems start finish")


def _run_comm(comm, name):
    n_in, n_out = len(comm.inputs), len(comm.out_shapes)

    def body(*refs):
        ins, outs, sems = refs[:n_in], refs[n_in:n_in + n_out], refs[n_in + n_out:]
        comm.start(ins, outs, sems)
        comm.finish(ins, outs, sems)

    return pl.pallas_call(body, name=name, in_specs=[ANY] * n_in, out_specs=[ANY] * n_out, out_shape=comm.out_shapes,
                          scratch_shapes=comm.sems)(*comm.inputs)


def _gather_comm(shards):
    n = len(shards)

    def copy(x_refs, out_refs, sems, k, t, chip_index, which_half, to, from_input=False):
        half = shards[t].shape[0] // 2
        rows = pl.ds(which_half * half, half)
        dst = out_refs[t].at[chip_index, rows, :]
        return pltpu.make_async_remote_copy(
            src_ref=x_refs[t].at[rows, :] if from_input else dst, dst_ref=dst,
            send_sem=sems[0].at[k, t], recv_sem=sems[1].at[k, t], device_id=to, device_id_type=MESH)

    def first(x_refs, out_refs, sems):
        x, y, c, chips = _place()
        return [copy(x_refs, out_refs, sems, p, t, 2 * x + y, c, (*chip, c), from_input=True)
                for t in range(n) for p, chip in enumerate(chips)]

    def start(x_refs, out_refs, sems):
        for cp in first(x_refs, out_refs, sems):
            cp.start()

    def finish(x_refs, out_refs, sems):
        x, y, c, chips = _place()
        passed = []
        for t in range(n):
            for p, chip in enumerate(chips):
                copy(x_refs, out_refs, sems, p, t, 2 * chip[0] + chip[1], c, (x, y, c)).wait_recv()
                fwd = copy(x_refs, out_refs, sems, 3 + p, t, 2 * chip[0] + chip[1], c, (x, y, 1 - c))
                fwd.start()
                passed.append(fwd)
        for t in range(n):
            for p, chip in enumerate(chips):
                copy(x_refs, out_refs, sems, 3 + p, t, 2 * chip[0] + chip[1], 1 - c, (x, y, c)).wait_recv()
        for cp in first(x_refs, out_refs, sems) + passed:
            cp.wait_send()

    return _Comm(list(shards), [jax.ShapeDtypeStruct((4, *s.shape), s.dtype) for s in shards],
                 [pltpu.SemaphoreType.DMA((6, n)), pltpu.SemaphoreType.DMA((6, n))], start, finish)


def _own_block(got, shards):
    chip = 2 * lax.axis_index("x") + lax.axis_index("y")
    return [lax.dynamic_update_index_in_dim(g, s, chip, 0) for g, s in zip(got, shards)]


def _pair_exchange_comm(gs):
    n = len(gs)

    def copies(g_refs, recv_refs, sems):
        x, y, c, _ = _place()
        return [pltpu.make_async_remote_copy(
            src_ref=g_refs[t].at[:, pl.ds((1 - c) * (gs[t].shape[1] // 2), gs[t].shape[1] // 2), :], dst_ref=recv_refs[t],
            send_sem=sems[0].at[t], recv_sem=sems[1].at[t], device_id=(x, y, 1 - c), device_id_type=MESH) for t in range(n)]

    def start(g_refs, recv_refs, sems):
        for cp in copies(g_refs, recv_refs, sems):
            cp.start()

    def finish(g_refs, recv_refs, sems):
        for cp in copies(g_refs, recv_refs, sems):
            cp.wait()

    return _Comm(list(gs), [jax.ShapeDtypeStruct((4, g.shape[1] // 2, g.shape[2]), g.dtype) for g in gs],
                 [pltpu.SemaphoreType.DMA((n,)), pltpu.SemaphoreType.DMA((n,))], start, finish)


def _pair_sum(g, recv, core, tr=256):
    n, r, cols = g.shape
    half = r // 2
    tr = _row_tile(half, tr)
    nb = half // tr

    def body(c_ref, g_ref, r_ref, o_ref):
        o_ref[...] = (g_ref[...] + r_ref[...]).astype(BF16)

    grid_spec = pltpu.PrefetchScalarGridSpec(
        num_scalar_prefetch=1, grid=(n, nb),
        in_specs=[pl.BlockSpec((None, tr, cols), lambda k, i, c: (k, c[0] * nb + i, 0)),
                  pl.BlockSpec((None, tr, cols), lambda k, i, c: (k, i, 0))],
        out_specs=pl.BlockSpec((None, tr, cols), lambda k, i, c: (k, i, 0)))
    return pl.pallas_call(
        body, name="rs_pair_sum", grid_spec=grid_spec, out_shape=jax.ShapeDtypeStruct((n, half, cols), BF16),
        compiler_params=_params(("parallel", "parallel")),
    )(core, g, recv)


def _row_tile(rows, pref):
    best = None
    for t in range(16, min(rows, pref) + 1, 16):
        if rows % t == 0:
            best = t
    assert best is not None, (rows, pref)
    return best


def _chip_exchange_comm(s1s):
    n = len(s1s)

    def copies(s_refs, recv_refs, sems):
        x, y, c, chips = _place()
        return [pltpu.make_async_remote_copy(
            src_ref=s_refs[t].at[2 * chip[0] + chip[1]], dst_ref=recv_refs[t].at[p],
            send_sem=sems[0].at[p, t], recv_sem=sems[1].at[p, t], device_id=(*chip, c), device_id_type=MESH)
            for t in range(n) for p, chip in enumerate(chips)]

    def start(s_refs, recv_refs, sems):
        for cp in copies(s_refs, recv_refs, sems):
            cp.start()

    def finish(s_refs, recv_refs, sems):
        for cp in copies(s_refs, recv_refs, sems):
            cp.wait()

    return _Comm(list(s1s), [jax.ShapeDtypeStruct((3, *s.shape[1:]), s.dtype) for s in s1s],
                 [pltpu.SemaphoreType.DMA((3, n)), pltpu.SemaphoreType.DMA((3, n))], start, finish)


def _final_sum(g, recv_a, recv_b, core, chip, tr=256):
    n, r, cols = g.shape
    half = r // 2
    tr = _row_tile(half, tr)
    nb = half // tr

    def body(c_ref, k_ref, g_ref, a_ref, b0_ref, b1_ref, b2_ref, o_ref):
        total = g_ref[...] + a_ref[...]
        for b_ref in (b0_ref, b1_ref, b2_ref):
            total = total + b_ref[...].astype(F32)
        o_ref[...] = total

    rel = lambda p: pl.BlockSpec((None, tr, cols), lambda i, c, k, p=p: (p, i, 0))
    grid_spec = pltpu.PrefetchScalarGridSpec(
        num_scalar_prefetch=2, grid=(nb,),
        in_specs=[pl.BlockSpec((None, tr, cols), lambda i, c, k: (k[0], c[0] * nb + i, 0)),
                  pl.BlockSpec((None, tr, cols), lambda i, c, k: (k[0], i, 0)), rel(0), rel(1), rel(2)],
        out_specs=pl.BlockSpec((tr, cols), lambda i, c, k: (c[0] * nb + i, 0)))
    return pl.pallas_call(
        body, name="rs_final_sum", grid_spec=grid_spec, out_shape=jax.ShapeDtypeStruct((r, cols), F32),
        compiler_params=_params(("parallel",)),
    )(core, chip, g, recv_a, recv_b, recv_b, recv_b)


def _pair_join(fs):
    n = len(fs)

    def body(*refs):
        out_refs = refs[n:2 * n]
        send_sems, recv_sems = refs[2 * n:]
        x, y, c, _ = _place()

        def copy(t, which_half):
            h = fs[t].shape[0] // 2
            rows = out_refs[t].at[pl.ds(which_half * h, h), :]
            return pltpu.make_async_remote_copy(
                src_ref=rows, dst_ref=rows, send_sem=send_sems.at[t], recv_sem=recv_sems.at[t],
                device_id=(x, y, 1 - c), device_id_type=MESH)

        sends = [copy(t, c) for t in range(n)]
        for cp in sends:
            cp.start()
        for t in range(n):
            sends[t].wait_send()
            copy(t, 1 - c).wait_recv()

    return pl.pallas_call(
        body, name="rs_pair_join", in_specs=[ANY] * n, out_specs=[ANY] * n,
        out_shape=[jax.ShapeDtypeStruct(f.shape, f.dtype) for f in fs],
        input_output_aliases={t: t for t in range(n)},
        scratch_shapes=[pltpu.SemaphoreType.DMA((n,)), pltpu.SemaphoreType.DMA((n,))],
    )(*fs)


class _Reduction:
    def __init__(self, gs):
        self.gs = gs
        self.core = lax.axis_index("c").astype(jnp.int32).reshape(1)
        self.chip = (2 * lax.axis_index("x") + lax.axis_index("y")).astype(jnp.int32).reshape(1)

    def exchange(self):
        return _pair_exchange_comm(self.gs)

    def sums(self, recv_a):
        self.recv_a = list(recv_a)
        return _chip_exchange_comm([_pair_sum(g, ra, self.core) for g, ra in zip(self.gs, self.recv_a)])

    def finish(self, recv_b):
        return _pair_join([_final_sum(g, ra, rb, self.core, self.chip) for g, ra, rb in zip(self.gs, self.recv_a, recv_b)])

    def run(self):
        recv_a = _run_comm(self.exchange(), "rs_pair_exchange")
        return self.finish(_run_comm(self.sums(recv_a), "rs_chip_exchange"))


def _all_reduce_small(v):
    r, cols = v.shape

    def body(v_ref, out_ref, buf_ref, send_sems, recv_sems):
        x, y, c, _ = _place()
        flip = lambda a, bit: 1 - a if bit else a
        buf_ref[4 * x + 2 * y + c] = v_ref[...]
        cps = []
        for rel in range(1, 8):
            peer = (flip(x, rel & 4), flip(y, rel & 2), flip(c, rel & 1))
            cps.append(pltpu.make_async_remote_copy(
                src_ref=v_ref, dst_ref=buf_ref.at[4 * x + 2 * y + c], send_sem=send_sems.at[rel - 1], recv_sem=recv_sems.at[rel - 1],
                device_id=peer, device_id_type=MESH))
        for cp in cps:
            cp.start()
        for cp in cps:
            cp.wait()
        total = buf_ref[0]
        for d in range(1, 8):
            total = total + buf_ref[d]
        out_ref[...] = total

    vm = pl.BlockSpec(memory_space=pltpu.VMEM)
    return pl.pallas_call(
        body, name="all_reduce_small", in_specs=[vm], out_specs=vm, out_shape=jax.ShapeDtypeStruct((r, cols), F32),
        scratch_shapes=[pltpu.VMEM((8, r, cols), F32), pltpu.SemaphoreType.DMA((7,)), pltpu.SemaphoreType.DMA((7,))],
    )(v)


SHARD_AXIS = {"w_in": 1, "conv_w": 1, "w_proj_conv": 1, "w_proj_fox": 1, "w_proj_sb": 1, "w_out": 0, "w_up": 1,
              "ffn_conv_w": 1, "w_down": 0}
SMALL_SHARDED = ("conv_w", "ffn_conv_w")
BIG = tuple(k for k in SHARD_AXIS if k not in SMALL_SHARDED)
BIG_IN = ("w_in",)
BIG_REST = tuple(k for k in BIG if k not in BIG_IN)
REPLICATED = tuple(k for k in WEIGHTS if k not in SHARD_AXIS)
SMALL = REPLICATED + SMALL_SHARDED


def _pack_small(parts, row_align):
    flat = jnp.concatenate([p.reshape(-1) for p in parts])
    rows = -(-flat.shape[0] // (PACK_COLS * row_align)) * row_align
    return jnp.pad(flat, (0, rows * PACK_COLS - flat.shape[0])).reshape(rows, PACK_COLS)


def _unpack_small(packed, shapes):
    flat = packed.reshape(-1)
    out, off = [], 0
    for shape in shapes:
        size = int(np.prod(shape))
        out.append(flat[off:off + size].reshape(shape))
        off += size
    return out


def _blocks(stacked, own):
    chip = 2 * lax.axis_index("x") + lax.axis_index("y")
    return [jnp.where(chip == k, own, stacked[k]) for k in range(4)]


def _to_chips(full, axis):
    a, b = full.shape
    if axis == 0:
        return full.reshape(4, a // 4, b)
    return jnp.moveaxis(full.reshape(a, 4, b // 4), 1, 0)


class _Plan:
    def __init__(self, given):
        self.given = given
        self.depth = given["norm1_g"].shape[0]
        conv_shapes = [given[k].shape for k in SMALL_SHARDED]
        packed = [_pack_small([given[k] for k in SMALL_SHARDED], 16)]
        (got,) = _own_block(_run_comm(_gather_comm(packed), "gather_conv_weights"), packed)
        by_chip = [_unpack_small(got[j], conv_shapes) for j in range(4)]
        self.conv = {k: jnp.concatenate([by_chip[j][i] for j in range(4)], axis=-1) for i, k in enumerate(SMALL_SHARDED)}
        self.shards = {}

    def _shards(self, l, names):
        if (l, names) not in self.shards:
            self.shards[l, names] = [self.given[k][l].astype(BF16) for k in names]
        return self.shards[l, names]

    def gather(self, l, names):
        return _gather_comm(self._shards(l, names))

    def weights(self, l, names, got):
        out = {}
        for k, stacked, own in zip(names, got, self._shards(l, names)):
            if k == "w_in":
                out["wa"], out["wf"] = _w_in_from_chips(_blocks(stacked, own))
            else:
                out[k] = jnp.concatenate(_blocks(stacked, own), axis=SHARD_AXIS[k])
        return out

    def layer(self, l, got_in):
        if got_in is None:
            got_in = _run_comm(self.gather(l, BIG_IN), "gather_w_in")
        p = {k: self.given[k][l] for k in REPLICATED}
        p.update({k: self.conv[k][l] for k in SMALL_SHARDED})
        p.update(self.weights(l, BIG_IN, got_in))
        return p

    def reduction(self, w_in_grad, rest):
        gs = [] if w_in_grad is None else [w_in_grad]
        return _Reduction(gs + [_to_chips(rest[k], SHARD_AXIS[k]) for k in BIG_REST if k in rest])


def kernel(x, norm1_g, w_in, fox_f_bias, gate_bias, conv_w, fox_q_norm_g, fox_k_norm_g, w_proj_conv, w_proj_fox, w_proj_sb, w_out, norm2_g, w_up, ffn_conv_w, ffn_conv_b, w_down, loss_target, m_norm1_g, m_w_in, m_fox_f_bias, m_gate_bias, m_conv_w, m_fox_q_norm_g, m_fox_k_norm_g, m_w_proj_conv, m_w_proj_fox, m_w_proj_sb, m_w_out, m_norm2_g, m_w_up, m_ffn_conv_w, m_ffn_conv_b, m_w_down, v_norm1_g, v_w_in, v_fox_f_bias, v_gate_bias, v_conv_w, v_fox_q_norm_g, v_fox_k_norm_g, v_w_proj_conv, v_w_proj_fox, v_w_proj_sb, v_w_out, v_norm2_g, v_w_up, v_ffn_conv_w, v_ffn_conv_b, v_w_down):
    given = dict(x=x, norm1_g=norm1_g, w_in=w_in, fox_f_bias=fox_f_bias, gate_bias=gate_bias, conv_w=conv_w, fox_q_norm_g=fox_q_norm_g, fox_k_norm_g=fox_k_norm_g, w_proj_conv=w_proj_conv, w_proj_fox=w_proj_fox, w_proj_sb=w_proj_sb, w_out=w_out, norm2_g=norm2_g, w_up=w_up, ffn_conv_w=ffn_conv_w, ffn_conv_b=ffn_conv_b, w_down=w_down, loss_target=loss_target, m_norm1_g=m_norm1_g, m_w_in=m_w_in, m_fox_f_bias=m_fox_f_bias, m_gate_bias=m_gate_bias, m_conv_w=m_conv_w, m_fox_q_norm_g=m_fox_q_norm_g, m_fox_k_norm_g=m_fox_k_norm_g, m_w_proj_conv=m_w_proj_conv, m_w_proj_fox=m_w_proj_fox, m_w_proj_sb=m_w_proj_sb, m_w_out=m_w_out, m_norm2_g=m_norm2_g, m_w_up=m_w_up, m_ffn_conv_w=m_ffn_conv_w, m_ffn_conv_b=m_ffn_conv_b, m_w_down=m_w_down, v_norm1_g=v_norm1_g, v_w_in=v_w_in, v_fox_f_bias=v_fox_f_bias, v_gate_bias=v_gate_bias, v_conv_w=v_conv_w, v_fox_q_norm_g=v_fox_q_norm_g, v_fox_k_norm_g=v_fox_k_norm_g, v_w_proj_conv=v_w_proj_conv, v_w_proj_fox=v_w_proj_fox, v_w_proj_sb=v_w_proj_sb, v_w_out=v_w_out, v_norm2_g=v_norm2_g, v_w_up=v_w_up, v_ffn_conv_w=v_ffn_conv_w, v_ffn_conv_b=v_ffn_conv_b, v_w_down=v_w_down)
    depth = given["norm1_g"].shape[0]
    chip = 2 * lax.axis_index("x") + lax.axis_index("y")

    sq, dx, grads, last = _local_step(given["x"][0], given["loss_target"][0], _Plan(given))
    loss = lax.psum(0.5 * sq[0, 0] / D_MODEL, ("x", "y", "c"))

    gsum = {k: jnp.stack([g[k] for g in grads]) for k in BIG_REST}
    small_shapes = [(depth, *grads[0][k].shape) for k in SMALL]
    summed = _all_reduce_small(_pack_small([jnp.stack([g[k] for g in grads]) for k in SMALL], 8))
    for k, total in zip(SMALL, _unpack_small(summed, small_shapes)):
        if k in SMALL_SHARDED:
            total = lax.dynamic_index_in_dim(total.reshape(*total.shape[:-1], 4, total.shape[-1] // 4), chip, axis=2, keepdims=False)
        gsum[k] = total

    step = lambda k, comm=None: _adamw(given[k], gsum[k], given["m_" + k], given["v_" + k], "adamw_" + k, comm)
    done = {}
    done["w_down"], recv_a = step("w_down", last.exchange())
    done["w_up"], recv_b = step("w_up", last.sums(recv_a))
    gsum["w_in"] = jnp.stack(list(last.finish(recv_b)) + [g["w_in"] for g in grads[1:]])
    for k in WEIGHTS:
        if k not in done:
            done[k], _ = step(k)
    deltas, new_m, new_v = ({k: done[k][i] for k in WEIGHTS} for i in range(3))
    return (loss, dx[None], *[gsum[k] for k in WEIGHTS], *[deltas[k] for k in WEIGHTS],
            *[new_m[k] for k in WEIGHTS], *[new_v[k] for k in WEIGHTS])
```

```python
import collections

import numpy as np
import jax
import jax.numpy as jnp
from jax import lax
from jax.experimental import pallas as pl
from jax.experimental.pallas import tpu as pltpu

F32 = jnp.float32
BF16 = jnp.bfloat16

D_MODEL = 1024
DEPTH = 4
HEAD_DIM = 64
N_HEADS = 8
WIDTH = 512
D_FF = 2816
NORM_EPS = 1e-6
NEG_INF = -1e30
QK_SCALE = HEAD_DIM ** -0.5
LANES = 128
N_PAIR = N_HEADS // 2

GATE_OFF = 0
CONV_OFF = 3 * D_MODEL
FOX_OFF = CONV_OFF + 3 * WIDTH
SB_OFF = FOX_OFF + 3 * WIDTH
D_INA = SB_OFF + 3 * WIDTH

ADAM_LR = 0.001
ADAM_B1 = 0.9
ADAM_B2 = 0.999
ADAM_EPS = 1e-08
ADAM_WD = 0.01
ADAM_STEP = 10

VMEM_LIMIT = 48 * 1024 * 1024
ADAMW_BLOCK_BYTES = 1024 * 1024

PACK_COLS = 1024
PACK_ROW_ALIGN = 32
MESH = pl.DeviceIdType.MESH


def _params(sem):
    return pltpu.CompilerParams(dimension_semantics=sem, vmem_limit_bytes=VMEM_LIMIT)


def _dot(a, b, dims):
    return lax.dot_general(a, b, (dims, ((), ())), preferred_element_type=F32)


_NN = ((1,), (0,))
_NT = ((1,), (1,))
_TN = ((0,), (0,))


def _pick(dim, pref):
    if dim <= pref:
        return dim
    best = None
    for mult in range(1, dim // LANES + 1):
        t = mult * LANES
        if t <= pref and dim % t == 0:
            best = t
    assert best is not None, (dim, pref)
    return best


def _mm(a, b, mode, out_dtype=F32, tm=1024, tn=1024, tk=2048, res=None, name="mm", comm=None):
    if mode == "nn":
        (m, k), (_, n) = a.shape, b.shape
    elif mode == "nt":
        (m, k), (n, _) = a.shape, b.shape
    else:
        (k, m), (_, n) = a.shape, b.shape
    tm, tn, tk = _pick(m, tm), _pick(n, tn), _pick(k, tk)
    nk = k // tk
    dims = {"nn": _NN, "nt": _NT, "tn": _TN}[mode]
    if mode == "tn":
        a_spec = pl.BlockSpec((tk, tm), lambda i, j, kk: (kk, i))
    else:
        a_spec = pl.BlockSpec((tm, tk), lambda i, j, kk: (i, kk))
    if mode == "nt":
        b_spec = pl.BlockSpec((tn, tk), lambda i, j, kk: (j, kk))
    else:
        b_spec = pl.BlockSpec((tk, tn), lambda i, j, kk: (kk, j))
    o_spec = pl.BlockSpec((tm, tn), lambda i, j, kk: (i, j))
    in_specs = [a_spec, b_spec] + ([o_spec] if res is not None else [])
    n_in, n_out = (len(comm.inputs), len(comm.out_shapes)) if comm else (0, 0)
    gm, gn = m // tm, n // tn

    def body(*refs):
        a_ref, b_ref, refs = refs[0], refs[1], refs[2:]
        r_ref, refs = (refs[0], refs[1:]) if res is not None else (None, refs)
        comm_in, o_ref, refs = refs[:n_in], refs[n_in], refs[n_in + 1:]
        comm_out, acc_ref, sems = refs[:n_out], refs[n_out], refs[n_out + 1:]
        kk = pl.program_id(2)
        if comm:
            at = lambda i, j, k: (pl.program_id(0) == i) & (pl.program_id(1) == j) & (kk == k)
            pl.when(at(0, 0, 0))(lambda: comm.start(comm_in, comm_out, sems))
        part = _dot(a_ref[...].astype(BF16), b_ref[...].astype(BF16), dims)

        def finish(total):
            if r_ref is not None:
                total = total + r_ref[...].astype(F32)
            o_ref[...] = total.astype(out_dtype)

        if nk == 1:
            finish(part)
        else:
            @pl.when(kk == 0)
            def _():
                acc_ref[...] = part

            @pl.when(kk > 0)
            def _():
                acc_ref[...] += part

            @pl.when(kk == nk - 1)
            def _():
                finish(acc_ref[...])

        if comm:
            pl.when(at(gm - 1, gn - 1, nk - 1))(lambda: comm.finish(comm_in, comm_out, sems))

    args = (a, b) + ((res,) if res is not None else ())
    out = jax.ShapeDtypeStruct((m, n), out_dtype)
    acc = pltpu.VMEM((tm, tn) if nk > 1 else (8, LANES), F32)
    if not comm:
        return pl.pallas_call(
            body, name=name, grid=(gm, gn, nk), in_specs=in_specs, out_specs=o_spec, out_shape=out, scratch_shapes=[acc],
            compiler_params=_params(("parallel", "parallel", "arbitrary")),
        )(*args)
    outs = pl.pallas_call(
        body, name=name, grid=(gm, gn, nk), in_specs=in_specs + [ANY] * n_in, out_specs=[o_spec] + [ANY] * n_out,
        out_shape=[out] + comm.out_shapes, scratch_shapes=[acc] + comm.sems,
        compiler_params=_params(("arbitrary", "arbitrary", "arbitrary")),
    )(*args, *comm.inputs)
    return outs[0], outs[1:]


def _norm_mm(x, g, w, out_dtype, tm=1024, tn=1536, name="norm_mm"):
    m, d = x.shape
    n = w.shape[1]
    tm, tn = _pick(m, tm), _pick(n, tn)

    def body(x_ref, g_ref, w_ref, o_ref, hn_ref):
        @pl.when(pl.program_id(1) == 0)
        def _():
            xf = x_ref[...]
            r = lax.rsqrt(jnp.mean(xf * xf, axis=-1, keepdims=True) + NORM_EPS)
            hn_ref[...] = (xf * r * g_ref[...]).astype(BF16)

        o_ref[...] = _dot(hn_ref[...], w_ref[...], _NN).astype(out_dtype)

    return pl.pallas_call(
        body, name=name, grid=(m // tm, n // tn),
        in_specs=[pl.BlockSpec((tm, d), lambda i, j: (i, 0)),
                  pl.BlockSpec((1, d), lambda i, j: (0, 0)),
                  pl.BlockSpec((d, tn), lambda i, j: (0, j))],
        out_specs=[pl.BlockSpec((tm, tn), lambda i, j: (i, j)),
                   pl.BlockSpec((tm, d), lambda i, j: (i, 0))],
        out_shape=[jax.ShapeDtypeStruct((m, n), out_dtype), jax.ShapeDtypeStruct((m, d), BF16)],
        compiler_params=_params(("parallel", "arbitrary")),
    )(x, g.reshape(1, d), w)


def _norm_bwd(dhn, x, g, dx_in, tm=256, name="norm_bwd"):
    m, d = x.shape
    tm = min(tm, m)

    def body(dhn_ref, x_ref, g_ref, dxi_ref, dx_ref, gg_ref):
        xf = x_ref[...]
        r = lax.rsqrt(jnp.mean(xf * xf, axis=-1, keepdims=True) + NORM_EPS)
        xhat = xf * r
        dh = dhn_ref[...].astype(F32)
        dxn = dh * g_ref[...]
        mean = jnp.mean(dxn * xhat, axis=-1, keepdims=True)
        dx_ref[...] = dxi_ref[...] + r * (dxn - xhat * mean)
        part = jnp.sum(dh * xhat, axis=0, keepdims=True)

        @pl.when(pl.program_id(0) == 0)
        def _():
            gg_ref[...] = part

        @pl.when(pl.program_id(0) > 0)
        def _():
            gg_ref[...] += part

    row = pl.BlockSpec((tm, d), lambda i: (i, 0))
    vec = pl.BlockSpec((1, d), lambda i: (0, 0))
    dx, gg = pl.pallas_call(
        body, name=name, grid=(m // tm,), in_specs=[row, row, vec, row], out_specs=[row, vec],
        out_shape=[jax.ShapeDtypeStruct((m, d), F32), jax.ShapeDtypeStruct((1, d), F32)],
        compiler_params=_params(("arbitrary",)),
    )(dhn, x, g.reshape(1, d), dx_in)
    return dx, gg.reshape(d)


def _down(u, k):
    s = u.shape[0]
    rows = lax.broadcasted_iota(jnp.int32, u.shape, 0)
    return jnp.where(rows < k, 0.0, pltpu.roll(u, k, axis=0))


def _up(u, k):
    s = u.shape[0]
    rows = lax.broadcasted_iota(jnp.int32, u.shape, 0)
    return jnp.where(rows >= s - k, 0.0, pltpu.roll(u, s - k, axis=0))


def _conv_mix_fwd(proj, conv_w, tc=128):
    s = proj.shape[0]
    nb = WIDTH // tc
    off = CONV_OFF // tc

    def body(b_ref, c_ref, h_ref, w_ref, o_ref):
        u = c_ref[...].astype(F32) * h_ref[...].astype(F32)
        w = w_ref[...]
        cv = w[0:1] * _down(u, 2) + w[1:2] * _down(u, 1) + w[2:3] * u
        o_ref[...] = (b_ref[...].astype(F32) * cv).astype(BF16)

    col = lambda k: pl.BlockSpec((s, tc), lambda j, k=k: (0, off + k * nb + j))
    return pl.pallas_call(
        body, name="conv_mix_fwd", grid=(nb,),
        in_specs=[col(0), col(1), col(2), pl.BlockSpec((3, tc), lambda j: (0, j))],
        out_specs=pl.BlockSpec((s, tc), lambda j: (0, j)),
        out_shape=jax.ShapeDtypeStruct((s, WIDTH), BF16),
        compiler_params=_params(("parallel",)),
    )(proj, proj, proj, conv_w)


def _conv_mix_bwd(do, proj, conv_w, tc=128):
    s = proj.shape[0]
    nb = WIDTH // tc
    off = CONV_OFF // tc

    def body(do_ref, b_ref, c_ref, h_ref, w_ref, db_ref, dc_ref, dh_ref, gw_ref):
        b = b_ref[...].astype(F32)
        c = c_ref[...].astype(F32)
        h = h_ref[...].astype(F32)
        g = do_ref[...].astype(F32)
        w = w_ref[...]
        u = c * h
        u1, u2 = _down(u, 1), _down(u, 2)
        cv = w[0:1] * u2 + w[1:2] * u1 + w[2:3] * u
        db_ref[...] = (g * cv).astype(BF16)
        dcv = g * b
        gw_ref[0:1, :] = jnp.sum(dcv * u2, axis=0, keepdims=True)
        gw_ref[1:2, :] = jnp.sum(dcv * u1, axis=0, keepdims=True)
        gw_ref[2:3, :] = jnp.sum(dcv * u, axis=0, keepdims=True)
        du = w[2:3] * dcv + w[1:2] * _up(dcv, 1) + w[0:1] * _up(dcv, 2)
        dc_ref[...] = (du * h).astype(BF16)
        dh_ref[...] = (du * c).astype(BF16)

    col = lambda k: pl.BlockSpec((s, tc), lambda j, k=k: (0, off + k * nb + j))
    own = pl.BlockSpec((s, tc), lambda j: (0, j))
    wsp = pl.BlockSpec((3, tc), lambda j: (0, j))
    act = jax.ShapeDtypeStruct((s, WIDTH), BF16)
    return pl.pallas_call(
        body, name="conv_mix_bwd", grid=(nb,),
        in_specs=[own, col(0), col(1), col(2), wsp], out_specs=[own, own, own, wsp],
        out_shape=[act, act, act, jax.ShapeDtypeStruct((3, WIDTH), F32)],
        compiler_params=_params(("parallel",)),
    )(do, proj, proj, proj, conv_w)


def _glu_fwd(up, w, b, tc=256):
    s = up.shape[0]
    nb = D_FF // tc

    def body(g_ref, v_ref, w_ref, b_ref, o_ref):
        ug = g_ref[...].astype(F32)
        wv = w_ref[...]
        cg = wv[0:1] * _down(ug, 2) + wv[1:2] * _down(ug, 1) + wv[2:3] * ug + b_ref[...]
        act = cg * jax.nn.sigmoid(cg)
        o_ref[...] = (act * v_ref[...].astype(F32)).astype(BF16)

    return pl.pallas_call(
        body, name="glu_fwd", grid=(nb,),
        in_specs=[pl.BlockSpec((s, tc), lambda j: (0, j)), pl.BlockSpec((s, tc), lambda j: (0, nb + j)),
                  pl.BlockSpec((3, tc), lambda j: (0, j)), pl.BlockSpec((1, tc), lambda j: (0, j))],
        out_specs=pl.BlockSpec((s, tc), lambda j: (0, j)),
        out_shape=jax.ShapeDtypeStruct((s, D_FF), BF16),
        compiler_params=_params(("parallel",)),
    )(up, up, w, b.reshape(1, D_FF))


def _glu_bwd(dh, up, w, b, tc=256):
    s = up.shape[0]
    nb = D_FF // tc

    def body(dh_ref, g_ref, v_ref, w_ref, b_ref, dg_ref, dv_ref, gw_ref, gb_ref):
        ug = g_ref[...].astype(F32)
        uv = v_ref[...].astype(F32)
        d = dh_ref[...].astype(F32)
        wv = w_ref[...]
        u1, u2 = _down(ug, 1), _down(ug, 2)
        cg = wv[0:1] * u2 + wv[1:2] * u1 + wv[2:3] * ug + b_ref[...]
        sg = jax.nn.sigmoid(cg)
        dv_ref[...] = (d * (cg * sg)).astype(BF16)
        dcg = d * uv * (sg * (1.0 + cg * (1.0 - sg)))
        gb_ref[...] = jnp.sum(dcg, axis=0, keepdims=True)
        gw_ref[0:1, :] = jnp.sum(dcg * u2, axis=0, keepdims=True)
        gw_ref[1:2, :] = jnp.sum(dcg * u1, axis=0, keepdims=True)
        gw_ref[2:3, :] = jnp.sum(dcg * ug, axis=0, keepdims=True)
        dg_ref[...] = (wv[2:3] * dcg + wv[1:2] * _up(dcg, 1) + wv[0:1] * _up(dcg, 2)).astype(BF16)

    own = pl.BlockSpec((s, tc), lambda j: (0, j))
    wsp = pl.BlockSpec((3, tc), lambda j: (0, j))
    bsp = pl.BlockSpec((1, tc), lambda j: (0, j))
    act = jax.ShapeDtypeStruct((s, D_FF), BF16)
    dg, dv, gw, gb = pl.pallas_call(
        body, name="glu_bwd", grid=(nb,),
        in_specs=[own, own, pl.BlockSpec((s, tc), lambda j: (0, nb + j)), wsp, bsp],
        out_specs=[own, own, wsp, bsp],
        out_shape=[act, act, jax.ShapeDtypeStruct((3, D_FF), F32), jax.ShapeDtypeStruct((1, D_FF), F32)],
        compiler_params=_params(("parallel",)),
    )(dh, up, up, w, b.reshape(1, D_FF))
    return dg, dv, gw, gb.reshape(D_FF)


def _merge_fwd(x, oc, of, osb, wpc, wpf, wps, proj, gb, wout, tm=256):
    s, d = x.shape
    tm = min(tm, s)

    def body(x_ref, oc_ref, of_ref, os_ref, wpc_ref, wpf_ref, wps_ref, g0_ref, g1_ref, g2_ref, gb_ref, wo_ref,
             xo_ref, mg_ref, yc_ref, yf_ref, ys_ref):
        merged = jnp.zeros((tm, d), F32)
        for k, (o_ref, w_ref, g_ref, y_ref) in enumerate(
                ((oc_ref, wpc_ref, g0_ref, yc_ref), (of_ref, wpf_ref, g1_ref, yf_ref), (os_ref, wps_ref, g2_ref, ys_ref))):
            y = _dot(o_ref[...].astype(BF16), w_ref[...], _NN)
            y_ref[...] = y.astype(BF16)
            gate = jax.nn.sigmoid(g_ref[...].astype(F32) + gb_ref[:, k * d:(k + 1) * d])
            merged = merged + gate * y
        mb = merged.astype(BF16)
        mg_ref[...] = mb
        xo_ref[...] = x_ref[...] + _dot(mb, wo_ref[...], _NN)

    rowd = pl.BlockSpec((tm, d), lambda i: (i, 0))
    roww = pl.BlockSpec((tm, WIDTH), lambda i: (i, 0))
    wp = pl.BlockSpec((WIDTH, d), lambda i: (0, 0))
    gcol = lambda k: pl.BlockSpec((tm, d), lambda i, k=k: (i, GATE_OFF // d + k))
    actd = jax.ShapeDtypeStruct((s, d), BF16)
    return pl.pallas_call(
        body, name="merge_fwd", grid=(s // tm,),
        in_specs=[rowd, roww, roww, roww, wp, wp, wp, gcol(0), gcol(1), gcol(2),
                  pl.BlockSpec((1, 3 * d), lambda i: (0, 0)), pl.BlockSpec((d, d), lambda i: (0, 0))],
        out_specs=[rowd, rowd, rowd, rowd, rowd],
        out_shape=[jax.ShapeDtypeStruct((s, d), F32), actd, actd, actd, actd],
        compiler_params=_params(("parallel",)),
    )(x, oc, of, osb, wpc, wpf, wps, proj, proj, proj, gb.reshape(1, 3 * d), wout)


def _gate_bwd(dm, yc, yf, ys, proj, gb, tm=256):
    s, d = dm.shape
    tm = min(tm, s)

    def body(dm_ref, yc_ref, yf_ref, ys_ref, g0_ref, g1_ref, g2_ref, gb_ref, dyc_ref, dyf_ref, dys_ref, dgl_ref, ggb_ref):
        g = dm_ref[...].astype(F32)
        parts = []
        for k, (y_ref, g_ref, dy_ref) in enumerate(((yc_ref, g0_ref, dyc_ref), (yf_ref, g1_ref, dyf_ref), (ys_ref, g2_ref, dys_ref))):
            gate = jax.nn.sigmoid(g_ref[...].astype(F32) + gb_ref[:, k * d:(k + 1) * d])
            dy_ref[...] = (g * gate).astype(BF16)
            dgl = g * y_ref[...].astype(F32) * gate * (1.0 - gate)
            dgl_ref[:, k * d:(k + 1) * d] = dgl.astype(BF16)
            parts.append(jnp.sum(dgl, axis=0, keepdims=True))
        part = jnp.concatenate(parts, axis=1)

        @pl.when(pl.program_id(0) == 0)
        def _():
            ggb_ref[...] = part

        @pl.when(pl.program_id(0) > 0)
        def _():
            ggb_ref[...] += part

    rowd = pl.BlockSpec((tm, d), lambda i: (i, 0))
    gcol = lambda k: pl.BlockSpec((tm, d), lambda i, k=k: (i, GATE_OFF // d + k))
    vec = pl.BlockSpec((1, 3 * d), lambda i: (0, 0))
    actd = jax.ShapeDtypeStruct((s, d), BF16)
    dyc, dyf, dys, dgl, ggb = pl.pallas_call(
        body, name="gate_bwd", grid=(s // tm,),
        in_specs=[rowd, rowd, rowd, rowd, gcol(0), gcol(1), gcol(2), vec],
        out_specs=[rowd, rowd, rowd, pl.BlockSpec((tm, 3 * d), lambda i: (i, 0)), vec],
        out_shape=[actd, actd, actd, jax.ShapeDtypeStruct((s, 3 * d), BF16), jax.ShapeDtypeStruct((1, 3 * d), F32)],
        compiler_params=_params(("arbitrary",)),
    )(dm, yc, yf, ys, proj, proj, proj, gb.reshape(1, 3 * d))
    return dyc, dyf, dys, dgl, ggb.reshape(3 * d)


def _tri_tables(nq, r, order):
    last = lambda qi: (qi + 1) * r - 1
    if order == "k_outer":
        pairs = [(qi, kj) for kj in range(nq * r) for qi in range(kj // r, nq)]
    elif order == "k_desc":
        pairs = [(qi, kj) for qi in range(nq) for kj in range(last(qi), -1, -1)]
    else:
        pairs = [(qi, kj) for qi in range(nq) for kj in range(last(qi) + 1)]
    qs, ks = zip(*pairs)
    return jnp.asarray(np.array(qs, np.int32)), jnp.asarray(np.array(ks, np.int32)), len(pairs)


def _lo_mask(shape):
    return lax.broadcasted_iota(jnp.int32, shape, len(shape) - 1) < HEAD_DIM


def _head(x, h):
    lo = _lo_mask(x.shape)
    return jnp.where(lo if h == 0 else jnp.logical_not(lo), x, jnp.zeros_like(x))


def _pair_cols(x, h):
    return x[:, h * HEAD_DIM:h * HEAD_DIM + 1]


def _rep(a0, a1, shape):
    return jnp.where(_lo_mask(shape), a0, a1)


def _positions(qi, kj, tq, tk):
    row = qi * tq + lax.broadcasted_iota(jnp.int32, (tq, tk), 0)
    col = kj * tk + lax.broadcasted_iota(jnp.int32, (tq, tk), 1)
    return row, col


def _head_norm(x, g):
    lo = _lo_mask(x.shape)
    sq = x * x
    s0 = jnp.sum(jnp.where(lo, sq, 0.0), axis=-1, keepdims=True)
    s1 = jnp.sum(jnp.where(lo, 0.0, sq), axis=-1, keepdims=True)
    r = jnp.where(lo, lax.rsqrt(s0 / HEAD_DIM + NORM_EPS), lax.rsqrt(s1 / HEAD_DIM + NORM_EPS))
    return x * r, r


def _fox_prep(proj, gq, gk, tm=512):
    s = proj.shape[0]
    tm = min(tm, s)
    off = FOX_OFF // LANES

    def body(q_ref, k_ref, gq_ref, gk_ref, qn_ref, kn_ref):
        qh, _ = _head_norm(q_ref[...].astype(F32), None)
        kh, _ = _head_norm(k_ref[...].astype(F32), None)
        qn_ref[...] = (qh * gq_ref[...] * QK_SCALE).astype(BF16)
        kn_ref[...] = (kh * gk_ref[...]).astype(BF16)

    vec = pl.BlockSpec((1, LANES), lambda p, i: (0, 0))
    own = pl.BlockSpec((tm, LANES), lambda p, i: (i, p))
    act = jax.ShapeDtypeStruct((s, WIDTH), BF16)
    return pl.pallas_call(
        body, name="fox_prep", grid=(N_PAIR, s // tm),
        in_specs=[pl.BlockSpec((tm, LANES), lambda p, i: (i, off + p)),
                  pl.BlockSpec((tm, LANES), lambda p, i: (i, off + N_PAIR + p)), vec, vec],
        out_specs=[own, own], out_shape=[act, act],
        compiler_params=_params(("parallel", "parallel")),
    )(proj, proj, jnp.tile(gq, 2).reshape(1, LANES), jnp.tile(gk, 2).reshape(1, LANES))


def _fox_post(dqs, dkn, proj, gq, gk, tm=512):
    s = proj.shape[0]
    tm = min(tm, s)
    off = FOX_OFF // LANES

    def one(d_ref, x_ref, g_ref, scale, dx_ref, gg_ref, first):
        xhat, r = _head_norm(x_ref[...].astype(F32), None)
        dy = d_ref[...] * scale
        part = jnp.sum(dy * xhat, axis=0, keepdims=True)

        @pl.when(first)
        def _():
            gg_ref[...] = part

        @pl.when(jnp.logical_not(first))
        def _():
            gg_ref[...] += part

        dxh = dy * g_ref[...]
        lo = _lo_mask(dxh.shape)
        pr = dxh * xhat
        m0 = jnp.sum(jnp.where(lo, pr, 0.0), axis=-1, keepdims=True)
        m1 = jnp.sum(jnp.where(lo, 0.0, pr), axis=-1, keepdims=True)
        mean = jnp.where(lo, m0, m1) / HEAD_DIM
        dx_ref[...] = (r * (dxh - xhat * mean)).astype(BF16)

    def body(dq_ref, dk_ref, q_ref, k_ref, gq_ref, gk_ref, dxq_ref, dxk_ref, ggq_ref, ggk_ref):
        first = pl.program_id(1) == 0
        one(dq_ref, q_ref, gq_ref, QK_SCALE, dxq_ref, ggq_ref, first)
        one(dk_ref, k_ref, gk_ref, 1.0, dxk_ref, ggk_ref, first)

    vec = pl.BlockSpec((1, LANES), lambda p, i: (0, 0))
    own = pl.BlockSpec((tm, LANES), lambda p, i: (i, p))
    ggs = pl.BlockSpec((None, 1, LANES), lambda p, i: (p, 0, 0))
    act = jax.ShapeDtypeStruct((s, WIDTH), BF16)
    ggo = jax.ShapeDtypeStruct((N_PAIR, 1, LANES), F32)
    dxq, dxk, ggq, ggk = pl.pallas_call(
        body, name="fox_post", grid=(N_PAIR, s // tm),
        in_specs=[own, own, pl.BlockSpec((tm, LANES), lambda p, i: (i, off + p)),
                  pl.BlockSpec((tm, LANES), lambda p, i: (i, off + N_PAIR + p)), vec, vec],
        out_specs=[own, own, ggs, ggs], out_shape=[act, act, ggo, ggo],
        compiler_params=_params(("parallel", "arbitrary")),
    )(dqs, dkn, proj, proj, jnp.tile(gq, 2).reshape(1, LANES), jnp.tile(gk, 2).reshape(1, LANES))
    fold = lambda a: a.reshape(N_HEADS, HEAD_DIM).sum(axis=0)
    return dxq, dxk, fold(ggq), fold(ggk)


def _split3(x):
    a = x.astype(BF16)
    r = x - a.astype(F32)
    b = r.astype(BF16)
    c = (r - b.astype(F32)).astype(BF16)
    return a, b, c


def _split2(x):
    a = x.astype(BF16)
    b = (x - a.astype(F32)).astype(BF16)
    return a, b


def _log_sigmoid(x):
    return jnp.minimum(x, 0.0) - jnp.log(1.0 + jnp.exp(-jnp.abs(x)))


def _fox_gates(ft, bias):
    h, s = ft.shape
    nb = s // LANES

    def body(f_ref, b_ref, c_ref):
        lf = _log_sigmoid(f_ref[...] + b_ref[...])
        i = lax.broadcasted_iota(jnp.int32, (s, LANES), 0)
        j = pl.program_id(0) * LANES + lax.broadcasted_iota(jnp.int32, (s, LANES), 1)
        tri = jnp.where(i <= j, 1.0, 0.0).astype(BF16)
        c_ref[...] = sum(_dot(p, tri, _NN) for p in _split3(lf))

    return pl.pallas_call(
        body, name="fox_gates", grid=(nb,),
        in_specs=[pl.BlockSpec((h, s), lambda j: (0, 0)), pl.BlockSpec((h, 1), lambda j: (0, 0))],
        out_specs=pl.BlockSpec((h, LANES), lambda j: (0, j)),
        out_shape=jax.ShapeDtypeStruct((h, s), F32),
        compiler_params=_params(("parallel",)),
    )(ft, bias.reshape(h, 1))


def _fox_gates_bwd(dc, ft, bias):
    h, s = ft.shape
    nb = s // LANES

    def body(dc_ref, f_ref, fb_ref, b_ref, df_ref, gb_ref):
        i = lax.broadcasted_iota(jnp.int32, (s, LANES), 0)
        j = pl.program_id(0) * LANES + lax.broadcasted_iota(jnp.int32, (s, LANES), 1)
        tri = jnp.where(i >= j, 1.0, 0.0).astype(BF16)
        dlf = sum(_dot(p, tri, _NN) for p in _split3(dc_ref[...]))
        df = dlf * jax.nn.sigmoid(-(fb_ref[...] + b_ref[...]))
        df_ref[...] = df
        part = jnp.sum(df, axis=-1, keepdims=True)

        @pl.when(pl.program_id(0) == 0)
        def _():
            gb_ref[...] = part

        @pl.when(pl.program_id(0) > 0)
        def _():
            gb_ref[...] += part

    full = pl.BlockSpec((h, s), lambda j: (0, 0))
    blk = pl.BlockSpec((h, LANES), lambda j: (0, j))
    one = pl.BlockSpec((h, 1), lambda j: (0, 0))
    df, gb = pl.pallas_call(
        body, name="fox_gates_bwd", grid=(nb,), in_specs=[full, full, blk, one], out_specs=[blk, one],
        out_shape=[jax.ShapeDtypeStruct((h, s), F32), jax.ShapeDtypeStruct((h, 1), F32)],
        compiler_params=_params(("arbitrary",)),
    )(dc, ft, ft, bias.reshape(h, 1))
    return df, gb.reshape(h)


def _delta_rep(do, o, tm=512):
    s = do.shape[0]
    tm = min(tm, s)

    def body(do_ref, o_ref, d_ref):
        pr = do_ref[...].astype(F32) * o_ref[...].astype(F32)
        lo = _lo_mask(pr.shape)
        d0 = jnp.sum(jnp.where(lo, pr, 0.0), axis=-1, keepdims=True)
        d1 = jnp.sum(jnp.where(lo, 0.0, pr), axis=-1, keepdims=True)
        d_ref[...] = jnp.where(lo, d0, d1)

    own = pl.BlockSpec((tm, LANES), lambda p, i: (i, p))
    return pl.pallas_call(
        body, name="delta_rep", grid=(N_PAIR, s // tm), in_specs=[own, own],
        out_specs=pl.BlockSpec((None, tm, LANES), lambda p, i: (p, i, 0)),
        out_shape=jax.ShapeDtypeStruct((N_PAIR, s, LANES), F32),
        compiler_params=_params(("parallel", "parallel")),
    )(do, o)


def _tile(s, t):
    t = min(t, s)
    assert s % t == 0
    return t


def _fox_fwd(qn, kn, proj, ccol, crow, t=512, comm=None):
    s = qn.shape[0]
    t = _tile(s, t)
    n = s // t
    qtab, ktab, ntri = _tri_tables(n, 1, "k_asc")
    voff = FOX_OFF // LANES + 2 * N_PAIR
    n_in, n_out = (len(comm.inputs), len(comm.out_shapes)) if comm else (0, 0)

    def body(qt_ref, kt_ref, q_ref, k_ref, v_ref, cc_ref, cr_ref, *rest):
        comm_in, rest = rest[:n_in], rest[n_in:]
        (o_ref, lse_ref), rest = rest[:2], rest[2:]
        comm_out, rest = rest[:n_out], rest[n_out:]
        (m_ref, l_ref, acc_ref), sems = rest[:3], rest[3:]
        i = pl.program_id(1)
        qi, kj = qt_ref[i], kt_ref[i]
        if comm:
            pl.when((pl.program_id(0) == 0) & (i == 0))(lambda: comm.start(comm_in, comm_out, sems))

        @pl.when(kj == 0)
        def _():
            m_ref[...] = jnp.full(m_ref.shape, NEG_INF, F32)
            l_ref[...] = jnp.zeros(l_ref.shape, F32)
            acc_ref[...] = jnp.zeros(acc_ref.shape, F32)

        q, k, v = q_ref[...], k_ref[...], v_ref[...]
        row, col = _positions(qi, kj, t, t)
        causal = col <= row
        m_old = m_ref[...]
        mn, rs, pv = [], [], []
        for h in range(2):
            sc = _dot(_head(q, h), k, _NT) + _pair_cols(cc_ref[...], h) - cr_ref[h:h + 1, :]
            sc = jnp.where(causal, sc, NEG_INF)
            m_new = jnp.maximum(_pair_cols(m_old, h), jnp.max(sc, axis=-1, keepdims=True))
            p = jnp.exp(sc - m_new)
            mn.append(m_new)
            rs.append(jnp.sum(p, axis=-1, keepdims=True))
            pv.append(_dot(p.astype(BF16), _head(v, h), _NN))
        m_rep = _rep(mn[0], mn[1], m_old.shape)
        alpha = jnp.exp(m_old - m_rep)
        l_ref[...] = alpha * l_ref[...] + _rep(rs[0], rs[1], m_old.shape)
        acc_ref[...] = alpha * acc_ref[...] + pv[0] + pv[1]
        m_ref[...] = m_rep

        @pl.when(kj == qi)
        def _():
            o_ref[...] = acc_ref[...] / l_ref[...]
            lse_ref[...] = m_ref[...] + jnp.log(l_ref[...])

        if comm:
            pl.when((pl.program_id(0) == N_PAIR - 1) & (i == ntri - 1))(lambda: comm.finish(comm_in, comm_out, sems))

    grid_spec = pltpu.PrefetchScalarGridSpec(
        num_scalar_prefetch=2, grid=(N_PAIR, ntri),
        in_specs=[pl.BlockSpec((t, LANES), lambda p, i, qt, kt: (qt[i], p)),
                  pl.BlockSpec((t, LANES), lambda p, i, qt, kt: (kt[i], p)),
                  pl.BlockSpec((t, LANES), lambda p, i, qt, kt: (kt[i], voff + p)),
                  pl.BlockSpec((None, t, LANES), lambda p, i, qt, kt: (p, qt[i], 0)),
                  pl.BlockSpec((None, 2, t), lambda p, i, qt, kt: (p, 0, kt[i]))] + [ANY] * n_in,
        out_specs=[pl.BlockSpec((t, LANES), lambda p, i, qt, kt: (qt[i], p)),
                   pl.BlockSpec((None, t, LANES), lambda p, i, qt, kt: (p, qt[i], 0))] + [ANY] * n_out,
        scratch_shapes=[pltpu.VMEM((t, LANES), F32)] * 3 + (comm.sems if comm else []))
    outs = pl.pallas_call(
        body, name="fox_fwd", grid_spec=grid_spec,
        out_shape=[jax.ShapeDtypeStruct((s, WIDTH), F32), jax.ShapeDtypeStruct((N_PAIR, s, LANES), F32)]
        + (comm.out_shapes if comm else []),
        compiler_params=_params(("arbitrary", "arbitrary") if comm else ("parallel", "arbitrary")),
    )(qtab, ktab, qn, kn, proj, ccol, crow, *(comm.inputs if comm else []))
    return outs[:2], outs[2:]


def _fox_bwd(qn, kn, proj, do, lse, delta, ccol, crow, t=512, comm=None):
    s = qn.shape[0]
    t = _tile(s, t)
    n = s // t
    qtab, ktab, ntri = _tri_tables(n, 1, "k_outer")
    voff = FOX_OFF // LANES + 2 * N_PAIR
    n_in, n_out = (len(comm.inputs), len(comm.out_shapes)) if comm else (0, 0)

    def body(qt_ref, kt_ref, q_ref, k_ref, v_ref, do_ref, lse_ref, dl_ref, cc_ref, cr_ref, *rest):
        comm_in, rest = rest[:n_in], rest[n_in:]
        (dq_ref, dk_ref, dv_ref, dc_ref, dcq_ref), rest = rest[:5], rest[5:]
        comm_out, rest = rest[:n_out], rest[n_out:]
        (dka_ref, dva_ref, dca_ref, dcqa_ref), sems = rest[:4], rest[4:]
        i = pl.program_id(1)
        qi, kj = qt_ref[i], kt_ref[i]
        if comm:
            pl.when((pl.program_id(0) == 0) & (i == 0))(lambda: comm.start(comm_in, comm_out, sems))

        @pl.when(i == 0)
        def _():
            dq_ref[...] = jnp.zeros(dq_ref.shape, F32)
            dcqa_ref[...] = jnp.zeros(dcqa_ref.shape, F32)

        @pl.when(qi == kj)
        def _():
            dka_ref[...] = jnp.zeros(dka_ref.shape, F32)
            dva_ref[...] = jnp.zeros(dva_ref.shape, F32)
            dca_ref[...] = jnp.zeros(dca_ref.shape, F32)

        q, k, v, g = q_ref[...], k_ref[...], v_ref[...], do_ref[...]
        row, col = _positions(qi, kj, t, t)
        causal = col <= row
        dq = jnp.zeros((t, LANES), F32)
        dk = jnp.zeros((t, LANES), F32)
        dv = jnp.zeros((t, LANES), F32)
        rowsum = []
        for h in range(2):
            qh, gh = _head(q, h), _head(g, h)
            sc = _dot(qh, k, _NT) + _pair_cols(cc_ref[...], h) - cr_ref[h:h + 1, :]
            p = jnp.where(causal, jnp.exp(sc - _pair_cols(lse_ref[...], h)), 0.0)
            dp = _dot(gh, v, _NT)
            ds = p * (dp - _pair_cols(dl_ref[...], h))
            dsb = ds.astype(BF16)
            dv = dv + _dot(p.astype(BF16), gh, _TN)
            dk = dk + _dot(dsb, qh, _TN)
            dq = dq + _dot(dsb, _head(k, h), _NN)
            dca_ref[h:h + 1, :] -= jnp.sum(ds, axis=0, keepdims=True)
            rowsum.append(jnp.sum(ds, axis=-1, keepdims=True))
        dka_ref[...] += dk
        dva_ref[...] += dv
        rows = pl.ds(pl.multiple_of(qi * t, t), t)
        dq_ref[rows, :] += dq
        dcqa_ref[rows, :] += _rep(rowsum[0], rowsum[1], (t, LANES))

        @pl.when(qi == n - 1)
        def _():
            dk_ref[...] = dka_ref[...]
            dv_ref[...] = dva_ref[...].astype(BF16)
            dc_ref[...] = dca_ref[...]

        @pl.when(i == ntri - 1)
        def _():
            across = dcqa_ref[...].T
            dcq_ref[0:1, :] = across[0:1, :]
            dcq_ref[1:2, :] = across[HEAD_DIM:HEAD_DIM + 1, :]

        if comm:
            pl.when((pl.program_id(0) == N_PAIR - 1) & (i == ntri - 1))(lambda: comm.finish(comm_in, comm_out, sems))

    qblk = lambda p, i, qt, kt: (qt[i], p)
    kblk = lambda p, i, qt, kt: (kt[i], p)
    qrep = pl.BlockSpec((None, t, LANES), lambda p, i, qt, kt: (p, qt[i], 0))
    crs = pl.BlockSpec((None, 2, t), lambda p, i, qt, kt: (p, 0, kt[i]))
    grid_spec = pltpu.PrefetchScalarGridSpec(
        num_scalar_prefetch=2, grid=(N_PAIR, ntri),
        in_specs=[pl.BlockSpec((t, LANES), qblk), pl.BlockSpec((t, LANES), kblk),
                  pl.BlockSpec((t, LANES), lambda p, i, qt, kt: (kt[i], voff + p)),
                  pl.BlockSpec((t, LANES), qblk), qrep, qrep, qrep, crs] + [ANY] * n_in,
        out_specs=[pl.BlockSpec((s, LANES), lambda p, i, qt, kt: (0, p)),
                   pl.BlockSpec((t, LANES), kblk), pl.BlockSpec((t, LANES), kblk), crs,
                   pl.BlockSpec((None, 2, s), lambda p, i, qt, kt: (p, 0, 0))] + [ANY] * n_out,
        scratch_shapes=[pltpu.VMEM((t, LANES), F32), pltpu.VMEM((t, LANES), F32), pltpu.VMEM((2, t), F32),
                        pltpu.VMEM((s, LANES), F32)] + (comm.sems if comm else []))
    outs = pl.pallas_call(
        body, name="fox_bwd", grid_spec=grid_spec,
        out_shape=[jax.ShapeDtypeStruct((s, WIDTH), F32), jax.ShapeDtypeStruct((s, WIDTH), F32),
                   jax.ShapeDtypeStruct((s, WIDTH), BF16), jax.ShapeDtypeStruct((N_PAIR, 2, s), F32),
                   jax.ShapeDtypeStruct((N_PAIR, 2, s), F32)] + (comm.out_shapes if comm else []),
        compiler_params=_params(("arbitrary", "arbitrary") if comm else ("parallel", "arbitrary")),
    )(qtab, ktab, qn, kn, proj, do, lse, delta, ccol, crow, *(comm.inputs if comm else []))
    return outs[:5], outs[5:]


def _sb_tile(qh, k, strict, tk, r_col):
    z = _dot(qh, k, _NT)
    lg = jnp.where(strict, -(jnp.maximum(z, 0.0) + jnp.log(1.0 + jnp.exp(-jnp.abs(z)))), 0.0)
    jj = lax.broadcasted_iota(jnp.int32, (tk, tk), 0)
    ss = lax.broadcasted_iota(jnp.int32, (tk, tk), 1)
    above = jnp.where(jj > ss, 1.0, 0.0).astype(BF16)
    suffix = sum(_dot(p, above, _NN) for p in _split2(lg)) + r_col
    a = jnp.where(strict, jnp.exp(lg + z + suffix), 0.0)
    return z, lg, a


def _sb_fwd(proj, tq=512, tk=256, comm=None):
    s = proj.shape[0]
    tq = _tile(s, tq)
    tk = _tile(tq, tk)
    nq, r = s // tq, tq // tk
    qtab, ktab, ntri = _tri_tables(nq, r, "k_desc")
    off = SB_OFF // LANES
    n_in, n_out = (len(comm.inputs), len(comm.out_shapes)) if comm else (0, 0)

    def body(qt_ref, kt_ref, q_ref, k_ref, v_ref, *rest):
        comm_in, o_ref, rest = rest[:n_in], rest[n_in], rest[n_in + 1:]
        comm_out, rest = rest[:n_out], rest[n_out:]
        (acc_ref, r_ref), sems = rest[:2], rest[2:]
        i = pl.program_id(1)
        qi, kj = qt_ref[i], kt_ref[i]
        if comm:
            pl.when((pl.program_id(0) == 0) & (i == 0))(lambda: comm.start(comm_in, comm_out, sems))

        @pl.when(kj == (qi + 1) * r - 1)
        def _():
            acc_ref[...] = jnp.zeros(acc_ref.shape, F32)
            r_ref[...] = jnp.zeros(r_ref.shape, F32)

        q = q_ref[...] * QK_SCALE
        k, v = k_ref[...], v_ref[...]
        row, col = _positions(qi, kj, tq, tk)
        strict = col < row
        acc = acc_ref[...]
        for h in range(2):
            _, lg, a = _sb_tile(_head(q, h), k, strict, tk, r_ref[h])
            acc = acc + _dot(a.astype(BF16), _head(v, h), _NN)
            r_ref[h] += jnp.sum(lg, axis=-1, keepdims=True)
        acc_ref[...] = acc

        @pl.when(kj == 0)
        def _():
            o_ref[...] = acc_ref[...]

        if comm:
            pl.when((pl.program_id(0) == N_PAIR - 1) & (i == ntri - 1))(lambda: comm.finish(comm_in, comm_out, sems))

    grid_spec = pltpu.PrefetchScalarGridSpec(
        num_scalar_prefetch=2, grid=(N_PAIR, ntri),
        in_specs=[pl.BlockSpec((tq, LANES), lambda p, i, qt, kt: (qt[i], off + p)),
                  pl.BlockSpec((tk, LANES), lambda p, i, qt, kt: (kt[i], off + N_PAIR + p)),
                  pl.BlockSpec((tk, LANES), lambda p, i, qt, kt: (kt[i], off + 2 * N_PAIR + p))] + [ANY] * n_in,
        out_specs=[pl.BlockSpec((tq, LANES), lambda p, i, qt, kt: (qt[i], p))] + [ANY] * n_out,
        scratch_shapes=[pltpu.VMEM((tq, LANES), F32), pltpu.VMEM((2, tq, 1), F32)] + (comm.sems if comm else []))
    outs = pl.pallas_call(
        body, name="sb_fwd", grid_spec=grid_spec,
        out_shape=[jax.ShapeDtypeStruct((s, WIDTH), F32)] + (comm.out_shapes if comm else []),
        compiler_params=_params(("arbitrary", "arbitrary") if comm else ("parallel", "arbitrary")),
    )(qtab, ktab, proj, proj, proj, *(comm.inputs if comm else []))
    return outs[0], outs[1:]


def _sb_bwd(proj, do, delta, tq=512, tk=256, comm=None):
    s = proj.shape[0]
    tq = _tile(s, tq)
    tk = _tile(tq, tk)
    nq, r = s // tq, tq // tk
    qtab, ktab, ntri = _tri_tables(nq, r, "k_desc")
    off = SB_OFF // LANES
    n_in, n_out = (len(comm.inputs), len(comm.out_shapes)) if comm else (0, 0)

    def body(qt_ref, kt_ref, q_ref, k_ref, v_ref, do_ref, dl_ref, *rest):
        comm_in, rest = rest[:n_in], rest[n_in:]
        (dq_ref, dk_ref, dv_ref), rest = rest[:3], rest[3:]
        comm_out, rest = rest[:n_out], rest[n_out:]
        (dqa_ref, dka_ref, dva_ref, r_ref, rd_ref), sems = rest[:5], rest[5:]
        i = pl.program_id(1)
        qi, kj = qt_ref[i], kt_ref[i]
        if comm:
            pl.when((pl.program_id(0) == 0) & (i == 0))(lambda: comm.start(comm_in, comm_out, sems))

        @pl.when(i == 0)
        def _():
            dka_ref[...] = jnp.zeros(dka_ref.shape, F32)
            dva_ref[...] = jnp.zeros(dva_ref.shape, F32)

        @pl.when(kj == (qi + 1) * r - 1)
        def _():
            dqa_ref[...] = jnp.zeros(dqa_ref.shape, F32)
            r_ref[...] = jnp.zeros(r_ref.shape, F32)
            rd_ref[...] = jnp.zeros(rd_ref.shape, F32)

        q = q_ref[...] * QK_SCALE
        k, v, g = k_ref[...], v_ref[...], do_ref[...]
        row, col = _positions(qi, kj, tq, tk)
        strict = col < row
        ss = lax.broadcasted_iota(jnp.int32, (tk, tk), 0)
        jj = lax.broadcasted_iota(jnp.int32, (tk, tk), 1)
        at_or_after = jnp.where(ss >= jj, 1.0, 0.0).astype(BF16)
        dq = jnp.zeros((tq, LANES), F32)
        dk = jnp.zeros((tk, LANES), F32)
        dv = jnp.zeros((tk, LANES), F32)
        for h in range(2):
            qh, gh = _head(q, h), _head(g, h)
            z, lg, a = _sb_tile(qh, k, strict, tk, r_ref[h])
            da = _dot(gh, v, _NT)
            ab = a.astype(BF16)
            d_a = da * ab.astype(F32)
            incl = _dot(d_a.astype(BF16), at_or_after, _NN)
            d_l = _pair_cols(dl_ref[...], h) - rd_ref[h] - incl
            sig = jnp.exp(lg + z)
            dz = jnp.where(strict, d_a * (1.0 - sig) - d_l * sig, 0.0).astype(BF16)
            dq = dq + _dot(dz, _head(k, h), _NN)
            dk = dk + _dot(dz, qh, _TN)
            dv = dv + _dot(ab, gh, _TN)
            r_ref[h] += jnp.sum(lg, axis=-1, keepdims=True)
            rd_ref[h] += jnp.sum(d_a, axis=-1, keepdims=True)
        dqa_ref[...] += dq
        rows = pl.ds(pl.multiple_of(kj * tk, tk), tk)
        dka_ref[rows, :] += dk
        dva_ref[rows, :] += dv

        @pl.when(kj == 0)
        def _():
            dq_ref[...] = (dqa_ref[...] * QK_SCALE).astype(BF16)

        @pl.when(i == ntri - 1)
        def _():
            dk_ref[...] = dka_ref[...].astype(BF16)
            dv_ref[...] = dva_ref[...].astype(BF16)

        if comm:
            pl.when((pl.program_id(0) == N_PAIR - 1) & (i == ntri - 1))(lambda: comm.finish(comm_in, comm_out, sems))

    qblk = lambda p, i, qt, kt: (qt[i], p)
    whole = pl.BlockSpec((s, LANES), lambda p, i, qt, kt: (0, p))
    grid_spec = pltpu.PrefetchScalarGridSpec(
        num_scalar_prefetch=2, grid=(N_PAIR, ntri),
        in_specs=[pl.BlockSpec((tq, LANES), lambda p, i, qt, kt: (qt[i], off + p)),
                  pl.BlockSpec((tk, LANES), lambda p, i, qt, kt: (kt[i], off + N_PAIR + p)),
                  pl.BlockSpec((tk, LANES), lambda p, i, qt, kt: (kt[i], off + 2 * N_PAIR + p)),
                  pl.BlockSpec((tq, LANES), qblk),
                  pl.BlockSpec((None, tq, LANES), lambda p, i, qt, kt: (p, qt[i], 0))] + [ANY] * n_in,
        out_specs=[pl.BlockSpec((tq, LANES), qblk), whole, whole] + [ANY] * n_out,
        scratch_shapes=[pltpu.VMEM((tq, LANES), F32), pltpu.VMEM((s, LANES), F32), pltpu.VMEM((s, LANES), F32),
                        pltpu.VMEM((2, tq, 1), F32), pltpu.VMEM((2, tq, 1), F32)] + (comm.sems if comm else []))
    act = jax.ShapeDtypeStruct((s, WIDTH), BF16)
    outs = pl.pallas_call(
        body, name="sb_bwd", grid_spec=grid_spec, out_shape=[act, act, act] + (comm.out_shapes if comm else []),
        compiler_params=_params(("arbitrary", "arbitrary") if comm else ("parallel", "arbitrary")),
    )(qtab, ktab, proj, proj, proj, do, delta, *(comm.inputs if comm else []))
    return outs[:3], outs[3:]


def _loss_head(y, target, tm=256):
    s, d = y.shape
    tm = min(tm, s)

    def body(y_ref, t_ref, l_ref, dy_ref):
        e = y_ref[...] - t_ref[...]
        dy_ref[...] = e / d
        part = jnp.sum(jnp.sum(e * e, axis=0, keepdims=True), axis=1, keepdims=True)

        @pl.when(pl.program_id(0) == 0)
        def _():
            l_ref[...] = jnp.broadcast_to(part, l_ref.shape)

        @pl.when(pl.program_id(0) > 0)
        def _():
            l_ref[...] += jnp.broadcast_to(part, l_ref.shape)

    row = pl.BlockSpec((tm, d), lambda i: (i, 0))
    return pl.pallas_call(
        body, name="loss_head", grid=(s // tm,), in_specs=[row, row],
        out_specs=[pl.BlockSpec((1, LANES), lambda i: (0, 0)), row],
        out_shape=[jax.ShapeDtypeStruct((1, LANES), F32), jax.ShapeDtypeStruct((s, d), F32)],
        compiler_params=_params(("arbitrary",)),
    )(y, target)


def _adamw(w, g, m, v, name):
    def body(w_ref, g_ref, m_ref, v_ref, d_ref, mo_ref, vo_ref):
        gr = g_ref[...]
        mn = ADAM_B1 * m_ref[...] + (1.0 - ADAM_B1) * gr
        vn = ADAM_B2 * v_ref[...] + (1.0 - ADAM_B2) * (gr * gr)
        m_hat = mn / (1.0 - ADAM_B1 ** ADAM_STEP)
        v_hat = vn / (1.0 - ADAM_B2 ** ADAM_STEP)
        d_ref[...] = -ADAM_LR * (m_hat / (jnp.sqrt(v_hat) + ADAM_EPS) + ADAM_WD * w_ref[...])
        mo_ref[...] = mn
        vo_ref[...] = vn

    if w.ndim == 2:
        grid, blk = (1,), pl.BlockSpec(w.shape, lambda i: (0, 0))
    else:
        layers, rows, cols = w.shape
        tm = rows
        for cand in (512, 256, 128, 64, 32, 16, 8):
            if rows % cand == 0 and rows > cand and cand * cols * 4 <= ADAMW_BLOCK_BYTES:
                tm = cand
                break
        grid, blk = (layers, rows // tm), pl.BlockSpec((None, tm, cols), lambda l, i: (l, i, 0))
    out = jax.ShapeDtypeStruct(w.shape, F32)
    return pl.pallas_call(
        body, name=name, grid=grid, in_specs=[blk] * 4, out_specs=[blk] * 3, out_shape=[out] * 3,
        compiler_params=_params(("parallel",) * len(grid)),
    )(w, g, m, v)


W_IN_NATURAL = (("a", CONV_OFF, 3 * WIDTH), ("a", FOX_OFF, 3 * WIDTH), ("f", 0, N_HEADS), ("a", SB_OFF, 3 * WIDTH),
                ("a", GATE_OFF, 3 * D_MODEL))
W_IN_COLS = D_INA + N_HEADS
W_IN_SHARD = W_IN_COLS // 4


def _w_in_from_chips(blocks):
    def natural(lo, hi):
        cols = []
        for k, blk in enumerate(blocks):
            a, b = max(lo, k * W_IN_SHARD), min(hi, (k + 1) * W_IN_SHARD)
            if a < b:
                cols.append(blk[:, a - k * W_IN_SHARD:b - k * W_IN_SHARD])
        return cols

    start = {}
    pos = 0
    for part, at, length in W_IN_NATURAL:
        start[part, at] = (pos, pos + length)
        pos += length
    order = sorted((at, rng) for (part, at), rng in start.items() if part == "a")
    wa = jnp.concatenate([c for _, (lo, hi) in order for c in natural(lo, hi)], axis=1)
    wf = jnp.pad(jnp.concatenate(natural(*start["f", 0]), axis=1), ((0, 0), (0, LANES - N_HEADS)))
    return wa, wf


def _w_in_to_chips(ga, gf):
    chips = []
    for k in range(4):
        lo, hi, pos, cols = k * W_IN_SHARD, (k + 1) * W_IN_SHARD, 0, []
        for part, at, length in W_IN_NATURAL:
            a, b = max(lo, pos), min(hi, pos + length)
            if a < b:
                cols.append((ga if part == "a" else gf)[:, at + a - pos:at + b - pos])
            pos += length
        chips.append(jnp.concatenate(cols, axis=1))
    return jnp.stack(chips)


def _layer_fwd(x, p, plan, l):
    s = x.shape[0]
    proj, hn1 = _norm_mm(x, p["norm1_g"], p["wa"], BF16, name="in_proj")
    fraw, _ = _norm_mm(x, p["norm1_g"], p["wf"], F32, name="in_proj_f")
    ft = fraw[:, :N_HEADS].T
    crow8 = _fox_gates(ft, p["fox_f_bias"])
    crow = crow8.reshape(N_PAIR, 2, s)
    ccol = jnp.repeat(crow.transpose(0, 2, 1), HEAD_DIM, axis=2)
    oc = _conv_mix_fwd(proj, p["conv_w"])
    qn, kn = _fox_prep(proj, p["fox_q_norm_g"], p["fox_k_norm_g"])
    last = l + 1 == plan.depth
    (of, lse), got_in = _fox_fwd(qn, kn, proj, ccol, crow, comm=None if last else plan.gather(l + 1, BIG_IN))
    osb, got_rest = _sb_fwd(proj, comm=plan.gather(l, BIG_REST))
    p.update(plan.weights(l, BIG_REST, got_rest))
    nxt = None if last else plan.layer(l + 1, got_in)
    xm, merged, yc, yf, ys = _merge_fwd(x, oc, of, osb, p["w_proj_conv"], p["w_proj_fox"], p["w_proj_sb"], proj,
                                        p["gate_bias"], p["w_out"])
    up, hn2 = _norm_mm(xm, p["norm2_g"], p["w_up"], BF16, name="up_proj")
    hh = _glu_fwd(up, p["ffn_conv_w"], p["ffn_conv_b"])
    xo = _mm(hh, p["w_down"], "nn", F32, res=xm, name="down_proj")
    saved = dict(x=x, hn1=hn1, proj=proj, ft=ft, crow=crow, ccol=ccol, oc=oc, qn=qn, kn=kn, of=of, lse=lse, osb=osb,
                 merged=merged, yc=yc, yf=yf, ys=ys, xm=xm, hn2=hn2, up=up, hh=hh)
    return xo, saved, nxt


def _layer_bwd(dx, p, a, plan, pending_in, first_layer):
    s = dx.shape[0]
    g = {}
    dhh = _mm(dx, p["w_down"], "nt", BF16, tn=1408, name="d_down_in")
    g["w_down"] = _mm(a["hh"], dx, "tn", F32, name="g_w_down")
    dug, duv, g["ffn_conv_w"], g["ffn_conv_b"] = _glu_bwd(dhh, a["up"], p["ffn_conv_w"], p["ffn_conv_b"])
    dup = jnp.concatenate([dug, duv], axis=1)
    g["w_up"] = _mm(a["hn2"], dup, "tn", F32, tn=1408, name="g_w_up")
    dhn2 = _mm(dup, p["w_up"], "nt", F32, name="d_up_in")
    dx, g["norm2_g"] = _norm_bwd(dhn2, a["xm"], p["norm2_g"], dx, name="norm2_bwd")
    dm = _mm(dx, p["w_out"], "nt", BF16, name="d_out_in")
    g["w_out"] = _mm(a["merged"], dx, "tn", F32, tn=512, name="g_w_out")
    dyc, dyf, dys, dgl, g["gate_bias"] = _gate_bwd(dm, a["yc"], a["yf"], a["ys"], a["proj"], p["gate_bias"])
    doc = _mm(dyc, p["w_proj_conv"], "nt", BF16, name="d_pc_in")
    dof = _mm(dyf, p["w_proj_fox"], "nt", BF16, name="d_pf_in")
    dos = _mm(dys, p["w_proj_sb"], "nt", BF16, name="d_ps_in")
    g["w_proj_conv"] = _mm(a["oc"], dyc, "tn", F32, name="g_w_pc")
    g["w_proj_fox"] = _mm(a["of"], dyf, "tn", F32, name="g_w_pf")
    g["w_proj_sb"] = _mm(a["osb"], dys, "tn", F32, name="g_w_ps")
    dcb, dcc, dch, g["conv_w"] = _conv_mix_bwd(doc, a["proj"], p["conv_w"])
    delta_f = _delta_rep(dof, a["of"])
    pending = plan.reduction(pending_in, {k: g[k] for k in BIG_REST})
    (dqs, dkn, dfv, dcrow, dcq), recv_a = _fox_bwd(a["qn"], a["kn"], a["proj"], dof, a["lse"], delta_f, a["ccol"], a["crow"],
                                                   comm=pending.exchange() if pending else None)
    dc = (dcrow + dcq).reshape(N_HEADS, s)
    dfq, dfk, g["fox_q_norm_g"], g["fox_k_norm_g"] = _fox_post(dqs, dkn, a["proj"], p["fox_q_norm_g"], p["fox_k_norm_g"])
    dft, g["fox_f_bias"] = _fox_gates_bwd(dc, a["ft"], p["fox_f_bias"])
    delta_s = _delta_rep(dos, a["osb"])
    (dsq, dsk, dsv), recv_b = _sb_bwd(a["proj"], dos, delta_s, comm=pending.sums(recv_a) if pending else None)
    done_in = None
    if pending:
        done = pending.finish(recv_b)
        if pending_in is not None:
            done_in, done = done[0], done[1:]
        g.update(zip(BIG_REST, done))
    dproj = jnp.concatenate([dgl, dcb, dcc, dch, dfq, dfk, dfv, dsq, dsk, dsv], axis=1)
    dfp = jnp.pad(dft.T, ((0, 0), (0, LANES - N_HEADS))).astype(BF16)
    ga = _mm(a["hn1"], dproj, "tn", F32, tn=768, name="g_w_in")
    gf = _mm(a["hn1"], dfp, "tn", F32, name="g_w_in_f")
    g["w_in"] = _w_in_to_chips(ga, gf)
    dhn1 = _mm(dfp, p["wf"], "nt", F32, name="d_in_f")
    own = plan.reduction(g["w_in"], {}) if first_layer else None
    if own:
        second = own.sums(_run_comm(own.exchange(), "rs_pair_exchange"))
        dhn1, recv_b = _mm(dproj, p["wa"], "nt", F32, res=dhn1, tk=1536, name="d_in", comm=second)
        (g["w_in"],) = own.finish(recv_b)
    else:
        dhn1 = _mm(dproj, p["wa"], "nt", F32, res=dhn1, tk=1536, name="d_in")
    dx, g["norm1_g"] = _norm_bwd(dhn1, a["x"], p["norm1_g"], dx, name="norm1_bwd")
    return dx, g, done_in


MATMUL_WEIGHTS = ("w_in", "w_proj_conv", "w_proj_fox", "w_proj_sb", "w_out", "w_up", "w_down")
WEIGHTS = ("norm1_g", "w_in", "fox_f_bias", "gate_bias", "conv_w", "fox_q_norm_g", "fox_k_norm_g", "w_proj_conv",
           "w_proj_fox", "w_proj_sb", "w_out", "norm2_g", "w_up", "ffn_conv_w", "ffn_conv_b", "w_down")


def _local_step(x, target, plan):
    depth = plan.depth
    layers, saved = [plan.layer(0, None)], []
    for l in range(depth):
        x, a, nxt = _layer_fwd(x, layers[l], plan, l)
        saved.append(a)
        if nxt is not None:
            layers.append(nxt)
    sq, dx = _loss_head(x, target)
    grads, pending_in = [None] * depth, None
    for l in reversed(range(depth)):
        dx, grads[l], done_in = _layer_bwd(dx, layers[l], saved[l], plan, pending_in, l == 0)
        if done_in is not None:
            grads[l + 1]["w_in"] = done_in
        pending_in = grads[l]["w_in"]
    return sq, dx, grads


ANY = pl.BlockSpec(memory_space=pl.ANY)


def _place():
    x, y, c = lax.axis_index("x"), lax.axis_index("y"), lax.axis_index("c")
    chips = [(1 - x, y), (x, 1 - y), (1 - x, 1 - y)]
    return x, y, c, chips


_Comm = collections.namedtuple("_Comm", "inputs out_shapes sems start finish")


def _run_comm(comm, name):
    n_in, n_out = len(comm.inputs), len(comm.out_shapes)

    def body(*refs):
        ins, outs, sems = refs[:n_in], refs[n_in:n_in + n_out], refs[n_in + n_out:]
        comm.start(ins, outs, sems)
        comm.finish(ins, outs, sems)

    return pl.pallas_call(body, name=name, in_specs=[ANY] * n_in, out_specs=[ANY] * n_out, out_shape=comm.out_shapes,
                          scratch_shapes=comm.sems)(*comm.inputs)


def _gather_comm(shards):
    n = len(shards)

    def copy(x_refs, out_refs, sems, k, t, chip_index, which_half, to, from_input=False):
        half = shards[t].shape[0] // 2
        rows = pl.ds(which_half * half, half)
        dst = out_refs[t].at[chip_index, rows, :]
        return pltpu.make_async_remote_copy(
            src_ref=x_refs[t].at[rows, :] if from_input else dst, dst_ref=dst,
            send_sem=sems[0].at[k, t], recv_sem=sems[1].at[k, t], device_id=to, device_id_type=MESH)

    def first(x_refs, out_refs, sems):
        x, y, c, chips = _place()
        return [copy(x_refs, out_refs, sems, p, t, 2 * x + y, c, (*chip, c), from_input=True)
                for t in range(n) for p, chip in enumerate(chips)]

    def start(x_refs, out_refs, sems):
        for cp in first(x_refs, out_refs, sems):
            cp.start()

    def finish(x_refs, out_refs, sems):
        x, y, c, chips = _place()
        passed = []
        for t in range(n):
            for p, chip in enumerate(chips):
                copy(x_refs, out_refs, sems, p, t, 2 * chip[0] + chip[1], c, (x, y, c)).wait_recv()
                fwd = copy(x_refs, out_refs, sems, 3 + p, t, 2 * chip[0] + chip[1], c, (x, y, 1 - c))
                fwd.start()
                passed.append(fwd)
        for t in range(n):
            for p, chip in enumerate(chips):
                copy(x_refs, out_refs, sems, 3 + p, t, 2 * chip[0] + chip[1], 1 - c, (x, y, c)).wait_recv()
        for cp in first(x_refs, out_refs, sems) + passed:
            cp.wait_send()

    return _Comm(list(shards), [jax.ShapeDtypeStruct((4, *s.shape), s.dtype) for s in shards],
                 [pltpu.SemaphoreType.DMA((6, n)), pltpu.SemaphoreType.DMA((6, n))], start, finish)


def _own_block(got, shards):
    chip = 2 * lax.axis_index("x") + lax.axis_index("y")
    return [lax.dynamic_update_index_in_dim(g, s, chip, 0) for g, s in zip(got, shards)]


def _pair_exchange_comm(gs):
    n = len(gs)

    def copies(g_refs, recv_refs, sems):
        x, y, c, _ = _place()
        return [pltpu.make_async_remote_copy(
            src_ref=g_refs[t].at[:, pl.ds((1 - c) * (gs[t].shape[1] // 2), gs[t].shape[1] // 2), :], dst_ref=recv_refs[t],
            send_sem=sems[0].at[t], recv_sem=sems[1].at[t], device_id=(x, y, 1 - c), device_id_type=MESH) for t in range(n)]

    def start(g_refs, recv_refs, sems):
        for cp in copies(g_refs, recv_refs, sems):
            cp.start()

    def finish(g_refs, recv_refs, sems):
        for cp in copies(g_refs, recv_refs, sems):
            cp.wait()

    return _Comm(list(gs), [jax.ShapeDtypeStruct((4, g.shape[1] // 2, g.shape[2]), g.dtype) for g in gs],
                 [pltpu.SemaphoreType.DMA((n,)), pltpu.SemaphoreType.DMA((n,))], start, finish)


def _pair_sum(g, recv, core, tr=256):
    n, r, cols = g.shape
    half = r // 2
    tr = _row_tile(half, tr)
    nb = half // tr

    def body(c_ref, g_ref, r_ref, o_ref):
        o_ref[...] = (g_ref[...] + r_ref[...]).astype(BF16)

    grid_spec = pltpu.PrefetchScalarGridSpec(
        num_scalar_prefetch=1, grid=(n, nb),
        in_specs=[pl.BlockSpec((None, tr, cols), lambda k, i, c: (k, c[0] * nb + i, 0)),
                  pl.BlockSpec((None, tr, cols), lambda k, i, c: (k, i, 0))],
        out_specs=pl.BlockSpec((None, tr, cols), lambda k, i, c: (k, i, 0)))
    return pl.pallas_call(
        body, name="rs_pair_sum", grid_spec=grid_spec, out_shape=jax.ShapeDtypeStruct((n, half, cols), BF16),
        compiler_params=_params(("parallel", "parallel")),
    )(core, g, recv)


def _row_tile(rows, pref):
    best = None
    for t in range(16, min(rows, pref) + 1, 16):
        if rows % t == 0:
            best = t
    assert best is not None, (rows, pref)
    return best


def _chip_exchange_comm(s1s):
    n = len(s1s)

    def copies(s_refs, recv_refs, sems):
        x, y, c, chips = _place()
        return [pltpu.make_async_remote_copy(
            src_ref=s_refs[t].at[2 * chip[0] + chip[1]], dst_ref=recv_refs[t].at[p],
            send_sem=sems[0].at[p, t], recv_sem=sems[1].at[p, t], device_id=(*chip, c), device_id_type=MESH)
            for t in range(n) for p, chip in enumerate(chips)]

    def start(s_refs, recv_refs, sems):
        for cp in copies(s_refs, recv_refs, sems):
            cp.start()

    def finish(s_refs, recv_refs, sems):
        for cp in copies(s_refs, recv_refs, sems):
            cp.wait()

    return _Comm(list(s1s), [jax.ShapeDtypeStruct((3, *s.shape[1:]), s.dtype) for s in s1s],
                 [pltpu.SemaphoreType.DMA((3, n)), pltpu.SemaphoreType.DMA((3, n))], start, finish)


def _final_sum(g, recv_a, recv_b, core, chip, tr=256):
    n, r, cols = g.shape
    half = r // 2
    tr = _row_tile(half, tr)
    nb = half // tr

    def body(c_ref, k_ref, g_ref, a_ref, b0_ref, b1_ref, b2_ref, o_ref):
        total = g_ref[...] + a_ref[...]
        for b_ref in (b0_ref, b1_ref, b2_ref):
            total = total + b_ref[...].astype(F32)
        o_ref[...] = total

    rel = lambda p: pl.BlockSpec((None, tr, cols), lambda i, c, k, p=p: (p, i, 0))
    grid_spec = pltpu.PrefetchScalarGridSpec(
        num_scalar_prefetch=2, grid=(nb,),
        in_specs=[pl.BlockSpec((None, tr, cols), lambda i, c, k: (k[0], c[0] * nb + i, 0)),
                  pl.BlockSpec((None, tr, cols), lambda i, c, k: (k[0], i, 0)), rel(0), rel(1), rel(2)],
        out_specs=pl.BlockSpec((tr, cols), lambda i, c, k: (c[0] * nb + i, 0)))
    return pl.pallas_call(
        body, name="rs_final_sum", grid_spec=grid_spec, out_shape=jax.ShapeDtypeStruct((r, cols), F32),
        compiler_params=_params(("parallel",)),
    )(core, chip, g, recv_a, recv_b, recv_b, recv_b)


def _pair_join(fs):
    n = len(fs)

    def body(*refs):
        out_refs = refs[n:2 * n]
        send_sems, recv_sems = refs[2 * n:]
        x, y, c, _ = _place()

        def copy(t, which_half):
            h = fs[t].shape[0] // 2
            rows = out_refs[t].at[pl.ds(which_half * h, h), :]
            return pltpu.make_async_remote_copy(
                src_ref=rows, dst_ref=rows, send_sem=send_sems.at[t], recv_sem=recv_sems.at[t],
                device_id=(x, y, 1 - c), device_id_type=MESH)

        sends = [copy(t, c) for t in range(n)]
        for cp in sends:
            cp.start()
        for t in range(n):
            sends[t].wait_send()
            copy(t, 1 - c).wait_recv()

    return pl.pallas_call(
        body, name="rs_pair_join", in_specs=[ANY] * n, out_specs=[ANY] * n,
        out_shape=[jax.ShapeDtypeStruct(f.shape, f.dtype) for f in fs],
        input_output_aliases={t: t for t in range(n)},
        scratch_shapes=[pltpu.SemaphoreType.DMA((n,)), pltpu.SemaphoreType.DMA((n,))],
    )(*fs)


class _Reduction:
    def __init__(self, gs):
        self.gs = gs
        self.core = lax.axis_index("c").astype(jnp.int32).reshape(1)
        self.chip = (2 * lax.axis_index("x") + lax.axis_index("y")).astype(jnp.int32).reshape(1)

    def exchange(self):
        return _pair_exchange_comm(self.gs)

    def sums(self, recv_a):
        self.recv_a = list(recv_a)
        return _chip_exchange_comm([_pair_sum(g, ra, self.core) for g, ra in zip(self.gs, self.recv_a)])

    def finish(self, recv_b):
        return _pair_join([_final_sum(g, ra, rb, self.core, self.chip) for g, ra, rb in zip(self.gs, self.recv_a, recv_b)])

    def run(self):
        recv_a = _run_comm(self.exchange(), "rs_pair_exchange")
        return self.finish(_run_comm(self.sums(recv_a), "rs_chip_exchange"))


def _all_reduce_small(v):
    r, cols = v.shape

    def body(v_ref, out_ref, buf_ref, send_sems, recv_sems):
        x, y, c, _ = _place()
        flip = lambda a, bit: 1 - a if bit else a
        buf_ref[4 * x + 2 * y + c] = v_ref[...]
        cps = []
        for rel in range(1, 8):
            peer = (flip(x, rel & 4), flip(y, rel & 2), flip(c, rel & 1))
            cps.append(pltpu.make_async_remote_copy(
                src_ref=v_ref, dst_ref=buf_ref.at[4 * x + 2 * y + c], send_sem=send_sems.at[rel - 1], recv_sem=recv_sems.at[rel - 1],
                device_id=peer, device_id_type=MESH))
        for cp in cps:
            cp.start()
        for cp in cps:
            cp.wait()
        total = buf_ref[0]
        for d in range(1, 8):
            total = total + buf_ref[d]
        out_ref[...] = total

    vm = pl.BlockSpec(memory_space=pltpu.VMEM)
    return pl.pallas_call(
        body, name="all_reduce_small", in_specs=[vm], out_specs=vm, out_shape=jax.ShapeDtypeStruct((r, cols), F32),
        scratch_shapes=[pltpu.VMEM((8, r, cols), F32), pltpu.SemaphoreType.DMA((7,)), pltpu.SemaphoreType.DMA((7,))],
    )(v)


SHARD_AXIS = {"w_in": 1, "conv_w": 1, "w_proj_conv": 1, "w_proj_fox": 1, "w_proj_sb": 1, "w_out": 0, "w_up": 1,
              "ffn_conv_w": 1, "w_down": 0}
SMALL_SHARDED = ("conv_w", "ffn_conv_w")
BIG = tuple(k for k in SHARD_AXIS if k not in SMALL_SHARDED)
BIG_IN = ("w_in",)
BIG_REST = tuple(k for k in BIG if k not in BIG_IN)
REPLICATED = tuple(k for k in WEIGHTS if k not in SHARD_AXIS)
SMALL = REPLICATED + SMALL_SHARDED


def _pack_small(parts, row_align):
    flat = jnp.concatenate([p.reshape(-1) for p in parts])
    rows = -(-flat.shape[0] // (PACK_COLS * row_align)) * row_align
    return jnp.pad(flat, (0, rows * PACK_COLS - flat.shape[0])).reshape(rows, PACK_COLS)


def _unpack_small(packed, shapes):
    flat = packed.reshape(-1)
    out, off = [], 0
    for shape in shapes:
        size = int(np.prod(shape))
        out.append(flat[off:off + size].reshape(shape))
        off += size
    return out


def _blocks(stacked, own):
    chip = 2 * lax.axis_index("x") + lax.axis_index("y")
    return [jnp.where(chip == k, own, stacked[k]) for k in range(4)]


def _to_chips(full, axis):
    a, b = full.shape
    if axis == 0:
        return full.reshape(4, a // 4, b)
    return jnp.moveaxis(full.reshape(a, 4, b // 4), 1, 0)


class _Plan:
    def __init__(self, given):
        self.given = given
        self.depth = given["norm1_g"].shape[0]
        conv_shapes = [given[k].shape for k in SMALL_SHARDED]
        packed = [_pack_small([given[k] for k in SMALL_SHARDED], 16)]
        (got,) = _own_block(_run_comm(_gather_comm(packed), "gather_conv_weights"), packed)
        by_chip = [_unpack_small(got[j], conv_shapes) for j in range(4)]
        self.conv = {k: jnp.concatenate([by_chip[j][i] for j in range(4)], axis=-1) for i, k in enumerate(SMALL_SHARDED)}
        self.shards = {}

    def _shards(self, l, names):
        if (l, names) not in self.shards:
            self.shards[l, names] = [self.given[k][l].astype(BF16) for k in names]
        return self.shards[l, names]

    def gather(self, l, names):
        return _gather_comm(self._shards(l, names))

    def weights(self, l, names, got):
        out = {}
        for k, stacked, own in zip(names, got, self._shards(l, names)):
            if k == "w_in":
                out["wa"], out["wf"] = _w_in_from_chips(_blocks(stacked, own))
            else:
                out[k] = jnp.concatenate(_blocks(stacked, own), axis=SHARD_AXIS[k])
        return out

    def layer(self, l, got_in):
        if got_in is None:
            got_in = _run_comm(self.gather(l, BIG_IN), "gather_w_in")
        p = {k: self.given[k][l] for k in REPLICATED}
        p.update({k: self.conv[k][l] for k in SMALL_SHARDED})
        p.update(self.weights(l, BIG_IN, got_in))
        return p

    def reduction(self, w_in_grad, rest):
        gs = [] if w_in_grad is None else [w_in_grad]
        return _Reduction(gs + [_to_chips(rest[k], SHARD_AXIS[k]) for k in BIG_REST if k in rest])


def kernel(x, norm1_g, w_in, fox_f_bias, gate_bias, conv_w, fox_q_norm_g, fox_k_norm_g, w_proj_conv, w_proj_fox, w_proj_sb, w_out, norm2_g, w_up, ffn_conv_w, ffn_conv_b, w_down, loss_target, m_norm1_g, m_w_in, m_fox_f_bias, m_gate_bias, m_conv_w, m_fox_q_norm_g, m_fox_k_norm_g, m_w_proj_conv, m_w_proj_fox, m_w_proj_sb, m_w_out, m_norm2_g, m_w_up, m_ffn_conv_w, m_ffn_conv_b, m_w_down, v_norm1_g, v_w_in, v_fox_f_bias, v_gate_bias, v_conv_w, v_fox_q_norm_g, v_fox_k_norm_g, v_w_proj_conv, v_w_proj_fox, v_w_proj_sb, v_w_out, v_norm2_g, v_w_up, v_ffn_conv_w, v_ffn_conv_b, v_w_down):
    given = dict(x=x, norm1_g=norm1_g, w_in=w_in, fox_f_bias=fox_f_bias, gate_bias=gate_bias, conv_w=conv_w, fox_q_norm_g=fox_q_norm_g, fox_k_norm_g=fox_k_norm_g, w_proj_conv=w_proj_conv, w_proj_fox=w_proj_fox, w_proj_sb=w_proj_sb, w_out=w_out, norm2_g=norm2_g, w_up=w_up, ffn_conv_w=ffn_conv_w, ffn_conv_b=ffn_conv_b, w_down=w_down, loss_target=loss_target, m_norm1_g=m_norm1_g, m_w_in=m_w_in, m_fox_f_bias=m_fox_f_bias, m_gate_bias=m_gate_bias, m_conv_w=m_conv_w, m_fox_q_norm_g=m_fox_q_norm_g, m_fox_k_norm_g=m_fox_k_norm_g, m_w_proj_conv=m_w_proj_conv, m_w_proj_fox=m_w_proj_fox, m_w_proj_sb=m_w_proj_sb, m_w_out=m_w_out, m_norm2_g=m_norm2_g, m_w_up=m_w_up, m_ffn_conv_w=m_ffn_conv_w, m_ffn_conv_b=m_ffn_conv_b, m_w_down=m_w_down, v_norm1_g=v_norm1_g, v_w_in=v_w_in, v_fox_f_bias=v_fox_f_bias, v_gate_bias=v_gate_bias, v_conv_w=v_conv_w, v_fox_q_norm_g=v_fox_q_norm_g, v_fox_k_norm_g=v_fox_k_norm_g, v_w_proj_conv=v_w_proj_conv, v_w_proj_fox=v_w_proj_fox, v_w_proj_sb=v_w_proj_sb, v_w_out=v_w_out, v_norm2_g=v_norm2_g, v_w_up=v_w_up, v_ffn_conv_w=v_ffn_conv_w, v_ffn_conv_b=v_ffn_conv_b, v_w_down=v_w_down)
    depth = given["norm1_g"].shape[0]
    chip = 2 * lax.axis_index("x") + lax.axis_index("y")

    sq, dx, grads = _local_step(given["x"][0], given["loss_target"][0], _Plan(given))
    loss = lax.psum(0.5 * sq[0, 0] / D_MODEL, ("x", "y", "c"))

    gsum = {k: jnp.stack([g[k] for g in grads]) for k in BIG}
    small_shapes = [(depth, *grads[0][k].shape) for k in SMALL]
    summed = _all_reduce_small(_pack_small([jnp.stack([g[k] for g in grads]) for k in SMALL], 8))
    for k, total in zip(SMALL, _unpack_small(summed, small_shapes)):
        if k in SMALL_SHARDED:
            total = lax.dynamic_index_in_dim(total.reshape(*total.shape[:-1], 4, total.shape[-1] // 4), chip, axis=2, keepdims=False)
        gsum[k] = total

    deltas, new_m, new_v = {}, {}, {}
    for k in WEIGHTS:
        deltas[k], new_m[k], new_v[k] = _adamw(given[k], gsum[k], given["m_" + k], given["v_" + k], "adamw_" + k)
    return (loss, dx[None], *[gsum[k] for k in WEIGHTS], *[deltas[k] for k in WEIGHTS],
            *[new_m[k] for k in WEIGHTS], *[new_v[k] for k in WEIGHTS])
```

```python
import collections

import numpy as np
import jax
import jax.numpy as jnp
from jax import lax
from jax.experimental import pallas as pl
from jax.experimental.pallas import tpu as pltpu

F32 = jnp.float32
BF16 = jnp.bfloat16

D_MODEL = 1024
DEPTH = 4
HEAD_DIM = 64
N_HEADS = 8
WIDTH = 512
D_FF = 2816
NORM_EPS = 1e-6
NEG_INF = -1e30
QK_SCALE = HEAD_DIM ** -0.5
LANES = 128
N_PAIR = N_HEADS // 2

GATE_OFF = 0
CONV_OFF = 3 * D_MODEL
FOX_OFF = CONV_OFF + 3 * WIDTH
SB_OFF = FOX_OFF + 3 * WIDTH
D_INA = SB_OFF + 3 * WIDTH

ADAM_LR = 0.001
ADAM_B1 = 0.9
ADAM_B2 = 0.999
ADAM_EPS = 1e-08
ADAM_WD = 0.01
ADAM_STEP = 10

VMEM_LIMIT = 48 * 1024 * 1024
ADAMW_BLOCK_BYTES = 1024 * 1024

PACK_COLS = 1024
PACK_ROW_ALIGN = 32
MESH = pl.DeviceIdType.MESH


def _params(sem):
    return pltpu.CompilerParams(dimension_semantics=sem, vmem_limit_bytes=VMEM_LIMIT)


def _dot(a, b, dims):
    return lax.dot_general(a, b, (dims, ((), ())), preferred_element_type=F32)


_NN = ((1,), (0,))
_NT = ((1,), (1,))
_TN = ((0,), (0,))


def _pick(dim, pref):
    if dim <= pref:
        return dim
    best = None
    for mult in range(1, dim // LANES + 1):
        t = mult * LANES
        if t <= pref and dim % t == 0:
            best = t
    assert best is not None, (dim, pref)
    return best


def _mm(a, b, mode, out_dtype=F32, tm=1024, tn=1024, tk=2048, res=None, name="mm", comm=None):
    if mode == "nn":
        (m, k), (_, n) = a.shape, b.shape
    elif mode == "nt":
        (m, k), (n, _) = a.shape, b.shape
    else:
        (k, m), (_, n) = a.shape, b.shape
    tm, tn, tk = _pick(m, tm), _pick(n, tn), _pick(k, tk)
    nk = k // tk
    dims = {"nn": _NN, "nt": _NT, "tn": _TN}[mode]
    if mode == "tn":
        a_spec = pl.BlockSpec((tk, tm), lambda i, j, kk: (kk, i))
    else:
        a_spec = pl.BlockSpec((tm, tk), lambda i, j, kk: (i, kk))
    if mode == "nt":
        b_spec = pl.BlockSpec((tn, tk), lambda i, j, kk: (j, kk))
    else:
        b_spec = pl.BlockSpec((tk, tn), lambda i, j, kk: (kk, j))
    o_spec = pl.BlockSpec((tm, tn), lambda i, j, kk: (i, j))
    in_specs = [a_spec, b_spec] + ([o_spec] if res is not None else [])
    n_in, n_out = (len(comm.inputs), len(comm.out_shapes)) if comm else (0, 0)
    gm, gn = m // tm, n // tn

    def body(*refs):
        a_ref, b_ref, refs = refs[0], refs[1], refs[2:]
        r_ref, refs = (refs[0], refs[1:]) if res is not None else (None, refs)
        comm_in, o_ref, refs = refs[:n_in], refs[n_in], refs[n_in + 1:]
        comm_out, acc_ref, sems = refs[:n_out], refs[n_out], refs[n_out + 1:]
        kk = pl.program_id(2)
        if comm:
            at = lambda i, j, k: (pl.program_id(0) == i) & (pl.program_id(1) == j) & (kk == k)
            pl.when(at(0, 0, 0))(lambda: comm.start(comm_in, comm_out, sems))
        part = _dot(a_ref[...].astype(BF16), b_ref[...].astype(BF16), dims)

        def finish(total):
            if r_ref is not None:
                total = total + r_ref[...].astype(F32)
            o_ref[...] = total.astype(out_dtype)

        if nk == 1:
            finish(part)
        else:
            @pl.when(kk == 0)
            def _():
                acc_ref[...] = part

            @pl.when(kk > 0)
            def _():
                acc_ref[...] += part

            @pl.when(kk == nk - 1)
            def _():
                finish(acc_ref[...])

        if comm:
            pl.when(at(gm - 1, gn - 1, nk - 1))(lambda: comm.finish(comm_in, comm_out, sems))

    args = (a, b) + ((res,) if res is not None else ())
    out = jax.ShapeDtypeStruct((m, n), out_dtype)
    acc = pltpu.VMEM((tm, tn) if nk > 1 else (8, LANES), F32)
    if not comm:
        return pl.pallas_call(
            body, name=name, grid=(gm, gn, nk), in_specs=in_specs, out_specs=o_spec, out_shape=out, scratch_shapes=[acc],
            compiler_params=_params(("parallel", "parallel", "arbitrary")),
        )(*args)
    outs = pl.pallas_call(
        body, name=name, grid=(gm, gn, nk), in_specs=in_specs + [ANY] * n_in, out_specs=[o_spec] + [ANY] * n_out,
        out_shape=[out] + comm.out_shapes, scratch_shapes=[acc] + comm.sems,
        compiler_params=_params(("arbitrary", "arbitrary", "arbitrary")),
    )(*args, *comm.inputs)
    return outs[0], outs[1:]


def _norm_mm(x, g, w, out_dtype, tm=1024, tn=1536, name="norm_mm"):
    m, d = x.shape
    n = w.shape[1]
    tm, tn = _pick(m, tm), _pick(n, tn)

    def body(x_ref, g_ref, w_ref, o_ref, hn_ref):
        @pl.when(pl.program_id(1) == 0)
        def _():
            xf = x_ref[...]
            r = lax.rsqrt(jnp.mean(xf * xf, axis=-1, keepdims=True) + NORM_EPS)
            hn_ref[...] = (xf * r * g_ref[...]).astype(BF16)

        o_ref[...] = _dot(hn_ref[...], w_ref[...], _NN).astype(out_dtype)

    return pl.pallas_call(
        body, name=name, grid=(m // tm, n // tn),
        in_specs=[pl.BlockSpec((tm, d), lambda i, j: (i, 0)),
                  pl.BlockSpec((1, d), lambda i, j: (0, 0)),
                  pl.BlockSpec((d, tn), lambda i, j: (0, j))],
        out_specs=[pl.BlockSpec((tm, tn), lambda i, j: (i, j)),
                   pl.BlockSpec((tm, d), lambda i, j: (i, 0))],
        out_shape=[jax.ShapeDtypeStruct((m, n), out_dtype), jax.ShapeDtypeStruct((m, d), BF16)],
        compiler_params=_params(("parallel", "arbitrary")),
    )(x, g.reshape(1, d), w)


def _norm_bwd(dhn, x, g, dx_in, tm=256, name="norm_bwd"):
    m, d = x.shape
    tm = min(tm, m)

    def body(dhn_ref, x_ref, g_ref, dxi_ref, dx_ref, gg_ref):
        xf = x_ref[...]
        r = lax.rsqrt(jnp.mean(xf * xf, axis=-1, keepdims=True) + NORM_EPS)
        xhat = xf * r
        dh = dhn_ref[...].astype(F32)
        dxn = dh * g_ref[...]
        mean = jnp.mean(dxn * xhat, axis=-1, keepdims=True)
        dx_ref[...] = dxi_ref[...] + r * (dxn - xhat * mean)
        part = jnp.sum(dh * xhat, axis=0, keepdims=True)

        @pl.when(pl.program_id(0) == 0)
        def _():
            gg_ref[...] = part

        @pl.when(pl.program_id(0) > 0)
        def _():
            gg_ref[...] += part

    row = pl.BlockSpec((tm, d), lambda i: (i, 0))
    vec = pl.BlockSpec((1, d), lambda i: (0, 0))
    dx, gg = pl.pallas_call(
        body, name=name, grid=(m // tm,), in_specs=[row, row, vec, row], out_specs=[row, vec],
        out_shape=[jax.ShapeDtypeStruct((m, d), F32), jax.ShapeDtypeStruct((1, d), F32)],
        compiler_params=_params(("arbitrary",)),
    )(dhn, x, g.reshape(1, d), dx_in)
    return dx, gg.reshape(d)


def _down(u, k):
    s = u.shape[0]
    rows = lax.broadcasted_iota(jnp.int32, u.shape, 0)
    return jnp.where(rows < k, 0.0, pltpu.roll(u, k, axis=0))


def _up(u, k):
    s = u.shape[0]
    rows = lax.broadcasted_iota(jnp.int32, u.shape, 0)
    return jnp.where(rows >= s - k, 0.0, pltpu.roll(u, s - k, axis=0))


def _conv_mix_fwd(proj, conv_w, tc=128):
    s = proj.shape[0]
    nb = WIDTH // tc
    off = CONV_OFF // tc

    def body(b_ref, c_ref, h_ref, w_ref, o_ref):
        u = c_ref[...].astype(F32) * h_ref[...].astype(F32)
        w = w_ref[...]
        cv = w[0:1] * _down(u, 2) + w[1:2] * _down(u, 1) + w[2:3] * u
        o_ref[...] = (b_ref[...].astype(F32) * cv).astype(BF16)

    col = lambda k: pl.BlockSpec((s, tc), lambda j, k=k: (0, off + k * nb + j))
    return pl.pallas_call(
        body, name="conv_mix_fwd", grid=(nb,),
        in_specs=[col(0), col(1), col(2), pl.BlockSpec((3, tc), lambda j: (0, j))],
        out_specs=pl.BlockSpec((s, tc), lambda j: (0, j)),
        out_shape=jax.ShapeDtypeStruct((s, WIDTH), BF16),
        compiler_params=_params(("parallel",)),
    )(proj, proj, proj, conv_w)


def _conv_mix_bwd(do, proj, conv_w, tc=128):
    s = proj.shape[0]
    nb = WIDTH // tc
    off = CONV_OFF // tc

    def body(do_ref, b_ref, c_ref, h_ref, w_ref, db_ref, dc_ref, dh_ref, gw_ref):
        b = b_ref[...].astype(F32)
        c = c_ref[...].astype(F32)
        h = h_ref[...].astype(F32)
        g = do_ref[...].astype(F32)
        w = w_ref[...]
        u = c * h
        u1, u2 = _down(u, 1), _down(u, 2)
        cv = w[0:1] * u2 + w[1:2] * u1 + w[2:3] * u
        db_ref[...] = (g * cv).astype(BF16)
        dcv = g * b
        gw_ref[0:1, :] = jnp.sum(dcv * u2, axis=0, keepdims=True)
        gw_ref[1:2, :] = jnp.sum(dcv * u1, axis=0, keepdims=True)
        gw_ref[2:3, :] = jnp.sum(dcv * u, axis=0, keepdims=True)
        du = w[2:3] * dcv + w[1:2] * _up(dcv, 1) + w[0:1] * _up(dcv, 2)
        dc_ref[...] = (du * h).astype(BF16)
        dh_ref[...] = (du * c).astype(BF16)

    col = lambda k: pl.BlockSpec((s, tc), lambda j, k=k: (0, off + k * nb + j))
    own = pl.BlockSpec((s, tc), lambda j: (0, j))
    wsp = pl.BlockSpec((3, tc), lambda j: (0, j))
    act = jax.ShapeDtypeStruct((s, WIDTH), BF16)
    return pl.pallas_call(
        body, name="conv_mix_bwd", grid=(nb,),
        in_specs=[own, col(0), col(1), col(2), wsp], out_specs=[own, own, own, wsp],
        out_shape=[act, act, act, jax.ShapeDtypeStruct((3, WIDTH), F32)],
        compiler_params=_params(("parallel",)),
    )(do, proj, proj, proj, conv_w)


def _glu_fwd(up, w, b, tc=256):
    s = up.shape[0]
    nb = D_FF // tc

    def body(g_ref, v_ref, w_ref, b_ref, o_ref):
        ug = g_ref[...].astype(F32)
        wv = w_ref[...]
        cg = wv[0:1] * _down(ug, 2) + wv[1:2] * _down(ug, 1) + wv[2:3] * ug + b_ref[...]
        act = cg * jax.nn.sigmoid(cg)
        o_ref[...] = (act * v_ref[...].astype(F32)).astype(BF16)

    return pl.pallas_call(
        body, name="glu_fwd", grid=(nb,),
        in_specs=[pl.BlockSpec((s, tc), lambda j: (0, j)), pl.BlockSpec((s, tc), lambda j: (0, nb + j)),
                  pl.BlockSpec((3, tc), lambda j: (0, j)), pl.BlockSpec((1, tc), lambda j: (0, j))],
        out_specs=pl.BlockSpec((s, tc), lambda j: (0, j)),
        out_shape=jax.ShapeDtypeStruct((s, D_FF), BF16),
        compiler_params=_params(("parallel",)),
    )(up, up, w, b.reshape(1, D_FF))


def _glu_bwd(dh, up, w, b, tc=256):
    s = up.shape[0]
    nb = D_FF // tc

    def body(dh_ref, g_ref, v_ref, w_ref, b_ref, dg_ref, dv_ref, gw_ref, gb_ref):
        ug = g_ref[...].astype(F32)
        uv = v_ref[...].astype(F32)
        d = dh_ref[...].astype(F32)
        wv = w_ref[...]
        u1, u2 = _down(ug, 1), _down(ug, 2)
        cg = wv[0:1] * u2 + wv[1:2] * u1 + wv[2:3] * ug + b_ref[...]
        sg = jax.nn.sigmoid(cg)
        dv_ref[...] = (d * (cg * sg)).astype(BF16)
        dcg = d * uv * (sg * (1.0 + cg * (1.0 - sg)))
        gb_ref[...] = jnp.sum(dcg, axis=0, keepdims=True)
        gw_ref[0:1, :] = jnp.sum(dcg * u2, axis=0, keepdims=True)
        gw_ref[1:2, :] = jnp.sum(dcg * u1, axis=0, keepdims=True)
        gw_ref[2:3, :] = jnp.sum(dcg * ug, axis=0, keepdims=True)
        dg_ref[...] = (wv[2:3] * dcg + wv[1:2] * _up(dcg, 1) + wv[0:1] * _up(dcg, 2)).astype(BF16)

    own = pl.BlockSpec((s, tc), lambda j: (0, j))
    wsp = pl.BlockSpec((3, tc), lambda j: (0, j))
    bsp = pl.BlockSpec((1, tc), lambda j: (0, j))
    act = jax.ShapeDtypeStruct((s, D_FF), BF16)
    dg, dv, gw, gb = pl.pallas_call(
        body, name="glu_bwd", grid=(nb,),
        in_specs=[own, own, pl.BlockSpec((s, tc), lambda j: (0, nb + j)), wsp, bsp],
        out_specs=[own, own, wsp, bsp],
        out_shape=[act, act, jax.ShapeDtypeStruct((3, D_FF), F32), jax.ShapeDtypeStruct((1, D_FF), F32)],
        compiler_params=_params(("parallel",)),
    )(dh, up, up, w, b.reshape(1, D_FF))
    return dg, dv, gw, gb.reshape(D_FF)


def _merge_fwd(x, oc, of, osb, wpc, wpf, wps, proj, gb, wout, tm=256):
    s, d = x.shape
    tm = min(tm, s)

    def body(x_ref, oc_ref, of_ref, os_ref, wpc_ref, wpf_ref, wps_ref, g0_ref, g1_ref, g2_ref, gb_ref, wo_ref,
             xo_ref, mg_ref, yc_ref, yf_ref, ys_ref):
        merged = jnp.zeros((tm, d), F32)
        for k, (o_ref, w_ref, g_ref, y_ref) in enumerate(
                ((oc_ref, wpc_ref, g0_ref, yc_ref), (of_ref, wpf_ref, g1_ref, yf_ref), (os_ref, wps_ref, g2_ref, ys_ref))):
            y = _dot(o_ref[...].astype(BF16), w_ref[...], _NN)
            y_ref[...] = y.astype(BF16)
            gate = jax.nn.sigmoid(g_ref[...].astype(F32) + gb_ref[:, k * d:(k + 1) * d])
            merged = merged + gate * y
        mb = merged.astype(BF16)
        mg_ref[...] = mb
        xo_ref[...] = x_ref[...] + _dot(mb, wo_ref[...], _NN)

    rowd = pl.BlockSpec((tm, d), lambda i: (i, 0))
    roww = pl.BlockSpec((tm, WIDTH), lambda i: (i, 0))
    wp = pl.BlockSpec((WIDTH, d), lambda i: (0, 0))
    gcol = lambda k: pl.BlockSpec((tm, d), lambda i, k=k: (i, GATE_OFF // d + k))
    actd = jax.ShapeDtypeStruct((s, d), BF16)
    return pl.pallas_call(
        body, name="merge_fwd", grid=(s // tm,),
        in_specs=[rowd, roww, roww, roww, wp, wp, wp, gcol(0), gcol(1), gcol(2),
                  pl.BlockSpec((1, 3 * d), lambda i: (0, 0)), pl.BlockSpec((d, d), lambda i: (0, 0))],
        out_specs=[rowd, rowd, rowd, rowd, rowd],
        out_shape=[jax.ShapeDtypeStruct((s, d), F32), actd, actd, actd, actd],
        compiler_params=_params(("parallel",)),
    )(x, oc, of, osb, wpc, wpf, wps, proj, proj, proj, gb.reshape(1, 3 * d), wout)


def _gate_bwd(dm, yc, yf, ys, proj, gb, tm=256):
    s, d = dm.shape
    tm = min(tm, s)

    def body(dm_ref, yc_ref, yf_ref, ys_ref, g0_ref, g1_ref, g2_ref, gb_ref, dyc_ref, dyf_ref, dys_ref, dgl_ref, ggb_ref):
        g = dm_ref[...].astype(F32)
        parts = []
        for k, (y_ref, g_ref, dy_ref) in enumerate(((yc_ref, g0_ref, dyc_ref), (yf_ref, g1_ref, dyf_ref), (ys_ref, g2_ref, dys_ref))):
            gate = jax.nn.sigmoid(g_ref[...].astype(F32) + gb_ref[:, k * d:(k + 1) * d])
            dy_ref[...] = (g * gate).astype(BF16)
            dgl = g * y_ref[...].astype(F32) * gate * (1.0 - gate)
            dgl_ref[:, k * d:(k + 1) * d] = dgl.astype(BF16)
            parts.append(jnp.sum(dgl, axis=0, keepdims=True))
        part = jnp.concatenate(parts, axis=1)

        @pl.when(pl.program_id(0) == 0)
        def _():
            ggb_ref[...] = part

        @pl.when(pl.program_id(0) > 0)
        def _():
            ggb_ref[...] += part

    rowd = pl.BlockSpec((tm, d), lambda i: (i, 0))
    gcol = lambda k: pl.BlockSpec((tm, d), lambda i, k=k: (i, GATE_OFF // d + k))
    vec = pl.BlockSpec((1, 3 * d), lambda i: (0, 0))
    actd = jax.ShapeDtypeStruct((s, d), BF16)
    dyc, dyf, dys, dgl, ggb = pl.pallas_call(
        body, name="gate_bwd", grid=(s // tm,),
        in_specs=[rowd, rowd, rowd, rowd, gcol(0), gcol(1), gcol(2), vec],
        out_specs=[rowd, rowd, rowd, pl.BlockSpec((tm, 3 * d), lambda i: (i, 0)), vec],
        out_shape=[actd, actd, actd, jax.ShapeDtypeStruct((s, 3 * d), BF16), jax.ShapeDtypeStruct((1, 3 * d), F32)],
        compiler_params=_params(("arbitrary",)),
    )(dm, yc, yf, ys, proj, proj, proj, gb.reshape(1, 3 * d))
    return dyc, dyf, dys, dgl, ggb.reshape(3 * d)


def _tri_tables(nq, r, order):
    last = lambda qi: (qi + 1) * r - 1
    if order == "k_outer":
        pairs = [(qi, kj) for kj in range(nq * r) for qi in range(kj // r, nq)]
    elif order == "k_desc":
        pairs = [(qi, kj) for qi in range(nq) for kj in range(last(qi), -1, -1)]
    else:
        pairs = [(qi, kj) for qi in range(nq) for kj in range(last(qi) + 1)]
    qs, ks = zip(*pairs)
    return jnp.asarray(np.array(qs, np.int32)), jnp.asarray(np.array(ks, np.int32)), len(pairs)


def _lo_mask(shape):
    return lax.broadcasted_iota(jnp.int32, shape, len(shape) - 1) < HEAD_DIM


def _head(x, h):
    lo = _lo_mask(x.shape)
    return jnp.where(lo if h == 0 else jnp.logical_not(lo), x, jnp.zeros_like(x))


def _pair_cols(x, h):
    return x[:, h * HEAD_DIM:h * HEAD_DIM + 1]


def _rep(a0, a1, shape):
    return jnp.where(_lo_mask(shape), a0, a1)


def _positions(qi, kj, tq, tk):
    row = qi * tq + lax.broadcasted_iota(jnp.int32, (tq, tk), 0)
    col = kj * tk + lax.broadcasted_iota(jnp.int32, (tq, tk), 1)
    return row, col


def _head_norm(x, g):
    lo = _lo_mask(x.shape)
    sq = x * x
    s0 = jnp.sum(jnp.where(lo, sq, 0.0), axis=-1, keepdims=True)
    s1 = jnp.sum(jnp.where(lo, 0.0, sq), axis=-1, keepdims=True)
    r = jnp.where(lo, lax.rsqrt(s0 / HEAD_DIM + NORM_EPS), lax.rsqrt(s1 / HEAD_DIM + NORM_EPS))
    return x * r, r


def _fox_prep(proj, gq, gk, tm=512):
    s = proj.shape[0]
    tm = min(tm, s)
    off = FOX_OFF // LANES

    def body(q_ref, k_ref, gq_ref, gk_ref, qn_ref, kn_ref):
        qh, _ = _head_norm(q_ref[...].astype(F32), None)
        kh, _ = _head_norm(k_ref[...].astype(F32), None)
        qn_ref[...] = (qh * gq_ref[...] * QK_SCALE).astype(BF16)
        kn_ref[...] = (kh * gk_ref[...]).astype(BF16)

    vec = pl.BlockSpec((1, LANES), lambda p, i: (0, 0))
    own = pl.BlockSpec((tm, LANES), lambda p, i: (i, p))
    act = jax.ShapeDtypeStruct((s, WIDTH), BF16)
    return pl.pallas_call(
        body, name="fox_prep", grid=(N_PAIR, s // tm),
        in_specs=[pl.BlockSpec((tm, LANES), lambda p, i: (i, off + p)),
                  pl.BlockSpec((tm, LANES), lambda p, i: (i, off + N_PAIR + p)), vec, vec],
        out_specs=[own, own], out_shape=[act, act],
        compiler_params=_params(("parallel", "parallel")),
    )(proj, proj, jnp.tile(gq, 2).reshape(1, LANES), jnp.tile(gk, 2).reshape(1, LANES))


def _fox_post(dqs, dkn, proj, gq, gk, tm=512):
    s = proj.shape[0]
    tm = min(tm, s)
    off = FOX_OFF // LANES

    def one(d_ref, x_ref, g_ref, scale, dx_ref, gg_ref, first):
        xhat, r = _head_norm(x_ref[...].astype(F32), None)
        dy = d_ref[...] * scale
        part = jnp.sum(dy * xhat, axis=0, keepdims=True)

        @pl.when(first)
        def _():
            gg_ref[...] = part

        @pl.when(jnp.logical_not(first))
        def _():
            gg_ref[...] += part

        dxh = dy * g_ref[...]
        lo = _lo_mask(dxh.shape)
        pr = dxh * xhat
        m0 = jnp.sum(jnp.where(lo, pr, 0.0), axis=-1, keepdims=True)
        m1 = jnp.sum(jnp.where(lo, 0.0, pr), axis=-1, keepdims=True)
        mean = jnp.where(lo, m0, m1) / HEAD_DIM
        dx_ref[...] = (r * (dxh - xhat * mean)).astype(BF16)

    def body(dq_ref, dk_ref, q_ref, k_ref, gq_ref, gk_ref, dxq_ref, dxk_ref, ggq_ref, ggk_ref):
        first = pl.program_id(1) == 0
        one(dq_ref, q_ref, gq_ref, QK_SCALE, dxq_ref, ggq_ref, first)
        one(dk_ref, k_ref, gk_ref, 1.0, dxk_ref, ggk_ref, first)

    vec = pl.BlockSpec((1, LANES), lambda p, i: (0, 0))
    own = pl.BlockSpec((tm, LANES), lambda p, i: (i, p))
    ggs = pl.BlockSpec((None, 1, LANES), lambda p, i: (p, 0, 0))
    act = jax.ShapeDtypeStruct((s, WIDTH), BF16)
    ggo = jax.ShapeDtypeStruct((N_PAIR, 1, LANES), F32)
    dxq, dxk, ggq, ggk = pl.pallas_call(
        body, name="fox_post", grid=(N_PAIR, s // tm),
        in_specs=[own, own, pl.BlockSpec((tm, LANES), lambda p, i: (i, off + p)),
                  pl.BlockSpec((tm, LANES), lambda p, i: (i, off + N_PAIR + p)), vec, vec],
        out_specs=[own, own, ggs, ggs], out_shape=[act, act, ggo, ggo],
        compiler_params=_params(("parallel", "arbitrary")),
    )(dqs, dkn, proj, proj, jnp.tile(gq, 2).reshape(1, LANES), jnp.tile(gk, 2).reshape(1, LANES))
    fold = lambda a: a.reshape(N_HEADS, HEAD_DIM).sum(axis=0)
    return dxq, dxk, fold(ggq), fold(ggk)


def _split3(x):
    a = x.astype(BF16)
    r = x - a.astype(F32)
    b = r.astype(BF16)
    c = (r - b.astype(F32)).astype(BF16)
    return a, b, c


def _split2(x):
    a = x.astype(BF16)
    b = (x - a.astype(F32)).astype(BF16)
    return a, b


def _log_sigmoid(x):
    return jnp.minimum(x, 0.0) - jnp.log(1.0 + jnp.exp(-jnp.abs(x)))


def _fox_gates(ft, bias):
    h, s = ft.shape
    nb = s // LANES

    def body(f_ref, b_ref, c_ref):
        lf = _log_sigmoid(f_ref[...] + b_ref[...])
        i = lax.broadcasted_iota(jnp.int32, (s, LANES), 0)
        j = pl.program_id(0) * LANES + lax.broadcasted_iota(jnp.int32, (s, LANES), 1)
        tri = jnp.where(i <= j, 1.0, 0.0).astype(BF16)
        c_ref[...] = sum(_dot(p, tri, _NN) for p in _split3(lf))

    return pl.pallas_call(
        body, name="fox_gates", grid=(nb,),
        in_specs=[pl.BlockSpec((h, s), lambda j: (0, 0)), pl.BlockSpec((h, 1), lambda j: (0, 0))],
        out_specs=pl.BlockSpec((h, LANES), lambda j: (0, j)),
        out_shape=jax.ShapeDtypeStruct((h, s), F32),
        compiler_params=_params(("parallel",)),
    )(ft, bias.reshape(h, 1))


def _fox_gates_bwd(dc, ft, bias):
    h, s = ft.shape
    nb = s // LANES

    def body(dc_ref, f_ref, fb_ref, b_ref, df_ref, gb_ref):
        i = lax.broadcasted_iota(jnp.int32, (s, LANES), 0)
        j = pl.program_id(0) * LANES + lax.broadcasted_iota(jnp.int32, (s, LANES), 1)
        tri = jnp.where(i >= j, 1.0, 0.0).astype(BF16)
        dlf = sum(_dot(p, tri, _NN) for p in _split3(dc_ref[...]))
        df = dlf * jax.nn.sigmoid(-(fb_ref[...] + b_ref[...]))
        df_ref[...] = df
        part = jnp.sum(df, axis=-1, keepdims=True)

        @pl.when(pl.program_id(0) == 0)
        def _():
            gb_ref[...] = part

        @pl.when(pl.program_id(0) > 0)
        def _():
            gb_ref[...] += part

    full = pl.BlockSpec((h, s), lambda j: (0, 0))
    blk = pl.BlockSpec((h, LANES), lambda j: (0, j))
    one = pl.BlockSpec((h, 1), lambda j: (0, 0))
    df, gb = pl.pallas_call(
        body, name="fox_gates_bwd", grid=(nb,), in_specs=[full, full, blk, one], out_specs=[blk, one],
        out_shape=[jax.ShapeDtypeStruct((h, s), F32), jax.ShapeDtypeStruct((h, 1), F32)],
        compiler_params=_params(("arbitrary",)),
    )(dc, ft, ft, bias.reshape(h, 1))
    return df, gb.reshape(h)


def _delta_rep(do, o, tm=512):
    s = do.shape[0]
    tm = min(tm, s)

    def body(do_ref, o_ref, d_ref):
        pr = do_ref[...].astype(F32) * o_ref[...].astype(F32)
        lo = _lo_mask(pr.shape)
        d0 = jnp.sum(jnp.where(lo, pr, 0.0), axis=-1, keepdims=True)
        d1 = jnp.sum(jnp.where(lo, 0.0, pr), axis=-1, keepdims=True)
        d_ref[...] = jnp.where(lo, d0, d1)

    own = pl.BlockSpec((tm, LANES), lambda p, i: (i, p))
    return pl.pallas_call(
        body, name="delta_rep", grid=(N_PAIR, s // tm), in_specs=[own, own],
        out_specs=pl.BlockSpec((None, tm, LANES), lambda p, i: (p, i, 0)),
        out_shape=jax.ShapeDtypeStruct((N_PAIR, s, LANES), F32),
        compiler_params=_params(("parallel", "parallel")),
    )(do, o)


def _tile(s, t):
    t = min(t, s)
    assert s % t == 0
    return t


def _fox_fwd(qn, kn, proj, ccol, crow, t=512, comm=None):
    s = qn.shape[0]
    t = _tile(s, t)
    n = s // t
    qtab, ktab, ntri = _tri_tables(n, 1, "k_asc")
    voff = FOX_OFF // LANES + 2 * N_PAIR
    n_in, n_out = (len(comm.inputs), len(comm.out_shapes)) if comm else (0, 0)

    def body(qt_ref, kt_ref, q_ref, k_ref, v_ref, cc_ref, cr_ref, *rest):
        comm_in, rest = rest[:n_in], rest[n_in:]
        (o_ref, lse_ref), rest = rest[:2], rest[2:]
        comm_out, rest = rest[:n_out], rest[n_out:]
        (m_ref, l_ref, acc_ref), sems = rest[:3], rest[3:]
        i = pl.program_id(1)
        qi, kj = qt_ref[i], kt_ref[i]
        if comm:
            pl.when((pl.program_id(0) == 0) & (i == 0))(lambda: comm.start(comm_in, comm_out, sems))

        @pl.when(kj == 0)
        def _():
            m_ref[...] = jnp.full(m_ref.shape, NEG_INF, F32)
            l_ref[...] = jnp.zeros(l_ref.shape, F32)
            acc_ref[...] = jnp.zeros(acc_ref.shape, F32)

        q, k, v = q_ref[...], k_ref[...], v_ref[...]
        row, col = _positions(qi, kj, t, t)
        causal = col <= row
        m_old = m_ref[...]
        mn, rs, pv = [], [], []
        for h in range(2):
            sc = _dot(_head(q, h), k, _NT) + _pair_cols(cc_ref[...], h) - cr_ref[h:h + 1, :]
            sc = jnp.where(causal, sc, NEG_INF)
            m_new = jnp.maximum(_pair_cols(m_old, h), jnp.max(sc, axis=-1, keepdims=True))
            p = jnp.exp(sc - m_new)
            mn.append(m_new)
            rs.append(jnp.sum(p, axis=-1, keepdims=True))
            pv.append(_dot(p.astype(BF16), _head(v, h), _NN))
        m_rep = _rep(mn[0], mn[1], m_old.shape)
        alpha = jnp.exp(m_old - m_rep)
        l_ref[...] = alpha * l_ref[...] + _rep(rs[0], rs[1], m_old.shape)
        acc_ref[...] = alpha * acc_ref[...] + pv[0] + pv[1]
        m_ref[...] = m_rep

        @pl.when(kj == qi)
        def _():
            o_ref[...] = acc_ref[...] / l_ref[...]
            lse_ref[...] = m_ref[...] + jnp.log(l_ref[...])

        if comm:
            pl.when((pl.program_id(0) == N_PAIR - 1) & (i == ntri - 1))(lambda: comm.finish(comm_in, comm_out, sems))

    grid_spec = pltpu.PrefetchScalarGridSpec(
        num_scalar_prefetch=2, grid=(N_PAIR, ntri),
        in_specs=[pl.BlockSpec((t, LANES), lambda p, i, qt, kt: (qt[i], p)),
                  pl.BlockSpec((t, LANES), lambda p, i, qt, kt: (kt[i], p)),
                  pl.BlockSpec((t, LANES), lambda p, i, qt, kt: (kt[i], voff + p)),
                  pl.BlockSpec((None, t, LANES), lambda p, i, qt, kt: (p, qt[i], 0)),
                  pl.BlockSpec((None, 2, t), lambda p, i, qt, kt: (p, 0, kt[i]))] + [ANY] * n_in,
        out_specs=[pl.BlockSpec((t, LANES), lambda p, i, qt, kt: (qt[i], p)),
                   pl.BlockSpec((None, t, LANES), lambda p, i, qt, kt: (p, qt[i], 0))] + [ANY] * n_out,
        scratch_shapes=[pltpu.VMEM((t, LANES), F32)] * 3 + (comm.sems if comm else []))
    outs = pl.pallas_call(
        body, name="fox_fwd", grid_spec=grid_spec,
        out_shape=[jax.ShapeDtypeStruct((s, WIDTH), F32), jax.ShapeDtypeStruct((N_PAIR, s, LANES), F32)]
        + (comm.out_shapes if comm else []),
        compiler_params=_params(("arbitrary", "arbitrary") if comm else ("parallel", "arbitrary")),
    )(qtab, ktab, qn, kn, proj, ccol, crow, *(comm.inputs if comm else []))
    return outs[:2], outs[2:]


def _fox_bwd(qn, kn, proj, do, lse, delta, ccol, crow, t=512, comm=None):
    s = qn.shape[0]
    t = _tile(s, t)
    n = s // t
    qtab, ktab, ntri = _tri_tables(n, 1, "k_outer")
    voff = FOX_OFF // LANES + 2 * N_PAIR
    n_in, n_out = (len(comm.inputs), len(comm.out_shapes)) if comm else (0, 0)

    def body(qt_ref, kt_ref, q_ref, k_ref, v_ref, do_ref, lse_ref, dl_ref, cc_ref, cr_ref, *rest):
        comm_in, rest = rest[:n_in], rest[n_in:]
        (dq_ref, dk_ref, dv_ref, dc_ref, dcq_ref), rest = rest[:5], rest[5:]
        comm_out, rest = rest[:n_out], rest[n_out:]
        (dka_ref, dva_ref, dca_ref, dcqa_ref), sems = rest[:4], rest[4:]
        i = pl.program_id(1)
        qi, kj = qt_ref[i], kt_ref[i]
        if comm:
            pl.when((pl.program_id(0) == 0) & (i == 0))(lambda: comm.start(comm_in, comm_out, sems))

        @pl.when(i == 0)
        def _():
            dq_ref[...] = jnp.zeros(dq_ref.shape, F32)
            dcqa_ref[...] = jnp.zeros(dcqa_ref.shape, F32)

        @pl.when(qi == kj)
        def _():
            dka_ref[...] = jnp.zeros(dka_ref.shape, F32)
            dva_ref[...] = jnp.zeros(dva_ref.shape, F32)
            dca_ref[...] = jnp.zeros(dca_ref.shape, F32)

        q, k, v, g = q_ref[...], k_ref[...], v_ref[...], do_ref[...]
        row, col = _positions(qi, kj, t, t)
        causal = col <= row
        dq = jnp.zeros((t, LANES), F32)
        dk = jnp.zeros((t, LANES), F32)
        dv = jnp.zeros((t, LANES), F32)
        rowsum = []
        for h in range(2):
            qh, gh = _head(q, h), _head(g, h)
            sc = _dot(qh, k, _NT) + _pair_cols(cc_ref[...], h) - cr_ref[h:h + 1, :]
            p = jnp.where(causal, jnp.exp(sc - _pair_cols(lse_ref[...], h)), 0.0)
            dp = _dot(gh, v, _NT)
            ds = p * (dp - _pair_cols(dl_ref[...], h))
            dsb = ds.astype(BF16)
            dv = dv + _dot(p.astype(BF16), gh, _TN)
            dk = dk + _dot(dsb, qh, _TN)
            dq = dq + _dot(dsb, _head(k, h), _NN)
            dca_ref[h:h + 1, :] -= jnp.sum(ds, axis=0, keepdims=True)
            rowsum.append(jnp.sum(ds, axis=-1, keepdims=True))
        dka_ref[...] += dk
        dva_ref[...] += dv
        rows = pl.ds(pl.multiple_of(qi * t, t), t)
        dq_ref[rows, :] += dq
        dcqa_ref[rows, :] += _rep(rowsum[0], rowsum[1], (t, LANES))

        @pl.when(qi == n - 1)
        def _():
            dk_ref[...] = dka_ref[...]
            dv_ref[...] = dva_ref[...].astype(BF16)
            dc_ref[...] = dca_ref[...]

        @pl.when(i == ntri - 1)
        def _():
            across = dcqa_ref[...].T
            dcq_ref[0:1, :] = across[0:1, :]
            dcq_ref[1:2, :] = across[HEAD_DIM:HEAD_DIM + 1, :]

        if comm:
            pl.when((pl.program_id(0) == N_PAIR - 1) & (i == ntri - 1))(lambda: comm.finish(comm_in, comm_out, sems))

    qblk = lambda p, i, qt, kt: (qt[i], p)
    kblk = lambda p, i, qt, kt: (kt[i], p)
    qrep = pl.BlockSpec((None, t, LANES), lambda p, i, qt, kt: (p, qt[i], 0))
    crs = pl.BlockSpec((None, 2, t), lambda p, i, qt, kt: (p, 0, kt[i]))
    grid_spec = pltpu.PrefetchScalarGridSpec(
        num_scalar_prefetch=2, grid=(N_PAIR, ntri),
        in_specs=[pl.BlockSpec((t, LANES), qblk), pl.BlockSpec((t, LANES), kblk),
                  pl.BlockSpec((t, LANES), lambda p, i, qt, kt: (kt[i], voff + p)),
                  pl.BlockSpec((t, LANES), qblk), qrep, qrep, qrep, crs] + [ANY] * n_in,
        out_specs=[pl.BlockSpec((s, LANES), lambda p, i, qt, kt: (0, p)),
                   pl.BlockSpec((t, LANES), kblk), pl.BlockSpec((t, LANES), kblk), crs,
                   pl.BlockSpec((None, 2, s), lambda p, i, qt, kt: (p, 0, 0))] + [ANY] * n_out,
        scratch_shapes=[pltpu.VMEM((t, LANES), F32), pltpu.VMEM((t, LANES), F32), pltpu.VMEM((2, t), F32),
                        pltpu.VMEM((s, LANES), F32)] + (comm.sems if comm else []))
    outs = pl.pallas_call(
        body, name="fox_bwd", grid_spec=grid_spec,
        out_shape=[jax.ShapeDtypeStruct((s, WIDTH), F32), jax.ShapeDtypeStruct((s, WIDTH), F32),
                   jax.ShapeDtypeStruct((s, WIDTH), BF16), jax.ShapeDtypeStruct((N_PAIR, 2, s), F32),
                   jax.ShapeDtypeStruct((N_PAIR, 2, s), F32)] + (comm.out_shapes if comm else []),
        compiler_params=_params(("arbitrary", "arbitrary") if comm else ("parallel", "arbitrary")),
    )(qtab, ktab, qn, kn, proj, do, lse, delta, ccol, crow, *(comm.inputs if comm else []))
    return outs[:5], outs[5:]


def _sb_tile(qh, k, strict, tk, r_col):
    z = _dot(qh, k, _NT)
    lg = jnp.where(strict, -(jnp.maximum(z, 0.0) + jnp.log(1.0 + jnp.exp(-jnp.abs(z)))), 0.0)
    jj = lax.broadcasted_iota(jnp.int32, (tk, tk), 0)
    ss = lax.broadcasted_iota(jnp.int32, (tk, tk), 1)
    above = jnp.where(jj > ss, 1.0, 0.0).astype(BF16)
    suffix = sum(_dot(p, above, _NN) for p in _split2(lg)) + r_col
    a = jnp.where(strict, jnp.exp(lg + z + suffix), 0.0)
    return z, lg, a


def _sb_fwd(proj, tq=512, tk=256, comm=None):
    s = proj.shape[0]
    tq = _tile(s, tq)
    tk = _tile(tq, tk)
    nq, r = s // tq, tq // tk
    qtab, ktab, ntri = _tri_tables(nq, r, "k_desc")
    off = SB_OFF // LANES
    n_in, n_out = (len(comm.inputs), len(comm.out_shapes)) if comm else (0, 0)

    def body(qt_ref, kt_ref, q_ref, k_ref, v_ref, *rest):
        comm_in, o_ref, rest = rest[:n_in], rest[n_in], rest[n_in + 1:]
        comm_out, rest = rest[:n_out], rest[n_out:]
        (acc_ref, r_ref), sems = rest[:2], rest[2:]
        i = pl.program_id(1)
        qi, kj = qt_ref[i], kt_ref[i]
        if comm:
            pl.when((pl.program_id(0) == 0) & (i == 0))(lambda: comm.start(comm_in, comm_out, sems))

        @pl.when(kj == (qi + 1) * r - 1)
        def _():
            acc_ref[...] = jnp.zeros(acc_ref.shape, F32)
            r_ref[...] = jnp.zeros(r_ref.shape, F32)

        q = q_ref[...] * QK_SCALE
        k, v = k_ref[...], v_ref[...]
        row, col = _positions(qi, kj, tq, tk)
        strict = col < row
        acc = acc_ref[...]
        for h in range(2):
            _, lg, a = _sb_tile(_head(q, h), k, strict, tk, r_ref[h])
            acc = acc + _dot(a.astype(BF16), _head(v, h), _NN)
            r_ref[h] += jnp.sum(lg, axis=-1, keepdims=True)
        acc_ref[...] = acc

        @pl.when(kj == 0)
        def _():
            o_ref[...] = acc_ref[...]

        if comm:
            pl.when((pl.program_id(0) == N_PAIR - 1) & (i == ntri - 1))(lambda: comm.finish(comm_in, comm_out, sems))

    grid_spec = pltpu.PrefetchScalarGridSpec(
        num_scalar_prefetch=2, grid=(N_PAIR, ntri),
        in_specs=[pl.BlockSpec((tq, LANES), lambda p, i, qt, kt: (qt[i], off + p)),
                  pl.BlockSpec((tk, LANES), lambda p, i, qt, kt: (kt[i], off + N_PAIR + p)),
                  pl.BlockSpec((tk, LANES), lambda p, i, qt, kt: (kt[i], off + 2 * N_PAIR + p))] + [ANY] * n_in,
        out_specs=[pl.BlockSpec((tq, LANES), lambda p, i, qt, kt: (qt[i], p))] + [ANY] * n_out,
        scratch_shapes=[pltpu.VMEM((tq, LANES), F32), pltpu.VMEM((2, tq, 1), F32)] + (comm.sems if comm else []))
    outs = pl.pallas_call(
        body, name="sb_fwd", grid_spec=grid_spec,
        out_shape=[jax.ShapeDtypeStruct((s, WIDTH), F32)] + (comm.out_shapes if comm else []),
        compiler_params=_params(("arbitrary", "arbitrary") if comm else ("parallel", "arbitrary")),
    )(qtab, ktab, proj, proj, proj, *(comm.inputs if comm else []))
    return outs[0], outs[1:]


def _sb_bwd(proj, do, delta, tq=512, tk=256, comm=None):
    s = proj.shape[0]
    tq = _tile(s, tq)
    tk = _tile(tq, tk)
    nq, r = s // tq, tq // tk
    qtab, ktab, ntri = _tri_tables(nq, r, "k_desc")
    off = SB_OFF // LANES
    n_in, n_out = (len(comm.inputs), len(comm.out_shapes)) if comm else (0, 0)

    def body(qt_ref, kt_ref, q_ref, k_ref, v_ref, do_ref, dl_ref, *rest):
        comm_in, rest = rest[:n_in], rest[n_in:]
        (dq_ref, dk_ref, dv_ref), rest = rest[:3], rest[3:]
        comm_out, rest = rest[:n_out], rest[n_out:]
        (dqa_ref, dka_ref, dva_ref, r_ref, rd_ref), sems = rest[:5], rest[5:]
        i = pl.program_id(1)
        qi, kj = qt_ref[i], kt_ref[i]
        if comm:
            pl.when((pl.program_id(0) == 0) & (i == 0))(lambda: comm.start(comm_in, comm_out, sems))

        @pl.when(i == 0)
        def _():
            dka_ref[...] = jnp.zeros(dka_ref.shape, F32)
            dva_ref[...] = jnp.zeros(dva_ref.shape, F32)

        @pl.when(kj == (qi + 1) * r - 1)
        def _():
            dqa_ref[...] = jnp.zeros(dqa_ref.shape, F32)
            r_ref[...] = jnp.zeros(r_ref.shape, F32)
            rd_ref[...] = jnp.zeros(rd_ref.shape, F32)

        q = q_ref[...] * QK_SCALE
        k, v, g = k_ref[...], v_ref[...], do_ref[...]
        row, col = _positions(qi, kj, tq, tk)
        strict = col < row
        ss = lax.broadcasted_iota(jnp.int32, (tk, tk), 0)
        jj = lax.broadcasted_iota(jnp.int32, (tk, tk), 1)
        at_or_after = jnp.where(ss >= jj, 1.0, 0.0).astype(BF16)
        dq = jnp.zeros((tq, LANES), F32)
        dk = jnp.zeros((tk, LANES), F32)
        dv = jnp.zeros((tk, LANES), F32)
        for h in range(2):
            qh, gh = _head(q, h), _head(g, h)
            z, lg, a = _sb_tile(qh, k, strict, tk, r_ref[h])
            da = _dot(gh, v, _NT)
            ab = a.astype(BF16)
            d_a = da * ab.astype(F32)
            incl = _dot(d_a.astype(BF16), at_or_after, _NN)
            d_l = _pair_cols(dl_ref[...], h) - rd_ref[h] - incl
            sig = jnp.exp(lg + z)
            dz = jnp.where(strict, d_a * (1.0 - sig) - d_l * sig, 0.0).astype(BF16)
            dq = dq + _dot(dz, _head(k, h), _NN)
            dk = dk + _dot(dz, qh, _TN)
            dv = dv + _dot(ab, gh, _TN)
            r_ref[h] += jnp.sum(lg, axis=-1, keepdims=True)
            rd_ref[h] += jnp.sum(d_a, axis=-1, keepdims=True)
        dqa_ref[...] += dq
        rows = pl.ds(pl.multiple_of(kj * tk, tk), tk)
        dka_ref[rows, :] += dk
        dva_ref[rows, :] += dv

        @pl.when(kj == 0)
        def _():
            dq_ref[...] = (dqa_ref[...] * QK_SCALE).astype(BF16)

        @pl.when(i == ntri - 1)
        def _():
            dk_ref[...] = dka_ref[...].astype(BF16)
            dv_ref[...] = dva_ref[...].astype(BF16)

        if comm:
            pl.when((pl.program_id(0) == N_PAIR - 1) & (i == ntri - 1))(lambda: comm.finish(comm_in, comm_out, sems))

    qblk = lambda p, i, qt, kt: (qt[i], p)
    whole = pl.BlockSpec((s, LANES), lambda p, i, qt, kt: (0, p))
    grid_spec = pltpu.PrefetchScalarGridSpec(
        num_scalar_prefetch=2, grid=(N_PAIR, ntri),
        in_specs=[pl.BlockSpec((tq, LANES), lambda p, i, qt, kt: (qt[i], off + p)),
                  pl.BlockSpec((tk, LANES), lambda p, i, qt, kt: (kt[i], off + N_PAIR + p)),
                  pl.BlockSpec((tk, LANES), lambda p, i, qt, kt: (kt[i], off + 2 * N_PAIR + p)),
                  pl.BlockSpec((tq, LANES), qblk),
                  pl.BlockSpec((None, tq, LANES), lambda p, i, qt, kt: (p, qt[i], 0))] + [ANY] * n_in,
        out_specs=[pl.BlockSpec((tq, LANES), qblk), whole, whole] + [ANY] * n_out,
        scratch_shapes=[pltpu.VMEM((tq, LANES), F32), pltpu.VMEM((s, LANES), F32), pltpu.VMEM((s, LANES), F32),
                        pltpu.VMEM((2, tq, 1), F32), pltpu.VMEM((2, tq, 1), F32)] + (comm.sems if comm else []))
    act = jax.ShapeDtypeStruct((s, WIDTH), BF16)
    outs = pl.pallas_call(
        body, name="sb_bwd", grid_spec=grid_spec, out_shape=[act, act, act] + (comm.out_shapes if comm else []),
        compiler_params=_params(("arbitrary", "arbitrary") if comm else ("parallel", "arbitrary")),
    )(qtab, ktab, proj, proj, proj, do, delta, *(comm.inputs if comm else []))
    return outs[:3], outs[3:]


def _loss_head(y, target, tm=256):
    s, d = y.shape
    tm = min(tm, s)

    def body(y_ref, t_ref, l_ref, dy_ref):
        e = y_ref[...] - t_ref[...]
        dy_ref[...] = e / d
        part = jnp.sum(jnp.sum(e * e, axis=0, keepdims=True), axis=1, keepdims=True)

        @pl.when(pl.program_id(0) == 0)
        def _():
            l_ref[...] = jnp.broadcast_to(part, l_ref.shape)

        @pl.when(pl.program_id(0) > 0)
        def _():
            l_ref[...] += jnp.broadcast_to(part, l_ref.shape)

    row = pl.BlockSpec((tm, d), lambda i: (i, 0))
    return pl.pallas_call(
        body, name="loss_head", grid=(s // tm,), in_specs=[row, row],
        out_specs=[pl.BlockSpec((1, LANES), lambda i: (0, 0)), row],
        out_shape=[jax.ShapeDtypeStruct((1, LANES), F32), jax.ShapeDtypeStruct((s, d), F32)],
        compiler_params=_params(("arbitrary",)),
    )(y, target)


def _adamw(w, g, m, v, name):
    shape = w.shape
    cols = shape[-1]
    rows = int(np.prod(shape[:-1]))
    tm = rows
    for cand in (512, 256, 128, 64, 32, 16, 8):
        if rows % cand == 0 and rows > cand and cand * cols * 4 <= ADAMW_BLOCK_BYTES:
            tm = cand
            break

    def body(w_ref, g_ref, m_ref, v_ref, d_ref, mo_ref, vo_ref):
        gr = g_ref[...]
        mn = ADAM_B1 * m_ref[...] + (1.0 - ADAM_B1) * gr
        vn = ADAM_B2 * v_ref[...] + (1.0 - ADAM_B2) * (gr * gr)
        m_hat = mn / (1.0 - ADAM_B1 ** ADAM_STEP)
        v_hat = vn / (1.0 - ADAM_B2 ** ADAM_STEP)
        d_ref[...] = -ADAM_LR * (m_hat / (jnp.sqrt(v_hat) + ADAM_EPS) + ADAM_WD * w_ref[...])
        mo_ref[...] = mn
        vo_ref[...] = vn

    blk = pl.BlockSpec((tm, cols), lambda i: (i, 0))
    out = jax.ShapeDtypeStruct((rows, cols), F32)
    r2 = lambda a: a.reshape(rows, cols)
    outs = pl.pallas_call(
        body, name=name, grid=(rows // tm,), in_specs=[blk] * 4, out_specs=[blk] * 3, out_shape=[out] * 3,
        compiler_params=_params(("parallel",)),
    )(r2(w), r2(g), r2(m), r2(v))
    return tuple(o.reshape(shape) for o in outs)


W_IN_NATURAL = (("a", CONV_OFF, 3 * WIDTH), ("a", FOX_OFF, 3 * WIDTH), ("f", 0, N_HEADS), ("a", SB_OFF, 3 * WIDTH),
                ("a", GATE_OFF, 3 * D_MODEL))
W_IN_COLS = D_INA + N_HEADS
W_IN_SHARD = W_IN_COLS // 4


def _w_in_from_chips(blocks):
    def natural(lo, hi):
        cols = []
        for k, blk in enumerate(blocks):
            a, b = max(lo, k * W_IN_SHARD), min(hi, (k + 1) * W_IN_SHARD)
            if a < b:
                cols.append(blk[:, a - k * W_IN_SHARD:b - k * W_IN_SHARD])
        return cols

    start = {}
    pos = 0
    for part, at, length in W_IN_NATURAL:
        start[part, at] = (pos, pos + length)
        pos += length
    order = sorted((at, rng) for (part, at), rng in start.items() if part == "a")
    wa = jnp.concatenate([c for _, (lo, hi) in order for c in natural(lo, hi)], axis=1)
    wf = jnp.pad(jnp.concatenate(natural(*start["f", 0]), axis=1), ((0, 0), (0, LANES - N_HEADS)))
    return wa, wf


def _w_in_to_chips(ga, gf):
    chips = []
    for k in range(4):
        lo, hi, pos, cols = k * W_IN_SHARD, (k + 1) * W_IN_SHARD, 0, []
        for part, at, length in W_IN_NATURAL:
            a, b = max(lo, pos), min(hi, pos + length)
            if a < b:
                cols.append((ga if part == "a" else gf)[:, at + a - pos:at + b - pos])
            pos += length
        chips.append(jnp.concatenate(cols, axis=1))
    return jnp.stack(chips)


def _layer_fwd(x, p, plan, l):
    s = x.shape[0]
    proj, hn1 = _norm_mm(x, p["norm1_g"], p["wa"], BF16, name="in_proj")
    fraw, _ = _norm_mm(x, p["norm1_g"], p["wf"], F32, name="in_proj_f")
    ft = fraw[:, :N_HEADS].T
    crow8 = _fox_gates(ft, p["fox_f_bias"])
    crow = crow8.reshape(N_PAIR, 2, s)
    ccol = jnp.repeat(crow.transpose(0, 2, 1), HEAD_DIM, axis=2)
    oc = _conv_mix_fwd(proj, p["conv_w"])
    qn, kn = _fox_prep(proj, p["fox_q_norm_g"], p["fox_k_norm_g"])
    last = l + 1 == plan.depth
    (of, lse), got_in = _fox_fwd(qn, kn, proj, ccol, crow, comm=None if last else plan.gather(l + 1, BIG_IN))
    osb, got_rest = _sb_fwd(proj, comm=plan.gather(l, BIG_REST))
    p.update(plan.weights(l, BIG_REST, got_rest))
    nxt = None if last else plan.layer(l + 1, got_in)
    xm, merged, yc, yf, ys = _merge_fwd(x, oc, of, osb, p["w_proj_conv"], p["w_proj_fox"], p["w_proj_sb"], proj,
                                        p["gate_bias"], p["w_out"])
    up, hn2 = _norm_mm(xm, p["norm2_g"], p["w_up"], BF16, name="up_proj")
    hh = _glu_fwd(up, p["ffn_conv_w"], p["ffn_conv_b"])
    xo = _mm(hh, p["w_down"], "nn", F32, res=xm, name="down_proj")
    saved = dict(x=x, hn1=hn1, proj=proj, ft=ft, crow=crow, ccol=ccol, oc=oc, qn=qn, kn=kn, of=of, lse=lse, osb=osb,
                 merged=merged, yc=yc, yf=yf, ys=ys, xm=xm, hn2=hn2, up=up, hh=hh)
    return xo, saved, nxt


def _layer_bwd(dx, p, a, plan, pending_in, first_layer):
    s = dx.shape[0]
    g = {}
    dhh = _mm(dx, p["w_down"], "nt", BF16, tn=1408, name="d_down_in")
    g["w_down"] = _mm(a["hh"], dx, "tn", F32, name="g_w_down")
    dug, duv, g["ffn_conv_w"], g["ffn_conv_b"] = _glu_bwd(dhh, a["up"], p["ffn_conv_w"], p["ffn_conv_b"])
    dup = jnp.concatenate([dug, duv], axis=1)
    g["w_up"] = _mm(a["hn2"], dup, "tn", F32, tn=1408, name="g_w_up")
    dhn2 = _mm(dup, p["w_up"], "nt", F32, name="d_up_in")
    dx, g["norm2_g"] = _norm_bwd(dhn2, a["xm"], p["norm2_g"], dx, name="norm2_bwd")
    dm = _mm(dx, p["w_out"], "nt", BF16, name="d_out_in")
    g["w_out"] = _mm(a["merged"], dx, "tn", F32, tn=512, name="g_w_out")
    dyc, dyf, dys, dgl, g["gate_bias"] = _gate_bwd(dm, a["yc"], a["yf"], a["ys"], a["proj"], p["gate_bias"])
    doc = _mm(dyc, p["w_proj_conv"], "nt", BF16, name="d_pc_in")
    dof = _mm(dyf, p["w_proj_fox"], "nt", BF16, name="d_pf_in")
    dos = _mm(dys, p["w_proj_sb"], "nt", BF16, name="d_ps_in")
    g["w_proj_conv"] = _mm(a["oc"], dyc, "tn", F32, name="g_w_pc")
    g["w_proj_fox"] = _mm(a["of"], dyf, "tn", F32, name="g_w_pf")
    g["w_proj_sb"] = _mm(a["osb"], dys, "tn", F32, name="g_w_ps")
    dcb, dcc, dch, g["conv_w"] = _conv_mix_bwd(doc, a["proj"], p["conv_w"])
    delta_f = _delta_rep(dof, a["of"])
    pending = plan.reduction(pending_in, {k: g[k] for k in BIG_REST})
    (dqs, dkn, dfv, dcrow, dcq), recv_a = _fox_bwd(a["qn"], a["kn"], a["proj"], dof, a["lse"], delta_f, a["ccol"], a["crow"],
                                                   comm=pending.exchange() if pending else None)
    dc = (dcrow + dcq).reshape(N_HEADS, s)
    dfq, dfk, g["fox_q_norm_g"], g["fox_k_norm_g"] = _fox_post(dqs, dkn, a["proj"], p["fox_q_norm_g"], p["fox_k_norm_g"])
    dft, g["fox_f_bias"] = _fox_gates_bwd(dc, a["ft"], p["fox_f_bias"])
    delta_s = _delta_rep(dos, a["osb"])
    (dsq, dsk, dsv), recv_b = _sb_bwd(a["proj"], dos, delta_s, comm=pending.sums(recv_a) if pending else None)
    done_in = None
    if pending:
        done = pending.finish(recv_b)
        if pending_in is not None:
            done_in, done = done[0], done[1:]
        g.update(zip(BIG_REST, done))
    dproj = jnp.concatenate([dgl, dcb, dcc, dch, dfq, dfk, dfv, dsq, dsk, dsv], axis=1)
    dfp = jnp.pad(dft.T, ((0, 0), (0, LANES - N_HEADS))).astype(BF16)
    ga = _mm(a["hn1"], dproj, "tn", F32, tn=768, name="g_w_in")
    gf = _mm(a["hn1"], dfp, "tn", F32, name="g_w_in_f")
    g["w_in"] = _w_in_to_chips(ga, gf)
    dhn1 = _mm(dfp, p["wf"], "nt", F32, name="d_in_f")
    own = plan.reduction(g["w_in"], {}) if first_layer else None
    if own:
        second = own.sums(_run_comm(own.exchange(), "rs_pair_exchange"))
        dhn1, recv_b = _mm(dproj, p["wa"], "nt", F32, res=dhn1, tk=1536, name="d_in", comm=second)
        (g["w_in"],) = own.finish(recv_b)
    else:
        dhn1 = _mm(dproj, p["wa"], "nt", F32, res=dhn1, tk=1536, name="d_in")
    dx, g["norm1_g"] = _norm_bwd(dhn1, a["x"], p["norm1_g"], dx, name="norm1_bwd")
    return dx, g, done_in


MATMUL_WEIGHTS = ("w_in", "w_proj_conv", "w_proj_fox", "w_proj_sb", "w_out", "w_up", "w_down")
WEIGHTS = ("norm1_g", "w_in", "fox_f_bias", "gate_bias", "conv_w", "fox_q_norm_g", "fox_k_norm_g", "w_proj_conv",
           "w_proj_fox", "w_proj_sb", "w_out", "norm2_g", "w_up", "ffn_conv_w", "ffn_conv_b", "w_down")


def _local_step(x, target, plan):
    depth = plan.depth
    layers, saved = [plan.layer(0, None)], []
    for l in range(depth):
        x, a, nxt = _layer_fwd(x, layers[l], plan, l)
        saved.append(a)
        if nxt is not None:
            layers.append(nxt)
    sq, dx = _loss_head(x, target)
    grads, pending_in = [None] * depth, None
    for l in reversed(range(depth)):
        dx, grads[l], done_in = _layer_bwd(dx, layers[l], saved[l], plan, pending_in, l == 0)
        if done_in is not None:
            grads[l + 1]["w_in"] = done_in
        pending_in = grads[l]["w_in"]
    return sq, dx, grads


ANY = pl.BlockSpec(memory_space=pl.ANY)


def _place():
    x, y, c = lax.axis_index("x"), lax.axis_index("y"), lax.axis_index("c")
    chips = [(1 - x, y), (x, 1 - y), (1 - x, 1 - y)]
    return x, y, c, chips


_Comm = collections.namedtuple("_Comm", "inputs out_shapes sems start finish")


def _run_comm(comm, name):
    n_in, n_out = len(comm.inputs), len(comm.out_shapes)

    def body(*refs):
        ins, outs, sems = refs[:n_in], refs[n_in:n_in + n_out], refs[n_in + n_out:]
        comm.start(ins, outs, sems)
        comm.finish(ins, outs, sems)

    return pl.pallas_call(body, name=name, in_specs=[ANY] * n_in, out_specs=[ANY] * n_out, out_shape=comm.out_shapes,
                          scratch_shapes=comm.sems)(*comm.inputs)


def _gather_comm(shards):
    n = len(shards)

    def copy(x_refs, out_refs, sems, k, t, chip_index, which_half, to, from_input=False):
        half = shards[t].shape[0] // 2
        rows = pl.ds(which_half * half, half)
        dst = out_refs[t].at[chip_index, rows, :]
        return pltpu.make_async_remote_copy(
            src_ref=x_refs[t].at[rows, :] if from_input else dst, dst_ref=dst,
            send_sem=sems[0].at[k, t], recv_sem=sems[1].at[k, t], device_id=to, device_id_type=MESH)

    def first(x_refs, out_refs, sems):
        x, y, c, chips = _place()
        return [copy(x_refs, out_refs, sems, p, t, 2 * x + y, c, (*chip, c), from_input=True)
                for t in range(n) for p, chip in enumerate(chips)]

    def start(x_refs, out_refs, sems):
        for cp in first(x_refs, out_refs, sems):
            cp.start()

    def finish(x_refs, out_refs, sems):
        x, y, c, chips = _place()
        passed = []
        for t in range(n):
            for p, chip in enumerate(chips):
                copy(x_refs, out_refs, sems, p, t, 2 * chip[0] + chip[1], c, (x, y, c)).wait_recv()
                fwd = copy(x_refs, out_refs, sems, 3 + p, t, 2 * chip[0] + chip[1], c, (x, y, 1 - c))
                fwd.start()
                passed.append(fwd)
        for t in range(n):
            for p, chip in enumerate(chips):
                copy(x_refs, out_refs, sems, 3 + p, t, 2 * chip[0] + chip[1], 1 - c, (x, y, c)).wait_recv()
        for cp in first(x_refs, out_refs, sems) + passed:
            cp.wait_send()

    return _Comm(list(shards), [jax.ShapeDtypeStruct((4, *s.shape), s.dtype) for s in shards],
                 [pltpu.SemaphoreType.DMA((6, n)), pltpu.SemaphoreType.DMA((6, n))], start, finish)


def _own_block(got, shards):
    chip = 2 * lax.axis_index("x") + lax.axis_index("y")
    return [lax.dynamic_update_index_in_dim(g, s, chip, 0) for g, s in zip(got, shards)]


def _pair_exchange_comm(gs):
    n = len(gs)

    def copies(g_refs, recv_refs, sems):
        x, y, c, _ = _place()
        return [pltpu.make_async_remote_copy(
            src_ref=g_refs[t].at[:, pl.ds((1 - c) * (gs[t].shape[1] // 2), gs[t].shape[1] // 2), :], dst_ref=recv_refs[t],
            send_sem=sems[0].at[t], recv_sem=sems[1].at[t], device_id=(x, y, 1 - c), device_id_type=MESH) for t in range(n)]

    def start(g_refs, recv_refs, sems):
        for cp in copies(g_refs, recv_refs, sems):
            cp.start()

    def finish(g_refs, recv_refs, sems):
        for cp in copies(g_refs, recv_refs, sems):
            cp.wait()

    return _Comm(list(gs), [jax.ShapeDtypeStruct((4, g.shape[1] // 2, g.shape[2]), g.dtype) for g in gs],
                 [pltpu.SemaphoreType.DMA((n,)), pltpu.SemaphoreType.DMA((n,))], start, finish)


def _pair_sum(g, recv, core, tr=256):
    n, r, cols = g.shape
    half = r // 2
    tr = _row_tile(half, tr)
    nb = half // tr

    def body(c_ref, g_ref, r_ref, o_ref):
        o_ref[...] = (g_ref[...] + r_ref[...]).astype(BF16)

    grid_spec = pltpu.PrefetchScalarGridSpec(
        num_scalar_prefetch=1, grid=(n, nb),
        in_specs=[pl.BlockSpec((None, tr, cols), lambda k, i, c: (k, c[0] * nb + i, 0)),
                  pl.BlockSpec((None, tr, cols), lambda k, i, c: (k, i, 0))],
        out_specs=pl.BlockSpec((None, tr, cols), lambda k, i, c: (k, i, 0)))
    return pl.pallas_call(
        body, name="rs_pair_sum", grid_spec=grid_spec, out_shape=jax.ShapeDtypeStruct((n, half, cols), BF16),
        compiler_params=_params(("parallel", "parallel")),
    )(core, g, recv)


def _row_tile(rows, pref):
    best = None
    for t in range(16, min(rows, pref) + 1, 16):
        if rows % t == 0:
            best = t
    assert best is not None, (rows, pref)
    return best


def _chip_exchange_comm(s1s):
    n = len(s1s)

    def copies(s_refs, recv_refs, sems):
        x, y, c, chips = _place()
        return [pltpu.make_async_remote_copy(
            src_ref=s_refs[t].at[2 * chip[0] + chip[1]], dst_ref=recv_refs[t].at[p],
            send_sem=sems[0].at[p, t], recv_sem=sems[1].at[p, t], device_id=(*chip, c), device_id_type=MESH)
            for t in range(n) for p, chip in enumerate(chips)]

    def start(s_refs, recv_refs, sems):
        for cp in copies(s_refs, recv_refs, sems):
            cp.start()

    def finish(s_refs, recv_refs, sems):
        for cp in copies(s_refs, recv_refs, sems):
            cp.wait()

    return _Comm(list(s1s), [jax.ShapeDtypeStruct((3, *s.shape[1:]), s.dtype) for s in s1s],
                 [pltpu.SemaphoreType.DMA((3, n)), pltpu.SemaphoreType.DMA((3, n))], start, finish)


def _final_sum(g, recv_a, recv_b, core, chip, tr=256):
    n, r, cols = g.shape
    half = r // 2
    tr = _row_tile(half, tr)
    nb = half // tr

    def body(c_ref, k_ref, g_ref, a_ref, b0_ref, b1_ref, b2_ref, o_ref):
        total = g_ref[...] + a_ref[...]
        for b_ref in (b0_ref, b1_ref, b2_ref):
            total = total + b_ref[...].astype(F32)
        o_ref[...] = total

    rel = lambda p: pl.BlockSpec((None, tr, cols), lambda i, c, k, p=p: (p, i, 0))
    grid_spec = pltpu.PrefetchScalarGridSpec(
        num_scalar_prefetch=2, grid=(nb,),
        in_specs=[pl.BlockSpec((None, tr, cols), lambda i, c, k: (k[0], c[0] * nb + i, 0)),
                  pl.BlockSpec((None, tr, cols), lambda i, c, k: (k[0], i, 0)), rel(0), rel(1), rel(2)],
        out_specs=pl.BlockSpec((tr, cols), lambda i, c, k: (c[0] * nb + i, 0)))
    return pl.pallas_call(
        body, name="rs_final_sum", grid_spec=grid_spec, out_shape=jax.ShapeDtypeStruct((r, cols), F32),
        compiler_params=_params(("parallel",)),
    )(core, chip, g, recv_a, recv_b, recv_b, recv_b)


def _pair_join(fs):
    n = len(fs)

    def body(*refs):
        out_refs = refs[n:2 * n]
        send_sems, recv_sems = refs[2 * n:]
        x, y, c, _ = _place()

        def copy(t, which_half):
            h = fs[t].shape[0] // 2
            rows = out_refs[t].at[pl.ds(which_half * h, h), :]
            return pltpu.make_async_remote_copy(
                src_ref=rows, dst_ref=rows, send_sem=send_sems.at[t], recv_sem=recv_sems.at[t],
                device_id=(x, y, 1 - c), device_id_type=MESH)

        sends = [copy(t, c) for t in range(n)]
        for cp in sends:
            cp.start()
        for t in range(n):
            sends[t].wait_send()
            copy(t, 1 - c).wait_recv()

    return pl.pallas_call(
        body, name="rs_pair_join", in_specs=[ANY] * n, out_specs=[ANY] * n,
        out_shape=[jax.ShapeDtypeStruct(f.shape, f.dtype) for f in fs],
        input_output_aliases={t: t for t in range(n)},
        scratch_shapes=[pltpu.SemaphoreType.DMA((n,)), pltpu.SemaphoreType.DMA((n,))],
    )(*fs)


class _Reduction:
    def __init__(self, gs):
        self.gs = gs
        self.core = lax.axis_index("c").astype(jnp.int32).reshape(1)
        self.chip = (2 * lax.axis_index("x") + lax.axis_index("y")).astype(jnp.int32).reshape(1)

    def exchange(self):
        return _pair_exchange_comm(self.gs)

    def sums(self, recv_a):
        self.recv_a = list(recv_a)
        return _chip_exchange_comm([_pair_sum(g, ra, self.core) for g, ra in zip(self.gs, self.recv_a)])

    def finish(self, recv_b):
        return _pair_join([_final_sum(g, ra, rb, self.core, self.chip) for g, ra, rb in zip(self.gs, self.recv_a, recv_b)])

    def run(self):
        recv_a = _run_comm(self.exchange(), "rs_pair_exchange")
        return self.finish(_run_comm(self.sums(recv_a), "rs_chip_exchange"))


def _all_reduce_small(v):
    r, cols = v.shape

    def body(v_ref, out_ref, buf_ref, send_sems, recv_sems):
        x, y, c, _ = _place()
        flip = lambda a, bit: 1 - a if bit else a
        buf_ref[4 * x + 2 * y + c] = v_ref[...]
        cps = []
        for rel in range(1, 8):
            peer = (flip(x, rel & 4), flip(y, rel & 2), flip(c, rel & 1))
            cps.append(pltpu.make_async_remote_copy(
                src_ref=v_ref, dst_ref=buf_ref.at[4 * x + 2 * y + c], send_sem=send_sems.at[rel - 1], recv_sem=recv_sems.at[rel - 1],
                device_id=peer, device_id_type=MESH))
        for cp in cps:
            cp.start()
        for cp in cps:
            cp.wait()
        total = buf_ref[0]
        for d in range(1, 8):
            total = total + buf_ref[d]
        out_ref[...] = total

    vm = pl.BlockSpec(memory_space=pltpu.VMEM)
    return pl.pallas_call(
        body, name="all_reduce_small", in_specs=[vm], out_specs=vm, out_shape=jax.ShapeDtypeStruct((r, cols), F32),
        scratch_shapes=[pltpu.VMEM((8, r, cols), F32), pltpu.SemaphoreType.DMA((7,)), pltpu.SemaphoreType.DMA((7,))],
    )(v)


SHARD_AXIS = {"w_in": 1, "conv_w": 1, "w_proj_conv": 1, "w_proj_fox": 1, "w_proj_sb": 1, "w_out": 0, "w_up": 1,
              "ffn_conv_w": 1, "w_down": 0}
SMALL_SHARDED = ("conv_w", "ffn_conv_w")
BIG = tuple(k for k in SHARD_AXIS if k not in SMALL_SHARDED)
BIG_IN = ("w_in",)
BIG_REST = tuple(k for k in BIG if k not in BIG_IN)
REPLICATED = tuple(k for k in WEIGHTS if k not in SHARD_AXIS)
SMALL = REPLICATED + SMALL_SHARDED


def _pack_small(parts, row_align):
    flat = jnp.concatenate([p.reshape(-1) for p in parts])
    rows = -(-flat.shape[0] // (PACK_COLS * row_align)) * row_align
    return jnp.pad(flat, (0, rows * PACK_COLS - flat.shape[0])).reshape(rows, PACK_COLS)


def _unpack_small(packed, shapes):
    flat = packed.reshape(-1)
    out, off = [], 0
    for shape in shapes:
        size = int(np.prod(shape))
        out.append(flat[off:off + size].reshape(shape))
        off += size
    return out


def _blocks(stacked, own):
    chip = 2 * lax.axis_index("x") + lax.axis_index("y")
    return [jnp.where(chip == k, own, stacked[k]) for k in range(4)]


def _to_chips(full, axis):
    a, b = full.shape
    if axis == 0:
        return full.reshape(4, a // 4, b)
    return jnp.moveaxis(full.reshape(a, 4, b // 4), 1, 0)


class _Plan:
    def __init__(self, given):
        self.given = given
        self.depth = given["norm1_g"].shape[0]
        conv_shapes = [given[k].shape for k in SMALL_SHARDED]
        packed = [_pack_small([given[k] for k in SMALL_SHARDED], 16)]
        (got,) = _own_block(_run_comm(_gather_comm(packed), "gather_conv_weights"), packed)
        by_chip = [_unpack_small(got[j], conv_shapes) for j in range(4)]
        self.conv = {k: jnp.concatenate([by_chip[j][i] for j in range(4)], axis=-1) for i, k in enumerate(SMALL_SHARDED)}
        self.shards = {}

    def _shards(self, l, names):
        if (l, names) not in self.shards:
            self.shards[l, names] = [self.given[k][l].astype(BF16) for k in names]
        return self.shards[l, names]

    def gather(self, l, names):
        return _gather_comm(self._shards(l, names))

    def weights(self, l, names, got):
        out = {}
        for k, stacked, own in zip(names, got, self._shards(l, names)):
            if k == "w_in":
                out["wa"], out["wf"] = _w_in_from_chips(_blocks(stacked, own))
            else:
                out[k] = jnp.concatenate(_blocks(stacked, own), axis=SHARD_AXIS[k])
        return out

    def layer(self, l, got_in):
        if got_in is None:
            got_in = _run_comm(self.gather(l, BIG_IN), "gather_w_in")
        p = {k: self.given[k][l] for k in REPLICATED}
        p.update({k: self.conv[k][l] for k in SMALL_SHARDED})
        p.update(self.weights(l, BIG_IN, got_in))
        return p

    def reduction(self, w_in_grad, rest):
        gs = [] if w_in_grad is None else [w_in_grad]
        return _Reduction(gs + [_to_chips(rest[k], SHARD_AXIS[k]) for k in BIG_REST if k in rest])


def kernel(x, norm1_g, w_in, fox_f_bias, gate_bias, conv_w, fox_q_norm_g, fox_k_norm_g, w_proj_conv, w_proj_fox, w_proj_sb, w_out, norm2_g, w_up, ffn_conv_w, ffn_conv_b, w_down, loss_target, m_norm1_g, m_w_in, m_fox_f_bias, m_gate_bias, m_conv_w, m_fox_q_norm_g, m_fox_k_norm_g, m_w_proj_conv, m_w_proj_fox, m_w_proj_sb, m_w_out, m_norm2_g, m_w_up, m_ffn_conv_w, m_ffn_conv_b, m_w_down, v_norm1_g, v_w_in, v_fox_f_bias, v_gate_bias, v_conv_w, v_fox_q_norm_g, v_fox_k_norm_g, v_w_proj_conv, v_w_proj_fox, v_w_proj_sb, v_w_out, v_norm2_g, v_w_up, v_ffn_conv_w, v_ffn_conv_b, v_w_down):
    given = dict(x=x, norm1_g=norm1_g, w_in=w_in, fox_f_bias=fox_f_bias, gate_bias=gate_bias, conv_w=conv_w, fox_q_norm_g=fox_q_norm_g, fox_k_norm_g=fox_k_norm_g, w_proj_conv=w_proj_conv, w_proj_fox=w_proj_fox, w_proj_sb=w_proj_sb, w_out=w_out, norm2_g=norm2_g, w_up=w_up, ffn_conv_w=ffn_conv_w, ffn_conv_b=ffn_conv_b, w_down=w_down, loss_target=loss_target, m_norm1_g=m_norm1_g, m_w_in=m_w_in, m_fox_f_bias=m_fox_f_bias, m_gate_bias=m_gate_bias, m_conv_w=m_conv_w, m_fox_q_norm_g=m_fox_q_norm_g, m_fox_k_norm_g=m_fox_k_norm_g, m_w_proj_conv=m_w_proj_conv, m_w_proj_fox=m_w_proj_fox, m_w_proj_sb=m_w_proj_sb, m_w_out=m_w_out, m_norm2_g=m_norm2_g, m_w_up=m_w_up, m_ffn_conv_w=m_ffn_conv_w, m_ffn_conv_b=m_ffn_conv_b, m_w_down=m_w_down, v_norm1_g=v_norm1_g, v_w_in=v_w_in, v_fox_f_bias=v_fox_f_bias, v_gate_bias=v_gate_bias, v_conv_w=v_conv_w, v_fox_q_norm_g=v_fox_q_norm_g, v_fox_k_norm_g=v_fox_k_norm_g, v_w_proj_conv=v_w_proj_conv, v_w_proj_fox=v_w_proj_fox, v_w_proj_sb=v_w_proj_sb, v_w_out=v_w_out, v_norm2_g=v_norm2_g, v_w_up=v_w_up, v_ffn_conv_w=v_ffn_conv_w, v_ffn_conv_b=v_ffn_conv_b, v_w_down=v_w_down)
    depth = given["norm1_g"].shape[0]
    chip = 2 * lax.axis_index("x") + lax.axis_index("y")

    sq, dx, grads = _local_step(given["x"][0], given["loss_target"][0], _Plan(given))
    loss = lax.psum(0.5 * sq[0, 0] / D_MODEL, ("x", "y", "c"))

    gsum = {k: jnp.stack([g[k] for g in grads]) for k in BIG}
    small_shapes = [(depth, *grads[0][k].shape) for k in SMALL]
    summed = _all_reduce_small(_pack_small([jnp.stack([g[k] for g in grads]) for k in SMALL], 8))
    for k, total in zip(SMALL, _unpack_small(summed, small_shapes)):
        if k in SMALL_SHARDED:
            total = lax.dynamic_index_in_dim(total.reshape(*total.shape[:-1], 4, total.shape[-1] // 4), chip, axis=2, keepdims=False)
        gsum[k] = total

    deltas, new_m, new_v = {}, {}, {}
    for k in WEIGHTS:
        deltas[k], new_m[k], new_v[k] = _adamw(given[k], gsum[k], given["m_" + k], given["v_" + k], "adamw_" + k)
    return (loss, dx[None], *[gsum[k] for k in WEIGHTS], *[deltas[k] for k in WEIGHTS],
            *[new_m[k] for k in WEIGHTS], *[new_v[k] for k in WEIGHTS])
```

```python
import collections

import numpy as np
import jax
import jax.numpy as jnp
from jax import lax
from jax.experimental import pallas as pl
from jax.experimental.pallas import tpu as pltpu

F32 = jnp.float32
BF16 = jnp.bfloat16

D_MODEL = 1024
DEPTH = 4
HEAD_DIM = 64
N_HEADS = 8
WIDTH = 512
D_FF = 2816
NORM_EPS = 1e-6
NEG_INF = -1e30
QK_SCALE = HEAD_DIM ** -0.5
LANES = 128
N_PAIR = N_HEADS // 2

GATE_OFF = 0
CONV_OFF = 3 * D_MODEL
FOX_OFF = CONV_OFF + 3 * WIDTH
SB_OFF = FOX_OFF + 3 * WIDTH
D_INA = SB_OFF + 3 * WIDTH

ADAM_LR = 0.001
ADAM_B1 = 0.9
ADAM_B2 = 0.999
ADAM_EPS = 1e-08
ADAM_WD = 0.01
ADAM_STEP = 10

VMEM_LIMIT = 48 * 1024 * 1024
ADAMW_BLOCK_BYTES = 1024 * 1024

PACK_COLS = 1024
MESH = pl.DeviceIdType.MESH


def _params(sem):
    return pltpu.CompilerParams(dimension_semantics=sem, vmem_limit_bytes=VMEM_LIMIT)


def _dot(a, b, dims):
    return lax.dot_general(a, b, (dims, ((), ())), preferred_element_type=F32)


_NN = ((1,), (0,))
_NT = ((1,), (1,))
_TN = ((0,), (0,))


def _pick(dim, pref):
    if dim <= pref:
        return dim
    best = None
    for mult in range(1, dim // LANES + 1):
        t = mult * LANES
        if t <= pref and dim % t == 0:
            best = t
    assert best is not None, (dim, pref)
    return best


def _mm(a, b, mode, out_dtype=F32, tm=1024, tn=1024, tk=2048, res=None, name="mm", comm=None):
    if mode == "nn":
        (m, k), (_, n) = a.shape, b.shape
    elif mode == "nt":
        (m, k), (n, _) = a.shape, b.shape
    else:
        (k, m), (_, n) = a.shape, b.shape
    tm, tn, tk = _pick(m, tm), _pick(n, tn), _pick(k, tk)
    nk = k // tk
    dims = {"nn": _NN, "nt": _NT, "tn": _TN}[mode]
    if mode == "tn":
        a_spec = pl.BlockSpec((tk, tm), lambda i, j, kk: (kk, i))
    else:
        a_spec = pl.BlockSpec((tm, tk), lambda i, j, kk: (i, kk))
    if mode == "nt":
        b_spec = pl.BlockSpec((tn, tk), lambda i, j, kk: (j, kk))
    else:
        b_spec = pl.BlockSpec((tk, tn), lambda i, j, kk: (kk, j))
    o_spec = pl.BlockSpec((tm, tn), lambda i, j, kk: (i, j))
    in_specs = [a_spec, b_spec] + ([o_spec] if res is not None else [])
    n_in, n_out = (len(comm.inputs), len(comm.out_shapes)) if comm else (0, 0)
    gm, gn = m // tm, n // tn

    def body(*refs):
        a_ref, b_ref, refs = refs[0], refs[1], refs[2:]
        r_ref, refs = (refs[0], refs[1:]) if res is not None else (None, refs)
        comm_in, o_ref, refs = refs[:n_in], refs[n_in], refs[n_in + 1:]
        comm_out, acc_ref, sems = refs[:n_out], refs[n_out], refs[n_out + 1:]
        kk = pl.program_id(2)
        if comm:
            at = lambda i, j, k: (pl.program_id(0) == i) & (pl.program_id(1) == j) & (kk == k)
            pl.when(at(0, 0, 0))(lambda: comm.start(comm_in, comm_out, sems))
        part = _dot(a_ref[...].astype(BF16), b_ref[...].astype(BF16), dims)

        def finish(total):
            if r_ref is not None:
                total = total + r_ref[...].astype(F32)
            o_ref[...] = total.astype(out_dtype)

        if nk == 1:
            finish(part)
        else:
            @pl.when(kk == 0)
            def _():
                acc_ref[...] = part

            @pl.when(kk > 0)
            def _():
                acc_ref[...] += part

            @pl.when(kk == nk - 1)
            def _():
                finish(acc_ref[...])

        if comm:
            pl.when(at(gm - 1, gn - 1, nk - 1))(lambda: comm.finish(comm_in, comm_out, sems))

    args = (a, b) + ((res,) if res is not None else ())
    out = jax.ShapeDtypeStruct((m, n), out_dtype)
    acc = pltpu.VMEM((tm, tn) if nk > 1 else (8, LANES), F32)
    if not comm:
        return pl.pallas_call(
            body, name=name, grid=(gm, gn, nk), in_specs=in_specs, out_specs=o_spec, out_shape=out, scratch_shapes=[acc],
            compiler_params=_params(("parallel", "parallel", "arbitrary")),
        )(*args)
    outs = pl.pallas_call(
        body, name=name, grid=(gm, gn, nk), in_specs=in_specs + [ANY] * n_in, out_specs=[o_spec] + [ANY] * n_out,
        out_shape=[out] + comm.out_shapes, scratch_shapes=[acc] + comm.sems,
        compiler_params=_params(("arbitrary", "arbitrary", "arbitrary")),
    )(*args, *comm.inputs)
    return outs[0], outs[1:]


def _norm_mm(x, g, w, out_dtype, tm=1024, tn=1536, name="norm_mm"):
    m, d = x.shape
    n = w.shape[1]
    tm, tn = _pick(m, tm), _pick(n, tn)

    def body(x_ref, g_ref, w_ref, o_ref, hn_ref):
        @pl.when(pl.program_id(1) == 0)
        def _():
            xf = x_ref[...]
            r = lax.rsqrt(jnp.mean(xf * xf, axis=-1, keepdims=True) + NORM_EPS)
            hn_ref[...] = (xf * r * g_ref[...]).astype(BF16)

        o_ref[...] = _dot(hn_ref[...], w_ref[...], _NN).astype(out_dtype)

    return pl.pallas_call(
        body, name=name, grid=(m // tm, n // tn),
        in_specs=[pl.BlockSpec((tm, d), lambda i, j: (i, 0)),
                  pl.BlockSpec((1, d), lambda i, j: (0, 0)),
                  pl.BlockSpec((d, tn), lambda i, j: (0, j))],
        out_specs=[pl.BlockSpec((tm, tn), lambda i, j: (i, j)),
                   pl.BlockSpec((tm, d), lambda i, j: (i, 0))],
        out_shape=[jax.ShapeDtypeStruct((m, n), out_dtype), jax.ShapeDtypeStruct((m, d), BF16)],
        compiler_params=_params(("parallel", "arbitrary")),
    )(x, g.reshape(1, d), w)


def _norm_bwd(dhn, x, g, dx_in, tm=256, name="norm_bwd"):
    m, d = x.shape
    tm = min(tm, m)

    def body(dhn_ref, x_ref, g_ref, dxi_ref, dx_ref, gg_ref):
        xf = x_ref[...]
        r = lax.rsqrt(jnp.mean(xf * xf, axis=-1, keepdims=True) + NORM_EPS)
        xhat = xf * r
        dh = dhn_ref[...].astype(F32)
        dxn = dh * g_ref[...]
        mean = jnp.mean(dxn * xhat, axis=-1, keepdims=True)
        dx_ref[...] = dxi_ref[...] + r * (dxn - xhat * mean)
        part = jnp.sum(dh * xhat, axis=0, keepdims=True)

        @pl.when(pl.program_id(0) == 0)
        def _():
            gg_ref[...] = part

        @pl.when(pl.program_id(0) > 0)
        def _():
            gg_ref[...] += part

    row = pl.BlockSpec((tm, d), lambda i: (i, 0))
    vec = pl.BlockSpec((1, d), lambda i: (0, 0))
    dx, gg = pl.pallas_call(
        body, name=name, grid=(m // tm,), in_specs=[row, row, vec, row], out_specs=[row, vec],
        out_shape=[jax.ShapeDtypeStruct((m, d), F32), jax.ShapeDtypeStruct((1, d), F32)],
        compiler_params=_params(("arbitrary",)),
    )(dhn, x, g.reshape(1, d), dx_in)
    return dx, gg.reshape(d)


def _down(u, k):
    s = u.shape[0]
    rows = lax.broadcasted_iota(jnp.int32, u.shape, 0)
    return jnp.where(rows < k, 0.0, pltpu.roll(u, k, axis=0))


def _up(u, k):
    s = u.shape[0]
    rows = lax.broadcasted_iota(jnp.int32, u.shape, 0)
    return jnp.where(rows >= s - k, 0.0, pltpu.roll(u, s - k, axis=0))


def _conv_mix_fwd(proj, conv_w, tc=128):
    s = proj.shape[0]
    nb = WIDTH // tc
    off = CONV_OFF // tc

    def body(b_ref, c_ref, h_ref, w_ref, o_ref):
        u = c_ref[...].astype(F32) * h_ref[...].astype(F32)
        w = w_ref[...]
        cv = w[0:1] * _down(u, 2) + w[1:2] * _down(u, 1) + w[2:3] * u
        o_ref[...] = (b_ref[...].astype(F32) * cv).astype(BF16)

    col = lambda k: pl.BlockSpec((s, tc), lambda j, k=k: (0, off + k * nb + j))
    return pl.pallas_call(
        body, name="conv_mix_fwd", grid=(nb,),
        in_specs=[col(0), col(1), col(2), pl.BlockSpec((3, tc), lambda j: (0, j))],
        out_specs=pl.BlockSpec((s, tc), lambda j: (0, j)),
        out_shape=jax.ShapeDtypeStruct((s, WIDTH), BF16),
        compiler_params=_params(("parallel",)),
    )(proj, proj, proj, conv_w)


def _conv_mix_bwd(do, proj, conv_w, tc=128):
    s = proj.shape[0]
    nb = WIDTH // tc
    off = CONV_OFF // tc

    def body(do_ref, b_ref, c_ref, h_ref, w_ref, db_ref, dc_ref, dh_ref, gw_ref):
        b = b_ref[...].astype(F32)
        c = c_ref[...].astype(F32)
        h = h_ref[...].astype(F32)
        g = do_ref[...].astype(F32)
        w = w_ref[...]
        u = c * h
        u1, u2 = _down(u, 1), _down(u, 2)
        cv = w[0:1] * u2 + w[1:2] * u1 + w[2:3] * u
        db_ref[...] = (g * cv).astype(BF16)
        dcv = g * b
        gw_ref[0:1, :] = jnp.sum(dcv * u2, axis=0, keepdims=True)
        gw_ref[1:2, :] = jnp.sum(dcv * u1, axis=0, keepdims=True)
        gw_ref[2:3, :] = jnp.sum(dcv * u, axis=0, keepdims=True)
        du = w[2:3] * dcv + w[1:2] * _up(dcv, 1) + w[0:1] * _up(dcv, 2)
        dc_ref[...] = (du * h).astype(BF16)
        dh_ref[...] = (du * c).astype(BF16)

    col = lambda k: pl.BlockSpec((s, tc), lambda j, k=k: (0, off + k * nb + j))
    own = pl.BlockSpec((s, tc), lambda j: (0, j))
    wsp = pl.BlockSpec((3, tc), lambda j: (0, j))
    act = jax.ShapeDtypeStruct((s, WIDTH), BF16)
    return pl.pallas_call(
        body, name="conv_mix_bwd", grid=(nb,),
        in_specs=[own, col(0), col(1), col(2), wsp], out_specs=[own, own, own, wsp],
        out_shape=[act, act, act, jax.ShapeDtypeStruct((3, WIDTH), F32)],
        compiler_params=_params(("parallel",)),
    )(do, proj, proj, proj, conv_w)


def _glu_fwd(up, w, b, tc=256):
    s = up.shape[0]
    nb = D_FF // tc

    def body(g_ref, v_ref, w_ref, b_ref, o_ref):
        ug = g_ref[...].astype(F32)
        wv = w_ref[...]
        cg = wv[0:1] * _down(ug, 2) + wv[1:2] * _down(ug, 1) + wv[2:3] * ug + b_ref[...]
        act = cg * jax.nn.sigmoid(cg)
        o_ref[...] = (act * v_ref[...].astype(F32)).astype(BF16)

    return pl.pallas_call(
        body, name="glu_fwd", grid=(nb,),
        in_specs=[pl.BlockSpec((s, tc), lambda j: (0, j)), pl.BlockSpec((s, tc), lambda j: (0, nb + j)),
                  pl.BlockSpec((3, tc), lambda j: (0, j)), pl.BlockSpec((1, tc), lambda j: (0, j))],
        out_specs=pl.BlockSpec((s, tc), lambda j: (0, j)),
        out_shape=jax.ShapeDtypeStruct((s, D_FF), BF16),
        compiler_params=_params(("parallel",)),
    )(up, up, w, b.reshape(1, D_FF))


def _glu_bwd(dh, up, w, b, tc=256):
    s = up.shape[0]
    nb = D_FF // tc

    def body(dh_ref, g_ref, v_ref, w_ref, b_ref, dg_ref, dv_ref, gw_ref, gb_ref):
        ug = g_ref[...].astype(F32)
        uv = v_ref[...].astype(F32)
        d = dh_ref[...].astype(F32)
        wv = w_ref[...]
        u1, u2 = _down(ug, 1), _down(ug, 2)
        cg = wv[0:1] * u2 + wv[1:2] * u1 + wv[2:3] * ug + b_ref[...]
        sg = jax.nn.sigmoid(cg)
        dv_ref[...] = (d * (cg * sg)).astype(BF16)
        dcg = d * uv * (sg * (1.0 + cg * (1.0 - sg)))
        gb_ref[...] = jnp.sum(dcg, axis=0, keepdims=True)
        gw_ref[0:1, :] = jnp.sum(dcg * u2, axis=0, keepdims=True)
        gw_ref[1:2, :] = jnp.sum(dcg * u1, axis=0, keepdims=True)
        gw_ref[2:3, :] = jnp.sum(dcg * ug, axis=0, keepdims=True)
        dg_ref[...] = (wv[2:3] * dcg + wv[1:2] * _up(dcg, 1) + wv[0:1] * _up(dcg, 2)).astype(BF16)

    own = pl.BlockSpec((s, tc), lambda j: (0, j))
    wsp = pl.BlockSpec((3, tc), lambda j: (0, j))
    bsp = pl.BlockSpec((1, tc), lambda j: (0, j))
    act = jax.ShapeDtypeStruct((s, D_FF), BF16)
    dg, dv, gw, gb = pl.pallas_call(
        body, name="glu_bwd", grid=(nb,),
        in_specs=[own, own, pl.BlockSpec((s, tc), lambda j: (0, nb + j)), wsp, bsp],
        out_specs=[own, own, wsp, bsp],
        out_shape=[act, act, jax.ShapeDtypeStruct((3, D_FF), F32), jax.ShapeDtypeStruct((1, D_FF), F32)],
        compiler_params=_params(("parallel",)),
    )(dh, up, up, w, b.reshape(1, D_FF))
    return dg, dv, gw, gb.reshape(D_FF)


def _merge_fwd(x, oc, of, osb, wpc, wpf, wps, proj, gb, wout, tm=256):
    s, d = x.shape
    tm = min(tm, s)

    def body(x_ref, oc_ref, of_ref, os_ref, wpc_ref, wpf_ref, wps_ref, g0_ref, g1_ref, g2_ref, gb_ref, wo_ref,
             xo_ref, mg_ref, yc_ref, yf_ref, ys_ref):
        merged = jnp.zeros((tm, d), F32)
        for k, (o_ref, w_ref, g_ref, y_ref) in enumerate(
                ((oc_ref, wpc_ref, g0_ref, yc_ref), (of_ref, wpf_ref, g1_ref, yf_ref), (os_ref, wps_ref, g2_ref, ys_ref))):
            y = _dot(o_ref[...].astype(BF16), w_ref[...], _NN)
            y_ref[...] = y.astype(BF16)
            gate = jax.nn.sigmoid(g_ref[...].astype(F32) + gb_ref[:, k * d:(k + 1) * d])
            merged = merged + gate * y
        mb = merged.astype(BF16)
        mg_ref[...] = mb
        xo_ref[...] = x_ref[...] + _dot(mb, wo_ref[...], _NN)

    rowd = pl.BlockSpec((tm, d), lambda i: (i, 0))
    roww = pl.BlockSpec((tm, WIDTH), lambda i: (i, 0))
    wp = pl.BlockSpec((WIDTH, d), lambda i: (0, 0))
    gcol = lambda k: pl.BlockSpec((tm, d), lambda i, k=k: (i, GATE_OFF // d + k))
    actd = jax.ShapeDtypeStruct((s, d), BF16)
    return pl.pallas_call(
        body, name="merge_fwd", grid=(s // tm,),
        in_specs=[rowd, roww, roww, roww, wp, wp, wp, gcol(0), gcol(1), gcol(2),
                  pl.BlockSpec((1, 3 * d), lambda i: (0, 0)), pl.BlockSpec((d, d), lambda i: (0, 0))],
        out_specs=[rowd, rowd, rowd, rowd, rowd],
        out_shape=[jax.ShapeDtypeStruct((s, d), F32), actd, actd, actd, actd],
        compiler_params=_params(("parallel",)),
    )(x, oc, of, osb, wpc, wpf, wps, proj, proj, proj, gb.reshape(1, 3 * d), wout)


def _gate_bwd(dm, yc, yf, ys, proj, gb, tm=256):
    s, d = dm.shape
    tm = min(tm, s)

    def body(dm_ref, yc_ref, yf_ref, ys_ref, g0_ref, g1_ref, g2_ref, gb_ref, dyc_ref, dyf_ref, dys_ref, dgl_ref, ggb_ref):
        g = dm_ref[...].astype(F32)
        parts = []
        for k, (y_ref, g_ref, dy_ref) in enumerate(((yc_ref, g0_ref, dyc_ref), (yf_ref, g1_ref, dyf_ref), (ys_ref, g2_ref, dys_ref))):
            gate = jax.nn.sigmoid(g_ref[...].astype(F32) + gb_ref[:, k * d:(k + 1) * d])
            dy_ref[...] = (g * gate).astype(BF16)
            dgl = g * y_ref[...].astype(F32) * gate * (1.0 - gate)
            dgl_ref[:, k * d:(k + 1) * d] = dgl.astype(BF16)
            parts.append(jnp.sum(dgl, axis=0, keepdims=True))
        part = jnp.concatenate(parts, axis=1)

        @pl.when(pl.program_id(0) == 0)
        def _():
            ggb_ref[...] = part

        @pl.when(pl.program_id(0) > 0)
        def _():
            ggb_ref[...] += part

    rowd = pl.BlockSpec((tm, d), lambda i: (i, 0))
    gcol = lambda k: pl.BlockSpec((tm, d), lambda i, k=k: (i, GATE_OFF // d + k))
    vec = pl.BlockSpec((1, 3 * d), lambda i: (0, 0))
    actd = jax.ShapeDtypeStruct((s, d), BF16)
    dyc, dyf, dys, dgl, ggb = pl.pallas_call(
        body, name="gate_bwd", grid=(s // tm,),
        in_specs=[rowd, rowd, rowd, rowd, gcol(0), gcol(1), gcol(2), vec],
        out_specs=[rowd, rowd, rowd, pl.BlockSpec((tm, 3 * d), lambda i: (i, 0)), vec],
        out_shape=[actd, actd, actd, jax.ShapeDtypeStruct((s, 3 * d), BF16), jax.ShapeDtypeStruct((1, 3 * d), F32)],
        compiler_params=_params(("arbitrary",)),
    )(dm, yc, yf, ys, proj, proj, proj, gb.reshape(1, 3 * d))
    return dyc, dyf, dys, dgl, ggb.reshape(3 * d)


def _tri_tables(nq, r, order):
    last = lambda qi: (qi + 1) * r - 1
    if order == "k_outer":
        pairs = [(qi, kj) for kj in range(nq * r) for qi in range(kj // r, nq)]
    elif order == "k_desc":
        pairs = [(qi, kj) for qi in range(nq) for kj in range(last(qi), -1, -1)]
    else:
        pairs = [(qi, kj) for qi in range(nq) for kj in range(last(qi) + 1)]
    qs, ks = zip(*pairs)
    return jnp.asarray(np.array(qs, np.int32)), jnp.asarray(np.array(ks, np.int32)), len(pairs)


def _lo_mask(shape):
    return lax.broadcasted_iota(jnp.int32, shape, len(shape) - 1) < HEAD_DIM


def _head(x, h):
    lo = _lo_mask(x.shape)
    return jnp.where(lo if h == 0 else jnp.logical_not(lo), x, jnp.zeros_like(x))


def _pair_cols(x, h):
    return x[:, h * HEAD_DIM:h * HEAD_DIM + 1]


def _rep(a0, a1, shape):
    return jnp.where(_lo_mask(shape), a0, a1)


def _positions(qi, kj, tq, tk):
    row = qi * tq + lax.broadcasted_iota(jnp.int32, (tq, tk), 0)
    col = kj * tk + lax.broadcasted_iota(jnp.int32, (tq, tk), 1)
    return row, col


def _head_norm(x, g):
    lo = _lo_mask(x.shape)
    sq = x * x
    s0 = jnp.sum(jnp.where(lo, sq, 0.0), axis=-1, keepdims=True)
    s1 = jnp.sum(jnp.where(lo, 0.0, sq), axis=-1, keepdims=True)
    r = jnp.where(lo, lax.rsqrt(s0 / HEAD_DIM + NORM_EPS), lax.rsqrt(s1 / HEAD_DIM + NORM_EPS))
    return x * r, r


def _fox_prep(proj, gq, gk, tm=512):
    s = proj.shape[0]
    tm = min(tm, s)
    off = FOX_OFF // LANES

    def body(q_ref, k_ref, gq_ref, gk_ref, qn_ref, kn_ref):
        qh, _ = _head_norm(q_ref[...].astype(F32), None)
        kh, _ = _head_norm(k_ref[...].astype(F32), None)
        qn_ref[...] = (qh * gq_ref[...] * QK_SCALE).astype(BF16)
        kn_ref[...] = (kh * gk_ref[...]).astype(BF16)

    vec = pl.BlockSpec((1, LANES), lambda p, i: (0, 0))
    own = pl.BlockSpec((tm, LANES), lambda p, i: (i, p))
    act = jax.ShapeDtypeStruct((s, WIDTH), BF16)
    return pl.pallas_call(
        body, name="fox_prep", grid=(N_PAIR, s // tm),
        in_specs=[pl.BlockSpec((tm, LANES), lambda p, i: (i, off + p)),
                  pl.BlockSpec((tm, LANES), lambda p, i: (i, off + N_PAIR + p)), vec, vec],
        out_specs=[own, own], out_shape=[act, act],
        compiler_params=_params(("parallel", "parallel")),
    )(proj, proj, jnp.tile(gq, 2).reshape(1, LANES), jnp.tile(gk, 2).reshape(1, LANES))


def _fox_post(dqs, dkn, proj, gq, gk, tm=512):
    s = proj.shape[0]
    tm = min(tm, s)
    off = FOX_OFF // LANES

    def one(d_ref, x_ref, g_ref, scale, dx_ref, gg_ref, first):
        xhat, r = _head_norm(x_ref[...].astype(F32), None)
        dy = d_ref[...] * scale
        part = jnp.sum(dy * xhat, axis=0, keepdims=True)

        @pl.when(first)
        def _():
            gg_ref[...] = part

        @pl.when(jnp.logical_not(first))
        def _():
            gg_ref[...] += part

        dxh = dy * g_ref[...]
        lo = _lo_mask(dxh.shape)
        pr = dxh * xhat
        m0 = jnp.sum(jnp.where(lo, pr, 0.0), axis=-1, keepdims=True)
        m1 = jnp.sum(jnp.where(lo, 0.0, pr), axis=-1, keepdims=True)
        mean = jnp.where(lo, m0, m1) / HEAD_DIM
        dx_ref[...] = (r * (dxh - xhat * mean)).astype(BF16)

    def body(dq_ref, dk_ref, q_ref, k_ref, gq_ref, gk_ref, dxq_ref, dxk_ref, ggq_ref, ggk_ref):
        first = pl.program_id(1) == 0
        one(dq_ref, q_ref, gq_ref, QK_SCALE, dxq_ref, ggq_ref, first)
        one(dk_ref, k_ref, gk_ref, 1.0, dxk_ref, ggk_ref, first)

    vec = pl.BlockSpec((1, LANES), lambda p, i: (0, 0))
    own = pl.BlockSpec((tm, LANES), lambda p, i: (i, p))
    ggs = pl.BlockSpec((None, 1, LANES), lambda p, i: (p, 0, 0))
    act = jax.ShapeDtypeStruct((s, WIDTH), BF16)
    ggo = jax.ShapeDtypeStruct((N_PAIR, 1, LANES), F32)
    dxq, dxk, ggq, ggk = pl.pallas_call(
        body, name="fox_post", grid=(N_PAIR, s // tm),
        in_specs=[own, own, pl.BlockSpec((tm, LANES), lambda p, i: (i, off + p)),
                  pl.BlockSpec((tm, LANES), lambda p, i: (i, off + N_PAIR + p)), vec, vec],
        out_specs=[own, own, ggs, ggs], out_shape=[act, act, ggo, ggo],
        compiler_params=_params(("parallel", "arbitrary")),
    )(dqs, dkn, proj, proj, jnp.tile(gq, 2).reshape(1, LANES), jnp.tile(gk, 2).reshape(1, LANES))
    fold = lambda a: a.reshape(N_HEADS, HEAD_DIM).sum(axis=0)
    return dxq, dxk, fold(ggq), fold(ggk)


def _split3(x):
    a = x.astype(BF16)
    r = x - a.astype(F32)
    b = r.astype(BF16)
    c = (r - b.astype(F32)).astype(BF16)
    return a, b, c


def _split2(x):
    a = x.astype(BF16)
    b = (x - a.astype(F32)).astype(BF16)
    return a, b


def _log_sigmoid(x):
    return jnp.minimum(x, 0.0) - jnp.log(1.0 + jnp.exp(-jnp.abs(x)))


def _fox_gates(ft, bias):
    h, s = ft.shape
    nb = s // LANES

    def body(f_ref, b_ref, c_ref):
        lf = _log_sigmoid(f_ref[...] + b_ref[...])
        i = lax.broadcasted_iota(jnp.int32, (s, LANES), 0)
        j = pl.program_id(0) * LANES + lax.broadcasted_iota(jnp.int32, (s, LANES), 1)
        tri = jnp.where(i <= j, 1.0, 0.0).astype(BF16)
        c_ref[...] = sum(_dot(p, tri, _NN) for p in _split3(lf))

    return pl.pallas_call(
        body, name="fox_gates", grid=(nb,),
        in_specs=[pl.BlockSpec((h, s), lambda j: (0, 0)), pl.BlockSpec((h, 1), lambda j: (0, 0))],
        out_specs=pl.BlockSpec((h, LANES), lambda j: (0, j)),
        out_shape=jax.ShapeDtypeStruct((h, s), F32),
        compiler_params=_params(("parallel",)),
    )(ft, bias.reshape(h, 1))


def _fox_gates_bwd(dc, ft, bias):
    h, s = ft.shape
    nb = s // LANES

    def body(dc_ref, f_ref, fb_ref, b_ref, df_ref, gb_ref):
        i = lax.broadcasted_iota(jnp.int32, (s, LANES), 0)
        j = pl.program_id(0) * LANES + lax.broadcasted_iota(jnp.int32, (s, LANES), 1)
        tri = jnp.where(i >= j, 1.0, 0.0).astype(BF16)
        dlf = sum(_dot(p, tri, _NN) for p in _split3(dc_ref[...]))
        df = dlf * jax.nn.sigmoid(-(fb_ref[...] + b_ref[...]))
        df_ref[...] = df
        part = jnp.sum(df, axis=-1, keepdims=True)

        @pl.when(pl.program_id(0) == 0)
        def _():
            gb_ref[...] = part

        @pl.when(pl.program_id(0) > 0)
        def _():
            gb_ref[...] += part

    full = pl.BlockSpec((h, s), lambda j: (0, 0))
    blk = pl.BlockSpec((h, LANES), lambda j: (0, j))
    one = pl.BlockSpec((h, 1), lambda j: (0, 0))
    df, gb = pl.pallas_call(
        body, name="fox_gates_bwd", grid=(nb,), in_specs=[full, full, blk, one], out_specs=[blk, one],
        out_shape=[jax.ShapeDtypeStruct((h, s), F32), jax.ShapeDtypeStruct((h, 1), F32)],
        compiler_params=_params(("arbitrary",)),
    )(dc, ft, ft, bias.reshape(h, 1))
    return df, gb.reshape(h)


def _delta_rep(do, o, tm=2048):
    s = do.shape[0]
    tm = min(tm, s)

    def body(do_ref, o_ref, d_ref):
        pr = do_ref[...].astype(F32) * o_ref[...].astype(F32)
        lo = _lo_mask(pr.shape)
        d0 = jnp.sum(jnp.where(lo, pr, 0.0), axis=-1, keepdims=True)
        d1 = jnp.sum(jnp.where(lo, 0.0, pr), axis=-1, keepdims=True)
        d_ref[...] = jnp.where(lo, d0, d1)

    own = pl.BlockSpec((tm, LANES), lambda p, i: (i, p))
    return pl.pallas_call(
        body, name="delta_rep", grid=(N_PAIR, s // tm), in_specs=[own, own],
        out_specs=pl.BlockSpec((None, tm, LANES), lambda p, i: (p, i, 0)),
        out_shape=jax.ShapeDtypeStruct((N_PAIR, s, LANES), F32),
        compiler_params=_params(("parallel", "parallel")),
    )(do, o)


def _tile(s, t):
    t = min(t, s)
    assert s % t == 0
    return t


def _fox_fwd(qn, kn, proj, ccol, crow, t=512, comm=None):
    s = qn.shape[0]
    t = _tile(s, t)
    n = s // t
    qtab, ktab, ntri = _tri_tables(n, 1, "k_asc")
    voff = FOX_OFF // LANES + 2 * N_PAIR
    n_in, n_out = (len(comm.inputs), len(comm.out_shapes)) if comm else (0, 0)

    def body(qt_ref, kt_ref, q_ref, k_ref, v_ref, cc_ref, cr_ref, *rest):
        comm_in, rest = rest[:n_in], rest[n_in:]
        (o_ref, lse_ref), rest = rest[:2], rest[2:]
        comm_out, rest = rest[:n_out], rest[n_out:]
        (m_ref, l_ref, acc_ref), sems = rest[:3], rest[3:]
        i = pl.program_id(1)
        qi, kj = qt_ref[i], kt_ref[i]
        if comm:
            pl.when((pl.program_id(0) == 0) & (i == 0))(lambda: comm.start(comm_in, comm_out, sems))

        @pl.when(kj == 0)
        def _():
            m_ref[...] = jnp.full(m_ref.shape, NEG_INF, F32)
            l_ref[...] = jnp.zeros(l_ref.shape, F32)
            acc_ref[...] = jnp.zeros(acc_ref.shape, F32)

        q, k, v = q_ref[...], k_ref[...], v_ref[...]
        row, col = _positions(qi, kj, t, t)
        causal = col <= row
        m_old = m_ref[...]
        mn, rs, pv = [], [], []
        for h in range(2):
            sc = _dot(_head(q, h), k, _NT) + _pair_cols(cc_ref[...], h) - cr_ref[h:h + 1, :]
            sc = jnp.where(causal, sc, NEG_INF)
            m_new = jnp.maximum(_pair_cols(m_old, h), jnp.max(sc, axis=-1, keepdims=True))
            p = jnp.exp(sc - m_new)
            mn.append(m_new)
            rs.append(jnp.sum(p, axis=-1, keepdims=True))
            pv.append(_dot(p.astype(BF16), _head(v, h), _NN))
        m_rep = _rep(mn[0], mn[1], m_old.shape)
        alpha = jnp.exp(m_old - m_rep)
        l_ref[...] = alpha * l_ref[...] + _rep(rs[0], rs[1], m_old.shape)
        acc_ref[...] = alpha * acc_ref[...] + pv[0] + pv[1]
        m_ref[...] = m_rep

        @pl.when(kj == qi)
        def _():
            o_ref[...] = acc_ref[...] / l_ref[...]
            lse_ref[...] = m_ref[...] + jnp.log(l_ref[...])

        if comm:
            pl.when((pl.program_id(0) == N_PAIR - 1) & (i == ntri - 1))(lambda: comm.finish(comm_in, comm_out, sems))

    grid_spec = pltpu.PrefetchScalarGridSpec(
        num_scalar_prefetch=2, grid=(N_PAIR, ntri),
        in_specs=[pl.BlockSpec((t, LANES), lambda p, i, qt, kt: (qt[i], p)),
                  pl.BlockSpec((t, LANES), lambda p, i, qt, kt: (kt[i], p)),
                  pl.BlockSpec((t, LANES), lambda p, i, qt, kt: (kt[i], voff + p)),
                  pl.BlockSpec((None, t, LANES), lambda p, i, qt, kt: (p, qt[i], 0)),
                  pl.BlockSpec((None, 2, t), lambda p, i, qt, kt: (p, 0, kt[i]))] + [ANY] * n_in,
        out_specs=[pl.BlockSpec((t, LANES), lambda p, i, qt, kt: (qt[i], p)),
                   pl.BlockSpec((None, t, LANES), lambda p, i, qt, kt: (p, qt[i], 0))] + [ANY] * n_out,
        scratch_shapes=[pltpu.VMEM((t, LANES), F32)] * 3 + (comm.sems if comm else []))
    outs = pl.pallas_call(
        body, name="fox_fwd", grid_spec=grid_spec,
        out_shape=[jax.ShapeDtypeStruct((s, WIDTH), F32), jax.ShapeDtypeStruct((N_PAIR, s, LANES), F32)]
        + (comm.out_shapes if comm else []),
        compiler_params=_params(("arbitrary", "arbitrary") if comm else ("parallel", "arbitrary")),
    )(qtab, ktab, qn, kn, proj, ccol, crow, *(comm.inputs if comm else []))
    return outs[:2], outs[2:]


def _fox_bwd(qn, kn, proj, do, lse, delta, ccol, crow, t=512, comm=None):
    s = qn.shape[0]
    t = _tile(s, t)
    n = s // t
    qtab, ktab, ntri = _tri_tables(n, 1, "k_outer")
    voff = FOX_OFF // LANES + 2 * N_PAIR
    n_in, n_out = (len(comm.inputs), len(comm.out_shapes)) if comm else (0, 0)

    def body(qt_ref, kt_ref, q_ref, k_ref, v_ref, do_ref, lse_ref, dl_ref, cc_ref, cr_ref, *rest):
        comm_in, rest = rest[:n_in], rest[n_in:]
        (dq_ref, dk_ref, dv_ref, dc_ref, dcq_ref), rest = rest[:5], rest[5:]
        comm_out, rest = rest[:n_out], rest[n_out:]
        (dka_ref, dva_ref, dca_ref, dcqa_ref), sems = rest[:4], rest[4:]
        i = pl.program_id(1)
        qi, kj = qt_ref[i], kt_ref[i]
        if comm:
            pl.when((pl.program_id(0) == 0) & (i == 0))(lambda: comm.start(comm_in, comm_out, sems))

        @pl.when(i == 0)
        def _():
            dq_ref[...] = jnp.zeros(dq_ref.shape, F32)
            dcqa_ref[...] = jnp.zeros(dcqa_ref.shape, F32)

        @pl.when(qi == kj)
        def _():
            dka_ref[...] = jnp.zeros(dka_ref.shape, F32)
            dva_ref[...] = jnp.zeros(dva_ref.shape, F32)
            dca_ref[...] = jnp.zeros(dca_ref.shape, F32)

        q, k, v, g = q_ref[...], k_ref[...], v_ref[...], do_ref[...]
        row, col = _positions(qi, kj, t, t)
        causal = col <= row
        dq = jnp.zeros((t, LANES), F32)
        dk = jnp.zeros((t, LANES), F32)
        dv = jnp.zeros((t, LANES), F32)
        rowsum = []
        for h in range(2):
            qh, gh = _head(q, h), _head(g, h)
            sc = _dot(qh, k, _NT) + _pair_cols(cc_ref[...], h) - cr_ref[h:h + 1, :]
            p = jnp.where(causal, jnp.exp(sc - _pair_cols(lse_ref[...], h)), 0.0)
            dp = _dot(gh, v, _NT)
            ds = p * (dp - _pair_cols(dl_ref[...], h))
            dsb = ds.astype(BF16)
            dv = dv + _dot(p.astype(BF16), gh, _TN)
            dk = dk + _dot(dsb, qh, _TN)
            dq = dq + _dot(dsb, _head(k, h), _NN)
            dca_ref[h:h + 1, :] -= jnp.sum(ds, axis=0, keepdims=True)
            rowsum.append(jnp.sum(ds, axis=-1, keepdims=True))
        dka_ref[...] += dk
        dva_ref[...] += dv
        rows = pl.ds(pl.multiple_of(qi * t, t), t)
        dq_ref[rows, :] += dq
        dcqa_ref[rows, :] += _rep(rowsum[0], rowsum[1], (t, LANES))

        @pl.when(qi == n - 1)
        def _():
            dk_ref[...] = dka_ref[...]
            dv_ref[...] = dva_ref[...].astype(BF16)
            dc_ref[...] = dca_ref[...]

        @pl.when(i == ntri - 1)
        def _():
            across = dcqa_ref[...].T
            dcq_ref[0:1, :] = across[0:1, :]
            dcq_ref[1:2, :] = across[HEAD_DIM:HEAD_DIM + 1, :]

        if comm:
            pl.when((pl.program_id(0) == N_PAIR - 1) & (i == ntri - 1))(lambda: comm.finish(comm_in, comm_out, sems))

    qblk = lambda p, i, qt, kt: (qt[i], p)
    kblk = lambda p, i, qt, kt: (kt[i], p)
    qrep = pl.BlockSpec((None, t, LANES), lambda p, i, qt, kt: (p, qt[i], 0))
    crs = pl.BlockSpec((None, 2, t), lambda p, i, qt, kt: (p, 0, kt[i]))
    grid_spec = pltpu.PrefetchScalarGridSpec(
        num_scalar_prefetch=2, grid=(N_PAIR, ntri),
        in_specs=[pl.BlockSpec((t, LANES), qblk), pl.BlockSpec((t, LANES), kblk),
                  pl.BlockSpec((t, LANES), lambda p, i, qt, kt: (kt[i], voff + p)),
                  pl.BlockSpec((t, LANES), qblk), qrep, qrep, qrep, crs] + [ANY] * n_in,
        out_specs=[pl.BlockSpec((s, LANES), lambda p, i, qt, kt: (0, p)),
                   pl.BlockSpec((t, LANES), kblk), pl.BlockSpec((t, LANES), kblk), crs,
                   pl.BlockSpec((None, 2, s), lambda p, i, qt, kt: (p, 0, 0))] + [ANY] * n_out,
        scratch_shapes=[pltpu.VMEM((t, LANES), F32), pltpu.VMEM((t, LANES), F32), pltpu.VMEM((2, t), F32),
                        pltpu.VMEM((s, LANES), F32)] + (comm.sems if comm else []))
    outs = pl.pallas_call(
        body, name="fox_bwd", grid_spec=grid_spec,
        out_shape=[jax.ShapeDtypeStruct((s, WIDTH), F32), jax.ShapeDtypeStruct((s, WIDTH), F32),
                   jax.ShapeDtypeStruct((s, WIDTH), BF16), jax.ShapeDtypeStruct((N_PAIR, 2, s), F32),
                   jax.ShapeDtypeStruct((N_PAIR, 2, s), F32)] + (comm.out_shapes if comm else []),
        compiler_params=_params(("arbitrary", "arbitrary") if comm else ("parallel", "arbitrary")),
    )(qtab, ktab, qn, kn, proj, do, lse, delta, ccol, crow, *(comm.inputs if comm else []))
    return outs[:5], outs[5:]


def _sb_tile(qh, k, strict, tk, r_col):
    z = _dot(qh, k, _NT)
    lg = jnp.where(strict, -(jnp.maximum(z, 0.0) + jnp.log(1.0 + jnp.exp(-jnp.abs(z)))), 0.0)
    jj = lax.broadcasted_iota(jnp.int32, (tk, tk), 0)
    ss = lax.broadcasted_iota(jnp.int32, (tk, tk), 1)
    above = jnp.where(jj > ss, 1.0, 0.0).astype(BF16)
    suffix = sum(_dot(p, above, _NN) for p in _split2(lg)) + r_col
    a = jnp.where(strict, jnp.exp(lg + z + suffix), 0.0)
    return z, lg, a


def _sb_fwd(proj, tq=512, tk=256, comm=None):
    s = proj.shape[0]
    tq = _tile(s, tq)
    tk = _tile(tq, tk)
    nq, r = s // tq, tq // tk
    qtab, ktab, ntri = _tri_tables(nq, r, "k_desc")
    off = SB_OFF // LANES
    n_in, n_out = (len(comm.inputs), len(comm.out_shapes)) if comm else (0, 0)

    def body(qt_ref, kt_ref, q_ref, k_ref, v_ref, *rest):
        comm_in, o_ref, rest = rest[:n_in], rest[n_in], rest[n_in + 1:]
        comm_out, rest = rest[:n_out], rest[n_out:]
        (acc_ref, r_ref), sems = rest[:2], rest[2:]
        i = pl.program_id(1)
        qi, kj = qt_ref[i], kt_ref[i]
        if comm:
            pl.when((pl.program_id(0) == 0) & (i == 0))(lambda: comm.start(comm_in, comm_out, sems))

        @pl.when(kj == (qi + 1) * r - 1)
        def _():
            acc_ref[...] = jnp.zeros(acc_ref.shape, F32)
            r_ref[...] = jnp.zeros(r_ref.shape, F32)

        q = q_ref[...] * QK_SCALE
        k, v = k_ref[...], v_ref[...]
        row, col = _positions(qi, kj, tq, tk)
        strict = col < row
        acc = acc_ref[...]
        for h in range(2):
            _, lg, a = _sb_tile(_head(q, h), k, strict, tk, r_ref[h])
            acc = acc + _dot(a.astype(BF16), _head(v, h), _NN)
            r_ref[h] += jnp.sum(lg, axis=-1, keepdims=True)
        acc_ref[...] = acc

        @pl.when(kj == 0)
        def _():
            o_ref[...] = acc_ref[...]

        if comm:
            pl.when((pl.program_id(0) == N_PAIR - 1) & (i == ntri - 1))(lambda: comm.finish(comm_in, comm_out, sems))

    grid_spec = pltpu.PrefetchScalarGridSpec(
        num_scalar_prefetch=2, grid=(N_PAIR, ntri),
        in_specs=[pl.BlockSpec((tq, LANES), lambda p, i, qt, kt: (qt[i], off + p)),
                  pl.BlockSpec((tk, LANES), lambda p, i, qt, kt: (kt[i], off + N_PAIR + p)),
                  pl.BlockSpec((tk, LANES), lambda p, i, qt, kt: (kt[i], off + 2 * N_PAIR + p))] + [ANY] * n_in,
        out_specs=[pl.BlockSpec((tq, LANES), lambda p, i, qt, kt: (qt[i], p))] + [ANY] * n_out,
        scratch_shapes=[pltpu.VMEM((tq, LANES), F32), pltpu.VMEM((2, tq, 1), F32)] + (comm.sems if comm else []))
    outs = pl.pallas_call(
        body, name="sb_fwd", grid_spec=grid_spec,
        out_shape=[jax.ShapeDtypeStruct((s, WIDTH), F32)] + (comm.out_shapes if comm else []),
        compiler_params=_params(("arbitrary", "arbitrary") if comm else ("parallel", "arbitrary")),
    )(qtab, ktab, proj, proj, proj, *(comm.inputs if comm else []))
    return outs[0], outs[1:]


def _sb_bwd(proj, do, delta, tq=512, tk=256, comm=None):
    s = proj.shape[0]
    tq = _tile(s, tq)
    tk = _tile(tq, tk)
    nq, r = s // tq, tq // tk
    qtab, ktab, ntri = _tri_tables(nq, r, "k_desc")
    off = SB_OFF // LANES
    n_in, n_out = (len(comm.inputs), len(comm.out_shapes)) if comm else (0, 0)

    def body(qt_ref, kt_ref, q_ref, k_ref, v_ref, do_ref, dl_ref, *rest):
        comm_in, rest = rest[:n_in], rest[n_in:]
        (dq_ref, dk_ref, dv_ref), rest = rest[:3], rest[3:]
        comm_out, rest = rest[:n_out], rest[n_out:]
        (dqa_ref, dka_ref, dva_ref, r_ref, rd_ref), sems = rest[:5], rest[5:]
        i = pl.program_id(1)
        qi, kj = qt_ref[i], kt_ref[i]
        if comm:
            pl.when((pl.program_id(0) == 0) & (i == 0))(lambda: comm.start(comm_in, comm_out, sems))

        @pl.when(i == 0)
        def _():
            dka_ref[...] = jnp.zeros(dka_ref.shape, F32)
            dva_ref[...] = jnp.zeros(dva_ref.shape, F32)

        @pl.when(kj == (qi + 1) * r - 1)
        def _():
            dqa_ref[...] = jnp.zeros(dqa_ref.shape, F32)
            r_ref[...] = jnp.zeros(r_ref.shape, F32)
            rd_ref[...] = jnp.zeros(rd_ref.shape, F32)

        q = q_ref[...] * QK_SCALE
        k, v, g = k_ref[...], v_ref[...], do_ref[...]
        row, col = _positions(qi, kj, tq, tk)
        strict = col < row
        ss = lax.broadcasted_iota(jnp.int32, (tk, tk), 0)
        jj = lax.broadcasted_iota(jnp.int32, (tk, tk), 1)
        at_or_after = jnp.where(ss >= jj, 1.0, 0.0).astype(BF16)
        dq = jnp.zeros((tq, LANES), F32)
        dk = jnp.zeros((tk, LANES), F32)
        dv = jnp.zeros((tk, LANES), F32)
        for h in range(2):
            qh, gh = _head(q, h), _head(g, h)
            z, lg, a = _sb_tile(qh, k, strict, tk, r_ref[h])
            da = _dot(gh, v, _NT)
            ab = a.astype(BF16)
            d_a = da * ab.astype(F32)
            incl = _dot(d_a.astype(BF16), at_or_after, _NN)
            d_l = _pair_cols(dl_ref[...], h) - rd_ref[h] - incl
            sig = jnp.exp(lg + z)
            dz = jnp.where(strict, d_a * (1.0 - sig) - d_l * sig, 0.0).astype(BF16)
            dq = dq + _dot(dz, _head(k, h), _NN)
            dk = dk + _dot(dz, qh, _TN)
            dv = dv + _dot(ab, gh, _TN)
            r_ref[h] += jnp.sum(lg, axis=-1, keepdims=True)
            rd_ref[h] += jnp.sum(d_a, axis=-1, keepdims=True)
        dqa_ref[...] += dq
        rows = pl.ds(pl.multiple_of(kj * tk, tk), tk)
        dka_ref[rows, :] += dk
        dva_ref[rows, :] += dv

        @pl.when(kj == 0)
        def _():
            dq_ref[...] = (dqa_ref[...] * QK_SCALE).astype(BF16)

        @pl.when(i == ntri - 1)
        def _():
            dk_ref[...] = dka_ref[...].astype(BF16)
            dv_ref[...] = dva_ref[...].astype(BF16)

        if comm:
            pl.when((pl.program_id(0) == N_PAIR - 1) & (i == ntri - 1))(lambda: comm.finish(comm_in, comm_out, sems))

    qblk = lambda p, i, qt, kt: (qt[i], p)
    whole = pl.BlockSpec((s, LANES), lambda p, i, qt, kt: (0, p))
    grid_spec = pltpu.PrefetchScalarGridSpec(
        num_scalar_prefetch=2, grid=(N_PAIR, ntri),
        in_specs=[pl.BlockSpec((tq, LANES), lambda p, i, qt, kt: (qt[i], off + p)),
                  pl.BlockSpec((tk, LANES), lambda p, i, qt, kt: (kt[i], off + N_PAIR + p)),
                  pl.BlockSpec((tk, LANES), lambda p, i, qt, kt: (kt[i], off + 2 * N_PAIR + p)),
                  pl.BlockSpec((tq, LANES), qblk),
                  pl.BlockSpec((None, tq, LANES), lambda p, i, qt, kt: (p, qt[i], 0))] + [ANY] * n_in,
        out_specs=[pl.BlockSpec((tq, LANES), qblk), whole, whole] + [ANY] * n_out,
        scratch_shapes=[pltpu.VMEM((tq, LANES), F32), pltpu.VMEM((s, LANES), F32), pltpu.VMEM((s, LANES), F32),
                        pltpu.VMEM((2, tq, 1), F32), pltpu.VMEM((2, tq, 1), F32)] + (comm.sems if comm else []))
    act = jax.ShapeDtypeStruct((s, WIDTH), BF16)
    outs = pl.pallas_call(
        body, name="sb_bwd", grid_spec=grid_spec, out_shape=[act, act, act] + (comm.out_shapes if comm else []),
        compiler_params=_params(("arbitrary", "arbitrary") if comm else ("parallel", "arbitrary")),
    )(qtab, ktab, proj, proj, proj, do, delta, *(comm.inputs if comm else []))
    return outs[:3], outs[3:]


def _loss_head(y, target, tm=256):
    s, d = y.shape
    tm = min(tm, s)

    def body(y_ref, t_ref, l_ref, dy_ref):
        e = y_ref[...] - t_ref[...]
        dy_ref[...] = e / d
        part = jnp.sum(jnp.sum(e * e, axis=0, keepdims=True), axis=1, keepdims=True)

        @pl.when(pl.program_id(0) == 0)
        def _():
            l_ref[...] = jnp.broadcast_to(part, l_ref.shape)

        @pl.when(pl.program_id(0) > 0)
        def _():
            l_ref[...] += jnp.broadcast_to(part, l_ref.shape)

    row = pl.BlockSpec((tm, d), lambda i: (i, 0))
    return pl.pallas_call(
        body, name="loss_head", grid=(s // tm,), in_specs=[row, row],
        out_specs=[pl.BlockSpec((1, LANES), lambda i: (0, 0)), row],
        out_shape=[jax.ShapeDtypeStruct((1, LANES), F32), jax.ShapeDtypeStruct((s, d), F32)],
        compiler_params=_params(("arbitrary",)),
    )(y, target)


def _adamw(w, g, m, v, name):
    shape = w.shape
    cols = shape[-1]
    rows = int(np.prod(shape[:-1]))
    tm = rows
    for cand in (512, 256, 128, 64, 32, 16, 8):
        if rows % cand == 0 and rows > cand and cand * cols * 4 <= ADAMW_BLOCK_BYTES:
            tm = cand
            break

    def body(w_ref, g_ref, m_ref, v_ref, d_ref, mo_ref, vo_ref):
        gr = g_ref[...]
        mn = ADAM_B1 * m_ref[...] + (1.0 - ADAM_B1) * gr
        vn = ADAM_B2 * v_ref[...] + (1.0 - ADAM_B2) * (gr * gr)
        m_hat = mn / (1.0 - ADAM_B1 ** ADAM_STEP)
        v_hat = vn / (1.0 - ADAM_B2 ** ADAM_STEP)
        d_ref[...] = -ADAM_LR * (m_hat / (jnp.sqrt(v_hat) + ADAM_EPS) + ADAM_WD * w_ref[...])
        mo_ref[...] = mn
        vo_ref[...] = vn

    blk = pl.BlockSpec((tm, cols), lambda i: (i, 0))
    out = jax.ShapeDtypeStruct((rows, cols), F32)
    r2 = lambda a: a.reshape(rows, cols)
    outs = pl.pallas_call(
        body, name=name, grid=(rows // tm,), in_specs=[blk] * 4, out_specs=[blk] * 3, out_shape=[out] * 3,
        compiler_params=_params(("parallel",)),
    )(r2(w), r2(g), r2(m), r2(v))
    return tuple(o.reshape(shape) for o in outs)


W_IN_NATURAL = (("a", CONV_OFF, 3 * WIDTH), ("a", FOX_OFF, 3 * WIDTH), ("f", 0, N_HEADS), ("a", SB_OFF, 3 * WIDTH),
                ("a", GATE_OFF, 3 * D_MODEL))
W_IN_COLS = D_INA + N_HEADS
W_IN_SHARD = W_IN_COLS // 4


def _w_in_from_chips(blocks):
    def natural(lo, hi):
        cols = []
        for k, blk in enumerate(blocks):
            a, b = max(lo, k * W_IN_SHARD), min(hi, (k + 1) * W_IN_SHARD)
            if a < b:
                cols.append(blk[:, a - k * W_IN_SHARD:b - k * W_IN_SHARD])
        return cols

    start = {}
    pos = 0
    for part, at, length in W_IN_NATURAL:
        start[part, at] = (pos, pos + length)
        pos += length
    order = sorted((at, rng) for (part, at), rng in start.items() if part == "a")
    wa = jnp.concatenate([c for _, (lo, hi) in order for c in natural(lo, hi)], axis=1)
    wf = jnp.pad(jnp.concatenate(natural(*start["f", 0]), axis=1), ((0, 0), (0, LANES - N_HEADS)))
    return wa, wf


def _w_in_to_chips(ga, gf):
    chips = []
    for k in range(4):
        lo, hi, pos, cols = k * W_IN_SHARD, (k + 1) * W_IN_SHARD, 0, []
        for part, at, length in W_IN_NATURAL:
            a, b = max(lo, pos), min(hi, pos + length)
            if a < b:
                cols.append((ga if part == "a" else gf)[:, at + a - pos:at + b - pos])
            pos += length
        chips.append(jnp.concatenate(cols, axis=1))
    return jnp.stack(chips)


def _layer_fwd(x, p, plan, l):
    s = x.shape[0]
    proj, hn1 = _norm_mm(x, p["norm1_g"], p["wa"], BF16, name="in_proj")
    fraw = _mm(hn1, p["wf"], "nn", F32, name="in_proj_f")
    ft = fraw[:, :N_HEADS].T
    crow8 = _fox_gates(ft, p["fox_f_bias"])
    crow = crow8.reshape(N_PAIR, 2, s)
    ccol = jnp.repeat(crow.transpose(0, 2, 1), HEAD_DIM, axis=2)
    oc = _conv_mix_fwd(proj, p["conv_w"])
    qn, kn = _fox_prep(proj, p["fox_q_norm_g"], p["fox_k_norm_g"])
    last = l + 1 == plan.depth
    (of, lse), got_in = _fox_fwd(qn, kn, proj, ccol, crow, comm=None if last else plan.gather(l + 1, BIG_IN))
    osb, got_rest = _sb_fwd(proj, comm=plan.gather(l, BIG_REST))
    p.update(plan.weights(l, BIG_REST, got_rest))
    nxt = None if last else plan.layer(l + 1, got_in)
    xm, merged, yc, yf, ys = _merge_fwd(x, oc, of, osb, p["w_proj_conv"], p["w_proj_fox"], p["w_proj_sb"], proj,
                                        p["gate_bias"], p["w_out"])
    up, hn2 = _norm_mm(xm, p["norm2_g"], p["w_up"], BF16, name="up_proj")
    hh = _glu_fwd(up, p["ffn_conv_w"], p["ffn_conv_b"])
    xo = _mm(hh, p["w_down"], "nn", F32, res=xm, name="down_proj")
    saved = dict(x=x, hn1=hn1, proj=proj, ft=ft, crow=crow, ccol=ccol, oc=oc, qn=qn, kn=kn, of=of, lse=lse, osb=osb,
                 merged=merged, yc=yc, yf=yf, ys=ys, xm=xm, hn2=hn2, up=up, hh=hh)
    return xo, saved, nxt


def _layer_bwd(dx, p, a, plan, pending_in, first_layer):
    s = dx.shape[0]
    g = {}
    dhh = _mm(dx, p["w_down"], "nt", BF16, tn=1408, name="d_down_in")
    g["w_down"] = _mm(a["hh"], dx, "tn", F32, name="g_w_down")
    dug, duv, g["ffn_conv_w"], g["ffn_conv_b"] = _glu_bwd(dhh, a["up"], p["ffn_conv_w"], p["ffn_conv_b"])
    dup = jnp.concatenate([dug, duv], axis=1)
    g["w_up"] = _mm(a["hn2"], dup, "tn", F32, tn=1408, name="g_w_up")
    dhn2 = _mm(dup, p["w_up"], "nt", F32, name="d_up_in")
    dx, g["norm2_g"] = _norm_bwd(dhn2, a["xm"], p["norm2_g"], dx, name="norm2_bwd")
    dm = _mm(dx, p["w_out"], "nt", BF16, name="d_out_in")
    g["w_out"] = _mm(a["merged"], dx, "tn", F32, tn=512, name="g_w_out")
    dyc, dyf, dys, dgl, g["gate_bias"] = _gate_bwd(dm, a["yc"], a["yf"], a["ys"], a["proj"], p["gate_bias"])
    doc = _mm(dyc, p["w_proj_conv"], "nt", BF16, name="d_pc_in")
    dof = _mm(dyf, p["w_proj_fox"], "nt", BF16, name="d_pf_in")
    dos = _mm(dys, p["w_proj_sb"], "nt", BF16, name="d_ps_in")
    g["w_proj_conv"] = _mm(a["oc"], dyc, "tn", F32, name="g_w_pc")
    g["w_proj_fox"] = _mm(a["of"], dyf, "tn", F32, name="g_w_pf")
    g["w_proj_sb"] = _mm(a["osb"], dys, "tn", F32, name="g_w_ps")
    dcb, dcc, dch, g["conv_w"] = _conv_mix_bwd(doc, a["proj"], p["conv_w"])
    delta_f = _delta_rep(dof, a["of"])
    pending = plan.reduction(pending_in, {k: g[k] for k in BIG_REST})
    (dqs, dkn, dfv, dcrow, dcq), recv_a = _fox_bwd(a["qn"], a["kn"], a["proj"], dof, a["lse"], delta_f, a["ccol"], a["crow"],
                                                   comm=pending.exchange() if pending else None)
    dc = (dcrow + dcq).reshape(N_HEADS, s)
    dfq, dfk, g["fox_q_norm_g"], g["fox_k_norm_g"] = _fox_post(dqs, dkn, a["proj"], p["fox_q_norm_g"], p["fox_k_norm_g"])
    dft, g["fox_f_bias"] = _fox_gates_bwd(dc, a["ft"], p["fox_f_bias"])
    delta_s = _delta_rep(dos, a["osb"])
    (dsq, dsk, dsv), recv_b = _sb_bwd(a["proj"], dos, delta_s, comm=pending.sums(recv_a) if pending else None)
    done_in = None
    if pending:
        done = pending.finish(recv_b)
        if pending_in is not None:
            done_in, done = done[0], done[1:]
        g.update(zip(BIG_REST, done))
    dproj = jnp.concatenate([dgl, dcb, dcc, dch, dfq, dfk, dfv, dsq, dsk, dsv], axis=1)
    dfp = jnp.pad(dft.T, ((0, 0), (0, LANES - N_HEADS))).astype(BF16)
    ga = _mm(a["hn1"], dproj, "tn", F32, tn=768, name="g_w_in")
    gf = _mm(a["hn1"], dfp, "tn", F32, name="g_w_in_f")
    g["w_in"] = _w_in_to_chips(ga, gf)
    dhn1 = _mm(dfp, p["wf"], "nt", F32, name="d_in_f")
    own = plan.reduction(g["w_in"], {}) if first_layer else None
    if own:
        second = own.sums(_run_comm(own.exchange(), "rs_pair_exchange"))
        dhn1, recv_b = _mm(dproj, p["wa"], "nt", F32, res=dhn1, tk=1536, name="d_in", comm=second)
        (g["w_in"],) = own.finish(recv_b)
    else:
        dhn1 = _mm(dproj, p["wa"], "nt", F32, res=dhn1, tk=1536, name="d_in")
    dx, g["norm1_g"] = _norm_bwd(dhn1, a["x"], p["norm1_g"], dx, name="norm1_bwd")
    return dx, g, done_in


MATMUL_WEIGHTS = ("w_in", "w_proj_conv", "w_proj_fox", "w_proj_sb", "w_out", "w_up", "w_down")
WEIGHTS = ("norm1_g", "w_in", "fox_f_bias", "gate_bias", "conv_w", "fox_q_norm_g", "fox_k_norm_g", "w_proj_conv",
           "w_proj_fox", "w_proj_sb", "w_out", "norm2_g", "w_up", "ffn_conv_w", "ffn_conv_b", "w_down")


def _local_step(x, target, plan):
    depth = plan.depth
    layers, saved = [plan.layer(0, None)], []
    for l in range(depth):
        x, a, nxt = _layer_fwd(x, layers[l], plan, l)
        saved.append(a)
        if nxt is not None:
            layers.append(nxt)
    sq, dx = _loss_head(x, target)
    grads, pending_in = [None] * depth, None
    for l in reversed(range(depth)):
        dx, grads[l], done_in = _layer_bwd(dx, layers[l], saved[l], plan, pending_in, l == 0)
        if done_in is not None:
            grads[l + 1]["w_in"] = done_in
        pending_in = grads[l]["w_in"]
    return sq, dx, grads


ANY = pl.BlockSpec(memory_space=pl.ANY)


def _place():
    x, y, c = lax.axis_index("x"), lax.axis_index("y"), lax.axis_index("c")
    chips = [(1 - x, y), (x, 1 - y), (1 - x, 1 - y)]
    return x, y, c, chips


_Comm = collections.namedtuple("_Comm", "inputs out_shapes sems start finish")


def _run_comm(comm, name):
    n_in, n_out = len(comm.inputs), len(comm.out_shapes)

    def body(*refs):
        ins, outs, sems = refs[:n_in], refs[n_in:n_in + n_out], refs[n_in + n_out:]
        comm.start(ins, outs, sems)
        comm.finish(ins, outs, sems)

    return pl.pallas_call(body, name=name, in_specs=[ANY] * n_in, out_specs=[ANY] * n_out, out_shape=comm.out_shapes,
                          scratch_shapes=comm.sems)(*comm.inputs)


def _gather_comm(shards):
    n = len(shards)

    def copy(x_refs, out_refs, sems, k, t, chip_index, which_half, to, from_input=False):
        half = shards[t].shape[0] // 2
        rows = pl.ds(which_half * half, half)
        dst = out_refs[t].at[chip_index, rows, :]
        return pltpu.make_async_remote_copy(
            src_ref=x_refs[t].at[rows, :] if from_input else dst, dst_ref=dst,
            send_sem=sems[0].at[k, t], recv_sem=sems[1].at[k, t], device_id=to, device_id_type=MESH)

    def first(x_refs, out_refs, sems):
        x, y, c, chips = _place()
        return [copy(x_refs, out_refs, sems, p, t, 2 * x + y, c, (*chip, c), from_input=True)
                for t in range(n) for p, chip in enumerate(chips)]

    def start(x_refs, out_refs, sems):
        for cp in first(x_refs, out_refs, sems):
            cp.start()

    def finish(x_refs, out_refs, sems):
        x, y, c, chips = _place()
        passed = []
        for t in range(n):
            for p, chip in enumerate(chips):
                copy(x_refs, out_refs, sems, p, t, 2 * chip[0] + chip[1], c, (x, y, c)).wait_recv()
                fwd = copy(x_refs, out_refs, sems, 3 + p, t, 2 * chip[0] + chip[1], c, (x, y, 1 - c))
                fwd.start()
                passed.append(fwd)
        for t in range(n):
            for p, chip in enumerate(chips):
                copy(x_refs, out_refs, sems, 3 + p, t, 2 * chip[0] + chip[1], 1 - c, (x, y, c)).wait_recv()
        for cp in first(x_refs, out_refs, sems) + passed:
            cp.wait_send()

    return _Comm(list(shards), [jax.ShapeDtypeStruct((4, *s.shape), s.dtype) for s in shards],
                 [pltpu.SemaphoreType.DMA((6, n)), pltpu.SemaphoreType.DMA((6, n))], start, finish)


def _own_block(got, shards):
    chip = 2 * lax.axis_index("x") + lax.axis_index("y")
    return [lax.dynamic_update_index_in_dim(g, s, chip, 0) for g, s in zip(got, shards)]


def _pair_exchange_comm(gs):
    n = len(gs)

    def copies(g_refs, recv_refs, sems):
        x, y, c, _ = _place()
        return [pltpu.make_async_remote_copy(
            src_ref=g_refs[t].at[:, pl.ds((1 - c) * (gs[t].shape[1] // 2), gs[t].shape[1] // 2), :], dst_ref=recv_refs[t],
            send_sem=sems[0].at[t], recv_sem=sems[1].at[t], device_id=(x, y, 1 - c), device_id_type=MESH) for t in range(n)]

    def start(g_refs, recv_refs, sems):
        for cp in copies(g_refs, recv_refs, sems):
            cp.start()

    def finish(g_refs, recv_refs, sems):
        for cp in copies(g_refs, recv_refs, sems):
            cp.wait()

    return _Comm(list(gs), [jax.ShapeDtypeStruct((4, g.shape[1] // 2, g.shape[2]), g.dtype) for g in gs],
                 [pltpu.SemaphoreType.DMA((n,)), pltpu.SemaphoreType.DMA((n,))], start, finish)


def _pair_sum(g, recv, core, tr=256):
    n, r, cols = g.shape
    half = r // 2
    tr = _row_tile(half, tr)
    nb = half // tr

    def body(c_ref, g_ref, r_ref, o_ref):
        o_ref[...] = (g_ref[...] + r_ref[...]).astype(BF16)

    grid_spec = pltpu.PrefetchScalarGridSpec(
        num_scalar_prefetch=1, grid=(n, nb),
        in_specs=[pl.BlockSpec((None, tr, cols), lambda k, i, c: (k, c[0] * nb + i, 0)),
                  pl.BlockSpec((None, tr, cols), lambda k, i, c: (k, i, 0))],
        out_specs=pl.BlockSpec((None, tr, cols), lambda k, i, c: (k, i, 0)))
    return pl.pallas_call(
        body, name="rs_pair_sum", grid_spec=grid_spec, out_shape=jax.ShapeDtypeStruct((n, half, cols), BF16),
        compiler_params=_params(("parallel", "parallel")),
    )(core, g, recv)


def _row_tile(rows, pref):
    best = None
    for t in range(16, min(rows, pref) + 1, 16):
        if rows % t == 0:
            best = t
    assert best is not None, (rows, pref)
    return best


def _chip_exchange_comm(s1s):
    n = len(s1s)

    def copies(s_refs, recv_refs, sems):
        x, y, c, chips = _place()
        return [pltpu.make_async_remote_copy(
            src_ref=s_refs[t].at[2 * chip[0] + chip[1]], dst_ref=recv_refs[t].at[p],
            send_sem=sems[0].at[p, t], recv_sem=sems[1].at[p, t], device_id=(*chip, c), device_id_type=MESH)
            for t in range(n) for p, chip in enumerate(chips)]

    def start(s_refs, recv_refs, sems):
        for cp in copies(s_refs, recv_refs, sems):
            cp.start()

    def finish(s_refs, recv_refs, sems):
        for cp in copies(s_refs, recv_refs, sems):
            cp.wait()

    return _Comm(list(s1s), [jax.ShapeDtypeStruct((3, *s.shape[1:]), s.dtype) for s in s1s],
                 [pltpu.SemaphoreType.DMA((3, n)), pltpu.SemaphoreType.DMA((3, n))], start, finish)


def _final_sum(g, recv_a, recv_b, core, chip, tr=256):
    n, r, cols = g.shape
    half = r // 2
    tr = _row_tile(half, tr)
    nb = half // tr

    def body(c_ref, k_ref, g_ref, a_ref, b0_ref, b1_ref, b2_ref, o_ref):
        total = g_ref[...] + a_ref[...]
        for b_ref in (b0_ref, b1_ref, b2_ref):
            total = total + b_ref[...].astype(F32)
        o_ref[...] = total

    rel = lambda p: pl.BlockSpec((None, tr, cols), lambda i, c, k, p=p: (p, i, 0))
    grid_spec = pltpu.PrefetchScalarGridSpec(
        num_scalar_prefetch=2, grid=(nb,),
        in_specs=[pl.BlockSpec((None, tr, cols), lambda i, c, k: (k[0], c[0] * nb + i, 0)),
                  pl.BlockSpec((None, tr, cols), lambda i, c, k: (k[0], i, 0)), rel(0), rel(1), rel(2)],
        out_specs=pl.BlockSpec((tr, cols), lambda i, c, k: (c[0] * nb + i, 0)))
    return pl.pallas_call(
        body, name="rs_final_sum", grid_spec=grid_spec, out_shape=jax.ShapeDtypeStruct((r, cols), F32),
        compiler_params=_params(("parallel",)),
    )(core, chip, g, recv_a, recv_b, recv_b, recv_b)


def _pair_join(fs):
    n = len(fs)

    def body(*refs):
        out_refs = refs[n:2 * n]
        send_sems, recv_sems = refs[2 * n:]
        x, y, c, _ = _place()

        def copy(t, which_half):
            h = fs[t].shape[0] // 2
            rows = out_refs[t].at[pl.ds(which_half * h, h), :]
            return pltpu.make_async_remote_copy(
                src_ref=rows, dst_ref=rows, send_sem=send_sems.at[t], recv_sem=recv_sems.at[t],
                device_id=(x, y, 1 - c), device_id_type=MESH)

        sends = [copy(t, c) for t in range(n)]
        for cp in sends:
            cp.start()
        for t in range(n):
            sends[t].wait_send()
            copy(t, 1 - c).wait_recv()

    return pl.pallas_call(
        body, name="rs_pair_join", in_specs=[ANY] * n, out_specs=[ANY] * n,
        out_shape=[jax.ShapeDtypeStruct(f.shape, f.dtype) for f in fs],
        input_output_aliases={t: t for t in range(n)},
        scratch_shapes=[pltpu.SemaphoreType.DMA((n,)), pltpu.SemaphoreType.DMA((n,))],
    )(*fs)


class _Reduction:
    def __init__(self, gs):
        self.gs = gs
        self.core = lax.axis_index("c").astype(jnp.int32).reshape(1)
        self.chip = (2 * lax.axis_index("x") + lax.axis_index("y")).astype(jnp.int32).reshape(1)

    def exchange(self):
        return _pair_exchange_comm(self.gs)

    def sums(self, recv_a):
        self.recv_a = list(recv_a)
        return _chip_exchange_comm([_pair_sum(g, ra, self.core) for g, ra in zip(self.gs, self.recv_a)])

    def finish(self, recv_b):
        return _pair_join([_final_sum(g, ra, rb, self.core, self.chip) for g, ra, rb in zip(self.gs, self.recv_a, recv_b)])

    def run(self):
        recv_a = _run_comm(self.exchange(), "rs_pair_exchange")
        return self.finish(_run_comm(self.sums(recv_a), "rs_chip_exchange"))


def _all_reduce_small(v):
    r, cols = v.shape

    def body(v_ref, out_ref, buf_ref, send_sems, recv_sems):
        x, y, c, _ = _place()
        flip = lambda a, bit: 1 - a if bit else a
        buf_ref[4 * x + 2 * y + c] = v_ref[...]
        cps = []
        for rel in range(1, 8):
            peer = (flip(x, rel & 4), flip(y, rel & 2), flip(c, rel & 1))
            cps.append(pltpu.make_async_remote_copy(
                src_ref=v_ref, dst_ref=buf_ref.at[4 * x + 2 * y + c], send_sem=send_sems.at[rel - 1], recv_sem=recv_sems.at[rel - 1],
                device_id=peer, device_id_type=MESH))
        for cp in cps:
            cp.start()
        for cp in cps:
            cp.wait()
        total = buf_ref[0]
        for d in range(1, 8):
            total = total + buf_ref[d]
        out_ref[...] = total

    vm = pl.BlockSpec(memory_space=pltpu.VMEM)
    return pl.pallas_call(
        body, name="all_reduce_small", in_specs=[vm], out_specs=vm, out_shape=jax.ShapeDtypeStruct((r, cols), F32),
        scratch_shapes=[pltpu.VMEM((8, r, cols), F32), pltpu.SemaphoreType.DMA((7,)), pltpu.SemaphoreType.DMA((7,))],
    )(v)


SHARD_AXIS = {"w_in": 1, "conv_w": 1, "w_proj_conv": 1, "w_proj_fox": 1, "w_proj_sb": 1, "w_out": 0, "w_up": 1,
              "ffn_conv_w": 1, "w_down": 0}
SMALL_SHARDED = ("conv_w", "ffn_conv_w")
BIG = tuple(k for k in SHARD_AXIS if k not in SMALL_SHARDED)
BIG_IN = ("w_in",)
BIG_REST = tuple(k for k in BIG if k not in BIG_IN)
REPLICATED = tuple(k for k in WEIGHTS if k not in SHARD_AXIS)
SMALL = REPLICATED + SMALL_SHARDED


def _pack_small(parts, row_align):
    flat = jnp.concatenate([p.reshape(-1) for p in parts])
    rows = -(-flat.shape[0] // (PACK_COLS * row_align)) * row_align
    return jnp.pad(flat, (0, rows * PACK_COLS - flat.shape[0])).reshape(rows, PACK_COLS)


def _unpack_small(packed, shapes):
    flat = packed.reshape(-1)
    out, off = [], 0
    for shape in shapes:
        size = int(np.prod(shape))
        out.append(flat[off:off + size].reshape(shape))
        off += size
    return out


def _blocks(stacked, own):
    chip = 2 * lax.axis_index("x") + lax.axis_index("y")
    return [jnp.where(chip == k, own, stacked[k]) for k in range(4)]


def _to_chips(full, axis):
    a, b = full.shape
    if axis == 0:
        return full.reshape(4, a // 4, b)
    return jnp.moveaxis(full.reshape(a, 4, b // 4), 1, 0)


class _Plan:
    def __init__(self, given):
        self.given = given
        self.depth = given["norm1_g"].shape[0]
        conv_shapes = [given[k].shape for k in SMALL_SHARDED]
        packed = [_pack_small([given[k] for k in SMALL_SHARDED], 16)]
        (got,) = _own_block(_run_comm(_gather_comm(packed), "gather_conv_weights"), packed)
        by_chip = [_unpack_small(got[j], conv_shapes) for j in range(4)]
        self.conv = {k: jnp.concatenate([by_chip[j][i] for j in range(4)], axis=-1) for i, k in enumerate(SMALL_SHARDED)}
        self.shards = {}

    def _shards(self, l, names):
        if (l, names) not in self.shards:
            self.shards[l, names] = [self.given[k][l].astype(BF16) for k in names]
        return self.shards[l, names]

    def gather(self, l, names):
        return _gather_comm(self._shards(l, names))

    def weights(self, l, names, got):
        out = {}
        for k, stacked, own in zip(names, got, self._shards(l, names)):
            if k == "w_in":
                out["wa"], out["wf"] = _w_in_from_chips(_blocks(stacked, own))
            else:
                out[k] = jnp.concatenate(_blocks(stacked, own), axis=SHARD_AXIS[k])
        return out

    def layer(self, l, got_in):
        if got_in is None:
            got_in = _run_comm(self.gather(l, BIG_IN), "gather_w_in")
        p = {k: self.given[k][l] for k in REPLICATED}
        p.update({k: self.conv[k][l] for k in SMALL_SHARDED})
        p.update(self.weights(l, BIG_IN, got_in))
        return p

    def reduction(self, w_in_grad, rest):
        gs = [] if w_in_grad is None else [w_in_grad]
        return _Reduction(gs + [_to_chips(rest[k], SHARD_AXIS[k]) for k in BIG_REST if k in rest])


def kernel(x, norm1_g, w_in, fox_f_bias, gate_bias, conv_w, fox_q_norm_g, fox_k_norm_g, w_proj_conv, w_proj_fox, w_proj_sb, w_out, norm2_g, w_up, ffn_conv_w, ffn_conv_b, w_down, loss_target, m_norm1_g, m_w_in, m_fox_f_bias, m_gate_bias, m_conv_w, m_fox_q_norm_g, m_fox_k_norm_g, m_w_proj_conv, m_w_proj_fox, m_w_proj_sb, m_w_out, m_norm2_g, m_w_up, m_ffn_conv_w, m_ffn_conv_b, m_w_down, v_norm1_g, v_w_in, v_fox_f_bias, v_gate_bias, v_conv_w, v_fox_q_norm_g, v_fox_k_norm_g, v_w_proj_conv, v_w_proj_fox, v_w_proj_sb, v_w_out, v_norm2_g, v_w_up, v_ffn_conv_w, v_ffn_conv_b, v_w_down):
    given = dict(x=x, norm1_g=norm1_g, w_in=w_in, fox_f_bias=fox_f_bias, gate_bias=gate_bias, conv_w=conv_w, fox_q_norm_g=fox_q_norm_g, fox_k_norm_g=fox_k_norm_g, w_proj_conv=w_proj_conv, w_proj_fox=w_proj_fox, w_proj_sb=w_proj_sb, w_out=w_out, norm2_g=norm2_g, w_up=w_up, ffn_conv_w=ffn_conv_w, ffn_conv_b=ffn_conv_b, w_down=w_down, loss_target=loss_target, m_norm1_g=m_norm1_g, m_w_in=m_w_in, m_fox_f_bias=m_fox_f_bias, m_gate_bias=m_gate_bias, m_conv_w=m_conv_w, m_fox_q_norm_g=m_fox_q_norm_g, m_fox_k_norm_g=m_fox_k_norm_g, m_w_proj_conv=m_w_proj_conv, m_w_proj_fox=m_w_proj_fox, m_w_proj_sb=m_w_proj_sb, m_w_out=m_w_out, m_norm2_g=m_norm2_g, m_w_up=m_w_up, m_ffn_conv_w=m_ffn_conv_w, m_ffn_conv_b=m_ffn_conv_b, m_w_down=m_w_down, v_norm1_g=v_norm1_g, v_w_in=v_w_in, v_fox_f_bias=v_fox_f_bias, v_gate_bias=v_gate_bias, v_conv_w=v_conv_w, v_fox_q_norm_g=v_fox_q_norm_g, v_fox_k_norm_g=v_fox_k_norm_g, v_w_proj_conv=v_w_proj_conv, v_w_proj_fox=v_w_proj_fox, v_w_proj_sb=v_w_proj_sb, v_w_out=v_w_out, v_norm2_g=v_norm2_g, v_w_up=v_w_up, v_ffn_conv_w=v_ffn_conv_w, v_ffn_conv_b=v_ffn_conv_b, v_w_down=v_w_down)
    depth = given["norm1_g"].shape[0]
    chip = 2 * lax.axis_index("x") + lax.axis_index("y")

    sq, dx, grads = _local_step(given["x"][0], given["loss_target"][0], _Plan(given))
    loss = lax.psum(0.5 * sq[0, 0] / D_MODEL, ("x", "y", "c"))

    gsum = {k: jnp.stack([g[k] for g in grads]) for k in BIG}
    small_shapes = [(depth, *grads[0][k].shape) for k in SMALL]
    summed = _all_reduce_small(_pack_small([jnp.stack([g[k] for g in grads]) for k in SMALL], 8))
    for k, total in zip(SMALL, _unpack_small(summed, small_shapes)):
        if k in SMALL_SHARDED:
            total = lax.dynamic_index_in_dim(total.reshape(*total.shape[:-1], 4, total.shape[-1] // 4), chip, axis=2, keepdims=False)
        gsum[k] = total

    deltas, new_m, new_v = {}, {}, {}
    for k in WEIGHTS:
        deltas[k], new_m[k], new_v[k] = _adamw(given[k], gsum[k], given["m_" + k], given["v_" + k], "adamw_" + k)
    return (loss, dx[None], *[gsum[k] for k in WEIGHTS], *[deltas[k] for k in WEIGHTS],
            *[new_m[k] for k in WEIGHTS], *[new_v[k] for k in WEIGHTS])
```

```python
import collections

import numpy as np
import jax
import jax.numpy as jnp
from jax import lax
from jax.experimental import pallas as pl
from jax.experimental.pallas import tpu as pltpu

F32 = jnp.float32
BF16 = jnp.bfloat16

D_MODEL = 1024
DEPTH = 4
HEAD_DIM = 64
N_HEADS = 8
WIDTH = 512
D_FF = 2816
NORM_EPS = 1e-6
NEG_INF = -1e30
QK_SCALE = HEAD_DIM ** -0.5
LANES = 128
N_PAIR = N_HEADS // 2

GATE_OFF = 0
CONV_OFF = 3 * D_MODEL
FOX_OFF = CONV_OFF + 3 * WIDTH
SB_OFF = FOX_OFF + 3 * WIDTH
D_INA = SB_OFF + 3 * WIDTH

ADAM_LR = 0.001
ADAM_B1 = 0.9
ADAM_B2 = 0.999
ADAM_EPS = 1e-08
ADAM_WD = 0.01
ADAM_STEP = 10

VMEM_LIMIT = 48 * 1024 * 1024
ADAMW_BLOCK_BYTES = 1024 * 1024

PACK_COLS = 1024
MESH = pl.DeviceIdType.MESH


def _params(sem):
    return pltpu.CompilerParams(dimension_semantics=sem, vmem_limit_bytes=VMEM_LIMIT)


def _dot(a, b, dims):
    return lax.dot_general(a, b, (dims, ((), ())), preferred_element_type=F32)


_NN = ((1,), (0,))
_NT = ((1,), (1,))
_TN = ((0,), (0,))


def _pick(dim, pref):
    if dim <= pref:
        return dim
    best = None
    for mult in range(1, dim // LANES + 1):
        t = mult * LANES
        if t <= pref and dim % t == 0:
            best = t
    assert best is not None, (dim, pref)
    return best


def _mm(a, b, mode, out_dtype=F32, tm=1024, tn=1024, tk=2048, res=None, name="mm", comm=None):
    if mode == "nn":
        (m, k), (_, n) = a.shape, b.shape
    elif mode == "nt":
        (m, k), (n, _) = a.shape, b.shape
    else:
        (k, m), (_, n) = a.shape, b.shape
    tm, tn, tk = _pick(m, tm), _pick(n, tn), _pick(k, tk)
    nk = k // tk
    dims = {"nn": _NN, "nt": _NT, "tn": _TN}[mode]
    if mode == "tn":
        a_spec = pl.BlockSpec((tk, tm), lambda i, j, kk: (kk, i))
    else:
        a_spec = pl.BlockSpec((tm, tk), lambda i, j, kk: (i, kk))
    if mode == "nt":
        b_spec = pl.BlockSpec((tn, tk), lambda i, j, kk: (j, kk))
    else:
        b_spec = pl.BlockSpec((tk, tn), lambda i, j, kk: (kk, j))
    o_spec = pl.BlockSpec((tm, tn), lambda i, j, kk: (i, j))
    in_specs = [a_spec, b_spec] + ([o_spec] if res is not None else [])
    n_in, n_out = (len(comm.inputs), len(comm.out_shapes)) if comm else (0, 0)
    gm, gn = m // tm, n // tn

    def body(*refs):
        a_ref, b_ref, refs = refs[0], refs[1], refs[2:]
        r_ref, refs = (refs[0], refs[1:]) if res is not None else (None, refs)
        comm_in, o_ref, refs = refs[:n_in], refs[n_in], refs[n_in + 1:]
        comm_out, acc_ref, sems = refs[:n_out], refs[n_out], refs[n_out + 1:]
        kk = pl.program_id(2)
        if comm:
            at = lambda i, j, k: (pl.program_id(0) == i) & (pl.program_id(1) == j) & (kk == k)
            pl.when(at(0, 0, 0))(lambda: comm.start(comm_in, comm_out, sems))
        part = _dot(a_ref[...].astype(BF16), b_ref[...].astype(BF16), dims)

        def finish(total):
            if r_ref is not None:
                total = total + r_ref[...].astype(F32)
            o_ref[...] = total.astype(out_dtype)

        if nk == 1:
            finish(part)
        else:
            @pl.when(kk == 0)
            def _():
                acc_ref[...] = part

            @pl.when(kk > 0)
            def _():
                acc_ref[...] += part

            @pl.when(kk == nk - 1)
            def _():
                finish(acc_ref[...])

        if comm:
            pl.when(at(gm - 1, gn - 1, nk - 1))(lambda: comm.finish(comm_in, comm_out, sems))

    args = (a, b) + ((res,) if res is not None else ())
    out = jax.ShapeDtypeStruct((m, n), out_dtype)
    acc = pltpu.VMEM((tm, tn) if nk > 1 else (8, LANES), F32)
    if not comm:
        return pl.pallas_call(
            body, name=name, grid=(gm, gn, nk), in_specs=in_specs, out_specs=o_spec, out_shape=out, scratch_shapes=[acc],
            compiler_params=_params(("parallel", "parallel", "arbitrary")),
        )(*args)
    outs = pl.pallas_call(
        body, name=name, grid=(gm, gn, nk), in_specs=in_specs + [ANY] * n_in, out_specs=[o_spec] + [ANY] * n_out,
        out_shape=[out] + comm.out_shapes, scratch_shapes=[acc] + comm.sems,
        compiler_params=_params(("arbitrary", "arbitrary", "arbitrary")),
    )(*args, *comm.inputs)
    return outs[0], outs[1:]


def _norm_mm(x, g, w, out_dtype, tm=1024, tn=1536, name="norm_mm"):
    m, d = x.shape
    n = w.shape[1]
    tm, tn = _pick(m, tm), _pick(n, tn)

    def body(x_ref, g_ref, w_ref, o_ref, hn_ref):
        @pl.when(pl.program_id(1) == 0)
        def _():
            xf = x_ref[...]
            r = lax.rsqrt(jnp.mean(xf * xf, axis=-1, keepdims=True) + NORM_EPS)
            hn_ref[...] = (xf * r * g_ref[...]).astype(BF16)

        o_ref[...] = _dot(hn_ref[...], w_ref[...], _NN).astype(out_dtype)

    return pl.pallas_call(
        body, name=name, grid=(m // tm, n // tn),
        in_specs=[pl.BlockSpec((tm, d), lambda i, j: (i, 0)),
                  pl.BlockSpec((1, d), lambda i, j: (0, 0)),
                  pl.BlockSpec((d, tn), lambda i, j: (0, j))],
        out_specs=[pl.BlockSpec((tm, tn), lambda i, j: (i, j)),
                   pl.BlockSpec((tm, d), lambda i, j: (i, 0))],
        out_shape=[jax.ShapeDtypeStruct((m, n), out_dtype), jax.ShapeDtypeStruct((m, d), BF16)],
        compiler_params=_params(("parallel", "arbitrary")),
    )(x, g.reshape(1, d), w)


def _norm_bwd(dhn, x, g, dx_in, tm=512, name="norm_bwd"):
    m, d = x.shape
    tm = min(tm, m)

    def body(dhn_ref, x_ref, g_ref, dxi_ref, dx_ref, gg_ref):
        xf = x_ref[...]
        r = lax.rsqrt(jnp.mean(xf * xf, axis=-1, keepdims=True) + NORM_EPS)
        xhat = xf * r
        dh = dhn_ref[...].astype(F32)
        dxn = dh * g_ref[...]
        mean = jnp.mean(dxn * xhat, axis=-1, keepdims=True)
        dx_ref[...] = dxi_ref[...] + r * (dxn - xhat * mean)
        part = jnp.sum(dh * xhat, axis=0, keepdims=True)

        @pl.when(pl.program_id(0) == 0)
        def _():
            gg_ref[...] = part

        @pl.when(pl.program_id(0) > 0)
        def _():
            gg_ref[...] += part

    row = pl.BlockSpec((tm, d), lambda i: (i, 0))
    vec = pl.BlockSpec((1, d), lambda i: (0, 0))
    dx, gg = pl.pallas_call(
        body, name=name, grid=(m // tm,), in_specs=[row, row, vec, row], out_specs=[row, vec],
        out_shape=[jax.ShapeDtypeStruct((m, d), F32), jax.ShapeDtypeStruct((1, d), F32)],
        compiler_params=_params(("arbitrary",)),
    )(dhn, x, g.reshape(1, d), dx_in)
    return dx, gg.reshape(d)


def _down(u, k):
    s = u.shape[0]
    rows = lax.broadcasted_iota(jnp.int32, u.shape, 0)
    return jnp.where(rows < k, 0.0, pltpu.roll(u, k, axis=0))


def _up(u, k):
    s = u.shape[0]
    rows = lax.broadcasted_iota(jnp.int32, u.shape, 0)
    return jnp.where(rows >= s - k, 0.0, pltpu.roll(u, s - k, axis=0))


def _conv_mix_fwd(proj, conv_w, tc=128):
    s = proj.shape[0]
    nb = WIDTH // tc
    off = CONV_OFF // tc

    def body(b_ref, c_ref, h_ref, w_ref, o_ref):
        u = c_ref[...].astype(F32) * h_ref[...].astype(F32)
        w = w_ref[...]
        cv = w[0:1] * _down(u, 2) + w[1:2] * _down(u, 1) + w[2:3] * u
        o_ref[...] = (b_ref[...].astype(F32) * cv).astype(BF16)

    col = lambda k: pl.BlockSpec((s, tc), lambda j, k=k: (0, off + k * nb + j))
    return pl.pallas_call(
        body, name="conv_mix_fwd", grid=(nb,),
        in_specs=[col(0), col(1), col(2), pl.BlockSpec((3, tc), lambda j: (0, j))],
        out_specs=pl.BlockSpec((s, tc), lambda j: (0, j)),
        out_shape=jax.ShapeDtypeStruct((s, WIDTH), BF16),
        compiler_params=_params(("parallel",)),
    )(proj, proj, proj, conv_w)


def _conv_mix_bwd(do, proj, conv_w, tc=128):
    s = proj.shape[0]
    nb = WIDTH // tc
    off = CONV_OFF // tc

    def body(do_ref, b_ref, c_ref, h_ref, w_ref, db_ref, dc_ref, dh_ref, gw_ref):
        b = b_ref[...].astype(F32)
        c = c_ref[...].astype(F32)
        h = h_ref[...].astype(F32)
        g = do_ref[...].astype(F32)
        w = w_ref[...]
        u = c * h
        u1, u2 = _down(u, 1), _down(u, 2)
        cv = w[0:1] * u2 + w[1:2] * u1 + w[2:3] * u
        db_ref[...] = (g * cv).astype(BF16)
        dcv = g * b
        gw_ref[0:1, :] = jnp.sum(dcv * u2, axis=0, keepdims=True)
        gw_ref[1:2, :] = jnp.sum(dcv * u1, axis=0, keepdims=True)
        gw_ref[2:3, :] = jnp.sum(dcv * u, axis=0, keepdims=True)
        du = w[2:3] * dcv + w[1:2] * _up(dcv, 1) + w[0:1] * _up(dcv, 2)
        dc_ref[...] = (du * h).astype(BF16)
        dh_ref[...] = (du * c).astype(BF16)

    col = lambda k: pl.BlockSpec((s, tc), lambda j, k=k: (0, off + k * nb + j))
    own = pl.BlockSpec((s, tc), lambda j: (0, j))
    wsp = pl.BlockSpec((3, tc), lambda j: (0, j))
    act = jax.ShapeDtypeStruct((s, WIDTH), BF16)
    return pl.pallas_call(
        body, name="conv_mix_bwd", grid=(nb,),
        in_specs=[own, col(0), col(1), col(2), wsp], out_specs=[own, own, own, wsp],
        out_shape=[act, act, act, jax.ShapeDtypeStruct((3, WIDTH), F32)],
        compiler_params=_params(("parallel",)),
    )(do, proj, proj, proj, conv_w)


def _glu_fwd(up, w, b, tc=256):
    s = up.shape[0]
    nb = D_FF // tc

    def body(g_ref, v_ref, w_ref, b_ref, o_ref):
        ug = g_ref[...].astype(F32)
        wv = w_ref[...]
        cg = wv[0:1] * _down(ug, 2) + wv[1:2] * _down(ug, 1) + wv[2:3] * ug + b_ref[...]
        act = cg * jax.nn.sigmoid(cg)
        o_ref[...] = (act * v_ref[...].astype(F32)).astype(BF16)

    return pl.pallas_call(
        body, name="glu_fwd", grid=(nb,),
        in_specs=[pl.BlockSpec((s, tc), lambda j: (0, j)), pl.BlockSpec((s, tc), lambda j: (0, nb + j)),
                  pl.BlockSpec((3, tc), lambda j: (0, j)), pl.BlockSpec((1, tc), lambda j: (0, j))],
        out_specs=pl.BlockSpec((s, tc), lambda j: (0, j)),
        out_shape=jax.ShapeDtypeStruct((s, D_FF), BF16),
        compiler_params=_params(("parallel",)),
    )(up, up, w, b.reshape(1, D_FF))


def _glu_bwd(dh, up, w, b, tc=256):
    s = up.shape[0]
    nb = D_FF // tc

    def body(dh_ref, g_ref, v_ref, w_ref, b_ref, dg_ref, dv_ref, gw_ref, gb_ref):
        ug = g_ref[...].astype(F32)
        uv = v_ref[...].astype(F32)
        d = dh_ref[...].astype(F32)
        wv = w_ref[...]
        u1, u2 = _down(ug, 1), _down(ug, 2)
        cg = wv[0:1] * u2 + wv[1:2] * u1 + wv[2:3] * ug + b_ref[...]
        sg = jax.nn.sigmoid(cg)
        dv_ref[...] = (d * (cg * sg)).astype(BF16)
        dcg = d * uv * (sg * (1.0 + cg * (1.0 - sg)))
        gb_ref[...] = jnp.sum(dcg, axis=0, keepdims=True)
        gw_ref[0:1, :] = jnp.sum(dcg * u2, axis=0, keepdims=True)
        gw_ref[1:2, :] = jnp.sum(dcg * u1, axis=0, keepdims=True)
        gw_ref[2:3, :] = jnp.sum(dcg * ug, axis=0, keepdims=True)
        dg_ref[...] = (wv[2:3] * dcg + wv[1:2] * _up(dcg, 1) + wv[0:1] * _up(dcg, 2)).astype(BF16)

    own = pl.BlockSpec((s, tc), lambda j: (0, j))
    wsp = pl.BlockSpec((3, tc), lambda j: (0, j))
    bsp = pl.BlockSpec((1, tc), lambda j: (0, j))
    act = jax.ShapeDtypeStruct((s, D_FF), BF16)
    dg, dv, gw, gb = pl.pallas_call(
        body, name="glu_bwd", grid=(nb,),
        in_specs=[own, own, pl.BlockSpec((s, tc), lambda j: (0, nb + j)), wsp, bsp],
        out_specs=[own, own, wsp, bsp],
        out_shape=[act, act, jax.ShapeDtypeStruct((3, D_FF), F32), jax.ShapeDtypeStruct((1, D_FF), F32)],
        compiler_params=_params(("parallel",)),
    )(dh, up, up, w, b.reshape(1, D_FF))
    return dg, dv, gw, gb.reshape(D_FF)


def _merge_fwd(x, oc, of, osb, wpc, wpf, wps, proj, gb, wout, tm=256):
    s, d = x.shape
    tm = min(tm, s)

    def body(x_ref, oc_ref, of_ref, os_ref, wpc_ref, wpf_ref, wps_ref, g0_ref, g1_ref, g2_ref, gb_ref, wo_ref,
             xo_ref, mg_ref, yc_ref, yf_ref, ys_ref):
        merged = jnp.zeros((tm, d), F32)
        for k, (o_ref, w_ref, g_ref, y_ref) in enumerate(
                ((oc_ref, wpc_ref, g0_ref, yc_ref), (of_ref, wpf_ref, g1_ref, yf_ref), (os_ref, wps_ref, g2_ref, ys_ref))):
            y = _dot(o_ref[...].astype(BF16), w_ref[...], _NN)
            y_ref[...] = y.astype(BF16)
            gate = jax.nn.sigmoid(g_ref[...].astype(F32) + gb_ref[:, k * d:(k + 1) * d])
            merged = merged + gate * y
        mb = merged.astype(BF16)
        mg_ref[...] = mb
        xo_ref[...] = x_ref[...] + _dot(mb, wo_ref[...], _NN)

    rowd = pl.BlockSpec((tm, d), lambda i: (i, 0))
    roww = pl.BlockSpec((tm, WIDTH), lambda i: (i, 0))
    wp = pl.BlockSpec((WIDTH, d), lambda i: (0, 0))
    gcol = lambda k: pl.BlockSpec((tm, d), lambda i, k=k: (i, GATE_OFF // d + k))
    actd = jax.ShapeDtypeStruct((s, d), BF16)
    return pl.pallas_call(
        body, name="merge_fwd", grid=(s // tm,),
        in_specs=[rowd, roww, roww, roww, wp, wp, wp, gcol(0), gcol(1), gcol(2),
                  pl.BlockSpec((1, 3 * d), lambda i: (0, 0)), pl.BlockSpec((d, d), lambda i: (0, 0))],
        out_specs=[rowd, rowd, rowd, rowd, rowd],
        out_shape=[jax.ShapeDtypeStruct((s, d), F32), actd, actd, actd, actd],
        compiler_params=_params(("parallel",)),
    )(x, oc, of, osb, wpc, wpf, wps, proj, proj, proj, gb.reshape(1, 3 * d), wout)


def _gate_bwd(dm, yc, yf, ys, proj, gb, tm=512):
    s, d = dm.shape
    tm = min(tm, s)

    def body(dm_ref, yc_ref, yf_ref, ys_ref, g0_ref, g1_ref, g2_ref, gb_ref, dyc_ref, dyf_ref, dys_ref, dgl_ref, ggb_ref):
        g = dm_ref[...].astype(F32)
        parts = []
        for k, (y_ref, g_ref, dy_ref) in enumerate(((yc_ref, g0_ref, dyc_ref), (yf_ref, g1_ref, dyf_ref), (ys_ref, g2_ref, dys_ref))):
            gate = jax.nn.sigmoid(g_ref[...].astype(F32) + gb_ref[:, k * d:(k + 1) * d])
            dy_ref[...] = (g * gate).astype(BF16)
            dgl = g * y_ref[...].astype(F32) * gate * (1.0 - gate)
            dgl_ref[:, k * d:(k + 1) * d] = dgl.astype(BF16)
            parts.append(jnp.sum(dgl, axis=0, keepdims=True))
        part = jnp.concatenate(parts, axis=1)

        @pl.when(pl.program_id(0) == 0)
        def _():
            ggb_ref[...] = part

        @pl.when(pl.program_id(0) > 0)
        def _():
            ggb_ref[...] += part

    rowd = pl.BlockSpec((tm, d), lambda i: (i, 0))
    gcol = lambda k: pl.BlockSpec((tm, d), lambda i, k=k: (i, GATE_OFF // d + k))
    vec = pl.BlockSpec((1, 3 * d), lambda i: (0, 0))
    actd = jax.ShapeDtypeStruct((s, d), BF16)
    dyc, dyf, dys, dgl, ggb = pl.pallas_call(
        body, name="gate_bwd", grid=(s // tm,),
        in_specs=[rowd, rowd, rowd, rowd, gcol(0), gcol(1), gcol(2), vec],
        out_specs=[rowd, rowd, rowd, pl.BlockSpec((tm, 3 * d), lambda i: (i, 0)), vec],
        out_shape=[actd, actd, actd, jax.ShapeDtypeStruct((s, 3 * d), BF16), jax.ShapeDtypeStruct((1, 3 * d), F32)],
        compiler_params=_params(("arbitrary",)),
    )(dm, yc, yf, ys, proj, proj, proj, gb.reshape(1, 3 * d))
    return dyc, dyf, dys, dgl, ggb.reshape(3 * d)


def _tri_tables(nq, r, order):
    last = lambda qi: (qi + 1) * r - 1
    if order == "k_outer":
        pairs = [(qi, kj) for kj in range(nq * r) for qi in range(kj // r, nq)]
    elif order == "k_desc":
        pairs = [(qi, kj) for qi in range(nq) for kj in range(last(qi), -1, -1)]
    else:
        pairs = [(qi, kj) for qi in range(nq) for kj in range(last(qi) + 1)]
    qs, ks = zip(*pairs)
    return jnp.asarray(np.array(qs, np.int32)), jnp.asarray(np.array(ks, np.int32)), len(pairs)


def _lo_mask(shape):
    return lax.broadcasted_iota(jnp.int32, shape, len(shape) - 1) < HEAD_DIM


def _head(x, h):
    lo = _lo_mask(x.shape)
    return jnp.where(lo if h == 0 else jnp.logical_not(lo), x, jnp.zeros_like(x))


def _pair_cols(x, h):
    return x[:, h * HEAD_DIM:h * HEAD_DIM + 1]


def _rep(a0, a1, shape):
    return jnp.where(_lo_mask(shape), a0, a1)


def _positions(qi, kj, tq, tk):
    row = qi * tq + lax.broadcasted_iota(jnp.int32, (tq, tk), 0)
    col = kj * tk + lax.broadcasted_iota(jnp.int32, (tq, tk), 1)
    return row, col


def _head_norm(x, g):
    lo = _lo_mask(x.shape)
    sq = x * x
    s0 = jnp.sum(jnp.where(lo, sq, 0.0), axis=-1, keepdims=True)
    s1 = jnp.sum(jnp.where(lo, 0.0, sq), axis=-1, keepdims=True)
    r = jnp.where(lo, lax.rsqrt(s0 / HEAD_DIM + NORM_EPS), lax.rsqrt(s1 / HEAD_DIM + NORM_EPS))
    return x * r, r


def _fox_prep(proj, gq, gk, tm=2048):
    s = proj.shape[0]
    tm = min(tm, s)
    off = FOX_OFF // LANES

    def body(q_ref, k_ref, gq_ref, gk_ref, qn_ref, kn_ref):
        qh, _ = _head_norm(q_ref[...].astype(F32), None)
        kh, _ = _head_norm(k_ref[...].astype(F32), None)
        qn_ref[...] = (qh * gq_ref[...] * QK_SCALE).astype(BF16)
        kn_ref[...] = (kh * gk_ref[...]).astype(BF16)

    vec = pl.BlockSpec((1, LANES), lambda p, i: (0, 0))
    own = pl.BlockSpec((tm, LANES), lambda p, i: (i, p))
    act = jax.ShapeDtypeStruct((s, WIDTH), BF16)
    return pl.pallas_call(
        body, name="fox_prep", grid=(N_PAIR, s // tm),
        in_specs=[pl.BlockSpec((tm, LANES), lambda p, i: (i, off + p)),
                  pl.BlockSpec((tm, LANES), lambda p, i: (i, off + N_PAIR + p)), vec, vec],
        out_specs=[own, own], out_shape=[act, act],
        compiler_params=_params(("parallel", "parallel")),
    )(proj, proj, jnp.tile(gq, 2).reshape(1, LANES), jnp.tile(gk, 2).reshape(1, LANES))


def _fox_post(dqs, dkn, proj, gq, gk, tm=2048):
    s = proj.shape[0]
    tm = min(tm, s)
    off = FOX_OFF // LANES

    def one(d_ref, x_ref, g_ref, scale, dx_ref, gg_ref, first):
        xhat, r = _head_norm(x_ref[...].astype(F32), None)
        dy = d_ref[...] * scale
        part = jnp.sum(dy * xhat, axis=0, keepdims=True)

        @pl.when(first)
        def _():
            gg_ref[...] = part

        @pl.when(jnp.logical_not(first))
        def _():
            gg_ref[...] += part

        dxh = dy * g_ref[...]
        lo = _lo_mask(dxh.shape)
        pr = dxh * xhat
        m0 = jnp.sum(jnp.where(lo, pr, 0.0), axis=-1, keepdims=True)
        m1 = jnp.sum(jnp.where(lo, 0.0, pr), axis=-1, keepdims=True)
        mean = jnp.where(lo, m0, m1) / HEAD_DIM
        dx_ref[...] = (r * (dxh - xhat * mean)).astype(BF16)

    def body(dq_ref, dk_ref, q_ref, k_ref, gq_ref, gk_ref, dxq_ref, dxk_ref, ggq_ref, ggk_ref):
        first = pl.program_id(1) == 0
        one(dq_ref, q_ref, gq_ref, QK_SCALE, dxq_ref, ggq_ref, first)
        one(dk_ref, k_ref, gk_ref, 1.0, dxk_ref, ggk_ref, first)

    vec = pl.BlockSpec((1, LANES), lambda p, i: (0, 0))
    own = pl.BlockSpec((tm, LANES), lambda p, i: (i, p))
    ggs = pl.BlockSpec((None, 1, LANES), lambda p, i: (p, 0, 0))
    act = jax.ShapeDtypeStruct((s, WIDTH), BF16)
    ggo = jax.ShapeDtypeStruct((N_PAIR, 1, LANES), F32)
    dxq, dxk, ggq, ggk = pl.pallas_call(
        body, name="fox_post", grid=(N_PAIR, s // tm),
        in_specs=[own, own, pl.BlockSpec((tm, LANES), lambda p, i: (i, off + p)),
                  pl.BlockSpec((tm, LANES), lambda p, i: (i, off + N_PAIR + p)), vec, vec],
        out_specs=[own, own, ggs, ggs], out_shape=[act, act, ggo, ggo],
        compiler_params=_params(("parallel", "arbitrary")),
    )(dqs, dkn, proj, proj, jnp.tile(gq, 2).reshape(1, LANES), jnp.tile(gk, 2).reshape(1, LANES))
    fold = lambda a: a.reshape(N_HEADS, HEAD_DIM).sum(axis=0)
    return dxq, dxk, fold(ggq), fold(ggk)


def _split3(x):
    a = x.astype(BF16)
    r = x - a.astype(F32)
    b = r.astype(BF16)
    c = (r - b.astype(F32)).astype(BF16)
    return a, b, c


def _split2(x):
    a = x.astype(BF16)
    b = (x - a.astype(F32)).astype(BF16)
    return a, b


def _log_sigmoid(x):
    return jnp.minimum(x, 0.0) - jnp.log(1.0 + jnp.exp(-jnp.abs(x)))


def _fox_gates(ft, bias):
    h, s = ft.shape
    nb = s // LANES

    def body(f_ref, b_ref, c_ref):
        lf = _log_sigmoid(f_ref[...] + b_ref[...])
        i = lax.broadcasted_iota(jnp.int32, (s, LANES), 0)
        j = pl.program_id(0) * LANES + lax.broadcasted_iota(jnp.int32, (s, LANES), 1)
        tri = jnp.where(i <= j, 1.0, 0.0).astype(BF16)
        c_ref[...] = sum(_dot(p, tri, _NN) for p in _split3(lf))

    return pl.pallas_call(
        body, name="fox_gates", grid=(nb,),
        in_specs=[pl.BlockSpec((h, s), lambda j: (0, 0)), pl.BlockSpec((h, 1), lambda j: (0, 0))],
        out_specs=pl.BlockSpec((h, LANES), lambda j: (0, j)),
        out_shape=jax.ShapeDtypeStruct((h, s), F32),
        compiler_params=_params(("parallel",)),
    )(ft, bias.reshape(h, 1))


def _fox_gates_bwd(dc, ft, bias):
    h, s = ft.shape
    nb = s // LANES

    def body(dc_ref, f_ref, fb_ref, b_ref, df_ref, gb_ref):
        i = lax.broadcasted_iota(jnp.int32, (s, LANES), 0)
        j = pl.program_id(0) * LANES + lax.broadcasted_iota(jnp.int32, (s, LANES), 1)
        tri = jnp.where(i >= j, 1.0, 0.0).astype(BF16)
        dlf = sum(_dot(p, tri, _NN) for p in _split3(dc_ref[...]))
        df = dlf * jax.nn.sigmoid(-(fb_ref[...] + b_ref[...]))
        df_ref[...] = df
        part = jnp.sum(df, axis=-1, keepdims=True)

        @pl.when(pl.program_id(0) == 0)
        def _():
            gb_ref[...] = part

        @pl.when(pl.program_id(0) > 0)
        def _():
            gb_ref[...] += part

    full = pl.BlockSpec((h, s), lambda j: (0, 0))
    blk = pl.BlockSpec((h, LANES), lambda j: (0, j))
    one = pl.BlockSpec((h, 1), lambda j: (0, 0))
    df, gb = pl.pallas_call(
        body, name="fox_gates_bwd", grid=(nb,), in_specs=[full, full, blk, one], out_specs=[blk, one],
        out_shape=[jax.ShapeDtypeStruct((h, s), F32), jax.ShapeDtypeStruct((h, 1), F32)],
        compiler_params=_params(("arbitrary",)),
    )(dc, ft, ft, bias.reshape(h, 1))
    return df, gb.reshape(h)


def _delta_rep(do, o, tm=2048):
    s = do.shape[0]
    tm = min(tm, s)

    def body(do_ref, o_ref, d_ref):
        pr = do_ref[...].astype(F32) * o_ref[...].astype(F32)
        lo = _lo_mask(pr.shape)
        d0 = jnp.sum(jnp.where(lo, pr, 0.0), axis=-1, keepdims=True)
        d1 = jnp.sum(jnp.where(lo, 0.0, pr), axis=-1, keepdims=True)
        d_ref[...] = jnp.where(lo, d0, d1)

    own = pl.BlockSpec((tm, LANES), lambda p, i: (i, p))
    return pl.pallas_call(
        body, name="delta_rep", grid=(N_PAIR, s // tm), in_specs=[own, own],
        out_specs=pl.BlockSpec((None, tm, LANES), lambda p, i: (p, i, 0)),
        out_shape=jax.ShapeDtypeStruct((N_PAIR, s, LANES), F32),
        compiler_params=_params(("parallel", "parallel")),
    )(do, o)


def _tile(s, t):
    t = min(t, s)
    assert s % t == 0
    return t


def _fox_fwd(qn, kn, proj, ccol, crow, t=512, comm=None):
    s = qn.shape[0]
    t = _tile(s, t)
    n = s // t
    qtab, ktab, ntri = _tri_tables(n, 1, "k_asc")
    voff = FOX_OFF // LANES + 2 * N_PAIR
    n_in, n_out = (len(comm.inputs), len(comm.out_shapes)) if comm else (0, 0)

    def body(qt_ref, kt_ref, q_ref, k_ref, v_ref, cc_ref, cr_ref, *rest):
        comm_in, rest = rest[:n_in], rest[n_in:]
        (o_ref, lse_ref), rest = rest[:2], rest[2:]
        comm_out, rest = rest[:n_out], rest[n_out:]
        (m_ref, l_ref, acc_ref), sems = rest[:3], rest[3:]
        i = pl.program_id(1)
        qi, kj = qt_ref[i], kt_ref[i]
        if comm:
            pl.when((pl.program_id(0) == 0) & (i == 0))(lambda: comm.start(comm_in, comm_out, sems))

        @pl.when(kj == 0)
        def _():
            m_ref[...] = jnp.full(m_ref.shape, NEG_INF, F32)
            l_ref[...] = jnp.zeros(l_ref.shape, F32)
            acc_ref[...] = jnp.zeros(acc_ref.shape, F32)

        q, k, v = q_ref[...], k_ref[...], v_ref[...]
        row, col = _positions(qi, kj, t, t)
        causal = col <= row
        m_old = m_ref[...]
        mn, rs, pv = [], [], []
        for h in range(2):
            sc = _dot(_head(q, h), k, _NT) + _pair_cols(cc_ref[...], h) - cr_ref[h:h + 1, :]
            sc = jnp.where(causal, sc, NEG_INF)
            m_new = jnp.maximum(_pair_cols(m_old, h), jnp.max(sc, axis=-1, keepdims=True))
            p = jnp.exp(sc - m_new)
            mn.append(m_new)
            rs.append(jnp.sum(p, axis=-1, keepdims=True))
            pv.append(_dot(p.astype(BF16), _head(v, h), _NN))
        m_rep = _rep(mn[0], mn[1], m_old.shape)
        alpha = jnp.exp(m_old - m_rep)
        l_ref[...] = alpha * l_ref[...] + _rep(rs[0], rs[1], m_old.shape)
        acc_ref[...] = alpha * acc_ref[...] + pv[0] + pv[1]
        m_ref[...] = m_rep

        @pl.when(kj == qi)
        def _():
            o_ref[...] = acc_ref[...] / l_ref[...]
            lse_ref[...] = m_ref[...] + jnp.log(l_ref[...])

        if comm:
            pl.when((pl.program_id(0) == N_PAIR - 1) & (i == ntri - 1))(lambda: comm.finish(comm_in, comm_out, sems))

    grid_spec = pltpu.PrefetchScalarGridSpec(
        num_scalar_prefetch=2, grid=(N_PAIR, ntri),
        in_specs=[pl.BlockSpec((t, LANES), lambda p, i, qt, kt: (qt[i], p)),
                  pl.BlockSpec((t, LANES), lambda p, i, qt, kt: (kt[i], p)),
                  pl.BlockSpec((t, LANES), lambda p, i, qt, kt: (kt[i], voff + p)),
                  pl.BlockSpec((None, t, LANES), lambda p, i, qt, kt: (p, qt[i], 0)),
                  pl.BlockSpec((None, 2, t), lambda p, i, qt, kt: (p, 0, kt[i]))] + [ANY] * n_in,
        out_specs=[pl.BlockSpec((t, LANES), lambda p, i, qt, kt: (qt[i], p)),
                   pl.BlockSpec((None, t, LANES), lambda p, i, qt, kt: (p, qt[i], 0))] + [ANY] * n_out,
        scratch_shapes=[pltpu.VMEM((t, LANES), F32)] * 3 + (comm.sems if comm else []))
    outs = pl.pallas_call(
        body, name="fox_fwd", grid_spec=grid_spec,
        out_shape=[jax.ShapeDtypeStruct((s, WIDTH), F32), jax.ShapeDtypeStruct((N_PAIR, s, LANES), F32)]
        + (comm.out_shapes if comm else []),
        compiler_params=_params(("arbitrary", "arbitrary") if comm else ("parallel", "arbitrary")),
    )(qtab, ktab, qn, kn, proj, ccol, crow, *(comm.inputs if comm else []))
    return outs[:2], outs[2:]


def _fox_bwd(qn, kn, proj, do, lse, delta, ccol, crow, t=512, comm=None):
    s = qn.shape[0]
    t = _tile(s, t)
    n = s // t
    qtab, ktab, ntri = _tri_tables(n, 1, "k_outer")
    voff = FOX_OFF // LANES + 2 * N_PAIR
    n_in, n_out = (len(comm.inputs), len(comm.out_shapes)) if comm else (0, 0)

    def body(qt_ref, kt_ref, q_ref, k_ref, v_ref, do_ref, lse_ref, dl_ref, cc_ref, cr_ref, *rest):
        comm_in, rest = rest[:n_in], rest[n_in:]
        (dq_ref, dk_ref, dv_ref, dc_ref, dcq_ref), rest = rest[:5], rest[5:]
        comm_out, rest = rest[:n_out], rest[n_out:]
        (dka_ref, dva_ref, dca_ref, dcqa_ref), sems = rest[:4], rest[4:]
        i = pl.program_id(1)
        qi, kj = qt_ref[i], kt_ref[i]
        if comm:
            pl.when((pl.program_id(0) == 0) & (i == 0))(lambda: comm.start(comm_in, comm_out, sems))

        @pl.when(i == 0)
        def _():
            dq_ref[...] = jnp.zeros(dq_ref.shape, F32)
            dcqa_ref[...] = jnp.zeros(dcqa_ref.shape, F32)

        @pl.when(qi == kj)
        def _():
            dka_ref[...] = jnp.zeros(dka_ref.shape, F32)
            dva_ref[...] = jnp.zeros(dva_ref.shape, F32)
            dca_ref[...] = jnp.zeros(dca_ref.shape, F32)

        q, k, v, g = q_ref[...], k_ref[...], v_ref[...], do_ref[...]
        row, col = _positions(qi, kj, t, t)
        causal = col <= row
        dq = jnp.zeros((t, LANES), F32)
        dk = jnp.zeros((t, LANES), F32)
        dv = jnp.zeros((t, LANES), F32)
        rowsum = []
        for h in range(2):
            qh, gh = _head(q, h), _head(g, h)
            sc = _dot(qh, k, _NT) + _pair_cols(cc_ref[...], h) - cr_ref[h:h + 1, :]
            p = jnp.where(causal, jnp.exp(sc - _pair_cols(lse_ref[...], h)), 0.0)
            dp = _dot(gh, v, _NT)
            ds = p * (dp - _pair_cols(dl_ref[...], h))
            dsb = ds.astype(BF16)
            dv = dv + _dot(p.astype(BF16), gh, _TN)
            dk = dk + _dot(dsb, qh, _TN)
            dq = dq + _dot(dsb, _head(k, h), _NN)
            dca_ref[h:h + 1, :] -= jnp.sum(ds, axis=0, keepdims=True)
            rowsum.append(jnp.sum(ds, axis=-1, keepdims=True))
        dka_ref[...] += dk
        dva_ref[...] += dv
        rows = pl.ds(pl.multiple_of(qi * t, t), t)
        dq_ref[rows, :] += dq
        dcqa_ref[rows, :] += _rep(rowsum[0], rowsum[1], (t, LANES))

        @pl.when(qi == n - 1)
        def _():
            dk_ref[...] = dka_ref[...]
            dv_ref[...] = dva_ref[...].astype(BF16)
            dc_ref[...] = dca_ref[...]

        @pl.when(i == ntri - 1)
        def _():
            across = dcqa_ref[...].T
            dcq_ref[0:1, :] = across[0:1, :]
            dcq_ref[1:2, :] = across[HEAD_DIM:HEAD_DIM + 1, :]

        if comm:
            pl.when((pl.program_id(0) == N_PAIR - 1) & (i == ntri - 1))(lambda: comm.finish(comm_in, comm_out, sems))

    qblk = lambda p, i, qt, kt: (qt[i], p)
    kblk = lambda p, i, qt, kt: (kt[i], p)
    qrep = pl.BlockSpec((None, t, LANES), lambda p, i, qt, kt: (p, qt[i], 0))
    crs = pl.BlockSpec((None, 2, t), lambda p, i, qt, kt: (p, 0, kt[i]))
    grid_spec = pltpu.PrefetchScalarGridSpec(
        num_scalar_prefetch=2, grid=(N_PAIR, ntri),
        in_specs=[pl.BlockSpec((t, LANES), qblk), pl.BlockSpec((t, LANES), kblk),
                  pl.BlockSpec((t, LANES), lambda p, i, qt, kt: (kt[i], voff + p)),
                  pl.BlockSpec((t, LANES), qblk), qrep, qrep, qrep, crs] + [ANY] * n_in,
        out_specs=[pl.BlockSpec((s, LANES), lambda p, i, qt, kt: (0, p)),
                   pl.BlockSpec((t, LANES), kblk), pl.BlockSpec((t, LANES), kblk), crs,
                   pl.BlockSpec((None, 2, s), lambda p, i, qt, kt: (p, 0, 0))] + [ANY] * n_out,
        scratch_shapes=[pltpu.VMEM((t, LANES), F32), pltpu.VMEM((t, LANES), F32), pltpu.VMEM((2, t), F32),
                        pltpu.VMEM((s, LANES), F32)] + (comm.sems if comm else []))
    outs = pl.pallas_call(
        body, name="fox_bwd", grid_spec=grid_spec,
        out_shape=[jax.ShapeDtypeStruct((s, WIDTH), F32), jax.ShapeDtypeStruct((s, WIDTH), F32),
                   jax.ShapeDtypeStruct((s, WIDTH), BF16), jax.ShapeDtypeStruct((N_PAIR, 2, s), F32),
                   jax.ShapeDtypeStruct((N_PAIR, 2, s), F32)] + (comm.out_shapes if comm else []),
        compiler_params=_params(("arbitrary", "arbitrary") if comm else ("parallel", "arbitrary")),
    )(qtab, ktab, qn, kn, proj, do, lse, delta, ccol, crow, *(comm.inputs if comm else []))
    return outs[:5], outs[5:]


def _sb_tile(qh, k, strict, tk, r_col):
    z = _dot(qh, k, _NT)
    lg = jnp.where(strict, -(jnp.maximum(z, 0.0) + jnp.log(1.0 + jnp.exp(-jnp.abs(z)))), 0.0)
    jj = lax.broadcasted_iota(jnp.int32, (tk, tk), 0)
    ss = lax.broadcasted_iota(jnp.int32, (tk, tk), 1)
    above = jnp.where(jj > ss, 1.0, 0.0).astype(BF16)
    suffix = sum(_dot(p, above, _NN) for p in _split2(lg)) + r_col
    a = jnp.where(strict, jnp.exp(lg + z + suffix), 0.0)
    return z, lg, a


def _sb_fwd(proj, tq=512, tk=256, comm=None):
    s = proj.shape[0]
    tq = _tile(s, tq)
    tk = _tile(tq, tk)
    nq, r = s // tq, tq // tk
    qtab, ktab, ntri = _tri_tables(nq, r, "k_desc")
    off = SB_OFF // LANES
    n_in, n_out = (len(comm.inputs), len(comm.out_shapes)) if comm else (0, 0)

    def body(qt_ref, kt_ref, q_ref, k_ref, v_ref, *rest):
        comm_in, o_ref, rest = rest[:n_in], rest[n_in], rest[n_in + 1:]
        comm_out, rest = rest[:n_out], rest[n_out:]
        (acc_ref, r_ref), sems = rest[:2], rest[2:]
        i = pl.program_id(1)
        qi, kj = qt_ref[i], kt_ref[i]
        if comm:
            pl.when((pl.program_id(0) == 0) & (i == 0))(lambda: comm.start(comm_in, comm_out, sems))

        @pl.when(kj == (qi + 1) * r - 1)
        def _():
            acc_ref[...] = jnp.zeros(acc_ref.shape, F32)
            r_ref[...] = jnp.zeros(r_ref.shape, F32)

        q = q_ref[...] * QK_SCALE
        k, v = k_ref[...], v_ref[...]
        row, col = _positions(qi, kj, tq, tk)
        strict = col < row
        acc = acc_ref[...]
        for h in range(2):
            _, lg, a = _sb_tile(_head(q, h), k, strict, tk, r_ref[h])
            acc = acc + _dot(a.astype(BF16), _head(v, h), _NN)
            r_ref[h] += jnp.sum(lg, axis=-1, keepdims=True)
        acc_ref[...] = acc

        @pl.when(kj == 0)
        def _():
            o_ref[...] = acc_ref[...]

        if comm:
            pl.when((pl.program_id(0) == N_PAIR - 1) & (i == ntri - 1))(lambda: comm.finish(comm_in, comm_out, sems))

    grid_spec = pltpu.PrefetchScalarGridSpec(
        num_scalar_prefetch=2, grid=(N_PAIR, ntri),
        in_specs=[pl.BlockSpec((tq, LANES), lambda p, i, qt, kt: (qt[i], off + p)),
                  pl.BlockSpec((tk, LANES), lambda p, i, qt, kt: (kt[i], off + N_PAIR + p)),
                  pl.BlockSpec((tk, LANES), lambda p, i, qt, kt: (kt[i], off + 2 * N_PAIR + p))] + [ANY] * n_in,
        out_specs=[pl.BlockSpec((tq, LANES), lambda p, i, qt, kt: (qt[i], p))] + [ANY] * n_out,
        scratch_shapes=[pltpu.VMEM((tq, LANES), F32), pltpu.VMEM((2, tq, 1), F32)] + (comm.sems if comm else []))
    outs = pl.pallas_call(
        body, name="sb_fwd", grid_spec=grid_spec,
        out_shape=[jax.ShapeDtypeStruct((s, WIDTH), F32)] + (comm.out_shapes if comm else []),
        compiler_params=_params(("arbitrary", "arbitrary") if comm else ("parallel", "arbitrary")),
    )(qtab, ktab, proj, proj, proj, *(comm.inputs if comm else []))
    return outs[0], outs[1:]


def _sb_bwd(proj, do, delta, tq=512, tk=256, comm=None):
    s = proj.shape[0]
    tq = _tile(s, tq)
    tk = _tile(tq, tk)
    nq, r = s // tq, tq // tk
    qtab, ktab, ntri = _tri_tables(nq, r, "k_desc")
    off = SB_OFF // LANES
    n_in, n_out = (len(comm.inputs), len(comm.out_shapes)) if comm else (0, 0)

    def body(qt_ref, kt_ref, q_ref, k_ref, v_ref, do_ref, dl_ref, *rest):
        comm_in, rest = rest[:n_in], rest[n_in:]
        (dq_ref, dk_ref, dv_ref), rest = rest[:3], rest[3:]
        comm_out, rest = rest[:n_out], rest[n_out:]
        (dqa_ref, dka_ref, dva_ref, r_ref, rd_ref), sems = rest[:5], rest[5:]
        i = pl.program_id(1)
        qi, kj = qt_ref[i], kt_ref[i]
        if comm:
            pl.when((pl.program_id(0) == 0) & (i == 0))(lambda: comm.start(comm_in, comm_out, sems))

        @pl.when(i == 0)
        def _():
            dka_ref[...] = jnp.zeros(dka_ref.shape, F32)
            dva_ref[...] = jnp.zeros(dva_ref.shape, F32)

        @pl.when(kj == (qi + 1) * r - 1)
        def _():
            dqa_ref[...] = jnp.zeros(dqa_ref.shape, F32)
            r_ref[...] = jnp.zeros(r_ref.shape, F32)
            rd_ref[...] = jnp.zeros(rd_ref.shape, F32)

        q = q_ref[...] * QK_SCALE
        k, v, g = k_ref[...], v_ref[...], do_ref[...]
        row, col = _positions(qi, kj, tq, tk)
        strict = col < row
        ss = lax.broadcasted_iota(jnp.int32, (tk, tk), 0)
        jj = lax.broadcasted_iota(jnp.int32, (tk, tk), 1)
        at_or_after = jnp.where(ss >= jj, 1.0, 0.0).astype(BF16)
        dq = jnp.zeros((tq, LANES), F32)
        dk = jnp.zeros((tk, LANES), F32)
        dv = jnp.zeros((tk, LANES), F32)
        for h in range(2):
            qh, gh = _head(q, h), _head(g, h)
            z, lg, a = _sb_tile(qh, k, strict, tk, r_ref[h])
            da = _dot(gh, v, _NT)
            ab = a.astype(BF16)
            d_a = da * ab.astype(F32)
            incl = _dot(d_a.astype(BF16), at_or_after, _NN)
            d_l = _pair_cols(dl_ref[...], h) - rd_ref[h] - incl
            sig = jnp.exp(lg + z)
            dz = jnp.where(strict, d_a * (1.0 - sig) - d_l * sig, 0.0).astype(BF16)
            dq = dq + _dot(dz, _head(k, h), _NN)
            dk = dk + _dot(dz, qh, _TN)
            dv = dv + _dot(ab, gh, _TN)
            r_ref[h] += jnp.sum(lg, axis=-1, keepdims=True)
            rd_ref[h] += jnp.sum(d_a, axis=-1, keepdims=True)
        dqa_ref[...] += dq
        rows = pl.ds(pl.multiple_of(kj * tk, tk), tk)
        dka_ref[rows, :] += dk
        dva_ref[rows, :] += dv

        @pl.when(kj == 0)
        def _():
            dq_ref[...] = (dqa_ref[...] * QK_SCALE).astype(BF16)

        @pl.when(i == ntri - 1)
        def _():
            dk_ref[...] = dka_ref[...].astype(BF16)
            dv_ref[...] = dva_ref[...].astype(BF16)

        if comm:
            pl.when((pl.program_id(0) == N_PAIR - 1) & (i == ntri - 1))(lambda: comm.finish(comm_in, comm_out, sems))

    qblk = lambda p, i, qt, kt: (qt[i], p)
    whole = pl.BlockSpec((s, LANES), lambda p, i, qt, kt: (0, p))
    grid_spec = pltpu.PrefetchScalarGridSpec(
        num_scalar_prefetch=2, grid=(N_PAIR, ntri),
        in_specs=[pl.BlockSpec((tq, LANES), lambda p, i, qt, kt: (qt[i], off + p)),
                  pl.BlockSpec((tk, LANES), lambda p, i, qt, kt: (kt[i], off + N_PAIR + p)),
                  pl.BlockSpec((tk, LANES), lambda p, i, qt, kt: (kt[i], off + 2 * N_PAIR + p)),
                  pl.BlockSpec((tq, LANES), qblk),
                  pl.BlockSpec((None, tq, LANES), lambda p, i, qt, kt: (p, qt[i], 0))] + [ANY] * n_in,
        out_specs=[pl.BlockSpec((tq, LANES), qblk), whole, whole] + [ANY] * n_out,
        scratch_shapes=[pltpu.VMEM((tq, LANES), F32), pltpu.VMEM((s, LANES), F32), pltpu.VMEM((s, LANES), F32),
                        pltpu.VMEM((2, tq, 1), F32), pltpu.VMEM((2, tq, 1), F32)] + (comm.sems if comm else []))
    act = jax.ShapeDtypeStruct((s, WIDTH), BF16)
    outs = pl.pallas_call(
        body, name="sb_bwd", grid_spec=grid_spec, out_shape=[act, act, act] + (comm.out_shapes if comm else []),
        compiler_params=_params(("arbitrary", "arbitrary") if comm else ("parallel", "arbitrary")),
    )(qtab, ktab, proj, proj, proj, do, delta, *(comm.inputs if comm else []))
    return outs[:3], outs[3:]


def _loss_head(y, target, tm=512):
    s, d = y.shape
    tm = min(tm, s)

    def body(y_ref, t_ref, l_ref, dy_ref):
        e = y_ref[...] - t_ref[...]
        dy_ref[...] = e / d
        part = jnp.sum(jnp.sum(e * e, axis=0, keepdims=True), axis=1, keepdims=True)

        @pl.when(pl.program_id(0) == 0)
        def _():
            l_ref[...] = jnp.broadcast_to(part, l_ref.shape)

        @pl.when(pl.program_id(0) > 0)
        def _():
            l_ref[...] += jnp.broadcast_to(part, l_ref.shape)

    row = pl.BlockSpec((tm, d), lambda i: (i, 0))
    return pl.pallas_call(
        body, name="loss_head", grid=(s // tm,), in_specs=[row, row],
        out_specs=[pl.BlockSpec((1, LANES), lambda i: (0, 0)), row],
        out_shape=[jax.ShapeDtypeStruct((1, LANES), F32), jax.ShapeDtypeStruct((s, d), F32)],
        compiler_params=_params(("arbitrary",)),
    )(y, target)


def _adamw(w, g, m, v, name):
    shape = w.shape
    cols = shape[-1]
    rows = int(np.prod(shape[:-1]))
    tm = rows
    for cand in (512, 256, 128, 64, 32, 16, 8):
        if rows % cand == 0 and rows > cand and cand * cols * 4 <= ADAMW_BLOCK_BYTES:
            tm = cand
            break

    def body(w_ref, g_ref, m_ref, v_ref, d_ref, mo_ref, vo_ref):
        gr = g_ref[...]
        mn = ADAM_B1 * m_ref[...] + (1.0 - ADAM_B1) * gr
        vn = ADAM_B2 * v_ref[...] + (1.0 - ADAM_B2) * (gr * gr)
        m_hat = mn / (1.0 - ADAM_B1 ** ADAM_STEP)
        v_hat = vn / (1.0 - ADAM_B2 ** ADAM_STEP)
        d_ref[...] = -ADAM_LR * (m_hat / (jnp.sqrt(v_hat) + ADAM_EPS) + ADAM_WD * w_ref[...])
        mo_ref[...] = mn
        vo_ref[...] = vn

    blk = pl.BlockSpec((tm, cols), lambda i: (i, 0))
    out = jax.ShapeDtypeStruct((rows, cols), F32)
    r2 = lambda a: a.reshape(rows, cols)
    outs = pl.pallas_call(
        body, name=name, grid=(rows // tm,), in_specs=[blk] * 4, out_specs=[blk] * 3, out_shape=[out] * 3,
        compiler_params=_params(("parallel",)),
    )(r2(w), r2(g), r2(m), r2(v))
    return tuple(o.reshape(shape) for o in outs)


W_IN_NATURAL = (("a", CONV_OFF, 3 * WIDTH), ("a", FOX_OFF, 3 * WIDTH), ("f", 0, N_HEADS), ("a", SB_OFF, 3 * WIDTH),
                ("a", GATE_OFF, 3 * D_MODEL))
W_IN_COLS = D_INA + N_HEADS
W_IN_SHARD = W_IN_COLS // 4


def _w_in_from_chips(blocks):
    def natural(lo, hi):
        cols = []
        for k, blk in enumerate(blocks):
            a, b = max(lo, k * W_IN_SHARD), min(hi, (k + 1) * W_IN_SHARD)
            if a < b:
                cols.append(blk[:, a - k * W_IN_SHARD:b - k * W_IN_SHARD])
        return cols

    start = {}
    pos = 0
    for part, at, length in W_IN_NATURAL:
        start[part, at] = (pos, pos + length)
        pos += length
    order = sorted((at, rng) for (part, at), rng in start.items() if part == "a")
    wa = jnp.concatenate([c for _, (lo, hi) in order for c in natural(lo, hi)], axis=1)
    wf = jnp.pad(jnp.concatenate(natural(*start["f", 0]), axis=1), ((0, 0), (0, LANES - N_HEADS)))
    return wa, wf


def _w_in_to_chips(ga, gf):
    chips = []
    for k in range(4):
        lo, hi, pos, cols = k * W_IN_SHARD, (k + 1) * W_IN_SHARD, 0, []
        for part, at, length in W_IN_NATURAL:
            a, b = max(lo, pos), min(hi, pos + length)
            if a < b:
                cols.append((ga if part == "a" else gf)[:, at + a - pos:at + b - pos])
            pos += length
        chips.append(jnp.concatenate(cols, axis=1))
    return jnp.stack(chips)


def _layer_fwd(x, p, plan, l):
    s = x.shape[0]
    proj, hn1 = _norm_mm(x, p["norm1_g"], p["wa"], BF16, name="in_proj")
    fraw = _mm(hn1, p["wf"], "nn", F32, name="in_proj_f")
    ft = fraw[:, :N_HEADS].T
    crow8 = _fox_gates(ft, p["fox_f_bias"])
    crow = crow8.reshape(N_PAIR, 2, s)
    ccol = jnp.repeat(crow.transpose(0, 2, 1), HEAD_DIM, axis=2)
    oc = _conv_mix_fwd(proj, p["conv_w"])
    qn, kn = _fox_prep(proj, p["fox_q_norm_g"], p["fox_k_norm_g"])
    last = l + 1 == plan.depth
    (of, lse), got_in = _fox_fwd(qn, kn, proj, ccol, crow, comm=None if last else plan.gather(l + 1, BIG_IN))
    osb, got_rest = _sb_fwd(proj, comm=plan.gather(l, BIG_REST))
    p.update(plan.weights(l, BIG_REST, got_rest))
    nxt = None if last else plan.layer(l + 1, got_in)
    xm, merged, yc, yf, ys = _merge_fwd(x, oc, of, osb, p["w_proj_conv"], p["w_proj_fox"], p["w_proj_sb"], proj,
                                        p["gate_bias"], p["w_out"])
    up, hn2 = _norm_mm(xm, p["norm2_g"], p["w_up"], BF16, name="up_proj")
    hh = _glu_fwd(up, p["ffn_conv_w"], p["ffn_conv_b"])
    xo = _mm(hh, p["w_down"], "nn", F32, res=xm, name="down_proj")
    saved = dict(x=x, hn1=hn1, proj=proj, ft=ft, crow=crow, ccol=ccol, oc=oc, qn=qn, kn=kn, of=of, lse=lse, osb=osb,
                 merged=merged, yc=yc, yf=yf, ys=ys, xm=xm, hn2=hn2, up=up, hh=hh)
    return xo, saved, nxt


def _layer_bwd(dx, p, a, plan, pending_in, first_layer):
    s = dx.shape[0]
    g = {}
    dhh = _mm(dx, p["w_down"], "nt", BF16, tn=1408, name="d_down_in")
    g["w_down"] = _mm(a["hh"], dx, "tn", F32, name="g_w_down")
    dug, duv, g["ffn_conv_w"], g["ffn_conv_b"] = _glu_bwd(dhh, a["up"], p["ffn_conv_w"], p["ffn_conv_b"])
    dup = jnp.concatenate([dug, duv], axis=1)
    g["w_up"] = _mm(a["hn2"], dup, "tn", F32, tn=1408, name="g_w_up")
    dhn2 = _mm(dup, p["w_up"], "nt", F32, name="d_up_in")
    dx, g["norm2_g"] = _norm_bwd(dhn2, a["xm"], p["norm2_g"], dx, name="norm2_bwd")
    dm = _mm(dx, p["w_out"], "nt", BF16, name="d_out_in")
    g["w_out"] = _mm(a["merged"], dx, "tn", F32, tn=512, name="g_w_out")
    dyc, dyf, dys, dgl, g["gate_bias"] = _gate_bwd(dm, a["yc"], a["yf"], a["ys"], a["proj"], p["gate_bias"])
    doc = _mm(dyc, p["w_proj_conv"], "nt", BF16, name="d_pc_in")
    dof = _mm(dyf, p["w_proj_fox"], "nt", BF16, name="d_pf_in")
    dos = _mm(dys, p["w_proj_sb"], "nt", BF16, name="d_ps_in")
    g["w_proj_conv"] = _mm(a["oc"], dyc, "tn", F32, name="g_w_pc")
    g["w_proj_fox"] = _mm(a["of"], dyf, "tn", F32, name="g_w_pf")
    g["w_proj_sb"] = _mm(a["osb"], dys, "tn", F32, name="g_w_ps")
    dcb, dcc, dch, g["conv_w"] = _conv_mix_bwd(doc, a["proj"], p["conv_w"])
    delta_f = _delta_rep(dof, a["of"])
    pending = plan.reduction(pending_in, {k: g[k] for k in BIG_REST})
    (dqs, dkn, dfv, dcrow, dcq), recv_a = _fox_bwd(a["qn"], a["kn"], a["proj"], dof, a["lse"], delta_f, a["ccol"], a["crow"],
                                                   comm=pending.exchange() if pending else None)
    dc = (dcrow + dcq).reshape(N_HEADS, s)
    dfq, dfk, g["fox_q_norm_g"], g["fox_k_norm_g"] = _fox_post(dqs, dkn, a["proj"], p["fox_q_norm_g"], p["fox_k_norm_g"])
    dft, g["fox_f_bias"] = _fox_gates_bwd(dc, a["ft"], p["fox_f_bias"])
    delta_s = _delta_rep(dos, a["osb"])
    (dsq, dsk, dsv), recv_b = _sb_bwd(a["proj"], dos, delta_s, comm=pending.sums(recv_a) if pending else None)
    done_in = None
    if pending:
        done = pending.finish(recv_b)
        if pending_in is not None:
            done_in, done = done[0], done[1:]
        g.update(zip(BIG_REST, done))
    dproj = jnp.concatenate([dgl, dcb, dcc, dch, dfq, dfk, dfv, dsq, dsk, dsv], axis=1)
    dfp = jnp.pad(dft.T, ((0, 0), (0, LANES - N_HEADS))).astype(BF16)
    ga = _mm(a["hn1"], dproj, "tn", F32, tn=768, name="g_w_in")
    gf = _mm(a["hn1"], dfp, "tn", F32, name="g_w_in_f")
    g["w_in"] = _w_in_to_chips(ga, gf)
    dhn1 = _mm(dfp, p["wf"], "nt", F32, name="d_in_f")
    own = plan.reduction(g["w_in"], {}) if first_layer else None
    if own:
        second = own.sums(_run_comm(own.exchange(), "rs_pair_exchange"))
        dhn1, recv_b = _mm(dproj, p["wa"], "nt", F32, res=dhn1, tk=1536, name="d_in", comm=second)
        (g["w_in"],) = own.finish(recv_b)
    else:
        dhn1 = _mm(dproj, p["wa"], "nt", F32, res=dhn1, tk=1536, name="d_in")
    dx, g["norm1_g"] = _norm_bwd(dhn1, a["x"], p["norm1_g"], dx, name="norm1_bwd")
    return dx, g, done_in


MATMUL_WEIGHTS = ("w_in", "w_proj_conv", "w_proj_fox", "w_proj_sb", "w_out", "w_up", "w_down")
WEIGHTS = ("norm1_g", "w_in", "fox_f_bias", "gate_bias", "conv_w", "fox_q_norm_g", "fox_k_norm_g", "w_proj_conv",
           "w_proj_fox", "w_proj_sb", "w_out", "norm2_g", "w_up", "ffn_conv_w", "ffn_conv_b", "w_down")


def _local_step(x, target, plan):
    depth = plan.depth
    layers, saved = [plan.layer(0, None)], []
    for l in range(depth):
        x, a, nxt = _layer_fwd(x, layers[l], plan, l)
        saved.append(a)
        if nxt is not None:
            layers.append(nxt)
    sq, dx = _loss_head(x, target)
    grads, pending_in = [None] * depth, None
    for l in reversed(range(depth)):
        dx, grads[l], done_in = _layer_bwd(dx, layers[l], saved[l], plan, pending_in, l == 0)
        if done_in is not None:
            grads[l + 1]["w_in"] = done_in
        pending_in = grads[l]["w_in"]
    return sq, dx, grads


ANY = pl.BlockSpec(memory_space=pl.ANY)


def _place():
    x, y, c = lax.axis_index("x"), lax.axis_index("y"), lax.axis_index("c")
    chips = [(1 - x, y), (x, 1 - y), (1 - x, 1 - y)]
    return x, y, c, chips


_Comm = collections.namedtuple("_Comm", "inputs out_shapes sems start finish")


def _run_comm(comm, name):
    n_in, n_out = len(comm.inputs), len(comm.out_shapes)

    def body(*refs):
        ins, outs, sems = refs[:n_in], refs[n_in:n_in + n_out], refs[n_in + n_out:]
        comm.start(ins, outs, sems)
        comm.finish(ins, outs, sems)

    return pl.pallas_call(body, name=name, in_specs=[ANY] * n_in, out_specs=[ANY] * n_out, out_shape=comm.out_shapes,
                          scratch_shapes=comm.sems)(*comm.inputs)


def _gather_comm(shards):
    n = len(shards)

    def copy(x_refs, out_refs, sems, k, t, chip_index, which_half, to, from_input=False):
        half = shards[t].shape[0] // 2
        rows = pl.ds(which_half * half, half)
        dst = out_refs[t].at[chip_index, rows, :]
        return pltpu.make_async_remote_copy(
            src_ref=x_refs[t].at[rows, :] if from_input else dst, dst_ref=dst,
            send_sem=sems[0].at[k, t], recv_sem=sems[1].at[k, t], device_id=to, device_id_type=MESH)

    def first(x_refs, out_refs, sems):
        x, y, c, chips = _place()
        return [copy(x_refs, out_refs, sems, p, t, 2 * x + y, c, (*chip, c), from_input=True)
                for t in range(n) for p, chip in enumerate(chips)]

    def start(x_refs, out_refs, sems):
        for cp in first(x_refs, out_refs, sems):
            cp.start()

    def finish(x_refs, out_refs, sems):
        x, y, c, chips = _place()
        passed = []
        for t in range(n):
            for p, chip in enumerate(chips):
                copy(x_refs, out_refs, sems, p, t, 2 * chip[0] + chip[1], c, (x, y, c)).wait_recv()
                fwd = copy(x_refs, out_refs, sems, 3 + p, t, 2 * chip[0] + chip[1], c, (x, y, 1 - c))
                fwd.start()
                passed.append(fwd)
        for t in range(n):
            for p, chip in enumerate(chips):
                copy(x_refs, out_refs, sems, 3 + p, t, 2 * chip[0] + chip[1], 1 - c, (x, y, c)).wait_recv()
        for cp in first(x_refs, out_refs, sems) + passed:
            cp.wait_send()

    return _Comm(list(shards), [jax.ShapeDtypeStruct((4, *s.shape), s.dtype) for s in shards],
                 [pltpu.SemaphoreType.DMA((6, n)), pltpu.SemaphoreType.DMA((6, n))], start, finish)


def _own_block(got, shards):
    chip = 2 * lax.axis_index("x") + lax.axis_index("y")
    return [lax.dynamic_update_index_in_dim(g, s, chip, 0) for g, s in zip(got, shards)]


def _pair_exchange_comm(gs):
    n = len(gs)

    def copies(g_refs, recv_refs, sems):
        x, y, c, _ = _place()
        return [pltpu.make_async_remote_copy(
            src_ref=g_refs[t].at[:, pl.ds((1 - c) * (gs[t].shape[1] // 2), gs[t].shape[1] // 2), :], dst_ref=recv_refs[t],
            send_sem=sems[0].at[t], recv_sem=sems[1].at[t], device_id=(x, y, 1 - c), device_id_type=MESH) for t in range(n)]

    def start(g_refs, recv_refs, sems):
        for cp in copies(g_refs, recv_refs, sems):
            cp.start()

    def finish(g_refs, recv_refs, sems):
        for cp in copies(g_refs, recv_refs, sems):
            cp.wait()

    return _Comm(list(gs), [jax.ShapeDtypeStruct((4, g.shape[1] // 2, g.shape[2]), g.dtype) for g in gs],
                 [pltpu.SemaphoreType.DMA((n,)), pltpu.SemaphoreType.DMA((n,))], start, finish)


def _pair_sum(g, recv, core, tr=512):
    n, r, cols = g.shape
    half = r // 2
    tr = _row_tile(half, tr)
    nb = half // tr

    def body(c_ref, g_ref, r_ref, o_ref):
        o_ref[...] = (g_ref[...] + r_ref[...]).astype(BF16)

    grid_spec = pltpu.PrefetchScalarGridSpec(
        num_scalar_prefetch=1, grid=(n, nb),
        in_specs=[pl.BlockSpec((None, tr, cols), lambda k, i, c: (k, c[0] * nb + i, 0)),
                  pl.BlockSpec((None, tr, cols), lambda k, i, c: (k, i, 0))],
        out_specs=pl.BlockSpec((None, tr, cols), lambda k, i, c: (k, i, 0)))
    return pl.pallas_call(
        body, name="rs_pair_sum", grid_spec=grid_spec, out_shape=jax.ShapeDtypeStruct((n, half, cols), BF16),
        compiler_params=_params(("parallel", "parallel")),
    )(core, g, recv)


def _row_tile(rows, pref):
    best = None
    for t in range(16, min(rows, pref) + 1, 16):
        if rows % t == 0:
            best = t
    assert best is not None, (rows, pref)
    return best


def _chip_exchange_comm(s1s):
    n = len(s1s)

    def copies(s_refs, recv_refs, sems):
        x, y, c, chips = _place()
        return [pltpu.make_async_remote_copy(
            src_ref=s_refs[t].at[2 * chip[0] + chip[1]], dst_ref=recv_refs[t].at[p],
            send_sem=sems[0].at[p, t], recv_sem=sems[1].at[p, t], device_id=(*chip, c), device_id_type=MESH)
            for t in range(n) for p, chip in enumerate(chips)]

    def start(s_refs, recv_refs, sems):
        for cp in copies(s_refs, recv_refs, sems):
            cp.start()

    def finish(s_refs, recv_refs, sems):
        for cp in copies(s_refs, recv_refs, sems):
            cp.wait()

    return _Comm(list(s1s), [jax.ShapeDtypeStruct((3, *s.shape[1:]), s.dtype) for s in s1s],
                 [pltpu.SemaphoreType.DMA((3, n)), pltpu.SemaphoreType.DMA((3, n))], start, finish)


def _final_sum(g, recv_a, recv_b, core, chip, tr=256):
    n, r, cols = g.shape
    half = r // 2
    tr = _row_tile(half, tr)
    nb = half // tr

    def body(c_ref, k_ref, g_ref, a_ref, b0_ref, b1_ref, b2_ref, o_ref):
        total = g_ref[...] + a_ref[...]
        for b_ref in (b0_ref, b1_ref, b2_ref):
            total = total + b_ref[...].astype(F32)
        o_ref[...] = total

    rel = lambda p: pl.BlockSpec((None, tr, cols), lambda i, c, k, p=p: (p, i, 0))
    grid_spec = pltpu.PrefetchScalarGridSpec(
        num_scalar_prefetch=2, grid=(nb,),
        in_specs=[pl.BlockSpec((None, tr, cols), lambda i, c, k: (k[0], c[0] * nb + i, 0)),
                  pl.BlockSpec((None, tr, cols), lambda i, c, k: (k[0], i, 0)), rel(0), rel(1), rel(2)],
        out_specs=pl.BlockSpec((tr, cols), lambda i, c, k: (c[0] * nb + i, 0)))
    return pl.pallas_call(
        body, name="rs_final_sum", grid_spec=grid_spec, out_shape=jax.ShapeDtypeStruct((r, cols), F32),
        compiler_params=_params(("parallel",)),
    )(core, chip, g, recv_a, recv_b, recv_b, recv_b)


def _pair_join(fs):
    n = len(fs)

    def body(*refs):
        out_refs = refs[n:2 * n]
        send_sems, recv_sems = refs[2 * n:]
        x, y, c, _ = _place()

        def copy(t, which_half):
            h = fs[t].shape[0] // 2
            rows = out_refs[t].at[pl.ds(which_half * h, h), :]
            return pltpu.make_async_remote_copy(
                src_ref=rows, dst_ref=rows, send_sem=send_sems.at[t], recv_sem=recv_sems.at[t],
                device_id=(x, y, 1 - c), device_id_type=MESH)

        sends = [copy(t, c) for t in range(n)]
        for cp in sends:
            cp.start()
        for t in range(n):
            sends[t].wait_send()
            copy(t, 1 - c).wait_recv()

    return pl.pallas_call(
        body, name="rs_pair_join", in_specs=[ANY] * n, out_specs=[ANY] * n,
        out_shape=[jax.ShapeDtypeStruct(f.shape, f.dtype) for f in fs],
        input_output_aliases={t: t for t in range(n)},
        scratch_shapes=[pltpu.SemaphoreType.DMA((n,)), pltpu.SemaphoreType.DMA((n,))],
    )(*fs)


class _Reduction:
    def __init__(self, gs):
        self.gs = gs
        self.core = lax.axis_index("c").astype(jnp.int32).reshape(1)
        self.chip = (2 * lax.axis_index("x") + lax.axis_index("y")).astype(jnp.int32).reshape(1)

    def exchange(self):
        return _pair_exchange_comm(self.gs)

    def sums(self, recv_a):
        self.recv_a = list(recv_a)
        return _chip_exchange_comm([_pair_sum(g, ra, self.core) for g, ra in zip(self.gs, self.recv_a)])

    def finish(self, recv_b):
        return _pair_join([_final_sum(g, ra, rb, self.core, self.chip) for g, ra, rb in zip(self.gs, self.recv_a, recv_b)])

    def run(self):
        recv_a = _run_comm(self.exchange(), "rs_pair_exchange")
        return self.finish(_run_comm(self.sums(recv_a), "rs_chip_exchange"))


def _all_reduce_small(v):
    r, cols = v.shape

    def body(v_ref, out_ref, buf_ref, send_sems, recv_sems):
        x, y, c, _ = _place()
        flip = lambda a, bit: 1 - a if bit else a
        buf_ref[4 * x + 2 * y + c] = v_ref[...]
        cps = []
        for rel in range(1, 8):
            peer = (flip(x, rel & 4), flip(y, rel & 2), flip(c, rel & 1))
            cps.append(pltpu.make_async_remote_copy(
                src_ref=v_ref, dst_ref=buf_ref.at[4 * x + 2 * y + c], send_sem=send_sems.at[rel - 1], recv_sem=recv_sems.at[rel - 1],
                device_id=peer, device_id_type=MESH))
        for cp in cps:
            cp.start()
        for cp in cps:
            cp.wait()
        total = buf_ref[0]
        for d in range(1, 8):
            total = total + buf_ref[d]
        out_ref[...] = total

    vm = pl.BlockSpec(memory_space=pltpu.VMEM)
    return pl.pallas_call(
        body, name="all_reduce_small", in_specs=[vm], out_specs=vm, out_shape=jax.ShapeDtypeStruct((r, cols), F32),
        scratch_shapes=[pltpu.VMEM((8, r, cols), F32), pltpu.SemaphoreType.DMA((7,)), pltpu.SemaphoreType.DMA((7,))],
    )(v)


SHARD_AXIS = {"w_in": 1, "conv_w": 1, "w_proj_conv": 1, "w_proj_fox": 1, "w_proj_sb": 1, "w_out": 0, "w_up": 1,
              "ffn_conv_w": 1, "w_down": 0}
SMALL_SHARDED = ("conv_w", "ffn_conv_w")
BIG = tuple(k for k in SHARD_AXIS if k not in SMALL_SHARDED)
BIG_IN = ("w_in",)
BIG_REST = tuple(k for k in BIG if k not in BIG_IN)
REPLICATED = tuple(k for k in WEIGHTS if k not in SHARD_AXIS)
SMALL = REPLICATED + SMALL_SHARDED


def _pack_small(parts, row_align):
    flat = jnp.concatenate([p.reshape(-1) for p in parts])
    rows = -(-flat.shape[0] // (PACK_COLS * row_align)) * row_align
    return jnp.pad(flat, (0, rows * PACK_COLS - flat.shape[0])).reshape(rows, PACK_COLS)


def _unpack_small(packed, shapes):
    flat = packed.reshape(-1)
    out, off = [], 0
    for shape in shapes:
        size = int(np.prod(shape))
        out.append(flat[off:off + size].reshape(shape))
        off += size
    return out


def _blocks(stacked, own):
    chip = 2 * lax.axis_index("x") + lax.axis_index("y")
    return [jnp.where(chip == k, own, stacked[k]) for k in range(4)]


def _to_chips(full, axis):
    a, b = full.shape
    if axis == 0:
        return full.reshape(4, a // 4, b)
    return jnp.moveaxis(full.reshape(a, 4, b // 4), 1, 0)


class _Plan:
    def __init__(self, given):
        self.given = given
        self.depth = given["norm1_g"].shape[0]
        conv_shapes = [given[k].shape for k in SMALL_SHARDED]
        packed = [_pack_small([given[k] for k in SMALL_SHARDED], 16)]
        (got,) = _own_block(_run_comm(_gather_comm(packed), "gather_conv_weights"), packed)
        by_chip = [_unpack_small(got[j], conv_shapes) for j in range(4)]
        self.conv = {k: jnp.concatenate([by_chip[j][i] for j in range(4)], axis=-1) for i, k in enumerate(SMALL_SHARDED)}
        self.shards = {}

    def _shards(self, l, names):
        if (l, names) not in self.shards:
            self.shards[l, names] = [self.given[k][l].astype(BF16) for k in names]
        return self.shards[l, names]

    def gather(self, l, names):
        return _gather_comm(self._shards(l, names))

    def weights(self, l, names, got):
        out = {}
        for k, stacked, own in zip(names, got, self._shards(l, names)):
            if k == "w_in":
                out["wa"], out["wf"] = _w_in_from_chips(_blocks(stacked, own))
            else:
                out[k] = jnp.concatenate(_blocks(stacked, own), axis=SHARD_AXIS[k])
        return out

    def layer(self, l, got_in):
        if got_in is None:
            got_in = _run_comm(self.gather(l, BIG_IN), "gather_w_in")
        p = {k: self.given[k][l] for k in REPLICATED}
        p.update({k: self.conv[k][l] for k in SMALL_SHARDED})
        p.update(self.weights(l, BIG_IN, got_in))
        return p

    def reduction(self, w_in_grad, rest):
        gs = [] if w_in_grad is None else [w_in_grad]
        return _Reduction(gs + [_to_chips(rest[k], SHARD_AXIS[k]) for k in BIG_REST if k in rest])


def kernel(x, norm1_g, w_in, fox_f_bias, gate_bias, conv_w, fox_q_norm_g, fox_k_norm_g, w_proj_conv, w_proj_fox, w_proj_sb, w_out, norm2_g, w_up, ffn_conv_w, ffn_conv_b, w_down, loss_target, m_norm1_g, m_w_in, m_fox_f_bias, m_gate_bias, m_conv_w, m_fox_q_norm_g, m_fox_k_norm_g, m_w_proj_conv, m_w_proj_fox, m_w_proj_sb, m_w_out, m_norm2_g, m_w_up, m_ffn_conv_w, m_ffn_conv_b, m_w_down, v_norm1_g, v_w_in, v_fox_f_bias, v_gate_bias, v_conv_w, v_fox_q_norm_g, v_fox_k_norm_g, v_w_proj_conv, v_w_proj_fox, v_w_proj_sb, v_w_out, v_norm2_g, v_w_up, v_ffn_conv_w, v_ffn_conv_b, v_w_down):
    given = dict(x=x, norm1_g=norm1_g, w_in=w_in, fox_f_bias=fox_f_bias, gate_bias=gate_bias, conv_w=conv_w, fox_q_norm_g=fox_q_norm_g, fox_k_norm_g=fox_k_norm_g, w_proj_conv=w_proj_conv, w_proj_fox=w_proj_fox, w_proj_sb=w_proj_sb, w_out=w_out, norm2_g=norm2_g, w_up=w_up, ffn_conv_w=ffn_conv_w, ffn_conv_b=ffn_conv_b, w_down=w_down, loss_target=loss_target, m_norm1_g=m_norm1_g, m_w_in=m_w_in, m_fox_f_bias=m_fox_f_bias, m_gate_bias=m_gate_bias, m_conv_w=m_conv_w, m_fox_q_norm_g=m_fox_q_norm_g, m_fox_k_norm_g=m_fox_k_norm_g, m_w_proj_conv=m_w_proj_conv, m_w_proj_fox=m_w_proj_fox, m_w_proj_sb=m_w_proj_sb, m_w_out=m_w_out, m_norm2_g=m_norm2_g, m_w_up=m_w_up, m_ffn_conv_w=m_ffn_conv_w, m_ffn_conv_b=m_ffn_conv_b, m_w_down=m_w_down, v_norm1_g=v_norm1_g, v_w_in=v_w_in, v_fox_f_bias=v_fox_f_bias, v_gate_bias=v_gate_bias, v_conv_w=v_conv_w, v_fox_q_norm_g=v_fox_q_norm_g, v_fox_k_norm_g=v_fox_k_norm_g, v_w_proj_conv=v_w_proj_conv, v_w_proj_fox=v_w_proj_fox, v_w_proj_sb=v_w_proj_sb, v_w_out=v_w_out, v_norm2_g=v_norm2_g, v_w_up=v_w_up, v_ffn_conv_w=v_ffn_conv_w, v_ffn_conv_b=v_ffn_conv_b, v_w_down=v_w_down)
    depth = given["norm1_g"].shape[0]
    chip = 2 * lax.axis_index("x") + lax.axis_index("y")

    sq, dx, grads = _local_step(given["x"][0], given["loss_target"][0], _Plan(given))
    loss = lax.psum(0.5 * sq[0, 0] / D_MODEL, ("x", "y", "c"))

    gsum = {k: jnp.stack([g[k] for g in grads]) for k in BIG}
    small_shapes = [(depth, *grads[0][k].shape) for k in SMALL]
    summed = _all_reduce_small(_pack_small([jnp.stack([g[k] for g in grads]) for k in SMALL], 8))
    for k, total in zip(SMALL, _unpack_small(summed, small_shapes)):
        if k in SMALL_SHARDED:
            total = lax.dynamic_index_in_dim(total.reshape(*total.shape[:-1], 4, total.shape[-1] // 4), chip, axis=2, keepdims=False)
        gsum[k] = total

    deltas, new_m, new_v = {}, {}, {}
    for k in WEIGHTS:
        deltas[k], new_m[k], new_v[k] = _adamw(given[k], gsum[k], given["m_" + k], given["v_" + k], "adamw_" + k)
    return (loss, dx[None], *[gsum[k] for k in WEIGHTS], *[deltas[k] for k in WEIGHTS],
            *[new_m[k] for k in WEIGHTS], *[new_v[k] for k in WEIGHTS])
```

```python
import collections

import numpy as np
import jax
import jax.numpy as jnp
from jax import lax
from jax.experimental import pallas as pl
from jax.experimental.pallas import tpu as pltpu

F32 = jnp.float32
BF16 = jnp.bfloat16

D_MODEL = 1024
DEPTH = 4
HEAD_DIM = 64
N_HEADS = 8
WIDTH = 512
D_FF = 2816
NORM_EPS = 1e-6
NEG_INF = -1e30
QK_SCALE = HEAD_DIM ** -0.5
LANES = 128
N_PAIR = N_HEADS // 2

GATE_OFF = 0
CONV_OFF = 3 * D_MODEL
FOX_OFF = CONV_OFF + 3 * WIDTH
SB_OFF = FOX_OFF + 3 * WIDTH
D_INA = SB_OFF + 3 * WIDTH

ADAM_LR = 0.001
ADAM_B1 = 0.9
ADAM_B2 = 0.999
ADAM_EPS = 1e-08
ADAM_WD = 0.01
ADAM_STEP = 10

VMEM_LIMIT = 48 * 1024 * 1024
ADAMW_BLOCK_BYTES = 2048 * 1024

PACK_COLS = 1024
MESH = pl.DeviceIdType.MESH


def _params(sem):
    return pltpu.CompilerParams(dimension_semantics=sem, vmem_limit_bytes=VMEM_LIMIT)


def _dot(a, b, dims):
    return lax.dot_general(a, b, (dims, ((), ())), preferred_element_type=F32)


_NN = ((1,), (0,))
_NT = ((1,), (1,))
_TN = ((0,), (0,))


def _pick(dim, pref):
    if dim <= pref:
        return dim
    best = None
    for mult in range(1, dim // LANES + 1):
        t = mult * LANES
        if t <= pref and dim % t == 0:
            best = t
    assert best is not None, (dim, pref)
    return best


def _mm(a, b, mode, out_dtype=F32, tm=1024, tn=1024, tk=2048, res=None, name="mm", comm=None):
    if mode == "nn":
        (m, k), (_, n) = a.shape, b.shape
    elif mode == "nt":
        (m, k), (n, _) = a.shape, b.shape
    else:
        (k, m), (_, n) = a.shape, b.shape
    tm, tn, tk = _pick(m, tm), _pick(n, tn), _pick(k, tk)
    nk = k // tk
    dims = {"nn": _NN, "nt": _NT, "tn": _TN}[mode]
    if mode == "tn":
        a_spec = pl.BlockSpec((tk, tm), lambda i, j, kk: (kk, i))
    else:
        a_spec = pl.BlockSpec((tm, tk), lambda i, j, kk: (i, kk))
    if mode == "nt":
        b_spec = pl.BlockSpec((tn, tk), lambda i, j, kk: (j, kk))
    else:
        b_spec = pl.BlockSpec((tk, tn), lambda i, j, kk: (kk, j))
    o_spec = pl.BlockSpec((tm, tn), lambda i, j, kk: (i, j))
    in_specs = [a_spec, b_spec] + ([o_spec] if res is not None else [])
    n_in, n_out = (len(comm.inputs), len(comm.out_shapes)) if comm else (0, 0)
    gm, gn = m // tm, n // tn

    def body(*refs):
        a_ref, b_ref, refs = refs[0], refs[1], refs[2:]
        r_ref, refs = (refs[0], refs[1:]) if res is not None else (None, refs)
        comm_in, o_ref, refs = refs[:n_in], refs[n_in], refs[n_in + 1:]
        comm_out, acc_ref, sems = refs[:n_out], refs[n_out], refs[n_out + 1:]
        kk = pl.program_id(2)
        if comm:
            at = lambda i, j, k: (pl.program_id(0) == i) & (pl.program_id(1) == j) & (kk == k)
            pl.when(at(0, 0, 0))(lambda: comm.start(comm_in, comm_out, sems))
        part = _dot(a_ref[...].astype(BF16), b_ref[...].astype(BF16), dims)

        def finish(total):
            if r_ref is not None:
                total = total + r_ref[...].astype(F32)
            o_ref[...] = total.astype(out_dtype)

        if nk == 1:
            finish(part)
        else:
            @pl.when(kk == 0)
            def _():
                acc_ref[...] = part

            @pl.when(kk > 0)
            def _():
                acc_ref[...] += part

            @pl.when(kk == nk - 1)
            def _():
                finish(acc_ref[...])

        if comm:
            pl.when(at(gm - 1, gn - 1, nk - 1))(lambda: comm.finish(comm_in, comm_out, sems))

    args = (a, b) + ((res,) if res is not None else ())
    out = jax.ShapeDtypeStruct((m, n), out_dtype)
    acc = pltpu.VMEM((tm, tn) if nk > 1 else (8, LANES), F32)
    if not comm:
        return pl.pallas_call(
            body, name=name, grid=(gm, gn, nk), in_specs=in_specs, out_specs=o_spec, out_shape=out, scratch_shapes=[acc],
            compiler_params=_params(("parallel", "parallel", "arbitrary")),
        )(*args)
    outs = pl.pallas_call(
        body, name=name, grid=(gm, gn, nk), in_specs=in_specs + [ANY] * n_in, out_specs=[o_spec] + [ANY] * n_out,
        out_shape=[out] + comm.out_shapes, scratch_shapes=[acc] + comm.sems,
        compiler_params=_params(("arbitrary", "arbitrary", "arbitrary")),
    )(*args, *comm.inputs)
    return outs[0], outs[1:]


def _norm_mm(x, g, w, out_dtype, tm=1024, tn=1536, name="norm_mm"):
    m, d = x.shape
    n = w.shape[1]
    tm, tn = _pick(m, tm), _pick(n, tn)

    def body(x_ref, g_ref, w_ref, o_ref, hn_ref):
        @pl.when(pl.program_id(1) == 0)
        def _():
            xf = x_ref[...]
            r = lax.rsqrt(jnp.mean(xf * xf, axis=-1, keepdims=True) + NORM_EPS)
            hn_ref[...] = (xf * r * g_ref[...]).astype(BF16)

        o_ref[...] = _dot(hn_ref[...], w_ref[...], _NN).astype(out_dtype)

    return pl.pallas_call(
        body, name=name, grid=(m // tm, n // tn),
        in_specs=[pl.BlockSpec((tm, d), lambda i, j: (i, 0)),
                  pl.BlockSpec((1, d), lambda i, j: (0, 0)),
                  pl.BlockSpec((d, tn), lambda i, j: (0, j))],
        out_specs=[pl.BlockSpec((tm, tn), lambda i, j: (i, j)),
                   pl.BlockSpec((tm, d), lambda i, j: (i, 0))],
        out_shape=[jax.ShapeDtypeStruct((m, n), out_dtype), jax.ShapeDtypeStruct((m, d), BF16)],
        compiler_params=_params(("parallel", "arbitrary")),
    )(x, g.reshape(1, d), w)


def _norm_bwd(dhn, x, g, dx_in, tm=512, name="norm_bwd"):
    m, d = x.shape
    tm = min(tm, m)

    def body(dhn_ref, x_ref, g_ref, dxi_ref, dx_ref, gg_ref):
        xf = x_ref[...]
        r = lax.rsqrt(jnp.mean(xf * xf, axis=-1, keepdims=True) + NORM_EPS)
        xhat = xf * r
        dh = dhn_ref[...].astype(F32)
        dxn = dh * g_ref[...]
        mean = jnp.mean(dxn * xhat, axis=-1, keepdims=True)
        dx_ref[...] = dxi_ref[...] + r * (dxn - xhat * mean)
        part = jnp.sum(dh * xhat, axis=0, keepdims=True)

        @pl.when(pl.program_id(0) == 0)
        def _():
            gg_ref[...] = part

        @pl.when(pl.program_id(0) > 0)
        def _():
            gg_ref[...] += part

    row = pl.BlockSpec((tm, d), lambda i: (i, 0))
    vec = pl.BlockSpec((1, d), lambda i: (0, 0))
    dx, gg = pl.pallas_call(
        body, name=name, grid=(m // tm,), in_specs=[row, row, vec, row], out_specs=[row, vec],
        out_shape=[jax.ShapeDtypeStruct((m, d), F32), jax.ShapeDtypeStruct((1, d), F32)],
        compiler_params=_params(("arbitrary",)),
    )(dhn, x, g.reshape(1, d), dx_in)
    return dx, gg.reshape(d)


def _down(u, k):
    s = u.shape[0]
    rows = lax.broadcasted_iota(jnp.int32, u.shape, 0)
    return jnp.where(rows < k, 0.0, pltpu.roll(u, k, axis=0))


def _up(u, k):
    s = u.shape[0]
    rows = lax.broadcasted_iota(jnp.int32, u.shape, 0)
    return jnp.where(rows >= s - k, 0.0, pltpu.roll(u, s - k, axis=0))


def _conv_mix_fwd(proj, conv_w, tc=256):
    s = proj.shape[0]
    nb = WIDTH // tc
    off = CONV_OFF // tc

    def body(b_ref, c_ref, h_ref, w_ref, o_ref):
        u = c_ref[...].astype(F32) * h_ref[...].astype(F32)
        w = w_ref[...]
        cv = w[0:1] * _down(u, 2) + w[1:2] * _down(u, 1) + w[2:3] * u
        o_ref[...] = (b_ref[...].astype(F32) * cv).astype(BF16)

    col = lambda k: pl.BlockSpec((s, tc), lambda j, k=k: (0, off + k * nb + j))
    return pl.pallas_call(
        body, name="conv_mix_fwd", grid=(nb,),
        in_specs=[col(0), col(1), col(2), pl.BlockSpec((3, tc), lambda j: (0, j))],
        out_specs=pl.BlockSpec((s, tc), lambda j: (0, j)),
        out_shape=jax.ShapeDtypeStruct((s, WIDTH), BF16),
        compiler_params=_params(("parallel",)),
    )(proj, proj, proj, conv_w)


def _conv_mix_bwd(do, proj, conv_w, tc=256):
    s = proj.shape[0]
    nb = WIDTH // tc
    off = CONV_OFF // tc

    def body(do_ref, b_ref, c_ref, h_ref, w_ref, db_ref, dc_ref, dh_ref, gw_ref):
        b = b_ref[...].astype(F32)
        c = c_ref[...].astype(F32)
        h = h_ref[...].astype(F32)
        g = do_ref[...].astype(F32)
        w = w_ref[...]
        u = c * h
        u1, u2 = _down(u, 1), _down(u, 2)
        cv = w[0:1] * u2 + w[1:2] * u1 + w[2:3] * u
        db_ref[...] = (g * cv).astype(BF16)
        dcv = g * b
        gw_ref[0:1, :] = jnp.sum(dcv * u2, axis=0, keepdims=True)
        gw_ref[1:2, :] = jnp.sum(dcv * u1, axis=0, keepdims=True)
        gw_ref[2:3, :] = jnp.sum(dcv * u, axis=0, keepdims=True)
        du = w[2:3] * dcv + w[1:2] * _up(dcv, 1) + w[0:1] * _up(dcv, 2)
        dc_ref[...] = (du * h).astype(BF16)
        dh_ref[...] = (du * c).astype(BF16)

    col = lambda k: pl.BlockSpec((s, tc), lambda j, k=k: (0, off + k * nb + j))
    own = pl.BlockSpec((s, tc), lambda j: (0, j))
    wsp = pl.BlockSpec((3, tc), lambda j: (0, j))
    act = jax.ShapeDtypeStruct((s, WIDTH), BF16)
    return pl.pallas_call(
        body, name="conv_mix_bwd", grid=(nb,),
        in_specs=[own, col(0), col(1), col(2), wsp], out_specs=[own, own, own, wsp],
        out_shape=[act, act, act, jax.ShapeDtypeStruct((3, WIDTH), F32)],
        compiler_params=_params(("parallel",)),
    )(do, proj, proj, proj, conv_w)


def _glu_fwd(up, w, b, tc=256):
    s = up.shape[0]
    nb = D_FF // tc

    def body(g_ref, v_ref, w_ref, b_ref, o_ref):
        ug = g_ref[...].astype(F32)
        wv = w_ref[...]
        cg = wv[0:1] * _down(ug, 2) + wv[1:2] * _down(ug, 1) + wv[2:3] * ug + b_ref[...]
        act = cg * jax.nn.sigmoid(cg)
        o_ref[...] = (act * v_ref[...].astype(F32)).astype(BF16)

    return pl.pallas_call(
        body, name="glu_fwd", grid=(nb,),
        in_specs=[pl.BlockSpec((s, tc), lambda j: (0, j)), pl.BlockSpec((s, tc), lambda j: (0, nb + j)),
                  pl.BlockSpec((3, tc), lambda j: (0, j)), pl.BlockSpec((1, tc), lambda j: (0, j))],
        out_specs=pl.BlockSpec((s, tc), lambda j: (0, j)),
        out_shape=jax.ShapeDtypeStruct((s, D_FF), BF16),
        compiler_params=_params(("parallel",)),
    )(up, up, w, b.reshape(1, D_FF))


def _glu_bwd(dh, up, w, b, tc=256):
    s = up.shape[0]
    nb = D_FF // tc

    def body(dh_ref, g_ref, v_ref, w_ref, b_ref, dg_ref, dv_ref, gw_ref, gb_ref):
        ug = g_ref[...].astype(F32)
        uv = v_ref[...].astype(F32)
        d = dh_ref[...].astype(F32)
        wv = w_ref[...]
        u1, u2 = _down(ug, 1), _down(ug, 2)
        cg = wv[0:1] * u2 + wv[1:2] * u1 + wv[2:3] * ug + b_ref[...]
        sg = jax.nn.sigmoid(cg)
        dv_ref[...] = (d * (cg * sg)).astype(BF16)
        dcg = d * uv * (sg * (1.0 + cg * (1.0 - sg)))
        gb_ref[...] = jnp.sum(dcg, axis=0, keepdims=True)
        gw_ref[0:1, :] = jnp.sum(dcg * u2, axis=0, keepdims=True)
        gw_ref[1:2, :] = jnp.sum(dcg * u1, axis=0, keepdims=True)
        gw_ref[2:3, :] = jnp.sum(dcg * ug, axis=0, keepdims=True)
        dg_ref[...] = (wv[2:3] * dcg + wv[1:2] * _up(dcg, 1) + wv[0:1] * _up(dcg, 2)).astype(BF16)

    own = pl.BlockSpec((s, tc), lambda j: (0, j))
    wsp = pl.BlockSpec((3, tc), lambda j: (0, j))
    bsp = pl.BlockSpec((1, tc), lambda j: (0, j))
    act = jax.ShapeDtypeStruct((s, D_FF), BF16)
    dg, dv, gw, gb = pl.pallas_call(
        body, name="glu_bwd", grid=(nb,),
        in_specs=[own, own, pl.BlockSpec((s, tc), lambda j: (0, nb + j)), wsp, bsp],
        out_specs=[own, own, wsp, bsp],
        out_shape=[act, act, jax.ShapeDtypeStruct((3, D_FF), F32), jax.ShapeDtypeStruct((1, D_FF), F32)],
        compiler_params=_params(("parallel",)),
    )(dh, up, up, w, b.reshape(1, D_FF))
    return dg, dv, gw, gb.reshape(D_FF)


def _merge_fwd(x, oc, of, osb, wpc, wpf, wps, proj, gb, wout, tm=256):
    s, d = x.shape
    tm = min(tm, s)

    def body(x_ref, oc_ref, of_ref, os_ref, wpc_ref, wpf_ref, wps_ref, g0_ref, g1_ref, g2_ref, gb_ref, wo_ref,
             xo_ref, mg_ref, yc_ref, yf_ref, ys_ref):
        merged = jnp.zeros((tm, d), F32)
        for k, (o_ref, w_ref, g_ref, y_ref) in enumerate(
                ((oc_ref, wpc_ref, g0_ref, yc_ref), (of_ref, wpf_ref, g1_ref, yf_ref), (os_ref, wps_ref, g2_ref, ys_ref))):
            y = _dot(o_ref[...].astype(BF16), w_ref[...], _NN)
            y_ref[...] = y.astype(BF16)
            gate = jax.nn.sigmoid(g_ref[...].astype(F32) + gb_ref[:, k * d:(k + 1) * d])
            merged = merged + gate * y
        mb = merged.astype(BF16)
        mg_ref[...] = mb
        xo_ref[...] = x_ref[...] + _dot(mb, wo_ref[...], _NN)

    rowd = pl.BlockSpec((tm, d), lambda i: (i, 0))
    roww = pl.BlockSpec((tm, WIDTH), lambda i: (i, 0))
    wp = pl.BlockSpec((WIDTH, d), lambda i: (0, 0))
    gcol = lambda k: pl.BlockSpec((tm, d), lambda i, k=k: (i, GATE_OFF // d + k))
    actd = jax.ShapeDtypeStruct((s, d), BF16)
    return pl.pallas_call(
        body, name="merge_fwd", grid=(s // tm,),
        in_specs=[rowd, roww, roww, roww, wp, wp, wp, gcol(0), gcol(1), gcol(2),
                  pl.BlockSpec((1, 3 * d), lambda i: (0, 0)), pl.BlockSpec((d, d), lambda i: (0, 0))],
        out_specs=[rowd, rowd, rowd, rowd, rowd],
        out_shape=[jax.ShapeDtypeStruct((s, d), F32), actd, actd, actd, actd],
        compiler_params=_params(("parallel",)),
    )(x, oc, of, osb, wpc, wpf, wps, proj, proj, proj, gb.reshape(1, 3 * d), wout)


def _gate_bwd(dm, yc, yf, ys, proj, gb, tm=512):
    s, d = dm.shape
    tm = min(tm, s)

    def body(dm_ref, yc_ref, yf_ref, ys_ref, g0_ref, g1_ref, g2_ref, gb_ref, dyc_ref, dyf_ref, dys_ref, dgl_ref, ggb_ref):
        g = dm_ref[...].astype(F32)
        parts = []
        for k, (y_ref, g_ref, dy_ref) in enumerate(((yc_ref, g0_ref, dyc_ref), (yf_ref, g1_ref, dyf_ref), (ys_ref, g2_ref, dys_ref))):
            gate = jax.nn.sigmoid(g_ref[...].astype(F32) + gb_ref[:, k * d:(k + 1) * d])
            dy_ref[...] = (g * gate).astype(BF16)
            dgl = g * y_ref[...].astype(F32) * gate * (1.0 - gate)
            dgl_ref[:, k * d:(k + 1) * d] = dgl.astype(BF16)
            parts.append(jnp.sum(dgl, axis=0, keepdims=True))
        part = jnp.concatenate(parts, axis=1)

        @pl.when(pl.program_id(0) == 0)
        def _():
            ggb_ref[...] = part

        @pl.when(pl.program_id(0) > 0)
        def _():
            ggb_ref[...] += part

    rowd = pl.BlockSpec((tm, d), lambda i: (i, 0))
    gcol = lambda k: pl.BlockSpec((tm, d), lambda i, k=k: (i, GATE_OFF // d + k))
    vec = pl.BlockSpec((1, 3 * d), lambda i: (0, 0))
    actd = jax.ShapeDtypeStruct((s, d), BF16)
    dyc, dyf, dys, dgl, ggb = pl.pallas_call(
        body, name="gate_bwd", grid=(s // tm,),
        in_specs=[rowd, rowd, rowd, rowd, gcol(0), gcol(1), gcol(2), vec],
        out_specs=[rowd, rowd, rowd, pl.BlockSpec((tm, 3 * d), lambda i: (i, 0)), vec],
        out_shape=[actd, actd, actd, jax.ShapeDtypeStruct((s, 3 * d), BF16), jax.ShapeDtypeStruct((1, 3 * d), F32)],
        compiler_params=_params(("arbitrary",)),
    )(dm, yc, yf, ys, proj, proj, proj, gb.reshape(1, 3 * d))
    return dyc, dyf, dys, dgl, ggb.reshape(3 * d)


def _tri_tables(nq, r, order):
    last = lambda qi: (qi + 1) * r - 1
    if order == "k_outer":
        pairs = [(qi, kj) for kj in range(nq * r) for qi in range(kj // r, nq)]
    elif order == "k_desc":
        pairs = [(qi, kj) for qi in range(nq) for kj in range(last(qi), -1, -1)]
    else:
        pairs = [(qi, kj) for qi in range(nq) for kj in range(last(qi) + 1)]
    qs, ks = zip(*pairs)
    return jnp.asarray(np.array(qs, np.int32)), jnp.asarray(np.array(ks, np.int32)), len(pairs)


def _lo_mask(shape):
    return lax.broadcasted_iota(jnp.int32, shape, len(shape) - 1) < HEAD_DIM


def _head(x, h):
    lo = _lo_mask(x.shape)
    return jnp.where(lo if h == 0 else jnp.logical_not(lo), x, jnp.zeros_like(x))


def _pair_cols(x, h):
    return x[:, h * HEAD_DIM:h * HEAD_DIM + 1]


def _rep(a0, a1, shape):
    return jnp.where(_lo_mask(shape), a0, a1)


def _positions(qi, kj, tq, tk):
    row = qi * tq + lax.broadcasted_iota(jnp.int32, (tq, tk), 0)
    col = kj * tk + lax.broadcasted_iota(jnp.int32, (tq, tk), 1)
    return row, col


def _head_norm(x, g):
    lo = _lo_mask(x.shape)
    sq = x * x
    s0 = jnp.sum(jnp.where(lo, sq, 0.0), axis=-1, keepdims=True)
    s1 = jnp.sum(jnp.where(lo, 0.0, sq), axis=-1, keepdims=True)
    r = jnp.where(lo, lax.rsqrt(s0 / HEAD_DIM + NORM_EPS), lax.rsqrt(s1 / HEAD_DIM + NORM_EPS))
    return x * r, r


def _fox_prep(proj, gq, gk, tm=2048):
    s = proj.shape[0]
    tm = min(tm, s)
    off = FOX_OFF // LANES

    def body(q_ref, k_ref, gq_ref, gk_ref, qn_ref, kn_ref):
        qh, _ = _head_norm(q_ref[...].astype(F32), None)
        kh, _ = _head_norm(k_ref[...].astype(F32), None)
        qn_ref[...] = (qh * gq_ref[...] * QK_SCALE).astype(BF16)
        kn_ref[...] = (kh * gk_ref[...]).astype(BF16)

    vec = pl.BlockSpec((1, LANES), lambda p, i: (0, 0))
    own = pl.BlockSpec((tm, LANES), lambda p, i: (i, p))
    act = jax.ShapeDtypeStruct((s, WIDTH), BF16)
    return pl.pallas_call(
        body, name="fox_prep", grid=(N_PAIR, s // tm),
        in_specs=[pl.BlockSpec((tm, LANES), lambda p, i: (i, off + p)),
                  pl.BlockSpec((tm, LANES), lambda p, i: (i, off + N_PAIR + p)), vec, vec],
        out_specs=[own, own], out_shape=[act, act],
        compiler_params=_params(("parallel", "parallel")),
    )(proj, proj, jnp.tile(gq, 2).reshape(1, LANES), jnp.tile(gk, 2).reshape(1, LANES))


def _fox_post(dqs, dkn, proj, gq, gk, tm=2048):
    s = proj.shape[0]
    tm = min(tm, s)
    off = FOX_OFF // LANES

    def one(d_ref, x_ref, g_ref, scale, dx_ref, gg_ref, first):
        xhat, r = _head_norm(x_ref[...].astype(F32), None)
        dy = d_ref[...] * scale
        part = jnp.sum(dy * xhat, axis=0, keepdims=True)

        @pl.when(first)
        def _():
            gg_ref[...] = part

        @pl.when(jnp.logical_not(first))
        def _():
            gg_ref[...] += part

        dxh = dy * g_ref[...]
        lo = _lo_mask(dxh.shape)
        pr = dxh * xhat
        m0 = jnp.sum(jnp.where(lo, pr, 0.0), axis=-1, keepdims=True)
        m1 = jnp.sum(jnp.where(lo, 0.0, pr), axis=-1, keepdims=True)
        mean = jnp.where(lo, m0, m1) / HEAD_DIM
        dx_ref[...] = (r * (dxh - xhat * mean)).astype(BF16)

    def body(dq_ref, dk_ref, q_ref, k_ref, gq_ref, gk_ref, dxq_ref, dxk_ref, ggq_ref, ggk_ref):
        first = pl.program_id(1) == 0
        one(dq_ref, q_ref, gq_ref, QK_SCALE, dxq_ref, ggq_ref, first)
        one(dk_ref, k_ref, gk_ref, 1.0, dxk_ref, ggk_ref, first)

    vec = pl.BlockSpec((1, LANES), lambda p, i: (0, 0))
    own = pl.BlockSpec((tm, LANES), lambda p, i: (i, p))
    ggs = pl.BlockSpec((None, 1, LANES), lambda p, i: (p, 0, 0))
    act = jax.ShapeDtypeStruct((s, WIDTH), BF16)
    ggo = jax.ShapeDtypeStruct((N_PAIR, 1, LANES), F32)
    dxq, dxk, ggq, ggk = pl.pallas_call(
        body, name="fox_post", grid=(N_PAIR, s // tm),
        in_specs=[own, own, pl.BlockSpec((tm, LANES), lambda p, i: (i, off + p)),
                  pl.BlockSpec((tm, LANES), lambda p, i: (i, off + N_PAIR + p)), vec, vec],
        out_specs=[own, own, ggs, ggs], out_shape=[act, act, ggo, ggo],
        compiler_params=_params(("parallel", "arbitrary")),
    )(dqs, dkn, proj, proj, jnp.tile(gq, 2).reshape(1, LANES), jnp.tile(gk, 2).reshape(1, LANES))
    fold = lambda a: a.reshape(N_HEADS, HEAD_DIM).sum(axis=0)
    return dxq, dxk, fold(ggq), fold(ggk)


def _split3(x):
    a = x.astype(BF16)
    r = x - a.astype(F32)
    b = r.astype(BF16)
    c = (r - b.astype(F32)).astype(BF16)
    return a, b, c


def _split2(x):
    a = x.astype(BF16)
    b = (x - a.astype(F32)).astype(BF16)
    return a, b


def _log_sigmoid(x):
    return jnp.minimum(x, 0.0) - jnp.log(1.0 + jnp.exp(-jnp.abs(x)))


def _fox_gates(ft, bias):
    h, s = ft.shape
    nb = s // LANES

    def body(f_ref, b_ref, c_ref):
        lf = _log_sigmoid(f_ref[...] + b_ref[...])
        i = lax.broadcasted_iota(jnp.int32, (s, LANES), 0)
        j = pl.program_id(0) * LANES + lax.broadcasted_iota(jnp.int32, (s, LANES), 1)
        tri = jnp.where(i <= j, 1.0, 0.0).astype(BF16)
        c_ref[...] = sum(_dot(p, tri, _NN) for p in _split3(lf))

    return pl.pallas_call(
        body, name="fox_gates", grid=(nb,),
        in_specs=[pl.BlockSpec((h, s), lambda j: (0, 0)), pl.BlockSpec((h, 1), lambda j: (0, 0))],
        out_specs=pl.BlockSpec((h, LANES), lambda j: (0, j)),
        out_shape=jax.ShapeDtypeStruct((h, s), F32),
        compiler_params=_params(("parallel",)),
    )(ft, bias.reshape(h, 1))


def _fox_gates_bwd(dc, ft, bias):
    h, s = ft.shape
    nb = s // LANES

    def body(dc_ref, f_ref, fb_ref, b_ref, df_ref, gb_ref):
        i = lax.broadcasted_iota(jnp.int32, (s, LANES), 0)
        j = pl.program_id(0) * LANES + lax.broadcasted_iota(jnp.int32, (s, LANES), 1)
        tri = jnp.where(i >= j, 1.0, 0.0).astype(BF16)
        dlf = sum(_dot(p, tri, _NN) for p in _split3(dc_ref[...]))
        df = dlf * jax.nn.sigmoid(-(fb_ref[...] + b_ref[...]))
        df_ref[...] = df
        part = jnp.sum(df, axis=-1, keepdims=True)

        @pl.when(pl.program_id(0) == 0)
        def _():
            gb_ref[...] = part

        @pl.when(pl.program_id(0) > 0)
        def _():
            gb_ref[...] += part

    full = pl.BlockSpec((h, s), lambda j: (0, 0))
    blk = pl.BlockSpec((h, LANES), lambda j: (0, j))
    one = pl.BlockSpec((h, 1), lambda j: (0, 0))
    df, gb = pl.pallas_call(
        body, name="fox_gates_bwd", grid=(nb,), in_specs=[full, full, blk, one], out_specs=[blk, one],
        out_shape=[jax.ShapeDtypeStruct((h, s), F32), jax.ShapeDtypeStruct((h, 1), F32)],
        compiler_params=_params(("arbitrary",)),
    )(dc, ft, ft, bias.reshape(h, 1))
    return df, gb.reshape(h)


def _delta_rep(do, o, tm=2048):
    s = do.shape[0]
    tm = min(tm, s)

    def body(do_ref, o_ref, d_ref):
        pr = do_ref[...].astype(F32) * o_ref[...].astype(F32)
        lo = _lo_mask(pr.shape)
        d0 = jnp.sum(jnp.where(lo, pr, 0.0), axis=-1, keepdims=True)
        d1 = jnp.sum(jnp.where(lo, 0.0, pr), axis=-1, keepdims=True)
        d_ref[...] = jnp.where(lo, d0, d1)

    own = pl.BlockSpec((tm, LANES), lambda p, i: (i, p))
    return pl.pallas_call(
        body, name="delta_rep", grid=(N_PAIR, s // tm), in_specs=[own, own],
        out_specs=pl.BlockSpec((None, tm, LANES), lambda p, i: (p, i, 0)),
        out_shape=jax.ShapeDtypeStruct((N_PAIR, s, LANES), F32),
        compiler_params=_params(("parallel", "parallel")),
    )(do, o)


def _tile(s, t):
    t = min(t, s)
    assert s % t == 0
    return t


def _fox_fwd(qn, kn, proj, ccol, crow, t=512, comm=None):
    s = qn.shape[0]
    t = _tile(s, t)
    n = s // t
    qtab, ktab, ntri = _tri_tables(n, 1, "k_asc")
    voff = FOX_OFF // LANES + 2 * N_PAIR
    n_in, n_out = (len(comm.inputs), len(comm.out_shapes)) if comm else (0, 0)

    def body(qt_ref, kt_ref, q_ref, k_ref, v_ref, cc_ref, cr_ref, *rest):
        comm_in, rest = rest[:n_in], rest[n_in:]
        (o_ref, lse_ref), rest = rest[:2], rest[2:]
        comm_out, rest = rest[:n_out], rest[n_out:]
        (m_ref, l_ref, acc_ref), sems = rest[:3], rest[3:]
        i = pl.program_id(1)
        qi, kj = qt_ref[i], kt_ref[i]
        if comm:
            pl.when((pl.program_id(0) == 0) & (i == 0))(lambda: comm.start(comm_in, comm_out, sems))

        @pl.when(kj == 0)
        def _():
            m_ref[...] = jnp.full(m_ref.shape, NEG_INF, F32)
            l_ref[...] = jnp.zeros(l_ref.shape, F32)
            acc_ref[...] = jnp.zeros(acc_ref.shape, F32)

        q, k, v = q_ref[...], k_ref[...], v_ref[...]
        row, col = _positions(qi, kj, t, t)
        causal = col <= row
        m_old = m_ref[...]
        mn, rs, pv = [], [], []
        for h in range(2):
            sc = _dot(_head(q, h), k, _NT) + _pair_cols(cc_ref[...], h) - cr_ref[h:h + 1, :]
            sc = jnp.where(causal, sc, NEG_INF)
            m_new = jnp.maximum(_pair_cols(m_old, h), jnp.max(sc, axis=-1, keepdims=True))
            p = jnp.exp(sc - m_new)
            mn.append(m_new)
            rs.append(jnp.sum(p, axis=-1, keepdims=True))
            pv.append(_dot(p.astype(BF16), _head(v, h), _NN))
        m_rep = _rep(mn[0], mn[1], m_old.shape)
        alpha = jnp.exp(m_old - m_rep)
        l_ref[...] = alpha * l_ref[...] + _rep(rs[0], rs[1], m_old.shape)
        acc_ref[...] = alpha * acc_ref[...] + pv[0] + pv[1]
        m_ref[...] = m_rep

        @pl.when(kj == qi)
        def _():
            o_ref[...] = acc_ref[...] / l_ref[...]
            lse_ref[...] = m_ref[...] + jnp.log(l_ref[...])

        if comm:
            pl.when((pl.program_id(0) == N_PAIR - 1) & (i == ntri - 1))(lambda: comm.finish(comm_in, comm_out, sems))

    grid_spec = pltpu.PrefetchScalarGridSpec(
        num_scalar_prefetch=2, grid=(N_PAIR, ntri),
        in_specs=[pl.BlockSpec((t, LANES), lambda p, i, qt, kt: (qt[i], p)),
                  pl.BlockSpec((t, LANES), lambda p, i, qt, kt: (kt[i], p)),
                  pl.BlockSpec((t, LANES), lambda p, i, qt, kt: (kt[i], voff + p)),
                  pl.BlockSpec((None, t, LANES), lambda p, i, qt, kt: (p, qt[i], 0)),
                  pl.BlockSpec((None, 2, t), lambda p, i, qt, kt: (p, 0, kt[i]))] + [ANY] * n_in,
        out_specs=[pl.BlockSpec((t, LANES), lambda p, i, qt, kt: (qt[i], p)),
                   pl.BlockSpec((None, t, LANES), lambda p, i, qt, kt: (p, qt[i], 0))] + [ANY] * n_out,
        scratch_shapes=[pltpu.VMEM((t, LANES), F32)] * 3 + (comm.sems if comm else []))
    outs = pl.pallas_call(
        body, name="fox_fwd", grid_spec=grid_spec,
        out_shape=[jax.ShapeDtypeStruct((s, WIDTH), F32), jax.ShapeDtypeStruct((N_PAIR, s, LANES), F32)]
        + (comm.out_shapes if comm else []),
        compiler_params=_params(("arbitrary", "arbitrary") if comm else ("parallel", "arbitrary")),
    )(qtab, ktab, qn, kn, proj, ccol, crow, *(comm.inputs if comm else []))
    return outs[:2], outs[2:]


def _fox_bwd(qn, kn, proj, do, lse, delta, ccol, crow, t=512, comm=None):
    s = qn.shape[0]
    t = _tile(s, t)
    n = s // t
    qtab, ktab, ntri = _tri_tables(n, 1, "k_outer")
    voff = FOX_OFF // LANES + 2 * N_PAIR
    n_in, n_out = (len(comm.inputs), len(comm.out_shapes)) if comm else (0, 0)

    def body(qt_ref, kt_ref, q_ref, k_ref, v_ref, do_ref, lse_ref, dl_ref, cc_ref, cr_ref, *rest):
        comm_in, rest = rest[:n_in], rest[n_in:]
        (dq_ref, dk_ref, dv_ref, dc_ref, dcq_ref), rest = rest[:5], rest[5:]
        comm_out, rest = rest[:n_out], rest[n_out:]
        (dka_ref, dva_ref, dca_ref, dcqa_ref), sems = rest[:4], rest[4:]
        i = pl.program_id(1)
        qi, kj = qt_ref[i], kt_ref[i]
        if comm:
            pl.when((pl.program_id(0) == 0) & (i == 0))(lambda: comm.start(comm_in, comm_out, sems))

        @pl.when(i == 0)
        def _():
            dq_ref[...] = jnp.zeros(dq_ref.shape, F32)
            dcqa_ref[...] = jnp.zeros(dcqa_ref.shape, F32)

        @pl.when(qi == kj)
        def _():
            dka_ref[...] = jnp.zeros(dka_ref.shape, F32)
            dva_ref[...] = jnp.zeros(dva_ref.shape, F32)
            dca_ref[...] = jnp.zeros(dca_ref.shape, F32)

        q, k, v, g = q_ref[...], k_ref[...], v_ref[...], do_ref[...]
        row, col = _positions(qi, kj, t, t)
        causal = col <= row
        dq = jnp.zeros((t, LANES), F32)
        dk = jnp.zeros((t, LANES), F32)
        dv = jnp.zeros((t, LANES), F32)
        rowsum = []
        for h in range(2):
            qh, gh = _head(q, h), _head(g, h)
            sc = _dot(qh, k, _NT) + _pair_cols(cc_ref[...], h) - cr_ref[h:h + 1, :]
            p = jnp.where(causal, jnp.exp(sc - _pair_cols(lse_ref[...], h)), 0.0)
            dp = _dot(gh, v, _NT)
            ds = p * (dp - _pair_cols(dl_ref[...], h))
            dsb = ds.astype(BF16)
            dv = dv + _dot(p.astype(BF16), gh, _TN)
            dk = dk + _dot(dsb, qh, _TN)
            dq = dq + _dot(dsb, _head(k, h), _NN)
            dca_ref[h:h + 1, :] -= jnp.sum(ds, axis=0, keepdims=True)
            rowsum.append(jnp.sum(ds, axis=-1, keepdims=True))
        dka_ref[...] += dk
        dva_ref[...] += dv
        rows = pl.ds(pl.multiple_of(qi * t, t), t)
        dq_ref[rows, :] += dq
        dcqa_ref[rows, :] += _rep(rowsum[0], rowsum[1], (t, LANES))

        @pl.when(qi == n - 1)
        def _():
            dk_ref[...] = dka_ref[...]
            dv_ref[...] = dva_ref[...].astype(BF16)
            dc_ref[...] = dca_ref[...]

        @pl.when(i == ntri - 1)
        def _():
            across = dcqa_ref[...].T
            dcq_ref[0:1, :] = across[0:1, :]
            dcq_ref[1:2, :] = across[HEAD_DIM:HEAD_DIM + 1, :]

        if comm:
            pl.when((pl.program_id(0) == N_PAIR - 1) & (i == ntri - 1))(lambda: comm.finish(comm_in, comm_out, sems))

    qblk = lambda p, i, qt, kt: (qt[i], p)
    kblk = lambda p, i, qt, kt: (kt[i], p)
    qrep = pl.BlockSpec((None, t, LANES), lambda p, i, qt, kt: (p, qt[i], 0))
    crs = pl.BlockSpec((None, 2, t), lambda p, i, qt, kt: (p, 0, kt[i]))
    grid_spec = pltpu.PrefetchScalarGridSpec(
        num_scalar_prefetch=2, grid=(N_PAIR, ntri),
        in_specs=[pl.BlockSpec((t, LANES), qblk), pl.BlockSpec((t, LANES), kblk),
                  pl.BlockSpec((t, LANES), lambda p, i, qt, kt: (kt[i], voff + p)),
                  pl.BlockSpec((t, LANES), qblk), qrep, qrep, qrep, crs] + [ANY] * n_in,
        out_specs=[pl.BlockSpec((s, LANES), lambda p, i, qt, kt: (0, p)),
                   pl.BlockSpec((t, LANES), kblk), pl.BlockSpec((t, LANES), kblk), crs,
                   pl.BlockSpec((None, 2, s), lambda p, i, qt, kt: (p, 0, 0))] + [ANY] * n_out,
        scratch_shapes=[pltpu.VMEM((t, LANES), F32), pltpu.VMEM((t, LANES), F32), pltpu.VMEM((2, t), F32),
                        pltpu.VMEM((s, LANES), F32)] + (comm.sems if comm else []))
    outs = pl.pallas_call(
        body, name="fox_bwd", grid_spec=grid_spec,
        out_shape=[jax.ShapeDtypeStruct((s, WIDTH), F32), jax.ShapeDtypeStruct((s, WIDTH), F32),
                   jax.ShapeDtypeStruct((s, WIDTH), BF16), jax.ShapeDtypeStruct((N_PAIR, 2, s), F32),
                   jax.ShapeDtypeStruct((N_PAIR, 2, s), F32)] + (comm.out_shapes if comm else []),
        compiler_params=_params(("arbitrary", "arbitrary") if comm else ("parallel", "arbitrary")),
    )(qtab, ktab, qn, kn, proj, do, lse, delta, ccol, crow, *(comm.inputs if comm else []))
    return outs[:5], outs[5:]


def _sb_tile(qh, k, strict, tk, r_col):
    z = _dot(qh, k, _NT)
    lg = jnp.where(strict, -(jnp.maximum(z, 0.0) + jnp.log(1.0 + jnp.exp(-jnp.abs(z)))), 0.0)
    jj = lax.broadcasted_iota(jnp.int32, (tk, tk), 0)
    ss = lax.broadcasted_iota(jnp.int32, (tk, tk), 1)
    above = jnp.where(jj > ss, 1.0, 0.0).astype(BF16)
    suffix = sum(_dot(p, above, _NN) for p in _split2(lg)) + r_col
    a = jnp.where(strict, jnp.exp(lg + z + suffix), 0.0)
    return z, lg, a


def _sb_fwd(proj, tq=512, tk=256, comm=None):
    s = proj.shape[0]
    tq = _tile(s, tq)
    tk = _tile(tq, tk)
    nq, r = s // tq, tq // tk
    qtab, ktab, ntri = _tri_tables(nq, r, "k_desc")
    off = SB_OFF // LANES
    n_in, n_out = (len(comm.inputs), len(comm.out_shapes)) if comm else (0, 0)

    def body(qt_ref, kt_ref, q_ref, k_ref, v_ref, *rest):
        comm_in, o_ref, rest = rest[:n_in], rest[n_in], rest[n_in + 1:]
        comm_out, rest = rest[:n_out], rest[n_out:]
        (acc_ref, r_ref), sems = rest[:2], rest[2:]
        i = pl.program_id(1)
        qi, kj = qt_ref[i], kt_ref[i]
        if comm:
            pl.when((pl.program_id(0) == 0) & (i == 0))(lambda: comm.start(comm_in, comm_out, sems))

        @pl.when(kj == (qi + 1) * r - 1)
        def _():
            acc_ref[...] = jnp.zeros(acc_ref.shape, F32)
            r_ref[...] = jnp.zeros(r_ref.shape, F32)

        q = q_ref[...] * QK_SCALE
        k, v = k_ref[...], v_ref[...]
        row, col = _positions(qi, kj, tq, tk)
        strict = col < row
        acc = acc_ref[...]
        for h in range(2):
            _, lg, a = _sb_tile(_head(q, h), k, strict, tk, r_ref[h])
            acc = acc + _dot(a.astype(BF16), _head(v, h), _NN)
            r_ref[h] += jnp.sum(lg, axis=-1, keepdims=True)
        acc_ref[...] = acc

        @pl.when(kj == 0)
        def _():
            o_ref[...] = acc_ref[...]

        if comm:
            pl.when((pl.program_id(0) == N_PAIR - 1) & (i == ntri - 1))(lambda: comm.finish(comm_in, comm_out, sems))

    grid_spec = pltpu.PrefetchScalarGridSpec(
        num_scalar_prefetch=2, grid=(N_PAIR, ntri),
        in_specs=[pl.BlockSpec((tq, LANES), lambda p, i, qt, kt: (qt[i], off + p)),
                  pl.BlockSpec((tk, LANES), lambda p, i, qt, kt: (kt[i], off + N_PAIR + p)),
                  pl.BlockSpec((tk, LANES), lambda p, i, qt, kt: (kt[i], off + 2 * N_PAIR + p))] + [ANY] * n_in,
        out_specs=[pl.BlockSpec((tq, LANES), lambda p, i, qt, kt: (qt[i], p))] + [ANY] * n_out,
        scratch_shapes=[pltpu.VMEM((tq, LANES), F32), pltpu.VMEM((2, tq, 1), F32)] + (comm.sems if comm else []))
    outs = pl.pallas_call(
        body, name="sb_fwd", grid_spec=grid_spec,
        out_shape=[jax.ShapeDtypeStruct((s, WIDTH), F32)] + (comm.out_shapes if comm else []),
        compiler_params=_params(("arbitrary", "arbitrary") if comm else ("parallel", "arbitrary")),
    )(qtab, ktab, proj, proj, proj, *(comm.inputs if comm else []))
    return outs[0], outs[1:]


def _sb_bwd(proj, do, delta, tq=512, tk=256, comm=None):
    s = proj.shape[0]
    tq = _tile(s, tq)
    tk = _tile(tq, tk)
    nq, r = s // tq, tq // tk
    qtab, ktab, ntri = _tri_tables(nq, r, "k_desc")
    off = SB_OFF // LANES
    n_in, n_out = (len(comm.inputs), len(comm.out_shapes)) if comm else (0, 0)

    def body(qt_ref, kt_ref, q_ref, k_ref, v_ref, do_ref, dl_ref, *rest):
        comm_in, rest = rest[:n_in], rest[n_in:]
        (dq_ref, dk_ref, dv_ref), rest = rest[:3], rest[3:]
        comm_out, rest = rest[:n_out], rest[n_out:]
        (dqa_ref, dka_ref, dva_ref, r_ref, rd_ref), sems = rest[:5], rest[5:]
        i = pl.program_id(1)
        qi, kj = qt_ref[i], kt_ref[i]
        if comm:
            pl.when((pl.program_id(0) == 0) & (i == 0))(lambda: comm.start(comm_in, comm_out, sems))

        @pl.when(i == 0)
        def _():
            dka_ref[...] = jnp.zeros(dka_ref.shape, F32)
            dva_ref[...] = jnp.zeros(dva_ref.shape, F32)

        @pl.when(kj == (qi + 1) * r - 1)
        def _():
            dqa_ref[...] = jnp.zeros(dqa_ref.shape, F32)
            r_ref[...] = jnp.zeros(r_ref.shape, F32)
            rd_ref[...] = jnp.zeros(rd_ref.shape, F32)

        q = q_ref[...] * QK_SCALE
        k, v, g = k_ref[...], v_ref[...], do_ref[...]
        row, col = _positions(qi, kj, tq, tk)
        strict = col < row
        ss = lax.broadcasted_iota(jnp.int32, (tk, tk), 0)
        jj = lax.broadcasted_iota(jnp.int32, (tk, tk), 1)
        at_or_after = jnp.where(ss >= jj, 1.0, 0.0).astype(BF16)
        dq = jnp.zeros((tq, LANES), F32)
        dk = jnp.zeros((tk, LANES), F32)
        dv = jnp.zeros((tk, LANES), F32)
        for h in range(2):
            qh, gh = _head(q, h), _head(g, h)
            z, lg, a = _sb_tile(qh, k, strict, tk, r_ref[h])
            da = _dot(gh, v, _NT)
            ab = a.astype(BF16)
            d_a = da * ab.astype(F32)
            incl = _dot(d_a.astype(BF16), at_or_after, _NN)
            d_l = _pair_cols(dl_ref[...], h) - rd_ref[h] - incl
            sig = jnp.exp(lg + z)
            dz = jnp.where(strict, d_a * (1.0 - sig) - d_l * sig, 0.0).astype(BF16)
            dq = dq + _dot(dz, _head(k, h), _NN)
            dk = dk + _dot(dz, qh, _TN)
            dv = dv + _dot(ab, gh, _TN)
            r_ref[h] += jnp.sum(lg, axis=-1, keepdims=True)
            rd_ref[h] += jnp.sum(d_a, axis=-1, keepdims=True)
        dqa_ref[...] += dq
        rows = pl.ds(pl.multiple_of(kj * tk, tk), tk)
        dka_ref[rows, :] += dk
        dva_ref[rows, :] += dv

        @pl.when(kj == 0)
        def _():
            dq_ref[...] = (dqa_ref[...] * QK_SCALE).astype(BF16)

        @pl.when(i == ntri - 1)
        def _():
            dk_ref[...] = dka_ref[...].astype(BF16)
            dv_ref[...] = dva_ref[...].astype(BF16)

        if comm:
            pl.when((pl.program_id(0) == N_PAIR - 1) & (i == ntri - 1))(lambda: comm.finish(comm_in, comm_out, sems))

    qblk = lambda p, i, qt, kt: (qt[i], p)
    whole = pl.BlockSpec((s, LANES), lambda p, i, qt, kt: (0, p))
    grid_spec = pltpu.PrefetchScalarGridSpec(
        num_scalar_prefetch=2, grid=(N_PAIR, ntri),
        in_specs=[pl.BlockSpec((tq, LANES), lambda p, i, qt, kt: (qt[i], off + p)),
                  pl.BlockSpec((tk, LANES), lambda p, i, qt, kt: (kt[i], off + N_PAIR + p)),
                  pl.BlockSpec((tk, LANES), lambda p, i, qt, kt: (kt[i], off + 2 * N_PAIR + p)),
                  pl.BlockSpec((tq, LANES), qblk),
                  pl.BlockSpec((None, tq, LANES), lambda p, i, qt, kt: (p, qt[i], 0))] + [ANY] * n_in,
        out_specs=[pl.BlockSpec((tq, LANES), qblk), whole, whole] + [ANY] * n_out,
        scratch_shapes=[pltpu.VMEM((tq, LANES), F32), pltpu.VMEM((s, LANES), F32), pltpu.VMEM((s, LANES), F32),
                        pltpu.VMEM((2, tq, 1), F32), pltpu.VMEM((2, tq, 1), F32)] + (comm.sems if comm else []))
    act = jax.ShapeDtypeStruct((s, WIDTH), BF16)
    outs = pl.pallas_call(
        body, name="sb_bwd", grid_spec=grid_spec, out_shape=[act, act, act] + (comm.out_shapes if comm else []),
        compiler_params=_params(("arbitrary", "arbitrary") if comm else ("parallel", "arbitrary")),
    )(qtab, ktab, proj, proj, proj, do, delta, *(comm.inputs if comm else []))
    return outs[:3], outs[3:]


def _loss_head(y, target, tm=512):
    s, d = y.shape
    tm = min(tm, s)

    def body(y_ref, t_ref, l_ref, dy_ref):
        e = y_ref[...] - t_ref[...]
        dy_ref[...] = e / d
        part = jnp.sum(jnp.sum(e * e, axis=0, keepdims=True), axis=1, keepdims=True)

        @pl.when(pl.program_id(0) == 0)
        def _():
            l_ref[...] = jnp.broadcast_to(part, l_ref.shape)

        @pl.when(pl.program_id(0) > 0)
        def _():
            l_ref[...] += jnp.broadcast_to(part, l_ref.shape)

    row = pl.BlockSpec((tm, d), lambda i: (i, 0))
    return pl.pallas_call(
        body, name="loss_head", grid=(s // tm,), in_specs=[row, row],
        out_specs=[pl.BlockSpec((1, LANES), lambda i: (0, 0)), row],
        out_shape=[jax.ShapeDtypeStruct((1, LANES), F32), jax.ShapeDtypeStruct((s, d), F32)],
        compiler_params=_params(("arbitrary",)),
    )(y, target)


def _adamw(w, g, m, v, name):
    shape = w.shape
    cols = shape[-1]
    rows = int(np.prod(shape[:-1]))
    tm = rows
    for cand in (512, 256, 128, 64, 32, 16, 8):
        if rows % cand == 0 and rows > cand and cand * cols * 4 <= ADAMW_BLOCK_BYTES:
            tm = cand
            break

    def body(w_ref, g_ref, m_ref, v_ref, d_ref, mo_ref, vo_ref):
        gr = g_ref[...]
        mn = ADAM_B1 * m_ref[...] + (1.0 - ADAM_B1) * gr
        vn = ADAM_B2 * v_ref[...] + (1.0 - ADAM_B2) * (gr * gr)
        m_hat = mn / (1.0 - ADAM_B1 ** ADAM_STEP)
        v_hat = vn / (1.0 - ADAM_B2 ** ADAM_STEP)
        d_ref[...] = -ADAM_LR * (m_hat / (jnp.sqrt(v_hat) + ADAM_EPS) + ADAM_WD * w_ref[...])
        mo_ref[...] = mn
        vo_ref[...] = vn

    blk = pl.BlockSpec((tm, cols), lambda i: (i, 0))
    out = jax.ShapeDtypeStruct((rows, cols), F32)
    r2 = lambda a: a.reshape(rows, cols)
    outs = pl.pallas_call(
        body, name=name, grid=(rows // tm,), in_specs=[blk] * 4, out_specs=[blk] * 3, out_shape=[out] * 3,
        compiler_params=_params(("parallel",)),
    )(r2(w), r2(g), r2(m), r2(v))
    return tuple(o.reshape(shape) for o in outs)


W_IN_NATURAL = (("a", CONV_OFF, 3 * WIDTH), ("a", FOX_OFF, 3 * WIDTH), ("f", 0, N_HEADS), ("a", SB_OFF, 3 * WIDTH),
                ("a", GATE_OFF, 3 * D_MODEL))
W_IN_COLS = D_INA + N_HEADS
W_IN_SHARD = W_IN_COLS // 4


def _w_in_from_chips(blocks):
    def natural(lo, hi):
        cols = []
        for k, blk in enumerate(blocks):
            a, b = max(lo, k * W_IN_SHARD), min(hi, (k + 1) * W_IN_SHARD)
            if a < b:
                cols.append(blk[:, a - k * W_IN_SHARD:b - k * W_IN_SHARD])
        return cols

    start = {}
    pos = 0
    for part, at, length in W_IN_NATURAL:
        start[part, at] = (pos, pos + length)
        pos += length
    order = sorted((at, rng) for (part, at), rng in start.items() if part == "a")
    wa = jnp.concatenate([c for _, (lo, hi) in order for c in natural(lo, hi)], axis=1)
    wf = jnp.pad(jnp.concatenate(natural(*start["f", 0]), axis=1), ((0, 0), (0, LANES - N_HEADS)))
    return wa, wf


def _w_in_to_chips(ga, gf):
    chips = []
    for k in range(4):
        lo, hi, pos, cols = k * W_IN_SHARD, (k + 1) * W_IN_SHARD, 0, []
        for part, at, length in W_IN_NATURAL:
            a, b = max(lo, pos), min(hi, pos + length)
            if a < b:
                cols.append((ga if part == "a" else gf)[:, at + a - pos:at + b - pos])
            pos += length
        chips.append(jnp.concatenate(cols, axis=1))
    return jnp.stack(chips)


def _layer_fwd(x, p, plan, l):
    s = x.shape[0]
    proj, hn1 = _norm_mm(x, p["norm1_g"], p["wa"], BF16, name="in_proj")
    fraw = _mm(hn1, p["wf"], "nn", F32, name="in_proj_f")
    ft = fraw[:, :N_HEADS].T
    crow8 = _fox_gates(ft, p["fox_f_bias"])
    crow = crow8.reshape(N_PAIR, 2, s)
    ccol = jnp.repeat(crow.transpose(0, 2, 1), HEAD_DIM, axis=2)
    oc = _conv_mix_fwd(proj, p["conv_w"])
    qn, kn = _fox_prep(proj, p["fox_q_norm_g"], p["fox_k_norm_g"])
    last = l + 1 == plan.depth
    (of, lse), got_in = _fox_fwd(qn, kn, proj, ccol, crow, comm=None if last else plan.gather(l + 1, BIG_IN))
    osb, got_rest = _sb_fwd(proj, comm=plan.gather(l, BIG_REST))
    p.update(plan.weights(l, BIG_REST, got_rest))
    nxt = None if last else plan.layer(l + 1, got_in)
    xm, merged, yc, yf, ys = _merge_fwd(x, oc, of, osb, p["w_proj_conv"], p["w_proj_fox"], p["w_proj_sb"], proj,
                                        p["gate_bias"], p["w_out"])
    up, hn2 = _norm_mm(xm, p["norm2_g"], p["w_up"], BF16, name="up_proj")
    hh = _glu_fwd(up, p["ffn_conv_w"], p["ffn_conv_b"])
    xo = _mm(hh, p["w_down"], "nn", F32, res=xm, name="down_proj")
    saved = dict(x=x, hn1=hn1, proj=proj, ft=ft, crow=crow, ccol=ccol, oc=oc, qn=qn, kn=kn, of=of, lse=lse, osb=osb,
                 merged=merged, yc=yc, yf=yf, ys=ys, xm=xm, hn2=hn2, up=up, hh=hh)
    return xo, saved, nxt


def _layer_bwd(dx, p, a, plan, pending_in, first_layer):
    s = dx.shape[0]
    g = {}
    dhh = _mm(dx, p["w_down"], "nt", BF16, tn=1408, name="d_down_in")
    g["w_down"] = _mm(a["hh"], dx, "tn", F32, name="g_w_down")
    dug, duv, g["ffn_conv_w"], g["ffn_conv_b"] = _glu_bwd(dhh, a["up"], p["ffn_conv_w"], p["ffn_conv_b"])
    dup = jnp.concatenate([dug, duv], axis=1)
    g["w_up"] = _mm(a["hn2"], dup, "tn", F32, tn=1408, name="g_w_up")
    dhn2 = _mm(dup, p["w_up"], "nt", F32, name="d_up_in")
    dx, g["norm2_g"] = _norm_bwd(dhn2, a["xm"], p["norm2_g"], dx, name="norm2_bwd")
    dm = _mm(dx, p["w_out"], "nt", BF16, name="d_out_in")
    g["w_out"] = _mm(a["merged"], dx, "tn", F32, tn=512, name="g_w_out")
    dyc, dyf, dys, dgl, g["gate_bias"] = _gate_bwd(dm, a["yc"], a["yf"], a["ys"], a["proj"], p["gate_bias"])
    doc = _mm(dyc, p["w_proj_conv"], "nt", BF16, name="d_pc_in")
    dof = _mm(dyf, p["w_proj_fox"], "nt", BF16, name="d_pf_in")
    dos = _mm(dys, p["w_proj_sb"], "nt", BF16, name="d_ps_in")
    g["w_proj_conv"] = _mm(a["oc"], dyc, "tn", F32, name="g_w_pc")
    g["w_proj_fox"] = _mm(a["of"], dyf, "tn", F32, name="g_w_pf")
    g["w_proj_sb"] = _mm(a["osb"], dys, "tn", F32, name="g_w_ps")
    dcb, dcc, dch, g["conv_w"] = _conv_mix_bwd(doc, a["proj"], p["conv_w"])
    delta_f = _delta_rep(dof, a["of"])
    pending = plan.reduction(pending_in, {k: g[k] for k in BIG_REST})
    (dqs, dkn, dfv, dcrow, dcq), recv_a = _fox_bwd(a["qn"], a["kn"], a["proj"], dof, a["lse"], delta_f, a["ccol"], a["crow"],
                                                   comm=pending.exchange() if pending else None)
    dc = (dcrow + dcq).reshape(N_HEADS, s)
    dfq, dfk, g["fox_q_norm_g"], g["fox_k_norm_g"] = _fox_post(dqs, dkn, a["proj"], p["fox_q_norm_g"], p["fox_k_norm_g"])
    dft, g["fox_f_bias"] = _fox_gates_bwd(dc, a["ft"], p["fox_f_bias"])
    delta_s = _delta_rep(dos, a["osb"])
    (dsq, dsk, dsv), recv_b = _sb_bwd(a["proj"], dos, delta_s, comm=pending.sums(recv_a) if pending else None)
    done_in = None
    if pending:
        done = pending.finish(recv_b)
        if pending_in is not None:
            done_in, done = done[0], done[1:]
        g.update(zip(BIG_REST, done))
    dproj = jnp.concatenate([dgl, dcb, dcc, dch, dfq, dfk, dfv, dsq, dsk, dsv], axis=1)
    dfp = jnp.pad(dft.T, ((0, 0), (0, LANES - N_HEADS))).astype(BF16)
    ga = _mm(a["hn1"], dproj, "tn", F32, tn=768, name="g_w_in")
    gf = _mm(a["hn1"], dfp, "tn", F32, name="g_w_in_f")
    g["w_in"] = _w_in_to_chips(ga, gf)
    dhn1 = _mm(dfp, p["wf"], "nt", F32, name="d_in_f")
    own = plan.reduction(g["w_in"], {}) if first_layer else None
    if own:
        second = own.sums(_run_comm(own.exchange(), "rs_pair_exchange"))
        dhn1, recv_b = _mm(dproj, p["wa"], "nt", F32, res=dhn1, tk=1536, name="d_in", comm=second)
        (g["w_in"],) = own.finish(recv_b)
    else:
        dhn1 = _mm(dproj, p["wa"], "nt", F32, res=dhn1, tk=1536, name="d_in")
    dx, g["norm1_g"] = _norm_bwd(dhn1, a["x"], p["norm1_g"], dx, name="norm1_bwd")
    return dx, g, done_in


MATMUL_WEIGHTS = ("w_in", "w_proj_conv", "w_proj_fox", "w_proj_sb", "w_out", "w_up", "w_down")
WEIGHTS = ("norm1_g", "w_in", "fox_f_bias", "gate_bias", "conv_w", "fox_q_norm_g", "fox_k_norm_g", "w_proj_conv",
           "w_proj_fox", "w_proj_sb", "w_out", "norm2_g", "w_up", "ffn_conv_w", "ffn_conv_b", "w_down")


def _local_step(x, target, plan):
    depth = plan.depth
    layers, saved = [plan.layer(0, None)], []
    for l in range(depth):
        x, a, nxt = _layer_fwd(x, layers[l], plan, l)
        saved.append(a)
        if nxt is not None:
            layers.append(nxt)
    sq, dx = _loss_head(x, target)
    grads, pending_in = [None] * depth, None
    for l in reversed(range(depth)):
        dx, grads[l], done_in = _layer_bwd(dx, layers[l], saved[l], plan, pending_in, l == 0)
        if done_in is not None:
            grads[l + 1]["w_in"] = done_in
        pending_in = grads[l]["w_in"]
    return sq, dx, grads


ANY = pl.BlockSpec(memory_space=pl.ANY)


def _place():
    x, y, c = lax.axis_index("x"), lax.axis_index("y"), lax.axis_index("c")
    chips = [(1 - x, y), (x, 1 - y), (1 - x, 1 - y)]
    return x, y, c, chips


_Comm = collections.namedtuple("_Comm", "inputs out_shapes sems start finish")


def _run_comm(comm, name):
    n_in, n_out = len(comm.inputs), len(comm.out_shapes)

    def body(*refs):
        ins, outs, sems = refs[:n_in], refs[n_in:n_in + n_out], refs[n_in + n_out:]
        comm.start(ins, outs, sems)
        comm.finish(ins, outs, sems)

    return pl.pallas_call(body, name=name, in_specs=[ANY] * n_in, out_specs=[ANY] * n_out, out_shape=comm.out_shapes,
                          scratch_shapes=comm.sems)(*comm.inputs)


def _gather_comm(shards):
    n = len(shards)

    def copy(x_refs, out_refs, sems, k, t, chip_index, which_half, to, from_input=False):
        half = shards[t].shape[0] // 2
        rows = pl.ds(which_half * half, half)
        dst = out_refs[t].at[chip_index, rows, :]
        return pltpu.make_async_remote_copy(
            src_ref=x_refs[t].at[rows, :] if from_input else dst, dst_ref=dst,
            send_sem=sems[0].at[k, t], recv_sem=sems[1].at[k, t], device_id=to, device_id_type=MESH)

    def first(x_refs, out_refs, sems):
        x, y, c, chips = _place()
        return [copy(x_refs, out_refs, sems, p, t, 2 * x + y, c, (*chip, c), from_input=True)
                for t in range(n) for p, chip in enumerate(chips)]

    def start(x_refs, out_refs, sems):
        for cp in first(x_refs, out_refs, sems):
            cp.start()

    def finish(x_refs, out_refs, sems):
        x, y, c, chips = _place()
        passed = []
        for t in range(n):
            for p, chip in enumerate(chips):
                copy(x_refs, out_refs, sems, p, t, 2 * chip[0] + chip[1], c, (x, y, c)).wait_recv()
                fwd = copy(x_refs, out_refs, sems, 3 + p, t, 2 * chip[0] + chip[1], c, (x, y, 1 - c))
                fwd.start()
                passed.append(fwd)
        for t in range(n):
            for p, chip in enumerate(chips):
                copy(x_refs, out_refs, sems, 3 + p, t, 2 * chip[0] + chip[1], 1 - c, (x, y, c)).wait_recv()
        for cp in first(x_refs, out_refs, sems) + passed:
            cp.wait_send()

    return _Comm(list(shards), [jax.ShapeDtypeStruct((4, *s.shape), s.dtype) for s in shards],
                 [pltpu.SemaphoreType.DMA((6, n)), pltpu.SemaphoreType.DMA((6, n))], start, finish)


def _own_block(got, shards):
    chip = 2 * lax.axis_index("x") + lax.axis_index("y")
    return [lax.dynamic_update_index_in_dim(g, s, chip, 0) for g, s in zip(got, shards)]


def _pair_exchange_comm(gs):
    n = len(gs)

    def copies(g_refs, recv_refs, sems):
        x, y, c, _ = _place()
        return [pltpu.make_async_remote_copy(
            src_ref=g_refs[t].at[:, pl.ds((1 - c) * (gs[t].shape[1] // 2), gs[t].shape[1] // 2), :], dst_ref=recv_refs[t],
            send_sem=sems[0].at[t], recv_sem=sems[1].at[t], device_id=(x, y, 1 - c), device_id_type=MESH) for t in range(n)]

    def start(g_refs, recv_refs, sems):
        for cp in copies(g_refs, recv_refs, sems):
            cp.start()

    def finish(g_refs, recv_refs, sems):
        for cp in copies(g_refs, recv_refs, sems):
            cp.wait()

    return _Comm(list(gs), [jax.ShapeDtypeStruct((4, g.shape[1] // 2, g.shape[2]), g.dtype) for g in gs],
                 [pltpu.SemaphoreType.DMA((n,)), pltpu.SemaphoreType.DMA((n,))], start, finish)


def _pair_sum(g, recv, core, tr=512):
    n, r, cols = g.shape
    half = r // 2
    tr = _row_tile(half, tr)
    nb = half // tr

    def body(c_ref, g_ref, r_ref, o_ref):
        o_ref[...] = (g_ref[...] + r_ref[...]).astype(BF16)

    grid_spec = pltpu.PrefetchScalarGridSpec(
        num_scalar_prefetch=1, grid=(n, nb),
        in_specs=[pl.BlockSpec((None, tr, cols), lambda k, i, c: (k, c[0] * nb + i, 0)),
                  pl.BlockSpec((None, tr, cols), lambda k, i, c: (k, i, 0))],
        out_specs=pl.BlockSpec((None, tr, cols), lambda k, i, c: (k, i, 0)))
    return pl.pallas_call(
        body, name="rs_pair_sum", grid_spec=grid_spec, out_shape=jax.ShapeDtypeStruct((n, half, cols), BF16),
        compiler_params=_params(("parallel", "parallel")),
    )(core, g, recv)


def _row_tile(rows, pref):
    best = None
    for t in range(16, min(rows, pref) + 1, 16):
        if rows % t == 0:
            best = t
    assert best is not None, (rows, pref)
    return best


def _chip_exchange_comm(s1s):
    n = len(s1s)

    def copies(s_refs, recv_refs, sems):
        x, y, c, chips = _place()
        return [pltpu.make_async_remote_copy(
            src_ref=s_refs[t].at[2 * chip[0] + chip[1]], dst_ref=recv_refs[t].at[p],
            send_sem=sems[0].at[p, t], recv_sem=sems[1].at[p, t], device_id=(*chip, c), device_id_type=MESH)
            for t in range(n) for p, chip in enumerate(chips)]

    def start(s_refs, recv_refs, sems):
        for cp in copies(s_refs, recv_refs, sems):
            cp.start()

    def finish(s_refs, recv_refs, sems):
        for cp in copies(s_refs, recv_refs, sems):
            cp.wait()

    return _Comm(list(s1s), [jax.ShapeDtypeStruct((3, *s.shape[1:]), s.dtype) for s in s1s],
                 [pltpu.SemaphoreType.DMA((3, n)), pltpu.SemaphoreType.DMA((3, n))], start, finish)


def _final_sum(g, recv_a, recv_b, core, chip, tr=256):
    n, r, cols = g.shape
    half = r // 2
    tr = _row_tile(half, tr)
    nb = half // tr

    def body(c_ref, k_ref, g_ref, a_ref, b0_ref, b1_ref, b2_ref, o_ref):
        total = g_ref[...] + a_ref[...]
        for b_ref in (b0_ref, b1_ref, b2_ref):
            total = total + b_ref[...].astype(F32)
        o_ref[...] = total

    rel = lambda p: pl.BlockSpec((None, tr, cols), lambda i, c, k, p=p: (p, i, 0))
    grid_spec = pltpu.PrefetchScalarGridSpec(
        num_scalar_prefetch=2, grid=(nb,),
        in_specs=[pl.BlockSpec((None, tr, cols), lambda i, c, k: (k[0], c[0] * nb + i, 0)),
                  pl.BlockSpec((None, tr, cols), lambda i, c, k: (k[0], i, 0)), rel(0), rel(1), rel(2)],
        out_specs=pl.BlockSpec((tr, cols), lambda i, c, k: (c[0] * nb + i, 0)))
    return pl.pallas_call(
        body, name="rs_final_sum", grid_spec=grid_spec, out_shape=jax.ShapeDtypeStruct((r, cols), F32),
        compiler_params=_params(("parallel",)),
    )(core, chip, g, recv_a, recv_b, recv_b, recv_b)


def _pair_join(fs):
    n = len(fs)

    def body(*refs):
        out_refs = refs[n:2 * n]
        send_sems, recv_sems = refs[2 * n:]
        x, y, c, _ = _place()

        def copy(t, which_half):
            h = fs[t].shape[0] // 2
            rows = out_refs[t].at[pl.ds(which_half * h, h), :]
            return pltpu.make_async_remote_copy(
                src_ref=rows, dst_ref=rows, send_sem=send_sems.at[t], recv_sem=recv_sems.at[t],
                device_id=(x, y, 1 - c), device_id_type=MESH)

        sends = [copy(t, c) for t in range(n)]
        for cp in sends:
            cp.start()
        for t in range(n):
            sends[t].wait_send()
            copy(t, 1 - c).wait_recv()

    return pl.pallas_call(
        body, name="rs_pair_join", in_specs=[ANY] * n, out_specs=[ANY] * n,
        out_shape=[jax.ShapeDtypeStruct(f.shape, f.dtype) for f in fs],
        input_output_aliases={t: t for t in range(n)},
        scratch_shapes=[pltpu.SemaphoreType.DMA((n,)), pltpu.SemaphoreType.DMA((n,))],
    )(*fs)


class _Reduction:
    def __init__(self, gs):
        self.gs = gs
        self.core = lax.axis_index("c").astype(jnp.int32).reshape(1)
        self.chip = (2 * lax.axis_index("x") + lax.axis_index("y")).astype(jnp.int32).reshape(1)

    def exchange(self):
        return _pair_exchange_comm(self.gs)

    def sums(self, recv_a):
        self.recv_a = list(recv_a)
        return _chip_exchange_comm([_pair_sum(g, ra, self.core) for g, ra in zip(self.gs, self.recv_a)])

    def finish(self, recv_b):
        return _pair_join([_final_sum(g, ra, rb, self.core, self.chip) for g, ra, rb in zip(self.gs, self.recv_a, recv_b)])

    def run(self):
        recv_a = _run_comm(self.exchange(), "rs_pair_exchange")
        return self.finish(_run_comm(self.sums(recv_a), "rs_chip_exchange"))


def _all_reduce_small(v):
    r, cols = v.shape

    def body(v_ref, out_ref, buf_ref, send_sems, recv_sems):
        x, y, c, _ = _place()
        flip = lambda a, bit: 1 - a if bit else a
        buf_ref[4 * x + 2 * y + c] = v_ref[...]
        cps = []
        for rel in range(1, 8):
            peer = (flip(x, rel & 4), flip(y, rel & 2), flip(c, rel & 1))
            cps.append(pltpu.make_async_remote_copy(
                src_ref=v_ref, dst_ref=buf_ref.at[4 * x + 2 * y + c], send_sem=send_sems.at[rel - 1], recv_sem=recv_sems.at[rel - 1],
                device_id=peer, device_id_type=MESH))
        for cp in cps:
            cp.start()
        for cp in cps:
            cp.wait()
        total = buf_ref[0]
        for d in range(1, 8):
            total = total + buf_ref[d]
        out_ref[...] = total

    vm = pl.BlockSpec(memory_space=pltpu.VMEM)
    return pl.pallas_call(
        body, name="all_reduce_small", in_specs=[vm], out_specs=vm, out_shape=jax.ShapeDtypeStruct((r, cols), F32),
        scratch_shapes=[pltpu.VMEM((8, r, cols), F32), pltpu.SemaphoreType.DMA((7,)), pltpu.SemaphoreType.DMA((7,))],
    )(v)


SHARD_AXIS = {"w_in": 1, "conv_w": 1, "w_proj_conv": 1, "w_proj_fox": 1, "w_proj_sb": 1, "w_out": 0, "w_up": 1,
              "ffn_conv_w": 1, "w_down": 0}
SMALL_SHARDED = ("conv_w", "ffn_conv_w")
BIG = tuple(k for k in SHARD_AXIS if k not in SMALL_SHARDED)
BIG_IN = ("w_in",)
BIG_REST = tuple(k for k in BIG if k not in BIG_IN)
REPLICATED = tuple(k for k in WEIGHTS if k not in SHARD_AXIS)
SMALL = REPLICATED + SMALL_SHARDED


def _pack_small(parts, row_align):
    flat = jnp.concatenate([p.reshape(-1) for p in parts])
    rows = -(-flat.shape[0] // (PACK_COLS * row_align)) * row_align
    return jnp.pad(flat, (0, rows * PACK_COLS - flat.shape[0])).reshape(rows, PACK_COLS)


def _unpack_small(packed, shapes):
    flat = packed.reshape(-1)
    out, off = [], 0
    for shape in shapes:
        size = int(np.prod(shape))
        out.append(flat[off:off + size].reshape(shape))
        off += size
    return out


def _blocks(stacked, own):
    chip = 2 * lax.axis_index("x") + lax.axis_index("y")
    return [jnp.where(chip == k, own, stacked[k]) for k in range(4)]


def _to_chips(full, axis):
    a, b = full.shape
    if axis == 0:
        return full.reshape(4, a // 4, b)
    return jnp.moveaxis(full.reshape(a, 4, b // 4), 1, 0)


class _Plan:
    def __init__(self, given):
        self.given = given
        self.depth = given["norm1_g"].shape[0]
        conv_shapes = [given[k].shape for k in SMALL_SHARDED]
        packed = [_pack_small([given[k] for k in SMALL_SHARDED], 16)]
        (got,) = _own_block(_run_comm(_gather_comm(packed), "gather_conv_weights"), packed)
        by_chip = [_unpack_small(got[j], conv_shapes) for j in range(4)]
        self.conv = {k: jnp.concatenate([by_chip[j][i] for j in range(4)], axis=-1) for i, k in enumerate(SMALL_SHARDED)}
        self.shards = {}

    def _shards(self, l, names):
        if (l, names) not in self.shards:
            self.shards[l, names] = [self.given[k][l].astype(BF16) for k in names]
        return self.shards[l, names]

    def gather(self, l, names):
        return _gather_comm(self._shards(l, names))

    def weights(self, l, names, got):
        out = {}
        for k, stacked, own in zip(names, got, self._shards(l, names)):
            if k == "w_in":
                out["wa"], out["wf"] = _w_in_from_chips(_blocks(stacked, own))
            else:
                out[k] = jnp.concatenate(_blocks(stacked, own), axis=SHARD_AXIS[k])
        return out

    def layer(self, l, got_in):
        if got_in is None:
            got_in = _run_comm(self.gather(l, BIG_IN), "gather_w_in")
        p = {k: self.given[k][l] for k in REPLICATED}
        p.update({k: self.conv[k][l] for k in SMALL_SHARDED})
        p.update(self.weights(l, BIG_IN, got_in))
        return p

    def reduction(self, w_in_grad, rest):
        gs = [] if w_in_grad is None else [w_in_grad]
        return _Reduction(gs + [_to_chips(rest[k], SHARD_AXIS[k]) for k in BIG_REST if k in rest])


def kernel(x, norm1_g, w_in, fox_f_bias, gate_bias, conv_w, fox_q_norm_g, fox_k_norm_g, w_proj_conv, w_proj_fox, w_proj_sb, w_out, norm2_g, w_up, ffn_conv_w, ffn_conv_b, w_down, loss_target, m_norm1_g, m_w_in, m_fox_f_bias, m_gate_bias, m_conv_w, m_fox_q_norm_g, m_fox_k_norm_g, m_w_proj_conv, m_w_proj_fox, m_w_proj_sb, m_w_out, m_norm2_g, m_w_up, m_ffn_conv_w, m_ffn_conv_b, m_w_down, v_norm1_g, v_w_in, v_fox_f_bias, v_gate_bias, v_conv_w, v_fox_q_norm_g, v_fox_k_norm_g, v_w_proj_conv, v_w_proj_fox, v_w_proj_sb, v_w_out, v_norm2_g, v_w_up, v_ffn_conv_w, v_ffn_conv_b, v_w_down):
    given = dict(x=x, norm1_g=norm1_g, w_in=w_in, fox_f_bias=fox_f_bias, gate_bias=gate_bias, conv_w=conv_w, fox_q_norm_g=fox_q_norm_g, fox_k_norm_g=fox_k_norm_g, w_proj_conv=w_proj_conv, w_proj_fox=w_proj_fox, w_proj_sb=w_proj_sb, w_out=w_out, norm2_g=norm2_g, w_up=w_up, ffn_conv_w=ffn_conv_w, ffn_conv_b=ffn_conv_b, w_down=w_down, loss_target=loss_target, m_norm1_g=m_norm1_g, m_w_in=m_w_in, m_fox_f_bias=m_fox_f_bias, m_gate_bias=m_gate_bias, m_conv_w=m_conv_w, m_fox_q_norm_g=m_fox_q_norm_g, m_fox_k_norm_g=m_fox_k_norm_g, m_w_proj_conv=m_w_proj_conv, m_w_proj_fox=m_w_proj_fox, m_w_proj_sb=m_w_proj_sb, m_w_out=m_w_out, m_norm2_g=m_norm2_g, m_w_up=m_w_up, m_ffn_conv_w=m_ffn_conv_w, m_ffn_conv_b=m_ffn_conv_b, m_w_down=m_w_down, v_norm1_g=v_norm1_g, v_w_in=v_w_in, v_fox_f_bias=v_fox_f_bias, v_gate_bias=v_gate_bias, v_conv_w=v_conv_w, v_fox_q_norm_g=v_fox_q_norm_g, v_fox_k_norm_g=v_fox_k_norm_g, v_w_proj_conv=v_w_proj_conv, v_w_proj_fox=v_w_proj_fox, v_w_proj_sb=v_w_proj_sb, v_w_out=v_w_out, v_norm2_g=v_norm2_g, v_w_up=v_w_up, v_ffn_conv_w=v_ffn_conv_w, v_ffn_conv_b=v_ffn_conv_b, v_w_down=v_w_down)
    depth = given["norm1_g"].shape[0]
    chip = 2 * lax.axis_index("x") + lax.axis_index("y")

    sq, dx, grads = _local_step(given["x"][0], given["loss_target"][0], _Plan(given))
    loss = lax.psum(0.5 * sq[0, 0] / D_MODEL, ("x", "y", "c"))

    gsum = {k: jnp.stack([g[k] for g in grads]) for k in BIG}
    small_shapes = [(depth, *grads[0][k].shape) for k in SMALL]
    summed = _all_reduce_small(_pack_small([jnp.stack([g[k] for g in grads]) for k in SMALL], 8))
    for k, total in zip(SMALL, _unpack_small(summed, small_shapes)):
        if k in SMALL_SHARDED:
            total = lax.dynamic_index_in_dim(total.reshape(*total.shape[:-1], 4, total.shape[-1] // 4), chip, axis=2, keepdims=False)
        gsum[k] = total

    deltas, new_m, new_v = {}, {}, {}
    for k in WEIGHTS:
        deltas[k], new_m[k], new_v[k] = _adamw(given[k], gsum[k], given["m_" + k], given["v_" + k], "adamw_" + k)
    return (loss, dx[None], *[gsum[k] for k in WEIGHTS], *[deltas[k] for k in WEIGHTS],
            *[new_m[k] for k in WEIGHTS], *[new_v[k] for k in WEIGHTS])
```
